```python
import jax, jax.numpy as jnp
from jax import lax
import numpy as np

D_MODEL = 2048
BATCH = 8
SEQ = 2048
DEPTH = 2

D_MIX = D_MODEL
HEAD_DIM = 64
ATTN_WIDTH = D_MIX // 2
N_Q_HEADS = ATTN_WIDTH // HEAD_DIM
N_KV_HEADS = N_Q_HEADS // 4
KV_WIDTH = N_KV_HEADS * HEAD_DIM
WINDOW = 128
CONV_WIDTH = D_MIX // 4
CONV_KERNEL = 31
SGU_WIDTH = D_MIX // 4
SGU_HEADS = SGU_WIDTH // HEAD_DIM
CHUNK = 128
D_FF = 4 * D_MODEL
EPS = 1e-6
NEG_INF = -1e30
SPLITS = (ATTN_WIDTH,
          ATTN_WIDTH + KV_WIDTH,
          ATTN_WIDTH + 2 * KV_WIDTH,
          ATTN_WIDTH + 2 * KV_WIDTH + 2 * CONV_WIDTH)
D_IN = ATTN_WIDTH + 2 * KV_WIDTH + 2 * CONV_WIDTH + 2 * SGU_WIDTH

kernel_name = "hymba_conv_sgu_swa_hybrid"


def rms_norm(x, g):
    xf = x.astype(jnp.float32)
    y = xf * lax.rsqrt(jnp.mean(xf * xf, axis=-1, keepdims=True) + EPS)
    return (y * g.astype(jnp.float32)).astype(x.dtype)


def layer_norm(x, g, b):
    xf = x.astype(jnp.float32)
    mu = jnp.mean(xf, axis=-1, keepdims=True)
    xc = xf - mu
    y = xc * lax.rsqrt(jnp.mean(xc * xc, axis=-1, keepdims=True) + EPS)
    return (y * g.astype(jnp.float32) + b.astype(jnp.float32)).astype(x.dtype)


def sliding_window_attention(q, k, v, sinks):
    B, S = q.shape[0], q.shape[1]
    nb = S // WINDOW
    G = N_Q_HEADS // N_KV_HEADS
    qb = q.reshape(B, nb, WINDOW, N_KV_HEADS, G, HEAD_DIM)

    def with_prev(t):
        tb = t.reshape(B, nb, WINDOW, N_KV_HEADS, HEAD_DIM)
        prev = jnp.pad(tb, ((0, 0), (1, 0), (0, 0), (0, 0), (0, 0)))[:, :-1]
        return jnp.concatenate([prev, tb], axis=2)

    kb, vb = with_prev(k), with_prev(v)
    scale = HEAD_DIM ** -0.5
    logits = jnp.einsum('bnqkgd,bnskd->bnkgqs', qb, kb).astype(jnp.float32) * scale
    qi = jnp.arange(WINDOW)[None, :, None]
    sj = jnp.arange(2 * WINDOW)[None, None, :]
    blk = jnp.arange(nb)[:, None, None]
    rel = qi + WINDOW - sj
    key_pos = blk * WINDOW - WINDOW + sj
    mask = (rel >= 0) & (rel < WINDOW) & (key_pos >= 0)
    logits = jnp.where(mask[None, :, None, None], logits, NEG_INF)
    sink = sinks.astype(jnp.float32).reshape(N_KV_HEADS, G)[None, None, :, :, None, None]
    m = jnp.maximum(jnp.max(logits, axis=-1, keepdims=True), sink)
    p = jnp.exp(logits - m)
    denom = jnp.sum(p, axis=-1, keepdims=True) + jnp.exp(sink - m)
    probs = (p / denom).astype(v.dtype)
    out = jnp.einsum('bnkgqs,bnskd->bnqkgd', probs, vb)
    return out.reshape(B, S, ATTN_WIDTH)


def conv_module(xc, conv_w, conv_b, ln_g, ln_b):
    a, gate = jnp.split(xc, 2, axis=-1)
    h = a * jax.nn.sigmoid(gate)
    h = lax.conv_general_dilated(
        h, conv_w[:, None, :].astype(h.dtype), window_strides=(1,),
        padding=[(CONV_KERNEL - 1, 0)],
        dimension_numbers=('NWC', 'WIO', 'NWC'),
        feature_group_count=CONV_WIDTH) + conv_b
    h = layer_norm(h, ln_g, ln_b)
    return jax.nn.silu(h)


def spatial_gating(xs, ln_g, ln_b, w_s, b_s):
    B, S = xs.shape[0], xs.shape[1]
    u, v = jnp.split(xs, 2, axis=-1)
    v = layer_norm(v, ln_g, ln_b)
    vb = v.reshape(B, S // CHUNK, CHUNK, SGU_HEADS, HEAD_DIM)
    causal = jnp.tril(jnp.ones((CHUNK, CHUNK), dtype=bool))
    w = jnp.where(causal[None], w_s, jnp.zeros_like(w_s))
    s = jnp.einsum('hij,bnjhd->bnihd', w, vb) + b_s.T[None, None, :, :, None]
    return u * s.reshape(B, S, SGU_WIDTH)


def _fwd_setup_inputs(seed: int = 0) -> dict:
    key = jax.random.key(seed)
    ks = jax.random.split(key, 20)
    f32 = jnp.float32

    def nrm(k, shape, scale):
        return jax.random.normal(k, shape, f32) * scale

    def gain(k, shape):
        return 1.0 + 0.02 * jax.random.normal(k, shape, f32)

    return {
        "x": jax.random.normal(ks[0], (BATCH, SEQ, D_MODEL), f32),
        "ln1_g": gain(ks[1], (DEPTH, D_MODEL)),
        "w_in": nrm(ks[2], (DEPTH, D_MODEL, D_IN), D_MODEL ** -0.5),
        "q_norm_g": gain(ks[3], (DEPTH, HEAD_DIM)),
        "k_norm_g": gain(ks[4], (DEPTH, HEAD_DIM)),
        "sinks": nrm(ks[5], (DEPTH, N_Q_HEADS), 0.5),
        "conv_w": nrm(ks[6], (DEPTH, CONV_KERNEL, CONV_WIDTH), CONV_KERNEL ** -0.5),
        "conv_b": nrm(ks[7], (DEPTH, CONV_WIDTH), 0.02),
        "conv_ln_g": gain(ks[8], (DEPTH, CONV_WIDTH)),
        "conv_ln_b": nrm(ks[9], (DEPTH, CONV_WIDTH), 0.02),
        "sgu_ln_g": gain(ks[10], (DEPTH, SGU_WIDTH)),
        "sgu_ln_b": nrm(ks[11], (DEPTH, SGU_WIDTH), 0.02),
        "sgu_w": nrm(ks[12], (DEPTH, SGU_HEADS, CHUNK, CHUNK), CHUNK ** -0.5),
        "sgu_b": gain(ks[13], (DEPTH, SGU_HEADS, CHUNK)),
        "out_norm_g": gain(ks[14], (DEPTH, D_MIX)),
        "w_out": nrm(ks[15], (DEPTH, D_MIX, D_MODEL), D_MIX ** -0.5),
        "ln2_g": gain(ks[16], (DEPTH, D_MODEL)),
        "w_up": nrm(ks[17], (DEPTH, D_MODEL, D_FF), D_MODEL ** -0.5),
        "w_down": nrm(ks[18], (DEPTH, D_FF, D_MODEL), D_FF ** -0.5),
    }


def _fwd_reference(x, ln1_g, w_in, q_norm_g, k_norm_g, sinks, conv_w, conv_b, conv_ln_g,
              conv_ln_b, sgu_ln_g, sgu_ln_b, sgu_w, sgu_b, out_norm_g, w_out, ln2_g,
              w_up, w_down):
    B, S = x.shape[0], x.shape[1]
    for l in range(DEPTH):
        h = rms_norm(x, ln1_g[l])
        proj = h @ w_in[l]
        q, k, v, xc, xs = jnp.split(proj, SPLITS, axis=-1)
        q = rms_norm(q.reshape(B, S, N_Q_HEADS, HEAD_DIM), q_norm_g[l])
        k = rms_norm(k.reshape(B, S, N_KV_HEADS, HEAD_DIM), k_norm_g[l])
        v = v.reshape(B, S, N_KV_HEADS, HEAD_DIM)
        y_attn = sliding_window_attention(q, k, v, sinks[l])
        y_conv = conv_module(xc, conv_w[l], conv_b[l], conv_ln_g[l], conv_ln_b[l])
        y_sgu = spatial_gating(xs, sgu_ln_g[l], sgu_ln_b[l], sgu_w[l], sgu_b[l])
        g = out_norm_g[l]
        mix = jnp.concatenate([
            rms_norm(y_attn, g[:ATTN_WIDTH]),
            rms_norm(y_conv, g[ATTN_WIDTH:ATTN_WIDTH + CONV_WIDTH]),
            rms_norm(y_sgu, g[ATTN_WIDTH + CONV_WIDTH:]),
        ], axis=-1)
        x = x + mix @ w_out[l]
        h = rms_norm(x, ln2_g[l])
        x = x + jnp.square(jax.nn.relu(h @ w_up[l])) @ w_down[l]
    return x


import jax as _jax
import jax.numpy as _jnp

TWIN_FORMAT = 'train_step'
FWD_PARAMS = ['x', 'ln1_g', 'w_in', 'q_norm_g', 'k_norm_g', 'sinks', 'conv_w', 'conv_b', 'conv_ln_g', 'conv_ln_b', 'sgu_ln_g', 'sgu_ln_b', 'sgu_w', 'sgu_b', 'out_norm_g', 'w_out', 'ln2_g', 'w_up', 'w_down']
TWIN_WEIGHTS = ['ln1_g', 'w_in', 'q_norm_g', 'k_norm_g', 'sinks', 'conv_w', 'conv_b', 'conv_ln_g', 'conv_ln_b', 'sgu_ln_g', 'sgu_ln_b', 'sgu_w', 'sgu_b', 'out_norm_g', 'w_out', 'ln2_g', 'w_up', 'w_down']
TWIN_DIFF_INPUT = 'x'
TWIN_INPUTS = ['x', 'ln1_g', 'w_in', 'q_norm_g', 'k_norm_g', 'sinks', 'conv_w', 'conv_b', 'conv_ln_g', 'conv_ln_b', 'sgu_ln_g', 'sgu_ln_b', 'sgu_w', 'sgu_b', 'out_norm_g', 'w_out', 'ln2_g', 'w_up', 'w_down', 'loss_target', 'm_ln1_g', 'm_w_in', 'm_q_norm_g', 'm_k_norm_g', 'm_sinks', 'm_conv_w', 'm_conv_b', 'm_conv_ln_g', 'm_conv_ln_b', 'm_sgu_ln_g', 'm_sgu_ln_b', 'm_sgu_w', 'm_sgu_b', 'm_out_norm_g', 'm_w_out', 'm_ln2_g', 'm_w_up', 'm_w_down', 'v_ln1_g', 'v_w_in', 'v_q_norm_g', 'v_k_norm_g', 'v_sinks', 'v_conv_w', 'v_conv_b', 'v_conv_ln_g', 'v_conv_ln_b', 'v_sgu_ln_g', 'v_sgu_ln_b', 'v_sgu_w', 'v_sgu_b', 'v_out_norm_g', 'v_w_out', 'v_ln2_g', 'v_w_up', 'v_w_down']
TWIN_OUTPUTS = ['loss', 'grad_x', 'grad_ln1_g', 'grad_w_in', 'grad_q_norm_g', 'grad_k_norm_g', 'grad_sinks', 'grad_conv_w', 'grad_conv_b', 'grad_conv_ln_g', 'grad_conv_ln_b', 'grad_sgu_ln_g', 'grad_sgu_ln_b', 'grad_sgu_w', 'grad_sgu_b', 'grad_out_norm_g', 'grad_w_out', 'grad_ln2_g', 'grad_w_up', 'grad_w_down', 'delta_ln1_g', 'delta_w_in', 'delta_q_norm_g', 'delta_k_norm_g', 'delta_sinks', 'delta_conv_w', 'delta_conv_b', 'delta_conv_ln_g', 'delta_conv_ln_b', 'delta_sgu_ln_g', 'delta_sgu_ln_b', 'delta_sgu_w', 'delta_sgu_b', 'delta_out_norm_g', 'delta_w_out', 'delta_ln2_g', 'delta_w_up', 'delta_w_down', 'new_m_ln1_g', 'new_m_w_in', 'new_m_q_norm_g', 'new_m_k_norm_g', 'new_m_sinks', 'new_m_conv_w', 'new_m_conv_b', 'new_m_conv_ln_g', 'new_m_conv_ln_b', 'new_m_sgu_ln_g', 'new_m_sgu_ln_b', 'new_m_sgu_w', 'new_m_sgu_b', 'new_m_out_norm_g', 'new_m_w_out', 'new_m_ln2_g', 'new_m_w_up', 'new_m_w_down', 'new_v_ln1_g', 'new_v_w_in', 'new_v_q_norm_g', 'new_v_k_norm_g', 'new_v_sinks', 'new_v_conv_w', 'new_v_conv_b', 'new_v_conv_ln_g', 'new_v_conv_ln_b', 'new_v_sgu_ln_g', 'new_v_sgu_ln_b', 'new_v_sgu_w', 'new_v_sgu_b', 'new_v_out_norm_g', 'new_v_w_out', 'new_v_ln2_g', 'new_v_w_up', 'new_v_w_down']
TWIN_LEAF_KINDS = {'loss': 'loss', 'grad_x': 'grad_x', 'grad_ln1_g': 'grad_w', 'grad_w_in': 'grad_w', 'grad_q_norm_g': 'grad_w', 'grad_k_norm_g': 'grad_w', 'grad_sinks': 'grad_w', 'grad_conv_w': 'grad_w', 'grad_conv_b': 'grad_w', 'grad_conv_ln_g': 'grad_w', 'grad_conv_ln_b': 'grad_w', 'grad_sgu_ln_g': 'grad_w', 'grad_sgu_ln_b': 'grad_w', 'grad_sgu_w': 'grad_w', 'grad_sgu_b': 'grad_w', 'grad_out_norm_g': 'grad_w', 'grad_w_out': 'grad_w', 'grad_ln2_g': 'grad_w', 'grad_w_up': 'grad_w', 'grad_w_down': 'grad_w', 'delta_ln1_g': 'delta_w', 'delta_w_in': 'delta_w', 'delta_q_norm_g': 'delta_w', 'delta_k_norm_g': 'delta_w', 'delta_sinks': 'delta_w', 'delta_conv_w': 'delta_w', 'delta_conv_b': 'delta_w', 'delta_conv_ln_g': 'delta_w', 'delta_conv_ln_b': 'delta_w', 'delta_sgu_ln_g': 'delta_w', 'delta_sgu_ln_b': 'delta_w', 'delta_sgu_w': 'delta_w', 'delta_sgu_b': 'delta_w', 'delta_out_norm_g': 'delta_w', 'delta_w_out': 'delta_w', 'delta_ln2_g': 'delta_w', 'delta_w_up': 'delta_w', 'delta_w_down': 'delta_w', 'new_m_ln1_g': 'new_m', 'new_m_w_in': 'new_m', 'new_m_q_norm_g': 'new_m', 'new_m_k_norm_g': 'new_m', 'new_m_sinks': 'new_m', 'new_m_conv_w': 'new_m', 'new_m_conv_b': 'new_m', 'new_m_conv_ln_g': 'new_m', 'new_m_conv_ln_b': 'new_m', 'new_m_sgu_ln_g': 'new_m', 'new_m_sgu_ln_b': 'new_m', 'new_m_sgu_w': 'new_m', 'new_m_sgu_b': 'new_m', 'new_m_out_norm_g': 'new_m', 'new_m_w_out': 'new_m', 'new_m_ln2_g': 'new_m', 'new_m_w_up': 'new_m', 'new_m_w_down': 'new_m', 'new_v_ln1_g': 'new_v', 'new_v_w_in': 'new_v', 'new_v_q_norm_g': 'new_v', 'new_v_k_norm_g': 'new_v', 'new_v_sinks': 'new_v', 'new_v_conv_w': 'new_v', 'new_v_conv_b': 'new_v', 'new_v_conv_ln_g': 'new_v', 'new_v_conv_ln_b': 'new_v', 'new_v_sgu_ln_g': 'new_v', 'new_v_sgu_ln_b': 'new_v', 'new_v_sgu_w': 'new_v', 'new_v_sgu_b': 'new_v', 'new_v_out_norm_g': 'new_v', 'new_v_w_out': 'new_v', 'new_v_ln2_g': 'new_v', 'new_v_w_up': 'new_v', 'new_v_w_down': 'new_v'}


def _forward(args):
    return _fwd_reference(*[args[k] for k in FWD_PARAMS])


def _output_shape():
    out = _jax.eval_shape(lambda: _forward(_fwd_setup_inputs(0)))
    return out.shape, out.dtype

N_MICROBATCH = 1
ADAM_LR = 0.001
ADAM_B1 = 0.9
ADAM_B2 = 0.999
ADAM_EPS = 1e-08
ADAM_WD = 0.01
ADAM_STEP = 10
PER_EXAMPLE_BATCH_AXIS = {'x': 0, 'loss_target': 0}
SHARED_INPUTS = []
_WEIGHT_DTYPES = {'ln1_g': _jnp.float32, 'w_in': _jnp.float32, 'q_norm_g': _jnp.float32, 'k_norm_g': _jnp.float32, 'sinks': _jnp.float32, 'conv_w': _jnp.float32, 'conv_b': _jnp.float32, 'conv_ln_g': _jnp.float32, 'conv_ln_b': _jnp.float32, 'sgu_ln_g': _jnp.float32, 'sgu_ln_b': _jnp.float32, 'sgu_w': _jnp.float32, 'sgu_b': _jnp.float32, 'out_norm_g': _jnp.float32, 'w_out': _jnp.float32, 'ln2_g': _jnp.float32, 'w_up': _jnp.float32, 'w_down': _jnp.float32}
MOMENT_SCALE = {'ln1_g': 3.324660e+00, 'w_in': 2.498310e+00, 'q_norm_g': 1.288188e+00, 'k_norm_g': 1.291037e+00, 'sinks': 3.319084e-01, 'conv_w': 1.908173e+00, 'conv_b': 1.957098e+01, 'conv_ln_g': 7.677336e+00, 'conv_ln_b': 1.073025e+01, 'sgu_ln_g': 2.137819e-01, 'sgu_ln_b': 1.766068e-01, 'sgu_w': 1.243245e-01, 'sgu_b': 1.828649e-01, 'out_norm_g': 9.665348e+00, 'w_out': 3.897291e+00, 'ln2_g': 2.405082e+01, 'w_up': 1.364079e+00, 'w_down': 5.681089e+00}


def _to_microbatches(a, axis):
    t = _jnp.moveaxis(a, axis, 0)
    t = t.reshape((N_MICROBATCH, t.shape[0] // N_MICROBATCH) + t.shape[1:])
    return _jnp.moveaxis(t, 1, axis + 1)


def setup_inputs(seed: int = 0) -> dict:
    inp = _fwd_setup_inputs(seed)
    key = _jax.random.fold_in(_jax.random.key(seed), 7919)
    shape, _ = _output_shape()
    out = dict(inp)
    out["loss_target"] = _jax.random.normal(_jax.random.fold_in(key, 0), shape, _jnp.float32)
    for i, name in enumerate(TWIN_WEIGHTS):
        w = inp[name].astype(_jnp.float32)
        if MOMENT_SCALE is None:
            s = _jnp.sqrt(_jnp.mean(_jnp.square(w)) + 1e-30)
        else:
            s = MOMENT_SCALE[name]
        km, kv = _jax.random.split(_jax.random.fold_in(key, i + 1))
        out[name] = w
        out["m_" + name] = s * _jax.random.normal(km, w.shape, _jnp.float32)
        out["v_" + name] = (s * s) * _jax.random.uniform(kv, w.shape, _jnp.float32, 0.5, 1.5)
    if N_MICROBATCH > 1:
        for name, axis in PER_EXAMPLE_BATCH_AXIS.items():
            out[name] = _to_microbatches(out[name], axis)
    return {'x': out['x'], 'ln1_g': out['ln1_g'], 'w_in': out['w_in'], 'q_norm_g': out['q_norm_g'], 'k_norm_g': out['k_norm_g'], 'sinks': out['sinks'], 'conv_w': out['conv_w'], 'conv_b': out['conv_b'], 'conv_ln_g': out['conv_ln_g'], 'conv_ln_b': out['conv_ln_b'], 'sgu_ln_g': out['sgu_ln_g'], 'sgu_ln_b': out['sgu_ln_b'], 'sgu_w': out['sgu_w'], 'sgu_b': out['sgu_b'], 'out_norm_g': out['out_norm_g'], 'w_out': out['w_out'], 'ln2_g': out['ln2_g'], 'w_up': out['w_up'], 'w_down': out['w_down'], 'loss_target': out['loss_target'], 'm_ln1_g': out['m_ln1_g'], 'm_w_in': out['m_w_in'], 'm_q_norm_g': out['m_q_norm_g'], 'm_k_norm_g': out['m_k_norm_g'], 'm_sinks': out['m_sinks'], 'm_conv_w': out['m_conv_w'], 'm_conv_b': out['m_conv_b'], 'm_conv_ln_g': out['m_conv_ln_g'], 'm_conv_ln_b': out['m_conv_ln_b'], 'm_sgu_ln_g': out['m_sgu_ln_g'], 'm_sgu_ln_b': out['m_sgu_ln_b'], 'm_sgu_w': out['m_sgu_w'], 'm_sgu_b': out['m_sgu_b'], 'm_out_norm_g': out['m_out_norm_g'], 'm_w_out': out['m_w_out'], 'm_ln2_g': out['m_ln2_g'], 'm_w_up': out['m_w_up'], 'm_w_down': out['m_w_down'], 'v_ln1_g': out['v_ln1_g'], 'v_w_in': out['v_w_in'], 'v_q_norm_g': out['v_q_norm_g'], 'v_k_norm_g': out['v_k_norm_g'], 'v_sinks': out['v_sinks'], 'v_conv_w': out['v_conv_w'], 'v_conv_b': out['v_conv_b'], 'v_conv_ln_g': out['v_conv_ln_g'], 'v_conv_ln_b': out['v_conv_ln_b'], 'v_sgu_ln_g': out['v_sgu_ln_g'], 'v_sgu_ln_b': out['v_sgu_ln_b'], 'v_sgu_w': out['v_sgu_w'], 'v_sgu_b': out['v_sgu_b'], 'v_out_norm_g': out['v_out_norm_g'], 'v_w_out': out['v_w_out'], 'v_ln2_g': out['v_ln2_g'], 'v_w_up': out['v_w_up'], 'v_w_down': out['v_w_down']}


def _loss(weights, diff, rest, loss_target):
    with _jax.named_scope("forward"):
        args = {**rest, TWIN_DIFF_INPUT: diff, **{k: w.astype(_WEIGHT_DTYPES[k]) for k, w in weights.items()}}
        y = _forward(args)
    with _jax.named_scope("loss_head"):
        err = _jnp.square(y.astype(_jnp.float32) - loss_target)
        return 0.5 * _jnp.sum(_jnp.mean(err, axis=-1)) if err.ndim else 0.5 * err


def _adamw(w, g, m, v):
    m = ADAM_B1 * m + (1.0 - ADAM_B1) * g
    v = ADAM_B2 * v + (1.0 - ADAM_B2) * _jnp.square(g)
    m_hat = m / (1.0 - ADAM_B1 ** ADAM_STEP)
    v_hat = v / (1.0 - ADAM_B2 ** ADAM_STEP)
    delta = -ADAM_LR * (m_hat / (_jnp.sqrt(v_hat) + ADAM_EPS) + ADAM_WD * w)
    return delta, m, v


def reference(x, ln1_g, w_in, q_norm_g, k_norm_g, sinks, conv_w, conv_b, conv_ln_g, conv_ln_b, sgu_ln_g, sgu_ln_b, sgu_w, sgu_b, out_norm_g, w_out, ln2_g, w_up, w_down, loss_target, m_ln1_g, m_w_in, m_q_norm_g, m_k_norm_g, m_sinks, m_conv_w, m_conv_b, m_conv_ln_g, m_conv_ln_b, m_sgu_ln_g, m_sgu_ln_b, m_sgu_w, m_sgu_b, m_out_norm_g, m_w_out, m_ln2_g, m_w_up, m_w_down, v_ln1_g, v_w_in, v_q_norm_g, v_k_norm_g, v_sinks, v_conv_w, v_conv_b, v_conv_ln_g, v_conv_ln_b, v_sgu_ln_g, v_sgu_ln_b, v_sgu_w, v_sgu_b, v_out_norm_g, v_w_out, v_ln2_g, v_w_up, v_w_down):
    given = dict(x=x, ln1_g=ln1_g, w_in=w_in, q_norm_g=q_norm_g, k_norm_g=k_norm_g, sinks=sinks, conv_w=conv_w, conv_b=conv_b, conv_ln_g=conv_ln_g, conv_ln_b=conv_ln_b, sgu_ln_g=sgu_ln_g, sgu_ln_b=sgu_ln_b, sgu_w=sgu_w, sgu_b=sgu_b, out_norm_g=out_norm_g, w_out=w_out, ln2_g=ln2_g, w_up=w_up, w_down=w_down, loss_target=loss_target, m_ln1_g=m_ln1_g, m_w_in=m_w_in, m_q_norm_g=m_q_norm_g, m_k_norm_g=m_k_norm_g, m_sinks=m_sinks, m_conv_w=m_conv_w, m_conv_b=m_conv_b, m_conv_ln_g=m_conv_ln_g, m_conv_ln_b=m_conv_ln_b, m_sgu_ln_g=m_sgu_ln_g, m_sgu_ln_b=m_sgu_ln_b, m_sgu_w=m_sgu_w, m_sgu_b=m_sgu_b, m_out_norm_g=m_out_norm_g, m_w_out=m_w_out, m_ln2_g=m_ln2_g, m_w_up=m_w_up, m_w_down=m_w_down, v_ln1_g=v_ln1_g, v_w_in=v_w_in, v_q_norm_g=v_q_norm_g, v_k_norm_g=v_k_norm_g, v_sinks=v_sinks, v_conv_w=v_conv_w, v_conv_b=v_conv_b, v_conv_ln_g=v_conv_ln_g, v_conv_ln_b=v_conv_ln_b, v_sgu_ln_g=v_sgu_ln_g, v_sgu_ln_b=v_sgu_ln_b, v_sgu_w=v_sgu_w, v_sgu_b=v_sgu_b, v_out_norm_g=v_out_norm_g, v_w_out=v_w_out, v_ln2_g=v_ln2_g, v_w_up=v_w_up, v_w_down=v_w_down)
    weights = {n: given[n] for n in TWIN_WEIGHTS}
    shared = {n: given[n] for n in SHARED_INPUTS}
    per_example = {n: given[n] for n in ['x']}
    grad_fn = _jax.value_and_grad(_loss, argnums=(0, 1))

    def one_microbatch(ex, loss_target):
        ex = dict(ex)
        diff = ex.pop(TWIN_DIFF_INPUT)
        return grad_fn(weights, diff, {**shared, **ex}, loss_target)

    if N_MICROBATCH == 1:
        loss, (grad_w, grad_x) = one_microbatch(per_example, given["loss_target"])
    else:
        def body(carry, xs):
            loss_sum, grad_sum = carry
            l_k, (gw_k, gx_k) = one_microbatch(xs[0], xs[1])
            with _jax.named_scope("update"):
                return (loss_sum + l_k, _jax.tree.map(_jnp.add, grad_sum, gw_k)), gx_k

        init = (_jnp.zeros((), _jnp.float32), _jax.tree.map(_jnp.zeros_like, weights))
        (loss, grad_w), grad_x = _jax.lax.scan(body, init, (per_example, given["loss_target"]))
    with _jax.named_scope("update"):
        delta_w, new_m, new_v = {}, {}, {}
        for n in TWIN_WEIGHTS:
            delta_w[n], new_m[n], new_v[n] = _adamw(weights[n], grad_w[n], given["m_" + n], given["v_" + n])
    return (loss, grad_x, *[grad_w[n] for n in TWIN_WEIGHTS], *[delta_w[n] for n in TWIN_WEIGHTS],
            *[new_m[n] for n in TWIN_WEIGHTS], *[new_v[n] for n in TWIN_WEIGHTS])
```

```python
import functools

import jax
import jax.numpy as jnp
from jax import lax
from jax.experimental import pallas as pl
from jax.experimental.pallas import tpu as pltpu

F32 = jnp.float32
BF16 = jnp.bfloat16
SDS = jax.ShapeDtypeStruct

EPS = 1e-6
NEG_INF = -1e30
HEAD = 64
BLK = 128
CONV_K = 31
HALO = 32
N_DEV = 8

ADAM_LR = 0.001
ADAM_B1 = 0.9
ADAM_B2 = 0.999
ADAM_EPS = 1e-08
ADAM_WD = 0.01
ADAM_STEP = 10

VMEM_LIMIT = 56 * 1024 * 1024

MESH = pl.DeviceIdType.MESH


def _params(*sem):
    return pltpu.CompilerParams(dimension_semantics=sem, vmem_limit_bytes=VMEM_LIMIT)


def _nt(a, b):
    return lax.dot_general(a, b, (((1,), (1,)), ((), ())), preferred_element_type=F32)


def _tn(a, b):
    return lax.dot_general(a, b, (((0,), (0,)), ((), ())), preferred_element_type=F32)


def _nn(a, b):
    return jnp.dot(a, b, preferred_element_type=F32)


def _sigmoid(x):
    return 1.0 / (1.0 + jnp.exp(-x))


def _seg_ones():
    r = lax.broadcasted_iota(jnp.int32, (BLK, BLK), 0) // HEAD
    c = lax.broadcasted_iota(jnp.int32, (BLK, BLK), 1) // HEAD
    return (r == c).astype(BF16)


def _segsum(x, ones):
    hi = x.astype(BF16)
    r1 = x - hi.astype(F32)
    mid = r1.astype(BF16)
    lo = (r1 - mid.astype(F32)).astype(BF16)
    return _nn(hi, ones) + _nn(mid, ones) + _nn(lo, ones)


def _head_rms(x, gain, ones):
    rstd = lax.rsqrt(_segsum(x * x, ones) * (1.0 / HEAD) + EPS)
    xhat = x * rstd
    return xhat * gain, xhat, rstd


def _expand(x, odd, lo):
    if odd:
        xl = pltpu.roll(jnp.where(lo, 0.0, x), HEAD, axis=1)
    else:
        xl = jnp.where(lo, x, 0.0)
    xh = pltpu.roll(xl, HEAD, axis=1)
    return jnp.concatenate([xl, xh], axis=0).astype(BF16)


def _fold(g2, odd, lo):
    r = g2.shape[0] // 2
    s = jnp.where(lo, g2[:r], 0.0) + pltpu.roll(jnp.where(lo, 0.0, g2[r:]), HEAD, axis=1)
    if odd:
        s = pltpu.roll(s, HEAD, axis=1)
    return s


def _attn_mask(n):
    qi = lax.broadcasted_iota(jnp.int32, (BLK, 2 * BLK), 0)
    sj = lax.broadcasted_iota(jnp.int32, (BLK, 2 * BLK), 1)
    rel = qi + BLK - sj
    return (rel >= 0) & (rel < BLK) & ((sj >= BLK) | (n > 0))


def _attn_specs(t, aw, kv):
    prev = lambda n: jnp.maximum(n - 1, 0)
    kb, vb = aw // kv, aw // kv + 1
    return [
        pl.BlockSpec(memory_space=pltpu.SMEM),
        pl.BlockSpec((BLK, aw), lambda n: (n, 0)),
        pl.BlockSpec((BLK, kv), lambda n: (prev(n), kb)),
        pl.BlockSpec((BLK, kv), lambda n: (n, kb)),
        pl.BlockSpec((BLK, kv), lambda n: (prev(n), vb)),
        pl.BlockSpec((BLK, kv), lambda n: (n, vb)),
        pl.BlockSpec((1, BLK), lambda n: (0, 0)),
        pl.BlockSpec((1, BLK), lambda n: (0, 0)),
    ]


def _softmax_pair(s2, valid, sink0, sink1):
    out, psink = [], []
    for half, sink in ((0, sink0), (1, sink1)):
        s = jnp.where(valid, s2[:, 2 * BLK * half:2 * BLK * (half + 1)], NEG_INF)
        m = jnp.maximum(jnp.max(s, axis=-1, keepdims=True), sink)
        p = jnp.exp(s - m)
        es = jnp.exp(sink - m)
        den = jnp.sum(p, axis=-1, keepdims=True) + es
        out.append(p / den)
        psink.append(es / den)
    return jnp.concatenate(out, axis=1), psink


def _attn_fwd(proj, qg, kg, sinks, aw, kv):
    t = proj.shape[0]
    n_pairs, n_kvblk = aw // BLK, kv // BLK

    def body(sink_ref, q_ref, kp_ref, kc_ref, vp_ref, vc_ref, qg_ref, kg_ref, o_ref):
        n = pl.program_id(0)
        ones = _seg_ones()
        lo = lax.broadcasted_iota(jnp.int32, (1, BLK), 1) < HEAD
        valid = _attn_mask(n)
        kraw = jnp.concatenate([kp_ref[...], kc_ref[...]], axis=0)
        vraw = jnp.concatenate([vp_ref[...], vc_ref[...]], axis=0)
        for b in range(n_kvblk):
            kn = _head_rms(kraw[:, BLK * b:BLK * (b + 1)], kg_ref[...], ones)[0]
            vb = vraw[:, BLK * b:BLK * (b + 1)]
            for odd in (0, 1):
                j = 2 * b + odd
                k2 = _expand(kn, odd, lo)
                v2 = _expand(vb, odd, lo)
                for p in (2 * j, 2 * j + 1):
                    qn = _head_rms(q_ref[:, BLK * p:BLK * (p + 1)], qg_ref[...], ones)[0]
                    s2 = _nt(qn.astype(BF16), k2) * (HEAD ** -0.5)
                    p2, _ = _softmax_pair(s2, valid, sink_ref[2 * p], sink_ref[2 * p + 1])
                    o_ref[:, BLK * p:BLK * (p + 1)] = _nn(p2.astype(BF16), v2)

    del n_pairs
    return pl.pallas_call(
        body, name="attn_fwd", grid=(t // BLK,),
        in_specs=_attn_specs(t, aw, kv),
        out_specs=pl.BlockSpec((BLK, aw), lambda n: (n, 0)),
        out_shape=SDS((t, aw), F32),
        compiler_params=_params("arbitrary"),
    )(sinks, proj, proj, proj, proj, proj, qg, kg)


def _attn_bwd(proj, dy, qg, kg, sinks, aw, kv):
    t = proj.shape[0]
    nb = t // BLK
    n_kvblk = kv // BLK
    kb = aw // kv

    def body(sink_ref, q_ref, kp_ref, kc_ref, vp_ref, vc_ref, qg_ref, kg_ref, dy_ref, kall_ref,
             dqkv_ref, dstat_ref, dk_acc, dv_acc, dqg_acc):
        n = pl.program_id(0)
        ones = _seg_ones()
        lane = lax.broadcasted_iota(jnp.int32, (1, BLK), 1)
        lo = lane < HEAD
        valid = _attn_mask(n)

        @pl.when(n == 0)
        def _():
            dk_acc[...] = jnp.zeros_like(dk_acc)
            dv_acc[...] = jnp.zeros_like(dv_acc)
            dqg_acc[...] = jnp.zeros_like(dqg_acc)
            dstat_ref[...] = jnp.zeros_like(dstat_ref)

        kraw = jnp.concatenate([kp_ref[...], kc_ref[...]], axis=0)
        vraw = jnp.concatenate([vp_ref[...], vc_ref[...]], axis=0)
        row = pl.multiple_of(n * BLK, BLK)
        prow = pl.multiple_of(jnp.maximum(n - 1, 0) * BLK, BLK)
        dsink = jnp.zeros((1, BLK), F32)
        for b in range(n_kvblk):
            kn = _head_rms(kraw[:, BLK * b:BLK * (b + 1)], kg_ref[...], ones)[0]
            vb = vraw[:, BLK * b:BLK * (b + 1)]
            dkn = jnp.zeros((2 * BLK, BLK), F32)
            dvb = jnp.zeros((2 * BLK, BLK), F32)
            for odd in (0, 1):
                j = 2 * b + odd
                k2 = _expand(kn, odd, lo)
                v2 = _expand(vb, odd, lo)
                dk2 = jnp.zeros((4 * BLK, BLK), F32)
                dv2 = jnp.zeros((4 * BLK, BLK), F32)
                for p in (2 * j, 2 * j + 1):
                    cols = slice(BLK * p, BLK * (p + 1))
                    qn, qhat, rstd = _head_rms(q_ref[:, cols], qg_ref[...], ones)
                    qb = qn.astype(BF16)
                    s2 = _nt(qb, k2) * (HEAD ** -0.5)
                    p2, psink = _softmax_pair(s2, valid, sink_ref[2 * p], sink_ref[2 * p + 1])
                    dob = dy_ref[:, cols].astype(BF16)
                    dp2 = _nt(dob, v2)
                    ds = []
                    for half in (0, 1):
                        hs = slice(2 * BLK * half, 2 * BLK * (half + 1))
                        ph = p2[:, hs]
                        delta = jnp.sum(ph * dp2[:, hs], axis=-1, keepdims=True)
                        ds.append(ph * (dp2[:, hs] - delta))
                        dsk = -jnp.sum(psink[half] * delta, axis=0, keepdims=True)
                        dsink = dsink + jnp.where(lane == 2 * p + half, dsk, 0.0)
                    ds2 = (jnp.concatenate(ds, axis=1) * (HEAD ** -0.5)).astype(BF16)
                    dqn = _nn(ds2, k2)
                    dk2 = dk2 + _tn(ds2, qb)
                    dv2 = dv2 + _tn(p2.astype(BF16), dob)
                    dqhat = dqn * qg_ref[...]
                    proj_q = _segsum(dqhat * qhat, ones) * (1.0 / HEAD)
                    dqkv_ref[pl.ds(row, BLK), cols] = (rstd * (dqhat - qhat * proj_q)).astype(BF16)
                    dqg_acc[:, cols] += jnp.sum(dqn * qhat, axis=0, keepdims=True)
                dkn = dkn + _fold(dk2, odd, lo)
                dvb = dvb + _fold(dv2, odd, lo)
            kcols = slice(BLK * b, BLK * (b + 1))
            dk_acc[pl.ds(prow, BLK), kcols] += dkn[:BLK]
            dv_acc[pl.ds(prow, BLK), kcols] += dvb[:BLK]
            dk_acc[pl.ds(row, BLK), kcols] += dkn[BLK:]
            dv_acc[pl.ds(row, BLK), kcols] += dvb[BLK:]
        dstat_ref[2:3, :] += dsink

        @pl.when(n == nb - 1)
        def _():
            dqg = dqg_acc[:, 0:BLK]
            for p in range(1, aw // BLK):
                dqg = dqg + dqg_acc[:, BLK * p:BLK * (p + 1)]
            dstat_ref[0:1, :] = dqg + pltpu.roll(dqg, HEAD, axis=1)

            def kblock(i, dkg):
                r = pl.multiple_of(i * BLK, BLK)
                for b in range(n_kvblk):
                    kcols = slice(BLK * b, BLK * (b + 1))
                    _, khat, rstd = _head_rms(kall_ref[pl.ds(r, BLK), kcols], kg_ref[...], ones)
                    dkn = dk_acc[pl.ds(r, BLK), kcols]
                    dkhat = dkn * kg_ref[...]
                    proj_k = _segsum(dkhat * khat, ones) * (1.0 / HEAD)
                    dqkv_ref[pl.ds(r, BLK), aw + BLK * b:aw + BLK * (b + 1)] = (rstd * (dkhat - khat * proj_k)).astype(BF16)
                    dqkv_ref[pl.ds(r, BLK), aw + kv + BLK * b:aw + kv + BLK * (b + 1)] = dv_acc[pl.ds(r, BLK), kcols].astype(BF16)
                    dkg = dkg + jnp.sum(dkn * khat, axis=0, keepdims=True)
                return dkg

            dkg = lax.fori_loop(0, nb, kblock, jnp.zeros((1, BLK), F32))
            dstat_ref[1:2, :] = dkg + pltpu.roll(dkg, HEAD, axis=1)

    return pl.pallas_call(
        body, name="attn_bwd", grid=(nb,),
        in_specs=_attn_specs(t, aw, kv) + [
            pl.BlockSpec((BLK, aw), lambda n: (n, 0)),
            pl.BlockSpec((t, kv), lambda n: (0, kb)),
        ],
        out_specs=[pl.BlockSpec((t, aw + 2 * kv), lambda n: (0, 0)), pl.BlockSpec((8, BLK), lambda n: (0, 0))],
        out_shape=[SDS((t, aw + 2 * kv), BF16), SDS((8, BLK), F32)],
        scratch_shapes=[pltpu.VMEM((t, kv), F32), pltpu.VMEM((t, kv), F32), pltpu.VMEM((1, aw), F32)],
        compiler_params=_params("arbitrary"),
    )(sinks, proj, proj, proj, proj, proj, qg, kg, dy, proj)


def _conv_taps(win, w_ref, shift_of):
    rows = win.shape[0]
    acc = None
    for j in range(CONV_K):
        term = pltpu.roll(win, (rows - shift_of(j)) % rows, axis=0)[:BLK] * w_ref[j:j + 1, :]
        acc = term if acc is None else acc + term
    return acc


def _layer_norm_fwd(z):
    mu = jnp.mean(z, axis=-1, keepdims=True)
    zc = z - mu
    rstd = lax.rsqrt(jnp.mean(zc * zc, axis=-1, keepdims=True) + EPS)
    return zc * rstd, rstd


def _layer_norm_bwd(dy, yhat, rstd, g):
    dyh = dy * g
    return rstd * (dyh - jnp.mean(dyh, axis=-1, keepdims=True) - yhat * jnp.mean(dyh * yhat, axis=-1, keepdims=True))


def _conv_fill_glu(a_ref, g_ref, hpad, nb):
    hpad[0:HALO, :] = jnp.zeros((HALO, hpad.shape[1]), F32)

    def fill(i, c):
        r = pl.multiple_of(i * BLK, BLK)
        hpad[pl.ds(pl.multiple_of(r + HALO, HALO), BLK), :] = a_ref[pl.ds(r, BLK), :] * _sigmoid(g_ref[pl.ds(r, BLK), :])
        return c

    lax.fori_loop(0, nb, fill, 0)


def _conv_specs(t, cw, d_in):
    base = (d_in - 4 * cw) // cw
    vec = pl.BlockSpec((1, cw), lambda i: (0, 0))
    return [
        pl.BlockSpec((t, cw), lambda i: (0, base)),
        pl.BlockSpec((t, cw), lambda i: (0, base + 1)),
        pl.BlockSpec((HALO, cw), lambda i: (0, 0)),
        vec, vec, vec,
    ]


def _conv_fwd(proj, cw_pad, cb, lg, lb, cw):
    t, d_in = proj.shape
    nb = t // BLK

    def body(a_ref, g_ref, w_ref, b_ref, lg_ref, lb_ref, o_ref, hpad):
        _conv_fill_glu(a_ref, g_ref, hpad, nb)

        def blk(i, c):
            r = pl.multiple_of(i * BLK, BLK)
            z = _conv_taps(hpad[pl.ds(r, BLK + HALO), :], w_ref, lambda j: j + HALO - (CONV_K - 1)) + b_ref[...]
            yhat, _ = _layer_norm_fwd(z)
            y = yhat * lg_ref[...] + lb_ref[...]
            o_ref[pl.ds(r, BLK), :] = y * _sigmoid(y)
            return c

        lax.fori_loop(0, nb, blk, 0)

    return pl.pallas_call(
        body, name="conv_fwd", grid=(1,),
        in_specs=_conv_specs(t, cw, d_in),
        out_specs=pl.BlockSpec((t, cw), lambda i: (0, 0)),
        out_shape=SDS((t, cw), F32),
        scratch_shapes=[pltpu.VMEM((t + HALO, cw), F32)],
        compiler_params=_params("arbitrary"),
    )(proj, proj, cw_pad, cb, lg, lb)


def _conv_bwd(proj, dy, cw_pad, cb, lg, lb, cw):
    t, d_in = proj.shape
    nb = t // BLK

    def body(a_ref, g_ref, w_ref, b_ref, lg_ref, lb_ref, dy_ref, dc_ref, dw_ref, dvec_ref, hpad, dzpad, dwacc):
        _conv_fill_glu(a_ref, g_ref, hpad, nb)
        dzpad[t:t + HALO, :] = jnp.zeros((HALO, cw), F32)
        dwacc[...] = jnp.zeros_like(dwacc)

        def blk(i, carry):
            db, dlg, dlb = carry
            r = pl.multiple_of(i * BLK, BLK)
            win = hpad[pl.ds(r, BLK + HALO), :]
            z = _conv_taps(win, w_ref, lambda j: j + HALO - (CONV_K - 1)) + b_ref[...]
            yhat, rstd = _layer_norm_fwd(z)
            y = yhat * lg_ref[...] + lb_ref[...]
            sg = _sigmoid(y)
            dyl = dy_ref[pl.ds(r, BLK), :] * (sg * (1.0 + y * (1.0 - sg)))
            dz = _layer_norm_bwd(dyl, yhat, rstd, lg_ref[...])
            dzpad[pl.ds(r, BLK), :] = dz
            for j in range(CONV_K):
                sh = j + HALO - (CONV_K - 1)
                prod = dz * pltpu.roll(win, (BLK + HALO - sh) % (BLK + HALO), axis=0)[:BLK]
                dwacc[8 * j:8 * j + 8, :] += jnp.sum(prod.reshape(BLK // 8, 8, cw), axis=0)
            return (db + jnp.sum(dz, axis=0, keepdims=True),
                    dlg + jnp.sum(dyl * yhat, axis=0, keepdims=True),
                    dlb + jnp.sum(dyl, axis=0, keepdims=True))

        zero = jnp.zeros((1, cw), F32)
        db, dlg, dlb = lax.fori_loop(0, nb, blk, (zero, zero, zero))
        dvec_ref[...] = jnp.zeros_like(dvec_ref)
        dvec_ref[0:1, :] = db
        dvec_ref[1:2, :] = dlg
        dvec_ref[2:3, :] = dlb
        dw_ref[...] = jnp.sum(dwacc[...].reshape(HALO, 8, cw), axis=1)

        def blk2(i, c):
            r = pl.multiple_of(i * BLK, BLK)
            dh = _conv_taps(dzpad[pl.ds(r, BLK + HALO), :], w_ref, lambda j: CONV_K - 1 - j)
            a = a_ref[pl.ds(r, BLK), :]
            sg = _sigmoid(g_ref[pl.ds(r, BLK), :])
            dc_ref[pl.ds(r, BLK), 0:cw] = (dh * sg).astype(BF16)
            dc_ref[pl.ds(r, BLK), cw:2 * cw] = (dh * a * sg * (1.0 - sg)).astype(BF16)
            return c

        lax.fori_loop(0, nb, blk2, 0)

    return pl.pallas_call(
        body, name="conv_bwd", grid=(1,),
        in_specs=_conv_specs(t, cw, d_in) + [pl.BlockSpec((t, cw), lambda i: (0, 0))],
        out_specs=[pl.BlockSpec((t, 2 * cw), lambda i: (0, 0)), pl.BlockSpec((HALO, cw), lambda i: (0, 0)),
                   pl.BlockSpec((8, cw), lambda i: (0, 0))],
        out_shape=[SDS((t, 2 * cw), BF16), SDS((HALO, cw), F32), SDS((8, cw), F32)],
        scratch_shapes=[pltpu.VMEM((t + HALO, cw), F32), pltpu.VMEM((t + HALO, cw), F32), pltpu.VMEM((8 * HALO, cw), F32)],
        compiler_params=_params("arbitrary"),
    )(proj, proj, cw_pad, cb, lg, lb, dy)


def _tril_bf16(w):
    r = lax.broadcasted_iota(jnp.int32, (BLK, BLK), 0)
    c = lax.broadcasted_iota(jnp.int32, (BLK, BLK), 1)
    return jnp.where(r >= c, w, 0.0).astype(BF16)


def _sgu_specs(sw, d_in, heads):
    base = (d_in - 2 * sw) // sw
    vec = pl.BlockSpec((1, sw), lambda n: (0, 0))
    return [
        pl.BlockSpec((BLK, sw), lambda n: (n, base)),
        pl.BlockSpec((BLK, sw), lambda n: (n, base + 1)),
        vec, vec,
        pl.BlockSpec((heads, BLK, BLK), lambda n: (0, 0, 0)),
        pl.BlockSpec((BLK, sw), lambda n: (0, 0)),
    ]


def _sgu_mix(w_ref, vnb, heads, sw, transpose):
    head_of = lax.broadcasted_iota(jnp.int32, (1, sw), 1) // HEAD
    s = jnp.zeros((BLK, sw), F32)
    for h in range(heads):
        wt = _tril_bf16(w_ref[h])
        mixed = _tn(wt, vnb) if transpose else _nn(wt, vnb)
        s = jnp.where(head_of == h, mixed, s)
    return s


def _sgu_fwd(proj, lg, lb, w, bias_full, sw):
    t, d_in = proj.shape
    heads = sw // HEAD

    def body(u_ref, v_ref, lg_ref, lb_ref, w_ref, bias_ref, o_ref):
        vhat, _ = _layer_norm_fwd(v_ref[...])
        vn = (vhat * lg_ref[...] + lb_ref[...]).astype(BF16)
        s = _sgu_mix(w_ref, vn, heads, sw, False) + bias_ref[...]
        o_ref[...] = u_ref[...] * s

    return pl.pallas_call(
        body, name="sgu_fwd", grid=(t // BLK,),
        in_specs=_sgu_specs(sw, d_in, heads),
        out_specs=pl.BlockSpec((BLK, sw), lambda n: (n, 0)),
        out_shape=SDS((t, sw), F32),
        compiler_params=_params("arbitrary"),
    )(proj, proj, lg, lb, w, bias_full)


def _sgu_bwd(proj, dy, lg, lb, w, bias_full, sw):
    t, d_in = proj.shape
    heads = sw // HEAD
    nb = t // BLK

    def body(u_ref, v_ref, lg_ref, lb_ref, w_ref, bias_ref, dy_ref, ds_ref, dw_ref, db_ref, dvec_ref, dbfull):
        n = pl.program_id(0)

        @pl.when(n == 0)
        def _():
            dw_ref[...] = jnp.zeros_like(dw_ref)
            dvec_ref[...] = jnp.zeros_like(dvec_ref)
            dbfull[...] = jnp.zeros_like(dbfull)

        vhat, rstd = _layer_norm_fwd(v_ref[...])
        vn = (vhat * lg_ref[...] + lb_ref[...]).astype(BF16)
        s = _sgu_mix(w_ref, vn, heads, sw, False) + bias_ref[...]
        dy = dy_ref[...]
        ds_ref[:, 0:sw] = (dy * s).astype(BF16)
        dsv = dy * u_ref[...]
        dbfull[...] += dsv
        head_of = lax.broadcasted_iota(jnp.int32, (1, sw), 1) // HEAD
        r = lax.broadcasted_iota(jnp.int32, (BLK, BLK), 0)
        c = lax.broadcasted_iota(jnp.int32, (BLK, BLK), 1)
        dsb = dsv.astype(BF16)
        for h in range(heads):
            dsh = jnp.where(head_of == h, dsv, 0.0).astype(BF16)
            dw_ref[h] += jnp.where(r >= c, _nt(dsh, vn), 0.0)
        dvn = _sgu_mix(w_ref, dsb, heads, sw, True)
        dvec_ref[0:1, :] += jnp.sum(dvn * vhat, axis=0, keepdims=True)
        dvec_ref[1:2, :] += jnp.sum(dvn, axis=0, keepdims=True)
        ds_ref[:, sw:2 * sw] = _layer_norm_bwd(dvn, vhat, rstd, lg_ref[...]).astype(BF16)

        @pl.when(n == nb - 1)
        def _():
            sel = (lax.broadcasted_iota(jnp.int32, (sw, BLK), 0) // HEAD == lax.broadcasted_iota(jnp.int32, (sw, BLK), 1)).astype(BF16)
            x = dbfull[...]
            hi = x.astype(BF16)
            r1 = x - hi.astype(F32)
            mid = r1.astype(BF16)
            low = (r1 - mid.astype(F32)).astype(BF16)
            db_ref[...] = _nn(hi, sel) + _nn(mid, sel) + _nn(low, sel)

    return pl.pallas_call(
        body, name="sgu_bwd", grid=(nb,),
        in_specs=_sgu_specs(sw, d_in, heads) + [pl.BlockSpec((BLK, sw), lambda n: (n, 0))],
        out_specs=[pl.BlockSpec((BLK, 2 * sw), lambda n: (n, 0)), pl.BlockSpec((heads, BLK, BLK), lambda n: (0, 0, 0)),
                   pl.BlockSpec((BLK, BLK), lambda n: (0, 0)), pl.BlockSpec((8, sw), lambda n: (0, 0))],
        out_shape=[SDS((t, 2 * sw), BF16), SDS((heads, BLK, BLK), F32), SDS((BLK, BLK), F32), SDS((8, sw), F32)],
        scratch_shapes=[pltpu.VMEM((BLK, sw), F32)],
        compiler_params=_params("arbitrary"),
    )(proj, proj, lg, lb, w, bias_full, dy)


def _rms_fwd(x, g):
    return (x * lax.rsqrt(jnp.mean(x * x, axis=-1, keepdims=True) + EPS)) * g


def _rms_bwd(dh, x, g):
    rstd = lax.rsqrt(jnp.mean(x * x, axis=-1, keepdims=True) + EPS)
    xhat = x * rstd
    dgx = dh * g
    dx = rstd * (dgx - xhat * jnp.mean(dgx * xhat, axis=-1, keepdims=True))
    return dx, jnp.sum(dh * xhat, axis=0, keepdims=True)


def _rms_matmul(x, g, w, tm, tn, relu2, name):
    t, d = x.shape
    if w.ndim == 3:
        assert w.shape[2] == tn
        n = w.shape[0] * tn
        w_spec = pl.BlockSpec((None, d, tn), lambda i, j: (j, 0, 0))
    else:
        n = w.shape[1]
        w_spec = pl.BlockSpec((d, tn), lambda i, j: (0, j))

    def body(x_ref, g_ref, w_ref, h_ref, *outs):
        @pl.when(pl.program_id(1) == 0)
        def _():
            h_ref[...] = _rms_fwd(x_ref[...], g_ref[...]).astype(BF16)

        acc = _nn(h_ref[...], w_ref[...])
        if relu2:
            r = jnp.maximum(acc, 0.0)
            outs[0][...] = (r * r).astype(BF16)
            outs[1][...] = r.astype(BF16)
        else:
            outs[0][...] = acc

    tile = pl.BlockSpec((tm, tn), lambda i, j: (i, j))
    row = pl.BlockSpec((tm, d), lambda i, j: (i, 0))
    outs = [SDS((t, n), BF16), SDS((t, n), BF16)] if relu2 else [SDS((t, n), F32)]
    return pl.pallas_call(
        body, name=name, grid=(t // tm, n // tn),
        in_specs=[row, pl.BlockSpec((1, d), lambda i, j: (0, 0)), w_spec],
        out_specs=[row] + [tile] * len(outs),
        out_shape=[SDS((t, d), BF16)] + outs,
        compiler_params=_params("parallel", "arbitrary"),
    )(x, g, w)


def _group_rms_matmul(ys, g, w, res, tm, tn):
    t, d = res.shape
    widths = [y.shape[1] for y in ys]
    k = sum(widths)

    def body(*refs):
        y_refs, (g_ref, w_ref, res_ref, mix_ref, o_ref) = refs[:len(ys)], refs[len(ys):]

        @pl.when(pl.program_id(1) == 0)
        def _():
            c = 0
            for y_ref, wd in zip(y_refs, widths):
                mix_ref[:, c:c + wd] = _rms_fwd(y_ref[...], g_ref[:, c:c + wd]).astype(BF16)
                c += wd

        o_ref[...] = res_ref[...] + _nn(mix_ref[...], w_ref[...])

    tile = pl.BlockSpec((tm, tn), lambda i, j: (i, j))
    return pl.pallas_call(
        body, name="mix_out", grid=(t // tm, d // tn),
        in_specs=[pl.BlockSpec((tm, wd), lambda i, j: (i, 0)) for wd in widths] + [
            pl.BlockSpec((1, k), lambda i, j: (0, 0)), pl.BlockSpec((k, tn), lambda i, j: (0, j)), tile],
        out_specs=[pl.BlockSpec((tm, k), lambda i, j: (i, 0)), tile],
        out_shape=[SDS((t, k), BF16), SDS((t, d), F32)],
        compiler_params=_params("parallel", "arbitrary"),
    )(*ys, g, w, res)


def _matmul_res(a, w, res, tm, tn, tk):
    t, k = a.shape
    n = w.shape[1]
    nk = k // tk

    def body(a_ref, w_ref, res_ref, o_ref, acc):
        kk = pl.program_id(2)

        @pl.when(kk == 0)
        def _():
            acc[...] = res_ref[...]

        acc[...] += _nn(a_ref[...], w_ref[...])

        @pl.when(kk == nk - 1)
        def _():
            o_ref[...] = acc[...]

    tile = pl.BlockSpec((tm, tn), lambda i, j, kk: (i, j))
    return pl.pallas_call(
        body, name="mlp_down", grid=(t // tm, n // tn, nk),
        in_specs=[pl.BlockSpec((tm, tk), lambda i, j, kk: (i, kk)), pl.BlockSpec((tk, tn), lambda i, j, kk: (kk, j)), tile],
        out_specs=tile,
        out_shape=SDS((t, n), F32),
        scratch_shapes=[pltpu.VMEM((tm, tn), F32)],
        compiler_params=_params("parallel", "parallel", "arbitrary"),
    )(a, w, res)


def _loss_grad(y, target, tm):
    t, d = y.shape

    def body(y_ref, t_ref, dy_ref, l_ref):
        @pl.when(pl.program_id(0) == 0)
        def _():
            l_ref[...] = jnp.zeros_like(l_ref)

        err = y_ref[...] - t_ref[...]
        dy_ref[...] = err * (1.0 / d)
        per_row = jnp.mean(err * err, axis=-1, keepdims=True)
        l_ref[...] += jnp.sum(per_row, axis=0, keepdims=True) * 0.5

    row = pl.BlockSpec((tm, d), lambda i: (i, 0))
    return pl.pallas_call(
        body, name="loss_grad", grid=(t // tm,),
        in_specs=[row, row], out_specs=[row, pl.BlockSpec((8, BLK), lambda i: (0, 0))],
        out_shape=[SDS((t, d), F32), SDS((8, BLK), F32)],
        compiler_params=_params("arbitrary"),
    )(y, target)


def _mlp_dact(dx, w_down, r, tm, tn):
    t, d = dx.shape
    f = w_down.shape[0]

    def body(dx_ref, w_ref, r_ref, dxb_ref, dup_ref):
        @pl.when(pl.program_id(1) == 0)
        def _():
            dxb_ref[...] = dx_ref[...].astype(BF16)

        dup_ref[...] = (_nt(dxb_ref[...], w_ref[...]) * (2.0 * r_ref[...].astype(F32))).astype(BF16)

    row = pl.BlockSpec((tm, d), lambda i, j: (i, 0))
    tile = pl.BlockSpec((tm, tn), lambda i, j: (i, j))
    return pl.pallas_call(
        body, name="mlp_dact", grid=(t // tm, f // tn),
        in_specs=[row, pl.BlockSpec((tn, d), lambda i, j: (j, 0)), tile],
        out_specs=[row, tile],
        out_shape=[SDS((t, d), BF16), SDS((t, f), BF16)],
        compiler_params=_params("parallel", "arbitrary"),
    )(dx, w_down, r)


def _grad_w(a, bs, tm, tn, name, col_blocks=False):
    t, m = a.shape
    widths = [b.shape[1] for b in bs]
    if len(bs) == 1:
        assert widths[0] % tn == 0
        n_tiles = widths[0] // tn

        def body(a_ref, b_ref, o_ref):
            o_ref[...] = _tn(a_ref[...], b_ref[...]).astype(BF16)

        if col_blocks:
            out_spec = pl.BlockSpec((None, tm, tn), lambda i, j: (j, i, 0))
            out_shape = SDS((n_tiles, m, tn), BF16)
        else:
            out_spec = pl.BlockSpec((tm, tn), lambda i, j: (i, j))
            out_shape = SDS((m, widths[0]), BF16)
        return pl.pallas_call(
            body, name=name, grid=(m // tm, n_tiles),
            in_specs=[pl.BlockSpec((t, tm), lambda i, j: (0, i)), pl.BlockSpec((t, tn), lambda i, j: (0, j))],
            out_specs=out_spec, out_shape=out_shape,
            compiler_params=_params("parallel", "arbitrary"),
        )(a, bs[0])

    offs = [sum(widths[:p]) for p in range(len(bs))]

    def body(a_ref, *refs):
        b_refs, o_ref = refs[:-1], refs[-1]
        for b_ref, off, wd in zip(b_refs, offs, widths):
            o_ref[:, off:off + wd] = _tn(a_ref[...], b_ref[...]).astype(BF16)

    return pl.pallas_call(
        body, name=name, grid=(m // tm,),
        in_specs=[pl.BlockSpec((t, tm), lambda i: (0, i))] + [pl.BlockSpec((t, wd), lambda i: (0, 0)) for wd in widths],
        out_specs=pl.BlockSpec((tm, sum(widths)), lambda i: (i, 0)),
        out_shape=SDS((m, sum(widths)), BF16),
        compiler_params=_params("arbitrary"),
    )(a, *bs)


def _mlp_dnorm(dup, w_up, x, g, dres, tm):
    t, f = dup.shape
    d = x.shape[1]
    nk, _, tk = w_up.shape

    def body(a_ref, w_ref, x_ref, g_ref, dres_ref, dx_ref, dg_ref, acc):
        i, kk = pl.program_id(0), pl.program_id(1)

        @pl.when((i == 0) & (kk == 0))
        def _():
            dg_ref[...] = jnp.zeros_like(dg_ref)

        @pl.when(kk == 0)
        def _():
            acc[...] = jnp.zeros_like(acc)

        acc[...] += _nt(a_ref[...], w_ref[...])

        @pl.when(kk == nk - 1)
        def _():
            dx, dg = _rms_bwd(acc[...], x_ref[...], g_ref[...])
            dx_ref[...] = dres_ref[...] + dx
            dg_ref[0:1, :] += dg

    row = pl.BlockSpec((tm, d), lambda i, kk: (i, 0))
    return pl.pallas_call(
        body, name="mlp_dnorm", grid=(t // tm, nk),
        in_specs=[pl.BlockSpec((tm, tk), lambda i, kk: (i, kk)), pl.BlockSpec((None, d, tk), lambda i, kk: (kk, 0, 0)),
                  row, pl.BlockSpec((1, d), lambda i, kk: (0, 0)), row],
        out_specs=[row, pl.BlockSpec((8, d), lambda i, kk: (0, 0))],
        out_shape=[SDS((t, d), F32), SDS((8, d), F32)],
        scratch_shapes=[pltpu.VMEM((tm, d), F32)],
        compiler_params=_params("arbitrary", "arbitrary"),
    )(dup, w_up, x, g, dres)


def _mix_dnorm(dx, w_out, ys, g, tm):
    t, d = dx.shape
    k = w_out.shape[0]
    widths = [y.shape[1] for y in ys]

    def body(dx_ref, w_ref, *refs):
        y_refs = refs[:len(ys)]
        g_ref, dxb_ref = refs[len(ys)], refs[len(ys) + 1]
        dy_refs = refs[len(ys) + 2:2 * len(ys) + 2]
        dg_ref = refs[-1]

        @pl.when(pl.program_id(0) == 0)
        def _():
            dg_ref[...] = jnp.zeros_like(dg_ref)

        dxb = dx_ref[...].astype(BF16)
        dxb_ref[...] = dxb
        dmix = _nt(dxb, w_ref[...])
        c = 0
        for y_ref, dy_ref, wd in zip(y_refs, dy_refs, widths):
            dy, dg = _rms_bwd(dmix[:, c:c + wd], y_ref[...], g_ref[:, c:c + wd])
            dy_ref[...] = dy
            dg_ref[0:1, c:c + wd] += dg
            c += wd

    row = pl.BlockSpec((tm, d), lambda i: (i, 0))
    yspecs = [pl.BlockSpec((tm, wd), lambda i: (i, 0)) for wd in widths]
    return pl.pallas_call(
        body, name="mix_dnorm", grid=(t // tm,),
        in_specs=[row, pl.BlockSpec((k, d), lambda i: (0, 0))] + yspecs + [pl.BlockSpec((1, k), lambda i: (0, 0))],
        out_specs=[row] + yspecs + [pl.BlockSpec((8, k), lambda i: (0, 0))],
        out_shape=[SDS((t, d), BF16)] + [SDS((t, wd), F32) for wd in widths] + [SDS((8, k), F32)],
        compiler_params=_params("arbitrary"),
    )(dx, w_out, *ys, g)


def _in_dnorm(dps, w_in, x, g, dres, tm):
    t, d = x.shape
    widths = [p.shape[1] for p in dps]
    offs = [sum(widths[:p]) for p in range(len(dps))]
    n_in = w_in.shape[1]

    def body(*refs):
        p_refs = refs[:len(dps)]
        w_ref, x_ref, g_ref, dres_ref, dx_ref, dg_ref = refs[len(dps):]

        @pl.when(pl.program_id(0) == 0)
        def _():
            dg_ref[...] = jnp.zeros_like(dg_ref)

        acc = None
        for p_ref, off, wd in zip(p_refs, offs, widths):
            term = _nt(p_ref[...], w_ref[:, off:off + wd])
            acc = term if acc is None else acc + term
        dx, dg = _rms_bwd(acc, x_ref[...], g_ref[...])
        dx_ref[...] = dres_ref[...] + dx
        dg_ref[0:1, :] += dg

    row = pl.BlockSpec((tm, d), lambda i: (i, 0))
    return pl.pallas_call(
        body, name="in_dnorm", grid=(t // tm,),
        in_specs=[pl.BlockSpec((tm, wd), lambda i: (i, 0)) for wd in widths] + [
            pl.BlockSpec((d, n_in), lambda i: (0, 0)), row, pl.BlockSpec((1, d), lambda i: (0, 0)), row],
        out_specs=[row, pl.BlockSpec((8, d), lambda i: (0, 0))],
        out_shape=[SDS((t, d), F32), SDS((8, d), F32)],
        compiler_params=_params("arbitrary"),
    )(*dps, w_in, x, g, dres)


def _tile(n, want):
    return min(n, want)


def _layer_fwd(x, p):
    t, d = x.shape
    aw, kv, cw, sw = d // 2, d // 8, d // 4, d // 4
    tm = _tile(t, 512)
    h1, proj = _rms_matmul(x, p["ln1_g"], p["w_in"], tm, 512 if p["w_in"].shape[1] % 512 == 0 else 256, False, "in_proj")
    y_attn = _attn_fwd(proj, p["qg"], p["kg"], p["sinks"], aw, kv)
    y_conv = _conv_fwd(proj, p["conv_w"], p["conv_b"], p["conv_ln_g"], p["conv_ln_b"], cw)
    y_sgu = _sgu_fwd(proj, p["sgu_ln_g"], p["sgu_ln_b"], p["sgu_w"], p["sgu_bias"], sw)
    ys = [y_attn, y_conv, y_sgu]
    mix, x1 = _group_rms_matmul(ys, p["out_norm_g"], p["w_out"], x, tm, _tile(d, 1024))
    h2, act, r = _rms_matmul(x1, p["ln2_g"], p["w_up"], tm, p["w_up"].shape[2], True, "mlp_up")
    x2 = _matmul_res(act, p["w_down"], x1, tm, _tile(d, 1024), 1024)
    saved = dict(x=x, h1=h1, proj=proj, ys=ys, mix=mix, x1=x1, h2=h2, act=act, r=r)
    return x2, saved


def _layer_bwd(dx2, p, s):
    t, d = dx2.shape
    aw, kv, cw, sw = d // 2, d // 8, d // 4, d // 4
    tm = _tile(t, 512)
    dx2b, dup = _mlp_dact(dx2, p["w_down"], s["r"], tm, 1024)
    g_w_down = _grad_w(s["act"], [dx2b], 512, _tile(d, 2048), "grad_w_down")
    dx1, d_ln2 = _mlp_dnorm(dup, p["w_up"], s["x1"], p["ln2_g"], dx2, tm)
    g_w_up = _grad_w(s["h2"], [dup], 512, p["w_up"].shape[2], "grad_w_up", col_blocks=True)
    dx1b, dya, dyc, dys, d_onorm = _mix_dnorm(dx1, p["w_out"], s["ys"], p["out_norm_g"], _tile(t, 256))
    g_w_out = _grad_w(s["mix"], [dx1b], 512, _tile(d, 2048), "grad_w_out")
    dqkv, d_attn = _attn_bwd(s["proj"], dya, p["qg"], p["kg"], p["sinks"], aw, kv)
    dconv, d_cw, d_cvec = _conv_bwd(s["proj"], dyc, p["conv_w"], p["conv_b"], p["conv_ln_g"], p["conv_ln_b"], cw)
    dsgu, d_sw, d_sb, d_svec = _sgu_bwd(s["proj"], dys, p["sgu_ln_g"], p["sgu_ln_b"], p["sgu_w"], p["sgu_bias"], sw)
    dps = [dqkv, dconv, dsgu]
    dx, d_ln1 = _in_dnorm(dps, p["w_in"], s["x"], p["ln1_g"], dx1, _tile(t, 256))
    g_w_in = _grad_w(s["h1"], dps, 512, 512, "grad_w_in")
    heads = sw // HEAD
    small = dict(
        ln1_g=d_ln1[0], q_norm_g=d_attn[0, :HEAD], k_norm_g=d_attn[1, :HEAD], sinks=d_attn[2, :aw // HEAD],
        conv_w=d_cw[:CONV_K], conv_b=d_cvec[0], conv_ln_g=d_cvec[1], conv_ln_b=d_cvec[2],
        sgu_ln_g=d_svec[0], sgu_ln_b=d_svec[1], sgu_w=d_sw, sgu_b=d_sb[:, :heads].T,
        out_norm_g=d_onorm[0], ln2_g=d_ln2[0])
    big = dict(w_in=g_w_in, w_out=g_w_out, w_up=g_w_up, w_down=g_w_down)
    return dx, big, small


def _layer_params(l, big, small):
    row = lambda v: v[l][None, :]
    two = lambda v: jnp.tile(v[l], 2)[None, :]
    sw = small["sgu_ln_g"].shape[1]
    return dict(
        w_in=big["w_in"][l], w_out=big["w_out"][l], w_up=big["w_up"][l], w_down=big["w_down"][l],
        ln1_g=row(small["ln1_g"]), ln2_g=row(small["ln2_g"]), out_norm_g=row(small["out_norm_g"]),
        qg=two(small["q_norm_g"]), kg=two(small["k_norm_g"]), sinks=small["sinks"][l],
        conv_w=jnp.pad(small["conv_w"][l], ((0, HALO - CONV_K), (0, 0))),
        conv_b=row(small["conv_b"]), conv_ln_g=row(small["conv_ln_g"]), conv_ln_b=row(small["conv_ln_b"]),
        sgu_ln_g=row(small["sgu_ln_g"]), sgu_ln_b=row(small["sgu_ln_b"]), sgu_w=small["sgu_w"][l],
        sgu_bias=jnp.repeat(small["sgu_b"][l].T, HEAD, axis=1)[:, :sw],
    )


def _local_step(x, target, big, small, depth):
    params = [_layer_params(l, big, small) for l in range(depth)]
    saved = []
    h = x
    for l in range(depth):
        h, s = _layer_fwd(h, params[l])
        saved.append(s)
    dy, lsum = _loss_grad(h, target, _tile(x.shape[0], 512))
    gb, gs = [None] * depth, [None] * depth
    for l in reversed(range(depth)):
        dy, gb[l], gs[l] = _layer_bwd(dy, params[l], saved[l])
    return lsum[0, 0], dy, gb, gs


ANY = pl.BlockSpec(memory_space=pl.ANY)
BIG = ("w_in", "w_out", "w_up", "w_down")


def _mesh_pos():
    return lax.axis_index("x"), lax.axis_index("y"), lax.axis_index("c")


def _other_chips(x, y):
    return [(1 - x, y), (x, 1 - y), (1 - x, 1 - y)]


def _all_gather(shards, name):
    n = len(shards)

    def body(*refs):
        ins, outs = refs[:n], refs[n:2 * n]
        send_sems, recv_sems, local_sems = refs[2 * n:]
        x, y, c = _mesh_pos()
        me, sibling = (x, y, c), (x, y, 1 - c)
        chips = _other_chips(x, y)

        def copy(a, k, block, to, src=None):
            dst = outs[a].at[4 * block[0] + 2 * block[1] + block[2]]
            return pltpu.make_async_remote_copy(
                src_ref=dst if src is None else src, dst_ref=dst,
                send_sem=send_sems.at[a, k], recv_sem=recv_sems.at[a, k], device_id=to, device_id_type=MESH)

        mine = [pltpu.make_async_copy(ins[a], outs[a].at[4 * x + 2 * y + c], local_sems.at[a]) for a in range(n)]
        for cp in mine:
            cp.start()
        first = []
        for a in range(n):
            first.append(copy(a, 0, me, sibling, src=ins[a]))
            first += [copy(a, 1 + j, me, (*chip, c), src=ins[a]) for j, chip in enumerate(chips)]
        for cp in first:
            cp.start()
        passed = []
        for j, chip in enumerate(chips):
            for a in range(n):
                copy(a, 1 + j, (*chip, c), me).wait_recv()
                fwd = copy(a, 4 + j, (*chip, c), sibling)
                fwd.start()
                passed.append(fwd)
        for a in range(n):
            copy(a, 0, sibling, me).wait_recv()
            for j, chip in enumerate(chips):
                copy(a, 4 + j, (*chip, 1 - c), me).wait_recv()
        for cp in first + passed:
            cp.wait_send()
        for cp in mine:
            cp.wait()

    outs = pl.pallas_call(
        body, name=name,
        in_specs=[ANY] * n, out_specs=[ANY] * n,
        out_shape=[SDS((N_DEV,) + s.shape, s.dtype) for s in shards],
        scratch_shapes=[pltpu.SemaphoreType.DMA((n, 7)), pltpu.SemaphoreType.DMA((n, 7)), pltpu.SemaphoreType.DMA((n,))],
    )(*shards)
    return list(outs)


def _exchange_pair(grads, name):
    n = len(grads)

    def body(*refs):
        ins, outs = refs[:n], refs[n:2 * n]
        send_sems, recv_sems = refs[2 * n:]
        x, y, c = _mesh_pos()
        cps = []
        for a in range(n):
            for q in range(4):
                cps.append(pltpu.make_async_remote_copy(
                    src_ref=ins[a].at[2 * q + (1 - c)], dst_ref=outs[a].at[q],
                    send_sem=send_sems.at[a, q], recv_sem=recv_sems.at[a, q],
                    device_id=(x, y, 1 - c), device_id_type=MESH))
        for cp in cps:
            cp.start()
        for cp in cps:
            cp.wait()

    outs = pl.pallas_call(
        body, name=name,
        in_specs=[ANY] * n, out_specs=[ANY] * n,
        out_shape=[SDS((4,) + g.shape[1:], g.dtype) for g in grads],
        scratch_shapes=[pltpu.SemaphoreType.DMA((n, 4)), pltpu.SemaphoreType.DMA((n, 4))],
    )(*grads)
    return list(outs)


def _exchange_chips(sums, name):
    n = len(sums)

    def body(*refs):
        ins, outs = refs[:n], refs[n:2 * n]
        send_sems, recv_sems = refs[2 * n:]
        x, y, c = _mesh_pos()
        cps = []
        for a in range(n):
            for k, chip in enumerate(_other_chips(x, y)):
                cps.append(pltpu.make_async_remote_copy(
                    src_ref=ins[a].at[2 * chip[0] + chip[1]], dst_ref=outs[a].at[k],
                    send_sem=send_sems.at[a, k], recv_sem=recv_sems.at[a, k],
                    device_id=(*chip, c), device_id_type=MESH))
        for cp in cps:
            cp.start()
        for cp in cps:
            cp.wait()

    outs = pl.pallas_call(
        body, name=name,
        in_specs=[ANY] * n, out_specs=[ANY] * n,
        out_shape=[SDS((3,) + s.shape[1:], s.dtype) for s in sums],
        scratch_shapes=[pltpu.SemaphoreType.DMA((n, 3)), pltpu.SemaphoreType.DMA((n, 3))],
    )(*sums)
    return list(outs)


def _pair_add(own, got, c, name):
    _, r, cols = own.shape
    tr = _tile(r, 512)

    def body(c_ref, own_ref, got_ref, o_ref):
        o_ref[...] = (own_ref[...].astype(F32) + got_ref[...].astype(F32)).astype(BF16)

    return pl.pallas_call(
        body, name=name,
        grid_spec=pltpu.PrefetchScalarGridSpec(
            num_scalar_prefetch=1, grid=(4, r // tr),
            in_specs=[pl.BlockSpec((None, tr, cols), lambda q, i, c_ref: (2 * q + c_ref[0], i, 0)),
                      pl.BlockSpec((None, tr, cols), lambda q, i, c_ref: (q, i, 0))],
            out_specs=pl.BlockSpec((None, tr, cols), lambda q, i, c_ref: (q, i, 0))),
        out_shape=SDS((4, r, cols), BF16),
        compiler_params=_params("arbitrary", "arbitrary"),
    )(c, own, got)


def _adamw(w, g, m, v):
    m = ADAM_B1 * m + (1.0 - ADAM_B1) * g
    v = ADAM_B2 * v + (1.0 - ADAM_B2) * (g * g)
    m_hat = m / (1.0 - ADAM_B1 ** ADAM_STEP)
    v_hat = v / (1.0 - ADAM_B2 ** ADAM_STEP)
    delta = -ADAM_LR * (m_hat / (jnp.sqrt(v_hat) + ADAM_EPS) + ADAM_WD * w)
    return delta, m, v


def _adamw_big(sums, gots, chip, w, m, v, name):
    depth, r, cols = w.shape
    tr = _tile(r, 256)
    nt = r // tr

    def body(chip_ref, *refs):
        sum_refs, got_refs = refs[:depth], refs[depth:2 * depth]
        w_ref, m_ref, v_ref, g_out, d_out, m_out, v_out = refs[2 * depth:]
        l = pl.program_id(0)
        for layer in range(depth):
            @pl.when(l == layer)
            def _(layer=layer):
                got = got_refs[layer]
                g = sum_refs[layer][...].astype(F32) + got[0].astype(F32) + got[1].astype(F32) + got[2].astype(F32)
                delta, mm, vv = _adamw(w_ref[...], g, m_ref[...], v_ref[...])
                g_out[...] = g
                d_out[...] = delta
                m_out[...] = mm
                v_out[...] = vv

    def hold(layer):
        return lambda l, i: jnp.where(l == layer, i, jnp.where(l < layer, 0, nt - 1))

    shard = pl.BlockSpec((None, tr, cols), lambda l, i, chip_ref: (l, i, 0))
    return pl.pallas_call(
        body, name=name,
        grid_spec=pltpu.PrefetchScalarGridSpec(
            num_scalar_prefetch=1, grid=(depth, nt),
            in_specs=[pl.BlockSpec((None, tr, cols), lambda l, i, chip_ref, layer=layer: (chip_ref[0], hold(layer)(l, i), 0))
                      for layer in range(depth)]
            + [pl.BlockSpec((3, tr, cols), lambda l, i, chip_ref, layer=layer: (0, hold(layer)(l, i), 0)) for layer in range(depth)]
            + [shard] * 3,
            out_specs=[shard] * 4),
        out_shape=[SDS(w.shape, F32)] * 4,
        compiler_params=_params("arbitrary", "arbitrary"),
    )(chip, *sums, *gots, w, m, v)


def _sum_devices(stacked):
    _, r, _ = stacked.shape

    def body(s_ref, o_ref):
        acc = s_ref[0]
        for k in range(1, N_DEV):
            acc = acc + s_ref[k]
        o_ref[...] = acc

    return pl.pallas_call(
        body, name="sum_devices", grid=(1,),
        in_specs=[pl.BlockSpec((N_DEV, r, BLK), lambda i: (0, 0, 0))],
        out_specs=pl.BlockSpec((r, BLK), lambda i: (0, 0)),
        out_shape=SDS((r, BLK), F32),
        compiler_params=_params("arbitrary"),
    )(stacked)


def _adamw_small(g, w, m, v):
    def body(g_ref, w_ref, m_ref, v_ref, d_out, m_out, v_out):
        delta, mm, vv = _adamw(w_ref[...], g_ref[...], m_ref[...], v_ref[...])
        d_out[...] = delta
        m_out[...] = mm
        v_out[...] = vv

    spec = pl.BlockSpec(g.shape, lambda i: (0, 0))
    return pl.pallas_call(
        body, name="adamw_small", grid=(1,),
        in_specs=[spec] * 4, out_specs=[spec] * 3,
        out_shape=[SDS(g.shape, F32)] * 3,
        compiler_params=_params("arbitrary"),
    )(g, w, m, v)


def _pack(arrays):
    flat = jnp.concatenate([a.reshape(-1) for a in arrays])
    rows = -(-flat.shape[0] // (8 * BLK)) * 8
    return jnp.pad(flat, (0, rows * BLK - flat.shape[0])).reshape(rows, BLK)


def _unpack(packed, like):
    flat = packed.reshape(-1)
    out, off = [], 0
    for a in like:
        out.append(flat[off:off + a.size].reshape(a.shape))
        off += a.size
    return out


SMALL = ("ln1_g", "q_norm_g", "k_norm_g", "sinks", "conv_b", "conv_ln_g", "conv_ln_b", "sgu_ln_g", "sgu_ln_b",
         "sgu_w", "sgu_b", "out_norm_g", "ln2_g")
ORDER = ("ln1_g", "w_in", "q_norm_g", "k_norm_g", "sinks", "conv_w", "conv_b", "conv_ln_g", "conv_ln_b", "sgu_ln_g",
         "sgu_ln_b", "sgu_w", "sgu_b", "out_norm_g", "w_out", "ln2_g", "w_up", "w_down")


def _step(x, target, w, m, v):
    depth = w["ln1_g"].shape[0]
    xpos, ypos, cpos = _mesh_pos()
    me = 4 * xpos + 2 * ypos + cpos
    c_arr = jnp.reshape(cpos, (1,)).astype(jnp.int32)
    chip_arr = jnp.reshape(2 * xpos + ypos, (1,)).astype(jnp.int32)

    d = x.shape[1]
    gathered = {n: [] for n in BIG}
    for l in range(depth):
        got = _all_gather([w[n][l].astype(BF16) for n in BIG], "gather_weights")
        gathered["w_in"].append(jnp.transpose(got[0], (1, 0, 2)).reshape(d, -1))
        gathered["w_out"].append(got[1].reshape(-1, d))
        gathered["w_up"].append(got[2])
        gathered["w_down"].append(got[3].reshape(-1, d))
    cw = w["conv_w"]
    cw_all = _all_gather([_pack([cw])], "gather_conv_w")[0]
    cw_full = jnp.concatenate([_unpack(cw_all[k], [cw])[0] for k in range(N_DEV)], axis=-1)
    small = {n: w[n] for n in SMALL}
    small["conv_w"] = cw_full

    loss, grad_x, gb, gs = _local_step(x, target, gathered, small, depth)

    sums, gots = {n: [] for n in BIG}, {n: [] for n in BIG}
    for l in reversed(range(depth)):
        blocks = [
            jnp.transpose(gb[l]["w_in"].reshape(d, N_DEV, -1), (1, 0, 2)),
            gb[l]["w_out"].reshape(N_DEV, -1, d),
            gb[l]["w_up"],
            gb[l]["w_down"].reshape(N_DEV, -1, d),
        ]
        from_sibling = _exchange_pair(blocks, "reduce_pair")
        chip_sums = [_pair_add(b, s, c_arr, "pair_add_" + n) for b, s, n in zip(blocks, from_sibling, BIG)]
        from_chips = _exchange_chips(chip_sums, "reduce_chips")
        for n, s, g in zip(BIG, chip_sums, from_chips):
            sums[n].insert(0, s)
            gots[n].insert(0, g)

    out = {}
    for n in BIG:
        shape = w[n].shape
        as3d = lambda a: a.reshape(depth, -1, a.shape[-1])
        res = _adamw_big(sums[n], gots[n], chip_arr, as3d(w[n]), as3d(m[n]), as3d(v[n]), "adamw_" + n)
        out[n] = [r.reshape(shape) for r in res]

    names = SMALL + ("conv_w",)
    g_small = [jnp.stack([gs[l][n] for l in range(depth)]) for n in names]
    g_sum = _unpack(_sum_devices(_all_gather([_pack(g_small)], "gather_small_grads")[0]), g_small)
    cshard = cw.shape[-1]
    g_sum[-1] = lax.dynamic_slice_in_dim(g_sum[-1], me * cshard, cshard, axis=2)
    like = [w[n] for n in names]
    res = _adamw_small(_pack(g_sum), _pack(like), _pack([m[n] for n in names]), _pack([v[n] for n in names]))
    res = [_unpack(r, like) for r in res]
    for i, n in enumerate(names):
        out[n] = [g_sum[i], res[0][i], res[1][i], res[2][i]]

    loss = lax.psum(loss, ("x", "y", "c"))
    return (loss, grad_x[None]) + tuple(out[n][k] for k in range(4) for n in ORDER)


def kernel(x, ln1_g, w_in, q_norm_g, k_norm_g, sinks, conv_w, conv_b, conv_ln_g, conv_ln_b, sgu_ln_g, sgu_ln_b, sgu_w, sgu_b, out_norm_g, w_out, ln2_g, w_up, w_down, loss_target, m_ln1_g, m_w_in, m_q_norm_g, m_k_norm_g, m_sinks, m_conv_w, m_conv_b, m_conv_ln_g, m_conv_ln_b, m_sgu_ln_g, m_sgu_ln_b, m_sgu_w, m_sgu_b, m_out_norm_g, m_w_out, m_ln2_g, m_w_up, m_w_down, v_ln1_g, v_w_in, v_q_norm_g, v_k_norm_g, v_sinks, v_conv_w, v_conv_b, v_conv_ln_g, v_conv_ln_b, v_sgu_ln_g, v_sgu_ln_b, v_sgu_w, v_sgu_b, v_out_norm_g, v_w_out, v_ln2_g, v_w_up, v_w_down):
    w = dict(zip(ORDER, (ln1_g, w_in, q_norm_g, k_norm_g, sinks, conv_w, conv_b, conv_ln_g, conv_ln_b, sgu_ln_g, sgu_ln_b,
                         sgu_w, sgu_b, out_norm_g, w_out, ln2_g, w_up, w_down)))
    m = dict(zip(ORDER, (m_ln1_g, m_w_in, m_q_norm_g, m_k_norm_g, m_sinks, m_conv_w, m_conv_b, m_conv_ln_g, m_conv_ln_b,
                         m_sgu_ln_g, m_sgu_ln_b, m_sgu_w, m_sgu_b, m_out_norm_g, m_w_out, m_ln2_g, m_w_up, m_w_down)))
    v = dict(zip(ORDER, (v_ln1_g, v_w_in, v_q_norm_g, v_k_norm_g, v_sinks, v_conv_w, v_conv_b, v_conv_ln_g, v_conv_ln_b,
                         v_sgu_ln_g, v_sgu_ln_b, v_sgu_w, v_sgu_b, v_out_norm_g, v_w_out, v_ln2_g, v_w_up, v_w_down)))
    return _step(x[0], loss_target[0], w, m, v)
```

```python
import functools

import jax
import jax.numpy as jnp
from jax import lax
from jax.experimental import pallas as pl
from jax.experimental.pallas import tpu as pltpu

F32 = jnp.float32
BF16 = jnp.bfloat16
SDS = jax.ShapeDtypeStruct

EPS = 1e-6
NEG_INF = -1e30
HEAD = 64
BLK = 128
CONV_K = 31
HALO = 32
N_DEV = 8

ADAM_LR = 0.001
ADAM_B1 = 0.9
ADAM_B2 = 0.999
ADAM_EPS = 1e-08
ADAM_WD = 0.01
ADAM_STEP = 10

VMEM_LIMIT = 56 * 1024 * 1024

MESH = pl.DeviceIdType.MESH


def _params(*sem):
    return pltpu.CompilerParams(dimension_semantics=sem, vmem_limit_bytes=VMEM_LIMIT)


def _nt(a, b):
    return lax.dot_general(a, b, (((1,), (1,)), ((), ())), preferred_element_type=F32)


def _tn(a, b):
    return lax.dot_general(a, b, (((0,), (0,)), ((), ())), preferred_element_type=F32)


def _nn(a, b):
    return jnp.dot(a, b, preferred_element_type=F32)


def _sigmoid(x):
    return 1.0 / (1.0 + jnp.exp(-x))


ANY = pl.BlockSpec(memory_space=pl.ANY)


def _ordered_behind(body, n_in, in_specs, args, after):
    if after is None:
        return body, in_specs, args
    return (lambda *refs: body(*refs[:n_in], *refs[n_in + 1:])), list(in_specs) + [ANY], list(args) + [after]


def _seg_ones():
    r = lax.broadcasted_iota(jnp.int32, (BLK, BLK), 0) // HEAD
    c = lax.broadcasted_iota(jnp.int32, (BLK, BLK), 1) // HEAD
    return (r == c).astype(BF16)


def _segsum(x, ones):
    hi = x.astype(BF16)
    r1 = x - hi.astype(F32)
    mid = r1.astype(BF16)
    lo = (r1 - mid.astype(F32)).astype(BF16)
    return _nn(hi, ones) + _nn(mid, ones) + _nn(lo, ones)


def _head_rms(x, gain, ones):
    rstd = lax.rsqrt(_segsum(x * x, ones) * (1.0 / HEAD) + EPS)
    xhat = x * rstd
    return xhat * gain, xhat, rstd


def _expand(x, odd, lo):
    if odd:
        xl = pltpu.roll(jnp.where(lo, 0.0, x), HEAD, axis=1)
    else:
        xl = jnp.where(lo, x, 0.0)
    xh = pltpu.roll(xl, HEAD, axis=1)
    return jnp.concatenate([xl, xh], axis=0).astype(BF16)


def _fold(g2, odd, lo):
    r = g2.shape[0] // 2
    s = jnp.where(lo, g2[:r], 0.0) + pltpu.roll(jnp.where(lo, 0.0, g2[r:]), HEAD, axis=1)
    if odd:
        s = pltpu.roll(s, HEAD, axis=1)
    return s


def _attn_mask(n):
    qi = lax.broadcasted_iota(jnp.int32, (BLK, 2 * BLK), 0)
    sj = lax.broadcasted_iota(jnp.int32, (BLK, 2 * BLK), 1)
    rel = qi + BLK - sj
    return (rel >= 0) & (rel < BLK) & ((sj >= BLK) | (n > 0))


def _attn_specs(t, aw, kv):
    prev = lambda n: jnp.maximum(n - 1, 0)
    kb, vb = aw // kv, aw // kv + 1
    return [
        pl.BlockSpec(memory_space=pltpu.SMEM),
        pl.BlockSpec((BLK, aw), lambda n: (n, 0)),
        pl.BlockSpec((BLK, kv), lambda n: (prev(n), kb)),
        pl.BlockSpec((BLK, kv), lambda n: (n, kb)),
        pl.BlockSpec((BLK, kv), lambda n: (prev(n), vb)),
        pl.BlockSpec((BLK, kv), lambda n: (n, vb)),
        pl.BlockSpec((1, BLK), lambda n: (0, 0)),
        pl.BlockSpec((1, BLK), lambda n: (0, 0)),
    ]


def _softmax_pair(s2, valid, sink0, sink1):
    out, psink = [], []
    for half, sink in ((0, sink0), (1, sink1)):
        s = jnp.where(valid, s2[:, 2 * BLK * half:2 * BLK * (half + 1)], NEG_INF)
        m = jnp.maximum(jnp.max(s, axis=-1, keepdims=True), sink)
        p = jnp.exp(s - m)
        es = jnp.exp(sink - m)
        den = jnp.sum(p, axis=-1, keepdims=True) + es
        out.append(p / den)
        psink.append(es / den)
    return jnp.concatenate(out, axis=1), psink


def _attn_fwd(proj, qg, kg, sinks, aw, kv):
    t = proj.shape[0]
    n_pairs, n_kvblk = aw // BLK, kv // BLK

    def body(sink_ref, q_ref, kp_ref, kc_ref, vp_ref, vc_ref, qg_ref, kg_ref, o_ref):
        n = pl.program_id(0)
        ones = _seg_ones()
        lo = lax.broadcasted_iota(jnp.int32, (1, BLK), 1) < HEAD
        valid = _attn_mask(n)
        kraw = jnp.concatenate([kp_ref[...], kc_ref[...]], axis=0)
        vraw = jnp.concatenate([vp_ref[...], vc_ref[...]], axis=0)
        for b in range(n_kvblk):
            kn = _head_rms(kraw[:, BLK * b:BLK * (b + 1)], kg_ref[...], ones)[0]
            vb = vraw[:, BLK * b:BLK * (b + 1)]
            for odd in (0, 1):
                j = 2 * b + odd
                k2 = _expand(kn, odd, lo)
                v2 = _expand(vb, odd, lo)
                for p in (2 * j, 2 * j + 1):
                    qn = _head_rms(q_ref[:, BLK * p:BLK * (p + 1)], qg_ref[...], ones)[0]
                    s2 = _nt(qn.astype(BF16), k2) * (HEAD ** -0.5)
                    p2, _ = _softmax_pair(s2, valid, sink_ref[2 * p], sink_ref[2 * p + 1])
                    o_ref[:, BLK * p:BLK * (p + 1)] = _nn(p2.astype(BF16), v2)

    del n_pairs
    return pl.pallas_call(
        body, name="attn_fwd", grid=(t // BLK,),
        in_specs=_attn_specs(t, aw, kv),
        out_specs=pl.BlockSpec((BLK, aw), lambda n: (n, 0)),
        out_shape=SDS((t, aw), F32),
        compiler_params=_params("arbitrary"),
    )(sinks, proj, proj, proj, proj, proj, qg, kg)


def _attn_bwd(proj, dy, qg, kg, sinks, aw, kv, after=None):
    t = proj.shape[0]
    nb = t // BLK
    n_kvblk = kv // BLK
    kb = aw // kv

    def body(sink_ref, q_ref, kp_ref, kc_ref, vp_ref, vc_ref, qg_ref, kg_ref, dy_ref, kall_ref,
             dqkv_ref, dstat_ref, dk_acc, dv_acc, dqg_acc):
        n = pl.program_id(0)
        ones = _seg_ones()
        lane = lax.broadcasted_iota(jnp.int32, (1, BLK), 1)
        lo = lane < HEAD
        valid = _attn_mask(n)

        @pl.when(n == 0)
        def _():
            dk_acc[...] = jnp.zeros_like(dk_acc)
            dv_acc[...] = jnp.zeros_like(dv_acc)
            dqg_acc[...] = jnp.zeros_like(dqg_acc)
            dstat_ref[...] = jnp.zeros_like(dstat_ref)

        kraw = jnp.concatenate([kp_ref[...], kc_ref[...]], axis=0)
        vraw = jnp.concatenate([vp_ref[...], vc_ref[...]], axis=0)
        row = pl.multiple_of(n * BLK, BLK)
        prow = pl.multiple_of(jnp.maximum(n - 1, 0) * BLK, BLK)
        dsink = jnp.zeros((1, BLK), F32)
        for b in range(n_kvblk):
            kn = _head_rms(kraw[:, BLK * b:BLK * (b + 1)], kg_ref[...], ones)[0]
            vb = vraw[:, BLK * b:BLK * (b + 1)]
            dkn = jnp.zeros((2 * BLK, BLK), F32)
            dvb = jnp.zeros((2 * BLK, BLK), F32)
            for odd in (0, 1):
                j = 2 * b + odd
                k2 = _expand(kn, odd, lo)
                v2 = _expand(vb, odd, lo)
                dk2 = jnp.zeros((4 * BLK, BLK), F32)
                dv2 = jnp.zeros((4 * BLK, BLK), F32)
                for p in (2 * j, 2 * j + 1):
                    cols = slice(BLK * p, BLK * (p + 1))
                    qn, qhat, rstd = _head_rms(q_ref[:, cols], qg_ref[...], ones)
                    qb = qn.astype(BF16)
                    s2 = _nt(qb, k2) * (HEAD ** -0.5)
                    p2, psink = _softmax_pair(s2, valid, sink_ref[2 * p], sink_ref[2 * p + 1])
                    dob = dy_ref[:, cols].astype(BF16)
                    dp2 = _nt(dob, v2)
                    ds = []
                    for half in (0, 1):
                        hs = slice(2 * BLK * half, 2 * BLK * (half + 1))
                        ph = p2[:, hs]
                        delta = jnp.sum(ph * dp2[:, hs], axis=-1, keepdims=True)
                        ds.append(ph * (dp2[:, hs] - delta))
                        dsk = -jnp.sum(psink[half] * delta, axis=0, keepdims=True)
                        dsink = dsink + jnp.where(lane == 2 * p + half, dsk, 0.0)
                    ds2 = (jnp.concatenate(ds, axis=1) * (HEAD ** -0.5)).astype(BF16)
                    dqn = _nn(ds2, k2)
                    dk2 = dk2 + _tn(ds2, qb)
                    dv2 = dv2 + _tn(p2.astype(BF16), dob)
                    dqhat = dqn * qg_ref[...]
                    proj_q = _segsum(dqhat * qhat, ones) * (1.0 / HEAD)
                    dqkv_ref[pl.ds(row, BLK), cols] = (rstd * (dqhat - qhat * proj_q)).astype(BF16)
                    dqg_acc[:, cols] += jnp.sum(dqn * qhat, axis=0, keepdims=True)
                dkn = dkn + _fold(dk2, odd, lo)
                dvb = dvb + _fold(dv2, odd, lo)
            kcols = slice(BLK * b, BLK * (b + 1))
            dk_acc[pl.ds(prow, BLK), kcols] += dkn[:BLK]
            dv_acc[pl.ds(prow, BLK), kcols] += dvb[:BLK]
            dk_acc[pl.ds(row, BLK), kcols] += dkn[BLK:]
            dv_acc[pl.ds(row, BLK), kcols] += dvb[BLK:]
        dstat_ref[2:3, :] += dsink

        @pl.when(n == nb - 1)
        def _():
            dqg = dqg_acc[:, 0:BLK]
            for p in range(1, aw // BLK):
                dqg = dqg + dqg_acc[:, BLK * p:BLK * (p + 1)]
            dstat_ref[0:1, :] = dqg + pltpu.roll(dqg, HEAD, axis=1)

            def kblock(i, dkg):
                r = pl.multiple_of(i * BLK, BLK)
                for b in range(n_kvblk):
                    kcols = slice(BLK * b, BLK * (b + 1))
                    _, khat, rstd = _head_rms(kall_ref[pl.ds(r, BLK), kcols], kg_ref[...], ones)
                    dkn = dk_acc[pl.ds(r, BLK), kcols]
                    dkhat = dkn * kg_ref[...]
                    proj_k = _segsum(dkhat * khat, ones) * (1.0 / HEAD)
                    dqkv_ref[pl.ds(r, BLK), aw + BLK * b:aw + BLK * (b + 1)] = (rstd * (dkhat - khat * proj_k)).astype(BF16)
                    dqkv_ref[pl.ds(r, BLK), aw + kv + BLK * b:aw + kv + BLK * (b + 1)] = dv_acc[pl.ds(r, BLK), kcols].astype(BF16)
                    dkg = dkg + jnp.sum(dkn * khat, axis=0, keepdims=True)
                return dkg

            dkg = lax.fori_loop(0, nb, kblock, jnp.zeros((1, BLK), F32))
            dstat_ref[1:2, :] = dkg + pltpu.roll(dkg, HEAD, axis=1)

    in_specs = _attn_specs(t, aw, kv) + [
        pl.BlockSpec((BLK, aw), lambda n: (n, 0)),
        pl.BlockSpec((t, kv), lambda n: (0, kb)),
    ]
    args = [sinks, proj, proj, proj, proj, proj, qg, kg, dy, proj]
    body, in_specs, args = _ordered_behind(body, len(args), in_specs, args, after)
    return pl.pallas_call(
        body, name="attn_bwd", grid=(nb,),
        in_specs=in_specs,
        out_specs=[pl.BlockSpec((t, aw + 2 * kv), lambda n: (0, 0)), pl.BlockSpec((8, BLK), lambda n: (0, 0))],
        out_shape=[SDS((t, aw + 2 * kv), BF16), SDS((8, BLK), F32)],
        scratch_shapes=[pltpu.VMEM((t, kv), F32), pltpu.VMEM((t, kv), F32), pltpu.VMEM((1, aw), F32)],
        compiler_params=_params("arbitrary"),
    )(*args)


def _conv_taps(win, w_ref, shift_of):
    rows = win.shape[0]
    acc = None
    for j in range(CONV_K):
        term = pltpu.roll(win, (rows - shift_of(j)) % rows, axis=0)[:BLK] * w_ref[j:j + 1, :]
        acc = term if acc is None else acc + term
    return acc


def _layer_norm_fwd(z):
    mu = jnp.mean(z, axis=-1, keepdims=True)
    zc = z - mu
    rstd = lax.rsqrt(jnp.mean(zc * zc, axis=-1, keepdims=True) + EPS)
    return zc * rstd, rstd


def _layer_norm_bwd(dy, yhat, rstd, g):
    dyh = dy * g
    return rstd * (dyh - jnp.mean(dyh, axis=-1, keepdims=True) - yhat * jnp.mean(dyh * yhat, axis=-1, keepdims=True))


def _conv_fill_glu(a_ref, g_ref, hpad, nb):
    hpad[0:HALO, :] = jnp.zeros((HALO, hpad.shape[1]), F32)

    def fill(i, c):
        r = pl.multiple_of(i * BLK, BLK)
        hpad[pl.ds(pl.multiple_of(r + HALO, HALO), BLK), :] = a_ref[pl.ds(r, BLK), :] * _sigmoid(g_ref[pl.ds(r, BLK), :])
        return c

    lax.fori_loop(0, nb, fill, 0)


def _conv_specs(t, cw, d_in):
    base = (d_in - 4 * cw) // cw
    vec = pl.BlockSpec((1, cw), lambda i: (0, 0))
    return [
        pl.BlockSpec((t, cw), lambda i: (0, base)),
        pl.BlockSpec((t, cw), lambda i: (0, base + 1)),
        pl.BlockSpec((HALO, cw), lambda i: (0, 0)),
        vec, vec, vec,
    ]


def _conv_fwd(proj, cw_pad, cb, lg, lb, cw):
    t, d_in = proj.shape
    nb = t // BLK

    def body(a_ref, g_ref, w_ref, b_ref, lg_ref, lb_ref, o_ref, hpad):
        _conv_fill_glu(a_ref, g_ref, hpad, nb)

        def blk(i, c):
            r = pl.multiple_of(i * BLK, BLK)
            z = _conv_taps(hpad[pl.ds(r, BLK + HALO), :], w_ref, lambda j: j + HALO - (CONV_K - 1)) + b_ref[...]
            yhat, _ = _layer_norm_fwd(z)
            y = yhat * lg_ref[...] + lb_ref[...]
            o_ref[pl.ds(r, BLK), :] = y * _sigmoid(y)
            return c

        lax.fori_loop(0, nb, blk, 0)

    return pl.pallas_call(
        body, name="conv_fwd", grid=(1,),
        in_specs=_conv_specs(t, cw, d_in),
        out_specs=pl.BlockSpec((t, cw), lambda i: (0, 0)),
        out_shape=SDS((t, cw), F32),
        scratch_shapes=[pltpu.VMEM((t + HALO, cw), F32)],
        compiler_params=_params("arbitrary"),
    )(proj, proj, cw_pad, cb, lg, lb)


def _conv_bwd(proj, dy, cw_pad, cb, lg, lb, cw):
    t, d_in = proj.shape
    nb = t // BLK

    def body(a_ref, g_ref, w_ref, b_ref, lg_ref, lb_ref, dy_ref, dc_ref, dw_ref, dvec_ref, hpad, dzpad, dwacc):
        _conv_fill_glu(a_ref, g_ref, hpad, nb)
        dzpad[t:t + HALO, :] = jnp.zeros((HALO, cw), F32)
        dwacc[...] = jnp.zeros_like(dwacc)

        def blk(i, carry):
            db, dlg, dlb = carry
            r = pl.multiple_of(i * BLK, BLK)
            win = hpad[pl.ds(r, BLK + HALO), :]
            z = _conv_taps(win, w_ref, lambda j: j + HALO - (CONV_K - 1)) + b_ref[...]
            yhat, rstd = _layer_norm_fwd(z)
            y = yhat * lg_ref[...] + lb_ref[...]
            sg = _sigmoid(y)
            dyl = dy_ref[pl.ds(r, BLK), :] * (sg * (1.0 + y * (1.0 - sg)))
            dz = _layer_norm_bwd(dyl, yhat, rstd, lg_ref[...])
            dzpad[pl.ds(r, BLK), :] = dz
            for j in range(CONV_K):
                sh = j + HALO - (CONV_K - 1)
                prod = dz * pltpu.roll(win, (BLK + HALO - sh) % (BLK + HALO), axis=0)[:BLK]
                dwacc[8 * j:8 * j + 8, :] += jnp.sum(prod.reshape(BLK // 8, 8, cw), axis=0)
            return (db + jnp.sum(dz, axis=0, keepdims=True),
                    dlg + jnp.sum(dyl * yhat, axis=0, keepdims=True),
                    dlb + jnp.sum(dyl, axis=0, keepdims=True))

        zero = jnp.zeros((1, cw), F32)
        db, dlg, dlb = lax.fori_loop(0, nb, blk, (zero, zero, zero))
        dvec_ref[...] = jnp.zeros_like(dvec_ref)
        dvec_ref[0:1, :] = db
        dvec_ref[1:2, :] = dlg
        dvec_ref[2:3, :] = dlb
        dw_ref[...] = jnp.sum(dwacc[...].reshape(HALO, 8, cw), axis=1)

        def blk2(i, c):
            r = pl.multiple_of(i * BLK, BLK)
            dh = _conv_taps(dzpad[pl.ds(r, BLK + HALO), :], w_ref, lambda j: CONV_K - 1 - j)
            a = a_ref[pl.ds(r, BLK), :]
            sg = _sigmoid(g_ref[pl.ds(r, BLK), :])
            dc_ref[pl.ds(r, BLK), 0:cw] = (dh * sg).astype(BF16)
            dc_ref[pl.ds(r, BLK), cw:2 * cw] = (dh * a * sg * (1.0 - sg)).astype(BF16)
            return c

        lax.fori_loop(0, nb, blk2, 0)

    return pl.pallas_call(
        body, name="conv_bwd", grid=(1,),
        in_specs=_conv_specs(t, cw, d_in) + [pl.BlockSpec((t, cw), lambda i: (0, 0))],
        out_specs=[pl.BlockSpec((t, 2 * cw), lambda i: (0, 0)), pl.BlockSpec((HALO, cw), lambda i: (0, 0)),
                   pl.BlockSpec((8, cw), lambda i: (0, 0))],
        out_shape=[SDS((t, 2 * cw), BF16), SDS((HALO, cw), F32), SDS((8, cw), F32)],
        scratch_shapes=[pltpu.VMEM((t + HALO, cw), F32), pltpu.VMEM((t + HALO, cw), F32), pltpu.VMEM((8 * HALO, cw), F32)],
        compiler_params=_params("arbitrary"),
    )(proj, proj, cw_pad, cb, lg, lb, dy)


def _tril_bf16(w):
    r = lax.broadcasted_iota(jnp.int32, (BLK, BLK), 0)
    c = lax.broadcasted_iota(jnp.int32, (BLK, BLK), 1)
    return jnp.where(r >= c, w, 0.0).astype(BF16)


def _sgu_specs(sw, d_in, heads):
    base = (d_in - 2 * sw) // sw
    vec = pl.BlockSpec((1, sw), lambda n: (0, 0))
    return [
        pl.BlockSpec((BLK, sw), lambda n: (n, base)),
        pl.BlockSpec((BLK, sw), lambda n: (n, base + 1)),
        vec, vec,
        pl.BlockSpec((heads, BLK, BLK), lambda n: (0, 0, 0)),
        pl.BlockSpec((BLK, sw), lambda n: (0, 0)),
    ]


def _sgu_mix(w_ref, vnb, heads, sw, transpose):
    head_of = lax.broadcasted_iota(jnp.int32, (1, sw), 1) // HEAD
    s = jnp.zeros((BLK, sw), F32)
    for h in range(heads):
        wt = _tril_bf16(w_ref[h])
        mixed = _tn(wt, vnb) if transpose else _nn(wt, vnb)
        s = jnp.where(head_of == h, mixed, s)
    return s


def _sgu_fwd(proj, lg, lb, w, bias_full, sw):
    t, d_in = proj.shape
    heads = sw // HEAD

    def body(u_ref, v_ref, lg_ref, lb_ref, w_ref, bias_ref, o_ref):
        vhat, _ = _layer_norm_fwd(v_ref[...])
        vn = (vhat * lg_ref[...] + lb_ref[...]).astype(BF16)
        s = _sgu_mix(w_ref, vn, heads, sw, False) + bias_ref[...]
        o_ref[...] = u_ref[...] * s

    return pl.pallas_call(
        body, name="sgu_fwd", grid=(t // BLK,),
        in_specs=_sgu_specs(sw, d_in, heads),
        out_specs=pl.BlockSpec((BLK, sw), lambda n: (n, 0)),
        out_shape=SDS((t, sw), F32),
        compiler_params=_params("arbitrary"),
    )(proj, proj, lg, lb, w, bias_full)


def _sgu_bwd(proj, dy, lg, lb, w, bias_full, sw):
    t, d_in = proj.shape
    heads = sw // HEAD
    nb = t // BLK

    def body(u_ref, v_ref, lg_ref, lb_ref, w_ref, bias_ref, dy_ref, ds_ref, dw_ref, db_ref, dvec_ref, dbfull):
        n = pl.program_id(0)

        @pl.when(n == 0)
        def _():
            dw_ref[...] = jnp.zeros_like(dw_ref)
            dvec_ref[...] = jnp.zeros_like(dvec_ref)
            dbfull[...] = jnp.zeros_like(dbfull)

        vhat, rstd = _layer_norm_fwd(v_ref[...])
        vn = (vhat * lg_ref[...] + lb_ref[...]).astype(BF16)
        s = _sgu_mix(w_ref, vn, heads, sw, False) + bias_ref[...]
        dy = dy_ref[...]
        ds_ref[:, 0:sw] = (dy * s).astype(BF16)
        dsv = dy * u_ref[...]
        dbfull[...] += dsv
        head_of = lax.broadcasted_iota(jnp.int32, (1, sw), 1) // HEAD
        r = lax.broadcasted_iota(jnp.int32, (BLK, BLK), 0)
        c = lax.broadcasted_iota(jnp.int32, (BLK, BLK), 1)
        dsb = dsv.astype(BF16)
        for h in range(heads):
            dsh = jnp.where(head_of == h, dsv, 0.0).astype(BF16)
            dw_ref[h] += jnp.where(r >= c, _nt(dsh, vn), 0.0)
        dvn = _sgu_mix(w_ref, dsb, heads, sw, True)
        dvec_ref[0:1, :] += jnp.sum(dvn * vhat, axis=0, keepdims=True)
        dvec_ref[1:2, :] += jnp.sum(dvn, axis=0, keepdims=True)
        ds_ref[:, sw:2 * sw] = _layer_norm_bwd(dvn, vhat, rstd, lg_ref[...]).astype(BF16)

        @pl.when(n == nb - 1)
        def _():
            sel = (lax.broadcasted_iota(jnp.int32, (sw, BLK), 0) // HEAD == lax.broadcasted_iota(jnp.int32, (sw, BLK), 1)).astype(BF16)
            x = dbfull[...]
            hi = x.astype(BF16)
            r1 = x - hi.astype(F32)
            mid = r1.astype(BF16)
            low = (r1 - mid.astype(F32)).astype(BF16)
            db_ref[...] = _nn(hi, sel) + _nn(mid, sel) + _nn(low, sel)

    return pl.pallas_call(
        body, name="sgu_bwd", grid=(nb,),
        in_specs=_sgu_specs(sw, d_in, heads) + [pl.BlockSpec((BLK, sw), lambda n: (n, 0))],
        out_specs=[pl.BlockSpec((BLK, 2 * sw), lambda n: (n, 0)), pl.BlockSpec((heads, BLK, BLK), lambda n: (0, 0, 0)),
                   pl.BlockSpec((BLK, BLK), lambda n: (0, 0)), pl.BlockSpec((8, sw), lambda n: (0, 0))],
        out_shape=[SDS((t, 2 * sw), BF16), SDS((heads, BLK, BLK), F32), SDS((BLK, BLK), F32), SDS((8, sw), F32)],
        scratch_shapes=[pltpu.VMEM((BLK, sw), F32)],
        compiler_params=_params("arbitrary"),
    )(proj, proj, lg, lb, w, bias_full, dy)


def _rms_fwd(x, g):
    return (x * lax.rsqrt(jnp.mean(x * x, axis=-1, keepdims=True) + EPS)) * g


def _rms_bwd(dh, x, g):
    rstd = lax.rsqrt(jnp.mean(x * x, axis=-1, keepdims=True) + EPS)
    xhat = x * rstd
    dgx = dh * g
    dx = rstd * (dgx - xhat * jnp.mean(dgx * xhat, axis=-1, keepdims=True))
    return dx, jnp.sum(dh * xhat, axis=0, keepdims=True)


def _rms_matmul(x, g, w, tm, tn, relu2, name):
    t, d = x.shape
    if w.ndim == 3:
        assert w.shape[2] == tn
        n = w.shape[0] * tn
        w_spec = pl.BlockSpec((None, d, tn), lambda i, j: (j, 0, 0))
    else:
        n = w.shape[1]
        w_spec = pl.BlockSpec((d, tn), lambda i, j: (0, j))

    def body(x_ref, g_ref, w_ref, h_ref, *outs):
        @pl.when(pl.program_id(1) == 0)
        def _():
            h_ref[...] = _rms_fwd(x_ref[...], g_ref[...]).astype(BF16)

        acc = _nn(h_ref[...], w_ref[...])
        if relu2:
            r = jnp.maximum(acc, 0.0)
            outs[0][...] = (r * r).astype(BF16)
            outs[1][...] = r.astype(BF16)
        else:
            outs[0][...] = acc

    tile = pl.BlockSpec((tm, tn), lambda i, j: (i, j))
    row = pl.BlockSpec((tm, d), lambda i, j: (i, 0))
    outs = [SDS((t, n), BF16), SDS((t, n), BF16)] if relu2 else [SDS((t, n), F32)]
    return pl.pallas_call(
        body, name=name, grid=(t // tm, n // tn),
        in_specs=[row, pl.BlockSpec((1, d), lambda i, j: (0, 0)), w_spec],
        out_specs=[row] + [tile] * len(outs),
        out_shape=[SDS((t, d), BF16)] + outs,
        compiler_params=_params("parallel", "arbitrary"),
    )(x, g, w)


def _group_rms_matmul(ys, g, w, res, tm, tn):
    t, d = res.shape
    widths = [y.shape[1] for y in ys]
    k = sum(widths)

    def body(*refs):
        y_refs, (g_ref, w_ref, res_ref, mix_ref, o_ref) = refs[:len(ys)], refs[len(ys):]

        @pl.when(pl.program_id(1) == 0)
        def _():
            c = 0
            for y_ref, wd in zip(y_refs, widths):
                mix_ref[:, c:c + wd] = _rms_fwd(y_ref[...], g_ref[:, c:c + wd]).astype(BF16)
                c += wd

        o_ref[...] = res_ref[...] + _nn(mix_ref[...], w_ref[...])

    tile = pl.BlockSpec((tm, tn), lambda i, j: (i, j))
    return pl.pallas_call(
        body, name="mix_out", grid=(t // tm, d // tn),
        in_specs=[pl.BlockSpec((tm, wd), lambda i, j: (i, 0)) for wd in widths] + [
            pl.BlockSpec((1, k), lambda i, j: (0, 0)), pl.BlockSpec((k, tn), lambda i, j: (0, j)), tile],
        out_specs=[pl.BlockSpec((tm, k), lambda i, j: (i, 0)), tile],
        out_shape=[SDS((t, k), BF16), SDS((t, d), F32)],
        compiler_params=_params("parallel", "arbitrary"),
    )(*ys, g, w, res)


def _matmul_res(a, w, res, tm, tn, tk):
    t, k = a.shape
    n = w.shape[1]
    nk = k // tk

    def body(a_ref, w_ref, res_ref, o_ref, acc):
        kk = pl.program_id(2)

        @pl.when(kk == 0)
        def _():
            acc[...] = res_ref[...]

        acc[...] += _nn(a_ref[...], w_ref[...])

        @pl.when(kk == nk - 1)
        def _():
            o_ref[...] = acc[...]

    tile = pl.BlockSpec((tm, tn), lambda i, j, kk: (i, j))
    return pl.pallas_call(
        body, name="mlp_down", grid=(t // tm, n // tn, nk),
        in_specs=[pl.BlockSpec((tm, tk), lambda i, j, kk: (i, kk)), pl.BlockSpec((tk, tn), lambda i, j, kk: (kk, j)), tile],
        out_specs=tile,
        out_shape=SDS((t, n), F32),
        scratch_shapes=[pltpu.VMEM((tm, tn), F32)],
        compiler_params=_params("parallel", "parallel", "arbitrary"),
    )(a, w, res)


def _loss_grad(y, target, tm):
    t, d = y.shape

    def body(y_ref, t_ref, dy_ref, l_ref):
        @pl.when(pl.program_id(0) == 0)
        def _():
            l_ref[...] = jnp.zeros_like(l_ref)

        err = y_ref[...] - t_ref[...]
        dy_ref[...] = err * (1.0 / d)
        per_row = jnp.mean(err * err, axis=-1, keepdims=True)
        l_ref[...] += jnp.sum(per_row, axis=0, keepdims=True) * 0.5

    row = pl.BlockSpec((tm, d), lambda i: (i, 0))
    return pl.pallas_call(
        body, name="loss_grad", grid=(t // tm,),
        in_specs=[row, row], out_specs=[row, pl.BlockSpec((8, BLK), lambda i: (0, 0))],
        out_shape=[SDS((t, d), F32), SDS((8, BLK), F32)],
        compiler_params=_params("arbitrary"),
    )(y, target)


def _mlp_dact(dx, w_down, r, tm, tn, after=None):
    t, d = dx.shape
    f = w_down.shape[0]

    def body(dx_ref, w_ref, r_ref, dxb_ref, dup_ref):
        @pl.when(pl.program_id(1) == 0)
        def _():
            dxb_ref[...] = dx_ref[...].astype(BF16)

        dup_ref[...] = (_nt(dxb_ref[...], w_ref[...]) * (2.0 * r_ref[...].astype(F32))).astype(BF16)

    row = pl.BlockSpec((tm, d), lambda i, j: (i, 0))
    tile = pl.BlockSpec((tm, tn), lambda i, j: (i, j))
    body, in_specs, args = _ordered_behind(
        body, 3, [row, pl.BlockSpec((tn, d), lambda i, j: (j, 0)), tile], [dx, w_down, r], after)
    return pl.pallas_call(
        body, name="mlp_dact", grid=(t // tm, f // tn),
        in_specs=in_specs,
        out_specs=[row, tile],
        out_shape=[SDS((t, d), BF16), SDS((t, f), BF16)],
        compiler_params=_params("parallel", "arbitrary"),
    )(*args)


def _grad_w(a, bs, tm, tn, name, col_blocks=False):
    t, m = a.shape
    widths = [b.shape[1] for b in bs]
    if len(bs) == 1:
        assert widths[0] % tn == 0
        n_tiles = widths[0] // tn

        def body(a_ref, b_ref, o_ref):
            o_ref[...] = _tn(a_ref[...], b_ref[...]).astype(BF16)

        if col_blocks:
            out_spec = pl.BlockSpec((None, tm, tn), lambda i, j: (j, i, 0))
            out_shape = SDS((n_tiles, m, tn), BF16)
        else:
            out_spec = pl.BlockSpec((tm, tn), lambda i, j: (i, j))
            out_shape = SDS((m, widths[0]), BF16)
        return pl.pallas_call(
            body, name=name, grid=(m // tm, n_tiles),
            in_specs=[pl.BlockSpec((t, tm), lambda i, j: (0, i)), pl.BlockSpec((t, tn), lambda i, j: (0, j))],
            out_specs=out_spec, out_shape=out_shape,
            compiler_params=_params("parallel", "arbitrary"),
        )(a, bs[0])

    offs = [sum(widths[:p]) for p in range(len(bs))]

    def body(a_ref, *refs):
        b_refs, o_ref = refs[:-1], refs[-1]
        for b_ref, off, wd in zip(b_refs, offs, widths):
            o_ref[:, off:off + wd] = _tn(a_ref[...], b_ref[...]).astype(BF16)

    return pl.pallas_call(
        body, name=name, grid=(m // tm,),
        in_specs=[pl.BlockSpec((t, tm), lambda i: (0, i))] + [pl.BlockSpec((t, wd), lambda i: (0, 0)) for wd in widths],
        out_specs=pl.BlockSpec((tm, sum(widths)), lambda i: (i, 0)),
        out_shape=SDS((m, sum(widths)), BF16),
        compiler_params=_params("arbitrary"),
    )(a, *bs)


def _mlp_dnorm(dup, w_up, x, g, dres, tm, after=None):
    t, f = dup.shape
    d = x.shape[1]
    nk, _, tk = w_up.shape

    def body(a_ref, w_ref, x_ref, g_ref, dres_ref, dx_ref, dg_ref, acc):
        i, kk = pl.program_id(0), pl.program_id(1)

        @pl.when((i == 0) & (kk == 0))
        def _():
            dg_ref[...] = jnp.zeros_like(dg_ref)

        @pl.when(kk == 0)
        def _():
            acc[...] = jnp.zeros_like(acc)

        acc[...] += _nt(a_ref[...], w_ref[...])

        @pl.when(kk == nk - 1)
        def _():
            dx, dg = _rms_bwd(acc[...], x_ref[...], g_ref[...])
            dx_ref[...] = dres_ref[...] + dx
            dg_ref[0:1, :] += dg

    row = pl.BlockSpec((tm, d), lambda i, kk: (i, 0))
    in_specs = [pl.BlockSpec((tm, tk), lambda i, kk: (i, kk)), pl.BlockSpec((None, d, tk), lambda i, kk: (kk, 0, 0)),
                row, pl.BlockSpec((1, d), lambda i, kk: (0, 0)), row]
    body, in_specs, args = _ordered_behind(body, 5, in_specs, [dup, w_up, x, g, dres], after)
    return pl.pallas_call(
        body, name="mlp_dnorm", grid=(t // tm, nk),
        in_specs=in_specs,
        out_specs=[row, pl.BlockSpec((8, d), lambda i, kk: (0, 0))],
        out_shape=[SDS((t, d), F32), SDS((8, d), F32)],
        scratch_shapes=[pltpu.VMEM((tm, d), F32)],
        compiler_params=_params("arbitrary", "arbitrary"),
    )(*args)


def _mix_dnorm(dx, w_out, ys, g, tm, after=None):
    t, d = dx.shape
    k = w_out.shape[0]
    widths = [y.shape[1] for y in ys]

    def body(dx_ref, w_ref, *refs):
        y_refs = refs[:len(ys)]
        g_ref, dxb_ref = refs[len(ys)], refs[len(ys) + 1]
        dy_refs = refs[len(ys) + 2:2 * len(ys) + 2]
        dg_ref = refs[-1]

        @pl.when(pl.program_id(0) == 0)
        def _():
            dg_ref[...] = jnp.zeros_like(dg_ref)

        dxb = dx_ref[...].astype(BF16)
        dxb_ref[...] = dxb
        dmix = _nt(dxb, w_ref[...])
        c = 0
        for y_ref, dy_ref, wd in zip(y_refs, dy_refs, widths):
            dy, dg = _rms_bwd(dmix[:, c:c + wd], y_ref[...], g_ref[:, c:c + wd])
            dy_ref[...] = dy
            dg_ref[0:1, c:c + wd] += dg
            c += wd

    row = pl.BlockSpec((tm, d), lambda i: (i, 0))
    yspecs = [pl.BlockSpec((tm, wd), lambda i: (i, 0)) for wd in widths]
    in_specs = [row, pl.BlockSpec((k, d), lambda i: (0, 0))] + yspecs + [pl.BlockSpec((1, k), lambda i: (0, 0))]
    body, in_specs, args = _ordered_behind(body, len(in_specs), in_specs, [dx, w_out, *ys, g], after)
    return pl.pallas_call(
        body, name="mix_dnorm", grid=(t // tm,),
        in_specs=in_specs,
        out_specs=[row] + yspecs + [pl.BlockSpec((8, k), lambda i: (0, 0))],
        out_shape=[SDS((t, d), BF16)] + [SDS((t, wd), F32) for wd in widths] + [SDS((8, k), F32)],
        compiler_params=_params("arbitrary"),
    )(*args)


def _in_dnorm(dps, w_in, x, g, dres, tm):
    t, d = x.shape
    widths = [p.shape[1] for p in dps]
    offs = [sum(widths[:p]) for p in range(len(dps))]
    n_in = w_in.shape[1]

    def body(*refs):
        p_refs = refs[:len(dps)]
        w_ref, x_ref, g_ref, dres_ref, dx_ref, dg_ref = refs[len(dps):]

        @pl.when(pl.program_id(0) == 0)
        def _():
            dg_ref[...] = jnp.zeros_like(dg_ref)

        acc = None
        for p_ref, off, wd in zip(p_refs, offs, widths):
            term = _nt(p_ref[...], w_ref[:, off:off + wd])
            acc = term if acc is None else acc + term
        dx, dg = _rms_bwd(acc, x_ref[...], g_ref[...])
        dx_ref[...] = dres_ref[...] + dx
        dg_ref[0:1, :] += dg

    row = pl.BlockSpec((tm, d), lambda i: (i, 0))
    return pl.pallas_call(
        body, name="in_dnorm", grid=(t // tm,),
        in_specs=[pl.BlockSpec((tm, wd), lambda i: (i, 0)) for wd in widths] + [
            pl.BlockSpec((d, n_in), lambda i: (0, 0)), row, pl.BlockSpec((1, d), lambda i: (0, 0)), row],
        out_specs=[row, pl.BlockSpec((8, d), lambda i: (0, 0))],
        out_shape=[SDS((t, d), F32), SDS((8, d), F32)],
        compiler_params=_params("arbitrary"),
    )(*dps, w_in, x, g, dres)


def _tile(n, want):
    return min(n, want)


def _layer_fwd(x, p, fetch, after):
    t, d = x.shape
    aw, kv, cw, sw = d // 2, d // 8, d // 4, d // 4
    tm = _tile(t, 512)
    w_in = fetch("w_in", after)
    h1, proj = _rms_matmul(x, p["ln1_g"], w_in, tm, 512 if w_in.shape[1] % 512 == 0 else 256, False, "in_proj")
    y_attn = _attn_fwd(proj, p["qg"], p["kg"], p["sinks"], aw, kv)
    y_conv = _conv_fwd(proj, p["conv_w"], p["conv_b"], p["conv_ln_g"], p["conv_ln_b"], cw)
    y_sgu = _sgu_fwd(proj, p["sgu_ln_g"], p["sgu_ln_b"], p["sgu_w"], p["sgu_bias"], sw)
    ys = [y_attn, y_conv, y_sgu]
    w_out = fetch("w_out", y_sgu)
    mix, x1 = _group_rms_matmul(ys, p["out_norm_g"], w_out, x, tm, _tile(d, 1024))
    w_up = fetch("w_up", x1)
    h2, act, r = _rms_matmul(x1, p["ln2_g"], w_up, tm, w_up.shape[2], True, "mlp_up")
    w_down = fetch("w_down", act)
    x2 = _matmul_res(act, w_down, x1, tm, _tile(d, 1024), 1024)
    saved = dict(x=x, h1=h1, proj=proj, ys=ys, mix=mix, x1=x1, h2=h2, act=act, r=r,
                 w_in=w_in, w_out=w_out, w_up=w_up, w_down=w_down)
    return x2, saved


def _layer_bwd(dx2, p, s, reduce, after):
    t, d = dx2.shape
    aw, kv, cw, sw = d // 2, d // 8, d // 4, d // 4
    tm = _tile(t, 512)
    dx2b, dup = _mlp_dact(dx2, s["w_down"], s["r"], tm, 1024, after)
    tok = reduce("w_down", _grad_w(s["act"], [dx2b], 512, _tile(d, 2048), "grad_w_down"))
    dx1, d_ln2 = _mlp_dnorm(dup, s["w_up"], s["x1"], p["ln2_g"], dx2, tm, tok)
    tok = reduce("w_up", _grad_w(s["h2"], [dup], 512, s["w_up"].shape[2], "grad_w_up", col_blocks=True))
    dx1b, dya, dyc, dys, d_onorm = _mix_dnorm(dx1, s["w_out"], s["ys"], p["out_norm_g"], _tile(t, 256), tok)
    tok = reduce("w_out", _grad_w(s["mix"], [dx1b], 512, _tile(d, 2048), "grad_w_out"))
    dqkv, d_attn = _attn_bwd(s["proj"], dya, p["qg"], p["kg"], p["sinks"], aw, kv, tok)
    dconv, d_cw, d_cvec = _conv_bwd(s["proj"], dyc, p["conv_w"], p["conv_b"], p["conv_ln_g"], p["conv_ln_b"], cw)
    dsgu, d_sw, d_sb, d_svec = _sgu_bwd(s["proj"], dys, p["sgu_ln_g"], p["sgu_ln_b"], p["sgu_w"], p["sgu_bias"], sw)
    dps = [dqkv, dconv, dsgu]
    dx, d_ln1 = _in_dnorm(dps, s["w_in"], s["x"], p["ln1_g"], dx1, _tile(t, 256))
    tok = reduce("w_in", _grad_w(s["h1"], dps, 512, 512, "grad_w_in"))
    heads = sw // HEAD
    small = dict(
        ln1_g=d_ln1[0], q_norm_g=d_attn[0, :HEAD], k_norm_g=d_attn[1, :HEAD], sinks=d_attn[2, :aw // HEAD],
        conv_w=d_cw[:CONV_K], conv_b=d_cvec[0], conv_ln_g=d_cvec[1], conv_ln_b=d_cvec[2],
        sgu_ln_g=d_svec[0], sgu_ln_b=d_svec[1], sgu_w=d_sw, sgu_b=d_sb[:, :heads].T,
        out_norm_g=d_onorm[0], ln2_g=d_ln2[0])
    return dx, small, tok


def _layer_params(l, small):
    row = lambda v: v[l][None, :]
    two = lambda v: jnp.tile(v[l], 2)[None, :]
    return dict(
        ln1_g=row(small["ln1_g"]), ln2_g=row(small["ln2_g"]), out_norm_g=row(small["out_norm_g"]),
        qg=two(small["q_norm_g"]), kg=two(small["k_norm_g"]), sinks=small["sinks"][l],
        conv_w=jnp.pad(small["conv_w"][l], ((0, HALO - CONV_K), (0, 0))),
        conv_b=row(small["conv_b"]), conv_ln_g=row(small["conv_ln_g"]), conv_ln_b=row(small["conv_ln_b"]),
        sgu_ln_g=row(small["sgu_ln_g"]), sgu_ln_b=row(small["sgu_ln_b"]), sgu_w=small["sgu_w"][l],
        sgu_bias=jnp.repeat(small["sgu_b"][l].T, HEAD, axis=1),
    )


def _local_step(x, target, small, depth, fetch, reduce, after):
    params = [_layer_params(l, small) for l in range(depth)]
    saved = []
    h = x
    for l in range(depth):
        h, s = _layer_fwd(h, params[l], functools.partial(fetch, l), after)
        after = h
        saved.append(s)
    dy, lsum = _loss_grad(h, target, _tile(x.shape[0], 512))
    gs = [None] * depth
    tok = None
    for l in reversed(range(depth)):
        dy, gs[l], tok = _layer_bwd(dy, params[l], saved[l], functools.partial(reduce, l), tok)
    return lsum[0, 0], dy, gs, tok


BIG = ("w_in", "w_out", "w_up", "w_down")
HBM = pl.BlockSpec(memory_space=pltpu.HBM)
SEMS = pl.BlockSpec(memory_space=pltpu.SEMAPHORE)
EFFECT = pltpu.SideEffectType.DATAFLOW_SIDE_EFFECTING


def _mesh_pos():
    return lax.axis_index("x"), lax.axis_index("y"), lax.axis_index("c")


def _other_chips(x, y):
    return [(1 - x, y), (x, 1 - y), (1 - x, 1 - y)]


def _copies(plan, refs, sends, recvs):
    x, y, c = _mesh_pos()
    return [pltpu.make_async_remote_copy(src_ref=src, dst_ref=dst, send_sem=sends.at[k], recv_sem=recvs.at[k],
                                         device_id=dev, device_id_type=MESH)
            for k, (src, dst, dev) in enumerate(plan(x, y, c, refs))]


def _gather_plan(x, y, c, refs):
    shard, land = refs
    mine = land.at[4 * x + 2 * y + c]
    return [(shard, mine, (x, y, 1 - c))] + [(shard, mine, (*chip, c)) for chip in _other_chips(x, y)]


def _chips_plan(x, y, c, refs):
    sums, land = refs
    return [(sums.at[2 * chip[0] + chip[1]], land.at[k], (*chip, c)) for k, chip in enumerate(_other_chips(x, y))]


def _start_exchanges(name, groups):
    flat = [a for arrays, _, _ in groups for a in arrays]
    n_arr, n_g = len(flat), len(groups)

    def body(*refs):
        ins, sems, token = refs[:n_arr], refs[n_arr:n_arr + 2 * n_g], refs[-1]
        off = 0
        for gi, (arrays, plan, _) in enumerate(groups):
            for cp in _copies(plan, ins[off:off + len(arrays)], sems[2 * gi], sems[2 * gi + 1]):
                cp.start()
            off += len(arrays)
        token[...] = jnp.zeros_like(token)

    res = pl.pallas_call(
        body, name=name,
        out_shape=[pltpu.SemaphoreType.DMA((n,)) for _, _, n in groups for _ in (0, 1)]
        + [pltpu.HBM(a.shape, a.dtype) for a in flat] + [SDS((8, BLK), F32)],
        in_specs=[HBM] * n_arr,
        out_specs=[SEMS] * (2 * n_g) + [HBM] * n_arr + [pl.BlockSpec(memory_space=pltpu.VMEM)],
        input_output_aliases={i: 2 * n_g + i for i in range(n_arr)},
        compiler_params=pltpu.CompilerParams(has_side_effects=EFFECT),
    )(*[pltpu.with_memory_space_constraint(a, pltpu.HBM) for a in flat])
    sems, thru, token = res[:2 * n_g], res[2 * n_g:2 * n_g + n_arr], res[-1]
    out, off = [], 0
    for gi, (arrays, _, _) in enumerate(groups):
        out.append((list(thru[off:off + len(arrays)]), sems[2 * gi], sems[2 * gi + 1]))
        off += len(arrays)
    return out, token


def _wait_exchange(name, arrays, sends, recvs, plan, after):
    n = len(arrays)

    def body(*refs):
        for cp in _copies(plan, refs[:n], refs[n], refs[n + 1]):
            cp.wait_send()
            cp.wait_recv()

    return pl.pallas_call(
        body, name=name,
        out_shape=[pltpu.HBM(a.shape, a.dtype) for a in arrays],
        in_specs=[HBM] * n + [SEMS, SEMS, ANY],
        out_specs=[HBM] * n,
        input_output_aliases={i: i for i in range(n)},
        compiler_params=pltpu.CompilerParams(has_side_effects=EFFECT),
    )(*arrays, sends, recvs, after)


def _gather_finish(shard, land, name):
    def body(shard_ref, land_ref, out_ref, send_sems, recv_sems, local_sem):
        del land_ref
        x, y, c = _mesh_pos()
        local = pltpu.make_async_copy(shard_ref, out_ref.at[4 * x + 2 * y + c], local_sem)
        local.start()
        cps = []
        for k, chip in enumerate(_other_chips(x, y)):
            block = out_ref.at[4 * chip[0] + 2 * chip[1] + c]
            cps.append(pltpu.make_async_remote_copy(
                src_ref=block, dst_ref=block, send_sem=send_sems.at[k], recv_sem=recv_sems.at[k],
                device_id=(x, y, 1 - c), device_id_type=MESH))
        for cp in cps:
            cp.start()
        for cp in cps:
            cp.wait()
        local.wait()

    return pl.pallas_call(
        body, name=name,
        in_specs=[ANY, ANY], out_specs=ANY,
        out_shape=SDS(land.shape, land.dtype),
        input_output_aliases={1: 0},
        scratch_shapes=[pltpu.SemaphoreType.DMA((3,)), pltpu.SemaphoreType.DMA((3,)), pltpu.SemaphoreType.DMA],
    )(shard, land)


def _all_gather(shards, name):
    n = len(shards)

    def body(*refs):
        ins, outs = refs[:n], refs[n:2 * n]
        send_sems, recv_sems, local_sems = refs[2 * n:]
        x, y, c = _mesh_pos()
        me, sibling = (x, y, c), (x, y, 1 - c)
        chips = _other_chips(x, y)

        def copy(a, k, block, to, src=None):
            dst = outs[a].at[4 * block[0] + 2 * block[1] + block[2]]
            return pltpu.make_async_remote_copy(
                src_ref=dst if src is None else src, dst_ref=dst,
                send_sem=send_sems.at[a, k], recv_sem=recv_sems.at[a, k], device_id=to, device_id_type=MESH)

        mine = [pltpu.make_async_copy(ins[a], outs[a].at[4 * x + 2 * y + c], local_sems.at[a]) for a in range(n)]
        for cp in mine:
            cp.start()
        first = []
        for a in range(n):
            first.append(copy(a, 0, me, sibling, src=ins[a]))
            first += [copy(a, 1 + j, me, (*chip, c), src=ins[a]) for j, chip in enumerate(chips)]
        for cp in first:
            cp.start()
        passed = []
        for j, chip in enumerate(chips):
            for a in range(n):
                copy(a, 1 + j, (*chip, c), me).wait_recv()
                fwd = copy(a, 4 + j, (*chip, c), sibling)
                fwd.start()
                passed.append(fwd)
        for a in range(n):
            copy(a, 0, sibling, me).wait_recv()
            for j, chip in enumerate(chips):
                copy(a, 4 + j, (*chip, 1 - c), me).wait_recv()
        for cp in first + passed:
            cp.wait_send()
        for cp in mine:
            cp.wait()

    outs = pl.pallas_call(
        body, name=name,
        in_specs=[ANY] * n, out_specs=[ANY] * n,
        out_shape=[SDS((N_DEV,) + s.shape, s.dtype) for s in shards],
        scratch_shapes=[pltpu.SemaphoreType.DMA((n, 7)), pltpu.SemaphoreType.DMA((n, 7)), pltpu.SemaphoreType.DMA((n,))],
    )(*shards)
    return list(outs)


def _exchange_pair(grads, name):
    n = len(grads)

    def body(*refs):
        ins, outs = refs[:n], refs[n:2 * n]
        send_sems, recv_sems = refs[2 * n:]
        x, y, c = _mesh_pos()
        cps = []
        for a in range(n):
            for q in range(4):
                cps.append(pltpu.make_async_remote_copy(
                    src_ref=ins[a].at[2 * q + (1 - c)], dst_ref=outs[a].at[q],
                    send_sem=send_sems.at[a, q], recv_sem=recv_sems.at[a, q],
                    device_id=(x, y, 1 - c), device_id_type=MESH))
        for cp in cps:
            cp.start()
        for cp in cps:
            cp.wait()

    outs = pl.pallas_call(
        body, name=name,
        in_specs=[ANY] * n, out_specs=[ANY] * n,
        out_shape=[SDS((4,) + g.shape[1:], g.dtype) for g in grads],
        scratch_shapes=[pltpu.SemaphoreType.DMA((n, 4)), pltpu.SemaphoreType.DMA((n, 4))],
    )(*grads)
    return list(outs)


def _exchange_chips(sums, name):
    n = len(sums)

    def body(*refs):
        ins, outs = refs[:n], refs[n:2 * n]
        send_sems, recv_sems = refs[2 * n:]
        x, y, c = _mesh_pos()
        cps = []
        for a in range(n):
            for k, chip in enumerate(_other_chips(x, y)):
                cps.append(pltpu.make_async_remote_copy(
                    src_ref=ins[a].at[2 * chip[0] + chip[1]], dst_ref=outs[a].at[k],
                    send_sem=send_sems.at[a, k], recv_sem=recv_sems.at[a, k],
                    device_id=(*chip, c), device_id_type=MESH))
        for cp in cps:
            cp.start()
        for cp in cps:
            cp.wait()

    outs = pl.pallas_call(
        body, name=name,
        in_specs=[ANY] * n, out_specs=[ANY] * n,
        out_shape=[SDS((3,) + s.shape[1:], s.dtype) for s in sums],
        scratch_shapes=[pltpu.SemaphoreType.DMA((n, 3)), pltpu.SemaphoreType.DMA((n, 3))],
    )(*sums)
    return list(outs)


def _pair_add(own, got, c, name):
    _, r, cols = own.shape
    tr = _tile(r, 512)

    def body(c_ref, own_ref, got_ref, o_ref):
        o_ref[...] = (own_ref[...].astype(F32) + got_ref[...].astype(F32)).astype(BF16)

    return pl.pallas_call(
        body, name=name,
        grid_spec=pltpu.PrefetchScalarGridSpec(
            num_scalar_prefetch=1, grid=(4, r // tr),
            in_specs=[pl.BlockSpec((None, tr, cols), lambda q, i, c_ref: (2 * q + c_ref[0], i, 0)),
                      pl.BlockSpec((None, tr, cols), lambda q, i, c_ref: (q, i, 0))],
            out_specs=pl.BlockSpec((None, tr, cols), lambda q, i, c_ref: (q, i, 0))),
        out_shape=SDS((4, r, cols), BF16),
        compiler_params=_params("arbitrary", "arbitrary"),
    )(c, own, got)


def _adamw(w, g, m, v):
    m = ADAM_B1 * m + (1.0 - ADAM_B1) * g
    v = ADAM_B2 * v + (1.0 - ADAM_B2) * (g * g)
    m_hat = m / (1.0 - ADAM_B1 ** ADAM_STEP)
    v_hat = v / (1.0 - ADAM_B2 ** ADAM_STEP)
    delta = -ADAM_LR * (m_hat / (jnp.sqrt(v_hat) + ADAM_EPS) + ADAM_WD * w)
    return delta, m, v


def _adamw_big(sums, gots, chip, w, m, v, name):
    depth, r, cols = w.shape
    tr = _tile(r, 256)
    nt = r // tr

    def body(chip_ref, *refs):
        sum_refs, got_refs = refs[:depth], refs[depth:2 * depth]
        w_ref, m_ref, v_ref, g_out, d_out, m_out, v_out = refs[2 * depth:]
        l = pl.program_id(0)
        for layer in range(depth):
            @pl.when(l == layer)
            def _(layer=layer):
                got = got_refs[layer]
                g = sum_refs[layer][...].astype(F32) + got[0].astype(F32) + got[1].astype(F32) + got[2].astype(F32)
                delta, mm, vv = _adamw(w_ref[...], g, m_ref[...], v_ref[...])
                g_out[...] = g
                d_out[...] = delta
                m_out[...] = mm
                v_out[...] = vv

    def hold(layer):
        return lambda l, i: jnp.where(l == layer, i, jnp.where(l < layer, 0, nt - 1))

    shard = pl.BlockSpec((None, tr, cols), lambda l, i, chip_ref: (l, i, 0))
    return pl.pallas_call(
        body, name=name,
        grid_spec=pltpu.PrefetchScalarGridSpec(
            num_scalar_prefetch=1, grid=(depth, nt),
            in_specs=[pl.BlockSpec((None, tr, cols), lambda l, i, chip_ref, layer=layer: (chip_ref[0], hold(layer)(l, i), 0))
                      for layer in range(depth)]
            + [pl.BlockSpec((3, tr, cols), lambda l, i, chip_ref, layer=layer: (0, hold(layer)(l, i), 0)) for layer in range(depth)]
            + [shard] * 3,
            out_specs=[shard] * 4),
        out_shape=[SDS(w.shape, F32)] * 4,
        compiler_params=_params("arbitrary", "arbitrary"),
    )(chip, *sums, *gots, w, m, v)


def _sum_devices(stacked):
    _, r, _ = stacked.shape

    def body(s_ref, o_ref):
        acc = s_ref[0]
        for k in range(1, N_DEV):
            acc = acc + s_ref[k]
        o_ref[...] = acc

    return pl.pallas_call(
        body, name="sum_devices", grid=(1,),
        in_specs=[pl.BlockSpec((N_DEV, r, BLK), lambda i: (0, 0, 0))],
        out_specs=pl.BlockSpec((r, BLK), lambda i: (0, 0)),
        out_shape=SDS((r, BLK), F32),
        compiler_params=_params("arbitrary"),
    )(stacked)


def _adamw_small(g, w, m, v):
    def body(g_ref, w_ref, m_ref, v_ref, d_out, m_out, v_out):
        delta, mm, vv = _adamw(w_ref[...], g_ref[...], m_ref[...], v_ref[...])
        d_out[...] = delta
        m_out[...] = mm
        v_out[...] = vv

    spec = pl.BlockSpec(g.shape, lambda i: (0, 0))
    return pl.pallas_call(
        body, name="adamw_small", grid=(1,),
        in_specs=[spec] * 4, out_specs=[spec] * 3,
        out_shape=[SDS(g.shape, F32)] * 3,
        compiler_params=_params("arbitrary"),
    )(g, w, m, v)


def _pack(arrays):
    flat = jnp.concatenate([a.reshape(-1) for a in arrays])
    rows = -(-flat.shape[0] // (8 * BLK)) * 8
    return jnp.pad(flat, (0, rows * BLK - flat.shape[0])).reshape(rows, BLK)


def _unpack(packed, like):
    flat = packed.reshape(-1)
    out, off = [], 0
    for a in like:
        out.append(flat[off:off + a.size].reshape(a.shape))
        off += a.size
    return out


SMALL = ("ln1_g", "q_norm_g", "k_norm_g", "sinks", "conv_b", "conv_ln_g", "conv_ln_b", "sgu_ln_g", "sgu_ln_b",
         "sgu_w", "sgu_b", "out_norm_g", "ln2_g")
ORDER = ("ln1_g", "w_in", "q_norm_g", "k_norm_g", "sinks", "conv_w", "conv_b", "conv_ln_g", "conv_ln_b", "sgu_ln_g",
         "sgu_ln_b", "sgu_w", "sgu_b", "out_norm_g", "w_out", "ln2_g", "w_up", "w_down")


def _step(x, target, w, m, v):
    depth = w["ln1_g"].shape[0]
    xpos, ypos, cpos = _mesh_pos()
    me = 4 * xpos + 2 * ypos + cpos
    c_arr = jnp.reshape(cpos, (1,)).astype(jnp.int32)
    chip_arr = jnp.reshape(2 * xpos + ypos, (1,)).astype(jnp.int32)

    d = x.shape[1]
    order = [(l, n) for l in range(depth) for n in BIG]
    started, gather_token = _start_exchanges("gather_start", [
        ([w[n][l].astype(BF16), lax.empty((N_DEV,) + w[n][l].shape, BF16)], _gather_plan, 4) for l, n in order])
    pending = dict(zip(order, started))

    def fetch(l, n, after):
        arrays, sends, recvs = pending.pop((l, n))
        shard, land = _wait_exchange(f"gather_wait_{l}_{n}", arrays, sends, recvs, _gather_plan,
                                     gather_token if after is None else after)
        full = _gather_finish(shard, land, "gather_finish_" + n)
        if n == "w_in":
            return jnp.transpose(full, (1, 0, 2)).reshape(d, -1)
        return full if n == "w_up" else full.reshape(-1, d)

    cw = w["conv_w"]
    cw_all = _all_gather([_pack([cw])], "gather_conv_w")[0]
    cw_full = jnp.concatenate([_unpack(cw_all[k], [cw])[0] for k in range(N_DEV)], axis=-1)
    small = {n: w[n] for n in SMALL}
    small["conv_w"] = cw_full

    inflight = []

    def reduce(l, n, g):
        if n == "w_in":
            blocks = jnp.transpose(g.reshape(d, N_DEV, -1), (1, 0, 2))
        else:
            blocks = g if n == "w_up" else g.reshape(N_DEV, -1, d)
        from_sibling = _exchange_pair([blocks], "reduce_pair_" + n)[0]
        chip_sums = _pair_add(blocks, from_sibling, c_arr, "pair_add_" + n)
        (going,), token = _start_exchanges(f"reduce_start_{l}_{n}", [
            ([chip_sums, lax.empty((3,) + chip_sums.shape[1:], BF16)], _chips_plan, 3)])
        inflight.append((l, n, going))
        return token

    loss, grad_x, gs, after = _local_step(x, target, small, depth, fetch, reduce, None)

    sums, gots = {n: [None] * depth for n in BIG}, {n: [None] * depth for n in BIG}
    for l, n, (arrays, sends, recvs) in inflight:
        sums[n][l], gots[n][l] = _wait_exchange(f"reduce_wait_{l}_{n}", arrays, sends, recvs, _chips_plan, after)
        after = gots[n][l]

    out = {}
    for n in BIG:
        shape = w[n].shape
        as3d = lambda a: a.reshape(depth, -1, a.shape[-1])
        res = _adamw_big(sums[n], gots[n], chip_arr, as3d(w[n]), as3d(m[n]), as3d(v[n]), "adamw_" + n)
        out[n] = [r.reshape(shape) for r in res]

    names = SMALL + ("conv_w",)
    g_small = [jnp.stack([gs[l][n] for l in range(depth)]) for n in names]
    g_sum = _unpack(_sum_devices(_all_gather([_pack(g_small)], "gather_small_grads")[0]), g_small)
    cshard = cw.shape[-1]
    g_sum[-1] = lax.dynamic_slice_in_dim(g_sum[-1], me * cshard, cshard, axis=2)
    like = [w[n] for n in names]
    res = _adamw_small(_pack(g_sum), _pack(like), _pack([m[n] for n in names]), _pack([v[n] for n in names]))
    res = [_unpack(r, like) for r in res]
    for i, n in enumerate(names):
        out[n] = [g_sum[i], res[0][i], res[1][i], res[2][i]]

    loss = lax.psum(loss, ("x", "y", "c"))
    return (loss, grad_x[None]) + tuple(out[n][k] for k in range(4) for n in ORDER)


def kernel(x, ln1_g, w_in, q_norm_g, k_norm_g, sinks, conv_w, conv_b, conv_ln_g, conv_ln_b, sgu_ln_g, sgu_ln_b, sgu_w, sgu_b, out_norm_g, w_out, ln2_g, w_up, w_down, loss_target, m_ln1_g, m_w_in, m_q_norm_g, m_k_norm_g, m_sinks, m_conv_w, m_conv_b, m_conv_ln_g, m_conv_ln_b, m_sgu_ln_g, m_sgu_ln_b, m_sgu_w, m_sgu_b, m_out_norm_g, m_w_out, m_ln2_g, m_w_up, m_w_down, v_ln1_g, v_w_in, v_q_norm_g, v_k_norm_g, v_sinks, v_conv_w, v_conv_b, v_conv_ln_g, v_conv_ln_b, v_sgu_ln_g, v_sgu_ln_b, v_sgu_w, v_sgu_b, v_out_norm_g, v_w_out, v_ln2_g, v_w_up, v_w_down):
    w = dict(zip(ORDER, (ln1_g, w_in, q_norm_g, k_norm_g, sinks, conv_w, conv_b, conv_ln_g, conv_ln_b, sgu_ln_g, sgu_ln_b,
                         sgu_w, sgu_b, out_norm_g, w_out, ln2_g, w_up, w_down)))
    m = dict(zip(ORDER, (m_ln1_g, m_w_in, m_q_norm_g, m_k_norm_g, m_sinks, m_conv_w, m_conv_b, m_conv_ln_g, m_conv_ln_b,
                         m_sgu_ln_g, m_sgu_ln_b, m_sgu_w, m_sgu_b, m_out_norm_g, m_w_out, m_ln2_g, m_w_up, m_w_down)))
    v = dict(zip(ORDER, (v_ln1_g, v_w_in, v_q_norm_g, v_k_norm_g, v_sinks, v_conv_w, v_conv_b, v_conv_ln_g, v_conv_ln_b,
                         v_sgu_ln_g, v_sgu_ln_b, v_sgu_w, v_sgu_b, v_out_norm_g, v_w_out, v_ln2_g, v_w_up, v_w_down)))
    return _step(x[0], loss_target[0], w, m, v)
```

```python
import functools

import jax
import jax.numpy as jnp
from jax import lax
from jax.experimental import pallas as pl
from jax.experimental.pallas import tpu as pltpu

F32 = jnp.float32
BF16 = jnp.bfloat16
SDS = jax.ShapeDtypeStruct

EPS = 1e-6
NEG_INF = -1e30
HEAD = 64
BLK = 128
CONV_K = 31
HALO = 32
N_DEV = 8

ADAM_LR = 0.001
ADAM_B1 = 0.9
ADAM_B2 = 0.999
ADAM_EPS = 1e-08
ADAM_WD = 0.01
ADAM_STEP = 10

VMEM_LIMIT = 56 * 1024 * 1024

MESH = pl.DeviceIdType.MESH


def _params(*sem):
    return pltpu.CompilerParams(dimension_semantics=sem, vmem_limit_bytes=VMEM_LIMIT)


def _nt(a, b):
    return lax.dot_general(a, b, (((1,), (1,)), ((), ())), preferred_element_type=F32)


def _tn(a, b):
    return lax.dot_general(a, b, (((0,), (0,)), ((), ())), preferred_element_type=F32)


def _nn(a, b):
    return jnp.dot(a, b, preferred_element_type=F32)


def _sigmoid(x):
    return 1.0 / (1.0 + jnp.exp(-x))


ANY = pl.BlockSpec(memory_space=pl.ANY)


def _ordered_behind(body, n_in, in_specs, args, after):
    if after is None:
        return body, in_specs, args
    return (lambda *refs: body(*refs[:n_in], *refs[n_in + 1:])), list(in_specs) + [ANY], list(args) + [after]


def _seg_ones():
    r = lax.broadcasted_iota(jnp.int32, (BLK, BLK), 0) // HEAD
    c = lax.broadcasted_iota(jnp.int32, (BLK, BLK), 1) // HEAD
    return (r == c).astype(BF16)


def _segsum(x, ones):
    hi = x.astype(BF16)
    r1 = x - hi.astype(F32)
    mid = r1.astype(BF16)
    lo = (r1 - mid.astype(F32)).astype(BF16)
    return _nn(hi, ones) + _nn(mid, ones) + _nn(lo, ones)


def _head_rms(x, gain, ones):
    rstd = lax.rsqrt(_segsum(x * x, ones) * (1.0 / HEAD) + EPS)
    xhat = x * rstd
    return xhat * gain, xhat, rstd


def _expand(x, odd, lo):
    if odd:
        xl = pltpu.roll(jnp.where(lo, 0.0, x), HEAD, axis=1)
    else:
        xl = jnp.where(lo, x, 0.0)
    xh = pltpu.roll(xl, HEAD, axis=1)
    return jnp.concatenate([xl, xh], axis=0).astype(BF16)


def _fold(g2, odd, lo):
    r = g2.shape[0] // 2
    s = jnp.where(lo, g2[:r], 0.0) + pltpu.roll(jnp.where(lo, 0.0, g2[r:]), HEAD, axis=1)
    if odd:
        s = pltpu.roll(s, HEAD, axis=1)
    return s


def _attn_mask(n):
    qi = lax.broadcasted_iota(jnp.int32, (BLK, 2 * BLK), 0)
    sj = lax.broadcasted_iota(jnp.int32, (BLK, 2 * BLK), 1)
    rel = qi + BLK - sj
    return (rel >= 0) & (rel < BLK) & ((sj >= BLK) | (n > 0))


def _attn_specs(t, aw, kv):
    prev = lambda n: jnp.maximum(n - 1, 0)
    kb, vb = aw // kv, aw // kv + 1
    return [
        pl.BlockSpec(memory_space=pltpu.SMEM),
        pl.BlockSpec((BLK, aw), lambda n: (n, 0)),
        pl.BlockSpec((BLK, kv), lambda n: (prev(n), kb)),
        pl.BlockSpec((BLK, kv), lambda n: (n, kb)),
        pl.BlockSpec((BLK, kv), lambda n: (prev(n), vb)),
        pl.BlockSpec((BLK, kv), lambda n: (n, vb)),
        pl.BlockSpec((1, BLK), lambda n: (0, 0)),
        pl.BlockSpec((1, BLK), lambda n: (0, 0)),
    ]


def _softmax_pair(s2, valid, sink0, sink1):
    out, psink = [], []
    for half, sink in ((0, sink0), (1, sink1)):
        s = jnp.where(valid, s2[:, 2 * BLK * half:2 * BLK * (half + 1)], NEG_INF)
        m = jnp.maximum(jnp.max(s, axis=-1, keepdims=True), sink)
        p = jnp.exp(s - m)
        es = jnp.exp(sink - m)
        den = jnp.sum(p, axis=-1, keepdims=True) + es
        out.append(p / den)
        psink.append(es / den)
    return jnp.concatenate(out, axis=1), psink


def _attn_fwd(proj, qg, kg, sinks, aw, kv):
    t = proj.shape[0]
    n_pairs, n_kvblk = aw // BLK, kv // BLK

    def body(sink_ref, q_ref, kp_ref, kc_ref, vp_ref, vc_ref, qg_ref, kg_ref, o_ref):
        n = pl.program_id(0)
        ones = _seg_ones()
        lo = lax.broadcasted_iota(jnp.int32, (1, BLK), 1) < HEAD
        valid = _attn_mask(n)
        kraw = jnp.concatenate([kp_ref[...], kc_ref[...]], axis=0)
        vraw = jnp.concatenate([vp_ref[...], vc_ref[...]], axis=0)
        for b in range(n_kvblk):
            kn = _head_rms(kraw[:, BLK * b:BLK * (b + 1)], kg_ref[...], ones)[0]
            vb = vraw[:, BLK * b:BLK * (b + 1)]
            for odd in (0, 1):
                j = 2 * b + odd
                k2 = _expand(kn, odd, lo)
                v2 = _expand(vb, odd, lo)
                for p in (2 * j, 2 * j + 1):
                    qn = _head_rms(q_ref[:, BLK * p:BLK * (p + 1)], qg_ref[...], ones)[0]
                    s2 = _nt(qn.astype(BF16), k2) * (HEAD ** -0.5)
                    p2, _ = _softmax_pair(s2, valid, sink_ref[2 * p], sink_ref[2 * p + 1])
                    o_ref[:, BLK * p:BLK * (p + 1)] = _nn(p2.astype(BF16), v2)

    del n_pairs
    return pl.pallas_call(
        body, name="attn_fwd", grid=(t // BLK,),
        in_specs=_attn_specs(t, aw, kv),
        out_specs=pl.BlockSpec((BLK, aw), lambda n: (n, 0)),
        out_shape=SDS((t, aw), F32),
        compiler_params=_params("arbitrary"),
    )(sinks, proj, proj, proj, proj, proj, qg, kg)


def _attn_bwd(proj, dy, qg, kg, sinks, aw, kv, after=None):
    t = proj.shape[0]
    nb = t // BLK
    n_kvblk = kv // BLK
    kb = aw // kv

    def body(sink_ref, q_ref, kp_ref, kc_ref, vp_ref, vc_ref, qg_ref, kg_ref, dy_ref, kall_ref,
             dqkv_ref, dstat_ref, dk_acc, dv_acc, dqg_acc):
        n = pl.program_id(0)
        ones = _seg_ones()
        lane = lax.broadcasted_iota(jnp.int32, (1, BLK), 1)
        lo = lane < HEAD
        valid = _attn_mask(n)

        @pl.when(n == 0)
        def _():
            dk_acc[...] = jnp.zeros_like(dk_acc)
            dv_acc[...] = jnp.zeros_like(dv_acc)
            dqg_acc[...] = jnp.zeros_like(dqg_acc)
            dstat_ref[...] = jnp.zeros_like(dstat_ref)

        kraw = jnp.concatenate([kp_ref[...], kc_ref[...]], axis=0)
        vraw = jnp.concatenate([vp_ref[...], vc_ref[...]], axis=0)
        row = pl.multiple_of(n * BLK, BLK)
        prow = pl.multiple_of(jnp.maximum(n - 1, 0) * BLK, BLK)
        dsink = jnp.zeros((1, BLK), F32)
        for b in range(n_kvblk):
            kn = _head_rms(kraw[:, BLK * b:BLK * (b + 1)], kg_ref[...], ones)[0]
            vb = vraw[:, BLK * b:BLK * (b + 1)]
            dkn = jnp.zeros((2 * BLK, BLK), F32)
            dvb = jnp.zeros((2 * BLK, BLK), F32)
            for odd in (0, 1):
                j = 2 * b + odd
                k2 = _expand(kn, odd, lo)
                v2 = _expand(vb, odd, lo)
                dk2 = jnp.zeros((4 * BLK, BLK), F32)
                dv2 = jnp.zeros((4 * BLK, BLK), F32)
                for p in (2 * j, 2 * j + 1):
                    cols = slice(BLK * p, BLK * (p + 1))
                    qn, qhat, rstd = _head_rms(q_ref[:, cols], qg_ref[...], ones)
                    qb = qn.astype(BF16)
                    s2 = _nt(qb, k2) * (HEAD ** -0.5)
                    p2, psink = _softmax_pair(s2, valid, sink_ref[2 * p], sink_ref[2 * p + 1])
                    dob = dy_ref[:, cols].astype(BF16)
                    dp2 = _nt(dob, v2)
                    ds = []
                    for half in (0, 1):
                        hs = slice(2 * BLK * half, 2 * BLK * (half + 1))
                        ph = p2[:, hs]
                        delta = jnp.sum(ph * dp2[:, hs], axis=-1, keepdims=True)
                        ds.append(ph * (dp2[:, hs] - delta))
                        dsk = -jnp.sum(psink[half] * delta, axis=0, keepdims=True)
                        dsink = dsink + jnp.where(lane == 2 * p + half, dsk, 0.0)
                    ds2 = (jnp.concatenate(ds, axis=1) * (HEAD ** -0.5)).astype(BF16)
                    dqn = _nn(ds2, k2)
                    dk2 = dk2 + _tn(ds2, qb)
                    dv2 = dv2 + _tn(p2.astype(BF16), dob)
                    dqhat = dqn * qg_ref[...]
                    proj_q = _segsum(dqhat * qhat, ones) * (1.0 / HEAD)
                    dqkv_ref[pl.ds(row, BLK), cols] = (rstd * (dqhat - qhat * proj_q)).astype(BF16)
                    dqg_acc[:, cols] += jnp.sum(dqn * qhat, axis=0, keepdims=True)
                dkn = dkn + _fold(dk2, odd, lo)
                dvb = dvb + _fold(dv2, odd, lo)
            kcols = slice(BLK * b, BLK * (b + 1))
            dk_acc[pl.ds(prow, BLK), kcols] += dkn[:BLK]
            dv_acc[pl.ds(prow, BLK), kcols] += dvb[:BLK]
            dk_acc[pl.ds(row, BLK), kcols] += dkn[BLK:]
            dv_acc[pl.ds(row, BLK), kcols] += dvb[BLK:]
        dstat_ref[2:3, :] += dsink

        @pl.when(n == nb - 1)
        def _():
            dqg = dqg_acc[:, 0:BLK]
            for p in range(1, aw // BLK):
                dqg = dqg + dqg_acc[:, BLK * p:BLK * (p + 1)]
            dstat_ref[0:1, :] = dqg + pltpu.roll(dqg, HEAD, axis=1)

            def kblock(i, dkg):
                r = pl.multiple_of(i * BLK, BLK)
                for b in range(n_kvblk):
                    kcols = slice(BLK * b, BLK * (b + 1))
                    _, khat, rstd = _head_rms(kall_ref[pl.ds(r, BLK), kcols], kg_ref[...], ones)
                    dkn = dk_acc[pl.ds(r, BLK), kcols]
                    dkhat = dkn * kg_ref[...]
                    proj_k = _segsum(dkhat * khat, ones) * (1.0 / HEAD)
                    dqkv_ref[pl.ds(r, BLK), aw + BLK * b:aw + BLK * (b + 1)] = (rstd * (dkhat - khat * proj_k)).astype(BF16)
                    dqkv_ref[pl.ds(r, BLK), aw + kv + BLK * b:aw + kv + BLK * (b + 1)] = dv_acc[pl.ds(r, BLK), kcols].astype(BF16)
                    dkg = dkg + jnp.sum(dkn * khat, axis=0, keepdims=True)
                return dkg

            dkg = lax.fori_loop(0, nb, kblock, jnp.zeros((1, BLK), F32))
            dstat_ref[1:2, :] = dkg + pltpu.roll(dkg, HEAD, axis=1)

    in_specs = _attn_specs(t, aw, kv) + [
        pl.BlockSpec((BLK, aw), lambda n: (n, 0)),
        pl.BlockSpec((t, kv), lambda n: (0, kb)),
    ]
    args = [sinks, proj, proj, proj, proj, proj, qg, kg, dy, proj]
    body, in_specs, args = _ordered_behind(body, len(args), in_specs, args, after)
    return pl.pallas_call(
        body, name="attn_bwd", grid=(nb,),
        in_specs=in_specs,
        out_specs=[pl.BlockSpec((t, aw + 2 * kv), lambda n: (0, 0)), pl.BlockSpec((8, BLK), lambda n: (0, 0))],
        out_shape=[SDS((t, aw + 2 * kv), BF16), SDS((8, BLK), F32)],
        scratch_shapes=[pltpu.VMEM((t, kv), F32), pltpu.VMEM((t, kv), F32), pltpu.VMEM((1, aw), F32)],
        compiler_params=_params("arbitrary"),
    )(*args)


def _conv_taps(win, w_ref, shift_of):
    rows = win.shape[0]
    acc = None
    for j in range(CONV_K):
        term = pltpu.roll(win, (rows - shift_of(j)) % rows, axis=0)[:BLK] * w_ref[j:j + 1, :]
        acc = term if acc is None else acc + term
    return acc


def _layer_norm_fwd(z):
    mu = jnp.mean(z, axis=-1, keepdims=True)
    zc = z - mu
    rstd = lax.rsqrt(jnp.mean(zc * zc, axis=-1, keepdims=True) + EPS)
    return zc * rstd, rstd


def _layer_norm_bwd(dy, yhat, rstd, g):
    dyh = dy * g
    return rstd * (dyh - jnp.mean(dyh, axis=-1, keepdims=True) - yhat * jnp.mean(dyh * yhat, axis=-1, keepdims=True))


def _conv_fill_glu(a_ref, g_ref, hpad, nb):
    hpad[0:HALO, :] = jnp.zeros((HALO, hpad.shape[1]), F32)

    def fill(i, c):
        r = pl.multiple_of(i * BLK, BLK)
        hpad[pl.ds(pl.multiple_of(r + HALO, HALO), BLK), :] = a_ref[pl.ds(r, BLK), :] * _sigmoid(g_ref[pl.ds(r, BLK), :])
        return c

    lax.fori_loop(0, nb, fill, 0)


def _conv_specs(t, cw, d_in):
    base = (d_in - 4 * cw) // cw
    vec = pl.BlockSpec((1, cw), lambda i: (0, 0))
    return [
        pl.BlockSpec((t, cw), lambda i: (0, base)),
        pl.BlockSpec((t, cw), lambda i: (0, base + 1)),
        pl.BlockSpec((HALO, cw), lambda i: (0, 0)),
        vec, vec, vec,
    ]


def _conv_fwd(proj, cw_pad, cb, lg, lb, cw):
    t, d_in = proj.shape
    nb = t // BLK

    def body(a_ref, g_ref, w_ref, b_ref, lg_ref, lb_ref, o_ref, hpad):
        _conv_fill_glu(a_ref, g_ref, hpad, nb)

        def blk(i, c):
            r = pl.multiple_of(i * BLK, BLK)
            z = _conv_taps(hpad[pl.ds(r, BLK + HALO), :], w_ref, lambda j: j + HALO - (CONV_K - 1)) + b_ref[...]
            yhat, _ = _layer_norm_fwd(z)
            y = yhat * lg_ref[...] + lb_ref[...]
            o_ref[pl.ds(r, BLK), :] = y * _sigmoid(y)
            return c

        lax.fori_loop(0, nb, blk, 0)

    return pl.pallas_call(
        body, name="conv_fwd", grid=(1,),
        in_specs=_conv_specs(t, cw, d_in),
        out_specs=pl.BlockSpec((t, cw), lambda i: (0, 0)),
        out_shape=SDS((t, cw), F32),
        scratch_shapes=[pltpu.VMEM((t + HALO, cw), F32)],
        compiler_params=_params("arbitrary"),
    )(proj, proj, cw_pad, cb, lg, lb)


def _conv_bwd(proj, dy, cw_pad, cb, lg, lb, cw):
    t, d_in = proj.shape
    nb = t // BLK

    def body(a_ref, g_ref, w_ref, b_ref, lg_ref, lb_ref, dy_ref, dc_ref, dw_ref, dvec_ref, hpad, dzpad, dwacc):
        _conv_fill_glu(a_ref, g_ref, hpad, nb)
        dzpad[t:t + HALO, :] = jnp.zeros((HALO, cw), F32)
        dwacc[...] = jnp.zeros_like(dwacc)

        def blk(i, carry):
            db, dlg, dlb = carry
            r = pl.multiple_of(i * BLK, BLK)
            win = hpad[pl.ds(r, BLK + HALO), :]
            z = _conv_taps(win, w_ref, lambda j: j + HALO - (CONV_K - 1)) + b_ref[...]
            yhat, rstd = _layer_norm_fwd(z)
            y = yhat * lg_ref[...] + lb_ref[...]
            sg = _sigmoid(y)
            dyl = dy_ref[pl.ds(r, BLK), :] * (sg * (1.0 + y * (1.0 - sg)))
            dz = _layer_norm_bwd(dyl, yhat, rstd, lg_ref[...])
            dzpad[pl.ds(r, BLK), :] = dz
            for j in range(CONV_K):
                sh = j + HALO - (CONV_K - 1)
                prod = dz * pltpu.roll(win, (BLK + HALO - sh) % (BLK + HALO), axis=0)[:BLK]
                dwacc[8 * j:8 * j + 8, :] += jnp.sum(prod.reshape(BLK // 8, 8, cw), axis=0)
            return (db + jnp.sum(dz, axis=0, keepdims=True),
                    dlg + jnp.sum(dyl * yhat, axis=0, keepdims=True),
                    dlb + jnp.sum(dyl, axis=0, keepdims=True))

        zero = jnp.zeros((1, cw), F32)
        db, dlg, dlb = lax.fori_loop(0, nb, blk, (zero, zero, zero))
        dvec_ref[...] = jnp.zeros_like(dvec_ref)
        dvec_ref[0:1, :] = db
        dvec_ref[1:2, :] = dlg
        dvec_ref[2:3, :] = dlb
        dw_ref[...] = jnp.sum(dwacc[...].reshape(HALO, 8, cw), axis=1)

        def blk2(i, c):
            r = pl.multiple_of(i * BLK, BLK)
            dh = _conv_taps(dzpad[pl.ds(r, BLK + HALO), :], w_ref, lambda j: CONV_K - 1 - j)
            a = a_ref[pl.ds(r, BLK), :]
            sg = _sigmoid(g_ref[pl.ds(r, BLK), :])
            dc_ref[pl.ds(r, BLK), 0:cw] = (dh * sg).astype(BF16)
            dc_ref[pl.ds(r, BLK), cw:2 * cw] = (dh * a * sg * (1.0 - sg)).astype(BF16)
            return c

        lax.fori_loop(0, nb, blk2, 0)

    return pl.pallas_call(
        body, name="conv_bwd", grid=(1,),
        in_specs=_conv_specs(t, cw, d_in) + [pl.BlockSpec((t, cw), lambda i: (0, 0))],
        out_specs=[pl.BlockSpec((t, 2 * cw), lambda i: (0, 0)), pl.BlockSpec((HALO, cw), lambda i: (0, 0)),
                   pl.BlockSpec((8, cw), lambda i: (0, 0))],
        out_shape=[SDS((t, 2 * cw), BF16), SDS((HALO, cw), F32), SDS((8, cw), F32)],
        scratch_shapes=[pltpu.VMEM((t + HALO, cw), F32), pltpu.VMEM((t + HALO, cw), F32), pltpu.VMEM((8 * HALO, cw), F32)],
        compiler_params=_params("arbitrary"),
    )(proj, proj, cw_pad, cb, lg, lb, dy)


def _tril_bf16(w):
    r = lax.broadcasted_iota(jnp.int32, (BLK, BLK), 0)
    c = lax.broadcasted_iota(jnp.int32, (BLK, BLK), 1)
    return jnp.where(r >= c, w, 0.0).astype(BF16)


def _sgu_specs(sw, d_in, heads):
    base = (d_in - 2 * sw) // sw
    vec = pl.BlockSpec((1, sw), lambda n: (0, 0))
    return [
        pl.BlockSpec((BLK, sw), lambda n: (n, base)),
        pl.BlockSpec((BLK, sw), lambda n: (n, base + 1)),
        vec, vec,
        pl.BlockSpec((heads, BLK, BLK), lambda n: (0, 0, 0)),
        pl.BlockSpec((BLK, sw), lambda n: (0, 0)),
    ]


def _sgu_mix(w_ref, vnb, heads, sw, transpose):
    head_of = lax.broadcasted_iota(jnp.int32, (1, sw), 1) // HEAD
    s = jnp.zeros((BLK, sw), F32)
    for h in range(heads):
        wt = _tril_bf16(w_ref[h])
        mixed = _tn(wt, vnb) if transpose else _nn(wt, vnb)
        s = jnp.where(head_of == h, mixed, s)
    return s


def _sgu_fwd(proj, lg, lb, w, bias_full, sw):
    t, d_in = proj.shape
    heads = sw // HEAD

    def body(u_ref, v_ref, lg_ref, lb_ref, w_ref, bias_ref, o_ref):
        vhat, _ = _layer_norm_fwd(v_ref[...])
        vn = (vhat * lg_ref[...] + lb_ref[...]).astype(BF16)
        s = _sgu_mix(w_ref, vn, heads, sw, False) + bias_ref[...]
        o_ref[...] = u_ref[...] * s

    return pl.pallas_call(
        body, name="sgu_fwd", grid=(t // BLK,),
        in_specs=_sgu_specs(sw, d_in, heads),
        out_specs=pl.BlockSpec((BLK, sw), lambda n: (n, 0)),
        out_shape=SDS((t, sw), F32),
        compiler_params=_params("arbitrary"),
    )(proj, proj, lg, lb, w, bias_full)


def _sgu_bwd(proj, dy, lg, lb, w, bias_full, sw):
    t, d_in = proj.shape
    heads = sw // HEAD
    nb = t // BLK

    def body(u_ref, v_ref, lg_ref, lb_ref, w_ref, bias_ref, dy_ref, ds_ref, dw_ref, db_ref, dvec_ref, dbfull):
        n = pl.program_id(0)

        @pl.when(n == 0)
        def _():
            dw_ref[...] = jnp.zeros_like(dw_ref)
            dvec_ref[...] = jnp.zeros_like(dvec_ref)
            dbfull[...] = jnp.zeros_like(dbfull)

        vhat, rstd = _layer_norm_fwd(v_ref[...])
        vn = (vhat * lg_ref[...] + lb_ref[...]).astype(BF16)
        s = _sgu_mix(w_ref, vn, heads, sw, False) + bias_ref[...]
        dy = dy_ref[...]
        ds_ref[:, 0:sw] = (dy * s).astype(BF16)
        dsv = dy * u_ref[...]
        dbfull[...] += dsv
        head_of = lax.broadcasted_iota(jnp.int32, (1, sw), 1) // HEAD
        r = lax.broadcasted_iota(jnp.int32, (BLK, BLK), 0)
        c = lax.broadcasted_iota(jnp.int32, (BLK, BLK), 1)
        dsb = dsv.astype(BF16)
        for h in range(heads):
            dsh = jnp.where(head_of == h, dsv, 0.0).astype(BF16)
            dw_ref[h] += jnp.where(r >= c, _nt(dsh, vn), 0.0)
        dvn = _sgu_mix(w_ref, dsb, heads, sw, True)
        dvec_ref[0:1, :] += jnp.sum(dvn * vhat, axis=0, keepdims=True)
        dvec_ref[1:2, :] += jnp.sum(dvn, axis=0, keepdims=True)
        ds_ref[:, sw:2 * sw] = _layer_norm_bwd(dvn, vhat, rstd, lg_ref[...]).astype(BF16)

        @pl.when(n == nb - 1)
        def _():
            sel = (lax.broadcasted_iota(jnp.int32, (sw, BLK), 0) // HEAD == lax.broadcasted_iota(jnp.int32, (sw, BLK), 1)).astype(BF16)
            x = dbfull[...]
            hi = x.astype(BF16)
            r1 = x - hi.astype(F32)
            mid = r1.astype(BF16)
            low = (r1 - mid.astype(F32)).astype(BF16)
            db_ref[...] = _nn(hi, sel) + _nn(mid, sel) + _nn(low, sel)

    return pl.pallas_call(
        body, name="sgu_bwd", grid=(nb,),
        in_specs=_sgu_specs(sw, d_in, heads) + [pl.BlockSpec((BLK, sw), lambda n: (n, 0))],
        out_specs=[pl.BlockSpec((BLK, 2 * sw), lambda n: (n, 0)), pl.BlockSpec((heads, BLK, BLK), lambda n: (0, 0, 0)),
                   pl.BlockSpec((BLK, BLK), lambda n: (0, 0)), pl.BlockSpec((8, sw), lambda n: (0, 0))],
        out_shape=[SDS((t, 2 * sw), BF16), SDS((heads, BLK, BLK), F32), SDS((BLK, BLK), F32), SDS((8, sw), F32)],
        scratch_shapes=[pltpu.VMEM((BLK, sw), F32)],
        compiler_params=_params("arbitrary"),
    )(proj, proj, lg, lb, w, bias_full, dy)


def _rms_fwd(x, g):
    return (x * lax.rsqrt(jnp.mean(x * x, axis=-1, keepdims=True) + EPS)) * g


def _rms_bwd(dh, x, g):
    rstd = lax.rsqrt(jnp.mean(x * x, axis=-1, keepdims=True) + EPS)
    xhat = x * rstd
    dgx = dh * g
    dx = rstd * (dgx - xhat * jnp.mean(dgx * xhat, axis=-1, keepdims=True))
    return dx, jnp.sum(dh * xhat, axis=0, keepdims=True)


def _rms_matmul(x, g, w, tm, tn, relu2, name):
    t, d = x.shape
    if w.ndim == 3:
        assert w.shape[2] == tn
        n = w.shape[0] * tn
        w_spec = pl.BlockSpec((None, d, tn), lambda i, j: (j, 0, 0))
    else:
        n = w.shape[1]
        w_spec = pl.BlockSpec((d, tn), lambda i, j: (0, j))

    def body(x_ref, g_ref, w_ref, h_ref, *outs):
        @pl.when(pl.program_id(1) == 0)
        def _():
            h_ref[...] = _rms_fwd(x_ref[...], g_ref[...]).astype(BF16)

        acc = _nn(h_ref[...], w_ref[...])
        if relu2:
            r = jnp.maximum(acc, 0.0)
            outs[0][...] = (r * r).astype(BF16)
            outs[1][...] = r.astype(BF16)
        else:
            outs[0][...] = acc

    tile = pl.BlockSpec((tm, tn), lambda i, j: (i, j))
    row = pl.BlockSpec((tm, d), lambda i, j: (i, 0))
    outs = [SDS((t, n), BF16), SDS((t, n), BF16)] if relu2 else [SDS((t, n), F32)]
    return pl.pallas_call(
        body, name=name, grid=(t // tm, n // tn),
        in_specs=[row, pl.BlockSpec((1, d), lambda i, j: (0, 0)), w_spec],
        out_specs=[row] + [tile] * len(outs),
        out_shape=[SDS((t, d), BF16)] + outs,
        compiler_params=_params("parallel", "arbitrary"),
    )(x, g, w)


def _group_rms_matmul(ys, g, w, res, tm, tn):
    t, d = res.shape
    widths = [y.shape[1] for y in ys]
    k = sum(widths)

    def body(*refs):
        y_refs, (g_ref, w_ref, res_ref, mix_ref, o_ref) = refs[:len(ys)], refs[len(ys):]

        @pl.when(pl.program_id(1) == 0)
        def _():
            c = 0
            for y_ref, wd in zip(y_refs, widths):
                mix_ref[:, c:c + wd] = _rms_fwd(y_ref[...], g_ref[:, c:c + wd]).astype(BF16)
                c += wd

        o_ref[...] = res_ref[...] + _nn(mix_ref[...], w_ref[...])

    tile = pl.BlockSpec((tm, tn), lambda i, j: (i, j))
    return pl.pallas_call(
        body, name="mix_out", grid=(t // tm, d // tn),
        in_specs=[pl.BlockSpec((tm, wd), lambda i, j: (i, 0)) for wd in widths] + [
            pl.BlockSpec((1, k), lambda i, j: (0, 0)), pl.BlockSpec((k, tn), lambda i, j: (0, j)), tile],
        out_specs=[pl.BlockSpec((tm, k), lambda i, j: (i, 0)), tile],
        out_shape=[SDS((t, k), BF16), SDS((t, d), F32)],
        compiler_params=_params("parallel", "arbitrary"),
    )(*ys, g, w, res)


def _matmul_res(a, w, res, tm, tn, tk):
    t, k = a.shape
    n = w.shape[1]
    nk = k // tk

    def body(a_ref, w_ref, res_ref, o_ref, acc):
        kk = pl.program_id(2)

        @pl.when(kk == 0)
        def _():
            acc[...] = res_ref[...]

        acc[...] += _nn(a_ref[...], w_ref[...])

        @pl.when(kk == nk - 1)
        def _():
            o_ref[...] = acc[...]

    tile = pl.BlockSpec((tm, tn), lambda i, j, kk: (i, j))
    return pl.pallas_call(
        body, name="mlp_down", grid=(t // tm, n // tn, nk),
        in_specs=[pl.BlockSpec((tm, tk), lambda i, j, kk: (i, kk)), pl.BlockSpec((tk, tn), lambda i, j, kk: (kk, j)), tile],
        out_specs=tile,
        out_shape=SDS((t, n), F32),
        scratch_shapes=[pltpu.VMEM((tm, tn), F32)],
        compiler_params=_params("parallel", "parallel", "arbitrary"),
    )(a, w, res)


def _loss_grad(y, target, tm):
    t, d = y.shape

    def body(y_ref, t_ref, dy_ref, l_ref):
        @pl.when(pl.program_id(0) == 0)
        def _():
            l_ref[...] = jnp.zeros_like(l_ref)

        err = y_ref[...] - t_ref[...]
        dy_ref[...] = err * (1.0 / d)
        per_row = jnp.mean(err * err, axis=-1, keepdims=True)
        l_ref[...] += jnp.sum(per_row, axis=0, keepdims=True) * 0.5

    row = pl.BlockSpec((tm, d), lambda i: (i, 0))
    return pl.pallas_call(
        body, name="loss_grad", grid=(t // tm,),
        in_specs=[row, row], out_specs=[row, pl.BlockSpec((8, BLK), lambda i: (0, 0))],
        out_shape=[SDS((t, d), F32), SDS((8, BLK), F32)],
        compiler_params=_params("arbitrary"),
    )(y, target)


def _mlp_dact(dx, w_down, r, tm, tn, after=None):
    t, d = dx.shape
    f = w_down.shape[0]

    def body(dx_ref, w_ref, r_ref, dxb_ref, dup_ref):
        @pl.when(pl.program_id(1) == 0)
        def _():
            dxb_ref[...] = dx_ref[...].astype(BF16)

        dup_ref[...] = (_nt(dxb_ref[...], w_ref[...]) * (2.0 * r_ref[...].astype(F32))).astype(BF16)

    row = pl.BlockSpec((tm, d), lambda i, j: (i, 0))
    tile = pl.BlockSpec((tm, tn), lambda i, j: (i, j))
    body, in_specs, args = _ordered_behind(
        body, 3, [row, pl.BlockSpec((tn, d), lambda i, j: (j, 0)), tile], [dx, w_down, r], after)
    return pl.pallas_call(
        body, name="mlp_dact", grid=(t // tm, f // tn),
        in_specs=in_specs,
        out_specs=[row, tile],
        out_shape=[SDS((t, d), BF16), SDS((t, f), BF16)],
        compiler_params=_params("parallel", "arbitrary"),
    )(*args)


def _grad_w(a, bs, tm, tn, name, col_blocks=False):
    t, m = a.shape
    widths = [b.shape[1] for b in bs]
    if len(bs) == 1:
        assert widths[0] % tn == 0
        n_tiles = widths[0] // tn

        def body(a_ref, b_ref, o_ref):
            o_ref[...] = _tn(a_ref[...], b_ref[...]).astype(BF16)

        if col_blocks:
            out_spec = pl.BlockSpec((None, tm, tn), lambda i, j: (j, i, 0))
            out_shape = SDS((n_tiles, m, tn), BF16)
        else:
            out_spec = pl.BlockSpec((tm, tn), lambda i, j: (i, j))
            out_shape = SDS((m, widths[0]), BF16)
        return pl.pallas_call(
            body, name=name, grid=(m // tm, n_tiles),
            in_specs=[pl.BlockSpec((t, tm), lambda i, j: (0, i)), pl.BlockSpec((t, tn), lambda i, j: (0, j))],
            out_specs=out_spec, out_shape=out_shape,
            compiler_params=_params("parallel", "arbitrary"),
        )(a, bs[0])

    offs = [sum(widths[:p]) for p in range(len(bs))]

    def body(a_ref, *refs):
        b_refs, o_ref = refs[:-1], refs[-1]
        for b_ref, off, wd in zip(b_refs, offs, widths):
            o_ref[:, off:off + wd] = _tn(a_ref[...], b_ref[...]).astype(BF16)

    return pl.pallas_call(
        body, name=name, grid=(m // tm,),
        in_specs=[pl.BlockSpec((t, tm), lambda i: (0, i))] + [pl.BlockSpec((t, wd), lambda i: (0, 0)) for wd in widths],
        out_specs=pl.BlockSpec((tm, sum(widths)), lambda i: (i, 0)),
        out_shape=SDS((m, sum(widths)), BF16),
        compiler_params=_params("arbitrary"),
    )(a, *bs)


def _mlp_dnorm(dup, w_up, x, g, dres, tm, after=None):
    t, f = dup.shape
    d = x.shape[1]
    nk, _, tk = w_up.shape

    def body(a_ref, w_ref, x_ref, g_ref, dres_ref, dx_ref, dg_ref, acc):
        i, kk = pl.program_id(0), pl.program_id(1)

        @pl.when((i == 0) & (kk == 0))
        def _():
            dg_ref[...] = jnp.zeros_like(dg_ref)

        @pl.when(kk == 0)
        def _():
            acc[...] = jnp.zeros_like(acc)

        acc[...] += _nt(a_ref[...], w_ref[...])

        @pl.when(kk == nk - 1)
        def _():
            dx, dg = _rms_bwd(acc[...], x_ref[...], g_ref[...])
            dx_ref[...] = dres_ref[...] + dx
            dg_ref[0:1, :] += dg

    row = pl.BlockSpec((tm, d), lambda i, kk: (i, 0))
    in_specs = [pl.BlockSpec((tm, tk), lambda i, kk: (i, kk)), pl.BlockSpec((None, d, tk), lambda i, kk: (kk, 0, 0)),
                row, pl.BlockSpec((1, d), lambda i, kk: (0, 0)), row]
    body, in_specs, args = _ordered_behind(body, 5, in_specs, [dup, w_up, x, g, dres], after)
    return pl.pallas_call(
        body, name="mlp_dnorm", grid=(t // tm, nk),
        in_specs=in_specs,
        out_specs=[row, pl.BlockSpec((8, d), lambda i, kk: (0, 0))],
        out_shape=[SDS((t, d), F32), SDS((8, d), F32)],
        scratch_shapes=[pltpu.VMEM((tm, d), F32)],
        compiler_params=_params("arbitrary", "arbitrary"),
    )(*args)


def _mix_dnorm(dx, w_out, ys, g, tm, after=None):
    t, d = dx.shape
    k = w_out.shape[0]
    widths = [y.shape[1] for y in ys]

    def body(dx_ref, w_ref, *refs):
        y_refs = refs[:len(ys)]
        g_ref, dxb_ref = refs[len(ys)], refs[len(ys) + 1]
        dy_refs = refs[len(ys) + 2:2 * len(ys) + 2]
        dg_ref = refs[-1]

        @pl.when(pl.program_id(0) == 0)
        def _():
            dg_ref[...] = jnp.zeros_like(dg_ref)

        dxb = dx_ref[...].astype(BF16)
        dxb_ref[...] = dxb
        dmix = _nt(dxb, w_ref[...])
        c = 0
        for y_ref, dy_ref, wd in zip(y_refs, dy_refs, widths):
            dy, dg = _rms_bwd(dmix[:, c:c + wd], y_ref[...], g_ref[:, c:c + wd])
            dy_ref[...] = dy
            dg_ref[0:1, c:c + wd] += dg
            c += wd

    row = pl.BlockSpec((tm, d), lambda i: (i, 0))
    yspecs = [pl.BlockSpec((tm, wd), lambda i: (i, 0)) for wd in widths]
    in_specs = [row, pl.BlockSpec((k, d), lambda i: (0, 0))] + yspecs + [pl.BlockSpec((1, k), lambda i: (0, 0))]
    body, in_specs, args = _ordered_behind(body, len(in_specs), in_specs, [dx, w_out, *ys, g], after)
    return pl.pallas_call(
        body, name="mix_dnorm", grid=(t // tm,),
        in_specs=in_specs,
        out_specs=[row] + yspecs + [pl.BlockSpec((8, k), lambda i: (0, 0))],
        out_shape=[SDS((t, d), BF16)] + [SDS((t, wd), F32) for wd in widths] + [SDS((8, k), F32)],
        compiler_params=_params("arbitrary"),
    )(*args)


def _in_dnorm(dps, w_in, x, g, dres, tm):
    t, d = x.shape
    widths = [p.shape[1] for p in dps]
    offs = [sum(widths[:p]) for p in range(len(dps))]
    n_in = w_in.shape[1]

    def body(*refs):
        p_refs = refs[:len(dps)]
        w_ref, x_ref, g_ref, dres_ref, dx_ref, dg_ref = refs[len(dps):]

        @pl.when(pl.program_id(0) == 0)
        def _():
            dg_ref[...] = jnp.zeros_like(dg_ref)

        acc = None
        for p_ref, off, wd in zip(p_refs, offs, widths):
            term = _nt(p_ref[...], w_ref[:, off:off + wd])
            acc = term if acc is None else acc + term
        dx, dg = _rms_bwd(acc, x_ref[...], g_ref[...])
        dx_ref[...] = dres_ref[...] + dx
        dg_ref[0:1, :] += dg

    row = pl.BlockSpec((tm, d), lambda i: (i, 0))
    return pl.pallas_call(
        body, name="in_dnorm", grid=(t // tm,),
        in_specs=[pl.BlockSpec((tm, wd), lambda i: (i, 0)) for wd in widths] + [
            pl.BlockSpec((d, n_in), lambda i: (0, 0)), row, pl.BlockSpec((1, d), lambda i: (0, 0)), row],
        out_specs=[row, pl.BlockSpec((8, d), lambda i: (0, 0))],
        out_shape=[SDS((t, d), F32), SDS((8, d), F32)],
        compiler_params=_params("arbitrary"),
    )(*dps, w_in, x, g, dres)


def _tile(n, want):
    return min(n, want)


def _layer_fwd(x, p, fetch, after):
    t, d = x.shape
    aw, kv, cw, sw = d // 2, d // 8, d // 4, d // 4
    tm = _tile(t, 512)
    w_in = fetch("w_in", after)
    h1, proj = _rms_matmul(x, p["ln1_g"], w_in, tm, 512 if w_in.shape[1] % 512 == 0 else 256, False, "in_proj")
    y_attn = _attn_fwd(proj, p["qg"], p["kg"], p["sinks"], aw, kv)
    y_conv = _conv_fwd(proj, p["conv_w"], p["conv_b"], p["conv_ln_g"], p["conv_ln_b"], cw)
    y_sgu = _sgu_fwd(proj, p["sgu_ln_g"], p["sgu_ln_b"], p["sgu_w"], p["sgu_bias"], sw)
    ys = [y_attn, y_conv, y_sgu]
    w_out = fetch("w_out", y_sgu)
    mix, x1 = _group_rms_matmul(ys, p["out_norm_g"], w_out, x, tm, _tile(d, 1024))
    w_up = fetch("w_up", x1)
    h2, act, r = _rms_matmul(x1, p["ln2_g"], w_up, tm, w_up.shape[2], True, "mlp_up")
    w_down = fetch("w_down", act)
    x2 = _matmul_res(act, w_down, x1, tm, _tile(d, 1024), 1024)
    saved = dict(x=x, h1=h1, proj=proj, ys=ys, mix=mix, x1=x1, h2=h2, act=act, r=r,
                 w_in=w_in, w_out=w_out, w_up=w_up, w_down=w_down)
    return x2, saved


def _layer_bwd(dx2, p, s, reduce, after):
    t, d = dx2.shape
    aw, kv, cw, sw = d // 2, d // 8, d // 4, d // 4
    tm = _tile(t, 512)
    dx2b, dup = _mlp_dact(dx2, s["w_down"], s["r"], tm, 1024, after)
    tok = reduce("w_down", _grad_w(s["act"], [dx2b], 512, _tile(d, 2048), "grad_w_down"))
    dx1, d_ln2 = _mlp_dnorm(dup, s["w_up"], s["x1"], p["ln2_g"], dx2, tm, tok)
    tok = reduce("w_up", _grad_w(s["h2"], [dup], 512, s["w_up"].shape[2], "grad_w_up", col_blocks=True))
    dx1b, dya, dyc, dys, d_onorm = _mix_dnorm(dx1, s["w_out"], s["ys"], p["out_norm_g"], _tile(t, 256), tok)
    tok = reduce("w_out", _grad_w(s["mix"], [dx1b], 512, _tile(d, 2048), "grad_w_out"))
    dqkv, d_attn = _attn_bwd(s["proj"], dya, p["qg"], p["kg"], p["sinks"], aw, kv, tok)
    dconv, d_cw, d_cvec = _conv_bwd(s["proj"], dyc, p["conv_w"], p["conv_b"], p["conv_ln_g"], p["conv_ln_b"], cw)
    dsgu, d_sw, d_sb, d_svec = _sgu_bwd(s["proj"], dys, p["sgu_ln_g"], p["sgu_ln_b"], p["sgu_w"], p["sgu_bias"], sw)
    dps = [dqkv, dconv, dsgu]
    dx, d_ln1 = _in_dnorm(dps, s["w_in"], s["x"], p["ln1_g"], dx1, _tile(t, 256))
    tok = reduce("w_in", _grad_w(s["h1"], dps, 512, 512, "grad_w_in"))
    heads = sw // HEAD
    small = dict(
        ln1_g=d_ln1[0], q_norm_g=d_attn[0, :HEAD], k_norm_g=d_attn[1, :HEAD], sinks=d_attn[2, :aw // HEAD],
        conv_w=d_cw[:CONV_K], conv_b=d_cvec[0], conv_ln_g=d_cvec[1], conv_ln_b=d_cvec[2],
        sgu_ln_g=d_svec[0], sgu_ln_b=d_svec[1], sgu_w=d_sw, sgu_b=d_sb[:, :heads].T,
        out_norm_g=d_onorm[0], ln2_g=d_ln2[0])
    return dx, small, tok


def _layer_params(l, small):
    row = lambda v: v[l][None, :]
    two = lambda v: jnp.tile(v[l], 2)[None, :]
    return dict(
        ln1_g=row(small["ln1_g"]), ln2_g=row(small["ln2_g"]), out_norm_g=row(small["out_norm_g"]),
        qg=two(small["q_norm_g"]), kg=two(small["k_norm_g"]), sinks=small["sinks"][l],
        conv_w=jnp.pad(small["conv_w"][l], ((0, HALO - CONV_K), (0, 0))),
        conv_b=row(small["conv_b"]), conv_ln_g=row(small["conv_ln_g"]), conv_ln_b=row(small["conv_ln_b"]),
        sgu_ln_g=row(small["sgu_ln_g"]), sgu_ln_b=row(small["sgu_ln_b"]), sgu_w=small["sgu_w"][l],
        sgu_bias=jnp.repeat(small["sgu_b"][l].T, HEAD, axis=1),
    )


def _local_step(x, target, small, depth, fetch, reduce, after):
    params = [_layer_params(l, small) for l in range(depth)]
    saved = []
    h = x
    for l in range(depth):
        h, s = _layer_fwd(h, params[l], functools.partial(fetch, l), after)
        after = h
        saved.append(s)
    dy, lsum = _loss_grad(h, target, _tile(x.shape[0], 512))
    gs = [None] * depth
    tok = None
    for l in reversed(range(depth)):
        dy, gs[l], tok = _layer_bwd(dy, params[l], saved[l], functools.partial(reduce, l), tok)
    return lsum[0, 0], dy, gs, tok


BIG = ("w_in", "w_out", "w_up", "w_down")
HBM = pl.BlockSpec(memory_space=pltpu.HBM)
SEMS = pl.BlockSpec(memory_space=pltpu.SEMAPHORE)
EFFECT = pltpu.SideEffectType.DATAFLOW_SIDE_EFFECTING


def _mesh_pos():
    return lax.axis_index("x"), lax.axis_index("y"), lax.axis_index("c")


def _other_chips(x, y):
    return [(1 - x, y), (x, 1 - y), (1 - x, 1 - y)]


def _copies(plan, refs, sends, recvs):
    x, y, c = _mesh_pos()
    return [pltpu.make_async_remote_copy(src_ref=src, dst_ref=dst, send_sem=sends.at[k], recv_sem=recvs.at[k],
                                         device_id=dev, device_id_type=MESH)
            for k, (src, dst, dev) in enumerate(plan(x, y, c, refs))]


def _gather_plan(x, y, c, refs):
    mine = refs[0].at[4 * x + 2 * y + c]
    return [(mine, mine, (x, y, 1 - c))] + [(mine, mine, (*chip, c)) for chip in _other_chips(x, y)]


def _chips_plan(x, y, c, refs):
    sums, land = refs
    return [(sums.at[2 * chip[0] + chip[1]], land.at[k], (*chip, c)) for k, chip in enumerate(_other_chips(x, y))]


def _start_exchanges(name, groups):
    flat = [a for arrays, _, _ in groups for a in arrays]
    n_arr, n_g = len(flat), len(groups)

    def body(*refs):
        ins, sems, token = refs[:n_arr], refs[n_arr:n_arr + 2 * n_g], refs[-1]
        off = 0
        for gi, (arrays, plan, _) in enumerate(groups):
            for cp in _copies(plan, ins[off:off + len(arrays)], sems[2 * gi], sems[2 * gi + 1]):
                cp.start()
            off += len(arrays)
        token[...] = jnp.zeros_like(token)

    res = pl.pallas_call(
        body, name=name,
        out_shape=[pltpu.SemaphoreType.DMA((n,)) for _, _, n in groups for _ in (0, 1)]
        + [pltpu.HBM(a.shape, a.dtype) for a in flat] + [SDS((8, BLK), F32)],
        in_specs=[HBM] * n_arr,
        out_specs=[SEMS] * (2 * n_g) + [HBM] * n_arr + [pl.BlockSpec(memory_space=pltpu.VMEM)],
        input_output_aliases={i: 2 * n_g + i for i in range(n_arr)},
        compiler_params=pltpu.CompilerParams(has_side_effects=EFFECT),
    )(*[pltpu.with_memory_space_constraint(a, pltpu.HBM) for a in flat])
    sems, thru, token = res[:2 * n_g], res[2 * n_g:2 * n_g + n_arr], res[-1]
    out, off = [], 0
    for gi, (arrays, _, _) in enumerate(groups):
        out.append((list(thru[off:off + len(arrays)]), sems[2 * gi], sems[2 * gi + 1]))
        off += len(arrays)
    return out, token


def _wait_exchange(name, arrays, sends, recvs, plan, after):
    n = len(arrays)

    def body(*refs):
        for cp in _copies(plan, refs[:n], refs[n], refs[n + 1]):
            cp.wait_send()
            cp.wait_recv()

    return pl.pallas_call(
        body, name=name,
        out_shape=[pltpu.HBM(a.shape, a.dtype) for a in arrays],
        in_specs=[HBM] * n + [SEMS, SEMS, ANY],
        out_specs=[HBM] * n,
        input_output_aliases={i: i for i in range(n)},
        compiler_params=pltpu.CompilerParams(has_side_effects=EFFECT),
    )(*arrays, sends, recvs, after)


def _gather_finish(land, name):
    def body(land_ref, out_ref, send_sems, recv_sems):
        del land_ref
        x, y, c = _mesh_pos()
        cps = []
        for k, chip in enumerate(_other_chips(x, y)):
            block = out_ref.at[4 * chip[0] + 2 * chip[1] + c]
            cps.append(pltpu.make_async_remote_copy(
                src_ref=block, dst_ref=block, send_sem=send_sems.at[k], recv_sem=recv_sems.at[k],
                device_id=(x, y, 1 - c), device_id_type=MESH))
        for cp in cps:
            cp.start()
        for cp in cps:
            cp.wait()

    return pl.pallas_call(
        body, name=name,
        in_specs=[ANY], out_specs=ANY,
        out_shape=SDS(land.shape, land.dtype),
        input_output_aliases={0: 0},
        scratch_shapes=[pltpu.SemaphoreType.DMA((3,)), pltpu.SemaphoreType.DMA((3,))],
    )(land)


def _all_gather(shards, name):
    n = len(shards)

    def body(*refs):
        ins, outs = refs[:n], refs[n:2 * n]
        send_sems, recv_sems, local_sems = refs[2 * n:]
        x, y, c = _mesh_pos()
        me, sibling = (x, y, c), (x, y, 1 - c)
        chips = _other_chips(x, y)

        def copy(a, k, block, to, src=None):
            dst = outs[a].at[4 * block[0] + 2 * block[1] + block[2]]
            return pltpu.make_async_remote_copy(
                src_ref=dst if src is None else src, dst_ref=dst,
                send_sem=send_sems.at[a, k], recv_sem=recv_sems.at[a, k], device_id=to, device_id_type=MESH)

        mine = [pltpu.make_async_copy(ins[a], outs[a].at[4 * x + 2 * y + c], local_sems.at[a]) for a in range(n)]
        for cp in mine:
            cp.start()
        first = []
        for a in range(n):
            first.append(copy(a, 0, me, sibling, src=ins[a]))
            first += [copy(a, 1 + j, me, (*chip, c), src=ins[a]) for j, chip in enumerate(chips)]
        for cp in first:
            cp.start()
        passed = []
        for j, chip in enumerate(chips):
            for a in range(n):
                copy(a, 1 + j, (*chip, c), me).wait_recv()
                fwd = copy(a, 4 + j, (*chip, c), sibling)
                fwd.start()
                passed.append(fwd)
        for a in range(n):
            copy(a, 0, sibling, me).wait_recv()
            for j, chip in enumerate(chips):
                copy(a, 4 + j, (*chip, 1 - c), me).wait_recv()
        for cp in first + passed:
            cp.wait_send()
        for cp in mine:
            cp.wait()

    outs = pl.pallas_call(
        body, name=name,
        in_specs=[ANY] * n, out_specs=[ANY] * n,
        out_shape=[SDS((N_DEV,) + s.shape, s.dtype) for s in shards],
        scratch_shapes=[pltpu.SemaphoreType.DMA((n, 7)), pltpu.SemaphoreType.DMA((n, 7)), pltpu.SemaphoreType.DMA((n,))],
    )(*shards)
    return list(outs)


def _exchange_pair(grads, name):
    n = len(grads)

    def body(*refs):
        ins, outs = refs[:n], refs[n:2 * n]
        send_sems, recv_sems = refs[2 * n:]
        x, y, c = _mesh_pos()
        cps = []
        for a in range(n):
            for q in range(4):
                cps.append(pltpu.make_async_remote_copy(
                    src_ref=ins[a].at[2 * q + (1 - c)], dst_ref=outs[a].at[q],
                    send_sem=send_sems.at[a, q], recv_sem=recv_sems.at[a, q],
                    device_id=(x, y, 1 - c), device_id_type=MESH))
        for cp in cps:
            cp.start()
        for cp in cps:
            cp.wait()

    outs = pl.pallas_call(
        body, name=name,
        in_specs=[ANY] * n, out_specs=[ANY] * n,
        out_shape=[SDS((4,) + g.shape[1:], g.dtype) for g in grads],
        scratch_shapes=[pltpu.SemaphoreType.DMA((n, 4)), pltpu.SemaphoreType.DMA((n, 4))],
    )(*grads)
    return list(outs)


def _exchange_chips(sums, name):
    n = len(sums)

    def body(*refs):
        ins, outs = refs[:n], refs[n:2 * n]
        send_sems, recv_sems = refs[2 * n:]
        x, y, c = _mesh_pos()
        cps = []
        for a in range(n):
            for k, chip in enumerate(_other_chips(x, y)):
                cps.append(pltpu.make_async_remote_copy(
                    src_ref=ins[a].at[2 * chip[0] + chip[1]], dst_ref=outs[a].at[k],
                    send_sem=send_sems.at[a, k], recv_sem=recv_sems.at[a, k],
                    device_id=(*chip, c), device_id_type=MESH))
        for cp in cps:
            cp.start()
        for cp in cps:
            cp.wait()

    outs = pl.pallas_call(
        body, name=name,
        in_specs=[ANY] * n, out_specs=[ANY] * n,
        out_shape=[SDS((3,) + s.shape[1:], s.dtype) for s in sums],
        scratch_shapes=[pltpu.SemaphoreType.DMA((n, 3)), pltpu.SemaphoreType.DMA((n, 3))],
    )(*sums)
    return list(outs)


def _pair_add(own, got, c, name):
    _, r, cols = own.shape
    tr = _tile(r, 512)

    def body(c_ref, own_ref, got_ref, o_ref):
        o_ref[...] = (own_ref[...].astype(F32) + got_ref[...].astype(F32)).astype(BF16)

    return pl.pallas_call(
        body, name=name,
        grid_spec=pltpu.PrefetchScalarGridSpec(
            num_scalar_prefetch=1, grid=(4, r // tr),
            in_specs=[pl.BlockSpec((None, tr, cols), lambda q, i, c_ref: (2 * q + c_ref[0], i, 0)),
                      pl.BlockSpec((None, tr, cols), lambda q, i, c_ref: (q, i, 0))],
            out_specs=pl.BlockSpec((None, tr, cols), lambda q, i, c_ref: (q, i, 0))),
        out_shape=SDS((4, r, cols), BF16),
        compiler_params=_params("arbitrary", "arbitrary"),
    )(c, own, got)


def _adamw(w, g, m, v):
    m = ADAM_B1 * m + (1.0 - ADAM_B1) * g
    v = ADAM_B2 * v + (1.0 - ADAM_B2) * (g * g)
    m_hat = m / (1.0 - ADAM_B1 ** ADAM_STEP)
    v_hat = v / (1.0 - ADAM_B2 ** ADAM_STEP)
    delta = -ADAM_LR * (m_hat / (jnp.sqrt(v_hat) + ADAM_EPS) + ADAM_WD * w)
    return delta, m, v


def _adamw_big(sums, gots, chip, w, m, v, name):
    depth, r, cols = w.shape
    tr = _tile(r, 256)
    nt = r // tr

    def body(chip_ref, *refs):
        sum_refs, got_refs = refs[:depth], refs[depth:2 * depth]
        w_ref, m_ref, v_ref, g_out, d_out, m_out, v_out = refs[2 * depth:]
        l = pl.program_id(0)
        for layer in range(depth):
            @pl.when(l == layer)
            def _(layer=layer):
                got = got_refs[layer]
                g = sum_refs[layer][...].astype(F32) + got[0].astype(F32) + got[1].astype(F32) + got[2].astype(F32)
                delta, mm, vv = _adamw(w_ref[...], g, m_ref[...], v_ref[...])
                g_out[...] = g
                d_out[...] = delta
                m_out[...] = mm
                v_out[...] = vv

    def hold(layer):
        return lambda l, i: jnp.where(l == layer, i, jnp.where(l < layer, 0, nt - 1))

    shard = pl.BlockSpec((None, tr, cols), lambda l, i, chip_ref: (l, i, 0))
    return pl.pallas_call(
        body, name=name,
        grid_spec=pltpu.PrefetchScalarGridSpec(
            num_scalar_prefetch=1, grid=(depth, nt),
            in_specs=[pl.BlockSpec((None, tr, cols), lambda l, i, chip_ref, layer=layer: (chip_ref[0], hold(layer)(l, i), 0))
                      for layer in range(depth)]
            + [pl.BlockSpec((3, tr, cols), lambda l, i, chip_ref, layer=layer: (0, hold(layer)(l, i), 0)) for layer in range(depth)]
            + [shard] * 3,
            out_specs=[shard] * 4),
        out_shape=[SDS(w.shape, F32)] * 4,
        compiler_params=_params("arbitrary", "arbitrary"),
    )(chip, *sums, *gots, w, m, v)


def _sum_devices(stacked):
    _, r, _ = stacked.shape

    def body(s_ref, o_ref):
        acc = s_ref[0]
        for k in range(1, N_DEV):
            acc = acc + s_ref[k]
        o_ref[...] = acc

    return pl.pallas_call(
        body, name="sum_devices", grid=(1,),
        in_specs=[pl.BlockSpec((N_DEV, r, BLK), lambda i: (0, 0, 0))],
        out_specs=pl.BlockSpec((r, BLK), lambda i: (0, 0)),
        out_shape=SDS((r, BLK), F32),
        compiler_params=_params("arbitrary"),
    )(stacked)


def _adamw_small(g, w, m, v):
    def body(g_ref, w_ref, m_ref, v_ref, d_out, m_out, v_out):
        delta, mm, vv = _adamw(w_ref[...], g_ref[...], m_ref[...], v_ref[...])
        d_out[...] = delta
        m_out[...] = mm
        v_out[...] = vv

    spec = pl.BlockSpec(g.shape, lambda i: (0, 0))
    return pl.pallas_call(
        body, name="adamw_small", grid=(1,),
        in_specs=[spec] * 4, out_specs=[spec] * 3,
        out_shape=[SDS(g.shape, F32)] * 3,
        compiler_params=_params("arbitrary"),
    )(g, w, m, v)


def _pack(arrays):
    flat = jnp.concatenate([a.reshape(-1) for a in arrays])
    rows = -(-flat.shape[0] // (8 * BLK)) * 8
    return jnp.pad(flat, (0, rows * BLK - flat.shape[0])).reshape(rows, BLK)


def _unpack(packed, like):
    flat = packed.reshape(-1)
    out, off = [], 0
    for a in like:
        out.append(flat[off:off + a.size].reshape(a.shape))
        off += a.size
    return out


SMALL = ("ln1_g", "q_norm_g", "k_norm_g", "sinks", "conv_b", "conv_ln_g", "conv_ln_b", "sgu_ln_g", "sgu_ln_b",
         "sgu_w", "sgu_b", "out_norm_g", "ln2_g")
ORDER = ("ln1_g", "w_in", "q_norm_g", "k_norm_g", "sinks", "conv_w", "conv_b", "conv_ln_g", "conv_ln_b", "sgu_ln_g",
         "sgu_ln_b", "sgu_w", "sgu_b", "out_norm_g", "w_out", "ln2_g", "w_up", "w_down")


def _step(x, target, w, m, v):
    depth = w["ln1_g"].shape[0]
    xpos, ypos, cpos = _mesh_pos()
    me = 4 * xpos + 2 * ypos + cpos
    c_arr = jnp.reshape(cpos, (1,)).astype(jnp.int32)
    chip_arr = jnp.reshape(2 * xpos + ypos, (1,)).astype(jnp.int32)

    d = x.shape[1]
    def own_block(shard):
        return lax.dynamic_update_slice_in_dim(lax.empty((N_DEV,) + shard.shape, shard.dtype), shard[None], me, axis=0)

    cw = w["conv_w"]
    order = [(0, "conv_w")] + [(l, n) for l in range(depth) for n in BIG]
    started, gather_token = _start_exchanges("gather_start", [
        ([own_block(_pack([cw]) if n == "conv_w" else w[n][l].astype(BF16))], _gather_plan, 4) for l, n in order])
    pending = dict(zip(order, started))

    def fetch(l, n, after):
        arrays, sends, recvs = pending.pop((l, n))
        land, = _wait_exchange(f"gather_wait_{l}_{n}", arrays, sends, recvs, _gather_plan,
                               gather_token if after is None else after)
        full = _gather_finish(land, "gather_finish_" + n)
        if n == "w_in":
            return jnp.transpose(full, (1, 0, 2)).reshape(d, -1)
        return full if n in ("w_up", "conv_w") else full.reshape(-1, d)

    cw_all = fetch(0, "conv_w", None)
    cw_full = jnp.concatenate([_unpack(cw_all[k], [cw])[0] for k in range(N_DEV)], axis=-1)
    small = {n: w[n] for n in SMALL}
    small["conv_w"] = cw_full

    inflight = []

    def reduce(l, n, g):
        if n == "w_in":
            blocks = jnp.transpose(g.reshape(d, N_DEV, -1), (1, 0, 2))
        else:
            blocks = g if n == "w_up" else g.reshape(N_DEV, -1, d)
        from_sibling = _exchange_pair([blocks], "reduce_pair_" + n)[0]
        chip_sums = _pair_add(blocks, from_sibling, c_arr, "pair_add_" + n)
        (going,), token = _start_exchanges(f"reduce_start_{l}_{n}", [
            ([chip_sums, lax.empty((3,) + chip_sums.shape[1:], BF16)], _chips_plan, 3)])
        inflight.append((l, n, going))
        return token

    loss, grad_x, gs, after = _local_step(x, target, small, depth, fetch, reduce, None)

    sums, gots = {n: [None] * depth for n in BIG}, {n: [None] * depth for n in BIG}
    for l, n, (arrays, sends, recvs) in inflight:
        sums[n][l], gots[n][l] = _wait_exchange(f"reduce_wait_{l}_{n}", arrays, sends, recvs, _chips_plan, after)
        after = gots[n][l]

    out = {}
    for n in BIG:
        shape = w[n].shape
        as3d = lambda a: a.reshape(depth, -1, a.shape[-1])
        res = _adamw_big(sums[n], gots[n], chip_arr, as3d(w[n]), as3d(m[n]), as3d(v[n]), "adamw_" + n)
        out[n] = [r.reshape(shape) for r in res]

    names = SMALL + ("conv_w",)
    g_small = [jnp.stack([gs[l][n] for l in range(depth)]) for n in names]
    g_sum = _unpack(_sum_devices(_all_gather([_pack(g_small)], "gather_small_grads")[0]), g_small)
    cshard = cw.shape[-1]
    g_sum[-1] = lax.dynamic_slice_in_dim(g_sum[-1], me * cshard, cshard, axis=2)
    like = [w[n] for n in names]
    res = _adamw_small(_pack(g_sum), _pack(like), _pack([m[n] for n in names]), _pack([v[n] for n in names]))
    res = [_unpack(r, like) for r in res]
    for i, n in enumerate(names):
        out[n] = [g_sum[i], res[0][i], res[1][i], res[2][i]]

    loss = lax.psum(loss, ("x", "y", "c"))
    return (loss, grad_x[None]) + tuple(out[n][k] for k in range(4) for n in ORDER)


def kernel(x, ln1_g, w_in, q_norm_g, k_norm_g, sinks, conv_w, conv_b, conv_ln_g, conv_ln_b, sgu_ln_g, sgu_ln_b, sgu_w, sgu_b, out_norm_g, w_out, ln2_g, w_up, w_down, loss_target, m_ln1_g, m_w_in, m_q_norm_g, m_k_norm_g, m_sinks, m_conv_w, m_conv_b, m_conv_ln_g, m_conv_ln_b, m_sgu_ln_g, m_sgu_ln_b, m_sgu_w, m_sgu_b, m_out_norm_g, m_w_out, m_ln2_g, m_w_up, m_w_down, v_ln1_g, v_w_in, v_q_norm_g, v_k_norm_g, v_sinks, v_conv_w, v_conv_b, v_conv_ln_g, v_conv_ln_b, v_sgu_ln_g, v_sgu_ln_b, v_sgu_w, v_sgu_b, v_out_norm_g, v_w_out, v_ln2_g, v_w_up, v_w_down):
    w = dict(zip(ORDER, (ln1_g, w_in, q_norm_g, k_norm_g, sinks, conv_w, conv_b, conv_ln_g, conv_ln_b, sgu_ln_g, sgu_ln_b,
                         sgu_w, sgu_b, out_norm_g, w_out, ln2_g, w_up, w_down)))
    m = dict(zip(ORDER, (m_ln1_g, m_w_in, m_q_norm_g, m_k_norm_g, m_sinks, m_conv_w, m_conv_b, m_conv_ln_g, m_conv_ln_b,
                         m_sgu_ln_g, m_sgu_ln_b, m_sgu_w, m_sgu_b, m_out_norm_g, m_w_out, m_ln2_g, m_w_up, m_w_down)))
    v = dict(zip(ORDER, (v_ln1_g, v_w_in, v_q_norm_g, v_k_norm_g, v_sinks, v_conv_w, v_conv_b, v_conv_ln_g, v_conv_ln_b,
                         v_sgu_ln_g, v_sgu_ln_b, v_sgu_w, v_sgu_b, v_out_norm_g, v_w_out, v_ln2_g, v_w_up, v_w_down)))
    return _step(x[0], loss_target[0], w, m, v)
```

```python
import functools

import jax
import jax.numpy as jnp
from jax import lax
from jax.experimental import pallas as pl
from jax.experimental.pallas import tpu as pltpu

F32 = jnp.float32
BF16 = jnp.bfloat16
SDS = jax.ShapeDtypeStruct

EPS = 1e-6
NEG_INF = -1e30
HEAD = 64
BLK = 128
CONV_K = 31
HALO = 32
N_DEV = 8

ADAM_LR = 0.001
ADAM_B1 = 0.9
ADAM_B2 = 0.999
ADAM_EPS = 1e-08
ADAM_WD = 0.01
ADAM_STEP = 10

VMEM_LIMIT = 56 * 1024 * 1024

MESH = pl.DeviceIdType.MESH


def _params(*sem):
    return pltpu.CompilerParams(dimension_semantics=sem, vmem_limit_bytes=VMEM_LIMIT)


def _nt(a, b):
    return lax.dot_general(a, b, (((1,), (1,)), ((), ())), preferred_element_type=F32)


def _tn(a, b):
    return lax.dot_general(a, b, (((0,), (0,)), ((), ())), preferred_element_type=F32)


def _nn(a, b):
    return jnp.dot(a, b, preferred_element_type=F32)


def _sigmoid(x):
    return 1.0 / (1.0 + jnp.exp(-x))


ANY = pl.BlockSpec(memory_space=pl.ANY)


def _ordered_behind(body, n_in, in_specs, args, after):
    if after is None:
        return body, in_specs, args
    return (lambda *refs: body(*refs[:n_in], *refs[n_in + 1:])), list(in_specs) + [ANY], list(args) + [after]


def _seg_ones():
    r = lax.broadcasted_iota(jnp.int32, (BLK, BLK), 0) // HEAD
    c = lax.broadcasted_iota(jnp.int32, (BLK, BLK), 1) // HEAD
    return (r == c).astype(BF16)


def _segsum(x, ones):
    hi = x.astype(BF16)
    r1 = x - hi.astype(F32)
    mid = r1.astype(BF16)
    lo = (r1 - mid.astype(F32)).astype(BF16)
    return _nn(hi, ones) + _nn(mid, ones) + _nn(lo, ones)


def _head_rms(x, gain, ones):
    rstd = lax.rsqrt(_segsum(x * x, ones) * (1.0 / HEAD) + EPS)
    xhat = x * rstd
    return xhat * gain, xhat, rstd


def _expand(x, odd, lo):
    if odd:
        xl = pltpu.roll(jnp.where(lo, 0.0, x), HEAD, axis=1)
    else:
        xl = jnp.where(lo, x, 0.0)
    xh = pltpu.roll(xl, HEAD, axis=1)
    return jnp.concatenate([xl, xh], axis=0).astype(BF16)


def _fold(g2, odd, lo):
    r = g2.shape[0] // 2
    s = jnp.where(lo, g2[:r], 0.0) + pltpu.roll(jnp.where(lo, 0.0, g2[r:]), HEAD, axis=1)
    if odd:
        s = pltpu.roll(s, HEAD, axis=1)
    return s


def _attn_mask(n):
    qi = lax.broadcasted_iota(jnp.int32, (BLK, 2 * BLK), 0)
    sj = lax.broadcasted_iota(jnp.int32, (BLK, 2 * BLK), 1)
    rel = qi + BLK - sj
    return (rel >= 0) & (rel < BLK) & ((sj >= BLK) | (n > 0))


def _attn_specs(t, aw, kv):
    prev = lambda n: jnp.maximum(n - 1, 0)
    kb, vb = aw // kv, aw // kv + 1
    return [
        pl.BlockSpec(memory_space=pltpu.SMEM),
        pl.BlockSpec((BLK, aw), lambda n: (n, 0)),
        pl.BlockSpec((BLK, kv), lambda n: (prev(n), kb)),
        pl.BlockSpec((BLK, kv), lambda n: (n, kb)),
        pl.BlockSpec((BLK, kv), lambda n: (prev(n), vb)),
        pl.BlockSpec((BLK, kv), lambda n: (n, vb)),
        pl.BlockSpec((1, BLK), lambda n: (0, 0)),
        pl.BlockSpec((1, BLK), lambda n: (0, 0)),
    ]


def _softmax_pair(s2, valid, sink0, sink1):
    out, psink = [], []
    for half, sink in ((0, sink0), (1, sink1)):
        s = jnp.where(valid, s2[:, 2 * BLK * half:2 * BLK * (half + 1)], NEG_INF)
        m = jnp.maximum(jnp.max(s, axis=-1, keepdims=True), sink)
        p = jnp.exp(s - m)
        es = jnp.exp(sink - m)
        den = jnp.sum(p, axis=-1, keepdims=True) + es
        out.append(p / den)
        psink.append(es / den)
    return jnp.concatenate(out, axis=1), psink


def _attn_fwd(proj, qg, kg, sinks, aw, kv):
    t = proj.shape[0]
    n_pairs, n_kvblk = aw // BLK, kv // BLK

    def body(sink_ref, q_ref, kp_ref, kc_ref, vp_ref, vc_ref, qg_ref, kg_ref, o_ref):
        n = pl.program_id(0)
        ones = _seg_ones()
        lo = lax.broadcasted_iota(jnp.int32, (1, BLK), 1) < HEAD
        valid = _attn_mask(n)
        kraw = jnp.concatenate([kp_ref[...], kc_ref[...]], axis=0)
        vraw = jnp.concatenate([vp_ref[...], vc_ref[...]], axis=0)
        for b in range(n_kvblk):
            kn = _head_rms(kraw[:, BLK * b:BLK * (b + 1)], kg_ref[...], ones)[0]
            vb = vraw[:, BLK * b:BLK * (b + 1)]
            for odd in (0, 1):
                j = 2 * b + odd
                k2 = _expand(kn, odd, lo)
                v2 = _expand(vb, odd, lo)
                for p in (2 * j, 2 * j + 1):
                    qn = _head_rms(q_ref[:, BLK * p:BLK * (p + 1)], qg_ref[...], ones)[0]
                    s2 = _nt(qn.astype(BF16), k2) * (HEAD ** -0.5)
                    p2, _ = _softmax_pair(s2, valid, sink_ref[2 * p], sink_ref[2 * p + 1])
                    o_ref[:, BLK * p:BLK * (p + 1)] = _nn(p2.astype(BF16), v2)

    del n_pairs
    return pl.pallas_call(
        body, name="attn_fwd", grid=(t // BLK,),
        in_specs=_attn_specs(t, aw, kv),
        out_specs=pl.BlockSpec((BLK, aw), lambda n: (n, 0)),
        out_shape=SDS((t, aw), F32),
        compiler_params=_params("arbitrary"),
    )(sinks, proj, proj, proj, proj, proj, qg, kg)


def _attn_bwd(proj, dy, qg, kg, sinks, aw, kv, after=None):
    t = proj.shape[0]
    nb = t // BLK
    n_kvblk = kv // BLK
    kb = aw // kv

    def body(sink_ref, q_ref, kp_ref, kc_ref, vp_ref, vc_ref, qg_ref, kg_ref, dy_ref, kall_ref,
             dqkv_ref, dstat_ref, dk_acc, dv_acc, dqg_acc):
        n = pl.program_id(0)
        ones = _seg_ones()
        lane = lax.broadcasted_iota(jnp.int32, (1, BLK), 1)
        lo = lane < HEAD
        valid = _attn_mask(n)

        @pl.when(n == 0)
        def _():
            dk_acc[...] = jnp.zeros_like(dk_acc)
            dv_acc[...] = jnp.zeros_like(dv_acc)
            dqg_acc[...] = jnp.zeros_like(dqg_acc)
            dstat_ref[...] = jnp.zeros_like(dstat_ref)

        kraw = jnp.concatenate([kp_ref[...], kc_ref[...]], axis=0)
        vraw = jnp.concatenate([vp_ref[...], vc_ref[...]], axis=0)
        row = pl.multiple_of(n * BLK, BLK)
        prow = pl.multiple_of(jnp.maximum(n - 1, 0) * BLK, BLK)
        dsink = jnp.zeros((1, BLK), F32)
        for b in range(n_kvblk):
            kn = _head_rms(kraw[:, BLK * b:BLK * (b + 1)], kg_ref[...], ones)[0]
            vb = vraw[:, BLK * b:BLK * (b + 1)]
            dkn = jnp.zeros((2 * BLK, BLK), F32)
            dvb = jnp.zeros((2 * BLK, BLK), F32)
            for odd in (0, 1):
                j = 2 * b + odd
                k2 = _expand(kn, odd, lo)
                v2 = _expand(vb, odd, lo)
                dk2 = jnp.zeros((4 * BLK, BLK), F32)
                dv2 = jnp.zeros((4 * BLK, BLK), F32)
                for p in (2 * j, 2 * j + 1):
                    cols = slice(BLK * p, BLK * (p + 1))
                    qn, qhat, rstd = _head_rms(q_ref[:, cols], qg_ref[...], ones)
                    qb = qn.astype(BF16)
                    s2 = _nt(qb, k2) * (HEAD ** -0.5)
                    p2, psink = _softmax_pair(s2, valid, sink_ref[2 * p], sink_ref[2 * p + 1])
                    dob = dy_ref[:, cols].astype(BF16)
                    dp2 = _nt(dob, v2)
                    ds = []
                    for half in (0, 1):
                        hs = slice(2 * BLK * half, 2 * BLK * (half + 1))
                        ph = p2[:, hs]
                        delta = jnp.sum(ph * dp2[:, hs], axis=-1, keepdims=True)
                        ds.append(ph * (dp2[:, hs] - delta))
                        dsk = -jnp.sum(psink[half] * delta, axis=0, keepdims=True)
                        dsink = dsink + jnp.where(lane == 2 * p + half, dsk, 0.0)
                    ds2 = (jnp.concatenate(ds, axis=1) * (HEAD ** -0.5)).astype(BF16)
                    dqn = _nn(ds2, k2)
                    dk2 = dk2 + _tn(ds2, qb)
                    dv2 = dv2 + _tn(p2.astype(BF16), dob)
                    dqhat = dqn * qg_ref[...]
                    proj_q = _segsum(dqhat * qhat, ones) * (1.0 / HEAD)
                    dqkv_ref[pl.ds(row, BLK), cols] = (rstd * (dqhat - qhat * proj_q)).astype(BF16)
                    dqg_acc[:, cols] += jnp.sum(dqn * qhat, axis=0, keepdims=True)
                dkn = dkn + _fold(dk2, odd, lo)
                dvb = dvb + _fold(dv2, odd, lo)
            kcols = slice(BLK * b, BLK * (b + 1))
            dk_acc[pl.ds(prow, BLK), kcols] += dkn[:BLK]
            dv_acc[pl.ds(prow, BLK), kcols] += dvb[:BLK]
            dk_acc[pl.ds(row, BLK), kcols] += dkn[BLK:]
            dv_acc[pl.ds(row, BLK), kcols] += dvb[BLK:]
        dstat_ref[2:3, :] += dsink

        @pl.when(n == nb - 1)
        def _():
            dqg = dqg_acc[:, 0:BLK]
            for p in range(1, aw // BLK):
                dqg = dqg + dqg_acc[:, BLK * p:BLK * (p + 1)]
            dstat_ref[0:1, :] = dqg + pltpu.roll(dqg, HEAD, axis=1)

            def kblock(i, dkg):
                r = pl.multiple_of(i * BLK, BLK)
                for b in range(n_kvblk):
                    kcols = slice(BLK * b, BLK * (b + 1))
                    _, khat, rstd = _head_rms(kall_ref[pl.ds(r, BLK), kcols], kg_ref[...], ones)
                    dkn = dk_acc[pl.ds(r, BLK), kcols]
                    dkhat = dkn * kg_ref[...]
                    proj_k = _segsum(dkhat * khat, ones) * (1.0 / HEAD)
                    dqkv_ref[pl.ds(r, BLK), aw + BLK * b:aw + BLK * (b + 1)] = (rstd * (dkhat - khat * proj_k)).astype(BF16)
                    dqkv_ref[pl.ds(r, BLK), aw + kv + BLK * b:aw + kv + BLK * (b + 1)] = dv_acc[pl.ds(r, BLK), kcols].astype(BF16)
                    dkg = dkg + jnp.sum(dkn * khat, axis=0, keepdims=True)
                return dkg

            dkg = lax.fori_loop(0, nb, kblock, jnp.zeros((1, BLK), F32))
            dstat_ref[1:2, :] = dkg + pltpu.roll(dkg, HEAD, axis=1)

    in_specs = _attn_specs(t, aw, kv) + [
        pl.BlockSpec((BLK, aw), lambda n: (n, 0)),
        pl.BlockSpec((t, kv), lambda n: (0, kb)),
    ]
    args = [sinks, proj, proj, proj, proj, proj, qg, kg, dy, proj]
    body, in_specs, args = _ordered_behind(body, len(args), in_specs, args, after)
    return pl.pallas_call(
        body, name="attn_bwd", grid=(nb,),
        in_specs=in_specs,
        out_specs=[pl.BlockSpec((t, aw + 2 * kv), lambda n: (0, 0)), pl.BlockSpec((8, BLK), lambda n: (0, 0))],
        out_shape=[SDS((t, aw + 2 * kv), BF16), SDS((8, BLK), F32)],
        scratch_shapes=[pltpu.VMEM((t, kv), F32), pltpu.VMEM((t, kv), F32), pltpu.VMEM((1, aw), F32)],
        compiler_params=_params("arbitrary"),
    )(*args)


def _conv_taps(win, w_ref, shift_of):
    rows = win.shape[0]
    acc = None
    for j in range(CONV_K):
        term = pltpu.roll(win, (rows - shift_of(j)) % rows, axis=0)[:BLK] * w_ref[j:j + 1, :]
        acc = term if acc is None else acc + term
    return acc


def _layer_norm_fwd(z):
    mu = jnp.mean(z, axis=-1, keepdims=True)
    zc = z - mu
    rstd = lax.rsqrt(jnp.mean(zc * zc, axis=-1, keepdims=True) + EPS)
    return zc * rstd, rstd


def _layer_norm_bwd(dy, yhat, rstd, g):
    dyh = dy * g
    return rstd * (dyh - jnp.mean(dyh, axis=-1, keepdims=True) - yhat * jnp.mean(dyh * yhat, axis=-1, keepdims=True))


def _conv_fill_glu(a_ref, g_ref, hpad, nb):
    hpad[0:HALO, :] = jnp.zeros((HALO, hpad.shape[1]), F32)

    def fill(i, c):
        r = pl.multiple_of(i * BLK, BLK)
        hpad[pl.ds(pl.multiple_of(r + HALO, HALO), BLK), :] = a_ref[pl.ds(r, BLK), :] * _sigmoid(g_ref[pl.ds(r, BLK), :])
        return c

    lax.fori_loop(0, nb, fill, 0)


def _conv_specs(t, cw, d_in):
    base = (d_in - 4 * cw) // cw
    vec = pl.BlockSpec((1, cw), lambda i: (0, 0))
    return [
        pl.BlockSpec((t, cw), lambda i: (0, base)),
        pl.BlockSpec((t, cw), lambda i: (0, base + 1)),
        pl.BlockSpec((HALO, cw), lambda i: (0, 0)),
        vec, vec, vec,
    ]


def _conv_fwd(proj, cw_pad, cb, lg, lb, cw):
    t, d_in = proj.shape
    nb = t // BLK

    def body(a_ref, g_ref, w_ref, b_ref, lg_ref, lb_ref, o_ref, hpad):
        _conv_fill_glu(a_ref, g_ref, hpad, nb)

        def blk(i, c):
            r = pl.multiple_of(i * BLK, BLK)
            z = _conv_taps(hpad[pl.ds(r, BLK + HALO), :], w_ref, lambda j: j + HALO - (CONV_K - 1)) + b_ref[...]
            yhat, _ = _layer_norm_fwd(z)
            y = yhat * lg_ref[...] + lb_ref[...]
            o_ref[pl.ds(r, BLK), :] = y * _sigmoid(y)
            return c

        lax.fori_loop(0, nb, blk, 0)

    return pl.pallas_call(
        body, name="conv_fwd", grid=(1,),
        in_specs=_conv_specs(t, cw, d_in),
        out_specs=pl.BlockSpec((t, cw), lambda i: (0, 0)),
        out_shape=SDS((t, cw), F32),
        scratch_shapes=[pltpu.VMEM((t + HALO, cw), F32)],
        compiler_params=_params("arbitrary"),
    )(proj, proj, cw_pad, cb, lg, lb)


def _conv_bwd(proj, dy, cw_pad, cb, lg, lb, cw):
    t, d_in = proj.shape
    nb = t // BLK

    def body(a_ref, g_ref, w_ref, b_ref, lg_ref, lb_ref, dy_ref, dc_ref, dw_ref, dvec_ref, hpad, dzpad, dwacc):
        _conv_fill_glu(a_ref, g_ref, hpad, nb)
        dzpad[t:t + HALO, :] = jnp.zeros((HALO, cw), F32)
        dwacc[...] = jnp.zeros_like(dwacc)

        def blk(i, carry):
            db, dlg, dlb = carry
            r = pl.multiple_of(i * BLK, BLK)
            win = hpad[pl.ds(r, BLK + HALO), :]
            z = _conv_taps(win, w_ref, lambda j: j + HALO - (CONV_K - 1)) + b_ref[...]
            yhat, rstd = _layer_norm_fwd(z)
            y = yhat * lg_ref[...] + lb_ref[...]
            sg = _sigmoid(y)
            dyl = dy_ref[pl.ds(r, BLK), :] * (sg * (1.0 + y * (1.0 - sg)))
            dz = _layer_norm_bwd(dyl, yhat, rstd, lg_ref[...])
            dzpad[pl.ds(r, BLK), :] = dz
            for j in range(CONV_K):
                sh = j + HALO - (CONV_K - 1)
                prod = dz * pltpu.roll(win, (BLK + HALO - sh) % (BLK + HALO), axis=0)[:BLK]
                dwacc[8 * j:8 * j + 8, :] += jnp.sum(prod.reshape(BLK // 8, 8, cw), axis=0)
            return (db + jnp.sum(dz, axis=0, keepdims=True),
                    dlg + jnp.sum(dyl * yhat, axis=0, keepdims=True),
                    dlb + jnp.sum(dyl, axis=0, keepdims=True))

        zero = jnp.zeros((1, cw), F32)
        db, dlg, dlb = lax.fori_loop(0, nb, blk, (zero, zero, zero))
        dvec_ref[...] = jnp.zeros_like(dvec_ref)
        dvec_ref[0:1, :] = db
        dvec_ref[1:2, :] = dlg
        dvec_ref[2:3, :] = dlb
        dw_ref[...] = jnp.sum(dwacc[...].reshape(HALO, 8, cw), axis=1)

        def blk2(i, c):
            r = pl.multiple_of(i * BLK, BLK)
            dh = _conv_taps(dzpad[pl.ds(r, BLK + HALO), :], w_ref, lambda j: CONV_K - 1 - j)
            a = a_ref[pl.ds(r, BLK), :]
            sg = _sigmoid(g_ref[pl.ds(r, BLK), :])
            dc_ref[pl.ds(r, BLK), 0:cw] = (dh * sg).astype(BF16)
            dc_ref[pl.ds(r, BLK), cw:2 * cw] = (dh * a * sg * (1.0 - sg)).astype(BF16)
            return c

        lax.fori_loop(0, nb, blk2, 0)

    return pl.pallas_call(
        body, name="conv_bwd", grid=(1,),
        in_specs=_conv_specs(t, cw, d_in) + [pl.BlockSpec((t, cw), lambda i: (0, 0))],
        out_specs=[pl.BlockSpec((t, 2 * cw), lambda i: (0, 0)), pl.BlockSpec((HALO, cw), lambda i: (0, 0)),
                   pl.BlockSpec((8, cw), lambda i: (0, 0))],
        out_shape=[SDS((t, 2 * cw), BF16), SDS((HALO, cw), F32), SDS((8, cw), F32)],
        scratch_shapes=[pltpu.VMEM((t + HALO, cw), F32), pltpu.VMEM((t + HALO, cw), F32), pltpu.VMEM((8 * HALO, cw), F32)],
        compiler_params=_params("arbitrary"),
    )(proj, proj, cw_pad, cb, lg, lb, dy)


def _tril_bf16(w):
    r = lax.broadcasted_iota(jnp.int32, (BLK, BLK), 0)
    c = lax.broadcasted_iota(jnp.int32, (BLK, BLK), 1)
    return jnp.where(r >= c, w, 0.0).astype(BF16)


def _sgu_specs(sw, d_in, heads):
    base = (d_in - 2 * sw) // sw
    vec = pl.BlockSpec((1, sw), lambda n: (0, 0))
    return [
        pl.BlockSpec((BLK, sw), lambda n: (n, base)),
        pl.BlockSpec((BLK, sw), lambda n: (n, base + 1)),
        vec, vec,
        pl.BlockSpec((heads, BLK, BLK), lambda n: (0, 0, 0)),
        pl.BlockSpec((BLK, sw), lambda n: (0, 0)),
    ]


def _sgu_mix(w_ref, vnb, heads, sw, transpose):
    head_of = lax.broadcasted_iota(jnp.int32, (1, sw), 1) // HEAD
    s = jnp.zeros((BLK, sw), F32)
    for h in range(heads):
        wt = _tril_bf16(w_ref[h])
        mixed = _tn(wt, vnb) if transpose else _nn(wt, vnb)
        s = jnp.where(head_of == h, mixed, s)
    return s


def _sgu_fwd(proj, lg, lb, w, bias_full, sw):
    t, d_in = proj.shape
    heads = sw // HEAD

    def body(u_ref, v_ref, lg_ref, lb_ref, w_ref, bias_ref, o_ref):
        vhat, _ = _layer_norm_fwd(v_ref[...])
        vn = (vhat * lg_ref[...] + lb_ref[...]).astype(BF16)
        s = _sgu_mix(w_ref, vn, heads, sw, False) + bias_ref[...]
        o_ref[...] = u_ref[...] * s

    return pl.pallas_call(
        body, name="sgu_fwd", grid=(t // BLK,),
        in_specs=_sgu_specs(sw, d_in, heads),
        out_specs=pl.BlockSpec((BLK, sw), lambda n: (n, 0)),
        out_shape=SDS((t, sw), F32),
        compiler_params=_params("arbitrary"),
    )(proj, proj, lg, lb, w, bias_full)


def _sgu_bwd(proj, dy, lg, lb, w, bias_full, sw):
    t, d_in = proj.shape
    heads = sw // HEAD
    nb = t // BLK

    def body(u_ref, v_ref, lg_ref, lb_ref, w_ref, bias_ref, dy_ref, ds_ref, dw_ref, db_ref, dvec_ref, dbfull):
        n = pl.program_id(0)

        @pl.when(n == 0)
        def _():
            dw_ref[...] = jnp.zeros_like(dw_ref)
            dvec_ref[...] = jnp.zeros_like(dvec_ref)
            dbfull[...] = jnp.zeros_like(dbfull)

        vhat, rstd = _layer_norm_fwd(v_ref[...])
        vn = (vhat * lg_ref[...] + lb_ref[...]).astype(BF16)
        s = _sgu_mix(w_ref, vn, heads, sw, False) + bias_ref[...]
        dy = dy_ref[...]
        ds_ref[:, 0:sw] = (dy * s).astype(BF16)
        dsv = dy * u_ref[...]
        dbfull[...] += dsv
        head_of = lax.broadcasted_iota(jnp.int32, (1, sw), 1) // HEAD
        r = lax.broadcasted_iota(jnp.int32, (BLK, BLK), 0)
        c = lax.broadcasted_iota(jnp.int32, (BLK, BLK), 1)
        dsb = dsv.astype(BF16)
        for h in range(heads):
            dsh = jnp.where(head_of == h, dsv, 0.0).astype(BF16)
            dw_ref[h] += jnp.where(r >= c, _nt(dsh, vn), 0.0)
        dvn = _sgu_mix(w_ref, dsb, heads, sw, True)
        dvec_ref[0:1, :] += jnp.sum(dvn * vhat, axis=0, keepdims=True)
        dvec_ref[1:2, :] += jnp.sum(dvn, axis=0, keepdims=True)
        ds_ref[:, sw:2 * sw] = _layer_norm_bwd(dvn, vhat, rstd, lg_ref[...]).astype(BF16)

        @pl.when(n == nb - 1)
        def _():
            sel = (lax.broadcasted_iota(jnp.int32, (sw, BLK), 0) // HEAD == lax.broadcasted_iota(jnp.int32, (sw, BLK), 1)).astype(BF16)
            x = dbfull[...]
            hi = x.astype(BF16)
            r1 = x - hi.astype(F32)
            mid = r1.astype(BF16)
            low = (r1 - mid.astype(F32)).astype(BF16)
            db_ref[...] = _nn(hi, sel) + _nn(mid, sel) + _nn(low, sel)

    return pl.pallas_call(
        body, name="sgu_bwd", grid=(nb,),
        in_specs=_sgu_specs(sw, d_in, heads) + [pl.BlockSpec((BLK, sw), lambda n: (n, 0))],
        out_specs=[pl.BlockSpec((BLK, 2 * sw), lambda n: (n, 0)), pl.BlockSpec((heads, BLK, BLK), lambda n: (0, 0, 0)),
                   pl.BlockSpec((BLK, BLK), lambda n: (0, 0)), pl.BlockSpec((8, sw), lambda n: (0, 0))],
        out_shape=[SDS((t, 2 * sw), BF16), SDS((heads, BLK, BLK), F32), SDS((BLK, BLK), F32), SDS((8, sw), F32)],
        scratch_shapes=[pltpu.VMEM((BLK, sw), F32)],
        compiler_params=_params("arbitrary"),
    )(proj, proj, lg, lb, w, bias_full, dy)


def _rms_fwd(x, g):
    return (x * lax.rsqrt(jnp.mean(x * x, axis=-1, keepdims=True) + EPS)) * g


def _rms_bwd(dh, x, g):
    rstd = lax.rsqrt(jnp.mean(x * x, axis=-1, keepdims=True) + EPS)
    xhat = x * rstd
    dgx = dh * g
    dx = rstd * (dgx - xhat * jnp.mean(dgx * xhat, axis=-1, keepdims=True))
    return dx, jnp.sum(dh * xhat, axis=0, keepdims=True)


def _rms_matmul(x, g, w, tm, tn, relu2, name, transposed=False):
    t, d = x.shape
    if w.ndim == 3:
        assert w.shape[2] == tn
        n = w.shape[0] * tn
        w_spec = pl.BlockSpec((None, d, tn), lambda i, j: (j, 0, 0))
    elif transposed:
        n = w.shape[0]
        w_spec = pl.BlockSpec((tn, d), lambda i, j: (j, 0))
    else:
        n = w.shape[1]
        w_spec = pl.BlockSpec((d, tn), lambda i, j: (0, j))

    def body(x_ref, g_ref, w_ref, h_ref, *outs):
        @pl.when(pl.program_id(1) == 0)
        def _():
            h_ref[...] = _rms_fwd(x_ref[...], g_ref[...]).astype(BF16)

        acc = _nt(h_ref[...], w_ref[...]) if transposed else _nn(h_ref[...], w_ref[...])
        if relu2:
            r = jnp.maximum(acc, 0.0)
            outs[0][...] = (r * r).astype(BF16)
            outs[1][...] = r.astype(BF16)
        else:
            outs[0][...] = acc

    tile = pl.BlockSpec((tm, tn), lambda i, j: (i, j))
    row = pl.BlockSpec((tm, d), lambda i, j: (i, 0))
    outs = [SDS((t, n), BF16), SDS((t, n), BF16)] if relu2 else [SDS((t, n), F32)]
    return pl.pallas_call(
        body, name=name, grid=(t // tm, n // tn),
        in_specs=[row, pl.BlockSpec((1, d), lambda i, j: (0, 0)), w_spec],
        out_specs=[row] + [tile] * len(outs),
        out_shape=[SDS((t, d), BF16)] + outs,
        compiler_params=_params("parallel", "arbitrary"),
    )(x, g, w)


def _group_rms_matmul(ys, g, w, res, tm, tn):
    t, d = res.shape
    widths = [y.shape[1] for y in ys]
    k = sum(widths)

    def body(*refs):
        y_refs, (g_ref, w_ref, res_ref, mix_ref, o_ref) = refs[:len(ys)], refs[len(ys):]

        @pl.when(pl.program_id(1) == 0)
        def _():
            c = 0
            for y_ref, wd in zip(y_refs, widths):
                mix_ref[:, c:c + wd] = _rms_fwd(y_ref[...], g_ref[:, c:c + wd]).astype(BF16)
                c += wd

        o_ref[...] = res_ref[...] + _nn(mix_ref[...], w_ref[...])

    tile = pl.BlockSpec((tm, tn), lambda i, j: (i, j))
    return pl.pallas_call(
        body, name="mix_out", grid=(t // tm, d // tn),
        in_specs=[pl.BlockSpec((tm, wd), lambda i, j: (i, 0)) for wd in widths] + [
            pl.BlockSpec((1, k), lambda i, j: (0, 0)), pl.BlockSpec((k, tn), lambda i, j: (0, j)), tile],
        out_specs=[pl.BlockSpec((tm, k), lambda i, j: (i, 0)), tile],
        out_shape=[SDS((t, k), BF16), SDS((t, d), F32)],
        compiler_params=_params("parallel", "arbitrary"),
    )(*ys, g, w, res)


def _matmul_res(a, w, res, tm, tn, tk):
    t, k = a.shape
    n = w.shape[1]
    nk = k // tk

    def body(a_ref, w_ref, res_ref, o_ref, acc):
        kk = pl.program_id(2)

        @pl.when(kk == 0)
        def _():
            acc[...] = res_ref[...]

        acc[...] += _nn(a_ref[...], w_ref[...])

        @pl.when(kk == nk - 1)
        def _():
            o_ref[...] = acc[...]

    tile = pl.BlockSpec((tm, tn), lambda i, j, kk: (i, j))
    return pl.pallas_call(
        body, name="mlp_down", grid=(t // tm, n // tn, nk),
        in_specs=[pl.BlockSpec((tm, tk), lambda i, j, kk: (i, kk)), pl.BlockSpec((tk, tn), lambda i, j, kk: (kk, j)), tile],
        out_specs=tile,
        out_shape=SDS((t, n), F32),
        scratch_shapes=[pltpu.VMEM((tm, tn), F32)],
        compiler_params=_params("parallel", "parallel", "arbitrary"),
    )(a, w, res)


def _loss_grad(y, target, tm):
    t, d = y.shape

    def body(y_ref, t_ref, dy_ref, l_ref):
        @pl.when(pl.program_id(0) == 0)
        def _():
            l_ref[...] = jnp.zeros_like(l_ref)

        err = y_ref[...] - t_ref[...]
        dy_ref[...] = err * (1.0 / d)
        per_row = jnp.mean(err * err, axis=-1, keepdims=True)
        l_ref[...] += jnp.sum(per_row, axis=0, keepdims=True) * 0.5

    row = pl.BlockSpec((tm, d), lambda i: (i, 0))
    return pl.pallas_call(
        body, name="loss_grad", grid=(t // tm,),
        in_specs=[row, row], out_specs=[row, pl.BlockSpec((8, BLK), lambda i: (0, 0))],
        out_shape=[SDS((t, d), F32), SDS((8, BLK), F32)],
        compiler_params=_params("arbitrary"),
    )(y, target)


def _mlp_dact(dx, w_down, r, tm, tn, after=None):
    t, d = dx.shape
    f = w_down.shape[0]

    def body(dx_ref, w_ref, r_ref, dxb_ref, dup_ref):
        @pl.when(pl.program_id(1) == 0)
        def _():
            dxb_ref[...] = dx_ref[...].astype(BF16)

        dup_ref[...] = (_nt(dxb_ref[...], w_ref[...]) * (2.0 * r_ref[...].astype(F32))).astype(BF16)

    row = pl.BlockSpec((tm, d), lambda i, j: (i, 0))
    tile = pl.BlockSpec((tm, tn), lambda i, j: (i, j))
    body, in_specs, args = _ordered_behind(
        body, 3, [row, pl.BlockSpec((tn, d), lambda i, j: (j, 0)), tile], [dx, w_down, r], after)
    return pl.pallas_call(
        body, name="mlp_dact", grid=(t // tm, f // tn),
        in_specs=in_specs,
        out_specs=[row, tile],
        out_shape=[SDS((t, d), BF16), SDS((t, f), BF16)],
        compiler_params=_params("parallel", "arbitrary"),
    )(*args)


def _grad_w(a, b, tm, tn, name, col_blocks=False, after=None, rows=None, into=None):
    t, m = a.shape
    n = b.shape[1]
    assert n % tn == 0 and m % tm == 0
    total, first = rows if rows else (m, 0)
    assert first % tm == 0
    first_tile = first // tm

    def body(a_ref, b_ref, *rest):
        rest[-1][...] = _tn(a_ref[...], b_ref[...]).astype(BF16)

    if col_blocks:
        out_spec = pl.BlockSpec((None, tm, tn), lambda i, j: (j, i, 0))
        out_shape = SDS((n // tn, m, tn), BF16)
    else:
        out_spec = pl.BlockSpec((tm, tn), lambda i, j: (first_tile + i, j))
        out_shape = SDS((total, n), BF16)
    in_specs = [pl.BlockSpec((t, tm), lambda i, j: (0, i)), pl.BlockSpec((t, tn), lambda i, j: (0, j))]
    args = [a, b]
    aliases = {}
    if into is not None:
        in_specs, args, aliases = in_specs + [ANY], args + [into], {2: 0}
    body, in_specs, args = _ordered_behind(body, len(args), in_specs, args, after)
    return pl.pallas_call(
        body, name=name, grid=(m // tm, n // tn),
        in_specs=in_specs, out_specs=out_spec, out_shape=out_shape, input_output_aliases=aliases,
        compiler_params=_params("parallel", "arbitrary"),
    )(*args)


def _mlp_dnorm(dup, w_up, x, g, dres, tm, after=None):
    t, f = dup.shape
    d = x.shape[1]
    nk, _, tk = w_up.shape

    def body(a_ref, w_ref, x_ref, g_ref, dres_ref, dx_ref, dg_ref, acc):
        i, kk = pl.program_id(0), pl.program_id(1)

        @pl.when((i == 0) & (kk == 0))
        def _():
            dg_ref[...] = jnp.zeros_like(dg_ref)

        @pl.when(kk == 0)
        def _():
            acc[...] = jnp.zeros_like(acc)

        acc[...] += _nt(a_ref[...], w_ref[...])

        @pl.when(kk == nk - 1)
        def _():
            dx, dg = _rms_bwd(acc[...], x_ref[...], g_ref[...])
            dx_ref[...] = dres_ref[...] + dx
            dg_ref[0:1, :] += dg

    row = pl.BlockSpec((tm, d), lambda i, kk: (i, 0))
    in_specs = [pl.BlockSpec((tm, tk), lambda i, kk: (i, kk)), pl.BlockSpec((None, d, tk), lambda i, kk: (kk, 0, 0)),
                row, pl.BlockSpec((1, d), lambda i, kk: (0, 0)), row]
    body, in_specs, args = _ordered_behind(body, 5, in_specs, [dup, w_up, x, g, dres], after)
    return pl.pallas_call(
        body, name="mlp_dnorm", grid=(t // tm, nk),
        in_specs=in_specs,
        out_specs=[row, pl.BlockSpec((8, d), lambda i, kk: (0, 0))],
        out_shape=[SDS((t, d), F32), SDS((8, d), F32)],
        scratch_shapes=[pltpu.VMEM((tm, d), F32)],
        compiler_params=_params("arbitrary", "arbitrary"),
    )(*args)


def _mix_dnorm(dx, w_out, ys, g, tm, after=None):
    t, d = dx.shape
    k = w_out.shape[0]
    widths = [y.shape[1] for y in ys]

    def body(dx_ref, w_ref, *refs):
        y_refs = refs[:len(ys)]
        g_ref, dxb_ref = refs[len(ys)], refs[len(ys) + 1]
        dy_refs = refs[len(ys) + 2:2 * len(ys) + 2]
        dg_ref = refs[-1]

        @pl.when(pl.program_id(0) == 0)
        def _():
            dg_ref[...] = jnp.zeros_like(dg_ref)

        dxb = dx_ref[...].astype(BF16)
        dxb_ref[...] = dxb
        dmix = _nt(dxb, w_ref[...])
        c = 0
        for y_ref, dy_ref, wd in zip(y_refs, dy_refs, widths):
            dy, dg = _rms_bwd(dmix[:, c:c + wd], y_ref[...], g_ref[:, c:c + wd])
            dy_ref[...] = dy
            dg_ref[0:1, c:c + wd] += dg
            c += wd

    row = pl.BlockSpec((tm, d), lambda i: (i, 0))
    yspecs = [pl.BlockSpec((tm, wd), lambda i: (i, 0)) for wd in widths]
    in_specs = [row, pl.BlockSpec((k, d), lambda i: (0, 0))] + yspecs + [pl.BlockSpec((1, k), lambda i: (0, 0))]
    body, in_specs, args = _ordered_behind(body, len(in_specs), in_specs, [dx, w_out, *ys, g], after)
    return pl.pallas_call(
        body, name="mix_dnorm", grid=(t // tm,),
        in_specs=in_specs,
        out_specs=[row] + yspecs + [pl.BlockSpec((8, k), lambda i: (0, 0))],
        out_shape=[SDS((t, d), BF16)] + [SDS((t, wd), F32) for wd in widths] + [SDS((8, k), F32)],
        compiler_params=_params("arbitrary"),
    )(*args)


def _in_dnorm(dps, w_in_t, x, g, dres, tm):
    t, d = x.shape
    widths = [p.shape[1] for p in dps]
    offs = [sum(widths[:p]) for p in range(len(dps))]
    n_in = w_in_t.shape[0]

    def body(*refs):
        p_refs = refs[:len(dps)]
        w_ref, x_ref, g_ref, dres_ref, dx_ref, dg_ref = refs[len(dps):]

        @pl.when(pl.program_id(0) == 0)
        def _():
            dg_ref[...] = jnp.zeros_like(dg_ref)

        acc = None
        for p_ref, off, wd in zip(p_refs, offs, widths):
            term = _nn(p_ref[...], w_ref[off:off + wd, :])
            acc = term if acc is None else acc + term
        dx, dg = _rms_bwd(acc, x_ref[...], g_ref[...])
        dx_ref[...] = dres_ref[...] + dx
        dg_ref[0:1, :] += dg

    row = pl.BlockSpec((tm, d), lambda i: (i, 0))
    return pl.pallas_call(
        body, name="in_dnorm", grid=(t // tm,),
        in_specs=[pl.BlockSpec((tm, wd), lambda i: (i, 0)) for wd in widths] + [
            pl.BlockSpec((n_in, d), lambda i: (0, 0)), row, pl.BlockSpec((1, d), lambda i: (0, 0)), row],
        out_specs=[row, pl.BlockSpec((8, d), lambda i: (0, 0))],
        out_shape=[SDS((t, d), F32), SDS((8, d), F32)],
        compiler_params=_params("arbitrary"),
    )(*dps, w_in_t, x, g, dres)


def _tile(n, want):
    return min(n, want)


def _row_tile(n, want):
    return max(k for k in range(8, min(n, want) + 1, 8) if n % k == 0)


def _layer_fwd(x, p, fetch, after):
    t, d = x.shape
    aw, kv, cw, sw = d // 2, d // 8, d // 4, d // 4
    tm = _tile(t, 512)
    w_in = fetch("w_in", after)
    h1, proj = _rms_matmul(x, p["ln1_g"], w_in, tm, 512 if w_in.shape[0] % 512 == 0 else 256, False, "in_proj",
                           transposed=True)
    y_attn = _attn_fwd(proj, p["qg"], p["kg"], p["sinks"], aw, kv)
    y_conv = _conv_fwd(proj, p["conv_w"], p["conv_b"], p["conv_ln_g"], p["conv_ln_b"], cw)
    y_sgu = _sgu_fwd(proj, p["sgu_ln_g"], p["sgu_ln_b"], p["sgu_w"], p["sgu_bias"], sw)
    ys = [y_attn, y_conv, y_sgu]
    w_out = fetch("w_out", y_sgu)
    mix, x1 = _group_rms_matmul(ys, p["out_norm_g"], w_out, x, tm, _tile(d, 1024))
    w_up = fetch("w_up", x1)
    h2, act, r = _rms_matmul(x1, p["ln2_g"], w_up, tm, w_up.shape[2], True, "mlp_up")
    w_down = fetch("w_down", act)
    x2 = _matmul_res(act, w_down, x1, tm, _tile(d, 1024), 1024)
    saved = dict(x=x, h1=h1, proj=proj, ys=ys, mix=mix, x1=x1, h2=h2, act=act, r=r,
                 w_in=w_in, w_out=w_out, w_up=w_up, w_down=w_down)
    return x2, saved


def _layer_bwd(dx2, p, s, start, finish, carry):
    t, d = dx2.shape
    aw, kv, cw, sw = d // 2, d // 8, d // 4, d // 4
    tm = _tile(t, 512)
    dx2b, dup = _mlp_dact(dx2, s["w_down"], s["r"], tm, 1024, carry[0] if carry else None)
    tok = carry[1](dx2b) if carry else None
    tok = start("w_down", _grad_w(s["act"], dx2b, 512, _tile(d, 2048), "grad_w_down", after=tok))
    dx1, d_ln2 = _mlp_dnorm(dup, s["w_up"], s["x1"], p["ln2_g"], dx2, tm, tok)
    tok = finish("w_down", dx1)
    tok = start("w_up", _grad_w(s["h2"], dup, 512, s["w_up"].shape[2], "grad_w_up", col_blocks=True, after=tok))
    dx1b, dya, dyc, dys, d_onorm = _mix_dnorm(dx1, s["w_out"], s["ys"], p["out_norm_g"], _tile(t, 256), tok)
    tok = finish("w_up", dx1b)
    tok = start("w_out", _grad_w(s["mix"], dx1b, 512, _tile(d, 2048), "grad_w_out", after=tok))
    dqkv, d_attn = _attn_bwd(s["proj"], dya, p["qg"], p["kg"], p["sinks"], aw, kv, tok)
    tok = finish("w_out", dqkv)
    dconv, d_cw, d_cvec = _conv_bwd(s["proj"], dyc, p["conv_w"], p["conv_b"], p["conv_ln_g"], p["conv_ln_b"], cw)
    dsgu, d_sw, d_sb, d_svec = _sgu_bwd(s["proj"], dys, p["sgu_ln_g"], p["sgu_ln_b"], p["sgu_w"], p["sgu_bias"], sw)
    dps = [dqkv, dconv, dsgu]
    dx, d_ln1 = _in_dnorm(dps, s["w_in"], s["x"], p["ln1_g"], dx1, _tile(t, 256))
    d_in, first, g_in = sum(dp.shape[1] for dp in dps), 0, None
    tm_in = 512 if all(dp.shape[1] % 512 == 0 for dp in dps) else 256
    for k, dp in enumerate(dps):
        g_in = _grad_w(dp, s["h1"], tm_in, _tile(d, 2048), f"grad_w_in_{k}", after=tok if k == 0 else None,
                       rows=(d_in, first), into=g_in)
        first += dp.shape[1]
    tok = start("w_in", g_in)
    carry = (tok, functools.partial(finish, "w_in"))
    heads = sw // HEAD
    small = dict(
        ln1_g=d_ln1[0], q_norm_g=d_attn[0, :HEAD], k_norm_g=d_attn[1, :HEAD], sinks=d_attn[2, :aw // HEAD],
        conv_w=d_cw[:CONV_K], conv_b=d_cvec[0], conv_ln_g=d_cvec[1], conv_ln_b=d_cvec[2],
        sgu_ln_g=d_svec[0], sgu_ln_b=d_svec[1], sgu_w=d_sw, sgu_b=d_sb[:, :heads].T,
        out_norm_g=d_onorm[0], ln2_g=d_ln2[0])
    return dx, small, carry


def _layer_params(l, small):
    row = lambda v: v[l][None, :]
    two = lambda v: jnp.tile(v[l], 2)[None, :]
    return dict(
        ln1_g=row(small["ln1_g"]), ln2_g=row(small["ln2_g"]), out_norm_g=row(small["out_norm_g"]),
        qg=two(small["q_norm_g"]), kg=two(small["k_norm_g"]), sinks=small["sinks"][l],
        conv_w=jnp.pad(small["conv_w"][l], ((0, HALO - CONV_K), (0, 0))),
        conv_b=row(small["conv_b"]), conv_ln_g=row(small["conv_ln_g"]), conv_ln_b=row(small["conv_ln_b"]),
        sgu_ln_g=row(small["sgu_ln_g"]), sgu_ln_b=row(small["sgu_ln_b"]), sgu_w=small["sgu_w"][l],
        sgu_bias=jnp.repeat(small["sgu_b"][l].T, HEAD, axis=1),
    )


def _local_step(x, target, small, depth, fetch, start, finish, after):
    params = [_layer_params(l, small) for l in range(depth)]
    saved = []
    h = x
    for l in range(depth):
        h, s = _layer_fwd(h, params[l], functools.partial(fetch, l), after)
        after = h
        saved.append(s)
    dy, lsum = _loss_grad(h, target, _tile(x.shape[0], 512))
    gs = [None] * depth
    carry = None
    for l in reversed(range(depth)):
        dy, gs[l], carry = _layer_bwd(dy, params[l], saved[l], functools.partial(start, l), functools.partial(finish, l), carry)
    return lsum[0, 0], dy, gs, carry


BIG = ("w_in", "w_out", "w_up", "w_down")
HBM = pl.BlockSpec(memory_space=pltpu.HBM)
SEMS = pl.BlockSpec(memory_space=pltpu.SEMAPHORE)
EFFECT = pltpu.SideEffectType.DATAFLOW_SIDE_EFFECTING


def _mesh_pos():
    return lax.axis_index("x"), lax.axis_index("y"), lax.axis_index("c")


def _other_chips(x, y):
    return [(1 - x, y), (x, 1 - y), (1 - x, 1 - y)]


def _copies(plan, refs, sends, recvs):
    x, y, c = _mesh_pos()
    return [pltpu.make_async_remote_copy(src_ref=src, dst_ref=dst, send_sem=sends.at[k], recv_sem=recvs.at[k],
                                         device_id=dev, device_id_type=MESH)
            for k, (src, dst, dev) in enumerate(plan(x, y, c, refs))]


def _gather_plan(x, y, c, refs):
    mine = refs[0].at[4 * x + 2 * y + c]
    return [(mine, mine, (x, y, 1 - c))] + [(mine, mine, (*chip, c)) for chip in _other_chips(x, y)]


def _pair_plan(x, y, c, refs):
    blocks, land = refs
    return [(blocks.at[2 * q + (1 - c)], land.at[q], (x, y, 1 - c)) for q in range(4)]


def _chips_plan(x, y, c, refs):
    sums, land = refs
    return [(sums.at[2 * chip[0] + chip[1]], land.at[k], (*chip, c)) for k, chip in enumerate(_other_chips(x, y))]


def _start_exchanges(name, groups):
    flat = [a for arrays, _, _ in groups for a in arrays]
    n_arr, n_g = len(flat), len(groups)

    def body(*refs):
        ins, sems, token = refs[:n_arr], refs[n_arr:n_arr + 2 * n_g], refs[-1]
        off = 0
        for gi, (arrays, plan, _) in enumerate(groups):
            for cp in _copies(plan, ins[off:off + len(arrays)], sems[2 * gi], sems[2 * gi + 1]):
                cp.start()
            off += len(arrays)
        token[...] = jnp.zeros_like(token)

    res = pl.pallas_call(
        body, name=name,
        out_shape=[pltpu.SemaphoreType.DMA((n,)) for _, _, n in groups for _ in (0, 1)]
        + [pltpu.HBM(a.shape, a.dtype) for a in flat] + [SDS((8, BLK), F32)],
        in_specs=[HBM] * n_arr,
        out_specs=[SEMS] * (2 * n_g) + [HBM] * n_arr + [pl.BlockSpec(memory_space=pltpu.VMEM)],
        input_output_aliases={i: 2 * n_g + i for i in range(n_arr)},
        compiler_params=pltpu.CompilerParams(has_side_effects=EFFECT),
    )(*[pltpu.with_memory_space_constraint(a, pltpu.HBM) for a in flat])
    sems, thru, token = res[:2 * n_g], res[2 * n_g:2 * n_g + n_arr], res[-1]
    out, off = [], 0
    for gi, (arrays, _, _) in enumerate(groups):
        out.append((list(thru[off:off + len(arrays)]), sems[2 * gi], sems[2 * gi + 1]))
        off += len(arrays)
    return out, token


def _wait_exchange(name, arrays, sends, recvs, plan, after):
    n = len(arrays)

    def body(*refs):
        for cp in _copies(plan, refs[:n], refs[n], refs[n + 1]):
            cp.wait_send()
            cp.wait_recv()

    return pl.pallas_call(
        body, name=name,
        out_shape=[pltpu.HBM(a.shape, a.dtype) for a in arrays],
        in_specs=[HBM] * n + [SEMS, SEMS, ANY],
        out_specs=[HBM] * n,
        input_output_aliases={i: i for i in range(n)},
        compiler_params=pltpu.CompilerParams(has_side_effects=EFFECT),
    )(*arrays, sends, recvs, after)


def _gather_finish(land, name):
    def body(land_ref, out_ref, send_sems, recv_sems):
        del land_ref
        x, y, c = _mesh_pos()
        cps = []
        for k, chip in enumerate(_other_chips(x, y)):
            block = out_ref.at[4 * chip[0] + 2 * chip[1] + c]
            cps.append(pltpu.make_async_remote_copy(
                src_ref=block, dst_ref=block, send_sem=send_sems.at[k], recv_sem=recv_sems.at[k],
                device_id=(x, y, 1 - c), device_id_type=MESH))
        for cp in cps:
            cp.start()
        for cp in cps:
            cp.wait()

    return pl.pallas_call(
        body, name=name,
        in_specs=[ANY], out_specs=ANY,
        out_shape=SDS(land.shape, land.dtype),
        input_output_aliases={0: 0},
        scratch_shapes=[pltpu.SemaphoreType.DMA((3,)), pltpu.SemaphoreType.DMA((3,))],
    )(land)


def _all_gather(shards, name):
    n = len(shards)

    def body(*refs):
        ins, outs = refs[:n], refs[n:2 * n]
        send_sems, recv_sems, local_sems = refs[2 * n:]
        x, y, c = _mesh_pos()
        me, sibling = (x, y, c), (x, y, 1 - c)
        chips = _other_chips(x, y)

        def copy(a, k, block, to, src=None):
            dst = outs[a].at[4 * block[0] + 2 * block[1] + block[2]]
            return pltpu.make_async_remote_copy(
                src_ref=dst if src is None else src, dst_ref=dst,
                send_sem=send_sems.at[a, k], recv_sem=recv_sems.at[a, k], device_id=to, device_id_type=MESH)

        mine = [pltpu.make_async_copy(ins[a], outs[a].at[4 * x + 2 * y + c], local_sems.at[a]) for a in range(n)]
        for cp in mine:
            cp.start()
        first = []
        for a in range(n):
            first.append(copy(a, 0, me, sibling, src=ins[a]))
            first += [copy(a, 1 + j, me, (*chip, c), src=ins[a]) for j, chip in enumerate(chips)]
        for cp in first:
            cp.start()
        passed = []
        for j, chip in enumerate(chips):
            for a in range(n):
                copy(a, 1 + j, (*chip, c), me).wait_recv()
                fwd = copy(a, 4 + j, (*chip, c), sibling)
                fwd.start()
                passed.append(fwd)
        for a in range(n):
            copy(a, 0, sibling, me).wait_recv()
            for j, chip in enumerate(chips):
                copy(a, 4 + j, (*chip, 1 - c), me).wait_recv()
        for cp in first + passed:
            cp.wait_send()
        for cp in mine:
            cp.wait()

    outs = pl.pallas_call(
        body, name=name,
        in_specs=[ANY] * n, out_specs=[ANY] * n,
        out_shape=[SDS((N_DEV,) + s.shape, s.dtype) for s in shards],
        scratch_shapes=[pltpu.SemaphoreType.DMA((n, 7)), pltpu.SemaphoreType.DMA((n, 7)), pltpu.SemaphoreType.DMA((n,))],
    )(*shards)
    return list(outs)


def _pair_add(own, got, c, name):
    _, r, cols = own.shape
    tr = _row_tile(r, 512)

    def body(c_ref, own_ref, got_ref, o_ref):
        o_ref[...] = (own_ref[...].astype(F32) + got_ref[...].astype(F32)).astype(BF16)

    return pl.pallas_call(
        body, name=name,
        grid_spec=pltpu.PrefetchScalarGridSpec(
            num_scalar_prefetch=1, grid=(4, r // tr),
            in_specs=[pl.BlockSpec((None, tr, cols), lambda q, i, c_ref: (2 * q + c_ref[0], i, 0)),
                      pl.BlockSpec((None, tr, cols), lambda q, i, c_ref: (q, i, 0))],
            out_specs=pl.BlockSpec((None, tr, cols), lambda q, i, c_ref: (q, i, 0))),
        out_shape=SDS((4, r, cols), BF16),
        compiler_params=_params("arbitrary", "arbitrary"),
    )(c, own, got)


def _adamw(w, g, m, v):
    m = ADAM_B1 * m + (1.0 - ADAM_B1) * g
    v = ADAM_B2 * v + (1.0 - ADAM_B2) * (g * g)
    m_hat = m / (1.0 - ADAM_B1 ** ADAM_STEP)
    v_hat = v / (1.0 - ADAM_B2 ** ADAM_STEP)
    delta = -ADAM_LR * (m_hat / (jnp.sqrt(v_hat) + ADAM_EPS) + ADAM_WD * w)
    return delta, m, v


def _adamw_layer(chip_sum, got, chip, w, m, v, layer, prev, name):
    depth, r, cols = w.shape
    tr = _row_tile(r, 256)

    def body(chip_ref, sum_ref, got_ref, w_ref, m_ref, v_ref, *rest):
        g_out, d_out, m_out, v_out, token = rest[-5:]
        g = sum_ref[...].astype(F32) + got_ref[0].astype(F32) + got_ref[1].astype(F32) + got_ref[2].astype(F32)
        delta, mm, vv = _adamw(w_ref[...], g, m_ref[...], v_ref[...])
        g_out[...] = g
        d_out[...] = delta
        m_out[...] = mm
        v_out[...] = vv
        token[...] = jnp.zeros_like(token)

    shard = pl.BlockSpec((None, tr, cols), lambda i, chip_ref: (layer, i, 0))
    in_specs = [pl.BlockSpec((None, tr, cols), lambda i, chip_ref: (chip_ref[0], i, 0)),
                pl.BlockSpec((3, tr, cols), lambda i, chip_ref: (0, i, 0)), shard, shard, shard]
    args = [chip, chip_sum, got, w, m, v]
    aliases = {}
    if prev is not None:
        in_specs += [ANY] * 4
        aliases = {len(args) + k: k for k in range(4)}
        args += list(prev)
    res = pl.pallas_call(
        body, name=name,
        grid_spec=pltpu.PrefetchScalarGridSpec(
            num_scalar_prefetch=1, grid=(r // tr,), in_specs=in_specs,
            out_specs=[shard] * 4 + [pl.BlockSpec((8, BLK), lambda i, chip_ref: (0, 0))]),
        out_shape=[SDS(w.shape, F32)] * 4 + [SDS((8, BLK), F32)],
        input_output_aliases=aliases,
        compiler_params=_params("arbitrary"),
    )(*args)
    return list(res[:4]), res[4]


def _sum_devices(stacked):
    _, r, _ = stacked.shape

    def body(s_ref, o_ref):
        acc = s_ref[0]
        for k in range(1, N_DEV):
            acc = acc + s_ref[k]
        o_ref[...] = acc

    return pl.pallas_call(
        body, name="sum_devices", grid=(1,),
        in_specs=[pl.BlockSpec((N_DEV, r, BLK), lambda i: (0, 0, 0))],
        out_specs=pl.BlockSpec((r, BLK), lambda i: (0, 0)),
        out_shape=SDS((r, BLK), F32),
        compiler_params=_params("arbitrary"),
    )(stacked)


def _adamw_small(g, w, m, v):
    def body(g_ref, w_ref, m_ref, v_ref, d_out, m_out, v_out):
        delta, mm, vv = _adamw(w_ref[...], g_ref[...], m_ref[...], v_ref[...])
        d_out[...] = delta
        m_out[...] = mm
        v_out[...] = vv

    spec = pl.BlockSpec(g.shape, lambda i: (0, 0))
    return pl.pallas_call(
        body, name="adamw_small", grid=(1,),
        in_specs=[spec] * 4, out_specs=[spec] * 3,
        out_shape=[SDS(g.shape, F32)] * 3,
        compiler_params=_params("arbitrary"),
    )(g, w, m, v)


def _pack(arrays):
    flat = jnp.concatenate([a.reshape(-1) for a in arrays])
    rows = -(-flat.shape[0] // (8 * BLK)) * 8
    return jnp.pad(flat, (0, rows * BLK - flat.shape[0])).reshape(rows, BLK)


def _unpack(packed, like):
    flat = packed.reshape(-1)
    out, off = [], 0
    for a in like:
        out.append(flat[off:off + a.size].reshape(a.shape))
        off += a.size
    return out


SMALL = ("ln1_g", "q_norm_g", "k_norm_g", "sinks", "conv_b", "conv_ln_g", "conv_ln_b", "sgu_ln_g", "sgu_ln_b",
         "sgu_w", "sgu_b", "out_norm_g", "ln2_g")
ORDER = ("ln1_g", "w_in", "q_norm_g", "k_norm_g", "sinks", "conv_w", "conv_b", "conv_ln_g", "conv_ln_b", "sgu_ln_g",
         "sgu_ln_b", "sgu_w", "sgu_b", "out_norm_g", "w_out", "ln2_g", "w_up", "w_down")


def _step(x, target, w, m, v):
    depth = w["ln1_g"].shape[0]
    xpos, ypos, cpos = _mesh_pos()
    me = 4 * xpos + 2 * ypos + cpos
    c_arr = jnp.reshape(cpos, (1,)).astype(jnp.int32)
    chip_arr = jnp.reshape(2 * xpos + ypos, (1,)).astype(jnp.int32)

    d = x.shape[1]
    def own_block(shard):
        return lax.dynamic_update_slice_in_dim(lax.empty((N_DEV,) + shard.shape, shard.dtype), shard[None], me, axis=0)

    cw = w["conv_w"]
    order = [(0, "conv_w")] + [(l, n) for l in range(depth) for n in BIG]
    started, gather_token = _start_exchanges("gather_start", [
        ([own_block(_pack([cw]) if n == "conv_w" else w[n][l].astype(BF16))], _gather_plan, 4) for l, n in order])
    pending = dict(zip(order, started))

    def fetch(l, n, after):
        arrays, sends, recvs = pending.pop((l, n))
        land, = _wait_exchange(f"gather_wait_{l}_{n}", arrays, sends, recvs, _gather_plan,
                               gather_token if after is None else after)
        full = _gather_finish(land, "gather_finish_" + n)
        return full if n in ("w_up", "conv_w") else full.reshape(-1, d)

    cw_all = fetch(0, "conv_w", None)
    cw_full = jnp.concatenate([_unpack(cw_all[k], [cw])[0] for k in range(N_DEV)], axis=-1)
    small = {n: w[n] for n in SMALL}
    small["conv_w"] = cw_full

    to_sibling, inflight = {}, []

    def start(l, n, g):
        blocks = g if n == "w_up" else g.reshape(N_DEV, -1, d)
        (to_sibling[l, n],), token = _start_exchanges(f"pair_start_{l}_{n}", [
            ([blocks, lax.empty((4,) + blocks.shape[1:], BF16)], _pair_plan, 4)])
        return token

    def finish(l, n, after):
        arrays, sends, recvs = to_sibling.pop((l, n))
        blocks, from_sibling = _wait_exchange(f"pair_wait_{l}_{n}", arrays, sends, recvs, _pair_plan, after)
        chip_sums = _pair_add(blocks, from_sibling, c_arr, "pair_add_" + n)
        (going,), token = _start_exchanges(f"reduce_start_{l}_{n}", [
            ([chip_sums, lax.empty((3,) + chip_sums.shape[1:], BF16)], _chips_plan, 3)])
        inflight.append((l, n, going))
        return token

    loss, grad_x, gs, (after, finish_last) = _local_step(x, target, small, depth, fetch, start, finish, None)

    as3d = lambda a: a.reshape(depth, -1, a.shape[-1])
    results = {n: None for n in BIG}

    def settle(l, n, going, after):
        arrays, sends, recvs = going
        chip_sums, got = _wait_exchange(f"reduce_wait_{l}_{n}", arrays, sends, recvs, _chips_plan, after)
        results[n], token = _adamw_layer(chip_sums, got, chip_arr, as3d(w[n]), as3d(m[n]), as3d(v[n]), l, results[n],
                                         f"adamw_{l}_{n}")
        return token

    early, late = inflight[:2], inflight[2:]
    for l, n, going in early:
        after = settle(l, n, going, after)
    after = finish_last(after)
    for l, n, going in late + inflight[-1:]:
        after = settle(l, n, going, after)

    out = {n: [r.reshape(w[n].shape) for r in results[n]] for n in BIG}

    names = SMALL + ("conv_w",)
    g_small = [jnp.stack([gs[l][n] for l in range(depth)]) for n in names]
    g_sum = _unpack(_sum_devices(_all_gather([_pack(g_small)], "gather_small_grads")[0]), g_small)
    cshard = cw.shape[-1]
    g_sum[-1] = lax.dynamic_slice_in_dim(g_sum[-1], me * cshard, cshard, axis=2)
    like = [w[n] for n in names]
    res = _adamw_small(_pack(g_sum), _pack(like), _pack([m[n] for n in names]), _pack([v[n] for n in names]))
    res = [_unpack(r, like) for r in res]
    for i, n in enumerate(names):
        out[n] = [g_sum[i], res[0][i], res[1][i], res[2][i]]

    loss = lax.psum(loss, ("x", "y", "c"))
    return (loss, grad_x[None]) + tuple(out[n][k] for k in range(4) for n in ORDER)


def kernel(x, ln1_g, w_in, q_norm_g, k_norm_g, sinks, conv_w, conv_b, conv_ln_g, conv_ln_b, sgu_ln_g, sgu_ln_b, sgu_w, sgu_b, out_norm_g, w_out, ln2_g, w_up, w_down, loss_target, m_ln1_g, m_w_in, m_q_norm_g, m_k_norm_g, m_sinks, m_conv_w, m_conv_b, m_conv_ln_g, m_conv_ln_b, m_sgu_ln_g, m_sgu_ln_b, m_sgu_w, m_sgu_b, m_out_norm_g, m_w_out, m_ln2_g, m_w_up, m_w_down, v_ln1_g, v_w_in, v_q_norm_g, v_k_norm_g, v_sinks, v_conv_w, v_conv_b, v_conv_ln_g, v_conv_ln_b, v_sgu_ln_g, v_sgu_ln_b, v_sgu_w, v_sgu_b, v_out_norm_g, v_w_out, v_ln2_g, v_w_up, v_w_down):
    w = dict(zip(ORDER, (ln1_g, w_in, q_norm_g, k_norm_g, sinks, conv_w, conv_b, conv_ln_g, conv_ln_b, sgu_ln_g, sgu_ln_b,
                         sgu_w, sgu_b, out_norm_g, w_out, ln2_g, w_up, w_down)))
    m = dict(zip(ORDER, (m_ln1_g, m_w_in, m_q_norm_g, m_k_norm_g, m_sinks, m_conv_w, m_conv_b, m_conv_ln_g, m_conv_ln_b,
                         m_sgu_ln_g, m_sgu_ln_b, m_sgu_w, m_sgu_b, m_out_norm_g, m_w_out, m_ln2_g, m_w_up, m_w_down)))
    v = dict(zip(ORDER, (v_ln1_g, v_w_in, v_q_norm_g, v_k_norm_g, v_sinks, v_conv_w, v_conv_b, v_conv_ln_g, v_conv_ln_b,
                         v_sgu_ln_g, v_sgu_ln_b, v_sgu_w, v_sgu_b, v_out_norm_g, v_w_out, v_ln2_g, v_w_up, v_w_down)))
    for group in (w, m, v):
        group["w_in"] = jnp.swapaxes(group["w_in"], 1, 2)
    out = list(_step(x[0], loss_target[0], w, m, v))
    for k in range(4):
        i = 2 + k * len(ORDER) + ORDER.index("w_in")
        out[i] = jnp.swapaxes(out[i], 1, 2)
    return tuple(out)
```

```python
import functools

import jax
import jax.numpy as jnp
from jax import lax
from jax.experimental import pallas as pl
from jax.experimental.pallas import tpu as pltpu

F32 = jnp.float32
BF16 = jnp.bfloat16
SDS = jax.ShapeDtypeStruct

EPS = 1e-6
NEG_INF = -1e30
HEAD = 64
BLK = 128
CONV_K = 31
HALO = 32
N_DEV = 8

ADAM_LR = 0.001
ADAM_B1 = 0.9
ADAM_B2 = 0.999
ADAM_EPS = 1e-08
ADAM_WD = 0.01
ADAM_STEP = 10

VMEM_LIMIT = 56 * 1024 * 1024

MESH = pl.DeviceIdType.MESH


def _params(*sem):
    return pltpu.CompilerParams(dimension_semantics=sem, vmem_limit_bytes=VMEM_LIMIT)


def _nt(a, b):
    return lax.dot_general(a, b, (((1,), (1,)), ((), ())), preferred_element_type=F32)


def _tn(a, b):
    return lax.dot_general(a, b, (((0,), (0,)), ((), ())), preferred_element_type=F32)


def _nn(a, b):
    return jnp.dot(a, b, preferred_element_type=F32)


def _sigmoid(x):
    return 1.0 / (1.0 + jnp.exp(-x))


ANY = pl.BlockSpec(memory_space=pl.ANY)


def _ordered_behind(body, n_in, in_specs, args, after):
    if after is None:
        return body, in_specs, args
    return (lambda *refs: body(*refs[:n_in], *refs[n_in + 1:])), list(in_specs) + [ANY], list(args) + [after]


def _seg_ones():
    r = lax.broadcasted_iota(jnp.int32, (BLK, BLK), 0) // HEAD
    c = lax.broadcasted_iota(jnp.int32, (BLK, BLK), 1) // HEAD
    return (r == c).astype(BF16)


def _segsum(x, ones):
    hi = x.astype(BF16)
    r1 = x - hi.astype(F32)
    mid = r1.astype(BF16)
    lo = (r1 - mid.astype(F32)).astype(BF16)
    return _nn(hi, ones) + _nn(mid, ones) + _nn(lo, ones)


def _head_rms(x, gain, ones):
    rstd = lax.rsqrt(_segsum(x * x, ones) * (1.0 / HEAD) + EPS)
    xhat = x * rstd
    return xhat * gain, xhat, rstd


def _expand(x, odd, lo):
    if odd:
        xl = pltpu.roll(jnp.where(lo, 0.0, x), HEAD, axis=1)
    else:
        xl = jnp.where(lo, x, 0.0)
    xh = pltpu.roll(xl, HEAD, axis=1)
    return jnp.concatenate([xl, xh], axis=0).astype(BF16)


def _fold(g2, odd, lo):
    r = g2.shape[0] // 2
    s = jnp.where(lo, g2[:r], 0.0) + pltpu.roll(jnp.where(lo, 0.0, g2[r:]), HEAD, axis=1)
    if odd:
        s = pltpu.roll(s, HEAD, axis=1)
    return s


def _attn_mask(n):
    qi = lax.broadcasted_iota(jnp.int32, (BLK, 2 * BLK), 0)
    sj = lax.broadcasted_iota(jnp.int32, (BLK, 2 * BLK), 1)
    rel = qi + BLK - sj
    return (rel >= 0) & (rel < BLK) & ((sj >= BLK) | (n > 0))


def _attn_specs(t, aw, kv):
    prev = lambda n: jnp.maximum(n - 1, 0)
    kb, vb = aw // kv, aw // kv + 1
    return [
        pl.BlockSpec(memory_space=pltpu.SMEM),
        pl.BlockSpec((BLK, aw), lambda n: (n, 0)),
        pl.BlockSpec((BLK, kv), lambda n: (prev(n), kb)),
        pl.BlockSpec((BLK, kv), lambda n: (n, kb)),
        pl.BlockSpec((BLK, kv), lambda n: (prev(n), vb)),
        pl.BlockSpec((BLK, kv), lambda n: (n, vb)),
        pl.BlockSpec((1, BLK), lambda n: (0, 0)),
        pl.BlockSpec((1, BLK), lambda n: (0, 0)),
    ]


def _softmax_pair(s2, valid, sink0, sink1):
    out, psink = [], []
    for half, sink in ((0, sink0), (1, sink1)):
        s = jnp.where(valid, s2[:, 2 * BLK * half:2 * BLK * (half + 1)], NEG_INF)
        m = jnp.maximum(jnp.max(s, axis=-1, keepdims=True), sink)
        p = jnp.exp(s - m)
        es = jnp.exp(sink - m)
        den = jnp.sum(p, axis=-1, keepdims=True) + es
        out.append(p / den)
        psink.append(es / den)
    return jnp.concatenate(out, axis=1), psink


def _attn_fwd(proj, qg, kg, sinks, aw, kv):
    t = proj.shape[0]
    n_pairs, n_kvblk = aw // BLK, kv // BLK

    def body(sink_ref, q_ref, kp_ref, kc_ref, vp_ref, vc_ref, qg_ref, kg_ref, o_ref):
        n = pl.program_id(0)
        ones = _seg_ones()
        lo = lax.broadcasted_iota(jnp.int32, (1, BLK), 1) < HEAD
        valid = _attn_mask(n)
        kraw = jnp.concatenate([kp_ref[...], kc_ref[...]], axis=0)
        vraw = jnp.concatenate([vp_ref[...], vc_ref[...]], axis=0)
        for b in range(n_kvblk):
            kn = _head_rms(kraw[:, BLK * b:BLK * (b + 1)], kg_ref[...], ones)[0]
            vb = vraw[:, BLK * b:BLK * (b + 1)]
            for odd in (0, 1):
                j = 2 * b + odd
                k2 = _expand(kn, odd, lo)
                v2 = _expand(vb, odd, lo)
                for p in (2 * j, 2 * j + 1):
                    qn = _head_rms(q_ref[:, BLK * p:BLK * (p + 1)], qg_ref[...], ones)[0]
                    s2 = _nt(qn.astype(BF16), k2) * (HEAD ** -0.5)
                    p2, _ = _softmax_pair(s2, valid, sink_ref[2 * p], sink_ref[2 * p + 1])
                    o_ref[:, BLK * p:BLK * (p + 1)] = _nn(p2.astype(BF16), v2)

    del n_pairs
    return pl.pallas_call(
        body, name="attn_fwd", grid=(t // BLK,),
        in_specs=_attn_specs(t, aw, kv),
        out_specs=pl.BlockSpec((BLK, aw), lambda n: (n, 0)),
        out_shape=SDS((t, aw), F32),
        compiler_params=_params("arbitrary"),
    )(sinks, proj, proj, proj, proj, proj, qg, kg)


def _attn_bwd(proj, dy, qg, kg, sinks, aw, kv, after=None):
    t = proj.shape[0]
    nb = t // BLK
    n_kvblk = kv // BLK
    kb = aw // kv

    def body(sink_ref, q_ref, kp_ref, kc_ref, vp_ref, vc_ref, qg_ref, kg_ref, dy_ref, kall_ref,
             dqkv_ref, dstat_ref, dk_acc, dv_acc, dqg_acc):
        n = pl.program_id(0)
        ones = _seg_ones()
        lane = lax.broadcasted_iota(jnp.int32, (1, BLK), 1)
        lo = lane < HEAD
        valid = _attn_mask(n)

        @pl.when(n == 0)
        def _():
            dk_acc[...] = jnp.zeros_like(dk_acc)
            dv_acc[...] = jnp.zeros_like(dv_acc)
            dqg_acc[...] = jnp.zeros_like(dqg_acc)
            dstat_ref[...] = jnp.zeros_like(dstat_ref)

        kraw = jnp.concatenate([kp_ref[...], kc_ref[...]], axis=0)
        vraw = jnp.concatenate([vp_ref[...], vc_ref[...]], axis=0)
        row = pl.multiple_of(n * BLK, BLK)
        prow = pl.multiple_of(jnp.maximum(n - 1, 0) * BLK, BLK)
        dsink = jnp.zeros((1, BLK), F32)
        for b in range(n_kvblk):
            kn = _head_rms(kraw[:, BLK * b:BLK * (b + 1)], kg_ref[...], ones)[0]
            vb = vraw[:, BLK * b:BLK * (b + 1)]
            dkn = jnp.zeros((2 * BLK, BLK), F32)
            dvb = jnp.zeros((2 * BLK, BLK), F32)
            for odd in (0, 1):
                j = 2 * b + odd
                k2 = _expand(kn, odd, lo)
                v2 = _expand(vb, odd, lo)
                dk2 = jnp.zeros((4 * BLK, BLK), F32)
                dv2 = jnp.zeros((4 * BLK, BLK), F32)
                for p in (2 * j, 2 * j + 1):
                    cols = slice(BLK * p, BLK * (p + 1))
                    qn, qhat, rstd = _head_rms(q_ref[:, cols], qg_ref[...], ones)
                    qb = qn.astype(BF16)
                    s2 = _nt(qb, k2) * (HEAD ** -0.5)
                    p2, psink = _softmax_pair(s2, valid, sink_ref[2 * p], sink_ref[2 * p + 1])
                    dob = dy_ref[:, cols].astype(BF16)
                    dp2 = _nt(dob, v2)
                    ds = []
                    for half in (0, 1):
                        hs = slice(2 * BLK * half, 2 * BLK * (half + 1))
                        ph = p2[:, hs]
                        delta = jnp.sum(ph * dp2[:, hs], axis=-1, keepdims=True)
                        ds.append(ph * (dp2[:, hs] - delta))
                        dsk = -jnp.sum(psink[half] * delta, axis=0, keepdims=True)
                        dsink = dsink + jnp.where(lane == 2 * p + half, dsk, 0.0)
                    ds2 = (jnp.concatenate(ds, axis=1) * (HEAD ** -0.5)).astype(BF16)
                    dqn = _nn(ds2, k2)
                    dk2 = dk2 + _tn(ds2, qb)
                    dv2 = dv2 + _tn(p2.astype(BF16), dob)
                    dqhat = dqn * qg_ref[...]
                    proj_q = _segsum(dqhat * qhat, ones) * (1.0 / HEAD)
                    dqkv_ref[pl.ds(row, BLK), cols] = (rstd * (dqhat - qhat * proj_q)).astype(BF16)
                    dqg_acc[:, cols] += jnp.sum(dqn * qhat, axis=0, keepdims=True)
                dkn = dkn + _fold(dk2, odd, lo)
                dvb = dvb + _fold(dv2, odd, lo)
            kcols = slice(BLK * b, BLK * (b + 1))
            dk_acc[pl.ds(prow, BLK), kcols] += dkn[:BLK]
            dv_acc[pl.ds(prow, BLK), kcols] += dvb[:BLK]
            dk_acc[pl.ds(row, BLK), kcols] += dkn[BLK:]
            dv_acc[pl.ds(row, BLK), kcols] += dvb[BLK:]
        dstat_ref[2:3, :] += dsink

        @pl.when(n == nb - 1)
        def _():
            dqg = dqg_acc[:, 0:BLK]
            for p in range(1, aw // BLK):
                dqg = dqg + dqg_acc[:, BLK * p:BLK * (p + 1)]
            dstat_ref[0:1, :] = dqg + pltpu.roll(dqg, HEAD, axis=1)

            def kblock(i, dkg):
                r = pl.multiple_of(i * BLK, BLK)
                for b in range(n_kvblk):
                    kcols = slice(BLK * b, BLK * (b + 1))
                    _, khat, rstd = _head_rms(kall_ref[pl.ds(r, BLK), kcols], kg_ref[...], ones)
                    dkn = dk_acc[pl.ds(r, BLK), kcols]
                    dkhat = dkn * kg_ref[...]
                    proj_k = _segsum(dkhat * khat, ones) * (1.0 / HEAD)
                    dqkv_ref[pl.ds(r, BLK), aw + BLK * b:aw + BLK * (b + 1)] = (rstd * (dkhat - khat * proj_k)).astype(BF16)
                    dqkv_ref[pl.ds(r, BLK), aw + kv + BLK * b:aw + kv + BLK * (b + 1)] = dv_acc[pl.ds(r, BLK), kcols].astype(BF16)
                    dkg = dkg + jnp.sum(dkn * khat, axis=0, keepdims=True)
                return dkg

            dkg = lax.fori_loop(0, nb, kblock, jnp.zeros((1, BLK), F32))
            dstat_ref[1:2, :] = dkg + pltpu.roll(dkg, HEAD, axis=1)

    in_specs = _attn_specs(t, aw, kv) + [
        pl.BlockSpec((BLK, aw), lambda n: (n, 0)),
        pl.BlockSpec((t, kv), lambda n: (0, kb)),
    ]
    args = [sinks, proj, proj, proj, proj, proj, qg, kg, dy, proj]
    body, in_specs, args = _ordered_behind(body, len(args), in_specs, args, after)
    return pl.pallas_call(
        body, name="attn_bwd", grid=(nb,),
        in_specs=in_specs,
        out_specs=[pl.BlockSpec((t, aw + 2 * kv), lambda n: (0, 0)), pl.BlockSpec((8, BLK), lambda n: (0, 0))],
        out_shape=[SDS((t, aw + 2 * kv), BF16), SDS((8, BLK), F32)],
        scratch_shapes=[pltpu.VMEM((t, kv), F32), pltpu.VMEM((t, kv), F32), pltpu.VMEM((1, aw), F32)],
        compiler_params=_params("arbitrary"),
    )(*args)


def _conv_taps(win, w_ref, shift_of):
    rows = win.shape[0]
    acc = None
    for j in range(CONV_K):
        term = pltpu.roll(win, (rows - shift_of(j)) % rows, axis=0)[:BLK] * w_ref[j:j + 1, :]
        acc = term if acc is None else acc + term
    return acc


def _layer_norm_fwd(z):
    mu = jnp.mean(z, axis=-1, keepdims=True)
    zc = z - mu
    rstd = lax.rsqrt(jnp.mean(zc * zc, axis=-1, keepdims=True) + EPS)
    return zc * rstd, rstd


def _layer_norm_bwd(dy, yhat, rstd, g):
    dyh = dy * g
    return rstd * (dyh - jnp.mean(dyh, axis=-1, keepdims=True) - yhat * jnp.mean(dyh * yhat, axis=-1, keepdims=True))


def _conv_fill_glu(a_ref, g_ref, hpad, nb):
    hpad[0:HALO, :] = jnp.zeros((HALO, hpad.shape[1]), F32)

    def fill(i, c):
        r = pl.multiple_of(i * BLK, BLK)
        hpad[pl.ds(pl.multiple_of(r + HALO, HALO), BLK), :] = a_ref[pl.ds(r, BLK), :] * _sigmoid(g_ref[pl.ds(r, BLK), :])
        return c

    lax.fori_loop(0, nb, fill, 0)


def _conv_specs(t, cw, d_in):
    base = (d_in - 4 * cw) // cw
    vec = pl.BlockSpec((1, cw), lambda i: (0, 0))
    return [
        pl.BlockSpec((t, cw), lambda i: (0, base)),
        pl.BlockSpec((t, cw), lambda i: (0, base + 1)),
        pl.BlockSpec((HALO, cw), lambda i: (0, 0)),
        vec, vec, vec,
    ]


def _conv_fwd(proj, cw_pad, cb, lg, lb, cw):
    t, d_in = proj.shape
    nb = t // BLK

    def body(a_ref, g_ref, w_ref, b_ref, lg_ref, lb_ref, o_ref, hpad):
        _conv_fill_glu(a_ref, g_ref, hpad, nb)

        def blk(i, c):
            r = pl.multiple_of(i * BLK, BLK)
            z = _conv_taps(hpad[pl.ds(r, BLK + HALO), :], w_ref, lambda j: j + HALO - (CONV_K - 1)) + b_ref[...]
            yhat, _ = _layer_norm_fwd(z)
            y = yhat * lg_ref[...] + lb_ref[...]
            o_ref[pl.ds(r, BLK), :] = y * _sigmoid(y)
            return c

        lax.fori_loop(0, nb, blk, 0)

    return pl.pallas_call(
        body, name="conv_fwd", grid=(1,),
        in_specs=_conv_specs(t, cw, d_in),
        out_specs=pl.BlockSpec((t, cw), lambda i: (0, 0)),
        out_shape=SDS((t, cw), F32),
        scratch_shapes=[pltpu.VMEM((t + HALO, cw), F32)],
        compiler_params=_params("arbitrary"),
    )(proj, proj, cw_pad, cb, lg, lb)


def _conv_bwd(proj, dy, cw_pad, cb, lg, lb, cw):
    t, d_in = proj.shape
    nb = t // BLK

    def body(a_ref, g_ref, w_ref, b_ref, lg_ref, lb_ref, dy_ref, dc_ref, dw_ref, dvec_ref, hpad, dzpad, dwacc):
        _conv_fill_glu(a_ref, g_ref, hpad, nb)
        dzpad[t:t + HALO, :] = jnp.zeros((HALO, cw), F32)
        dwacc[...] = jnp.zeros_like(dwacc)

        def blk(i, carry):
            db, dlg, dlb = carry
            r = pl.multiple_of(i * BLK, BLK)
            win = hpad[pl.ds(r, BLK + HALO), :]
            z = _conv_taps(win, w_ref, lambda j: j + HALO - (CONV_K - 1)) + b_ref[...]
            yhat, rstd = _layer_norm_fwd(z)
            y = yhat * lg_ref[...] + lb_ref[...]
            sg = _sigmoid(y)
            dyl = dy_ref[pl.ds(r, BLK), :] * (sg * (1.0 + y * (1.0 - sg)))
            dz = _layer_norm_bwd(dyl, yhat, rstd, lg_ref[...])
            dzpad[pl.ds(r, BLK), :] = dz
            for j in range(CONV_K):
                sh = j + HALO - (CONV_K - 1)
                prod = dz * pltpu.roll(win, (BLK + HALO - sh) % (BLK + HALO), axis=0)[:BLK]
                dwacc[8 * j:8 * j + 8, :] += jnp.sum(prod.reshape(BLK // 8, 8, cw), axis=0)
            return (db + jnp.sum(dz, axis=0, keepdims=True),
                    dlg + jnp.sum(dyl * yhat, axis=0, keepdims=True),
                    dlb + jnp.sum(dyl, axis=0, keepdims=True))

        zero = jnp.zeros((1, cw), F32)
        db, dlg, dlb = lax.fori_loop(0, nb, blk, (zero, zero, zero))
        dvec_ref[...] = jnp.zeros_like(dvec_ref)
        dvec_ref[0:1, :] = db
        dvec_ref[1:2, :] = dlg
        dvec_ref[2:3, :] = dlb
        dw_ref[...] = jnp.sum(dwacc[...].reshape(HALO, 8, cw), axis=1)

        def blk2(i, c):
            r = pl.multiple_of(i * BLK, BLK)
            dh = _conv_taps(dzpad[pl.ds(r, BLK + HALO), :], w_ref, lambda j: CONV_K - 1 - j)
            a = a_ref[pl.ds(r, BLK), :]
            sg = _sigmoid(g_ref[pl.ds(r, BLK), :])
            dc_ref[pl.ds(r, BLK), 0:cw] = (dh * sg).astype(BF16)
            dc_ref[pl.ds(r, BLK), cw:2 * cw] = (dh * a * sg * (1.0 - sg)).astype(BF16)
            return c

        lax.fori_loop(0, nb, blk2, 0)

    return pl.pallas_call(
        body, name="conv_bwd", grid=(1,),
        in_specs=_conv_specs(t, cw, d_in) + [pl.BlockSpec((t, cw), lambda i: (0, 0))],
        out_specs=[pl.BlockSpec((t, 2 * cw), lambda i: (0, 0)), pl.BlockSpec((HALO, cw), lambda i: (0, 0)),
                   pl.BlockSpec((8, cw), lambda i: (0, 0))],
        out_shape=[SDS((t, 2 * cw), BF16), SDS((HALO, cw), F32), SDS((8, cw), F32)],
        scratch_shapes=[pltpu.VMEM((t + HALO, cw), F32), pltpu.VMEM((t + HALO, cw), F32), pltpu.VMEM((8 * HALO, cw), F32)],
        compiler_params=_params("arbitrary"),
    )(proj, proj, cw_pad, cb, lg, lb, dy)


def _tril_bf16(w):
    r = lax.broadcasted_iota(jnp.int32, (BLK, BLK), 0)
    c = lax.broadcasted_iota(jnp.int32, (BLK, BLK), 1)
    return jnp.where(r >= c, w, 0.0).astype(BF16)


def _sgu_specs(sw, d_in, heads):
    base = (d_in - 2 * sw) // sw
    vec = pl.BlockSpec((1, sw), lambda n: (0, 0))
    return [
        pl.BlockSpec((BLK, sw), lambda n: (n, base)),
        pl.BlockSpec((BLK, sw), lambda n: (n, base + 1)),
        vec, vec,
        pl.BlockSpec((heads, BLK, BLK), lambda n: (0, 0, 0)),
        pl.BlockSpec((BLK, sw), lambda n: (0, 0)),
    ]


def _sgu_mix(w_ref, vnb, heads, sw, transpose):
    head_of = lax.broadcasted_iota(jnp.int32, (1, sw), 1) // HEAD
    s = jnp.zeros((BLK, sw), F32)
    for h in range(heads):
        wt = _tril_bf16(w_ref[h])
        mixed = _tn(wt, vnb) if transpose else _nn(wt, vnb)
        s = jnp.where(head_of == h, mixed, s)
    return s


def _sgu_fwd(proj, lg, lb, w, bias_full, sw):
    t, d_in = proj.shape
    heads = sw // HEAD

    def body(u_ref, v_ref, lg_ref, lb_ref, w_ref, bias_ref, o_ref):
        vhat, _ = _layer_norm_fwd(v_ref[...])
        vn = (vhat * lg_ref[...] + lb_ref[...]).astype(BF16)
        s = _sgu_mix(w_ref, vn, heads, sw, False) + bias_ref[...]
        o_ref[...] = u_ref[...] * s

    return pl.pallas_call(
        body, name="sgu_fwd", grid=(t // BLK,),
        in_specs=_sgu_specs(sw, d_in, heads),
        out_specs=pl.BlockSpec((BLK, sw), lambda n: (n, 0)),
        out_shape=SDS((t, sw), F32),
        compiler_params=_params("arbitrary"),
    )(proj, proj, lg, lb, w, bias_full)


def _sgu_bwd(proj, dy, lg, lb, w, bias_full, sw):
    t, d_in = proj.shape
    heads = sw // HEAD
    nb = t // BLK

    def body(u_ref, v_ref, lg_ref, lb_ref, w_ref, bias_ref, dy_ref, ds_ref, dw_ref, db_ref, dvec_ref, dbfull):
        n = pl.program_id(0)

        @pl.when(n == 0)
        def _():
            dw_ref[...] = jnp.zeros_like(dw_ref)
            dvec_ref[...] = jnp.zeros_like(dvec_ref)
            dbfull[...] = jnp.zeros_like(dbfull)

        vhat, rstd = _layer_norm_fwd(v_ref[...])
        vn = (vhat * lg_ref[...] + lb_ref[...]).astype(BF16)
        s = _sgu_mix(w_ref, vn, heads, sw, False) + bias_ref[...]
        dy = dy_ref[...]
        ds_ref[:, 0:sw] = (dy * s).astype(BF16)
        dsv = dy * u_ref[...]
        dbfull[...] += dsv
        head_of = lax.broadcasted_iota(jnp.int32, (1, sw), 1) // HEAD
        r = lax.broadcasted_iota(jnp.int32, (BLK, BLK), 0)
        c = lax.broadcasted_iota(jnp.int32, (BLK, BLK), 1)
        dsb = dsv.astype(BF16)
        for h in range(heads):
            dsh = jnp.where(head_of == h, dsv, 0.0).astype(BF16)
            dw_ref[h] += jnp.where(r >= c, _nt(dsh, vn), 0.0)
        dvn = _sgu_mix(w_ref, dsb, heads, sw, True)
        dvec_ref[0:1, :] += jnp.sum(dvn * vhat, axis=0, keepdims=True)
        dvec_ref[1:2, :] += jnp.sum(dvn, axis=0, keepdims=True)
        ds_ref[:, sw:2 * sw] = _layer_norm_bwd(dvn, vhat, rstd, lg_ref[...]).astype(BF16)

        @pl.when(n == nb - 1)
        def _():
            sel = (lax.broadcasted_iota(jnp.int32, (sw, BLK), 0) // HEAD == lax.broadcasted_iota(jnp.int32, (sw, BLK), 1)).astype(BF16)
            x = dbfull[...]
            hi = x.astype(BF16)
            r1 = x - hi.astype(F32)
            mid = r1.astype(BF16)
            low = (r1 - mid.astype(F32)).astype(BF16)
            db_ref[...] = _nn(hi, sel) + _nn(mid, sel) + _nn(low, sel)

    return pl.pallas_call(
        body, name="sgu_bwd", grid=(nb,),
        in_specs=_sgu_specs(sw, d_in, heads) + [pl.BlockSpec((BLK, sw), lambda n: (n, 0))],
        out_specs=[pl.BlockSpec((BLK, 2 * sw), lambda n: (n, 0)), pl.BlockSpec((heads, BLK, BLK), lambda n: (0, 0, 0)),
                   pl.BlockSpec((BLK, BLK), lambda n: (0, 0)), pl.BlockSpec((8, sw), lambda n: (0, 0))],
        out_shape=[SDS((t, 2 * sw), BF16), SDS((heads, BLK, BLK), F32), SDS((BLK, BLK), F32), SDS((8, sw), F32)],
        scratch_shapes=[pltpu.VMEM((BLK, sw), F32)],
        compiler_params=_params("arbitrary"),
    )(proj, proj, lg, lb, w, bias_full, dy)


def _rms_fwd(x, g):
    return (x * lax.rsqrt(jnp.mean(x * x, axis=-1, keepdims=True) + EPS)) * g


def _rms_bwd(dh, x, g):
    rstd = lax.rsqrt(jnp.mean(x * x, axis=-1, keepdims=True) + EPS)
    xhat = x * rstd
    dgx = dh * g
    dx = rstd * (dgx - xhat * jnp.mean(dgx * xhat, axis=-1, keepdims=True))
    return dx, jnp.sum(dh * xhat, axis=0, keepdims=True)


def _rms_matmul(x, g, w, tm, tn, relu2, name, transposed=False):
    t, d = x.shape
    if w.ndim == 3:
        assert w.shape[2] == tn
        n = w.shape[0] * tn
        w_spec = pl.BlockSpec((None, d, tn), lambda i, j: (j, 0, 0))
    elif transposed:
        n = w.shape[0]
        w_spec = pl.BlockSpec((tn, d), lambda i, j: (j, 0))
    else:
        n = w.shape[1]
        w_spec = pl.BlockSpec((d, tn), lambda i, j: (0, j))

    def body(x_ref, g_ref, w_ref, h_ref, *outs):
        @pl.when(pl.program_id(1) == 0)
        def _():
            h_ref[...] = _rms_fwd(x_ref[...], g_ref[...]).astype(BF16)

        acc = _nt(h_ref[...], w_ref[...]) if transposed else _nn(h_ref[...], w_ref[...])
        if relu2:
            r = jnp.maximum(acc, 0.0)
            outs[0][...] = (r * r).astype(BF16)
            outs[1][...] = r.astype(BF16)
        else:
            outs[0][...] = acc

    tile = pl.BlockSpec((tm, tn), lambda i, j: (i, j))
    row = pl.BlockSpec((tm, d), lambda i, j: (i, 0))
    outs = [SDS((t, n), BF16), SDS((t, n), BF16)] if relu2 else [SDS((t, n), F32)]
    return pl.pallas_call(
        body, name=name, grid=(t // tm, n // tn),
        in_specs=[row, pl.BlockSpec((1, d), lambda i, j: (0, 0)), w_spec],
        out_specs=[row] + [tile] * len(outs),
        out_shape=[SDS((t, d), BF16)] + outs,
        compiler_params=_params("parallel", "arbitrary"),
    )(x, g, w)


def _group_rms_matmul(ys, g, w, res, tm, tn):
    t, d = res.shape
    widths = [y.shape[1] for y in ys]
    k = sum(widths)

    def body(*refs):
        y_refs, (g_ref, w_ref, res_ref, mix_ref, o_ref) = refs[:len(ys)], refs[len(ys):]

        @pl.when(pl.program_id(1) == 0)
        def _():
            c = 0
            for y_ref, wd in zip(y_refs, widths):
                mix_ref[:, c:c + wd] = _rms_fwd(y_ref[...], g_ref[:, c:c + wd]).astype(BF16)
                c += wd

        o_ref[...] = res_ref[...] + _nn(mix_ref[...], w_ref[...])

    tile = pl.BlockSpec((tm, tn), lambda i, j: (i, j))
    return pl.pallas_call(
        body, name="mix_out", grid=(t // tm, d // tn),
        in_specs=[pl.BlockSpec((tm, wd), lambda i, j: (i, 0)) for wd in widths] + [
            pl.BlockSpec((1, k), lambda i, j: (0, 0)), pl.BlockSpec((k, tn), lambda i, j: (0, j)), tile],
        out_specs=[pl.BlockSpec((tm, k), lambda i, j: (i, 0)), tile],
        out_shape=[SDS((t, k), BF16), SDS((t, d), F32)],
        compiler_params=_params("parallel", "arbitrary"),
    )(*ys, g, w, res)


def _matmul_res(a, w, res, tm, tn, tk):
    t, k = a.shape
    n = w.shape[1]
    nk = k // tk

    def body(a_ref, w_ref, res_ref, o_ref, acc):
        kk = pl.program_id(2)

        @pl.when(kk == 0)
        def _():
            acc[...] = res_ref[...]

        acc[...] += _nn(a_ref[...], w_ref[...])

        @pl.when(kk == nk - 1)
        def _():
            o_ref[...] = acc[...]

    tile = pl.BlockSpec((tm, tn), lambda i, j, kk: (i, j))
    return pl.pallas_call(
        body, name="mlp_down", grid=(t // tm, n // tn, nk),
        in_specs=[pl.BlockSpec((tm, tk), lambda i, j, kk: (i, kk)), pl.BlockSpec((tk, tn), lambda i, j, kk: (kk, j)), tile],
        out_specs=tile,
        out_shape=SDS((t, n), F32),
        scratch_shapes=[pltpu.VMEM((tm, tn), F32)],
        compiler_params=_params("parallel", "parallel", "arbitrary"),
    )(a, w, res)


def _loss_grad(y, target, tm):
    t, d = y.shape

    def body(y_ref, t_ref, dy_ref, l_ref):
        @pl.when(pl.program_id(0) == 0)
        def _():
            l_ref[...] = jnp.zeros_like(l_ref)

        err = y_ref[...] - t_ref[...]
        dy_ref[...] = err * (1.0 / d)
        per_row = jnp.mean(err * err, axis=-1, keepdims=True)
        l_ref[...] += jnp.sum(per_row, axis=0, keepdims=True) * 0.5

    row = pl.BlockSpec((tm, d), lambda i: (i, 0))
    return pl.pallas_call(
        body, name="loss_grad", grid=(t // tm,),
        in_specs=[row, row], out_specs=[row, pl.BlockSpec((8, BLK), lambda i: (0, 0))],
        out_shape=[SDS((t, d), F32), SDS((8, BLK), F32)],
        compiler_params=_params("arbitrary"),
    )(y, target)


def _mlp_dact(dx, w_down, r, tm, tn, after=None):
    t, d = dx.shape
    f = w_down.shape[0]

    def body(dx_ref, w_ref, r_ref, dxb_ref, dup_ref):
        @pl.when(pl.program_id(1) == 0)
        def _():
            dxb_ref[...] = dx_ref[...].astype(BF16)

        dup_ref[...] = (_nt(dxb_ref[...], w_ref[...]) * (2.0 * r_ref[...].astype(F32))).astype(BF16)

    row = pl.BlockSpec((tm, d), lambda i, j: (i, 0))
    tile = pl.BlockSpec((tm, tn), lambda i, j: (i, j))
    body, in_specs, args = _ordered_behind(
        body, 3, [row, pl.BlockSpec((tn, d), lambda i, j: (j, 0)), tile], [dx, w_down, r], after)
    return pl.pallas_call(
        body, name="mlp_dact", grid=(t // tm, f // tn),
        in_specs=in_specs,
        out_specs=[row, tile],
        out_shape=[SDS((t, d), BF16), SDS((t, f), BF16)],
        compiler_params=_params("parallel", "arbitrary"),
    )(*args)


def _grad_w(a, b, tm, tn, name, col_blocks=False, after=None, rows=None, into=None):
    t, m = a.shape
    n = b.shape[1]
    assert n % tn == 0 and m % tm == 0
    total, first = rows if rows else (m, 0)
    assert first % tm == 0
    first_tile = first // tm

    def body(a_ref, b_ref, *rest):
        rest[-1][...] = _tn(a_ref[...], b_ref[...]).astype(BF16)

    if col_blocks:
        out_spec = pl.BlockSpec((None, tm, tn), lambda i, j: (j, i, 0))
        out_shape = SDS((n // tn, m, tn), BF16)
    else:
        out_spec = pl.BlockSpec((tm, tn), lambda i, j: (first_tile + i, j))
        out_shape = SDS((total, n), BF16)
    in_specs = [pl.BlockSpec((t, tm), lambda i, j: (0, i)), pl.BlockSpec((t, tn), lambda i, j: (0, j))]
    args = [a, b]
    aliases = {}
    if into is not None:
        in_specs, args, aliases = in_specs + [ANY], args + [into], {2: 0}
    body, in_specs, args = _ordered_behind(body, len(args), in_specs, args, after)
    return pl.pallas_call(
        body, name=name, grid=(m // tm, n // tn),
        in_specs=in_specs, out_specs=out_spec, out_shape=out_shape, input_output_aliases=aliases,
        compiler_params=_params("parallel", "arbitrary"),
    )(*args)


def _mlp_dnorm(dup, w_up, x, g, dres, tm, after=None):
    t, f = dup.shape
    d = x.shape[1]
    nk, _, tk = w_up.shape

    def body(a_ref, w_ref, x_ref, g_ref, dres_ref, dx_ref, dg_ref, acc):
        i, kk = pl.program_id(0), pl.program_id(1)

        @pl.when((i == 0) & (kk == 0))
        def _():
            dg_ref[...] = jnp.zeros_like(dg_ref)

        @pl.when(kk == 0)
        def _():
            acc[...] = jnp.zeros_like(acc)

        acc[...] += _nt(a_ref[...], w_ref[...])

        @pl.when(kk == nk - 1)
        def _():
            dx, dg = _rms_bwd(acc[...], x_ref[...], g_ref[...])
            dx_ref[...] = dres_ref[...] + dx
            dg_ref[0:1, :] += dg

    row = pl.BlockSpec((tm, d), lambda i, kk: (i, 0))
    in_specs = [pl.BlockSpec((tm, tk), lambda i, kk: (i, kk)), pl.BlockSpec((None, d, tk), lambda i, kk: (kk, 0, 0)),
                row, pl.BlockSpec((1, d), lambda i, kk: (0, 0)), row]
    body, in_specs, args = _ordered_behind(body, 5, in_specs, [dup, w_up, x, g, dres], after)
    return pl.pallas_call(
        body, name="mlp_dnorm", grid=(t // tm, nk),
        in_specs=in_specs,
        out_specs=[row, pl.BlockSpec((8, d), lambda i, kk: (0, 0))],
        out_shape=[SDS((t, d), F32), SDS((8, d), F32)],
        scratch_shapes=[pltpu.VMEM((tm, d), F32)],
        compiler_params=_params("arbitrary", "arbitrary"),
    )(*args)


def _mix_dnorm(dx, w_out, ys, g, tm, after=None):
    t, d = dx.shape
    k = w_out.shape[0]
    widths = [y.shape[1] for y in ys]

    def body(dx_ref, w_ref, *refs):
        y_refs = refs[:len(ys)]
        g_ref, dxb_ref = refs[len(ys)], refs[len(ys) + 1]
        dy_refs = refs[len(ys) + 2:2 * len(ys) + 2]
        dg_ref = refs[-1]

        @pl.when(pl.program_id(0) == 0)
        def _():
            dg_ref[...] = jnp.zeros_like(dg_ref)

        dxb = dx_ref[...].astype(BF16)
        dxb_ref[...] = dxb
        dmix = _nt(dxb, w_ref[...])
        c = 0
        for y_ref, dy_ref, wd in zip(y_refs, dy_refs, widths):
            dy, dg = _rms_bwd(dmix[:, c:c + wd], y_ref[...], g_ref[:, c:c + wd])
            dy_ref[...] = dy
            dg_ref[0:1, c:c + wd] += dg
            c += wd

    row = pl.BlockSpec((tm, d), lambda i: (i, 0))
    yspecs = [pl.BlockSpec((tm, wd), lambda i: (i, 0)) for wd in widths]
    in_specs = [row, pl.BlockSpec((k, d), lambda i: (0, 0))] + yspecs + [pl.BlockSpec((1, k), lambda i: (0, 0))]
    body, in_specs, args = _ordered_behind(body, len(in_specs), in_specs, [dx, w_out, *ys, g], after)
    return pl.pallas_call(
        body, name="mix_dnorm", grid=(t // tm,),
        in_specs=in_specs,
        out_specs=[row] + yspecs + [pl.BlockSpec((8, k), lambda i: (0, 0))],
        out_shape=[SDS((t, d), BF16)] + [SDS((t, wd), F32) for wd in widths] + [SDS((8, k), F32)],
        compiler_params=_params("arbitrary"),
    )(*args)


def _in_dnorm(dps, w_in_t, x, g, dres, tm):
    t, d = x.shape
    widths = [p.shape[1] for p in dps]
    offs = [sum(widths[:p]) for p in range(len(dps))]
    n_in = w_in_t.shape[0]

    def body(*refs):
        p_refs = refs[:len(dps)]
        w_ref, x_ref, g_ref, dres_ref, dx_ref, dg_ref = refs[len(dps):]

        @pl.when(pl.program_id(0) == 0)
        def _():
            dg_ref[...] = jnp.zeros_like(dg_ref)

        acc = None
        for p_ref, off, wd in zip(p_refs, offs, widths):
            term = _nn(p_ref[...], w_ref[off:off + wd, :])
            acc = term if acc is None else acc + term
        dx, dg = _rms_bwd(acc, x_ref[...], g_ref[...])
        dx_ref[...] = dres_ref[...] + dx
        dg_ref[0:1, :] += dg

    row = pl.BlockSpec((tm, d), lambda i: (i, 0))
    return pl.pallas_call(
        body, name="in_dnorm", grid=(t // tm,),
        in_specs=[pl.BlockSpec((tm, wd), lambda i: (i, 0)) for wd in widths] + [
            pl.BlockSpec((n_in, d), lambda i: (0, 0)), row, pl.BlockSpec((1, d), lambda i: (0, 0)), row],
        out_specs=[row, pl.BlockSpec((8, d), lambda i: (0, 0))],
        out_shape=[SDS((t, d), F32), SDS((8, d), F32)],
        compiler_params=_params("arbitrary"),
    )(*dps, w_in_t, x, g, dres)


def _tile(n, want):
    return min(n, want)


def _row_tile(n, want):
    return max(k for k in range(8, min(n, want) + 1, 8) if n % k == 0)


def _layer_fwd(x, p, fetch, after):
    t, d = x.shape
    aw, kv, cw, sw = d // 2, d // 8, d // 4, d // 4
    tm = _tile(t, 1024)
    w_in = fetch("w_in", after)
    h1, proj = _rms_matmul(x, p["ln1_g"], w_in, tm, 512 if w_in.shape[0] % 512 == 0 else 256, False, "in_proj",
                           transposed=True)
    y_attn = _attn_fwd(proj, p["qg"], p["kg"], p["sinks"], aw, kv)
    y_conv = _conv_fwd(proj, p["conv_w"], p["conv_b"], p["conv_ln_g"], p["conv_ln_b"], cw)
    y_sgu = _sgu_fwd(proj, p["sgu_ln_g"], p["sgu_ln_b"], p["sgu_w"], p["sgu_bias"], sw)
    ys = [y_attn, y_conv, y_sgu]
    w_out = fetch("w_out", y_sgu)
    mix, x1 = _group_rms_matmul(ys, p["out_norm_g"], w_out, x, tm, _tile(d, 1024))
    w_up = fetch("w_up", x1)
    h2, act, r = _rms_matmul(x1, p["ln2_g"], w_up, tm, w_up.shape[2], True, "mlp_up")
    w_down = fetch("w_down", act)
    x2 = _matmul_res(act, w_down, x1, tm, _tile(d, 2048), 512)
    saved = dict(x=x, h1=h1, proj=proj, ys=ys, mix=mix, x1=x1, h2=h2, act=act, r=r,
                 w_in=w_in, w_out=w_out, w_up=w_up, w_down=w_down)
    return x2, saved


def _layer_bwd(dx2, p, s, start, finish, carry):
    t, d = dx2.shape
    aw, kv, cw, sw = d // 2, d // 8, d // 4, d // 4
    tm = _tile(t, 512)
    dx2b, dup = _mlp_dact(dx2, s["w_down"], s["r"], _tile(t, 1024), 1024, carry[0] if carry else None)
    tok = carry[1](dx2b) if carry else None
    tok = start("w_down", _grad_w(s["act"], dx2b, 512, _tile(d, 2048), "grad_w_down", after=tok))
    dx1, d_ln2 = _mlp_dnorm(dup, s["w_up"], s["x1"], p["ln2_g"], dx2, tm, tok)
    tok = finish("w_down", dx1)
    tok = start("w_up", _grad_w(s["h2"], dup, _tile(d, 1024), s["w_up"].shape[2], "grad_w_up", col_blocks=True, after=tok))
    dx1b, dya, dyc, dys, d_onorm = _mix_dnorm(dx1, s["w_out"], s["ys"], p["out_norm_g"], _tile(t, 256), tok)
    tok = finish("w_up", dx1b)
    tok = start("w_out", _grad_w(s["mix"], dx1b, 512, _tile(d, 2048), "grad_w_out", after=tok))
    dqkv, d_attn = _attn_bwd(s["proj"], dya, p["qg"], p["kg"], p["sinks"], aw, kv, tok)
    tok = finish("w_out", dqkv)
    dconv, d_cw, d_cvec = _conv_bwd(s["proj"], dyc, p["conv_w"], p["conv_b"], p["conv_ln_g"], p["conv_ln_b"], cw)
    dsgu, d_sw, d_sb, d_svec = _sgu_bwd(s["proj"], dys, p["sgu_ln_g"], p["sgu_ln_b"], p["sgu_w"], p["sgu_bias"], sw)
    dps = [dqkv, dconv, dsgu]
    dx, d_ln1 = _in_dnorm(dps, s["w_in"], s["x"], p["ln1_g"], dx1, _tile(t, 256))
    d_in, first, g_in = sum(dp.shape[1] for dp in dps), 0, None
    tm_in = 512 if all(dp.shape[1] % 512 == 0 for dp in dps) else 256
    for k, dp in enumerate(dps):
        g_in = _grad_w(dp, s["h1"], tm_in, _tile(d, 2048), f"grad_w_in_{k}", after=tok if k == 0 else None,
                       rows=(d_in, first), into=g_in)
        first += dp.shape[1]
    tok = start("w_in", g_in)
    carry = (tok, functools.partial(finish, "w_in"))
    heads = sw // HEAD
    small = dict(
        ln1_g=d_ln1[0], q_norm_g=d_attn[0, :HEAD], k_norm_g=d_attn[1, :HEAD], sinks=d_attn[2, :aw // HEAD],
        conv_w=d_cw[:CONV_K], conv_b=d_cvec[0], conv_ln_g=d_cvec[1], conv_ln_b=d_cvec[2],
        sgu_ln_g=d_svec[0], sgu_ln_b=d_svec[1], sgu_w=d_sw, sgu_b=d_sb[:, :heads].T,
        out_norm_g=d_onorm[0], ln2_g=d_ln2[0])
    return dx, small, carry


def _layer_params(l, small):
    row = lambda v: v[l][None, :]
    two = lambda v: jnp.tile(v[l], 2)[None, :]
    return dict(
        ln1_g=row(small["ln1_g"]), ln2_g=row(small["ln2_g"]), out_norm_g=row(small["out_norm_g"]),
        qg=two(small["q_norm_g"]), kg=two(small["k_norm_g"]), sinks=small["sinks"][l],
        conv_w=jnp.pad(small["conv_w"][l], ((0, HALO - CONV_K), (0, 0))),
        conv_b=row(small["conv_b"]), conv_ln_g=row(small["conv_ln_g"]), conv_ln_b=row(small["conv_ln_b"]),
        sgu_ln_g=row(small["sgu_ln_g"]), sgu_ln_b=row(small["sgu_ln_b"]), sgu_w=small["sgu_w"][l],
        sgu_bias=jnp.repeat(small["sgu_b"][l].T, HEAD, axis=1),
    )


def _local_step(x, target, small, depth, fetch, start, finish, after):
    params = [_layer_params(l, small) for l in range(depth)]
    saved = []
    h = x
    for l in range(depth):
        h, s = _layer_fwd(h, params[l], functools.partial(fetch, l), after)
        after = h
        saved.append(s)
    dy, lsum = _loss_grad(h, target, _tile(x.shape[0], 512))
    gs = [None] * depth
    carry = None
    for l in reversed(range(depth)):
        dy, gs[l], carry = _layer_bwd(dy, params[l], saved[l], functools.partial(start, l), functools.partial(finish, l), carry)
    return lsum[0, 0], dy, gs, carry


BIG = ("w_in", "w_out", "w_up", "w_down")
HBM = pl.BlockSpec(memory_space=pltpu.HBM)
SEMS = pl.BlockSpec(memory_space=pltpu.SEMAPHORE)
EFFECT = pltpu.SideEffectType.DATAFLOW_SIDE_EFFECTING


def _mesh_pos():
    return lax.axis_index("x"), lax.axis_index("y"), lax.axis_index("c")


def _other_chips(x, y):
    return [(1 - x, y), (x, 1 - y), (1 - x, 1 - y)]


def _copies(plan, refs, sends, recvs):
    x, y, c = _mesh_pos()
    return [pltpu.make_async_remote_copy(src_ref=src, dst_ref=dst, send_sem=sends.at[k], recv_sem=recvs.at[k],
                                         device_id=dev, device_id_type=MESH)
            for k, (src, dst, dev) in enumerate(plan(x, y, c, refs))]


def _gather_plan(x, y, c, refs):
    mine = refs[0].at[4 * x + 2 * y + c]
    return [(mine, mine, (x, y, 1 - c))] + [(mine, mine, (*chip, c)) for chip in _other_chips(x, y)]


def _pair_plan(x, y, c, refs):
    blocks, land = refs
    return [(blocks.at[2 * q + (1 - c)], land.at[q], (x, y, 1 - c)) for q in range(4)]


def _chips_plan(x, y, c, refs):
    sums, land = refs
    return [(sums.at[2 * chip[0] + chip[1]], land.at[k], (*chip, c)) for k, chip in enumerate(_other_chips(x, y))]


def _start_exchanges(name, groups):
    flat = [a for arrays, _, _ in groups for a in arrays]
    n_arr, n_g = len(flat), len(groups)

    def body(*refs):
        ins, sems, token = refs[:n_arr], refs[n_arr:n_arr + 2 * n_g], refs[-1]
        off = 0
        for gi, (arrays, plan, _) in enumerate(groups):
            for cp in _copies(plan, ins[off:off + len(arrays)], sems[2 * gi], sems[2 * gi + 1]):
                cp.start()
            off += len(arrays)
        token[...] = jnp.zeros_like(token)

    res = pl.pallas_call(
        body, name=name,
        out_shape=[pltpu.SemaphoreType.DMA((n,)) for _, _, n in groups for _ in (0, 1)]
        + [pltpu.HBM(a.shape, a.dtype) for a in flat] + [SDS((8, BLK), F32)],
        in_specs=[HBM] * n_arr,
        out_specs=[SEMS] * (2 * n_g) + [HBM] * n_arr + [pl.BlockSpec(memory_space=pltpu.VMEM)],
        input_output_aliases={i: 2 * n_g + i for i in range(n_arr)},
        compiler_params=pltpu.CompilerParams(has_side_effects=EFFECT),
    )(*[pltpu.with_memory_space_constraint(a, pltpu.HBM) for a in flat])
    sems, thru, token = res[:2 * n_g], res[2 * n_g:2 * n_g + n_arr], res[-1]
    out, off = [], 0
    for gi, (arrays, _, _) in enumerate(groups):
        out.append((list(thru[off:off + len(arrays)]), sems[2 * gi], sems[2 * gi + 1]))
        off += len(arrays)
    return out, token


def _wait_exchange(name, arrays, sends, recvs, plan, after):
    n = len(arrays)

    def body(*refs):
        for cp in _copies(plan, refs[:n], refs[n], refs[n + 1]):
            cp.wait_send()
            cp.wait_recv()

    return pl.pallas_call(
        body, name=name,
        out_shape=[pltpu.HBM(a.shape, a.dtype) for a in arrays],
        in_specs=[HBM] * n + [SEMS, SEMS, ANY],
        out_specs=[HBM] * n,
        input_output_aliases={i: i for i in range(n)},
        compiler_params=pltpu.CompilerParams(has_side_effects=EFFECT),
    )(*arrays, sends, recvs, after)


def _gather_finish(land, name):
    def body(land_ref, out_ref, send_sems, recv_sems):
        del land_ref
        x, y, c = _mesh_pos()
        cps = []
        for k, chip in enumerate(_other_chips(x, y)):
            block = out_ref.at[4 * chip[0] + 2 * chip[1] + c]
            cps.append(pltpu.make_async_remote_copy(
                src_ref=block, dst_ref=block, send_sem=send_sems.at[k], recv_sem=recv_sems.at[k],
                device_id=(x, y, 1 - c), device_id_type=MESH))
        for cp in cps:
            cp.start()
        for cp in cps:
            cp.wait()

    return pl.pallas_call(
        body, name=name,
        in_specs=[ANY], out_specs=ANY,
        out_shape=SDS(land.shape, land.dtype),
        input_output_aliases={0: 0},
        scratch_shapes=[pltpu.SemaphoreType.DMA((3,)), pltpu.SemaphoreType.DMA((3,))],
    )(land)


def _all_gather(shards, name):
    n = len(shards)

    def body(*refs):
        ins, outs = refs[:n], refs[n:2 * n]
        send_sems, recv_sems, local_sems = refs[2 * n:]
        x, y, c = _mesh_pos()
        me, sibling = (x, y, c), (x, y, 1 - c)
        chips = _other_chips(x, y)

        def copy(a, k, block, to, src=None):
            dst = outs[a].at[4 * block[0] + 2 * block[1] + block[2]]
            return pltpu.make_async_remote_copy(
                src_ref=dst if src is None else src, dst_ref=dst,
                send_sem=send_sems.at[a, k], recv_sem=recv_sems.at[a, k], device_id=to, device_id_type=MESH)

        mine = [pltpu.make_async_copy(ins[a], outs[a].at[4 * x + 2 * y + c], local_sems.at[a]) for a in range(n)]
        for cp in mine:
            cp.start()
        first = []
        for a in range(n):
            first.append(copy(a, 0, me, sibling, src=ins[a]))
            first += [copy(a, 1 + j, me, (*chip, c), src=ins[a]) for j, chip in enumerate(chips)]
        for cp in first:
            cp.start()
        passed = []
        for j, chip in enumerate(chips):
            for a in range(n):
                copy(a, 1 + j, (*chip, c), me).wait_recv()
                fwd = copy(a, 4 + j, (*chip, c), sibling)
                fwd.start()
                passed.append(fwd)
        for a in range(n):
            copy(a, 0, sibling, me).wait_recv()
            for j, chip in enumerate(chips):
                copy(a, 4 + j, (*chip, 1 - c), me).wait_recv()
        for cp in first + passed:
            cp.wait_send()
        for cp in mine:
            cp.wait()

    outs = pl.pallas_call(
        body, name=name,
        in_specs=[ANY] * n, out_specs=[ANY] * n,
        out_shape=[SDS((N_DEV,) + s.shape, s.dtype) for s in shards],
        scratch_shapes=[pltpu.SemaphoreType.DMA((n, 7)), pltpu.SemaphoreType.DMA((n, 7)), pltpu.SemaphoreType.DMA((n,))],
    )(*shards)
    return list(outs)


def _pair_add(own, got, c, name):
    _, r, cols = own.shape
    tr = _row_tile(r, 512)

    def body(c_ref, own_ref, got_ref, o_ref):
        o_ref[...] = (own_ref[...].astype(F32) + got_ref[...].astype(F32)).astype(BF16)

    return pl.pallas_call(
        body, name=name,
        grid_spec=pltpu.PrefetchScalarGridSpec(
            num_scalar_prefetch=1, grid=(4, r // tr),
            in_specs=[pl.BlockSpec((None, tr, cols), lambda q, i, c_ref: (2 * q + c_ref[0], i, 0)),
                      pl.BlockSpec((None, tr, cols), lambda q, i, c_ref: (q, i, 0))],
            out_specs=pl.BlockSpec((None, tr, cols), lambda q, i, c_ref: (q, i, 0))),
        out_shape=SDS((4, r, cols), BF16),
        compiler_params=_params("arbitrary", "arbitrary"),
    )(c, own, got)


def _adamw(w, g, m, v):
    m = ADAM_B1 * m + (1.0 - ADAM_B1) * g
    v = ADAM_B2 * v + (1.0 - ADAM_B2) * (g * g)
    m_hat = m / (1.0 - ADAM_B1 ** ADAM_STEP)
    v_hat = v / (1.0 - ADAM_B2 ** ADAM_STEP)
    delta = -ADAM_LR * (m_hat / (jnp.sqrt(v_hat) + ADAM_EPS) + ADAM_WD * w)
    return delta, m, v


def _adamw_layer(chip_sum, got, chip, w, m, v, layer, prev, name):
    depth, r, cols = w.shape
    tr = _row_tile(r, 256)

    def body(chip_ref, sum_ref, got_ref, w_ref, m_ref, v_ref, *rest):
        g_out, d_out, m_out, v_out, token = rest[-5:]
        g = sum_ref[...].astype(F32) + got_ref[0].astype(F32) + got_ref[1].astype(F32) + got_ref[2].astype(F32)
        delta, mm, vv = _adamw(w_ref[...], g, m_ref[...], v_ref[...])
        g_out[...] = g
        d_out[...] = delta
        m_out[...] = mm
        v_out[...] = vv
        token[...] = jnp.zeros_like(token)

    shard = pl.BlockSpec((None, tr, cols), lambda i, chip_ref: (layer, i, 0))
    in_specs = [pl.BlockSpec((None, tr, cols), lambda i, chip_ref: (chip_ref[0], i, 0)),
                pl.BlockSpec((3, tr, cols), lambda i, chip_ref: (0, i, 0)), shard, shard, shard]
    args = [chip, chip_sum, got, w, m, v]
    aliases = {}
    if prev is not None:
        in_specs += [ANY] * 4
        aliases = {len(args) + k: k for k in range(4)}
        args += list(prev)
    res = pl.pallas_call(
        body, name=name,
        grid_spec=pltpu.PrefetchScalarGridSpec(
            num_scalar_prefetch=1, grid=(r // tr,), in_specs=in_specs,
            out_specs=[shard] * 4 + [pl.BlockSpec((8, BLK), lambda i, chip_ref: (0, 0))]),
        out_shape=[SDS(w.shape, F32)] * 4 + [SDS((8, BLK), F32)],
        input_output_aliases=aliases,
        compiler_params=_params("arbitrary"),
    )(*args)
    return list(res[:4]), res[4]


def _sum_devices(stacked):
    _, r, _ = stacked.shape

    def body(s_ref, o_ref):
        acc = s_ref[0]
        for k in range(1, N_DEV):
            acc = acc + s_ref[k]
        o_ref[...] = acc

    return pl.pallas_call(
        body, name="sum_devices", grid=(1,),
        in_specs=[pl.BlockSpec((N_DEV, r, BLK), lambda i: (0, 0, 0))],
        out_specs=pl.BlockSpec((r, BLK), lambda i: (0, 0)),
        out_shape=SDS((r, BLK), F32),
        compiler_params=_params("arbitrary"),
    )(stacked)


def _adamw_small(g, w, m, v):
    def body(g_ref, w_ref, m_ref, v_ref, d_out, m_out, v_out):
        delta, mm, vv = _adamw(w_ref[...], g_ref[...], m_ref[...], v_ref[...])
        d_out[...] = delta
        m_out[...] = mm
        v_out[...] = vv

    spec = pl.BlockSpec(g.shape, lambda i: (0, 0))
    return pl.pallas_call(
        body, name="adamw_small", grid=(1,),
        in_specs=[spec] * 4, out_specs=[spec] * 3,
        out_shape=[SDS(g.shape, F32)] * 3,
        compiler_params=_params("arbitrary"),
    )(g, w, m, v)


def _pack(arrays):
    flat = jnp.concatenate([a.reshape(-1) for a in arrays])
    rows = -(-flat.shape[0] // (8 * BLK)) * 8
    return jnp.pad(flat, (0, rows * BLK - flat.shape[0])).reshape(rows, BLK)


def _unpack(packed, like):
    flat = packed.reshape(-1)
    out, off = [], 0
    for a in like:
        out.append(flat[off:off + a.size].reshape(a.shape))
        off += a.size
    return out


SMALL = ("ln1_g", "q_norm_g", "k_norm_g", "sinks", "conv_b", "conv_ln_g", "conv_ln_b", "sgu_ln_g", "sgu_ln_b",
         "sgu_w", "sgu_b", "out_norm_g", "ln2_g")
ORDER = ("ln1_g", "w_in", "q_norm_g", "k_norm_g", "sinks", "conv_w", "conv_b", "conv_ln_g", "conv_ln_b", "sgu_ln_g",
         "sgu_ln_b", "sgu_w", "sgu_b", "out_norm_g", "w_out", "ln2_g", "w_up", "w_down")


def _step(x, target, w, m, v):
    depth = w["ln1_g"].shape[0]
    xpos, ypos, cpos = _mesh_pos()
    me = 4 * xpos + 2 * ypos + cpos
    c_arr = jnp.reshape(cpos, (1,)).astype(jnp.int32)
    chip_arr = jnp.reshape(2 * xpos + ypos, (1,)).astype(jnp.int32)

    d = x.shape[1]
    def own_block(shard):
        return lax.dynamic_update_slice_in_dim(lax.empty((N_DEV,) + shard.shape, shard.dtype), shard[None], me, axis=0)

    cw = w["conv_w"]
    order = [(0, "conv_w")] + [(l, n) for l in range(depth) for n in BIG]
    started, gather_token = _start_exchanges("gather_start", [
        ([own_block(_pack([cw]) if n == "conv_w" else w[n][l].astype(BF16))], _gather_plan, 4) for l, n in order])
    pending = dict(zip(order, started))

    def fetch(l, n, after):
        arrays, sends, recvs = pending.pop((l, n))
        land, = _wait_exchange(f"gather_wait_{l}_{n}", arrays, sends, recvs, _gather_plan,
                               gather_token if after is None else after)
        full = _gather_finish(land, "gather_finish_" + n)
        return full if n in ("w_up", "conv_w") else full.reshape(-1, d)

    cw_all = fetch(0, "conv_w", None)
    cw_full = jnp.concatenate([_unpack(cw_all[k], [cw])[0] for k in range(N_DEV)], axis=-1)
    small = {n: w[n] for n in SMALL}
    small["conv_w"] = cw_full

    to_sibling, inflight = {}, []

    def start(l, n, g):
        blocks = g if n == "w_up" else g.reshape(N_DEV, -1, d)
        (to_sibling[l, n],), token = _start_exchanges(f"pair_start_{l}_{n}", [
            ([blocks, lax.empty((4,) + blocks.shape[1:], BF16)], _pair_plan, 4)])
        return token

    def finish(l, n, after):
        arrays, sends, recvs = to_sibling.pop((l, n))
        blocks, from_sibling = _wait_exchange(f"pair_wait_{l}_{n}", arrays, sends, recvs, _pair_plan, after)
        chip_sums = _pair_add(blocks, from_sibling, c_arr, "pair_add_" + n)
        (going,), token = _start_exchanges(f"reduce_start_{l}_{n}", [
            ([chip_sums, lax.empty((3,) + chip_sums.shape[1:], BF16)], _chips_plan, 3)])
        inflight.append((l, n, going))
        return token

    loss, grad_x, gs, (after, finish_last) = _local_step(x, target, small, depth, fetch, start, finish, None)

    as3d = lambda a: a.reshape(depth, -1, a.shape[-1])
    results = {n: None for n in BIG}

    def settle(l, n, going, after):
        arrays, sends, recvs = going
        chip_sums, got = _wait_exchange(f"reduce_wait_{l}_{n}", arrays, sends, recvs, _chips_plan, after)
        results[n], token = _adamw_layer(chip_sums, got, chip_arr, as3d(w[n]), as3d(m[n]), as3d(v[n]), l, results[n],
                                         f"adamw_{l}_{n}")
        return token

    early, late = inflight[:2], inflight[2:]
    for l, n, going in early:
        after = settle(l, n, going, after)
    after = finish_last(after)
    for l, n, going in late + inflight[-1:]:
        after = settle(l, n, going, after)

    out = {n: [r.reshape(w[n].shape) for r in results[n]] for n in BIG}

    names = SMALL + ("conv_w",)
    g_small = [jnp.stack([gs[l][n] for l in range(depth)]) for n in names]
    g_sum = _unpack(_sum_devices(_all_gather([_pack(g_small)], "gather_small_grads")[0]), g_small)
    cshard = cw.shape[-1]
    g_sum[-1] = lax.dynamic_slice_in_dim(g_sum[-1], me * cshard, cshard, axis=2)
    like = [w[n] for n in names]
    res = _adamw_small(_pack(g_sum), _pack(like), _pack([m[n] for n in names]), _pack([v[n] for n in names]))
    res = [_unpack(r, like) for r in res]
    for i, n in enumerate(names):
        out[n] = [g_sum[i], res[0][i], res[1][i], res[2][i]]

    loss = lax.psum(loss, ("x", "y", "c"))
    return (loss, grad_x[None]) + tuple(out[n][k] for k in range(4) for n in ORDER)


def kernel(x, ln1_g, w_in, q_norm_g, k_norm_g, sinks, conv_w, conv_b, conv_ln_g, conv_ln_b, sgu_ln_g, sgu_ln_b, sgu_w, sgu_b, out_norm_g, w_out, ln2_g, w_up, w_down, loss_target, m_ln1_g, m_w_in, m_q_norm_g, m_k_norm_g, m_sinks, m_conv_w, m_conv_b, m_conv_ln_g, m_conv_ln_b, m_sgu_ln_g, m_sgu_ln_b, m_sgu_w, m_sgu_b, m_out_norm_g, m_w_out, m_ln2_g, m_w_up, m_w_down, v_ln1_g, v_w_in, v_q_norm_g, v_k_norm_g, v_sinks, v_conv_w, v_conv_b, v_conv_ln_g, v_conv_ln_b, v_sgu_ln_g, v_sgu_ln_b, v_sgu_w, v_sgu_b, v_out_norm_g, v_w_out, v_ln2_g, v_w_up, v_w_down):
    w = dict(zip(ORDER, (ln1_g, w_in, q_norm_g, k_norm_g, sinks, conv_w, conv_b, conv_ln_g, conv_ln_b, sgu_ln_g, sgu_ln_b,
                         sgu_w, sgu_b, out_norm_g, w_out, ln2_g, w_up, w_down)))
    m = dict(zip(ORDER, (m_ln1_g, m_w_in, m_q_norm_g, m_k_norm_g, m_sinks, m_conv_w, m_conv_b, m_conv_ln_g, m_conv_ln_b,
                         m_sgu_ln_g, m_sgu_ln_b, m_sgu_w, m_sgu_b, m_out_norm_g, m_w_out, m_ln2_g, m_w_up, m_w_down)))
    v = dict(zip(ORDER, (v_ln1_g, v_w_in, v_q_norm_g, v_k_norm_g, v_sinks, v_conv_w, v_conv_b, v_conv_ln_g, v_conv_ln_b,
                         v_sgu_ln_g, v_sgu_ln_b, v_sgu_w, v_sgu_b, v_out_norm_g, v_w_out, v_ln2_g, v_w_up, v_w_down)))
    for group in (w, m, v):
        group["w_in"] = jnp.swapaxes(group["w_in"], 1, 2)
    out = list(_step(x[0], loss_target[0], w, m, v))
    for k in range(4):
        i = 2 + k * len(ORDER) + ORDER.index("w_in")
        out[i] = jnp.swapaxes(out[i], 1, 2)
    return tuple(out)
```

```python
import functools

import jax
import jax.numpy as jnp
from jax import lax
from jax.experimental import pallas as pl
from jax.experimental.pallas import tpu as pltpu

F32 = jnp.float32
BF16 = jnp.bfloat16
SDS = jax.ShapeDtypeStruct

EPS = 1e-6
NEG_INF = -1e30
HEAD = 64
BLK = 128
CONV_K = 31
HALO = 32
N_DEV = 8

ADAM_LR = 0.001
ADAM_B1 = 0.9
ADAM_B2 = 0.999
ADAM_EPS = 1e-08
ADAM_WD = 0.01
ADAM_STEP = 10

VMEM_LIMIT = 56 * 1024 * 1024

MESH = pl.DeviceIdType.MESH


def _params(*sem):
    return pltpu.CompilerParams(dimension_semantics=sem, vmem_limit_bytes=VMEM_LIMIT)


def _nt(a, b):
    return lax.dot_general(a, b, (((1,), (1,)), ((), ())), preferred_element_type=F32)


def _tn(a, b):
    return lax.dot_general(a, b, (((0,), (0,)), ((), ())), preferred_element_type=F32)


def _nn(a, b):
    return jnp.dot(a, b, preferred_element_type=F32)


def _sigmoid(x):
    return 1.0 / (1.0 + jnp.exp(-x))


ANY = pl.BlockSpec(memory_space=pl.ANY)


def _ordered_behind(body, n_in, in_specs, args, after):
    if after is None:
        return body, in_specs, args
    return (lambda *refs: body(*refs[:n_in], *refs[n_in + 1:])), list(in_specs) + [ANY], list(args) + [after]


def _seg_ones():
    r = lax.broadcasted_iota(jnp.int32, (BLK, BLK), 0) // HEAD
    c = lax.broadcasted_iota(jnp.int32, (BLK, BLK), 1) // HEAD
    return (r == c).astype(BF16)


def _segsum(x, ones):
    hi = x.astype(BF16)
    r1 = x - hi.astype(F32)
    mid = r1.astype(BF16)
    lo = (r1 - mid.astype(F32)).astype(BF16)
    return _nn(hi, ones) + _nn(mid, ones) + _nn(lo, ones)


def _head_rms(x, gain, ones):
    rstd = lax.rsqrt(_segsum(x * x, ones) * (1.0 / HEAD) + EPS)
    xhat = x * rstd
    return xhat * gain, xhat, rstd


def _expand(x, odd, lo):
    if odd:
        xl = pltpu.roll(jnp.where(lo, 0.0, x), HEAD, axis=1)
    else:
        xl = jnp.where(lo, x, 0.0)
    xh = pltpu.roll(xl, HEAD, axis=1)
    return jnp.concatenate([xl, xh], axis=0).astype(BF16)


def _fold(g2, odd, lo):
    r = g2.shape[0] // 2
    s = jnp.where(lo, g2[:r], 0.0) + pltpu.roll(jnp.where(lo, 0.0, g2[r:]), HEAD, axis=1)
    if odd:
        s = pltpu.roll(s, HEAD, axis=1)
    return s


def _attn_mask(n):
    qi = lax.broadcasted_iota(jnp.int32, (BLK, 2 * BLK), 0)
    sj = lax.broadcasted_iota(jnp.int32, (BLK, 2 * BLK), 1)
    rel = qi + BLK - sj
    return (rel >= 0) & (rel < BLK) & ((sj >= BLK) | (n > 0))


def _attn_specs(t, aw, kv):
    prev = lambda n: jnp.maximum(n - 1, 0)
    kb, vb = aw // kv, aw // kv + 1
    return [
        pl.BlockSpec(memory_space=pltpu.SMEM),
        pl.BlockSpec((BLK, aw), lambda n: (n, 0)),
        pl.BlockSpec((BLK, kv), lambda n: (prev(n), kb)),
        pl.BlockSpec((BLK, kv), lambda n: (n, kb)),
        pl.BlockSpec((BLK, kv), lambda n: (prev(n), vb)),
        pl.BlockSpec((BLK, kv), lambda n: (n, vb)),
        pl.BlockSpec((1, BLK), lambda n: (0, 0)),
        pl.BlockSpec((1, BLK), lambda n: (0, 0)),
    ]


def _softmax_pair(s2, valid, sink0, sink1):
    out, psink = [], []
    for half, sink in ((0, sink0), (1, sink1)):
        s = jnp.where(valid, s2[:, 2 * BLK * half:2 * BLK * (half + 1)], NEG_INF)
        m = jnp.maximum(jnp.max(s, axis=-1, keepdims=True), sink)
        p = jnp.exp(s - m)
        es = jnp.exp(sink - m)
        inv = 1.0 / (jnp.sum(p, axis=-1, keepdims=True) + es)
        out.append(p * inv)
        psink.append(es * inv)
    return jnp.concatenate(out, axis=1), psink


def _attn_fwd(proj, qg, kg, sinks, aw, kv):
    t = proj.shape[0]
    n_pairs, n_kvblk = aw // BLK, kv // BLK

    def body(sink_ref, q_ref, kp_ref, kc_ref, vp_ref, vc_ref, qg_ref, kg_ref, o_ref):
        n = pl.program_id(0)
        ones = _seg_ones()
        lo = lax.broadcasted_iota(jnp.int32, (1, BLK), 1) < HEAD
        valid = _attn_mask(n)
        kraw = jnp.concatenate([kp_ref[...], kc_ref[...]], axis=0)
        vraw = jnp.concatenate([vp_ref[...], vc_ref[...]], axis=0)
        for b in range(n_kvblk):
            kn = _head_rms(kraw[:, BLK * b:BLK * (b + 1)], kg_ref[...], ones)[0]
            vb = vraw[:, BLK * b:BLK * (b + 1)]
            for odd in (0, 1):
                j = 2 * b + odd
                k2 = _expand(kn, odd, lo)
                v2 = _expand(vb, odd, lo)
                for p in (2 * j, 2 * j + 1):
                    qn = _head_rms(q_ref[:, BLK * p:BLK * (p + 1)], qg_ref[...], ones)[0]
                    s2 = _nt(qn.astype(BF16), k2) * (HEAD ** -0.5)
                    p2, _ = _softmax_pair(s2, valid, sink_ref[2 * p], sink_ref[2 * p + 1])
                    o_ref[:, BLK * p:BLK * (p + 1)] = _nn(p2.astype(BF16), v2)

    del n_pairs
    return pl.pallas_call(
        body, name="attn_fwd", grid=(t // BLK,),
        in_specs=_attn_specs(t, aw, kv),
        out_specs=pl.BlockSpec((BLK, aw), lambda n: (n, 0)),
        out_shape=SDS((t, aw), F32),
        compiler_params=_params("arbitrary"),
    )(sinks, proj, proj, proj, proj, proj, qg, kg)


def _attn_bwd(proj, dy, qg, kg, sinks, aw, kv, after=None):
    t = proj.shape[0]
    nb = t // BLK
    n_kvblk = kv // BLK
    kb = aw // kv

    def body(sink_ref, q_ref, kp_ref, kc_ref, vp_ref, vc_ref, qg_ref, kg_ref, dy_ref, kall_ref,
             dqkv_ref, dstat_ref, dk_acc, dv_acc, dqg_acc):
        n = pl.program_id(0)
        ones = _seg_ones()
        lane = lax.broadcasted_iota(jnp.int32, (1, BLK), 1)
        lo = lane < HEAD
        valid = _attn_mask(n)

        @pl.when(n == 0)
        def _():
            dk_acc[...] = jnp.zeros_like(dk_acc)
            dv_acc[...] = jnp.zeros_like(dv_acc)
            dqg_acc[...] = jnp.zeros_like(dqg_acc)
            dstat_ref[...] = jnp.zeros_like(dstat_ref)

        kraw = jnp.concatenate([kp_ref[...], kc_ref[...]], axis=0)
        vraw = jnp.concatenate([vp_ref[...], vc_ref[...]], axis=0)
        row = pl.multiple_of(n * BLK, BLK)
        prow = pl.multiple_of(jnp.maximum(n - 1, 0) * BLK, BLK)
        dsink = jnp.zeros((1, BLK), F32)
        for b in range(n_kvblk):
            kn = _head_rms(kraw[:, BLK * b:BLK * (b + 1)], kg_ref[...], ones)[0]
            vb = vraw[:, BLK * b:BLK * (b + 1)]
            dkn = jnp.zeros((2 * BLK, BLK), F32)
            dvb = jnp.zeros((2 * BLK, BLK), F32)
            for odd in (0, 1):
                j = 2 * b + odd
                k2 = _expand(kn, odd, lo)
                v2 = _expand(vb, odd, lo)
                dk2 = jnp.zeros((4 * BLK, BLK), F32)
                dv2 = jnp.zeros((4 * BLK, BLK), F32)
                for p in (2 * j, 2 * j + 1):
                    cols = slice(BLK * p, BLK * (p + 1))
                    qn, qhat, rstd = _head_rms(q_ref[:, cols], qg_ref[...], ones)
                    qb = qn.astype(BF16)
                    s2 = _nt(qb, k2) * (HEAD ** -0.5)
                    p2, psink = _softmax_pair(s2, valid, sink_ref[2 * p], sink_ref[2 * p + 1])
                    dob = dy_ref[:, cols].astype(BF16)
                    dp2 = _nt(dob, v2)
                    ds = []
                    for half in (0, 1):
                        hs = slice(2 * BLK * half, 2 * BLK * (half + 1))
                        ph = p2[:, hs]
                        delta = jnp.sum(ph * dp2[:, hs], axis=-1, keepdims=True)
                        ds.append(ph * (dp2[:, hs] - delta))
                        dsk = -jnp.sum(psink[half] * delta, axis=0, keepdims=True)
                        dsink = dsink + jnp.where(lane == 2 * p + half, dsk, 0.0)
                    ds2 = (jnp.concatenate(ds, axis=1) * (HEAD ** -0.5)).astype(BF16)
                    dqn = _nn(ds2, k2)
                    dk2 = dk2 + _tn(ds2, qb)
                    dv2 = dv2 + _tn(p2.astype(BF16), dob)
                    dqhat = dqn * qg_ref[...]
                    proj_q = _segsum(dqhat * qhat, ones) * (1.0 / HEAD)
                    dqkv_ref[pl.ds(row, BLK), cols] = (rstd * (dqhat - qhat * proj_q)).astype(BF16)
                    dqg_acc[:, cols] += jnp.sum(dqn * qhat, axis=0, keepdims=True)
                dkn = dkn + _fold(dk2, odd, lo)
                dvb = dvb + _fold(dv2, odd, lo)
            kcols = slice(BLK * b, BLK * (b + 1))
            dk_acc[pl.ds(prow, BLK), kcols] += dkn[:BLK]
            dv_acc[pl.ds(prow, BLK), kcols] += dvb[:BLK]
            dk_acc[pl.ds(row, BLK), kcols] += dkn[BLK:]
            dv_acc[pl.ds(row, BLK), kcols] += dvb[BLK:]
        dstat_ref[2:3, :] += dsink

        @pl.when(n == nb - 1)
        def _():
            dqg = dqg_acc[:, 0:BLK]
            for p in range(1, aw // BLK):
                dqg = dqg + dqg_acc[:, BLK * p:BLK * (p + 1)]
            dstat_ref[0:1, :] = dqg + pltpu.roll(dqg, HEAD, axis=1)

            def kblock(i, dkg):
                r = pl.multiple_of(i * BLK, BLK)
                for b in range(n_kvblk):
                    kcols = slice(BLK * b, BLK * (b + 1))
                    _, khat, rstd = _head_rms(kall_ref[pl.ds(r, BLK), kcols], kg_ref[...], ones)
                    dkn = dk_acc[pl.ds(r, BLK), kcols]
                    dkhat = dkn * kg_ref[...]
                    proj_k = _segsum(dkhat * khat, ones) * (1.0 / HEAD)
                    dqkv_ref[pl.ds(r, BLK), aw + BLK * b:aw + BLK * (b + 1)] = (rstd * (dkhat - khat * proj_k)).astype(BF16)
                    dqkv_ref[pl.ds(r, BLK), aw + kv + BLK * b:aw + kv + BLK * (b + 1)] = dv_acc[pl.ds(r, BLK), kcols].astype(BF16)
                    dkg = dkg + jnp.sum(dkn * khat, axis=0, keepdims=True)
                return dkg

            dkg = lax.fori_loop(0, nb, kblock, jnp.zeros((1, BLK), F32))
            dstat_ref[1:2, :] = dkg + pltpu.roll(dkg, HEAD, axis=1)

    in_specs = _attn_specs(t, aw, kv) + [
        pl.BlockSpec((BLK, aw), lambda n: (n, 0)),
        pl.BlockSpec((t, kv), lambda n: (0, kb)),
    ]
    args = [sinks, proj, proj, proj, proj, proj, qg, kg, dy, proj]
    body, in_specs, args = _ordered_behind(body, len(args), in_specs, args, after)
    return pl.pallas_call(
        body, name="attn_bwd", grid=(nb,),
        in_specs=in_specs,
        out_specs=[pl.BlockSpec((t, aw + 2 * kv), lambda n: (0, 0)), pl.BlockSpec((8, BLK), lambda n: (0, 0))],
        out_shape=[SDS((t, aw + 2 * kv), BF16), SDS((8, BLK), F32)],
        scratch_shapes=[pltpu.VMEM((t, kv), F32), pltpu.VMEM((t, kv), F32), pltpu.VMEM((1, aw), F32)],
        compiler_params=_params("arbitrary"),
    )(*args)


def _conv_taps(win, w_ref, shift_of):
    rows = win.shape[0]
    acc = None
    for j in range(CONV_K):
        term = pltpu.roll(win, (rows - shift_of(j)) % rows, axis=0)[:BLK] * w_ref[j:j + 1, :]
        acc = term if acc is None else acc + term
    return acc


def _layer_norm_fwd(z):
    mu = jnp.mean(z, axis=-1, keepdims=True)
    zc = z - mu
    rstd = lax.rsqrt(jnp.mean(zc * zc, axis=-1, keepdims=True) + EPS)
    return zc * rstd, rstd


def _layer_norm_bwd(dy, yhat, rstd, g):
    dyh = dy * g
    return rstd * (dyh - jnp.mean(dyh, axis=-1, keepdims=True) - yhat * jnp.mean(dyh * yhat, axis=-1, keepdims=True))


def _conv_fill_glu(a_ref, g_ref, hpad, nb):
    hpad[0:HALO, :] = jnp.zeros((HALO, hpad.shape[1]), F32)

    def fill(i, c):
        r = pl.multiple_of(i * BLK, BLK)
        hpad[pl.ds(pl.multiple_of(r + HALO, HALO), BLK), :] = a_ref[pl.ds(r, BLK), :] * _sigmoid(g_ref[pl.ds(r, BLK), :])
        return c

    lax.fori_loop(0, nb, fill, 0)


def _conv_specs(t, cw, d_in):
    base = (d_in - 4 * cw) // cw
    vec = pl.BlockSpec((1, cw), lambda i: (0, 0))
    return [
        pl.BlockSpec((t, cw), lambda i: (0, base)),
        pl.BlockSpec((t, cw), lambda i: (0, base + 1)),
        pl.BlockSpec((HALO, cw), lambda i: (0, 0)),
        vec, vec, vec,
    ]


def _conv_fwd(proj, cw_pad, cb, lg, lb, cw):
    t, d_in = proj.shape
    nb = t // BLK

    def body(a_ref, g_ref, w_ref, b_ref, lg_ref, lb_ref, o_ref, hpad):
        _conv_fill_glu(a_ref, g_ref, hpad, nb)

        def blk(i, c):
            r = pl.multiple_of(i * BLK, BLK)
            z = _conv_taps(hpad[pl.ds(r, BLK + HALO), :], w_ref, lambda j: j + HALO - (CONV_K - 1)) + b_ref[...]
            yhat, _ = _layer_norm_fwd(z)
            y = yhat * lg_ref[...] + lb_ref[...]
            o_ref[pl.ds(r, BLK), :] = y * _sigmoid(y)
            return c

        lax.fori_loop(0, nb, blk, 0)

    return pl.pallas_call(
        body, name="conv_fwd", grid=(1,),
        in_specs=_conv_specs(t, cw, d_in),
        out_specs=pl.BlockSpec((t, cw), lambda i: (0, 0)),
        out_shape=SDS((t, cw), F32),
        scratch_shapes=[pltpu.VMEM((t + HALO, cw), F32)],
        compiler_params=_params("arbitrary"),
    )(proj, proj, cw_pad, cb, lg, lb)


def _conv_bwd(proj, dy, cw_pad, cb, lg, lb, cw):
    t, d_in = proj.shape
    nb = t // BLK

    def body(a_ref, g_ref, w_ref, b_ref, lg_ref, lb_ref, dy_ref, dc_ref, dw_ref, dvec_ref, hpad, dzpad, dwacc):
        _conv_fill_glu(a_ref, g_ref, hpad, nb)
        dzpad[t:t + HALO, :] = jnp.zeros((HALO, cw), F32)
        dwacc[...] = jnp.zeros_like(dwacc)

        def blk(i, carry):
            db, dlg, dlb = carry
            r = pl.multiple_of(i * BLK, BLK)
            win = hpad[pl.ds(r, BLK + HALO), :]
            z = _conv_taps(win, w_ref, lambda j: j + HALO - (CONV_K - 1)) + b_ref[...]
            yhat, rstd = _layer_norm_fwd(z)
            y = yhat * lg_ref[...] + lb_ref[...]
            sg = _sigmoid(y)
            dyl = dy_ref[pl.ds(r, BLK), :] * (sg * (1.0 + y * (1.0 - sg)))
            dz = _layer_norm_bwd(dyl, yhat, rstd, lg_ref[...])
            dzpad[pl.ds(r, BLK), :] = dz
            for j in range(CONV_K):
                sh = j + HALO - (CONV_K - 1)
                prod = dz * pltpu.roll(win, (BLK + HALO - sh) % (BLK + HALO), axis=0)[:BLK]
                dwacc[8 * j:8 * j + 8, :] += jnp.sum(prod.reshape(BLK // 8, 8, cw), axis=0)
            return (db + jnp.sum(dz, axis=0, keepdims=True),
                    dlg + jnp.sum(dyl * yhat, axis=0, keepdims=True),
                    dlb + jnp.sum(dyl, axis=0, keepdims=True))

        zero = jnp.zeros((1, cw), F32)
        db, dlg, dlb = lax.fori_loop(0, nb, blk, (zero, zero, zero))
        dvec_ref[...] = jnp.zeros_like(dvec_ref)
        dvec_ref[0:1, :] = db
        dvec_ref[1:2, :] = dlg
        dvec_ref[2:3, :] = dlb
        dw_ref[...] = jnp.sum(dwacc[...].reshape(HALO, 8, cw), axis=1)

        def blk2(i, c):
            r = pl.multiple_of(i * BLK, BLK)
            dh = _conv_taps(dzpad[pl.ds(r, BLK + HALO), :], w_ref, lambda j: CONV_K - 1 - j)
            a = a_ref[pl.ds(r, BLK), :]
            sg = _sigmoid(g_ref[pl.ds(r, BLK), :])
            dc_ref[pl.ds(r, BLK), 0:cw] = (dh * sg).astype(BF16)
            dc_ref[pl.ds(r, BLK), cw:2 * cw] = (dh * a * sg * (1.0 - sg)).astype(BF16)
            return c

        lax.fori_loop(0, nb, blk2, 0)

    return pl.pallas_call(
        body, name="conv_bwd", grid=(1,),
        in_specs=_conv_specs(t, cw, d_in) + [pl.BlockSpec((t, cw), lambda i: (0, 0))],
        out_specs=[pl.BlockSpec((t, 2 * cw), lambda i: (0, 0)), pl.BlockSpec((HALO, cw), lambda i: (0, 0)),
                   pl.BlockSpec((8, cw), lambda i: (0, 0))],
        out_shape=[SDS((t, 2 * cw), BF16), SDS((HALO, cw), F32), SDS((8, cw), F32)],
        scratch_shapes=[pltpu.VMEM((t + HALO, cw), F32), pltpu.VMEM((t + HALO, cw), F32), pltpu.VMEM((8 * HALO, cw), F32)],
        compiler_params=_params("arbitrary"),
    )(proj, proj, cw_pad, cb, lg, lb, dy)


def _tril_bf16(w):
    r = lax.broadcasted_iota(jnp.int32, (BLK, BLK), 0)
    c = lax.broadcasted_iota(jnp.int32, (BLK, BLK), 1)
    return jnp.where(r >= c, w, 0.0).astype(BF16)


def _sgu_specs(sw, d_in, heads):
    base = (d_in - 2 * sw) // sw
    vec = pl.BlockSpec((1, sw), lambda n: (0, 0))
    return [
        pl.BlockSpec((BLK, sw), lambda n: (n, base)),
        pl.BlockSpec((BLK, sw), lambda n: (n, base + 1)),
        vec, vec,
        pl.BlockSpec((heads, BLK, BLK), lambda n: (0, 0, 0)),
        pl.BlockSpec((BLK, sw), lambda n: (0, 0)),
    ]


def _sgu_mix(w_ref, vnb, heads, sw, transpose):
    head_of = lax.broadcasted_iota(jnp.int32, (1, sw), 1) // HEAD
    s = jnp.zeros((BLK, sw), F32)
    for h in range(heads):
        wt = _tril_bf16(w_ref[h])
        mixed = _tn(wt, vnb) if transpose else _nn(wt, vnb)
        s = jnp.where(head_of == h, mixed, s)
    return s


def _sgu_fwd(proj, lg, lb, w, bias_full, sw):
    t, d_in = proj.shape
    heads = sw // HEAD

    def body(u_ref, v_ref, lg_ref, lb_ref, w_ref, bias_ref, o_ref):
        vhat, _ = _layer_norm_fwd(v_ref[...])
        vn = (vhat * lg_ref[...] + lb_ref[...]).astype(BF16)
        s = _sgu_mix(w_ref, vn, heads, sw, False) + bias_ref[...]
        o_ref[...] = u_ref[...] * s

    return pl.pallas_call(
        body, name="sgu_fwd", grid=(t // BLK,),
        in_specs=_sgu_specs(sw, d_in, heads),
        out_specs=pl.BlockSpec((BLK, sw), lambda n: (n, 0)),
        out_shape=SDS((t, sw), F32),
        compiler_params=_params("arbitrary"),
    )(proj, proj, lg, lb, w, bias_full)


def _sgu_bwd(proj, dy, lg, lb, w, bias_full, sw, after=None):
    t, d_in = proj.shape
    heads = sw // HEAD
    nb = t // BLK

    def body(u_ref, v_ref, lg_ref, lb_ref, w_ref, bias_ref, dy_ref, ds_ref, dw_ref, db_ref, dvec_ref, dbfull):
        n = pl.program_id(0)

        @pl.when(n == 0)
        def _():
            dw_ref[...] = jnp.zeros_like(dw_ref)
            dvec_ref[...] = jnp.zeros_like(dvec_ref)
            dbfull[...] = jnp.zeros_like(dbfull)

        vhat, rstd = _layer_norm_fwd(v_ref[...])
        vn = (vhat * lg_ref[...] + lb_ref[...]).astype(BF16)
        s = _sgu_mix(w_ref, vn, heads, sw, False) + bias_ref[...]
        dy = dy_ref[...]
        ds_ref[:, 0:sw] = (dy * s).astype(BF16)
        dsv = dy * u_ref[...]
        dbfull[...] += dsv
        head_of = lax.broadcasted_iota(jnp.int32, (1, sw), 1) // HEAD
        r = lax.broadcasted_iota(jnp.int32, (BLK, BLK), 0)
        c = lax.broadcasted_iota(jnp.int32, (BLK, BLK), 1)
        dsb = dsv.astype(BF16)
        for h in range(heads):
            dsh = jnp.where(head_of == h, dsv, 0.0).astype(BF16)
            dw_ref[h] += jnp.where(r >= c, _nt(dsh, vn), 0.0)
        dvn = _sgu_mix(w_ref, dsb, heads, sw, True)
        dvec_ref[0:1, :] += jnp.sum(dvn * vhat, axis=0, keepdims=True)
        dvec_ref[1:2, :] += jnp.sum(dvn, axis=0, keepdims=True)
        ds_ref[:, sw:2 * sw] = _layer_norm_bwd(dvn, vhat, rstd, lg_ref[...]).astype(BF16)

        @pl.when(n == nb - 1)
        def _():
            sel = (lax.broadcasted_iota(jnp.int32, (sw, BLK), 0) // HEAD == lax.broadcasted_iota(jnp.int32, (sw, BLK), 1)).astype(BF16)
            x = dbfull[...]
            hi = x.astype(BF16)
            r1 = x - hi.astype(F32)
            mid = r1.astype(BF16)
            low = (r1 - mid.astype(F32)).astype(BF16)
            db_ref[...] = _nn(hi, sel) + _nn(mid, sel) + _nn(low, sel)

    in_specs = _sgu_specs(sw, d_in, heads) + [pl.BlockSpec((BLK, sw), lambda n: (n, 0))]
    body, in_specs, args = _ordered_behind(body, 7, in_specs, [proj, proj, lg, lb, w, bias_full, dy], after)
    return pl.pallas_call(
        body, name="sgu_bwd", grid=(nb,),
        in_specs=in_specs,
        out_specs=[pl.BlockSpec((BLK, 2 * sw), lambda n: (n, 0)), pl.BlockSpec((heads, BLK, BLK), lambda n: (0, 0, 0)),
                   pl.BlockSpec((BLK, BLK), lambda n: (0, 0)), pl.BlockSpec((8, sw), lambda n: (0, 0))],
        out_shape=[SDS((t, 2 * sw), BF16), SDS((heads, BLK, BLK), F32), SDS((BLK, BLK), F32), SDS((8, sw), F32)],
        scratch_shapes=[pltpu.VMEM((BLK, sw), F32)],
        compiler_params=_params("arbitrary"),
    )(*args)


def _rms_fwd(x, g):
    return (x * lax.rsqrt(jnp.mean(x * x, axis=-1, keepdims=True) + EPS)) * g


def _rms_bwd(dh, x, g):
    rstd = lax.rsqrt(jnp.mean(x * x, axis=-1, keepdims=True) + EPS)
    xhat = x * rstd
    dgx = dh * g
    dx = rstd * (dgx - xhat * jnp.mean(dgx * xhat, axis=-1, keepdims=True))
    return dx, jnp.sum(dh * xhat, axis=0, keepdims=True)


def _rms_matmul(x, g, w, tm, tn, relu2, name, transposed=False):
    t, d = x.shape
    if w.ndim == 3:
        assert w.shape[2] == tn
        n = w.shape[0] * tn
        w_spec = pl.BlockSpec((None, d, tn), lambda i, j: (j, 0, 0))
    elif transposed:
        n = w.shape[0]
        w_spec = pl.BlockSpec((tn, d), lambda i, j: (j, 0))
    else:
        n = w.shape[1]
        w_spec = pl.BlockSpec((d, tn), lambda i, j: (0, j))

    def body(x_ref, g_ref, w_ref, h_ref, *outs):
        @pl.when(pl.program_id(1) == 0)
        def _():
            h_ref[...] = _rms_fwd(x_ref[...], g_ref[...]).astype(BF16)

        acc = _nt(h_ref[...], w_ref[...]) if transposed else _nn(h_ref[...], w_ref[...])
        if relu2:
            r = jnp.maximum(acc, 0.0)
            outs[0][...] = (r * r).astype(BF16)
            outs[1][...] = r.astype(BF16)
        else:
            outs[0][...] = acc

    tile = pl.BlockSpec((tm, tn), lambda i, j: (i, j))
    row = pl.BlockSpec((tm, d), lambda i, j: (i, 0))
    outs = [SDS((t, n), BF16), SDS((t, n), BF16)] if relu2 else [SDS((t, n), F32)]
    return pl.pallas_call(
        body, name=name, grid=(t // tm, n // tn),
        in_specs=[row, pl.BlockSpec((1, d), lambda i, j: (0, 0)), w_spec],
        out_specs=[row] + [tile] * len(outs),
        out_shape=[SDS((t, d), BF16)] + outs,
        compiler_params=_params("parallel", "arbitrary"),
    )(x, g, w)


def _group_rms_matmul(ys, g, w, res, tm, tn):
    t, d = res.shape
    widths = [y.shape[1] for y in ys]
    k = sum(widths)

    def body(*refs):
        y_refs, (g_ref, w_ref, res_ref, mix_ref, o_ref) = refs[:len(ys)], refs[len(ys):]

        @pl.when(pl.program_id(1) == 0)
        def _():
            c = 0
            for y_ref, wd in zip(y_refs, widths):
                mix_ref[:, c:c + wd] = _rms_fwd(y_ref[...], g_ref[:, c:c + wd]).astype(BF16)
                c += wd

        o_ref[...] = res_ref[...] + _nn(mix_ref[...], w_ref[...])

    tile = pl.BlockSpec((tm, tn), lambda i, j: (i, j))
    return pl.pallas_call(
        body, name="mix_out", grid=(t // tm, d // tn),
        in_specs=[pl.BlockSpec((tm, wd), lambda i, j: (i, 0)) for wd in widths] + [
            pl.BlockSpec((1, k), lambda i, j: (0, 0)), pl.BlockSpec((k, tn), lambda i, j: (0, j)), tile],
        out_specs=[pl.BlockSpec((tm, k), lambda i, j: (i, 0)), tile],
        out_shape=[SDS((t, k), BF16), SDS((t, d), F32)],
        compiler_params=_params("parallel", "arbitrary"),
    )(*ys, g, w, res)


def _matmul_res(a, w, res, tm, tn, tk):
    t, k = a.shape
    n = w.shape[1]
    nk = k // tk

    def body(a_ref, w_ref, res_ref, o_ref, acc):
        kk = pl.program_id(2)

        @pl.when(kk == 0)
        def _():
            acc[...] = res_ref[...]

        acc[...] += _nn(a_ref[...], w_ref[...])

        @pl.when(kk == nk - 1)
        def _():
            o_ref[...] = acc[...]

    tile = pl.BlockSpec((tm, tn), lambda i, j, kk: (i, j))
    return pl.pallas_call(
        body, name="mlp_down", grid=(t // tm, n // tn, nk),
        in_specs=[pl.BlockSpec((tm, tk), lambda i, j, kk: (i, kk)), pl.BlockSpec((tk, tn), lambda i, j, kk: (kk, j)), tile],
        out_specs=tile,
        out_shape=SDS((t, n), F32),
        scratch_shapes=[pltpu.VMEM((tm, tn), F32)],
        compiler_params=_params("parallel", "parallel", "arbitrary"),
    )(a, w, res)


def _loss_grad(y, target, tm):
    t, d = y.shape

    def body(y_ref, t_ref, dy_ref, dyb_ref, l_ref):
        @pl.when(pl.program_id(0) == 0)
        def _():
            l_ref[...] = jnp.zeros_like(l_ref)

        err = y_ref[...] - t_ref[...]
        dy = err * (1.0 / d)
        dy_ref[...] = dy
        dyb_ref[...] = dy.astype(BF16)
        per_row = jnp.mean(err * err, axis=-1, keepdims=True)
        l_ref[...] += jnp.sum(per_row, axis=0, keepdims=True) * 0.5

    row = pl.BlockSpec((tm, d), lambda i: (i, 0))
    return pl.pallas_call(
        body, name="loss_grad", grid=(t // tm,),
        in_specs=[row, row], out_specs=[row, row, pl.BlockSpec((8, BLK), lambda i: (0, 0))],
        out_shape=[SDS((t, d), F32), SDS((t, d), BF16), SDS((8, BLK), F32)],
        compiler_params=_params("arbitrary"),
    )(y, target)


def _mlp_dact(dxb, w_down, r, tn, after=None):
    t, d = dxb.shape
    f = w_down.shape[0]

    def body(dxb_ref, w_ref, r_ref, dup_ref):
        dup_ref[...] = (_nt(dxb_ref[...], w_ref[...]) * (2.0 * r_ref[...].astype(F32))).astype(BF16)

    tile = pl.BlockSpec((t, tn), lambda j: (0, j))
    body, in_specs, args = _ordered_behind(
        body, 3, [pl.BlockSpec((t, d), lambda j: (0, 0)), pl.BlockSpec((tn, d), lambda j: (j, 0)), tile],
        [dxb, w_down, r], after)
    return pl.pallas_call(
        body, name="mlp_dact", grid=(f // tn,),
        in_specs=in_specs, out_specs=tile,
        out_shape=SDS((t, f), BF16),
        compiler_params=_params("arbitrary"),
    )(*args)


def _grad_w(a, b, tm, tn, name, col_blocks=False, after=None, rows=None, into=None):
    t, m = a.shape
    n = b.shape[1]
    assert n % tn == 0 and m % tm == 0
    total, first = rows if rows else (m, 0)
    assert first % tm == 0
    first_tile = first // tm

    def body(a_ref, b_ref, *rest):
        rest[-1][...] = _tn(a_ref[...], b_ref[...]).astype(BF16)

    if col_blocks:
        out_spec = pl.BlockSpec((None, tm, tn), lambda i, j: (j, i, 0))
        out_shape = SDS((n // tn, m, tn), BF16)
    else:
        out_spec = pl.BlockSpec((tm, tn), lambda i, j: (first_tile + i, j))
        out_shape = SDS((total, n), BF16)
    in_specs = [pl.BlockSpec((t, tm), lambda i, j: (0, i)), pl.BlockSpec((t, tn), lambda i, j: (0, j))]
    args = [a, b]
    aliases = {}
    if into is not None:
        in_specs, args, aliases = in_specs + [ANY], args + [into], {2: 0}
    body, in_specs, args = _ordered_behind(body, len(args), in_specs, args, after)
    return pl.pallas_call(
        body, name=name, grid=(m // tm, n // tn),
        in_specs=in_specs, out_specs=out_spec, out_shape=out_shape, input_output_aliases=aliases,
        compiler_params=_params("parallel", "arbitrary"),
    )(*args)


def _mlp_dnorm(dup, w_up, x, g, dres, tm, after=None):
    t, f = dup.shape
    d = x.shape[1]
    nk, _, tk = w_up.shape

    def body(a_ref, w_ref, x_ref, g_ref, dres_ref, dx_ref, dg_ref, acc):
        i, kk = pl.program_id(0), pl.program_id(1)

        @pl.when((i == 0) & (kk == 0))
        def _():
            dg_ref[...] = jnp.zeros_like(dg_ref)

        @pl.when(kk == 0)
        def _():
            acc[...] = jnp.zeros_like(acc)

        acc[...] += _nt(a_ref[...], w_ref[...])

        @pl.when(kk == nk - 1)
        def _():
            dx, dg = _rms_bwd(acc[...], x_ref[...], g_ref[...])
            dx_ref[...] = dres_ref[...] + dx
            dg_ref[0:1, :] += dg

    row = pl.BlockSpec((tm, d), lambda i, kk: (i, 0))
    in_specs = [pl.BlockSpec((tm, tk), lambda i, kk: (i, kk)), pl.BlockSpec((None, d, tk), lambda i, kk: (kk, 0, 0)),
                row, pl.BlockSpec((1, d), lambda i, kk: (0, 0)), row]
    body, in_specs, args = _ordered_behind(body, 5, in_specs, [dup, w_up, x, g, dres], after)
    return pl.pallas_call(
        body, name="mlp_dnorm", grid=(t // tm, nk),
        in_specs=in_specs,
        out_specs=[row, pl.BlockSpec((8, d), lambda i, kk: (0, 0))],
        out_shape=[SDS((t, d), F32), SDS((8, d), F32)],
        scratch_shapes=[pltpu.VMEM((tm, d), F32)],
        compiler_params=_params("arbitrary", "arbitrary"),
    )(*args)


def _mix_dnorm(dx, w_out, ys, g, tm, after=None):
    t, d = dx.shape
    k = w_out.shape[0]
    widths = [y.shape[1] for y in ys]

    def body(dx_ref, w_ref, *refs):
        y_refs = refs[:len(ys)]
        g_ref, dxb_ref = refs[len(ys)], refs[len(ys) + 1]
        dy_refs = refs[len(ys) + 2:2 * len(ys) + 2]
        dg_ref = refs[-1]

        @pl.when(pl.program_id(0) == 0)
        def _():
            dg_ref[...] = jnp.zeros_like(dg_ref)

        dxb = dx_ref[...].astype(BF16)
        dxb_ref[...] = dxb
        dmix = _nt(dxb, w_ref[...])
        c = 0
        for y_ref, dy_ref, wd in zip(y_refs, dy_refs, widths):
            dy, dg = _rms_bwd(dmix[:, c:c + wd], y_ref[...], g_ref[:, c:c + wd])
            dy_ref[...] = dy
            dg_ref[0:1, c:c + wd] += dg
            c += wd

    row = pl.BlockSpec((tm, d), lambda i: (i, 0))
    yspecs = [pl.BlockSpec((tm, wd), lambda i: (i, 0)) for wd in widths]
    in_specs = [row, pl.BlockSpec((k, d), lambda i: (0, 0))] + yspecs + [pl.BlockSpec((1, k), lambda i: (0, 0))]
    body, in_specs, args = _ordered_behind(body, len(in_specs), in_specs, [dx, w_out, *ys, g], after)
    return pl.pallas_call(
        body, name="mix_dnorm", grid=(t // tm,),
        in_specs=in_specs,
        out_specs=[row] + yspecs + [pl.BlockSpec((8, k), lambda i: (0, 0))],
        out_shape=[SDS((t, d), BF16)] + [SDS((t, wd), F32) for wd in widths] + [SDS((8, k), F32)],
        compiler_params=_params("arbitrary"),
    )(*args)


def _in_dnorm(dps, w_in_t, x, g, dres, tm):
    t, d = x.shape
    widths = [p.shape[1] for p in dps]
    offs = [sum(widths[:p]) for p in range(len(dps))]
    n_in = w_in_t.shape[0]

    def body(*refs):
        p_refs = refs[:len(dps)]
        w_ref, x_ref, g_ref, dres_ref, dx_ref, dxb_ref, dg_ref = refs[len(dps):]

        @pl.when(pl.program_id(0) == 0)
        def _():
            dg_ref[...] = jnp.zeros_like(dg_ref)

        acc = None
        for p_ref, off, wd in zip(p_refs, offs, widths):
            term = _nn(p_ref[...], w_ref[off:off + wd, :])
            acc = term if acc is None else acc + term
        dx, dg = _rms_bwd(acc, x_ref[...], g_ref[...])
        dx = dres_ref[...] + dx
        dx_ref[...] = dx
        dxb_ref[...] = dx.astype(BF16)
        dg_ref[0:1, :] += dg

    row = pl.BlockSpec((tm, d), lambda i: (i, 0))
    return pl.pallas_call(
        body, name="in_dnorm", grid=(t // tm,),
        in_specs=[pl.BlockSpec((tm, wd), lambda i: (i, 0)) for wd in widths] + [
            pl.BlockSpec((n_in, d), lambda i: (0, 0)), row, pl.BlockSpec((1, d), lambda i: (0, 0)), row],
        out_specs=[row, row, pl.BlockSpec((8, d), lambda i: (0, 0))],
        out_shape=[SDS((t, d), F32), SDS((t, d), BF16), SDS((8, d), F32)],
        compiler_params=_params("arbitrary"),
    )(*dps, w_in_t, x, g, dres)


def _tile(n, want):
    return min(n, want)


def _row_tile(n, want):
    return max(k for k in range(8, min(n, want) + 1, 8) if n % k == 0)


def _layer_fwd(x, p, fetch, after):
    t, d = x.shape
    aw, kv, cw, sw = d // 2, d // 8, d // 4, d // 4
    tm = _tile(t, 1024)
    w_in = fetch("w_in", after)
    h1, proj = _rms_matmul(x, p["ln1_g"], w_in, tm, 512 if w_in.shape[0] % 512 == 0 else 256, False, "in_proj",
                           transposed=True)
    y_attn = _attn_fwd(proj, p["qg"], p["kg"], p["sinks"], aw, kv)
    y_conv = _conv_fwd(proj, p["conv_w"], p["conv_b"], p["conv_ln_g"], p["conv_ln_b"], cw)
    y_sgu = _sgu_fwd(proj, p["sgu_ln_g"], p["sgu_ln_b"], p["sgu_w"], p["sgu_bias"], sw)
    ys = [y_attn, y_conv, y_sgu]
    w_out = fetch("w_out", y_sgu)
    mix, x1 = _group_rms_matmul(ys, p["out_norm_g"], w_out, x, tm, _tile(d, 1024))
    w_up = fetch("w_up", x1)
    h2, act, r = _rms_matmul(x1, p["ln2_g"], w_up, tm, w_up.shape[2], True, "mlp_up")
    w_down = fetch("w_down", act)
    x2 = _matmul_res(act, w_down, x1, tm, _tile(d, 2048), 512)
    saved = dict(x=x, h1=h1, proj=proj, ys=ys, mix=mix, x1=x1, h2=h2, act=act, r=r,
                 w_in=w_in, w_out=w_out, w_up=w_up, w_down=w_down)
    return x2, saved


def _layer_bwd(dx2, dx2b, p, s, start, finish, share, carry):
    t, d = dx2.shape
    aw, kv, cw, sw = d // 2, d // 8, d // 4, d // 4
    tm = _tile(t, 512)
    dup = _mlp_dact(dx2b, s["w_down"], s["r"], 512, carry[0] if carry else None)
    tok = carry[1](dup) if carry else None
    tok = start("w_down", _grad_w(s["act"], dx2b, 512, _tile(d, 2048), "grad_w_down", after=tok))
    dx1, d_ln2 = _mlp_dnorm(dup, s["w_up"], s["x1"], p["ln2_g"], dx2, tm, tok)
    tok = finish("w_down", dx1)
    tok = start("w_up", _grad_w(s["h2"], dup, _tile(d, 1024), s["w_up"].shape[2], "grad_w_up", col_blocks=True, after=tok))
    dx1b, dya, dyc, dys, d_onorm = _mix_dnorm(dx1, s["w_out"], s["ys"], p["out_norm_g"], _tile(t, 256), tok)
    tok = finish("w_up", dx1b)
    tok = start("w_out", _grad_w(s["mix"], dx1b, 512, _tile(d, 2048), "grad_w_out", after=tok))
    dsgu, d_sw, d_sb, d_svec = _sgu_bwd(s["proj"], dys, p["sgu_ln_g"], p["sgu_ln_b"], p["sgu_w"], p["sgu_bias"], sw, tok)
    tok = share("sgu_w", d_sw, None)
    dqkv, d_attn = _attn_bwd(s["proj"], dya, p["qg"], p["kg"], p["sinks"], aw, kv, tok)
    tok = finish("w_out", dqkv)
    dconv, d_cw, d_cvec = _conv_bwd(s["proj"], dyc, p["conv_w"], p["conv_b"], p["conv_ln_g"], p["conv_ln_b"], cw)
    dps = [dqkv, dconv, dsgu]
    dx, dxb, d_ln1 = _in_dnorm(dps, s["w_in"], s["x"], p["ln1_g"], dx1, _tile(t, 256))
    heads = sw // HEAD
    tok = share("rest", dict(
        ln1_g=d_ln1[0], q_norm_g=d_attn[0, :HEAD], k_norm_g=d_attn[1, :HEAD], sinks=d_attn[2, :aw // HEAD],
        conv_w=d_cw[:CONV_K], conv_b=d_cvec[0], conv_ln_g=d_cvec[1], conv_ln_b=d_cvec[2],
        sgu_ln_g=d_svec[0], sgu_ln_b=d_svec[1], sgu_b=d_sb[:, :heads].T,
        out_norm_g=d_onorm[0], ln2_g=d_ln2[0]), tok)
    d_in, first, g_in = sum(dp.shape[1] for dp in dps), 0, None
    tm_in = 512 if all(dp.shape[1] % 512 == 0 for dp in dps) else 256
    for k, dp in enumerate(dps):
        g_in = _grad_w(dp, s["h1"], tm_in, _tile(d, 2048), f"grad_w_in_{k}", after=tok if k == 0 else None,
                       rows=(d_in, first), into=g_in)
        first += dp.shape[1]
    tok = start("w_in", g_in)
    carry = (tok, functools.partial(finish, "w_in"))
    return dx, dxb, carry


def _layer_params(l, small):
    row = lambda v: v[l][None, :]
    two = lambda v: jnp.tile(v[l], 2)[None, :]
    return dict(
        ln1_g=row(small["ln1_g"]), ln2_g=row(small["ln2_g"]), out_norm_g=row(small["out_norm_g"]),
        qg=two(small["q_norm_g"]), kg=two(small["k_norm_g"]), sinks=small["sinks"][l],
        conv_w=jnp.pad(small["conv_w"][l], ((0, HALO - CONV_K), (0, 0))),
        conv_b=row(small["conv_b"]), conv_ln_g=row(small["conv_ln_g"]), conv_ln_b=row(small["conv_ln_b"]),
        sgu_ln_g=row(small["sgu_ln_g"]), sgu_ln_b=row(small["sgu_ln_b"]), sgu_w=small["sgu_w"][l],
        sgu_bias=jnp.repeat(small["sgu_b"][l].T, HEAD, axis=1),
    )


def _local_step(x, target, small, depth, fetch, start, finish, share, after):
    params = [_layer_params(l, small) for l in range(depth)]
    saved = []
    h = x
    for l in range(depth):
        h, s = _layer_fwd(h, params[l], functools.partial(fetch, l), after)
        after = h
        saved.append(s)
    dy, dyb, lsum = _loss_grad(h, target, _tile(x.shape[0], 512))
    carry = None
    for l in reversed(range(depth)):
        dy, dyb, carry = _layer_bwd(dy, dyb, params[l], saved[l], functools.partial(start, l),
                                    functools.partial(finish, l), functools.partial(share, l), carry)
    return lsum[0, 0], dy, carry


BIG = ("w_in", "w_out", "w_up", "w_down")
HBM = pl.BlockSpec(memory_space=pltpu.HBM)
SEMS = pl.BlockSpec(memory_space=pltpu.SEMAPHORE)
EFFECT = pltpu.SideEffectType.DATAFLOW_SIDE_EFFECTING


def _mesh_pos():
    return lax.axis_index("x"), lax.axis_index("y"), lax.axis_index("c")


def _other_chips(x, y):
    return [(1 - x, y), (x, 1 - y), (1 - x, 1 - y)]


def _copies(plan, refs, sends, recvs):
    x, y, c = _mesh_pos()
    return [pltpu.make_async_remote_copy(src_ref=src, dst_ref=dst, send_sem=sends.at[k], recv_sem=recvs.at[k],
                                         device_id=dev, device_id_type=MESH)
            for k, (src, dst, dev) in enumerate(plan(x, y, c, refs))]


def _gather_plan(x, y, c, refs):
    mine = refs[0].at[4 * x + 2 * y + c]
    return [(mine, mine, (x, y, 1 - c))] + [(mine, mine, (*chip, c)) for chip in _other_chips(x, y)]


def _all_plan(x, y, c, refs):
    mine = refs[0].at[4 * x + 2 * y + c]
    return [(mine, mine, (x ^ (k >> 2), y ^ ((k >> 1) & 1), c ^ (k & 1))) for k in range(1, N_DEV)]


def _pair_plan(x, y, c, refs):
    blocks, land = refs
    return [(blocks.at[2 * q + (1 - c)], land.at[q], (x, y, 1 - c)) for q in range(4)]


def _chips_plan(x, y, c, refs):
    sums, land = refs
    return [(sums.at[2 * chip[0] + chip[1]], land.at[k], (*chip, c)) for k, chip in enumerate(_other_chips(x, y))]


def _start_exchanges(name, groups, after=None):
    flat = [a for arrays, _, _ in groups for a in arrays]
    n_arr, n_g = len(flat), len(groups)
    n_in = n_arr + (after is not None)

    def body(*refs):
        ins, sems, token = refs[:n_arr], refs[n_in:n_in + 2 * n_g], refs[-1]
        off = 0
        for gi, (arrays, plan, _) in enumerate(groups):
            for cp in _copies(plan, ins[off:off + len(arrays)], sems[2 * gi], sems[2 * gi + 1]):
                cp.start()
            off += len(arrays)
        token[...] = jnp.zeros_like(token)

    res = pl.pallas_call(
        body, name=name,
        out_shape=[pltpu.SemaphoreType.DMA((n,)) for _, _, n in groups for _ in (0, 1)]
        + [pltpu.HBM(a.shape, a.dtype) for a in flat] + [SDS((8, BLK), F32)],
        in_specs=[HBM] * n_arr + [ANY] * (after is not None),
        out_specs=[SEMS] * (2 * n_g) + [HBM] * n_arr + [pl.BlockSpec(memory_space=pltpu.VMEM)],
        input_output_aliases={i: 2 * n_g + i for i in range(n_arr)},
        compiler_params=pltpu.CompilerParams(has_side_effects=EFFECT),
    )(*[pltpu.with_memory_space_constraint(a, pltpu.HBM) for a in flat], *([after] if after is not None else []))
    sems, thru, token = res[:2 * n_g], res[2 * n_g:2 * n_g + n_arr], res[-1]
    out, off = [], 0
    for gi, (arrays, _, _) in enumerate(groups):
        out.append((list(thru[off:off + len(arrays)]), sems[2 * gi], sems[2 * gi + 1]))
        off += len(arrays)
    return out, token


def _wait_exchange(name, arrays, sends, recvs, plan, after):
    n = len(arrays)

    def body(*refs):
        for cp in _copies(plan, refs[:n], refs[n], refs[n + 1]):
            cp.wait_send()
            cp.wait_recv()

    return pl.pallas_call(
        body, name=name,
        out_shape=[pltpu.HBM(a.shape, a.dtype) for a in arrays],
        in_specs=[HBM] * n + [SEMS, SEMS, ANY],
        out_specs=[HBM] * n,
        input_output_aliases={i: i for i in range(n)},
        compiler_params=pltpu.CompilerParams(has_side_effects=EFFECT),
    )(*arrays, sends, recvs, after)


def _gather_finish(land, name):
    def body(land_ref, out_ref, send_sems, recv_sems):
        del land_ref
        x, y, c = _mesh_pos()
        cps = []
        for k, chip in enumerate(_other_chips(x, y)):
            block = out_ref.at[4 * chip[0] + 2 * chip[1] + c]
            cps.append(pltpu.make_async_remote_copy(
                src_ref=block, dst_ref=block, send_sem=send_sems.at[k], recv_sem=recv_sems.at[k],
                device_id=(x, y, 1 - c), device_id_type=MESH))
        for cp in cps:
            cp.start()
        for cp in cps:
            cp.wait()

    return pl.pallas_call(
        body, name=name,
        in_specs=[ANY], out_specs=ANY,
        out_shape=SDS(land.shape, land.dtype),
        input_output_aliases={0: 0},
        scratch_shapes=[pltpu.SemaphoreType.DMA((3,)), pltpu.SemaphoreType.DMA((3,))],
    )(land)


def _all_gather(shards, name):
    n = len(shards)

    def body(*refs):
        ins, outs = refs[:n], refs[n:2 * n]
        send_sems, recv_sems, local_sems = refs[2 * n:]
        x, y, c = _mesh_pos()
        me, sibling = (x, y, c), (x, y, 1 - c)
        chips = _other_chips(x, y)

        def copy(a, k, block, to, src=None):
            dst = outs[a].at[4 * block[0] + 2 * block[1] + block[2]]
            return pltpu.make_async_remote_copy(
                src_ref=dst if src is None else src, dst_ref=dst,
                send_sem=send_sems.at[a, k], recv_sem=recv_sems.at[a, k], device_id=to, device_id_type=MESH)

        mine = [pltpu.make_async_copy(ins[a], outs[a].at[4 * x + 2 * y + c], local_sems.at[a]) for a in range(n)]
        for cp in mine:
            cp.start()
        first = []
        for a in range(n):
            first.append(copy(a, 0, me, sibling, src=ins[a]))
            first += [copy(a, 1 + j, me, (*chip, c), src=ins[a]) for j, chip in enumerate(chips)]
        for cp in first:
            cp.start()
        passed = []
        for j, chip in enumerate(chips):
            for a in range(n):
                copy(a, 1 + j, (*chip, c), me).wait_recv()
                fwd = copy(a, 4 + j, (*chip, c), sibling)
                fwd.start()
                passed.append(fwd)
        for a in range(n):
            copy(a, 0, sibling, me).wait_recv()
            for j, chip in enumerate(chips):
                copy(a, 4 + j, (*chip, 1 - c), me).wait_recv()
        for cp in first + passed:
            cp.wait_send()
        for cp in mine:
            cp.wait()

    outs = pl.pallas_call(
        body, name=name,
        in_specs=[ANY] * n, out_specs=[ANY] * n,
        out_shape=[SDS((N_DEV,) + s.shape, s.dtype) for s in shards],
        scratch_shapes=[pltpu.SemaphoreType.DMA((n, 7)), pltpu.SemaphoreType.DMA((n, 7)), pltpu.SemaphoreType.DMA((n,))],
    )(*shards)
    return list(outs)


def _pair_add(own, got, c, name):
    _, r, cols = own.shape
    tr = _row_tile(r, 512)

    def body(c_ref, own_ref, got_ref, o_ref):
        o_ref[...] = (own_ref[...].astype(F32) + got_ref[...].astype(F32)).astype(BF16)

    return pl.pallas_call(
        body, name=name,
        grid_spec=pltpu.PrefetchScalarGridSpec(
            num_scalar_prefetch=1, grid=(4, r // tr),
            in_specs=[pl.BlockSpec((None, tr, cols), lambda q, i, c_ref: (2 * q + c_ref[0], i, 0)),
                      pl.BlockSpec((None, tr, cols), lambda q, i, c_ref: (q, i, 0))],
            out_specs=pl.BlockSpec((None, tr, cols), lambda q, i, c_ref: (q, i, 0))),
        out_shape=SDS((4, r, cols), BF16),
        compiler_params=_params("arbitrary", "arbitrary"),
    )(c, own, got)


def _adamw(w, g, m, v):
    m = ADAM_B1 * m + (1.0 - ADAM_B1) * g
    v = ADAM_B2 * v + (1.0 - ADAM_B2) * (g * g)
    m_hat = m / (1.0 - ADAM_B1 ** ADAM_STEP)
    v_hat = v / (1.0 - ADAM_B2 ** ADAM_STEP)
    delta = -ADAM_LR * (m_hat / (jnp.sqrt(v_hat) + ADAM_EPS) + ADAM_WD * w)
    return delta, m, v


def _adamw_layer(chip_sum, got, chip, w, m, v, layer, prev, name):
    depth, r, cols = w.shape
    tr = _row_tile(r, 256)

    def body(chip_ref, sum_ref, got_ref, w_ref, m_ref, v_ref, *rest):
        g_out, d_out, m_out, v_out, token = rest[-5:]
        g = sum_ref[...].astype(F32) + got_ref[0].astype(F32) + got_ref[1].astype(F32) + got_ref[2].astype(F32)
        delta, mm, vv = _adamw(w_ref[...], g, m_ref[...], v_ref[...])
        g_out[...] = g
        d_out[...] = delta
        m_out[...] = mm
        v_out[...] = vv
        token[...] = jnp.zeros_like(token)

    shard = pl.BlockSpec((None, tr, cols), lambda i, chip_ref: (layer, i, 0))
    in_specs = [pl.BlockSpec((None, tr, cols), lambda i, chip_ref: (chip_ref[0], i, 0)),
                pl.BlockSpec((3, tr, cols), lambda i, chip_ref: (0, i, 0)), shard, shard, shard]
    args = [chip, chip_sum, got, w, m, v]
    aliases = {}
    if prev is not None:
        in_specs += [ANY] * 4
        aliases = {len(args) + k: k for k in range(4)}
        args += list(prev)
    res = pl.pallas_call(
        body, name=name,
        grid_spec=pltpu.PrefetchScalarGridSpec(
            num_scalar_prefetch=1, grid=(r // tr,), in_specs=in_specs,
            out_specs=[shard] * 4 + [pl.BlockSpec((8, BLK), lambda i, chip_ref: (0, 0))]),
        out_shape=[SDS(w.shape, F32)] * 4 + [SDS((8, BLK), F32)],
        input_output_aliases=aliases,
        compiler_params=_params("arbitrary"),
    )(*args)
    return list(res[:4]), res[4]


WIDE = ("ln1_g", "out_norm_g", "ln2_g", "conv_b", "conv_ln_g", "conv_ln_b", "sgu_ln_g", "sgu_ln_b")
NARROW = ("q_norm_g", "k_norm_g", "sinks")


def _small_rows(w):
    rows, r = {}, 0
    for n in WIDE:
        rows[n] = (r, w[n].shape[1] // BLK)
        r += rows[n][1]
    for n in NARROW:
        rows[n] = (r, 1)
        r += 1
    r = -(-r // 8) * 8
    rows["sgu_b"] = (r, w["sgu_b"].shape[1])
    r += -(-rows["sgu_b"][1] // 8) * 8
    rows["conv_w"] = (r, N_DEV * HALO)
    return rows, r + N_DEV * HALO


def _pack_small(small, rows, total):
    parts, r = [], 0

    def put(name, block):
        nonlocal r
        first = rows[name][0]
        if first > r:
            parts.append(jnp.zeros((first - r, BLK), F32))
        parts.append(block)
        r = first + block.shape[0]

    for n in WIDE:
        put(n, small[n].reshape(-1, BLK))
    for n in NARROW:
        put(n, jnp.pad(small[n], (0, BLK - small[n].shape[0]))[None])
    put("sgu_b", small["sgu_b"])
    cw = small["conv_w"]
    per_dev = cw.shape[1] // N_DEV
    blocks = jnp.transpose(cw.reshape(CONV_K, N_DEV, per_dev), (1, 0, 2))
    put("conv_w", jnp.pad(blocks, ((0, 0), (0, HALO - CONV_K), (0, BLK - per_dev))).reshape(N_DEV * HALO, BLK))
    if total > r:
        parts.append(jnp.zeros((total - r, BLK), F32))
    return jnp.concatenate(parts)


def _small_update(me, packed, sgu_w_all, w, m, v, rows):
    depth = len(packed)
    total = packed[0].shape[1]
    names = SMALL + ("conv_w",)
    heads = w["sgu_w"].shape[1]
    per_dev = w["conv_w"].shape[2]

    def body(me_ref, *refs):
        packed_refs, sgu_refs = refs[:depth], refs[depth:2 * depth]
        ins = refs[2 * depth:2 * depth + 3 * len(names)]
        outs = refs[2 * depth + 3 * len(names):2 * depth + 7 * len(names)]
        acc, conv = refs[-2:]
        io = {n: (ins[3 * i:3 * i + 3], outs[4 * i:4 * i + 4]) for i, n in enumerate(names)}

        def update(n):
            (w_ref, m_ref, v_ref), (g_out, d_out, m_out, v_out) = io[n]
            delta, mm, vv = _adamw(w_ref[...], g_out[...], m_ref[...], v_ref[...])
            d_out[...] = delta
            m_out[...] = mm
            v_out[...] = vv

        mine = pl.ds(pl.multiple_of(rows["conv_w"][0] + HALO * me_ref[0], 8), HALO)
        for l in range(depth):
            s = packed_refs[l][0]
            c = packed_refs[l][0, mine, :]
            for k in range(1, N_DEV):
                s = s + packed_refs[l][k]
                c = c + packed_refs[l][k, mine, :]
            acc[l] = s
            conv[l] = c
        for n in WIDE:
            first, nr = rows[n]
            for l in range(depth):
                for j in range(nr):
                    io[n][1][0][l:l + 1, BLK * j:BLK * (j + 1)] = acc[l, first + j:first + j + 1, :]
            update(n)
        for n in NARROW:
            first, lanes = rows[n][0], w[n].shape[1]
            for l in range(depth):
                io[n][1][0][l:l + 1, :] = acc[l, first:first + 1, 0:lanes]
            update(n)
        first, nr = rows["sgu_b"]
        for l in range(depth):
            io["sgu_b"][1][0][l] = acc[l, first:first + nr, :]
            io["conv_w"][1][0][l] = conv[l, 0:CONV_K, 0:per_dev]
        update("sgu_b")
        update("conv_w")
        (w_ref, m_ref, v_ref), (g_out, d_out, m_out, v_out) = io["sgu_w"]
        for l in range(depth):
            for h in range(heads):
                g = sgu_refs[l][0, h]
                for k in range(1, N_DEV):
                    g = g + sgu_refs[l][k, h]
                delta, mm, vv = _adamw(w_ref[l, h], g, m_ref[l, h], v_ref[l, h])
                g_out[l, h] = g
                d_out[l, h] = delta
                m_out[l, h] = mm
                v_out[l, h] = vv

    def whole(a):
        nd = len(a.shape)
        return pl.BlockSpec(a.shape, lambda i, me_ref: (0,) * nd)

    small_in = [t[n] for n in names for t in (w, m, v)]
    res = pl.pallas_call(
        body, name="small_update",
        grid_spec=pltpu.PrefetchScalarGridSpec(
            num_scalar_prefetch=1, grid=(1,),
            in_specs=[whole(a) for a in list(packed) + list(sgu_w_all) + small_in],
            out_specs=[whole(w[n]) for n in names for _ in range(4)],
            scratch_shapes=[pltpu.VMEM((depth, total, BLK), F32), pltpu.VMEM((depth, HALO, BLK), F32)]),
        out_shape=[SDS(w[n].shape, F32) for n in names for _ in range(4)],
        compiler_params=_params("arbitrary"),
    )(me, *packed, *sgu_w_all, *small_in)
    return {n: list(res[4 * i:4 * i + 4]) for i, n in enumerate(names)}


def _pack(arrays):
    flat = jnp.concatenate([a.reshape(-1) for a in arrays])
    rows = -(-flat.shape[0] // (8 * BLK)) * 8
    return jnp.pad(flat, (0, rows * BLK - flat.shape[0])).reshape(rows, BLK)


def _unpack(packed, like):
    flat = packed.reshape(-1)
    out, off = [], 0
    for a in like:
        out.append(flat[off:off + a.size].reshape(a.shape))
        off += a.size
    return out


SMALL = ("ln1_g", "q_norm_g", "k_norm_g", "sinks", "conv_b", "conv_ln_g", "conv_ln_b", "sgu_ln_g", "sgu_ln_b",
         "sgu_w", "sgu_b", "out_norm_g", "ln2_g")
ORDER = ("ln1_g", "w_in", "q_norm_g", "k_norm_g", "sinks", "conv_w", "conv_b", "conv_ln_g", "conv_ln_b", "sgu_ln_g",
         "sgu_ln_b", "sgu_w", "sgu_b", "out_norm_g", "w_out", "ln2_g", "w_up", "w_down")


def _step(x, target, w, m, v):
    depth = w["ln1_g"].shape[0]
    xpos, ypos, cpos = _mesh_pos()
    me = 4 * xpos + 2 * ypos + cpos
    c_arr = jnp.reshape(cpos, (1,)).astype(jnp.int32)
    chip_arr = jnp.reshape(2 * xpos + ypos, (1,)).astype(jnp.int32)

    d = x.shape[1]
    def own_block(shard):
        return lax.dynamic_update_slice_in_dim(lax.empty((N_DEV,) + shard.shape, shard.dtype), shard[None], me, axis=0)

    cw = w["conv_w"]
    order = [(0, "conv_w")] + [(l, n) for l in range(depth) for n in BIG]
    started, gather_token = _start_exchanges("gather_start", [
        ([own_block(_pack([cw]) if n == "conv_w" else w[n][l].astype(BF16))], _gather_plan, 4) for l, n in order])
    pending = dict(zip(order, started))

    def fetch(l, n, after):
        arrays, sends, recvs = pending.pop((l, n))
        land, = _wait_exchange(f"gather_wait_{l}_{n}", arrays, sends, recvs, _gather_plan,
                               gather_token if after is None else after)
        full = _gather_finish(land, "gather_finish_" + n)
        return full if n in ("w_up", "conv_w") else full.reshape(-1, d)

    cw_all = fetch(0, "conv_w", None)
    cw_full = jnp.concatenate([_unpack(cw_all[k], [cw])[0] for k in range(N_DEV)], axis=-1)
    small = {n: w[n] for n in SMALL}
    small["conv_w"] = cw_full

    to_sibling, inflight = {}, []

    def start(l, n, g):
        blocks = g if n == "w_up" else g.reshape(N_DEV, -1, d)
        (to_sibling[l, n],), token = _start_exchanges(f"pair_start_{l}_{n}", [
            ([blocks, lax.empty((4,) + blocks.shape[1:], BF16)], _pair_plan, 4)])
        return token

    def finish(l, n, after):
        arrays, sends, recvs = to_sibling.pop((l, n))
        blocks, from_sibling = _wait_exchange(f"pair_wait_{l}_{n}", arrays, sends, recvs, _pair_plan, after)
        chip_sums = _pair_add(blocks, from_sibling, c_arr, "pair_add_" + n)
        (going,), token = _start_exchanges(f"reduce_start_{l}_{n}", [
            ([chip_sums, lax.empty((3,) + chip_sums.shape[1:], BF16)], _chips_plan, 3)])
        inflight.append((l, n, going))
        return token

    rows, total = _small_rows(w)
    shared = {}

    def share(l, kind, grads, after):
        block = grads if kind == "sgu_w" else _pack_small(grads, rows, total)
        (shared[l, kind],), token = _start_exchanges(f"small_start_{l}_{kind}", [([own_block(block)], _all_plan, N_DEV - 1)],
                                                     after)
        return token

    loss, grad_x, (after, finish_last) = _local_step(x, target, small, depth, fetch, start, finish, share, None)

    as3d = lambda a: a.reshape(depth, -1, a.shape[-1])
    results = {n: None for n in BIG}

    def settle(l, n, going, after):
        arrays, sends, recvs = going
        chip_sums, got = _wait_exchange(f"reduce_wait_{l}_{n}", arrays, sends, recvs, _chips_plan, after)
        results[n], token = _adamw_layer(chip_sums, got, chip_arr, as3d(w[n]), as3d(m[n]), as3d(v[n]), l, results[n],
                                         f"adamw_{l}_{n}")
        return token

    early, late = inflight[:2], inflight[2:]
    for l, n, going in early:
        after = settle(l, n, going, after)
    after = finish_last(after)

    landed = {}
    for key in sorted(shared):
        arrays, sends, recvs = shared[key]
        landed[key], = _wait_exchange(f"small_wait_{key[0]}_{key[1]}", arrays, sends, recvs, _all_plan, after)
        after = landed[key]
    out = _small_update(jnp.reshape(me, (1,)).astype(jnp.int32), [landed[l, "rest"] for l in range(depth)],
                        [landed[l, "sgu_w"] for l in range(depth)], w, m, v, rows)
    after = out["sinks"][1]

    for l, n, going in late + inflight[-1:]:
        after = settle(l, n, going, after)
    out.update({n: [r.reshape(w[n].shape) for r in results[n]] for n in BIG})

    loss = lax.psum(loss, ("x", "y", "c"))
    return (loss, grad_x[None]) + tuple(out[n][k] for k in range(4) for n in ORDER)


def kernel(x, ln1_g, w_in, q_norm_g, k_norm_g, sinks, conv_w, conv_b, conv_ln_g, conv_ln_b, sgu_ln_g, sgu_ln_b, sgu_w, sgu_b, out_norm_g, w_out, ln2_g, w_up, w_down, loss_target, m_ln1_g, m_w_in, m_q_norm_g, m_k_norm_g, m_sinks, m_conv_w, m_conv_b, m_conv_ln_g, m_conv_ln_b, m_sgu_ln_g, m_sgu_ln_b, m_sgu_w, m_sgu_b, m_out_norm_g, m_w_out, m_ln2_g, m_w_up, m_w_down, v_ln1_g, v_w_in, v_q_norm_g, v_k_norm_g, v_sinks, v_conv_w, v_conv_b, v_conv_ln_g, v_conv_ln_b, v_sgu_ln_g, v_sgu_ln_b, v_sgu_w, v_sgu_b, v_out_norm_g, v_w_out, v_ln2_g, v_w_up, v_w_down):
    w = dict(zip(ORDER, (ln1_g, w_in, q_norm_g, k_norm_g, sinks, conv_w, conv_b, conv_ln_g, conv_ln_b, sgu_ln_g, sgu_ln_b,
                         sgu_w, sgu_b, out_norm_g, w_out, ln2_g, w_up, w_down)))
    m = dict(zip(ORDER, (m_ln1_g, m_w_in, m_q_norm_g, m_k_norm_g, m_sinks, m_conv_w, m_conv_b, m_conv_ln_g, m_conv_ln_b,
                         m_sgu_ln_g, m_sgu_ln_b, m_sgu_w, m_sgu_b, m_out_norm_g, m_w_out, m_ln2_g, m_w_up, m_w_down)))
    v = dict(zip(ORDER, (v_ln1_g, v_w_in, v_q_norm_g, v_k_norm_g, v_sinks, v_conv_w, v_conv_b, v_conv_ln_g, v_conv_ln_b,
                         v_sgu_ln_g, v_sgu_ln_b, v_sgu_w, v_sgu_b, v_out_norm_g, v_w_out, v_ln2_g, v_w_up, v_w_down)))
    for group in (w, m, v):
        group["w_in"] = jnp.swapaxes(group["w_in"], 1, 2)
    out = list(_step(x[0], loss_target[0], w, m, v))
    for k in range(4):
        i = 2 + k * len(ORDER) + ORDER.index("w_in")
        out[i] = jnp.swapaxes(out[i], 1, 2)
    return tuple(out)
```

```python
import functools

import jax
import jax.numpy as jnp
from jax import lax
from jax.experimental import pallas as pl
from jax.experimental.pallas import tpu as pltpu

F32 = jnp.float32
BF16 = jnp.bfloat16
SDS = jax.ShapeDtypeStruct

EPS = 1e-6
NEG_INF = -1e30
HEAD = 64
BLK = 128
CONV_K = 31
HALO = 32
N_DEV = 8

ADAM_LR = 0.001
ADAM_B1 = 0.9
ADAM_B2 = 0.999
ADAM_EPS = 1e-08
ADAM_WD = 0.01
ADAM_STEP = 10

VMEM_LIMIT = 56 * 1024 * 1024

MESH = pl.DeviceIdType.MESH


def _params(*sem):
    return pltpu.CompilerParams(dimension_semantics=sem, vmem_limit_bytes=VMEM_LIMIT)


def _nt(a, b):
    return lax.dot_general(a, b, (((1,), (1,)), ((), ())), preferred_element_type=F32)


def _tn(a, b):
    return lax.dot_general(a, b, (((0,), (0,)), ((), ())), preferred_element_type=F32)


def _nn(a, b):
    return jnp.dot(a, b, preferred_element_type=F32)


def _sigmoid(x):
    return 1.0 / (1.0 + jnp.exp(-x))


ANY = pl.BlockSpec(memory_space=pl.ANY)


def _ordered_behind(body, n_in, in_specs, args, after):
    if after is None:
        return body, in_specs, args
    return (lambda *refs: body(*refs[:n_in], *refs[n_in + 1:])), list(in_specs) + [ANY], list(args) + [after]


def _seg_ones():
    r = lax.broadcasted_iota(jnp.int32, (BLK, BLK), 0) // HEAD
    c = lax.broadcasted_iota(jnp.int32, (BLK, BLK), 1) // HEAD
    return (r == c).astype(BF16)


def _segsum(x, ones):
    hi = x.astype(BF16)
    r1 = x - hi.astype(F32)
    mid = r1.astype(BF16)
    lo = (r1 - mid.astype(F32)).astype(BF16)
    return _nn(hi, ones) + _nn(mid, ones) + _nn(lo, ones)


def _head_rms(x, gain, ones):
    rstd = lax.rsqrt(_segsum(x * x, ones) * (1.0 / HEAD) + EPS)
    xhat = x * rstd
    return xhat * gain, xhat, rstd


def _expand(x, odd, lo):
    if odd:
        xl = pltpu.roll(jnp.where(lo, 0.0, x), HEAD, axis=1)
    else:
        xl = jnp.where(lo, x, 0.0)
    xh = pltpu.roll(xl, HEAD, axis=1)
    return jnp.concatenate([xl, xh], axis=0).astype(BF16)


def _fold(g2, odd, lo):
    r = g2.shape[0] // 2
    s = jnp.where(lo, g2[:r], 0.0) + pltpu.roll(jnp.where(lo, 0.0, g2[r:]), HEAD, axis=1)
    if odd:
        s = pltpu.roll(s, HEAD, axis=1)
    return s


def _attn_mask(n):
    qi = lax.broadcasted_iota(jnp.int32, (BLK, 2 * BLK), 0)
    sj = lax.broadcasted_iota(jnp.int32, (BLK, 2 * BLK), 1)
    rel = qi + BLK - sj
    return (rel >= 0) & (rel < BLK) & ((sj >= BLK) | (n > 0))


def _attn_specs(t, aw, kv):
    prev = lambda n: jnp.maximum(n - 1, 0)
    kb, vb = aw // kv, aw // kv + 1
    return [
        pl.BlockSpec(memory_space=pltpu.SMEM),
        pl.BlockSpec((BLK, aw), lambda n: (n, 0)),
        pl.BlockSpec((BLK, kv), lambda n: (prev(n), kb)),
        pl.BlockSpec((BLK, kv), lambda n: (n, kb)),
        pl.BlockSpec((BLK, kv), lambda n: (prev(n), vb)),
        pl.BlockSpec((BLK, kv), lambda n: (n, vb)),
        pl.BlockSpec((1, BLK), lambda n: (0, 0)),
        pl.BlockSpec((1, BLK), lambda n: (0, 0)),
    ]


def _softmax_pair(s2, valid, sink0, sink1):
    out, psink = [], []
    for half, sink in ((0, sink0), (1, sink1)):
        s = jnp.where(valid, s2[:, 2 * BLK * half:2 * BLK * (half + 1)], NEG_INF)
        m = jnp.maximum(jnp.max(s, axis=-1, keepdims=True), sink)
        p = jnp.exp(s - m)
        es = jnp.exp(sink - m)
        inv = 1.0 / (jnp.sum(p, axis=-1, keepdims=True) + es)
        out.append(p * inv)
        psink.append(es * inv)
    return jnp.concatenate(out, axis=1), psink


def _attn_fwd(proj, qg, kg, sinks, aw, kv):
    t = proj.shape[0]
    n_pairs, n_kvblk = aw // BLK, kv // BLK

    def body(sink_ref, q_ref, kp_ref, kc_ref, vp_ref, vc_ref, qg_ref, kg_ref, o_ref):
        n = pl.program_id(0)
        ones = _seg_ones()
        lo = lax.broadcasted_iota(jnp.int32, (1, BLK), 1) < HEAD
        valid = _attn_mask(n)
        kraw = jnp.concatenate([kp_ref[...], kc_ref[...]], axis=0)
        vraw = jnp.concatenate([vp_ref[...], vc_ref[...]], axis=0)
        for b in range(n_kvblk):
            kn = _head_rms(kraw[:, BLK * b:BLK * (b + 1)], kg_ref[...], ones)[0]
            vb = vraw[:, BLK * b:BLK * (b + 1)]
            for odd in (0, 1):
                j = 2 * b + odd
                k2 = _expand(kn, odd, lo)
                v2 = _expand(vb, odd, lo)
                for p in (2 * j, 2 * j + 1):
                    qn = _head_rms(q_ref[:, BLK * p:BLK * (p + 1)], qg_ref[...], ones)[0]
                    s2 = _nt(qn.astype(BF16), k2) * (HEAD ** -0.5)
                    p2, _ = _softmax_pair(s2, valid, sink_ref[2 * p], sink_ref[2 * p + 1])
                    o_ref[:, BLK * p:BLK * (p + 1)] = _nn(p2.astype(BF16), v2)

    del n_pairs
    return pl.pallas_call(
        body, name="attn_fwd", grid=(t // BLK,),
        in_specs=_attn_specs(t, aw, kv),
        out_specs=pl.BlockSpec((BLK, aw), lambda n: (n, 0)),
        out_shape=SDS((t, aw), F32),
        compiler_params=_params("arbitrary"),
    )(sinks, proj, proj, proj, proj, proj, qg, kg)


def _attn_bwd(proj, dy, qg, kg, sinks, aw, kv, after=None):
    t = proj.shape[0]
    nb = t // BLK
    n_kvblk = kv // BLK
    kb = aw // kv

    def body(sink_ref, q_ref, kp_ref, kc_ref, vp_ref, vc_ref, qg_ref, kg_ref, dy_ref, kall_ref,
             dqkv_ref, dstat_ref, dk_acc, dv_acc, dqg_acc):
        n = pl.program_id(0)
        ones = _seg_ones()
        lane = lax.broadcasted_iota(jnp.int32, (1, BLK), 1)
        lo = lane < HEAD
        valid = _attn_mask(n)

        @pl.when(n == 0)
        def _():
            dk_acc[...] = jnp.zeros_like(dk_acc)
            dv_acc[...] = jnp.zeros_like(dv_acc)
            dqg_acc[...] = jnp.zeros_like(dqg_acc)
            dstat_ref[...] = jnp.zeros_like(dstat_ref)

        kraw = jnp.concatenate([kp_ref[...], kc_ref[...]], axis=0)
        vraw = jnp.concatenate([vp_ref[...], vc_ref[...]], axis=0)
        row = pl.multiple_of(n * BLK, BLK)
        prow = pl.multiple_of(jnp.maximum(n - 1, 0) * BLK, BLK)
        dsink = jnp.zeros((1, BLK), F32)
        for b in range(n_kvblk):
            kn = _head_rms(kraw[:, BLK * b:BLK * (b + 1)], kg_ref[...], ones)[0]
            vb = vraw[:, BLK * b:BLK * (b + 1)]
            dkn = jnp.zeros((2 * BLK, BLK), F32)
            dvb = jnp.zeros((2 * BLK, BLK), F32)
            for odd in (0, 1):
                j = 2 * b + odd
                k2 = _expand(kn, odd, lo)
                v2 = _expand(vb, odd, lo)
                dk2 = jnp.zeros((4 * BLK, BLK), F32)
                dv2 = jnp.zeros((4 * BLK, BLK), F32)
                for p in (2 * j, 2 * j + 1):
                    cols = slice(BLK * p, BLK * (p + 1))
                    qn, qhat, rstd = _head_rms(q_ref[:, cols], qg_ref[...], ones)
                    qb = qn.astype(BF16)
                    s2 = _nt(qb, k2) * (HEAD ** -0.5)
                    p2, psink = _softmax_pair(s2, valid, sink_ref[2 * p], sink_ref[2 * p + 1])
                    dob = dy_ref[:, cols].astype(BF16)
                    dp2 = _nt(dob, v2)
                    ds = []
                    for half in (0, 1):
                        hs = slice(2 * BLK * half, 2 * BLK * (half + 1))
                        ph = p2[:, hs]
                        delta = jnp.sum(ph * dp2[:, hs], axis=-1, keepdims=True)
                        ds.append(ph * (dp2[:, hs] - delta))
                        dsk = -jnp.sum(psink[half] * delta, axis=0, keepdims=True)
                        dsink = dsink + jnp.where(lane == 2 * p + half, dsk, 0.0)
                    ds2 = (jnp.concatenate(ds, axis=1) * (HEAD ** -0.5)).astype(BF16)
                    dqn = _nn(ds2, k2)
                    dk2 = dk2 + _tn(ds2, qb)
                    dv2 = dv2 + _tn(p2.astype(BF16), dob)
                    dqhat = dqn * qg_ref[...]
                    proj_q = _segsum(dqhat * qhat, ones) * (1.0 / HEAD)
                    dqkv_ref[pl.ds(row, BLK), cols] = (rstd * (dqhat - qhat * proj_q)).astype(BF16)
                    dqg_acc[:, cols] += jnp.sum(dqn * qhat, axis=0, keepdims=True)
                dkn = dkn + _fold(dk2, odd, lo)
                dvb = dvb + _fold(dv2, odd, lo)
            kcols = slice(BLK * b, BLK * (b + 1))
            dk_acc[pl.ds(prow, BLK), kcols] += dkn[:BLK]
            dv_acc[pl.ds(prow, BLK), kcols] += dvb[:BLK]
            dk_acc[pl.ds(row, BLK), kcols] += dkn[BLK:]
            dv_acc[pl.ds(row, BLK), kcols] += dvb[BLK:]
        dstat_ref[2:3, :] += dsink

        @pl.when(n == nb - 1)
        def _():
            dqg = dqg_acc[:, 0:BLK]
            for p in range(1, aw // BLK):
                dqg = dqg + dqg_acc[:, BLK * p:BLK * (p + 1)]
            dstat_ref[0:1, :] = dqg + pltpu.roll(dqg, HEAD, axis=1)

            def kblock(i, dkg):
                r = pl.multiple_of(i * BLK, BLK)
                for b in range(n_kvblk):
                    kcols = slice(BLK * b, BLK * (b + 1))
                    _, khat, rstd = _head_rms(kall_ref[pl.ds(r, BLK), kcols], kg_ref[...], ones)
                    dkn = dk_acc[pl.ds(r, BLK), kcols]
                    dkhat = dkn * kg_ref[...]
                    proj_k = _segsum(dkhat * khat, ones) * (1.0 / HEAD)
                    dqkv_ref[pl.ds(r, BLK), aw + BLK * b:aw + BLK * (b + 1)] = (rstd * (dkhat - khat * proj_k)).astype(BF16)
                    dqkv_ref[pl.ds(r, BLK), aw + kv + BLK * b:aw + kv + BLK * (b + 1)] = dv_acc[pl.ds(r, BLK), kcols].astype(BF16)
                    dkg = dkg + jnp.sum(dkn * khat, axis=0, keepdims=True)
                return dkg

            dkg = lax.fori_loop(0, nb, kblock, jnp.zeros((1, BLK), F32))
            dstat_ref[1:2, :] = dkg + pltpu.roll(dkg, HEAD, axis=1)

    in_specs = _attn_specs(t, aw, kv) + [
        pl.BlockSpec((BLK, aw), lambda n: (n, 0)),
        pl.BlockSpec((t, kv), lambda n: (0, kb)),
    ]
    args = [sinks, proj, proj, proj, proj, proj, qg, kg, dy, proj]
    body, in_specs, args = _ordered_behind(body, len(args), in_specs, args, after)
    return pl.pallas_call(
        body, name="attn_bwd", grid=(nb,),
        in_specs=in_specs,
        out_specs=[pl.BlockSpec((t, aw + 2 * kv), lambda n: (0, 0)), pl.BlockSpec((8, BLK), lambda n: (0, 0))],
        out_shape=[SDS((t, aw + 2 * kv), BF16), SDS((8, BLK), F32)],
        scratch_shapes=[pltpu.VMEM((t, kv), F32), pltpu.VMEM((t, kv), F32), pltpu.VMEM((1, aw), F32)],
        compiler_params=_params("arbitrary"),
    )(*args)


def _conv_taps(win, w_ref, shift_of):
    rows = win.shape[0]
    acc = None
    for j in range(CONV_K):
        term = pltpu.roll(win, (rows - shift_of(j)) % rows, axis=0)[:BLK] * w_ref[j:j + 1, :]
        acc = term if acc is None else acc + term
    return acc


def _layer_norm_fwd(z):
    mu = jnp.mean(z, axis=-1, keepdims=True)
    zc = z - mu
    rstd = lax.rsqrt(jnp.mean(zc * zc, axis=-1, keepdims=True) + EPS)
    return zc * rstd, rstd


def _layer_norm_bwd(dy, yhat, rstd, g):
    dyh = dy * g
    return rstd * (dyh - jnp.mean(dyh, axis=-1, keepdims=True) - yhat * jnp.mean(dyh * yhat, axis=-1, keepdims=True))


def _conv_fill_glu(a_ref, g_ref, hpad, nb):
    hpad[0:HALO, :] = jnp.zeros((HALO, hpad.shape[1]), F32)

    def fill(i, c):
        r = pl.multiple_of(i * BLK, BLK)
        hpad[pl.ds(pl.multiple_of(r + HALO, HALO), BLK), :] = a_ref[pl.ds(r, BLK), :] * _sigmoid(g_ref[pl.ds(r, BLK), :])
        return c

    lax.fori_loop(0, nb, fill, 0)


def _conv_specs(t, cw, d_in):
    base = (d_in - 4 * cw) // cw
    vec = pl.BlockSpec((1, cw), lambda i: (0, 0))
    return [
        pl.BlockSpec((t, cw), lambda i: (0, base)),
        pl.BlockSpec((t, cw), lambda i: (0, base + 1)),
        pl.BlockSpec((HALO, cw), lambda i: (0, 0)),
        vec, vec, vec,
    ]


def _conv_fwd(proj, cw_pad, cb, lg, lb, cw):
    t, d_in = proj.shape
    nb = t // BLK

    def body(a_ref, g_ref, w_ref, b_ref, lg_ref, lb_ref, o_ref, hpad):
        _conv_fill_glu(a_ref, g_ref, hpad, nb)

        def blk(i, c):
            r = pl.multiple_of(i * BLK, BLK)
            z = _conv_taps(hpad[pl.ds(r, BLK + HALO), :], w_ref, lambda j: j + HALO - (CONV_K - 1)) + b_ref[...]
            yhat, _ = _layer_norm_fwd(z)
            y = yhat * lg_ref[...] + lb_ref[...]
            o_ref[pl.ds(r, BLK), :] = y * _sigmoid(y)
            return c

        lax.fori_loop(0, nb, blk, 0)

    return pl.pallas_call(
        body, name="conv_fwd", grid=(1,),
        in_specs=_conv_specs(t, cw, d_in),
        out_specs=pl.BlockSpec((t, cw), lambda i: (0, 0)),
        out_shape=SDS((t, cw), F32),
        scratch_shapes=[pltpu.VMEM((t + HALO, cw), F32)],
        compiler_params=_params("arbitrary"),
    )(proj, proj, cw_pad, cb, lg, lb)


def _conv_bwd(proj, dy, cw_pad, cb, lg, lb, cw):
    t, d_in = proj.shape
    nb = t // BLK

    def body(a_ref, g_ref, w_ref, b_ref, lg_ref, lb_ref, dy_ref, dc_ref, dw_ref, dvec_ref, hpad, dzpad, dwacc):
        _conv_fill_glu(a_ref, g_ref, hpad, nb)
        dzpad[t:t + HALO, :] = jnp.zeros((HALO, cw), F32)
        dwacc[...] = jnp.zeros_like(dwacc)

        def blk(i, carry):
            db, dlg, dlb = carry
            r = pl.multiple_of(i * BLK, BLK)
            win = hpad[pl.ds(r, BLK + HALO), :]
            z = _conv_taps(win, w_ref, lambda j: j + HALO - (CONV_K - 1)) + b_ref[...]
            yhat, rstd = _layer_norm_fwd(z)
            y = yhat * lg_ref[...] + lb_ref[...]
            sg = _sigmoid(y)
            dyl = dy_ref[pl.ds(r, BLK), :] * (sg * (1.0 + y * (1.0 - sg)))
            dz = _layer_norm_bwd(dyl, yhat, rstd, lg_ref[...])
            dzpad[pl.ds(r, BLK), :] = dz
            for j in range(CONV_K):
                sh = j + HALO - (CONV_K - 1)
                prod = dz * pltpu.roll(win, (BLK + HALO - sh) % (BLK + HALO), axis=0)[:BLK]
                dwacc[8 * j:8 * j + 8, :] += jnp.sum(prod.reshape(BLK // 8, 8, cw), axis=0)
            return (db + jnp.sum(dz, axis=0, keepdims=True),
                    dlg + jnp.sum(dyl * yhat, axis=0, keepdims=True),
                    dlb + jnp.sum(dyl, axis=0, keepdims=True))

        zero = jnp.zeros((1, cw), F32)
        db, dlg, dlb = lax.fori_loop(0, nb, blk, (zero, zero, zero))
        dvec_ref[...] = jnp.zeros_like(dvec_ref)
        dvec_ref[0:1, :] = db
        dvec_ref[1:2, :] = dlg
        dvec_ref[2:3, :] = dlb
        dw_ref[...] = jnp.sum(dwacc[...].reshape(HALO, 8, cw), axis=1)

        def blk2(i, c):
            r = pl.multiple_of(i * BLK, BLK)
            dh = _conv_taps(dzpad[pl.ds(r, BLK + HALO), :], w_ref, lambda j: CONV_K - 1 - j)
            a = a_ref[pl.ds(r, BLK), :]
            sg = _sigmoid(g_ref[pl.ds(r, BLK), :])
            dc_ref[pl.ds(r, BLK), 0:cw] = (dh * sg).astype(BF16)
            dc_ref[pl.ds(r, BLK), cw:2 * cw] = (dh * a * sg * (1.0 - sg)).astype(BF16)
            return c

        lax.fori_loop(0, nb, blk2, 0)

    return pl.pallas_call(
        body, name="conv_bwd", grid=(1,),
        in_specs=_conv_specs(t, cw, d_in) + [pl.BlockSpec((t, cw), lambda i: (0, 0))],
        out_specs=[pl.BlockSpec((t, 2 * cw), lambda i: (0, 0)), pl.BlockSpec((HALO, cw), lambda i: (0, 0)),
                   pl.BlockSpec((8, cw), lambda i: (0, 0))],
        out_shape=[SDS((t, 2 * cw), BF16), SDS((HALO, cw), F32), SDS((8, cw), F32)],
        scratch_shapes=[pltpu.VMEM((t + HALO, cw), F32), pltpu.VMEM((t + HALO, cw), F32), pltpu.VMEM((8 * HALO, cw), F32)],
        compiler_params=_params("arbitrary"),
    )(proj, proj, cw_pad, cb, lg, lb, dy)


def _tril_bf16(w):
    r = lax.broadcasted_iota(jnp.int32, (BLK, BLK), 0)
    c = lax.broadcasted_iota(jnp.int32, (BLK, BLK), 1)
    return jnp.where(r >= c, w, 0.0).astype(BF16)


def _sgu_specs(sw, d_in, heads):
    base = (d_in - 2 * sw) // sw
    vec = pl.BlockSpec((1, sw), lambda n: (0, 0))
    return [
        pl.BlockSpec((BLK, sw), lambda n: (n, base)),
        pl.BlockSpec((BLK, sw), lambda n: (n, base + 1)),
        vec, vec,
        pl.BlockSpec((heads, BLK, BLK), lambda n: (0, 0, 0)),
        pl.BlockSpec((BLK, sw), lambda n: (0, 0)),
    ]


def _sgu_mix(w_ref, vnb, heads, sw, transpose):
    head_of = lax.broadcasted_iota(jnp.int32, (1, sw), 1) // HEAD
    s = jnp.zeros((BLK, sw), F32)
    for h in range(heads):
        wt = _tril_bf16(w_ref[h])
        mixed = _tn(wt, vnb) if transpose else _nn(wt, vnb)
        s = jnp.where(head_of == h, mixed, s)
    return s


def _sgu_fwd(proj, lg, lb, w, bias_full, sw):
    t, d_in = proj.shape
    heads = sw // HEAD

    def body(u_ref, v_ref, lg_ref, lb_ref, w_ref, bias_ref, o_ref):
        vhat, _ = _layer_norm_fwd(v_ref[...])
        vn = (vhat * lg_ref[...] + lb_ref[...]).astype(BF16)
        s = _sgu_mix(w_ref, vn, heads, sw, False) + bias_ref[...]
        o_ref[...] = u_ref[...] * s

    return pl.pallas_call(
        body, name="sgu_fwd", grid=(t // BLK,),
        in_specs=_sgu_specs(sw, d_in, heads),
        out_specs=pl.BlockSpec((BLK, sw), lambda n: (n, 0)),
        out_shape=SDS((t, sw), F32),
        compiler_params=_params("arbitrary"),
    )(proj, proj, lg, lb, w, bias_full)


def _sgu_bwd(proj, dy, lg, lb, w, bias_full, sw, after=None):
    t, d_in = proj.shape
    heads = sw // HEAD
    nb = t // BLK

    def body(u_ref, v_ref, lg_ref, lb_ref, w_ref, bias_ref, dy_ref, ds_ref, dw_ref, db_ref, dvec_ref, dbfull):
        n = pl.program_id(0)

        @pl.when(n == 0)
        def _():
            dw_ref[...] = jnp.zeros_like(dw_ref)
            dvec_ref[...] = jnp.zeros_like(dvec_ref)
            dbfull[...] = jnp.zeros_like(dbfull)

        vhat, rstd = _layer_norm_fwd(v_ref[...])
        vn = (vhat * lg_ref[...] + lb_ref[...]).astype(BF16)
        s = _sgu_mix(w_ref, vn, heads, sw, False) + bias_ref[...]
        dy = dy_ref[...]
        ds_ref[:, 0:sw] = (dy * s).astype(BF16)
        dsv = dy * u_ref[...]
        dbfull[...] += dsv
        head_of = lax.broadcasted_iota(jnp.int32, (1, sw), 1) // HEAD
        r = lax.broadcasted_iota(jnp.int32, (BLK, BLK), 0)
        c = lax.broadcasted_iota(jnp.int32, (BLK, BLK), 1)
        dsb = dsv.astype(BF16)
        for h in range(heads):
            dsh = jnp.where(head_of == h, dsv, 0.0).astype(BF16)
            dw_ref[h] += jnp.where(r >= c, _nt(dsh, vn), 0.0)
        dvn = _sgu_mix(w_ref, dsb, heads, sw, True)
        dvec_ref[0:1, :] += jnp.sum(dvn * vhat, axis=0, keepdims=True)
        dvec_ref[1:2, :] += jnp.sum(dvn, axis=0, keepdims=True)
        ds_ref[:, sw:2 * sw] = _layer_norm_bwd(dvn, vhat, rstd, lg_ref[...]).astype(BF16)

        @pl.when(n == nb - 1)
        def _():
            sel = (lax.broadcasted_iota(jnp.int32, (sw, BLK), 0) // HEAD == lax.broadcasted_iota(jnp.int32, (sw, BLK), 1)).astype(BF16)
            x = dbfull[...]
            hi = x.astype(BF16)
            r1 = x - hi.astype(F32)
            mid = r1.astype(BF16)
            low = (r1 - mid.astype(F32)).astype(BF16)
            db_ref[...] = _nn(hi, sel) + _nn(mid, sel) + _nn(low, sel)

    in_specs = _sgu_specs(sw, d_in, heads) + [pl.BlockSpec((BLK, sw), lambda n: (n, 0))]
    body, in_specs, args = _ordered_behind(body, 7, in_specs, [proj, proj, lg, lb, w, bias_full, dy], after)
    return pl.pallas_call(
        body, name="sgu_bwd", grid=(nb,),
        in_specs=in_specs,
        out_specs=[pl.BlockSpec((BLK, 2 * sw), lambda n: (n, 0)), pl.BlockSpec((heads, BLK, BLK), lambda n: (0, 0, 0)),
                   pl.BlockSpec((BLK, BLK), lambda n: (0, 0)), pl.BlockSpec((8, sw), lambda n: (0, 0))],
        out_shape=[SDS((t, 2 * sw), BF16), SDS((heads, BLK, BLK), F32), SDS((BLK, BLK), F32), SDS((8, sw), F32)],
        scratch_shapes=[pltpu.VMEM((BLK, sw), F32)],
        compiler_params=_params("arbitrary"),
    )(*args)


def _rms_fwd(x, g):
    return (x * lax.rsqrt(jnp.mean(x * x, axis=-1, keepdims=True) + EPS)) * g


def _rms_bwd(dh, x, g):
    rstd = lax.rsqrt(jnp.mean(x * x, axis=-1, keepdims=True) + EPS)
    xhat = x * rstd
    dgx = dh * g
    dx = rstd * (dgx - xhat * jnp.mean(dgx * xhat, axis=-1, keepdims=True))
    return dx, jnp.sum(dh * xhat, axis=0, keepdims=True)


def _rms_matmul(x, g, w, tm, tn, relu2, name, transposed=False):
    t, d = x.shape
    if w.ndim == 3:
        assert w.shape[2] == tn
        n = w.shape[0] * tn
        w_spec = pl.BlockSpec((None, d, tn), lambda i, j: (j, 0, 0))
    elif transposed:
        n = w.shape[0]
        w_spec = pl.BlockSpec((tn, d), lambda i, j: (j, 0))
    else:
        n = w.shape[1]
        w_spec = pl.BlockSpec((d, tn), lambda i, j: (0, j))

    def body(x_ref, g_ref, w_ref, h_ref, *outs):
        @pl.when(pl.program_id(1) == 0)
        def _():
            h_ref[...] = _rms_fwd(x_ref[...], g_ref[...]).astype(BF16)

        acc = _nt(h_ref[...], w_ref[...]) if transposed else _nn(h_ref[...], w_ref[...])
        if relu2:
            r = jnp.maximum(acc, 0.0)
            outs[0][...] = (r * r).astype(BF16)
            outs[1][...] = r.astype(BF16)
        else:
            outs[0][...] = acc

    tile = pl.BlockSpec((tm, tn), lambda i, j: (i, j))
    row = pl.BlockSpec((tm, d), lambda i, j: (i, 0))
    outs = [SDS((t, n), BF16), SDS((t, n), BF16)] if relu2 else [SDS((t, n), F32)]
    return pl.pallas_call(
        body, name=name, grid=(t // tm, n // tn),
        in_specs=[row, pl.BlockSpec((1, d), lambda i, j: (0, 0)), w_spec],
        out_specs=[row] + [tile] * len(outs),
        out_shape=[SDS((t, d), BF16)] + outs,
        compiler_params=_params("parallel", "arbitrary"),
    )(x, g, w)


def _group_rms_matmul(ys, g, w, res, tm, tn):
    t, d = res.shape
    widths = [y.shape[1] for y in ys]
    k = sum(widths)

    def body(*refs):
        y_refs, (g_ref, w_ref, res_ref, mix_ref, o_ref) = refs[:len(ys)], refs[len(ys):]

        @pl.when(pl.program_id(1) == 0)
        def _():
            c = 0
            for y_ref, wd in zip(y_refs, widths):
                mix_ref[:, c:c + wd] = _rms_fwd(y_ref[...], g_ref[:, c:c + wd]).astype(BF16)
                c += wd

        o_ref[...] = res_ref[...] + _nn(mix_ref[...], w_ref[...])

    tile = pl.BlockSpec((tm, tn), lambda i, j: (i, j))
    return pl.pallas_call(
        body, name="mix_out", grid=(t // tm, d // tn),
        in_specs=[pl.BlockSpec((tm, wd), lambda i, j: (i, 0)) for wd in widths] + [
            pl.BlockSpec((1, k), lambda i, j: (0, 0)), pl.BlockSpec((k, tn), lambda i, j: (0, j)), tile],
        out_specs=[pl.BlockSpec((tm, k), lambda i, j: (i, 0)), tile],
        out_shape=[SDS((t, k), BF16), SDS((t, d), F32)],
        compiler_params=_params("parallel", "arbitrary"),
    )(*ys, g, w, res)


def _matmul_res(a, w, res, tm, tn, tk):
    t, k = a.shape
    n = w.shape[1]
    nk = k // tk

    def body(a_ref, w_ref, res_ref, o_ref, acc):
        kk = pl.program_id(2)

        @pl.when(kk == 0)
        def _():
            acc[...] = res_ref[...]

        acc[...] += _nn(a_ref[...], w_ref[...])

        @pl.when(kk == nk - 1)
        def _():
            o_ref[...] = acc[...]

    tile = pl.BlockSpec((tm, tn), lambda i, j, kk: (i, j))
    return pl.pallas_call(
        body, name="mlp_down", grid=(t // tm, n // tn, nk),
        in_specs=[pl.BlockSpec((tm, tk), lambda i, j, kk: (i, kk)), pl.BlockSpec((tk, tn), lambda i, j, kk: (kk, j)), tile],
        out_specs=tile,
        out_shape=SDS((t, n), F32),
        scratch_shapes=[pltpu.VMEM((tm, tn), F32)],
        compiler_params=_params("parallel", "parallel", "arbitrary"),
    )(a, w, res)


def _loss_grad(y, target, tm):
    t, d = y.shape

    def body(y_ref, t_ref, dy_ref, dyb_ref, l_ref):
        @pl.when(pl.program_id(0) == 0)
        def _():
            l_ref[...] = jnp.zeros_like(l_ref)

        err = y_ref[...] - t_ref[...]
        dy = err * (1.0 / d)
        dy_ref[...] = dy
        dyb_ref[...] = dy.astype(BF16)
        per_row = jnp.mean(err * err, axis=-1, keepdims=True)
        l_ref[...] += jnp.sum(per_row, axis=0, keepdims=True) * 0.5

    row = pl.BlockSpec((tm, d), lambda i: (i, 0))
    return pl.pallas_call(
        body, name="loss_grad", grid=(t // tm,),
        in_specs=[row, row], out_specs=[row, row, pl.BlockSpec((8, BLK), lambda i: (0, 0))],
        out_shape=[SDS((t, d), F32), SDS((t, d), BF16), SDS((8, BLK), F32)],
        compiler_params=_params("arbitrary"),
    )(y, target)


def _mlp_dact(dxb, w_down, r, tn, after=None):
    t, d = dxb.shape
    f = w_down.shape[0]

    def body(dxb_ref, w_ref, r_ref, dup_ref):
        dup_ref[...] = (_nt(dxb_ref[...], w_ref[...]) * (2.0 * r_ref[...].astype(F32))).astype(BF16)

    tile = pl.BlockSpec((t, tn), lambda j: (0, j))
    body, in_specs, args = _ordered_behind(
        body, 3, [pl.BlockSpec((t, d), lambda j: (0, 0)), pl.BlockSpec((tn, d), lambda j: (j, 0)), tile],
        [dxb, w_down, r], after)
    return pl.pallas_call(
        body, name="mlp_dact", grid=(f // tn,),
        in_specs=in_specs, out_specs=tile,
        out_shape=SDS((t, f), BF16),
        compiler_params=_params("arbitrary"),
    )(*args)


def _grad_w(a, b, tm, tn, name, col_blocks=False, after=None, rows=None, into=None):
    t, m = a.shape
    n = b.shape[1]
    assert n % tn == 0 and m % tm == 0
    total, first = rows if rows else (m, 0)
    assert first % tm == 0
    first_tile = first // tm

    def body(a_ref, b_ref, *rest):
        rest[-1][...] = _tn(a_ref[...], b_ref[...]).astype(BF16)

    if col_blocks:
        out_spec = pl.BlockSpec((None, tm, tn), lambda i, j: (j, i, 0))
        out_shape = SDS((n // tn, m, tn), BF16)
    else:
        out_spec = pl.BlockSpec((tm, tn), lambda i, j: (first_tile + i, j))
        out_shape = SDS((total, n), BF16)
    in_specs = [pl.BlockSpec((t, tm), lambda i, j: (0, i)), pl.BlockSpec((t, tn), lambda i, j: (0, j))]
    args = [a, b]
    aliases = {}
    if into is not None:
        in_specs, args, aliases = in_specs + [ANY], args + [into], {2: 0}
    body, in_specs, args = _ordered_behind(body, len(args), in_specs, args, after)
    return pl.pallas_call(
        body, name=name, grid=(m // tm, n // tn),
        in_specs=in_specs, out_specs=out_spec, out_shape=out_shape, input_output_aliases=aliases,
        compiler_params=_params("parallel", "arbitrary"),
    )(*args)


def _grad_w_half(a, b, tm, tn, name, by_cols, sel, add=None, after=None):
    t, m = a.shape
    n = b.shape[1]
    if by_cols:
        per = n // N_DEV // tn
        grid, shape = (4, m // tm, per), (4, m, n // N_DEV)
        a_spec = pl.BlockSpec((t, tm), lambda q, i, j, s: (0, i))
        b_spec = pl.BlockSpec((t, tn), lambda q, i, j, s: (0, (2 * q + s[0]) * per + j))
    else:
        per = m // N_DEV // tm
        grid, shape = (4, per, n // tn), (4, m // N_DEV, n)
        a_spec = pl.BlockSpec((t, tm), lambda q, i, j, s: (0, (2 * q + s[0]) * per + i))
        b_spec = pl.BlockSpec((t, tn), lambda q, i, j, s: (0, j))
    assert per >= 1
    tile = pl.BlockSpec((None, tm, tn), lambda q, i, j, s: (q, i, j))

    def body(sel_ref, a_ref, b_ref, *rest):
        acc = _tn(a_ref[...], b_ref[...])
        if add is not None:
            acc = acc + rest[0][...].astype(F32)
        rest[-1][...] = acc.astype(BF16)

    in_specs, args = [a_spec, b_spec], [a, b]
    if add is not None:
        in_specs, args = in_specs + [tile], args + [add]
    if after is not None:
        in_specs, args = in_specs + [ANY], args + [after]
    return pl.pallas_call(
        body, name=name,
        grid_spec=pltpu.PrefetchScalarGridSpec(num_scalar_prefetch=1, grid=grid, in_specs=in_specs, out_specs=tile),
        out_shape=SDS(shape, BF16),
        compiler_params=_params("arbitrary", "arbitrary", "arbitrary"),
    )(sel, *args)


def _mlp_dnorm(dup, w_up, x, g, dres, tm, after=None):
    t, f = dup.shape
    d = x.shape[1]
    nk, _, tk = w_up.shape

    def body(a_ref, w_ref, x_ref, g_ref, dres_ref, dx_ref, dg_ref, acc):
        i, kk = pl.program_id(0), pl.program_id(1)

        @pl.when((i == 0) & (kk == 0))
        def _():
            dg_ref[...] = jnp.zeros_like(dg_ref)

        @pl.when(kk == 0)
        def _():
            acc[...] = jnp.zeros_like(acc)

        acc[...] += _nt(a_ref[...], w_ref[...])

        @pl.when(kk == nk - 1)
        def _():
            dx, dg = _rms_bwd(acc[...], x_ref[...], g_ref[...])
            dx_ref[...] = dres_ref[...] + dx
            dg_ref[0:1, :] += dg

    row = pl.BlockSpec((tm, d), lambda i, kk: (i, 0))
    in_specs = [pl.BlockSpec((tm, tk), lambda i, kk: (i, kk)), pl.BlockSpec((None, d, tk), lambda i, kk: (kk, 0, 0)),
                row, pl.BlockSpec((1, d), lambda i, kk: (0, 0)), row]
    body, in_specs, args = _ordered_behind(body, 5, in_specs, [dup, w_up, x, g, dres], after)
    return pl.pallas_call(
        body, name="mlp_dnorm", grid=(t // tm, nk),
        in_specs=in_specs,
        out_specs=[row, pl.BlockSpec((8, d), lambda i, kk: (0, 0))],
        out_shape=[SDS((t, d), F32), SDS((8, d), F32)],
        scratch_shapes=[pltpu.VMEM((tm, d), F32)],
        compiler_params=_params("arbitrary", "arbitrary"),
    )(*args)


def _mix_dnorm(dx, w_out, ys, g, tm, after=None):
    t, d = dx.shape
    k = w_out.shape[0]
    widths = [y.shape[1] for y in ys]

    def body(dx_ref, w_ref, *refs):
        y_refs = refs[:len(ys)]
        g_ref, dxb_ref = refs[len(ys)], refs[len(ys) + 1]
        dy_refs = refs[len(ys) + 2:2 * len(ys) + 2]
        dg_ref = refs[-1]

        @pl.when(pl.program_id(0) == 0)
        def _():
            dg_ref[...] = jnp.zeros_like(dg_ref)

        dxb = dx_ref[...].astype(BF16)
        dxb_ref[...] = dxb
        dmix = _nt(dxb, w_ref[...])
        c = 0
        for y_ref, dy_ref, wd in zip(y_refs, dy_refs, widths):
            dy, dg = _rms_bwd(dmix[:, c:c + wd], y_ref[...], g_ref[:, c:c + wd])
            dy_ref[...] = dy
            dg_ref[0:1, c:c + wd] += dg
            c += wd

    row = pl.BlockSpec((tm, d), lambda i: (i, 0))
    yspecs = [pl.BlockSpec((tm, wd), lambda i: (i, 0)) for wd in widths]
    in_specs = [row, pl.BlockSpec((k, d), lambda i: (0, 0))] + yspecs + [pl.BlockSpec((1, k), lambda i: (0, 0))]
    body, in_specs, args = _ordered_behind(body, len(in_specs), in_specs, [dx, w_out, *ys, g], after)
    return pl.pallas_call(
        body, name="mix_dnorm", grid=(t // tm,),
        in_specs=in_specs,
        out_specs=[row] + yspecs + [pl.BlockSpec((8, k), lambda i: (0, 0))],
        out_shape=[SDS((t, d), BF16)] + [SDS((t, wd), F32) for wd in widths] + [SDS((8, k), F32)],
        compiler_params=_params("arbitrary"),
    )(*args)


def _in_dnorm(dps, w_in_t, x, g, dres, tm):
    t, d = x.shape
    widths = [p.shape[1] for p in dps]
    offs = [sum(widths[:p]) for p in range(len(dps))]
    n_in = w_in_t.shape[0]

    def body(*refs):
        p_refs = refs[:len(dps)]
        w_ref, x_ref, g_ref, dres_ref, dx_ref, dxb_ref, dg_ref = refs[len(dps):]

        @pl.when(pl.program_id(0) == 0)
        def _():
            dg_ref[...] = jnp.zeros_like(dg_ref)

        acc = None
        for p_ref, off, wd in zip(p_refs, offs, widths):
            term = _nn(p_ref[...], w_ref[off:off + wd, :])
            acc = term if acc is None else acc + term
        dx, dg = _rms_bwd(acc, x_ref[...], g_ref[...])
        dx = dres_ref[...] + dx
        dx_ref[...] = dx
        dxb_ref[...] = dx.astype(BF16)
        dg_ref[0:1, :] += dg

    row = pl.BlockSpec((tm, d), lambda i: (i, 0))
    return pl.pallas_call(
        body, name="in_dnorm", grid=(t // tm,),
        in_specs=[pl.BlockSpec((tm, wd), lambda i: (i, 0)) for wd in widths] + [
            pl.BlockSpec((n_in, d), lambda i: (0, 0)), row, pl.BlockSpec((1, d), lambda i: (0, 0)), row],
        out_specs=[row, row, pl.BlockSpec((8, d), lambda i: (0, 0))],
        out_shape=[SDS((t, d), F32), SDS((t, d), BF16), SDS((8, d), F32)],
        compiler_params=_params("arbitrary"),
    )(*dps, w_in_t, x, g, dres)


def _tile(n, want):
    return min(n, want)


def _row_tile(n, want):
    return max(k for k in range(8, min(n, want) + 1, 8) if n % k == 0)


def _layer_fwd(x, p, fetch, after):
    t, d = x.shape
    aw, kv, cw, sw = d // 2, d // 8, d // 4, d // 4
    tm = _tile(t, 1024)
    w_in = fetch("w_in", after)
    h1, proj = _rms_matmul(x, p["ln1_g"], w_in, tm, 512 if w_in.shape[0] % 512 == 0 else 256, False, "in_proj",
                           transposed=True)
    y_attn = _attn_fwd(proj, p["qg"], p["kg"], p["sinks"], aw, kv)
    y_conv = _conv_fwd(proj, p["conv_w"], p["conv_b"], p["conv_ln_g"], p["conv_ln_b"], cw)
    y_sgu = _sgu_fwd(proj, p["sgu_ln_g"], p["sgu_ln_b"], p["sgu_w"], p["sgu_bias"], sw)
    ys = [y_attn, y_conv, y_sgu]
    w_out = fetch("w_out", y_sgu)
    mix, x1 = _group_rms_matmul(ys, p["out_norm_g"], w_out, x, tm, _tile(d, 1024))
    w_up = fetch("w_up", x1)
    h2, act, r = _rms_matmul(x1, p["ln2_g"], w_up, tm, w_up.shape[2], True, "mlp_up")
    w_down = fetch("w_down", act)
    x2 = _matmul_res(act, w_down, x1, tm, _tile(d, 2048), 512)
    saved = dict(x=x, h1=h1, proj=proj, ys=ys, mix=mix, x1=x1, h2=h2, act=act, r=r,
                 w_in=w_in, w_out=w_out, w_up=w_up, w_down=w_down)
    return x2, saved


def _layer_bwd(dx2, dx2b, p, s, start, finish, share, carry):
    t, d = dx2.shape
    aw, kv, cw, sw = d // 2, d // 8, d // 4, d // 4
    tm = _tile(t, 512)
    dup = _mlp_dact(dx2b, s["w_down"], s["r"], 512, carry[0] if carry else None)
    tok = carry[1](dup) if carry else None
    half = functools.partial
    f = s["act"].shape[1]
    tok = start("w_down", half(_grad_w_half, s["act"], dx2b, _tile(f // N_DEV, 512), _tile(d, 2048), "grad_w_down", False), tok)
    dx1, d_ln2 = _mlp_dnorm(dup, s["w_up"], s["x1"], p["ln2_g"], dx2, tm, tok)
    tok = finish("w_down", dx1)
    tok = start("w_up", half(_grad_w_half, s["h2"], dup, _tile(d, 1024), s["w_up"].shape[2], "grad_w_up", True), tok)
    dx1b, dya, dyc, dys, d_onorm = _mix_dnorm(dx1, s["w_out"], s["ys"], p["out_norm_g"], _tile(t, 256), tok)
    tok = finish("w_up", dx1b)
    tok = start("w_out", half(_grad_w_half, s["mix"], dx1b, _tile(d // N_DEV, 512), _tile(d, 2048), "grad_w_out", False), tok)
    dsgu, d_sw, d_sb, d_svec = _sgu_bwd(s["proj"], dys, p["sgu_ln_g"], p["sgu_ln_b"], p["sgu_w"], p["sgu_bias"], sw, tok)
    tok = share("sgu_w", d_sw, None)
    dqkv, d_attn = _attn_bwd(s["proj"], dya, p["qg"], p["kg"], p["sinks"], aw, kv, tok)
    tok = finish("w_out", dqkv)
    dconv, d_cw, d_cvec = _conv_bwd(s["proj"], dyc, p["conv_w"], p["conv_b"], p["conv_ln_g"], p["conv_ln_b"], cw)
    dps = [dqkv, dconv, dsgu]
    dx, dxb, d_ln1 = _in_dnorm(dps, s["w_in"], s["x"], p["ln1_g"], dx1, _tile(t, 256))
    heads = sw // HEAD
    tok = share("rest", dict(
        ln1_g=d_ln1[0], q_norm_g=d_attn[0, :HEAD], k_norm_g=d_attn[1, :HEAD], sinks=d_attn[2, :aw // HEAD],
        conv_w=d_cw[:CONV_K], conv_b=d_cvec[0], conv_ln_g=d_cvec[1], conv_ln_b=d_cvec[2],
        sgu_ln_g=d_svec[0], sgu_ln_b=d_svec[1], sgu_b=d_sb[:, :heads].T,
        out_norm_g=d_onorm[0], ln2_g=d_ln2[0]), tok)
    d_in, first, g_in = sum(dp.shape[1] for dp in dps), 0, None
    tm_in = 512 if all(dp.shape[1] % 512 == 0 for dp in dps) else 256
    for k, dp in enumerate(dps):
        g_in = _grad_w(dp, s["h1"], tm_in, _tile(d, 2048), f"grad_w_in_{k}", after=tok if k == 0 else None,
                       rows=(d_in, first), into=g_in)
        first += dp.shape[1]
    tok = start("w_in", g_in, None)
    carry = (tok, functools.partial(finish, "w_in"))
    return dx, dxb, carry


def _layer_params(l, small):
    row = lambda v: v[l][None, :]
    two = lambda v: jnp.tile(v[l], 2)[None, :]
    return dict(
        ln1_g=row(small["ln1_g"]), ln2_g=row(small["ln2_g"]), out_norm_g=row(small["out_norm_g"]),
        qg=two(small["q_norm_g"]), kg=two(small["k_norm_g"]), sinks=small["sinks"][l],
        conv_w=jnp.pad(small["conv_w"][l], ((0, HALO - CONV_K), (0, 0))),
        conv_b=row(small["conv_b"]), conv_ln_g=row(small["conv_ln_g"]), conv_ln_b=row(small["conv_ln_b"]),
        sgu_ln_g=row(small["sgu_ln_g"]), sgu_ln_b=row(small["sgu_ln_b"]), sgu_w=small["sgu_w"][l],
        sgu_bias=jnp.repeat(small["sgu_b"][l].T, HEAD, axis=1),
    )


def _local_step(x, target, small, depth, fetch, start, finish, share, after):
    params = [_layer_params(l, small) for l in range(depth)]
    saved = []
    h = x
    for l in range(depth):
        h, s = _layer_fwd(h, params[l], functools.partial(fetch, l), after)
        after = h
        saved.append(s)
    dy, dyb, lsum = _loss_grad(h, target, _tile(x.shape[0], 512))
    carry = None
    for l in reversed(range(depth)):
        dy, dyb, carry = _layer_bwd(dy, dyb, params[l], saved[l], functools.partial(start, l),
                                    functools.partial(finish, l), functools.partial(share, l), carry)
    return lsum[0, 0], dy, carry


BIG = ("w_in", "w_out", "w_up", "w_down")
HBM = pl.BlockSpec(memory_space=pltpu.HBM)
SEMS = pl.BlockSpec(memory_space=pltpu.SEMAPHORE)
EFFECT = pltpu.SideEffectType.DATAFLOW_SIDE_EFFECTING


def _mesh_pos():
    return lax.axis_index("x"), lax.axis_index("y"), lax.axis_index("c")


def _other_chips(x, y):
    return [(1 - x, y), (x, 1 - y), (1 - x, 1 - y)]


def _copies(plan, refs, sends, recvs):
    x, y, c = _mesh_pos()
    return [pltpu.make_async_remote_copy(src_ref=src, dst_ref=dst, send_sem=sends.at[k], recv_sem=recvs.at[k],
                                         device_id=dev, device_id_type=MESH)
            for k, (src, dst, dev) in enumerate(plan(x, y, c, refs))]


def _gather_plan(x, y, c, refs):
    mine = refs[0].at[4 * x + 2 * y + c]
    return [(mine, mine, (x, y, 1 - c))] + [(mine, mine, (*chip, c)) for chip in _other_chips(x, y)]


def _all_plan(x, y, c, refs):
    mine = refs[0].at[4 * x + 2 * y + c]
    return [(mine, mine, (x ^ (k >> 2), y ^ ((k >> 1) & 1), c ^ (k & 1))) for k in range(1, N_DEV)]


def _pair_plan(x, y, c, refs):
    blocks, land = refs
    return [(blocks.at[2 * q + (1 - c)], land.at[q], (x, y, 1 - c)) for q in range(4)]


def _half_plan(x, y, c, refs):
    blocks, land = refs
    return [(blocks.at[q], land.at[q], (x, y, 1 - c)) for q in range(4)]


def _chips_plan(x, y, c, refs):
    sums, land = refs
    return [(sums.at[2 * chip[0] + chip[1]], land.at[k], (*chip, c)) for k, chip in enumerate(_other_chips(x, y))]


def _start_exchanges(name, groups, after=None):
    flat = [a for arrays, _, _ in groups for a in arrays]
    n_arr, n_g = len(flat), len(groups)
    n_in = n_arr + (after is not None)

    def body(*refs):
        ins, sems, token = refs[:n_arr], refs[n_in:n_in + 2 * n_g], refs[-1]
        off = 0
        for gi, (arrays, plan, _) in enumerate(groups):
            for cp in _copies(plan, ins[off:off + len(arrays)], sems[2 * gi], sems[2 * gi + 1]):
                cp.start()
            off += len(arrays)
        token[...] = jnp.zeros_like(token)

    res = pl.pallas_call(
        body, name=name,
        out_shape=[pltpu.SemaphoreType.DMA((n,)) for _, _, n in groups for _ in (0, 1)]
        + [pltpu.HBM(a.shape, a.dtype) for a in flat] + [SDS((8, BLK), F32)],
        in_specs=[HBM] * n_arr + [ANY] * (after is not None),
        out_specs=[SEMS] * (2 * n_g) + [HBM] * n_arr + [pl.BlockSpec(memory_space=pltpu.VMEM)],
        input_output_aliases={i: 2 * n_g + i for i in range(n_arr)},
        compiler_params=pltpu.CompilerParams(has_side_effects=EFFECT),
    )(*[pltpu.with_memory_space_constraint(a, pltpu.HBM) for a in flat], *([after] if after is not None else []))
    sems, thru, token = res[:2 * n_g], res[2 * n_g:2 * n_g + n_arr], res[-1]
    out, off = [], 0
    for gi, (arrays, _, _) in enumerate(groups):
        out.append((list(thru[off:off + len(arrays)]), sems[2 * gi], sems[2 * gi + 1]))
        off += len(arrays)
    return out, token


def _wait_exchange(name, arrays, sends, recvs, plan, after):
    n = len(arrays)

    def body(*refs):
        for cp in _copies(plan, refs[:n], refs[n], refs[n + 1]):
            cp.wait_send()
            cp.wait_recv()

    return pl.pallas_call(
        body, name=name,
        out_shape=[pltpu.HBM(a.shape, a.dtype) for a in arrays],
        in_specs=[HBM] * n + [SEMS, SEMS, ANY],
        out_specs=[HBM] * n,
        input_output_aliases={i: i for i in range(n)},
        compiler_params=pltpu.CompilerParams(has_side_effects=EFFECT),
    )(*arrays, sends, recvs, after)


def _gather_finish(land, name):
    def body(land_ref, out_ref, send_sems, recv_sems):
        del land_ref
        x, y, c = _mesh_pos()
        cps = []
        for k, chip in enumerate(_other_chips(x, y)):
            block = out_ref.at[4 * chip[0] + 2 * chip[1] + c]
            cps.append(pltpu.make_async_remote_copy(
                src_ref=block, dst_ref=block, send_sem=send_sems.at[k], recv_sem=recv_sems.at[k],
                device_id=(x, y, 1 - c), device_id_type=MESH))
        for cp in cps:
            cp.start()
        for cp in cps:
            cp.wait()

    return pl.pallas_call(
        body, name=name,
        in_specs=[ANY], out_specs=ANY,
        out_shape=SDS(land.shape, land.dtype),
        input_output_aliases={0: 0},
        scratch_shapes=[pltpu.SemaphoreType.DMA((3,)), pltpu.SemaphoreType.DMA((3,))],
    )(land)


def _all_gather(shards, name):
    n = len(shards)

    def body(*refs):
        ins, outs = refs[:n], refs[n:2 * n]
        send_sems, recv_sems, local_sems = refs[2 * n:]
        x, y, c = _mesh_pos()
        me, sibling = (x, y, c), (x, y, 1 - c)
        chips = _other_chips(x, y)

        def copy(a, k, block, to, src=None):
            dst = outs[a].at[4 * block[0] + 2 * block[1] + block[2]]
            return pltpu.make_async_remote_copy(
                src_ref=dst if src is None else src, dst_ref=dst,
                send_sem=send_sems.at[a, k], recv_sem=recv_sems.at[a, k], device_id=to, device_id_type=MESH)

        mine = [pltpu.make_async_copy(ins[a], outs[a].at[4 * x + 2 * y + c], local_sems.at[a]) for a in range(n)]
        for cp in mine:
            cp.start()
        first = []
        for a in range(n):
            first.append(copy(a, 0, me, sibling, src=ins[a]))
            first += [copy(a, 1 + j, me, (*chip, c), src=ins[a]) for j, chip in enumerate(chips)]
        for cp in first:
            cp.start()
        passed = []
        for j, chip in enumerate(chips):
            for a in range(n):
                copy(a, 1 + j, (*chip, c), me).wait_recv()
                fwd = copy(a, 4 + j, (*chip, c), sibling)
                fwd.start()
                passed.append(fwd)
        for a in range(n):
            copy(a, 0, sibling, me).wait_recv()
            for j, chip in enumerate(chips):
                copy(a, 4 + j, (*chip, 1 - c), me).wait_recv()
        for cp in first + passed:
            cp.wait_send()
        for cp in mine:
            cp.wait()

    outs = pl.pallas_call(
        body, name=name,
        in_specs=[ANY] * n, out_specs=[ANY] * n,
        out_shape=[SDS((N_DEV,) + s.shape, s.dtype) for s in shards],
        scratch_shapes=[pltpu.SemaphoreType.DMA((n, 7)), pltpu.SemaphoreType.DMA((n, 7)), pltpu.SemaphoreType.DMA((n,))],
    )(*shards)
    return list(outs)


def _pair_add(own, got, c, name):
    _, r, cols = own.shape
    tr = _row_tile(r, 512)

    def body(c_ref, own_ref, got_ref, o_ref):
        o_ref[...] = (own_ref[...].astype(F32) + got_ref[...].astype(F32)).astype(BF16)

    return pl.pallas_call(
        body, name=name,
        grid_spec=pltpu.PrefetchScalarGridSpec(
            num_scalar_prefetch=1, grid=(4, r // tr),
            in_specs=[pl.BlockSpec((None, tr, cols), lambda q, i, c_ref: (2 * q + c_ref[0], i, 0)),
                      pl.BlockSpec((None, tr, cols), lambda q, i, c_ref: (q, i, 0))],
            out_specs=pl.BlockSpec((None, tr, cols), lambda q, i, c_ref: (q, i, 0))),
        out_shape=SDS((4, r, cols), BF16),
        compiler_params=_params("arbitrary", "arbitrary"),
    )(c, own, got)


def _adamw(w, g, m, v):
    m = ADAM_B1 * m + (1.0 - ADAM_B1) * g
    v = ADAM_B2 * v + (1.0 - ADAM_B2) * (g * g)
    m_hat = m / (1.0 - ADAM_B1 ** ADAM_STEP)
    v_hat = v / (1.0 - ADAM_B2 ** ADAM_STEP)
    delta = -ADAM_LR * (m_hat / (jnp.sqrt(v_hat) + ADAM_EPS) + ADAM_WD * w)
    return delta, m, v


def _adamw_layer(chip_sum, got, chip, w, m, v, layer, prev, name):
    depth, r, cols = w.shape
    tr = _row_tile(r, 256)

    def body(chip_ref, sum_ref, got_ref, w_ref, m_ref, v_ref, *rest):
        g_out, d_out, m_out, v_out, token = rest[-5:]
        g = sum_ref[...].astype(F32) + got_ref[0].astype(F32) + got_ref[1].astype(F32) + got_ref[2].astype(F32)
        delta, mm, vv = _adamw(w_ref[...], g, m_ref[...], v_ref[...])
        g_out[...] = g
        d_out[...] = delta
        m_out[...] = mm
        v_out[...] = vv
        token[...] = jnp.zeros_like(token)

    shard = pl.BlockSpec((None, tr, cols), lambda i, chip_ref: (layer, i, 0))
    in_specs = [pl.BlockSpec((None, tr, cols), lambda i, chip_ref: (chip_ref[0], i, 0)),
                pl.BlockSpec((3, tr, cols), lambda i, chip_ref: (0, i, 0)), shard, shard, shard]
    args = [chip, chip_sum, got, w, m, v]
    aliases = {}
    if prev is not None:
        in_specs += [ANY] * 4
        aliases = {len(args) + k: k for k in range(4)}
        args += list(prev)
    res = pl.pallas_call(
        body, name=name,
        grid_spec=pltpu.PrefetchScalarGridSpec(
            num_scalar_prefetch=1, grid=(r // tr,), in_specs=in_specs,
            out_specs=[shard] * 4 + [pl.BlockSpec((8, BLK), lambda i, chip_ref: (0, 0))]),
        out_shape=[SDS(w.shape, F32)] * 4 + [SDS((8, BLK), F32)],
        input_output_aliases=aliases,
        compiler_params=_params("arbitrary"),
    )(*args)
    return list(res[:4]), res[4]


WIDE = ("ln1_g", "out_norm_g", "ln2_g", "conv_b", "conv_ln_g", "conv_ln_b", "sgu_ln_g", "sgu_ln_b")
NARROW = ("q_norm_g", "k_norm_g", "sinks")


def _small_rows(w):
    rows, r = {}, 0
    for n in WIDE:
        rows[n] = (r, w[n].shape[1] // BLK)
        r += rows[n][1]
    for n in NARROW:
        rows[n] = (r, 1)
        r += 1
    r = -(-r // 8) * 8
    rows["sgu_b"] = (r, w["sgu_b"].shape[1])
    r += -(-rows["sgu_b"][1] // 8) * 8
    rows["conv_w"] = (r, N_DEV * HALO)
    return rows, r + N_DEV * HALO


def _pack_small(small, rows, total):
    parts, r = [], 0

    def put(name, block):
        nonlocal r
        first = rows[name][0]
        if first > r:
            parts.append(jnp.zeros((first - r, BLK), F32))
        parts.append(block)
        r = first + block.shape[0]

    for n in WIDE:
        put(n, small[n].reshape(-1, BLK))
    for n in NARROW:
        put(n, jnp.pad(small[n], (0, BLK - small[n].shape[0]))[None])
    put("sgu_b", small["sgu_b"])
    cw = small["conv_w"]
    per_dev = cw.shape[1] // N_DEV
    blocks = jnp.transpose(cw.reshape(CONV_K, N_DEV, per_dev), (1, 0, 2))
    put("conv_w", jnp.pad(blocks, ((0, 0), (0, HALO - CONV_K), (0, BLK - per_dev))).reshape(N_DEV * HALO, BLK))
    if total > r:
        parts.append(jnp.zeros((total - r, BLK), F32))
    return jnp.concatenate(parts)


def _small_update(me, packed, sgu_w_all, w, m, v, rows):
    depth = len(packed)
    total = packed[0].shape[1]
    names = SMALL + ("conv_w",)
    heads = w["sgu_w"].shape[1]
    per_dev = w["conv_w"].shape[2]

    def body(me_ref, *refs):
        packed_refs, sgu_refs = refs[:depth], refs[depth:2 * depth]
        ins = refs[2 * depth:2 * depth + 3 * len(names)]
        outs = refs[2 * depth + 3 * len(names):2 * depth + 7 * len(names)]
        acc, conv = refs[-2:]
        io = {n: (ins[3 * i:3 * i + 3], outs[4 * i:4 * i + 4]) for i, n in enumerate(names)}

        def update(n):
            (w_ref, m_ref, v_ref), (g_out, d_out, m_out, v_out) = io[n]
            delta, mm, vv = _adamw(w_ref[...], g_out[...], m_ref[...], v_ref[...])
            d_out[...] = delta
            m_out[...] = mm
            v_out[...] = vv

        mine = pl.ds(pl.multiple_of(rows["conv_w"][0] + HALO * me_ref[0], 8), HALO)
        for l in range(depth):
            s = packed_refs[l][0]
            c = packed_refs[l][0, mine, :]
            for k in range(1, N_DEV):
                s = s + packed_refs[l][k]
                c = c + packed_refs[l][k, mine, :]
            acc[l] = s
            conv[l] = c
        for n in WIDE:
            first, nr = rows[n]
            for l in range(depth):
                for j in range(nr):
                    io[n][1][0][l:l + 1, BLK * j:BLK * (j + 1)] = acc[l, first + j:first + j + 1, :]
            update(n)
        for n in NARROW:
            first, lanes = rows[n][0], w[n].shape[1]
            for l in range(depth):
                io[n][1][0][l:l + 1, :] = acc[l, first:first + 1, 0:lanes]
            update(n)
        first, nr = rows["sgu_b"]
        for l in range(depth):
            io["sgu_b"][1][0][l] = acc[l, first:first + nr, :]
            io["conv_w"][1][0][l] = conv[l, 0:CONV_K, 0:per_dev]
        update("sgu_b")
        update("conv_w")
        (w_ref, m_ref, v_ref), (g_out, d_out, m_out, v_out) = io["sgu_w"]
        for l in range(depth):
            for h in range(heads):
                g = sgu_refs[l][0, h]
                for k in range(1, N_DEV):
                    g = g + sgu_refs[l][k, h]
                delta, mm, vv = _adamw(w_ref[l, h], g, m_ref[l, h], v_ref[l, h])
                g_out[l, h] = g
                d_out[l, h] = delta
                m_out[l, h] = mm
                v_out[l, h] = vv

    def whole(a):
        nd = len(a.shape)
        return pl.BlockSpec(a.shape, lambda i, me_ref: (0,) * nd)

    small_in = [t[n] for n in names for t in (w, m, v)]
    res = pl.pallas_call(
        body, name="small_update",
        grid_spec=pltpu.PrefetchScalarGridSpec(
            num_scalar_prefetch=1, grid=(1,),
            in_specs=[whole(a) for a in list(packed) + list(sgu_w_all) + small_in],
            out_specs=[whole(w[n]) for n in names for _ in range(4)],
            scratch_shapes=[pltpu.VMEM((depth, total, BLK), F32), pltpu.VMEM((depth, HALO, BLK), F32)]),
        out_shape=[SDS(w[n].shape, F32) for n in names for _ in range(4)],
        compiler_params=_params("arbitrary"),
    )(me, *packed, *sgu_w_all, *small_in)
    return {n: list(res[4 * i:4 * i + 4]) for i, n in enumerate(names)}


def _pack(arrays):
    flat = jnp.concatenate([a.reshape(-1) for a in arrays])
    rows = -(-flat.shape[0] // (8 * BLK)) * 8
    return jnp.pad(flat, (0, rows * BLK - flat.shape[0])).reshape(rows, BLK)


def _unpack(packed, like):
    flat = packed.reshape(-1)
    out, off = [], 0
    for a in like:
        out.append(flat[off:off + a.size].reshape(a.shape))
        off += a.size
    return out


SMALL = ("ln1_g", "q_norm_g", "k_norm_g", "sinks", "conv_b", "conv_ln_g", "conv_ln_b", "sgu_ln_g", "sgu_ln_b",
         "sgu_w", "sgu_b", "out_norm_g", "ln2_g")
ORDER = ("ln1_g", "w_in", "q_norm_g", "k_norm_g", "sinks", "conv_w", "conv_b", "conv_ln_g", "conv_ln_b", "sgu_ln_g",
         "sgu_ln_b", "sgu_w", "sgu_b", "out_norm_g", "w_out", "ln2_g", "w_up", "w_down")


def _step(x, target, w, m, v):
    depth = w["ln1_g"].shape[0]
    xpos, ypos, cpos = _mesh_pos()
    me = 4 * xpos + 2 * ypos + cpos
    c_arr = jnp.reshape(cpos, (1,)).astype(jnp.int32)
    chip_arr = jnp.reshape(2 * xpos + ypos, (1,)).astype(jnp.int32)

    d = x.shape[1]
    def own_block(shard):
        return lax.dynamic_update_slice_in_dim(lax.empty((N_DEV,) + shard.shape, shard.dtype), shard[None], me, axis=0)

    cw = w["conv_w"]
    order = [(0, "conv_w")] + [(l, n) for l in range(depth) for n in BIG]
    started, gather_token = _start_exchanges("gather_start", [
        ([own_block(_pack([cw]) if n == "conv_w" else w[n][l].astype(BF16))], _gather_plan, 4) for l, n in order])
    pending = dict(zip(order, started))

    def fetch(l, n, after):
        arrays, sends, recvs = pending.pop((l, n))
        land, = _wait_exchange(f"gather_wait_{l}_{n}", arrays, sends, recvs, _gather_plan,
                               gather_token if after is None else after)
        full = _gather_finish(land, "gather_finish_" + n)
        return full if n in ("w_up", "conv_w") else full.reshape(-1, d)

    cw_all = fetch(0, "conv_w", None)
    cw_full = jnp.concatenate([_unpack(cw_all[k], [cw])[0] for k in range(N_DEV)], axis=-1)
    small = {n: w[n] for n in SMALL}
    small["conv_w"] = cw_full

    to_sibling, inflight = {}, []

    def start(l, n, grad, after):
        if callable(grad):
            blocks, plan = grad(sel=1 - c_arr, after=after), _half_plan
        else:
            blocks, plan = grad.reshape(N_DEV, -1, d), _pair_plan
        (going,), token = _start_exchanges(f"pair_start_{l}_{n}", [
            ([blocks, lax.empty((4,) + blocks.shape[1:], BF16)], plan, 4)])
        to_sibling[l, n] = (going, plan, grad)
        return token

    def finish(l, n, after):
        (arrays, sends, recvs), plan, grad = to_sibling.pop((l, n))
        blocks, from_sibling = _wait_exchange(f"pair_wait_{l}_{n}", arrays, sends, recvs, plan, after)
        if callable(grad):
            chip_sums = grad(sel=c_arr, add=from_sibling)
        else:
            chip_sums = _pair_add(blocks, from_sibling, c_arr, "pair_add_" + n)
        (going,), token = _start_exchanges(f"reduce_start_{l}_{n}", [
            ([chip_sums, lax.empty((3,) + chip_sums.shape[1:], BF16)], _chips_plan, 3)])
        inflight.append((l, n, going))
        return token

    rows, total = _small_rows(w)
    shared = {}

    def share(l, kind, grads, after):
        block = grads if kind == "sgu_w" else _pack_small(grads, rows, total)
        (shared[l, kind],), token = _start_exchanges(f"small_start_{l}_{kind}", [([own_block(block)], _all_plan, N_DEV - 1)],
                                                     after)
        return token

    loss, grad_x, (after, finish_last) = _local_step(x, target, small, depth, fetch, start, finish, share, None)

    as3d = lambda a: a.reshape(depth, -1, a.shape[-1])
    results = {n: None for n in BIG}

    def settle(l, n, going, after):
        arrays, sends, recvs = going
        chip_sums, got = _wait_exchange(f"reduce_wait_{l}_{n}", arrays, sends, recvs, _chips_plan, after)
        results[n], token = _adamw_layer(chip_sums, got, chip_arr, as3d(w[n]), as3d(m[n]), as3d(v[n]), l, results[n],
                                         f"adamw_{l}_{n}")
        return token

    early, late = inflight[:2], inflight[2:]
    for l, n, going in early:
        after = settle(l, n, going, after)
    after = finish_last(after)

    landed = {}
    for key in sorted(shared):
        arrays, sends, recvs = shared[key]
        landed[key], = _wait_exchange(f"small_wait_{key[0]}_{key[1]}", arrays, sends, recvs, _all_plan, after)
        after = landed[key]
    out = _small_update(jnp.reshape(me, (1,)).astype(jnp.int32), [landed[l, "rest"] for l in range(depth)],
                        [landed[l, "sgu_w"] for l in range(depth)], w, m, v, rows)
    after = out["sinks"][1]

    for l, n, going in late + inflight[-1:]:
        after = settle(l, n, going, after)
    out.update({n: [r.reshape(w[n].shape) for r in results[n]] for n in BIG})

    loss = lax.psum(loss, ("x", "y", "c"))
    return (loss, grad_x[None]) + tuple(out[n][k] for k in range(4) for n in ORDER)


def kernel(x, ln1_g, w_in, q_norm_g, k_norm_g, sinks, conv_w, conv_b, conv_ln_g, conv_ln_b, sgu_ln_g, sgu_ln_b, sgu_w, sgu_b, out_norm_g, w_out, ln2_g, w_up, w_down, loss_target, m_ln1_g, m_w_in, m_q_norm_g, m_k_norm_g, m_sinks, m_conv_w, m_conv_b, m_conv_ln_g, m_conv_ln_b, m_sgu_ln_g, m_sgu_ln_b, m_sgu_w, m_sgu_b, m_out_norm_g, m_w_out, m_ln2_g, m_w_up, m_w_down, v_ln1_g, v_w_in, v_q_norm_g, v_k_norm_g, v_sinks, v_conv_w, v_conv_b, v_conv_ln_g, v_conv_ln_b, v_sgu_ln_g, v_sgu_ln_b, v_sgu_w, v_sgu_b, v_out_norm_g, v_w_out, v_ln2_g, v_w_up, v_w_down):
    w = dict(zip(ORDER, (ln1_g, w_in, q_norm_g, k_norm_g, sinks, conv_w, conv_b, conv_ln_g, conv_ln_b, sgu_ln_g, sgu_ln_b,
                         sgu_w, sgu_b, out_norm_g, w_out, ln2_g, w_up, w_down)))
    m = dict(zip(ORDER, (m_ln1_g, m_w_in, m_q_norm_g, m_k_norm_g, m_sinks, m_conv_w, m_conv_b, m_conv_ln_g, m_conv_ln_b,
                         m_sgu_ln_g, m_sgu_ln_b, m_sgu_w, m_sgu_b, m_out_norm_g, m_w_out, m_ln2_g, m_w_up, m_w_down)))
    v = dict(zip(ORDER, (v_ln1_g, v_w_in, v_q_norm_g, v_k_norm_g, v_sinks, v_conv_w, v_conv_b, v_conv_ln_g, v_conv_ln_b,
                         v_sgu_ln_g, v_sgu_ln_b, v_sgu_w, v_sgu_b, v_out_norm_g, v_w_out, v_ln2_g, v_w_up, v_w_down)))
    for group in (w, m, v):
        group["w_in"] = jnp.swapaxes(group["w_in"], 1, 2)
    out = list(_step(x[0], loss_target[0], w, m, v))
    for k in range(4):
        i = 2 + k * len(ORDER) + ORDER.index("w_in")
        out[i] = jnp.swapaxes(out[i], 1, 2)
    return tuple(out)
```

```python
import functools

import jax
import jax.numpy as jnp
from jax import lax
from jax.experimental import pallas as pl
from jax.experimental.pallas import tpu as pltpu

F32 = jnp.float32
BF16 = jnp.bfloat16
SDS = jax.ShapeDtypeStruct

EPS = 1e-6
NEG_INF = -1e30
HEAD = 64
BLK = 128
CONV_K = 31
HALO = 32
N_DEV = 8

ADAM_LR = 0.001
ADAM_B1 = 0.9
ADAM_B2 = 0.999
ADAM_EPS = 1e-08
ADAM_WD = 0.01
ADAM_STEP = 10

VMEM_LIMIT = 56 * 1024 * 1024

MESH = pl.DeviceIdType.MESH


def _params(*sem):
    return pltpu.CompilerParams(dimension_semantics=sem, vmem_limit_bytes=VMEM_LIMIT)


def _nt(a, b):
    return lax.dot_general(a, b, (((1,), (1,)), ((), ())), preferred_element_type=F32)


def _tn(a, b):
    return lax.dot_general(a, b, (((0,), (0,)), ((), ())), preferred_element_type=F32)


def _nn(a, b):
    return jnp.dot(a, b, preferred_element_type=F32)


def _sigmoid(x):
    return 1.0 / (1.0 + jnp.exp(-x))


ANY = pl.BlockSpec(memory_space=pl.ANY)


def _ordered_behind(body, n_in, in_specs, args, after):
    if after is None:
        return body, in_specs, args
    return (lambda *refs: body(*refs[:n_in], *refs[n_in + 1:])), list(in_specs) + [ANY], list(args) + [after]


def _seg_ones():
    return lax.broadcasted_iota(jnp.int32, (1, BLK), 1) < HEAD


def _segsum(x, first):
    head0 = jnp.where(first, x, 0.0)
    s0 = jnp.sum(head0, axis=-1, keepdims=True)
    s1 = jnp.sum(x - head0, axis=-1, keepdims=True)
    return jnp.where(first, s0, s1)


def _head_rms(x, gain, ones):
    rstd = lax.rsqrt(_segsum(x * x, ones) * (1.0 / HEAD) + EPS)
    xhat = x * rstd
    return xhat * gain, xhat, rstd


def _expand(x, odd, lo):
    if odd:
        xl = pltpu.roll(jnp.where(lo, 0.0, x), HEAD, axis=1)
    else:
        xl = jnp.where(lo, x, 0.0)
    xh = pltpu.roll(xl, HEAD, axis=1)
    return jnp.concatenate([xl, xh], axis=0).astype(BF16)


def _fold(g2, odd, lo):
    r = g2.shape[0] // 2
    s = jnp.where(lo, g2[:r], 0.0) + pltpu.roll(jnp.where(lo, 0.0, g2[r:]), HEAD, axis=1)
    if odd:
        s = pltpu.roll(s, HEAD, axis=1)
    return s


def _attn_mask(n):
    qi = lax.broadcasted_iota(jnp.int32, (BLK, 2 * BLK), 0)
    sj = lax.broadcasted_iota(jnp.int32, (BLK, 2 * BLK), 1)
    rel = qi + BLK - sj
    return (rel >= 0) & (rel < BLK) & ((sj >= BLK) | (n > 0))


def _attn_specs(t, aw, kv):
    prev = lambda n: jnp.maximum(n - 1, 0)
    kb, vb = aw // kv, aw // kv + 1
    return [
        pl.BlockSpec(memory_space=pltpu.SMEM),
        pl.BlockSpec((BLK, aw), lambda n: (n, 0)),
        pl.BlockSpec((BLK, kv), lambda n: (prev(n), kb)),
        pl.BlockSpec((BLK, kv), lambda n: (n, kb)),
        pl.BlockSpec((BLK, kv), lambda n: (prev(n), vb)),
        pl.BlockSpec((BLK, kv), lambda n: (n, vb)),
        pl.BlockSpec((1, BLK), lambda n: (0, 0)),
        pl.BlockSpec((1, BLK), lambda n: (0, 0)),
    ]


def _softmax_pair(s2, valid, sink0, sink1):
    out, psink = [], []
    for half, sink in ((0, sink0), (1, sink1)):
        s = jnp.where(valid, s2[:, 2 * BLK * half:2 * BLK * (half + 1)], NEG_INF)
        m = jnp.maximum(jnp.max(s, axis=-1, keepdims=True), sink)
        p = jnp.exp(s - m)
        es = jnp.exp(sink - m)
        inv = 1.0 / (jnp.sum(p, axis=-1, keepdims=True) + es)
        out.append(p * inv)
        psink.append(es * inv)
    return jnp.concatenate(out, axis=1), psink


def _attn_fwd(proj, qg, kg, sinks, aw, kv):
    t = proj.shape[0]
    n_pairs, n_kvblk = aw // BLK, kv // BLK

    def body(sink_ref, q_ref, kp_ref, kc_ref, vp_ref, vc_ref, qg_ref, kg_ref, o_ref, probs_ref, sink_p_ref):
        n = pl.program_id(0)
        ones = _seg_ones()
        lo = lax.broadcasted_iota(jnp.int32, (1, BLK), 1) < HEAD
        valid = _attn_mask(n)
        kraw = jnp.concatenate([kp_ref[...], kc_ref[...]], axis=0)
        vraw = jnp.concatenate([vp_ref[...], vc_ref[...]], axis=0)
        qn = [_head_rms(q_ref[:, BLK * p:BLK * (p + 1)], qg_ref[...], ones)[0].astype(BF16) for p in range(n_pairs)]
        k2, v2 = [], []
        for b in range(n_kvblk):
            kn = _head_rms(kraw[:, BLK * b:BLK * (b + 1)], kg_ref[...], ones)[0]
            for odd in (0, 1):
                k2.append(_expand(kn, odd, lo))
                v2.append(_expand(vraw[:, BLK * b:BLK * (b + 1)], odd, lo))
        s2 = [_nt(qn[p], k2[p // 2]) * (HEAD ** -0.5) for p in range(n_pairs)]
        soft = [_softmax_pair(s2[p], valid, sink_ref[2 * p], sink_ref[2 * p + 1]) for p in range(n_pairs)]
        lane = lax.broadcasted_iota(jnp.int32, (1, BLK), 1)
        sink_p = jnp.zeros((BLK, BLK), F32)
        for p in range(n_pairs):
            probs_ref[:, 4 * BLK * p:4 * BLK * (p + 1)] = soft[p][0]
            for half in (0, 1):
                sink_p = jnp.where(lane == 2 * p + half, soft[p][1][half], sink_p)
            o_ref[:, BLK * p:BLK * (p + 1)] = _nn(soft[p][0].astype(BF16), v2[p // 2])
        sink_p_ref[...] = sink_p

    return pl.pallas_call(
        body, name="attn_fwd", grid=(t // BLK,),
        in_specs=_attn_specs(t, aw, kv),
        out_specs=[pl.BlockSpec((BLK, aw), lambda n: (n, 0)), pl.BlockSpec((BLK, 4 * aw), lambda n: (n, 0)),
                   pl.BlockSpec((BLK, BLK), lambda n: (n, 0))],
        out_shape=[SDS((t, aw), F32), SDS((t, 4 * aw), F32), SDS((t, BLK), F32)],
        compiler_params=_params("arbitrary"),
    )(sinks, proj, proj, proj, proj, proj, qg, kg)


def _attn_bwd(proj, dy, probs, sink_p, qg, kg, sinks, aw, kv, after=None):
    t = proj.shape[0]
    nb = t // BLK
    n_kvblk = kv // BLK
    kb = aw // kv

    def body(sink_ref, q_ref, kp_ref, kc_ref, vp_ref, vc_ref, qg_ref, kg_ref, dy_ref, kall_ref, probs_ref, sink_p_ref,
             dqkv_ref, dstat_ref, dk_acc, dv_acc, dqg_acc):
        del sink_ref
        n = pl.program_id(0)
        ones = _seg_ones()
        lane = lax.broadcasted_iota(jnp.int32, (1, BLK), 1)
        lo = lane < HEAD

        @pl.when(n == 0)
        def _():
            dk_acc[...] = jnp.zeros_like(dk_acc)
            dv_acc[...] = jnp.zeros_like(dv_acc)
            dqg_acc[...] = jnp.zeros_like(dqg_acc)
            dstat_ref[...] = jnp.zeros_like(dstat_ref)

        kraw = jnp.concatenate([kp_ref[...], kc_ref[...]], axis=0)
        vraw = jnp.concatenate([vp_ref[...], vc_ref[...]], axis=0)
        row = pl.multiple_of(n * BLK, BLK)
        prow = pl.multiple_of(jnp.maximum(n - 1, 0) * BLK, BLK)
        pairs = range(aw // BLK)
        cols = [slice(BLK * p, BLK * (p + 1)) for p in pairs]
        qs = [_head_rms(q_ref[:, cols[p]], qg_ref[...], ones) for p in pairs]
        qb = [qs[p][0].astype(BF16) for p in pairs]
        k2, v2 = [], []
        for b in range(n_kvblk):
            kn = _head_rms(kraw[:, BLK * b:BLK * (b + 1)], kg_ref[...], ones)[0]
            for odd in (0, 1):
                k2.append(_expand(kn, odd, lo))
                v2.append(_expand(vraw[:, BLK * b:BLK * (b + 1)], odd, lo))
        p2 = [probs_ref[:, 4 * BLK * p:4 * BLK * (p + 1)] for p in pairs]
        dob = [dy_ref[:, cols[p]].astype(BF16) for p in pairs]
        dp2 = [_nt(dob[p], v2[p // 2]) for p in pairs]
        deltas = jnp.zeros((BLK, BLK), F32)
        ds2 = []
        for p in pairs:
            ds = []
            for half in (0, 1):
                hs = slice(2 * BLK * half, 2 * BLK * (half + 1))
                ph = p2[p][:, hs]
                delta = jnp.sum(ph * dp2[p][:, hs], axis=-1, keepdims=True)
                ds.append(ph * (dp2[p][:, hs] - delta))
                deltas = jnp.where(lane == 2 * p + half, delta, deltas)
            ds2.append((jnp.concatenate(ds, axis=1) * (HEAD ** -0.5)).astype(BF16))
        dstat_ref[2:3, :] -= jnp.sum(sink_p_ref[...] * deltas, axis=0, keepdims=True)
        dqn = [_nn(ds2[p], k2[p // 2]) for p in pairs]
        dk2 = [_tn(ds2[p], qb[p]) for p in pairs]
        dv2 = [_tn(p2[p].astype(BF16), dob[p]) for p in pairs]
        for p in pairs:
            _, qhat, rstd = qs[p]
            dqhat = dqn[p] * qg_ref[...]
            proj_q = _segsum(dqhat * qhat, ones) * (1.0 / HEAD)
            dqkv_ref[pl.ds(row, BLK), cols[p]] = (rstd * (dqhat - qhat * proj_q)).astype(BF16)
            dqg_acc[:, cols[p]] += jnp.sum(dqn[p] * qhat, axis=0, keepdims=True)
        for b in range(n_kvblk):
            dkn = jnp.zeros((2 * BLK, BLK), F32)
            dvb = jnp.zeros((2 * BLK, BLK), F32)
            for odd in (0, 1):
                j = 2 * b + odd
                dkn = dkn + _fold(dk2[2 * j] + dk2[2 * j + 1], odd, lo)
                dvb = dvb + _fold(dv2[2 * j] + dv2[2 * j + 1], odd, lo)
            kcols = slice(BLK * b, BLK * (b + 1))
            dk_acc[pl.ds(prow, BLK), kcols] += dkn[:BLK]
            dv_acc[pl.ds(prow, BLK), kcols] += dvb[:BLK]
            dk_acc[pl.ds(row, BLK), kcols] += dkn[BLK:]
            dv_acc[pl.ds(row, BLK), kcols] += dvb[BLK:]

        @pl.when(n == nb - 1)
        def _():
            dqg = dqg_acc[:, 0:BLK]
            for p in range(1, aw // BLK):
                dqg = dqg + dqg_acc[:, BLK * p:BLK * (p + 1)]
            dstat_ref[0:1, :] = dqg + pltpu.roll(dqg, HEAD, axis=1)

            def kblock(i, dkg):
                r = pl.multiple_of(i * BLK, BLK)
                for b in range(n_kvblk):
                    kcols = slice(BLK * b, BLK * (b + 1))
                    _, khat, rstd = _head_rms(kall_ref[pl.ds(r, BLK), kcols], kg_ref[...], ones)
                    dkn = dk_acc[pl.ds(r, BLK), kcols]
                    dkhat = dkn * kg_ref[...]
                    proj_k = _segsum(dkhat * khat, ones) * (1.0 / HEAD)
                    dqkv_ref[pl.ds(r, BLK), aw + BLK * b:aw + BLK * (b + 1)] = (rstd * (dkhat - khat * proj_k)).astype(BF16)
                    dqkv_ref[pl.ds(r, BLK), aw + kv + BLK * b:aw + kv + BLK * (b + 1)] = dv_acc[pl.ds(r, BLK), kcols].astype(BF16)
                    dkg = dkg + jnp.sum(dkn * khat, axis=0, keepdims=True)
                return dkg

            dkg = lax.fori_loop(0, nb, kblock, jnp.zeros((1, BLK), F32))
            dstat_ref[1:2, :] = dkg + pltpu.roll(dkg, HEAD, axis=1)

    in_specs = _attn_specs(t, aw, kv) + [
        pl.BlockSpec((BLK, aw), lambda n: (n, 0)),
        pl.BlockSpec((t, kv), lambda n: (0, kb)),
        pl.BlockSpec((BLK, 4 * aw), lambda n: (n, 0)),
        pl.BlockSpec((BLK, BLK), lambda n: (n, 0)),
    ]
    args = [sinks, proj, proj, proj, proj, proj, qg, kg, dy, proj, probs, sink_p]
    body, in_specs, args = _ordered_behind(body, len(args), in_specs, args, after)
    return pl.pallas_call(
        body, name="attn_bwd", grid=(nb,),
        in_specs=in_specs,
        out_specs=[pl.BlockSpec((t, aw + 2 * kv), lambda n: (0, 0)), pl.BlockSpec((8, BLK), lambda n: (0, 0))],
        out_shape=[SDS((t, aw + 2 * kv), BF16), SDS((8, BLK), F32)],
        scratch_shapes=[pltpu.VMEM((t, kv), F32), pltpu.VMEM((t, kv), F32), pltpu.VMEM((1, aw), F32)],
        compiler_params=_params("arbitrary"),
    )(*args)


def _conv_taps(win, w_ref, shift_of):
    rows = win.shape[0]
    acc = None
    for j in range(CONV_K):
        term = pltpu.roll(win, (rows - shift_of(j)) % rows, axis=0)[:BLK] * w_ref[j:j + 1, :]
        acc = term if acc is None else acc + term
    return acc


def _layer_norm_fwd(z):
    mu = jnp.mean(z, axis=-1, keepdims=True)
    zc = z - mu
    rstd = lax.rsqrt(jnp.mean(zc * zc, axis=-1, keepdims=True) + EPS)
    return zc * rstd, rstd


def _layer_norm_bwd(dy, yhat, rstd, g):
    dyh = dy * g
    return rstd * (dyh - jnp.mean(dyh, axis=-1, keepdims=True) - yhat * jnp.mean(dyh * yhat, axis=-1, keepdims=True))


def _conv_fill_glu(a_ref, g_ref, hpad, nb):
    hpad[0:HALO, :] = jnp.zeros((HALO, hpad.shape[1]), F32)

    def fill(i, c):
        r = pl.multiple_of(i * BLK, BLK)
        hpad[pl.ds(pl.multiple_of(r + HALO, HALO), BLK), :] = a_ref[pl.ds(r, BLK), :] * _sigmoid(g_ref[pl.ds(r, BLK), :])
        return c

    lax.fori_loop(0, nb, fill, 0)


def _conv_specs(t, cw, d_in):
    base = (d_in - 4 * cw) // cw
    vec = pl.BlockSpec((1, cw), lambda i: (0, 0))
    return [
        pl.BlockSpec((t, cw), lambda i: (0, base)),
        pl.BlockSpec((t, cw), lambda i: (0, base + 1)),
        pl.BlockSpec((HALO, cw), lambda i: (0, 0)),
        vec, vec, vec,
    ]


def _conv_fwd(proj, cw_pad, cb, lg, lb, cw):
    t, d_in = proj.shape
    nb = t // BLK

    def body(a_ref, g_ref, w_ref, b_ref, lg_ref, lb_ref, o_ref, z_ref, hpad):
        _conv_fill_glu(a_ref, g_ref, hpad, nb)

        def blk(i, c):
            r = pl.multiple_of(i * BLK, BLK)
            z = _conv_taps(hpad[pl.ds(r, BLK + HALO), :], w_ref, lambda j: j + HALO - (CONV_K - 1)) + b_ref[...]
            z_ref[pl.ds(r, BLK), :] = z
            yhat, _ = _layer_norm_fwd(z)
            y = yhat * lg_ref[...] + lb_ref[...]
            o_ref[pl.ds(r, BLK), :] = y * _sigmoid(y)
            return c

        lax.fori_loop(0, nb, blk, 0)

    whole = pl.BlockSpec((t, cw), lambda i: (0, 0))
    return pl.pallas_call(
        body, name="conv_fwd", grid=(1,),
        in_specs=_conv_specs(t, cw, d_in),
        out_specs=[whole, whole],
        out_shape=[SDS((t, cw), F32), SDS((t, cw), F32)],
        scratch_shapes=[pltpu.VMEM((t + HALO, cw), F32)],
        compiler_params=_params("arbitrary"),
    )(proj, proj, cw_pad, cb, lg, lb)


def _conv_bwd(proj, dy, z, cw_pad, cb, lg, lb, cw):
    t, d_in = proj.shape
    nb = t // BLK

    def body(a_ref, g_ref, w_ref, b_ref, lg_ref, lb_ref, dy_ref, z_ref, dc_ref, dw_ref, dvec_ref, hpad, dzpad, dwacc):
        del b_ref
        _conv_fill_glu(a_ref, g_ref, hpad, nb)
        dzpad[t:t + HALO, :] = jnp.zeros((HALO, cw), F32)
        dwacc[...] = jnp.zeros_like(dwacc)

        def blk(i, carry):
            db, dlg, dlb = carry
            r = pl.multiple_of(i * BLK, BLK)
            win = hpad[pl.ds(r, BLK + HALO), :]
            yhat, rstd = _layer_norm_fwd(z_ref[pl.ds(r, BLK), :])
            y = yhat * lg_ref[...] + lb_ref[...]
            sg = _sigmoid(y)
            dyl = dy_ref[pl.ds(r, BLK), :] * (sg * (1.0 + y * (1.0 - sg)))
            dz = _layer_norm_bwd(dyl, yhat, rstd, lg_ref[...])
            dzpad[pl.ds(r, BLK), :] = dz
            for j in range(CONV_K):
                sh = j + HALO - (CONV_K - 1)
                prod = dz * pltpu.roll(win, (BLK + HALO - sh) % (BLK + HALO), axis=0)[:BLK]
                dwacc[8 * j:8 * j + 8, :] += jnp.sum(prod.reshape(BLK // 8, 8, cw), axis=0)
            return (db + jnp.sum(dz, axis=0, keepdims=True),
                    dlg + jnp.sum(dyl * yhat, axis=0, keepdims=True),
                    dlb + jnp.sum(dyl, axis=0, keepdims=True))

        zero = jnp.zeros((1, cw), F32)
        db, dlg, dlb = lax.fori_loop(0, nb, blk, (zero, zero, zero))
        dvec_ref[...] = jnp.zeros_like(dvec_ref)
        dvec_ref[0:1, :] = db
        dvec_ref[1:2, :] = dlg
        dvec_ref[2:3, :] = dlb
        dw_ref[...] = jnp.sum(dwacc[...].reshape(HALO, 8, cw), axis=1)

        def blk2(i, c):
            r = pl.multiple_of(i * BLK, BLK)
            dh = _conv_taps(dzpad[pl.ds(r, BLK + HALO), :], w_ref, lambda j: CONV_K - 1 - j)
            a = a_ref[pl.ds(r, BLK), :]
            sg = _sigmoid(g_ref[pl.ds(r, BLK), :])
            dc_ref[pl.ds(r, BLK), 0:cw] = (dh * sg).astype(BF16)
            dc_ref[pl.ds(r, BLK), cw:2 * cw] = (dh * a * sg * (1.0 - sg)).astype(BF16)
            return c

        lax.fori_loop(0, nb, blk2, 0)

    return pl.pallas_call(
        body, name="conv_bwd", grid=(1,),
        in_specs=_conv_specs(t, cw, d_in) + [pl.BlockSpec((t, cw), lambda i: (0, 0))] * 2,
        out_specs=[pl.BlockSpec((t, 2 * cw), lambda i: (0, 0)), pl.BlockSpec((HALO, cw), lambda i: (0, 0)),
                   pl.BlockSpec((8, cw), lambda i: (0, 0))],
        out_shape=[SDS((t, 2 * cw), BF16), SDS((HALO, cw), F32), SDS((8, cw), F32)],
        scratch_shapes=[pltpu.VMEM((t + HALO, cw), F32), pltpu.VMEM((t + HALO, cw), F32), pltpu.VMEM((8 * HALO, cw), F32)],
        compiler_params=_params("arbitrary"),
    )(proj, proj, cw_pad, cb, lg, lb, dy, z)


def _tril_bf16(w):
    r = lax.broadcasted_iota(jnp.int32, (BLK, BLK), 0)
    c = lax.broadcasted_iota(jnp.int32, (BLK, BLK), 1)
    return jnp.where(r >= c, w, 0.0).astype(BF16)


def _sgu_specs(sw, d_in, heads):
    base = (d_in - 2 * sw) // sw
    vec = pl.BlockSpec((1, sw), lambda n: (0, 0))
    return [
        pl.BlockSpec((BLK, sw), lambda n: (n, base)),
        pl.BlockSpec((BLK, sw), lambda n: (n, base + 1)),
        vec, vec,
        pl.BlockSpec((heads, BLK, BLK), lambda n: (0, 0, 0)),
        pl.BlockSpec((BLK, sw), lambda n: (0, 0)),
    ]


def _sgu_mix(w_ref, vnb, heads, sw, transpose):
    head_of = lax.broadcasted_iota(jnp.int32, (1, sw), 1) // HEAD
    s = jnp.zeros((BLK, sw), F32)
    for h in range(heads):
        wt = _tril_bf16(w_ref[h])
        mixed = _tn(wt, vnb) if transpose else _nn(wt, vnb)
        s = jnp.where(head_of == h, mixed, s)
    return s


def _sgu_fwd(proj, lg, lb, w, bias_full, sw):
    t, d_in = proj.shape
    heads = sw // HEAD

    def body(u_ref, v_ref, lg_ref, lb_ref, w_ref, bias_ref, o_ref):
        vhat, _ = _layer_norm_fwd(v_ref[...])
        vn = (vhat * lg_ref[...] + lb_ref[...]).astype(BF16)
        s = _sgu_mix(w_ref, vn, heads, sw, False) + bias_ref[...]
        o_ref[...] = u_ref[...] * s

    return pl.pallas_call(
        body, name="sgu_fwd", grid=(t // BLK,),
        in_specs=_sgu_specs(sw, d_in, heads),
        out_specs=pl.BlockSpec((BLK, sw), lambda n: (n, 0)),
        out_shape=SDS((t, sw), F32),
        compiler_params=_params("arbitrary"),
    )(proj, proj, lg, lb, w, bias_full)


def _sgu_bwd(proj, dy, lg, lb, w, bias_full, sw, after=None):
    t, d_in = proj.shape
    heads = sw // HEAD
    nb = t // BLK

    def body(u_ref, v_ref, lg_ref, lb_ref, w_ref, bias_ref, dy_ref, ds_ref, dw_ref, db_ref, dvec_ref, dbfull):
        n = pl.program_id(0)

        @pl.when(n == 0)
        def _():
            dw_ref[...] = jnp.zeros_like(dw_ref)
            dvec_ref[...] = jnp.zeros_like(dvec_ref)
            dbfull[...] = jnp.zeros_like(dbfull)

        vhat, rstd = _layer_norm_fwd(v_ref[...])
        vn = (vhat * lg_ref[...] + lb_ref[...]).astype(BF16)
        s = _sgu_mix(w_ref, vn, heads, sw, False) + bias_ref[...]
        dy = dy_ref[...]
        ds_ref[:, 0:sw] = (dy * s).astype(BF16)
        dsv = dy * u_ref[...]
        dbfull[...] += dsv
        head_of = lax.broadcasted_iota(jnp.int32, (1, sw), 1) // HEAD
        r = lax.broadcasted_iota(jnp.int32, (BLK, BLK), 0)
        c = lax.broadcasted_iota(jnp.int32, (BLK, BLK), 1)
        dsb = dsv.astype(BF16)
        for h in range(heads):
            dsh = jnp.where(head_of == h, dsv, 0.0).astype(BF16)
            dw_ref[h] += jnp.where(r >= c, _nt(dsh, vn), 0.0)
        dvn = _sgu_mix(w_ref, dsb, heads, sw, True)
        dvec_ref[0:1, :] += jnp.sum(dvn * vhat, axis=0, keepdims=True)
        dvec_ref[1:2, :] += jnp.sum(dvn, axis=0, keepdims=True)
        ds_ref[:, sw:2 * sw] = _layer_norm_bwd(dvn, vhat, rstd, lg_ref[...]).astype(BF16)

        @pl.when(n == nb - 1)
        def _():
            sel = (lax.broadcasted_iota(jnp.int32, (sw, BLK), 0) // HEAD == lax.broadcasted_iota(jnp.int32, (sw, BLK), 1)).astype(BF16)
            x = dbfull[...]
            hi = x.astype(BF16)
            r1 = x - hi.astype(F32)
            mid = r1.astype(BF16)
            low = (r1 - mid.astype(F32)).astype(BF16)
            db_ref[...] = _nn(hi, sel) + _nn(mid, sel) + _nn(low, sel)

    in_specs = _sgu_specs(sw, d_in, heads) + [pl.BlockSpec((BLK, sw), lambda n: (n, 0))]
    body, in_specs, args = _ordered_behind(body, 7, in_specs, [proj, proj, lg, lb, w, bias_full, dy], after)
    return pl.pallas_call(
        body, name="sgu_bwd", grid=(nb,),
        in_specs=in_specs,
        out_specs=[pl.BlockSpec((BLK, 2 * sw), lambda n: (n, 0)), pl.BlockSpec((heads, BLK, BLK), lambda n: (0, 0, 0)),
                   pl.BlockSpec((BLK, BLK), lambda n: (0, 0)), pl.BlockSpec((8, sw), lambda n: (0, 0))],
        out_shape=[SDS((t, 2 * sw), BF16), SDS((heads, BLK, BLK), F32), SDS((BLK, BLK), F32), SDS((8, sw), F32)],
        scratch_shapes=[pltpu.VMEM((BLK, sw), F32)],
        compiler_params=_params("arbitrary"),
    )(*args)


def _rms_fwd(x, g):
    return (x * lax.rsqrt(jnp.mean(x * x, axis=-1, keepdims=True) + EPS)) * g


def _rms_bwd(dh, x, g):
    rstd = lax.rsqrt(jnp.mean(x * x, axis=-1, keepdims=True) + EPS)
    xhat = x * rstd
    dgx = dh * g
    dx = rstd * (dgx - xhat * jnp.mean(dgx * xhat, axis=-1, keepdims=True))
    return dx, jnp.sum(dh * xhat, axis=0, keepdims=True)


def _rms_matmul(x, g, w, tm, tn, relu2, name, transposed=False):
    t, d = x.shape
    if w.ndim == 3:
        assert w.shape[2] == tn
        n = w.shape[0] * tn
        w_spec = pl.BlockSpec((None, d, tn), lambda i, j: (j, 0, 0))
    elif transposed:
        n = w.shape[0]
        w_spec = pl.BlockSpec((tn, d), lambda i, j: (j, 0))
    else:
        n = w.shape[1]
        w_spec = pl.BlockSpec((d, tn), lambda i, j: (0, j))

    def body(x_ref, g_ref, w_ref, h_ref, *outs):
        @pl.when(pl.program_id(1) == 0)
        def _():
            h_ref[...] = _rms_fwd(x_ref[...], g_ref[...]).astype(BF16)

        acc = _nt(h_ref[...], w_ref[...]) if transposed else _nn(h_ref[...], w_ref[...])
        if relu2:
            r = jnp.maximum(acc, 0.0)
            outs[0][...] = (r * r).astype(BF16)
            outs[1][...] = r.astype(BF16)
        else:
            outs[0][...] = acc

    tile = pl.BlockSpec((tm, tn), lambda i, j: (i, j))
    row = pl.BlockSpec((tm, d), lambda i, j: (i, 0))
    outs = [SDS((t, n), BF16), SDS((t, n), BF16)] if relu2 else [SDS((t, n), F32)]
    return pl.pallas_call(
        body, name=name, grid=(t // tm, n // tn),
        in_specs=[row, pl.BlockSpec((1, d), lambda i, j: (0, 0)), w_spec],
        out_specs=[row] + [tile] * len(outs),
        out_shape=[SDS((t, d), BF16)] + outs,
        compiler_params=_params("parallel", "arbitrary"),
    )(x, g, w)


def _group_rms_matmul(ys, g, w, res, tm, tn):
    t, d = res.shape
    widths = [y.shape[1] for y in ys]
    k = sum(widths)

    def body(*refs):
        y_refs, (g_ref, w_ref, res_ref, mix_ref, o_ref) = refs[:len(ys)], refs[len(ys):]

        @pl.when(pl.program_id(1) == 0)
        def _():
            c = 0
            for y_ref, wd in zip(y_refs, widths):
                mix_ref[:, c:c + wd] = _rms_fwd(y_ref[...], g_ref[:, c:c + wd]).astype(BF16)
                c += wd

        o_ref[...] = res_ref[...] + _nn(mix_ref[...], w_ref[...])

    tile = pl.BlockSpec((tm, tn), lambda i, j: (i, j))
    return pl.pallas_call(
        body, name="mix_out", grid=(t // tm, d // tn),
        in_specs=[pl.BlockSpec((tm, wd), lambda i, j: (i, 0)) for wd in widths] + [
            pl.BlockSpec((1, k), lambda i, j: (0, 0)), pl.BlockSpec((k, tn), lambda i, j: (0, j)), tile],
        out_specs=[pl.BlockSpec((tm, k), lambda i, j: (i, 0)), tile],
        out_shape=[SDS((t, k), BF16), SDS((t, d), F32)],
        compiler_params=_params("parallel", "arbitrary"),
    )(*ys, g, w, res)


def _matmul_res(a, w, res, tm, tn, tk):
    t, k = a.shape
    n = w.shape[1]
    nk = k // tk

    def body(a_ref, w_ref, res_ref, o_ref, acc):
        kk = pl.program_id(2)

        @pl.when(kk == 0)
        def _():
            acc[...] = res_ref[...]

        acc[...] += _nn(a_ref[...], w_ref[...])

        @pl.when(kk == nk - 1)
        def _():
            o_ref[...] = acc[...]

    tile = pl.BlockSpec((tm, tn), lambda i, j, kk: (i, j))
    return pl.pallas_call(
        body, name="mlp_down", grid=(t // tm, n // tn, nk),
        in_specs=[pl.BlockSpec((tm, tk), lambda i, j, kk: (i, kk)), pl.BlockSpec((tk, tn), lambda i, j, kk: (kk, j)), tile],
        out_specs=tile,
        out_shape=SDS((t, n), F32),
        scratch_shapes=[pltpu.VMEM((tm, tn), F32)],
        compiler_params=_params("parallel", "parallel", "arbitrary"),
    )(a, w, res)


def _loss_grad(y, target, tm):
    t, d = y.shape

    def body(y_ref, t_ref, dy_ref, dyb_ref, l_ref):
        @pl.when(pl.program_id(0) == 0)
        def _():
            l_ref[...] = jnp.zeros_like(l_ref)

        err = y_ref[...] - t_ref[...]
        dy = err * (1.0 / d)
        dy_ref[...] = dy
        dyb_ref[...] = dy.astype(BF16)
        per_row = jnp.mean(err * err, axis=-1, keepdims=True)
        l_ref[...] += jnp.sum(per_row, axis=0, keepdims=True) * 0.5

    row = pl.BlockSpec((tm, d), lambda i: (i, 0))
    return pl.pallas_call(
        body, name="loss_grad", grid=(t // tm,),
        in_specs=[row, row], out_specs=[row, row, pl.BlockSpec((8, BLK), lambda i: (0, 0))],
        out_shape=[SDS((t, d), F32), SDS((t, d), BF16), SDS((8, BLK), F32)],
        compiler_params=_params("arbitrary"),
    )(y, target)


def _mlp_dact(dxb, w_down, r, tn, after=None):
    t, d = dxb.shape
    f = w_down.shape[0]

    def body(dxb_ref, w_ref, r_ref, dup_ref):
        dup_ref[...] = (_nt(dxb_ref[...], w_ref[...]) * (2.0 * r_ref[...].astype(F32))).astype(BF16)

    tile = pl.BlockSpec((t, tn), lambda j: (0, j))
    body, in_specs, args = _ordered_behind(
        body, 3, [pl.BlockSpec((t, d), lambda j: (0, 0)), pl.BlockSpec((tn, d), lambda j: (j, 0)), tile],
        [dxb, w_down, r], after)
    return pl.pallas_call(
        body, name="mlp_dact", grid=(f // tn,),
        in_specs=in_specs, out_specs=tile,
        out_shape=SDS((t, f), BF16),
        compiler_params=_params("arbitrary"),
    )(*args)


def _grad_w(a, b, tm, tn, name, col_blocks=False, after=None, rows=None, into=None):
    t, m = a.shape
    n = b.shape[1]
    assert n % tn == 0 and m % tm == 0
    total, first = rows if rows else (m, 0)
    assert first % tm == 0
    first_tile = first // tm

    def body(a_ref, b_ref, *rest):
        rest[-1][...] = _tn(a_ref[...], b_ref[...]).astype(BF16)

    if col_blocks:
        out_spec = pl.BlockSpec((None, tm, tn), lambda i, j: (j, i, 0))
        out_shape = SDS((n // tn, m, tn), BF16)
    else:
        out_spec = pl.BlockSpec((tm, tn), lambda i, j: (first_tile + i, j))
        out_shape = SDS((total, n), BF16)
    in_specs = [pl.BlockSpec((t, tm), lambda i, j: (0, i)), pl.BlockSpec((t, tn), lambda i, j: (0, j))]
    args = [a, b]
    aliases = {}
    if into is not None:
        in_specs, args, aliases = in_specs + [ANY], args + [into], {2: 0}
    body, in_specs, args = _ordered_behind(body, len(args), in_specs, args, after)
    return pl.pallas_call(
        body, name=name, grid=(m // tm, n // tn),
        in_specs=in_specs, out_specs=out_spec, out_shape=out_shape, input_output_aliases=aliases,
        compiler_params=_params("parallel", "arbitrary"),
    )(*args)


def _grad_w_half(a, b, tm, tn, name, by_cols, sel, add=None, after=None):
    t, m = a.shape
    n = b.shape[1]
    if by_cols:
        per = n // N_DEV // tn
        grid, shape = (4, m // tm, per), (4, m, n // N_DEV)
        a_spec = pl.BlockSpec((t, tm), lambda q, i, j, s: (0, i))
        b_spec = pl.BlockSpec((t, tn), lambda q, i, j, s: (0, (2 * q + s[0]) * per + j))
    else:
        per = m // N_DEV // tm
        grid, shape = (4, per, n // tn), (4, m // N_DEV, n)
        a_spec = pl.BlockSpec((t, tm), lambda q, i, j, s: (0, (2 * q + s[0]) * per + i))
        b_spec = pl.BlockSpec((t, tn), lambda q, i, j, s: (0, j))
    assert per >= 1
    tile = pl.BlockSpec((None, tm, tn), lambda q, i, j, s: (q, i, j))

    def body(sel_ref, a_ref, b_ref, *rest):
        acc = _tn(a_ref[...], b_ref[...])
        if add is not None:
            acc = acc + rest[0][...].astype(F32)
        rest[-1][...] = acc.astype(BF16)

    in_specs, args = [a_spec, b_spec], [a, b]
    if add is not None:
        in_specs, args = in_specs + [tile], args + [add]
    if after is not None:
        in_specs, args = in_specs + [ANY], args + [after]
    return pl.pallas_call(
        body, name=name,
        grid_spec=pltpu.PrefetchScalarGridSpec(num_scalar_prefetch=1, grid=grid, in_specs=in_specs, out_specs=tile),
        out_shape=SDS(shape, BF16),
        compiler_params=_params("arbitrary", "arbitrary", "arbitrary"),
    )(sel, *args)


def _mlp_dnorm(dup, w_up, x, g, dres, tm, after=None):
    t, f = dup.shape
    d = x.shape[1]
    nk, _, tk = w_up.shape

    def body(a_ref, w_ref, x_ref, g_ref, dres_ref, dx_ref, dg_ref, acc):
        i, kk = pl.program_id(0), pl.program_id(1)

        @pl.when((i == 0) & (kk == 0))
        def _():
            dg_ref[...] = jnp.zeros_like(dg_ref)

        @pl.when(kk == 0)
        def _():
            acc[...] = jnp.zeros_like(acc)

        acc[...] += _nt(a_ref[...], w_ref[...])

        @pl.when(kk == nk - 1)
        def _():
            dx, dg = _rms_bwd(acc[...], x_ref[...], g_ref[...])
            dx_ref[...] = dres_ref[...] + dx
            dg_ref[0:1, :] += dg

    row = pl.BlockSpec((tm, d), lambda i, kk: (i, 0))
    in_specs = [pl.BlockSpec((tm, tk), lambda i, kk: (i, kk)), pl.BlockSpec((None, d, tk), lambda i, kk: (kk, 0, 0)),
                row, pl.BlockSpec((1, d), lambda i, kk: (0, 0)), row]
    body, in_specs, args = _ordered_behind(body, 5, in_specs, [dup, w_up, x, g, dres], after)
    return pl.pallas_call(
        body, name="mlp_dnorm", grid=(t // tm, nk),
        in_specs=in_specs,
        out_specs=[row, pl.BlockSpec((8, d), lambda i, kk: (0, 0))],
        out_shape=[SDS((t, d), F32), SDS((8, d), F32)],
        scratch_shapes=[pltpu.VMEM((tm, d), F32)],
        compiler_params=_params("arbitrary", "arbitrary"),
    )(*args)


def _mix_dnorm(dx, w_out, ys, g, tm, after=None):
    t, d = dx.shape
    k = w_out.shape[0]
    widths = [y.shape[1] for y in ys]

    def body(dx_ref, w_ref, *refs):
        y_refs = refs[:len(ys)]
        g_ref, dxb_ref = refs[len(ys)], refs[len(ys) + 1]
        dy_refs = refs[len(ys) + 2:2 * len(ys) + 2]
        dg_ref = refs[-1]

        @pl.when(pl.program_id(0) == 0)
        def _():
            dg_ref[...] = jnp.zeros_like(dg_ref)

        dxb = dx_ref[...].astype(BF16)
        dxb_ref[...] = dxb
        dmix = _nt(dxb, w_ref[...])
        c = 0
        for y_ref, dy_ref, wd in zip(y_refs, dy_refs, widths):
            dy, dg = _rms_bwd(dmix[:, c:c + wd], y_ref[...], g_ref[:, c:c + wd])
            dy_ref[...] = dy
            dg_ref[0:1, c:c + wd] += dg
            c += wd

    row = pl.BlockSpec((tm, d), lambda i: (i, 0))
    yspecs = [pl.BlockSpec((tm, wd), lambda i: (i, 0)) for wd in widths]
    in_specs = [row, pl.BlockSpec((k, d), lambda i: (0, 0))] + yspecs + [pl.BlockSpec((1, k), lambda i: (0, 0))]
    body, in_specs, args = _ordered_behind(body, len(in_specs), in_specs, [dx, w_out, *ys, g], after)
    return pl.pallas_call(
        body, name="mix_dnorm", grid=(t // tm,),
        in_specs=in_specs,
        out_specs=[row] + yspecs + [pl.BlockSpec((8, k), lambda i: (0, 0))],
        out_shape=[SDS((t, d), BF16)] + [SDS((t, wd), F32) for wd in widths] + [SDS((8, k), F32)],
        compiler_params=_params("arbitrary"),
    )(*args)


def _in_dnorm(dps, w_in_t, x, g, dres, tm):
    t, d = x.shape
    widths = [p.shape[1] for p in dps]
    offs = [sum(widths[:p]) for p in range(len(dps))]
    n_in = w_in_t.shape[0]

    def body(*refs):
        p_refs = refs[:len(dps)]
        w_ref, x_ref, g_ref, dres_ref, dx_ref, dxb_ref, dg_ref = refs[len(dps):]

        @pl.when(pl.program_id(0) == 0)
        def _():
            dg_ref[...] = jnp.zeros_like(dg_ref)

        acc = None
        for p_ref, off, wd in zip(p_refs, offs, widths):
            term = _nn(p_ref[...], w_ref[off:off + wd, :])
            acc = term if acc is None else acc + term
        dx, dg = _rms_bwd(acc, x_ref[...], g_ref[...])
        dx = dres_ref[...] + dx
        dx_ref[...] = dx
        dxb_ref[...] = dx.astype(BF16)
        dg_ref[0:1, :] += dg

    row = pl.BlockSpec((tm, d), lambda i: (i, 0))
    return pl.pallas_call(
        body, name="in_dnorm", grid=(t // tm,),
        in_specs=[pl.BlockSpec((tm, wd), lambda i: (i, 0)) for wd in widths] + [
            pl.BlockSpec((n_in, d), lambda i: (0, 0)), row, pl.BlockSpec((1, d), lambda i: (0, 0)), row],
        out_specs=[row, row, pl.BlockSpec((8, d), lambda i: (0, 0))],
        out_shape=[SDS((t, d), F32), SDS((t, d), BF16), SDS((8, d), F32)],
        compiler_params=_params("arbitrary"),
    )(*dps, w_in_t, x, g, dres)


def _tile(n, want):
    return min(n, want)


def _row_tile(n, want):
    return max(k for k in range(8, min(n, want) + 1, 8) if n % k == 0)


def _layer_fwd(x, p, fetch, after):
    t, d = x.shape
    aw, kv, cw, sw = d // 2, d // 8, d // 4, d // 4
    tm = _tile(t, 1024)
    w_in = fetch("w_in", after)
    h1, proj = _rms_matmul(x, p["ln1_g"], w_in, tm, 512 if w_in.shape[0] % 512 == 0 else 256, False, "in_proj",
                           transposed=True)
    y_attn, probs, sink_p = _attn_fwd(proj, p["qg"], p["kg"], p["sinks"], aw, kv)
    y_conv, z_conv = _conv_fwd(proj, p["conv_w"], p["conv_b"], p["conv_ln_g"], p["conv_ln_b"], cw)
    y_sgu = _sgu_fwd(proj, p["sgu_ln_g"], p["sgu_ln_b"], p["sgu_w"], p["sgu_bias"], sw)
    ys = [y_attn, y_conv, y_sgu]
    w_out = fetch("w_out", y_sgu)
    mix, x1 = _group_rms_matmul(ys, p["out_norm_g"], w_out, x, tm, _tile(d, 1024))
    w_up = fetch("w_up", x1)
    h2, act, r = _rms_matmul(x1, p["ln2_g"], w_up, tm, w_up.shape[2], True, "mlp_up")
    w_down = fetch("w_down", act)
    x2 = _matmul_res(act, w_down, x1, tm, _tile(d, 2048), 512)
    saved = dict(x=x, h1=h1, proj=proj, ys=ys, mix=mix, x1=x1, h2=h2, act=act, r=r,
                 w_in=w_in, w_out=w_out, w_up=w_up, w_down=w_down, probs=probs, sink_p=sink_p, z_conv=z_conv)
    return x2, saved


def _layer_bwd(dx2, dx2b, p, s, start, finish, share, carry):
    t, d = dx2.shape
    aw, kv, cw, sw = d // 2, d // 8, d // 4, d // 4
    tm = _tile(t, 512)
    dup = _mlp_dact(dx2b, s["w_down"], s["r"], 512, carry[0] if carry else None)
    tok = carry[1](dup) if carry else None
    half = functools.partial
    f = s["act"].shape[1]
    tok = start("w_down", half(_grad_w_half, s["act"], dx2b, _tile(f // N_DEV, 512), _tile(d, 2048), "grad_w_down", False), tok)
    dx1, d_ln2 = _mlp_dnorm(dup, s["w_up"], s["x1"], p["ln2_g"], dx2, tm, tok)
    tok = finish("w_down", dx1)
    tok = start("w_up", half(_grad_w_half, s["h2"], dup, _tile(d, 1024), s["w_up"].shape[2], "grad_w_up", True), tok)
    dx1b, dya, dyc, dys, d_onorm = _mix_dnorm(dx1, s["w_out"], s["ys"], p["out_norm_g"], _tile(t, 256), tok)
    tok = finish("w_up", dx1b)
    tok = start("w_out", half(_grad_w_half, s["mix"], dx1b, _tile(d // N_DEV, 512), _tile(d, 2048), "grad_w_out", False), tok)
    dsgu, d_sw, d_sb, d_svec = _sgu_bwd(s["proj"], dys, p["sgu_ln_g"], p["sgu_ln_b"], p["sgu_w"], p["sgu_bias"], sw, tok)
    tok = share("sgu_w", d_sw, None)
    dqkv, d_attn = _attn_bwd(s["proj"], dya, s["probs"], s["sink_p"], p["qg"], p["kg"], p["sinks"], aw, kv, tok)
    tok = finish("w_out", dqkv)
    dconv, d_cw, d_cvec = _conv_bwd(s["proj"], dyc, s["z_conv"], p["conv_w"], p["conv_b"], p["conv_ln_g"], p["conv_ln_b"], cw)
    dps = [dqkv, dconv, dsgu]
    dx, dxb, d_ln1 = _in_dnorm(dps, s["w_in"], s["x"], p["ln1_g"], dx1, _tile(t, 256))
    heads = sw // HEAD
    tok = share("rest", dict(
        ln1_g=d_ln1[0], q_norm_g=d_attn[0, :HEAD], k_norm_g=d_attn[1, :HEAD], sinks=d_attn[2, :aw // HEAD],
        conv_w=d_cw[:CONV_K], conv_b=d_cvec[0], conv_ln_g=d_cvec[1], conv_ln_b=d_cvec[2],
        sgu_ln_g=d_svec[0], sgu_ln_b=d_svec[1], sgu_b=d_sb[:, :heads].T,
        out_norm_g=d_onorm[0], ln2_g=d_ln2[0]), tok)
    d_in, first, g_in = sum(dp.shape[1] for dp in dps), 0, None
    tm_in = 512 if all(dp.shape[1] % 512 == 0 for dp in dps) else 256
    for k, dp in enumerate(dps):
        g_in = _grad_w(dp, s["h1"], tm_in, _tile(d, 2048), f"grad_w_in_{k}", after=tok if k == 0 else None,
                       rows=(d_in, first), into=g_in)
        first += dp.shape[1]
    tok = start("w_in", g_in, None)
    carry = (tok, functools.partial(finish, "w_in"))
    return dx, dxb, carry


def _layer_params(l, small):
    row = lambda v: v[l][None, :]
    two = lambda v: jnp.tile(v[l], 2)[None, :]
    return dict(
        ln1_g=row(small["ln1_g"]), ln2_g=row(small["ln2_g"]), out_norm_g=row(small["out_norm_g"]),
        qg=two(small["q_norm_g"]), kg=two(small["k_norm_g"]), sinks=small["sinks"][l],
        conv_w=jnp.pad(small["conv_w"][l], ((0, HALO - CONV_K), (0, 0))),
        conv_b=row(small["conv_b"]), conv_ln_g=row(small["conv_ln_g"]), conv_ln_b=row(small["conv_ln_b"]),
        sgu_ln_g=row(small["sgu_ln_g"]), sgu_ln_b=row(small["sgu_ln_b"]), sgu_w=small["sgu_w"][l],
        sgu_bias=jnp.repeat(small["sgu_b"][l].T, HEAD, axis=1),
    )


def _local_step(x, target, small, depth, fetch, start, finish, share, after):
    params = [_layer_params(l, small) for l in range(depth)]
    saved = []
    h = x
    for l in range(depth):
        h, s = _layer_fwd(h, params[l], functools.partial(fetch, l), after)
        after = h
        saved.append(s)
    dy, dyb, lsum = _loss_grad(h, target, _tile(x.shape[0], 512))
    carry = None
    for l in reversed(range(depth)):
        dy, dyb, carry = _layer_bwd(dy, dyb, params[l], saved[l], functools.partial(start, l),
                                    functools.partial(finish, l), functools.partial(share, l), carry)
    return lsum[0, 0], dy, carry


BIG = ("w_in", "w_out", "w_up", "w_down")
HBM = pl.BlockSpec(memory_space=pltpu.HBM)
SEMS = pl.BlockSpec(memory_space=pltpu.SEMAPHORE)
EFFECT = pltpu.SideEffectType.DATAFLOW_SIDE_EFFECTING


def _mesh_pos():
    return lax.axis_index("x"), lax.axis_index("y"), lax.axis_index("c")


def _other_chips(x, y):
    return [(1 - x, y), (x, 1 - y), (1 - x, 1 - y)]


def _copies(plan, refs, sends, recvs):
    x, y, c = _mesh_pos()
    return [pltpu.make_async_remote_copy(src_ref=src, dst_ref=dst, send_sem=sends.at[k], recv_sem=recvs.at[k],
                                         device_id=dev, device_id_type=MESH)
            for k, (src, dst, dev) in enumerate(plan(x, y, c, refs))]


def _gather_plan(x, y, c, refs):
    mine = refs[0].at[4 * x + 2 * y + c]
    return [(mine, mine, (x, y, 1 - c))] + [(mine, mine, (*chip, c)) for chip in _other_chips(x, y)]


def _all_plan(x, y, c, refs):
    mine = refs[0].at[4 * x + 2 * y + c]
    return [(mine, mine, (x ^ (k >> 2), y ^ ((k >> 1) & 1), c ^ (k & 1))) for k in range(1, N_DEV)]


def _pair_plan(x, y, c, refs):
    blocks, land = refs
    return [(blocks.at[2 * q + (1 - c)], land.at[q], (x, y, 1 - c)) for q in range(4)]


def _half_plan(x, y, c, refs):
    blocks, land = refs
    return [(blocks.at[q], land.at[q], (x, y, 1 - c)) for q in range(4)]


def _chips_plan(x, y, c, refs):
    sums, land = refs
    return [(sums.at[2 * chip[0] + chip[1]], land.at[k], (*chip, c)) for k, chip in enumerate(_other_chips(x, y))]


def _start_exchanges(name, groups, after=None):
    flat = [a for arrays, _, _ in groups for a in arrays]
    n_arr, n_g = len(flat), len(groups)
    n_in = n_arr + (after is not None)

    def body(*refs):
        ins, sems, token = refs[:n_arr], refs[n_in:n_in + 2 * n_g], refs[-1]
        off = 0
        for gi, (arrays, plan, _) in enumerate(groups):
            for cp in _copies(plan, ins[off:off + len(arrays)], sems[2 * gi], sems[2 * gi + 1]):
                cp.start()
            off += len(arrays)
        token[...] = jnp.zeros_like(token)

    res = pl.pallas_call(
        body, name=name,
        out_shape=[pltpu.SemaphoreType.DMA((n,)) for _, _, n in groups for _ in (0, 1)]
        + [pltpu.HBM(a.shape, a.dtype) for a in flat] + [SDS((8, BLK), F32)],
        in_specs=[HBM] * n_arr + [ANY] * (after is not None),
        out_specs=[SEMS] * (2 * n_g) + [HBM] * n_arr + [pl.BlockSpec(memory_space=pltpu.VMEM)],
        input_output_aliases={i: 2 * n_g + i for i in range(n_arr)},
        compiler_params=pltpu.CompilerParams(has_side_effects=EFFECT),
    )(*[pltpu.with_memory_space_constraint(a, pltpu.HBM) for a in flat], *([after] if after is not None else []))
    sems, thru, token = res[:2 * n_g], res[2 * n_g:2 * n_g + n_arr], res[-1]
    out, off = [], 0
    for gi, (arrays, _, _) in enumerate(groups):
        out.append((list(thru[off:off + len(arrays)]), sems[2 * gi], sems[2 * gi + 1]))
        off += len(arrays)
    return out, token


def _wait_exchange(name, arrays, sends, recvs, plan, after):
    n = len(arrays)

    def body(*refs):
        for cp in _copies(plan, refs[:n], refs[n], refs[n + 1]):
            cp.wait_send()
            cp.wait_recv()

    return pl.pallas_call(
        body, name=name,
        out_shape=[pltpu.HBM(a.shape, a.dtype) for a in arrays],
        in_specs=[HBM] * n + [SEMS, SEMS, ANY],
        out_specs=[HBM] * n,
        input_output_aliases={i: i for i in range(n)},
        compiler_params=pltpu.CompilerParams(has_side_effects=EFFECT),
    )(*arrays, sends, recvs, after)


def _gather_finish(land, name):
    def body(land_ref, out_ref, send_sems, recv_sems):
        del land_ref
        x, y, c = _mesh_pos()
        cps = []
        for k, chip in enumerate(_other_chips(x, y)):
            block = out_ref.at[4 * chip[0] + 2 * chip[1] + c]
            cps.append(pltpu.make_async_remote_copy(
                src_ref=block, dst_ref=block, send_sem=send_sems.at[k], recv_sem=recv_sems.at[k],
                device_id=(x, y, 1 - c), device_id_type=MESH))
        for cp in cps:
            cp.start()
        for cp in cps:
            cp.wait()

    return pl.pallas_call(
        body, name=name,
        in_specs=[ANY], out_specs=ANY,
        out_shape=SDS(land.shape, land.dtype),
        input_output_aliases={0: 0},
        scratch_shapes=[pltpu.SemaphoreType.DMA((3,)), pltpu.SemaphoreType.DMA((3,))],
    )(land)


def _all_gather(shards, name):
    n = len(shards)

    def body(*refs):
        ins, outs = refs[:n], refs[n:2 * n]
        send_sems, recv_sems, local_sems = refs[2 * n:]
        x, y, c = _mesh_pos()
        me, sibling = (x, y, c), (x, y, 1 - c)
        chips = _other_chips(x, y)

        def copy(a, k, block, to, src=None):
            dst = outs[a].at[4 * block[0] + 2 * block[1] + block[2]]
            return pltpu.make_async_remote_copy(
                src_ref=dst if src is None else src, dst_ref=dst,
                send_sem=send_sems.at[a, k], recv_sem=recv_sems.at[a, k], device_id=to, device_id_type=MESH)

        mine = [pltpu.make_async_copy(ins[a], outs[a].at[4 * x + 2 * y + c], local_sems.at[a]) for a in range(n)]
        for cp in mine:
            cp.start()
        first = []
        for a in range(n):
            first.append(copy(a, 0, me, sibling, src=ins[a]))
            first += [copy(a, 1 + j, me, (*chip, c), src=ins[a]) for j, chip in enumerate(chips)]
        for cp in first:
            cp.start()
        passed = []
        for j, chip in enumerate(chips):
            for a in range(n):
                copy(a, 1 + j, (*chip, c), me).wait_recv()
                fwd = copy(a, 4 + j, (*chip, c), sibling)
                fwd.start()
                passed.append(fwd)
        for a in range(n):
            copy(a, 0, sibling, me).wait_recv()
            for j, chip in enumerate(chips):
                copy(a, 4 + j, (*chip, 1 - c), me).wait_recv()
        for cp in first + passed:
            cp.wait_send()
        for cp in mine:
            cp.wait()

    outs = pl.pallas_call(
        body, name=name,
        in_specs=[ANY] * n, out_specs=[ANY] * n,
        out_shape=[SDS((N_DEV,) + s.shape, s.dtype) for s in shards],
        scratch_shapes=[pltpu.SemaphoreType.DMA((n, 7)), pltpu.SemaphoreType.DMA((n, 7)), pltpu.SemaphoreType.DMA((n,))],
    )(*shards)
    return list(outs)


def _pair_add(own, got, c, name):
    _, r, cols = own.shape
    tr = _row_tile(r, 512)

    def body(c_ref, own_ref, got_ref, o_ref):
        o_ref[...] = (own_ref[...].astype(F32) + got_ref[...].astype(F32)).astype(BF16)

    return pl.pallas_call(
        body, name=name,
        grid_spec=pltpu.PrefetchScalarGridSpec(
            num_scalar_prefetch=1, grid=(4, r // tr),
            in_specs=[pl.BlockSpec((None, tr, cols), lambda q, i, c_ref: (2 * q + c_ref[0], i, 0)),
                      pl.BlockSpec((None, tr, cols), lambda q, i, c_ref: (q, i, 0))],
            out_specs=pl.BlockSpec((None, tr, cols), lambda q, i, c_ref: (q, i, 0))),
        out_shape=SDS((4, r, cols), BF16),
        compiler_params=_params("arbitrary", "arbitrary"),
    )(c, own, got)


def _adamw(w, g, m, v):
    m = ADAM_B1 * m + (1.0 - ADAM_B1) * g
    v = ADAM_B2 * v + (1.0 - ADAM_B2) * (g * g)
    m_hat = m / (1.0 - ADAM_B1 ** ADAM_STEP)
    v_hat = v / (1.0 - ADAM_B2 ** ADAM_STEP)
    delta = -ADAM_LR * (m_hat / (jnp.sqrt(v_hat) + ADAM_EPS) + ADAM_WD * w)
    return delta, m, v


def _adamw_layer(chip_sum, got, chip, w, m, v, layer, prev, name):
    depth, r, cols = w.shape
    tr = _row_tile(r, 256)

    def body(chip_ref, sum_ref, got_ref, w_ref, m_ref, v_ref, *rest):
        g_out, d_out, m_out, v_out, token = rest[-5:]
        g = sum_ref[...].astype(F32) + got_ref[0].astype(F32) + got_ref[1].astype(F32) + got_ref[2].astype(F32)
        delta, mm, vv = _adamw(w_ref[...], g, m_ref[...], v_ref[...])
        g_out[...] = g
        d_out[...] = delta
        m_out[...] = mm
        v_out[...] = vv
        token[...] = jnp.zeros_like(token)

    shard = pl.BlockSpec((None, tr, cols), lambda i, chip_ref: (layer, i, 0))
    in_specs = [pl.BlockSpec((None, tr, cols), lambda i, chip_ref: (chip_ref[0], i, 0)),
                pl.BlockSpec((3, tr, cols), lambda i, chip_ref: (0, i, 0)), shard, shard, shard]
    args = [chip, chip_sum, got, w, m, v]
    aliases = {}
    if prev is not None:
        in_specs += [ANY] * 4
        aliases = {len(args) + k: k for k in range(4)}
        args += list(prev)
    res = pl.pallas_call(
        body, name=name,
        grid_spec=pltpu.PrefetchScalarGridSpec(
            num_scalar_prefetch=1, grid=(r // tr,), in_specs=in_specs,
            out_specs=[shard] * 4 + [pl.BlockSpec((8, BLK), lambda i, chip_ref: (0, 0))]),
        out_shape=[SDS(w.shape, F32)] * 4 + [SDS((8, BLK), F32)],
        input_output_aliases=aliases,
        compiler_params=_params("arbitrary"),
    )(*args)
    return list(res[:4]), res[4]


WIDE = ("ln1_g", "out_norm_g", "ln2_g", "conv_b", "conv_ln_g", "conv_ln_b", "sgu_ln_g", "sgu_ln_b")
NARROW = ("q_norm_g", "k_norm_g", "sinks")


def _small_rows(w):
    rows, r = {}, 0
    for n in WIDE:
        rows[n] = (r, w[n].shape[1] // BLK)
        r += rows[n][1]
    for n in NARROW:
        rows[n] = (r, 1)
        r += 1
    r = -(-r // 8) * 8
    rows["sgu_b"] = (r, w["sgu_b"].shape[1])
    r += -(-rows["sgu_b"][1] // 8) * 8
    rows["conv_w"] = (r, N_DEV * HALO)
    return rows, r + N_DEV * HALO


def _pack_small(small, rows, total):
    parts, r = [], 0

    def put(name, block):
        nonlocal r
        first = rows[name][0]
        if first > r:
            parts.append(jnp.zeros((first - r, BLK), F32))
        parts.append(block)
        r = first + block.shape[0]

    for n in WIDE:
        put(n, small[n].reshape(-1, BLK))
    for n in NARROW:
        put(n, jnp.pad(small[n], (0, BLK - small[n].shape[0]))[None])
    put("sgu_b", small["sgu_b"])
    cw = small["conv_w"]
    per_dev = cw.shape[1] // N_DEV
    blocks = jnp.transpose(cw.reshape(CONV_K, N_DEV, per_dev), (1, 0, 2))
    put("conv_w", jnp.pad(blocks, ((0, 0), (0, HALO - CONV_K), (0, BLK - per_dev))).reshape(N_DEV * HALO, BLK))
    if total > r:
        parts.append(jnp.zeros((total - r, BLK), F32))
    return jnp.concatenate(parts)


def _small_update(me, packed, sgu_w_all, w, m, v, rows):
    depth = len(packed)
    total = packed[0].shape[1]
    names = SMALL + ("conv_w",)
    heads = w["sgu_w"].shape[1]
    per_dev = w["conv_w"].shape[2]

    def body(me_ref, *refs):
        packed_refs, sgu_refs = refs[:depth], refs[depth:2 * depth]
        ins = refs[2 * depth:2 * depth + 3 * len(names)]
        outs = refs[2 * depth + 3 * len(names):2 * depth + 7 * len(names)]
        acc, conv = refs[-2:]
        io = {n: (ins[3 * i:3 * i + 3], outs[4 * i:4 * i + 4]) for i, n in enumerate(names)}

        def update(n):
            (w_ref, m_ref, v_ref), (g_out, d_out, m_out, v_out) = io[n]
            delta, mm, vv = _adamw(w_ref[...], g_out[...], m_ref[...], v_ref[...])
            d_out[...] = delta
            m_out[...] = mm
            v_out[...] = vv

        mine = pl.ds(pl.multiple_of(rows["conv_w"][0] + HALO * me_ref[0], 8), HALO)
        for l in range(depth):
            s = packed_refs[l][0]
            c = packed_refs[l][0, mine, :]
            for k in range(1, N_DEV):
                s = s + packed_refs[l][k]
                c = c + packed_refs[l][k, mine, :]
            acc[l] = s
            conv[l] = c
        for n in WIDE:
            first, nr = rows[n]
            for l in range(depth):
                for j in range(nr):
                    io[n][1][0][l:l + 1, BLK * j:BLK * (j + 1)] = acc[l, first + j:first + j + 1, :]
            update(n)
        for n in NARROW:
            first, lanes = rows[n][0], w[n].shape[1]
            for l in range(depth):
                io[n][1][0][l:l + 1, :] = acc[l, first:first + 1, 0:lanes]
            update(n)
        first, nr = rows["sgu_b"]
        for l in range(depth):
            io["sgu_b"][1][0][l] = acc[l, first:first + nr, :]
            io["conv_w"][1][0][l] = conv[l, 0:CONV_K, 0:per_dev]
        update("sgu_b")
        update("conv_w")
        (w_ref, m_ref, v_ref), (g_out, d_out, m_out, v_out) = io["sgu_w"]
        for l in range(depth):
            for h in range(heads):
                g = sgu_refs[l][0, h]
                for k in range(1, N_DEV):
                    g = g + sgu_refs[l][k, h]
                delta, mm, vv = _adamw(w_ref[l, h], g, m_ref[l, h], v_ref[l, h])
                g_out[l, h] = g
                d_out[l, h] = delta
                m_out[l, h] = mm
                v_out[l, h] = vv

    def whole(a):
        nd = len(a.shape)
        return pl.BlockSpec(a.shape, lambda i, me_ref: (0,) * nd)

    small_in = [t[n] for n in names for t in (w, m, v)]
    res = pl.pallas_call(
        body, name="small_update",
        grid_spec=pltpu.PrefetchScalarGridSpec(
            num_scalar_prefetch=1, grid=(1,),
            in_specs=[whole(a) for a in list(packed) + list(sgu_w_all) + small_in],
            out_specs=[whole(w[n]) for n in names for _ in range(4)],
            scratch_shapes=[pltpu.VMEM((depth, total, BLK), F32), pltpu.VMEM((depth, HALO, BLK), F32)]),
        out_shape=[SDS(w[n].shape, F32) for n in names for _ in range(4)],
        compiler_params=_params("arbitrary"),
    )(me, *packed, *sgu_w_all, *small_in)
    return {n: list(res[4 * i:4 * i + 4]) for i, n in enumerate(names)}


def _pack(arrays):
    flat = jnp.concatenate([a.reshape(-1) for a in arrays])
    rows = -(-flat.shape[0] // (8 * BLK)) * 8
    return jnp.pad(flat, (0, rows * BLK - flat.shape[0])).reshape(rows, BLK)


def _unpack(packed, like):
    flat = packed.reshape(-1)
    out, off = [], 0
    for a in like:
        out.append(flat[off:off + a.size].reshape(a.shape))
        off += a.size
    return out


SMALL = ("ln1_g", "q_norm_g", "k_norm_g", "sinks", "conv_b", "conv_ln_g", "conv_ln_b", "sgu_ln_g", "sgu_ln_b",
         "sgu_w", "sgu_b", "out_norm_g", "ln2_g")
ORDER = ("ln1_g", "w_in", "q_norm_g", "k_norm_g", "sinks", "conv_w", "conv_b", "conv_ln_g", "conv_ln_b", "sgu_ln_g",
         "sgu_ln_b", "sgu_w", "sgu_b", "out_norm_g", "w_out", "ln2_g", "w_up", "w_down")


def _step(x, target, w, m, v):
    depth = w["ln1_g"].shape[0]
    xpos, ypos, cpos = _mesh_pos()
    me = 4 * xpos + 2 * ypos + cpos
    c_arr = jnp.reshape(cpos, (1,)).astype(jnp.int32)
    chip_arr = jnp.reshape(2 * xpos + ypos, (1,)).astype(jnp.int32)

    d = x.shape[1]
    def own_block(shard):
        return lax.dynamic_update_slice_in_dim(lax.empty((N_DEV,) + shard.shape, shard.dtype), shard[None], me, axis=0)

    cw = w["conv_w"]
    order = [(0, "conv_w")] + [(l, n) for l in range(depth) for n in BIG]
    started, gather_token = _start_exchanges("gather_start", [
        ([own_block(_pack([cw]) if n == "conv_w" else w[n][l].astype(BF16))], _gather_plan, 4) for l, n in order])
    pending = dict(zip(order, started))

    def fetch(l, n, after):
        arrays, sends, recvs = pending.pop((l, n))
        land, = _wait_exchange(f"gather_wait_{l}_{n}", arrays, sends, recvs, _gather_plan,
                               gather_token if after is None else after)
        full = _gather_finish(land, "gather_finish_" + n)
        return full if n in ("w_up", "conv_w") else full.reshape(-1, d)

    cw_all = fetch(0, "conv_w", None)
    cw_full = jnp.concatenate([_unpack(cw_all[k], [cw])[0] for k in range(N_DEV)], axis=-1)
    small = {n: w[n] for n in SMALL}
    small["conv_w"] = cw_full

    to_sibling, inflight = {}, []

    def start(l, n, grad, after):
        if callable(grad):
            blocks, plan = grad(sel=1 - c_arr, after=after), _half_plan
        else:
            blocks, plan = grad.reshape(N_DEV, -1, d), _pair_plan
        (going,), token = _start_exchanges(f"pair_start_{l}_{n}", [
            ([blocks, lax.empty((4,) + blocks.shape[1:], BF16)], plan, 4)])
        to_sibling[l, n] = (going, plan, grad)
        return token

    def finish(l, n, after):
        (arrays, sends, recvs), plan, grad = to_sibling.pop((l, n))
        blocks, from_sibling = _wait_exchange(f"pair_wait_{l}_{n}", arrays, sends, recvs, plan, after)
        if callable(grad):
            chip_sums = grad(sel=c_arr, add=from_sibling)
        else:
            chip_sums = _pair_add(blocks, from_sibling, c_arr, "pair_add_" + n)
        (going,), token = _start_exchanges(f"reduce_start_{l}_{n}", [
            ([chip_sums, lax.empty((3,) + chip_sums.shape[1:], BF16)], _chips_plan, 3)])
        inflight.append((l, n, going))
        return token

    rows, total = _small_rows(w)
    shared = {}

    def share(l, kind, grads, after):
        block = grads if kind == "sgu_w" else _pack_small(grads, rows, total)
        (shared[l, kind],), token = _start_exchanges(f"small_start_{l}_{kind}", [([own_block(block)], _all_plan, N_DEV - 1)],
                                                     after)
        return token

    loss, grad_x, (after, finish_last) = _local_step(x, target, small, depth, fetch, start, finish, share, None)

    as3d = lambda a: a.reshape(depth, -1, a.shape[-1])
    results = {n: None for n in BIG}

    def settle(l, n, going, after):
        arrays, sends, recvs = going
        chip_sums, got = _wait_exchange(f"reduce_wait_{l}_{n}", arrays, sends, recvs, _chips_plan, after)
        results[n], token = _adamw_layer(chip_sums, got, chip_arr, as3d(w[n]), as3d(m[n]), as3d(v[n]), l, results[n],
                                         f"adamw_{l}_{n}")
        return token

    early, late = inflight[:2], inflight[2:]
    for l, n, going in early:
        after = settle(l, n, going, after)
    after = finish_last(after)

    landed = {}
    for key in sorted(shared):
        arrays, sends, recvs = shared[key]
        landed[key], = _wait_exchange(f"small_wait_{key[0]}_{key[1]}", arrays, sends, recvs, _all_plan, after)
        after = landed[key]
    out = _small_update(jnp.reshape(me, (1,)).astype(jnp.int32), [landed[l, "rest"] for l in range(depth)],
                        [landed[l, "sgu_w"] for l in range(depth)], w, m, v, rows)
    after = out["sinks"][1]

    for l, n, going in late + inflight[-1:]:
        after = settle(l, n, going, after)
    out.update({n: [r.reshape(w[n].shape) for r in results[n]] for n in BIG})

    loss = lax.psum(loss, ("x", "y", "c"))
    return (loss, grad_x[None]) + tuple(out[n][k] for k in range(4) for n in ORDER)


def kernel(x, ln1_g, w_in, q_norm_g, k_norm_g, sinks, conv_w, conv_b, conv_ln_g, conv_ln_b, sgu_ln_g, sgu_ln_b, sgu_w, sgu_b, out_norm_g, w_out, ln2_g, w_up, w_down, loss_target, m_ln1_g, m_w_in, m_q_norm_g, m_k_norm_g, m_sinks, m_conv_w, m_conv_b, m_conv_ln_g, m_conv_ln_b, m_sgu_ln_g, m_sgu_ln_b, m_sgu_w, m_sgu_b, m_out_norm_g, m_w_out, m_ln2_g, m_w_up, m_w_down, v_ln1_g, v_w_in, v_q_norm_g, v_k_norm_g, v_sinks, v_conv_w, v_conv_b, v_conv_ln_g, v_conv_ln_b, v_sgu_ln_g, v_sgu_ln_b, v_sgu_w, v_sgu_b, v_out_norm_g, v_w_out, v_ln2_g, v_w_up, v_w_down):
    w = dict(zip(ORDER, (ln1_g, w_in, q_norm_g, k_norm_g, sinks, conv_w, conv_b, conv_ln_g, conv_ln_b, sgu_ln_g, sgu_ln_b,
                         sgu_w, sgu_b, out_norm_g, w_out, ln2_g, w_up, w_down)))
    m = dict(zip(ORDER, (m_ln1_g, m_w_in, m_q_norm_g, m_k_norm_g, m_sinks, m_conv_w, m_conv_b, m_conv_ln_g, m_conv_ln_b,
                         m_sgu_ln_g, m_sgu_ln_b, m_sgu_w, m_sgu_b, m_out_norm_g, m_w_out, m_ln2_g, m_w_up, m_w_down)))
    v = dict(zip(ORDER, (v_ln1_g, v_w_in, v_q_norm_g, v_k_norm_g, v_sinks, v_conv_w, v_conv_b, v_conv_ln_g, v_conv_ln_b,
                         v_sgu_ln_g, v_sgu_ln_b, v_sgu_w, v_sgu_b, v_out_norm_g, v_w_out, v_ln2_g, v_w_up, v_w_down)))
    for group in (w, m, v):
        group["w_in"] = jnp.swapaxes(group["w_in"], 1, 2)
    out = list(_step(x[0], loss_target[0], w, m, v))
    for k in range(4):
        i = 2 + k * len(ORDER) + ORDER.index("w_in")
        out[i] = jnp.swapaxes(out[i], 1, 2)
    return tuple(out)
```

```python
import functools

import jax
import jax.numpy as jnp
from jax import lax
from jax.experimental import pallas as pl
from jax.experimental.pallas import tpu as pltpu

F32 = jnp.float32
BF16 = jnp.bfloat16
SDS = jax.ShapeDtypeStruct

EPS = 1e-6
NEG_INF = -1e30
HEAD = 64
BLK = 128
CONV_K = 31
HALO = 32
N_DEV = 8

ADAM_LR = 0.001
ADAM_B1 = 0.9
ADAM_B2 = 0.999
ADAM_EPS = 1e-08
ADAM_WD = 0.01
ADAM_STEP = 10

VMEM_LIMIT = 56 * 1024 * 1024

MESH = pl.DeviceIdType.MESH


def _params(*sem):
    return pltpu.CompilerParams(dimension_semantics=sem, vmem_limit_bytes=VMEM_LIMIT)


def _nt(a, b):
    return lax.dot_general(a, b, (((1,), (1,)), ((), ())), preferred_element_type=F32)


def _tn(a, b):
    return lax.dot_general(a, b, (((0,), (0,)), ((), ())), preferred_element_type=F32)


def _nn(a, b):
    return jnp.dot(a, b, preferred_element_type=F32)


def _sigmoid(x):
    return 1.0 / (1.0 + jnp.exp(-x))


ANY = pl.BlockSpec(memory_space=pl.ANY)


def _ordered_behind(body, n_in, in_specs, args, after):
    if after is None:
        return body, in_specs, args
    return (lambda *refs: body(*refs[:n_in], *refs[n_in + 1:])), list(in_specs) + [ANY], list(args) + [after]


def _seg_ones():
    return lax.broadcasted_iota(jnp.int32, (1, BLK), 1) < HEAD


def _segsum(x, first):
    head0 = jnp.where(first, x, 0.0)
    s0 = jnp.sum(head0, axis=-1, keepdims=True)
    s1 = jnp.sum(x - head0, axis=-1, keepdims=True)
    return jnp.where(first, s0, s1)


def _head_rms(x, gain, ones):
    rstd = lax.rsqrt(_segsum(x * x, ones) * (1.0 / HEAD) + EPS)
    xhat = x * rstd
    return xhat * gain, xhat, rstd


def _expand(x, odd, lo):
    if odd:
        xl = pltpu.roll(jnp.where(lo, 0.0, x), HEAD, axis=1)
    else:
        xl = jnp.where(lo, x, 0.0)
    xh = pltpu.roll(xl, HEAD, axis=1)
    return jnp.concatenate([xl, xh], axis=0).astype(BF16)


def _fold(g2, odd, lo):
    r = g2.shape[0] // 2
    s = jnp.where(lo, g2[:r], 0.0) + pltpu.roll(jnp.where(lo, 0.0, g2[r:]), HEAD, axis=1)
    if odd:
        s = pltpu.roll(s, HEAD, axis=1)
    return s


def _attn_mask(n):
    qi = lax.broadcasted_iota(jnp.int32, (BLK, 2 * BLK), 0)
    sj = lax.broadcasted_iota(jnp.int32, (BLK, 2 * BLK), 1)
    rel = qi + BLK - sj
    return (rel >= 0) & (rel < BLK) & ((sj >= BLK) | (n > 0))


def _attn_specs(t, aw, kv):
    prev = lambda n: jnp.maximum(n - 1, 0)
    kb, vb = aw // kv, aw // kv + 1
    return [
        pl.BlockSpec(memory_space=pltpu.SMEM),
        pl.BlockSpec((BLK, aw), lambda n: (n, 0)),
        pl.BlockSpec((BLK, kv), lambda n: (prev(n), kb)),
        pl.BlockSpec((BLK, kv), lambda n: (n, kb)),
        pl.BlockSpec((BLK, kv), lambda n: (prev(n), vb)),
        pl.BlockSpec((BLK, kv), lambda n: (n, vb)),
        pl.BlockSpec((1, BLK), lambda n: (0, 0)),
        pl.BlockSpec((1, BLK), lambda n: (0, 0)),
    ]


def _softmax_pair(s2, valid, sink0, sink1):
    out, psink = [], []
    for half, sink in ((0, sink0), (1, sink1)):
        s = jnp.where(valid, s2[:, 2 * BLK * half:2 * BLK * (half + 1)], NEG_INF)
        m = jnp.maximum(jnp.max(s, axis=-1, keepdims=True), sink)
        p = jnp.exp(s - m)
        es = jnp.exp(sink - m)
        inv = 1.0 / (jnp.sum(p, axis=-1, keepdims=True) + es)
        out.append(p * inv)
        psink.append(es * inv)
    return jnp.concatenate(out, axis=1), psink


def _attn_fwd(proj, qg, kg, sinks, aw, kv):
    t = proj.shape[0]
    n_pairs, n_kvblk = aw // BLK, kv // BLK

    def body(sink_ref, q_ref, kp_ref, kc_ref, vp_ref, vc_ref, qg_ref, kg_ref, o_ref, probs_ref, sink_p_ref):
        n = pl.program_id(0)
        ones = _seg_ones()
        lo = lax.broadcasted_iota(jnp.int32, (1, BLK), 1) < HEAD
        valid = _attn_mask(n)
        kraw = jnp.concatenate([kp_ref[...], kc_ref[...]], axis=0)
        vraw = jnp.concatenate([vp_ref[...], vc_ref[...]], axis=0)
        qn = [_head_rms(q_ref[:, BLK * p:BLK * (p + 1)], qg_ref[...], ones)[0].astype(BF16) for p in range(n_pairs)]
        k2, v2 = [], []
        for b in range(n_kvblk):
            kn = _head_rms(kraw[:, BLK * b:BLK * (b + 1)], kg_ref[...], ones)[0]
            for odd in (0, 1):
                k2.append(_expand(kn, odd, lo))
                v2.append(_expand(vraw[:, BLK * b:BLK * (b + 1)], odd, lo))
        s2 = [_nt(qn[p], k2[p // 2]) * (HEAD ** -0.5) for p in range(n_pairs)]
        soft = [_softmax_pair(s2[p], valid, sink_ref[2 * p], sink_ref[2 * p + 1]) for p in range(n_pairs)]
        lane = lax.broadcasted_iota(jnp.int32, (1, BLK), 1)
        sink_p = jnp.zeros((BLK, BLK), F32)
        for p in range(n_pairs):
            probs_ref[:, 4 * BLK * p:4 * BLK * (p + 1)] = soft[p][0]
            for half in (0, 1):
                sink_p = jnp.where(lane == 2 * p + half, soft[p][1][half], sink_p)
            o_ref[:, BLK * p:BLK * (p + 1)] = _nn(soft[p][0].astype(BF16), v2[p // 2])
        sink_p_ref[...] = sink_p

    return pl.pallas_call(
        body, name="attn_fwd", grid=(t // BLK,),
        in_specs=_attn_specs(t, aw, kv),
        out_specs=[pl.BlockSpec((BLK, aw), lambda n: (n, 0)), pl.BlockSpec((BLK, 4 * aw), lambda n: (n, 0)),
                   pl.BlockSpec((BLK, BLK), lambda n: (n, 0))],
        out_shape=[SDS((t, aw), F32), SDS((t, 4 * aw), F32), SDS((t, BLK), F32)],
        compiler_params=_params("arbitrary"),
    )(sinks, proj, proj, proj, proj, proj, qg, kg)


def _attn_bwd(proj, dy, probs, sink_p, qg, kg, sinks, aw, kv, after=None):
    t = proj.shape[0]
    nb = t // BLK
    n_kvblk = kv // BLK
    kb = aw // kv

    def body(sink_ref, q_ref, kp_ref, kc_ref, vp_ref, vc_ref, qg_ref, kg_ref, dy_ref, kall_ref, probs_ref, sink_p_ref,
             dqkv_ref, dstat_ref, dk_acc, dv_acc, dqg_acc):
        del sink_ref
        n = pl.program_id(0)
        ones = _seg_ones()
        lane = lax.broadcasted_iota(jnp.int32, (1, BLK), 1)
        lo = lane < HEAD

        @pl.when(n == 0)
        def _():
            dk_acc[...] = jnp.zeros_like(dk_acc)
            dv_acc[...] = jnp.zeros_like(dv_acc)
            dqg_acc[...] = jnp.zeros_like(dqg_acc)
            dstat_ref[...] = jnp.zeros_like(dstat_ref)

        kraw = jnp.concatenate([kp_ref[...], kc_ref[...]], axis=0)
        vraw = jnp.concatenate([vp_ref[...], vc_ref[...]], axis=0)
        row = pl.multiple_of(n * BLK, BLK)
        prow = pl.multiple_of(jnp.maximum(n - 1, 0) * BLK, BLK)
        pairs = range(aw // BLK)
        cols = [slice(BLK * p, BLK * (p + 1)) for p in pairs]
        qs = [_head_rms(q_ref[:, cols[p]], qg_ref[...], ones) for p in pairs]
        qb = [qs[p][0].astype(BF16) for p in pairs]
        k2, v2 = [], []
        for b in range(n_kvblk):
            kn = _head_rms(kraw[:, BLK * b:BLK * (b + 1)], kg_ref[...], ones)[0]
            for odd in (0, 1):
                k2.append(_expand(kn, odd, lo))
                v2.append(_expand(vraw[:, BLK * b:BLK * (b + 1)], odd, lo))
        p2 = [probs_ref[:, 4 * BLK * p:4 * BLK * (p + 1)] for p in pairs]
        dob = [dy_ref[:, cols[p]].astype(BF16) for p in pairs]
        dp2 = [_nt(dob[p], v2[p // 2]) for p in pairs]
        deltas = jnp.zeros((BLK, BLK), F32)
        ds2 = []
        for p in pairs:
            ds = []
            for half in (0, 1):
                hs = slice(2 * BLK * half, 2 * BLK * (half + 1))
                ph = p2[p][:, hs]
                delta = jnp.sum(ph * dp2[p][:, hs], axis=-1, keepdims=True)
                ds.append(ph * (dp2[p][:, hs] - delta))
                deltas = jnp.where(lane == 2 * p + half, delta, deltas)
            ds2.append((jnp.concatenate(ds, axis=1) * (HEAD ** -0.5)).astype(BF16))
        dstat_ref[2:3, :] -= jnp.sum(sink_p_ref[...] * deltas, axis=0, keepdims=True)
        dqn = [_nn(ds2[p], k2[p // 2]) for p in pairs]
        dk2 = [_tn(ds2[p], qb[p]) for p in pairs]
        dv2 = [_tn(p2[p].astype(BF16), dob[p]) for p in pairs]
        for p in pairs:
            _, qhat, rstd = qs[p]
            dqhat = dqn[p] * qg_ref[...]
            proj_q = _segsum(dqhat * qhat, ones) * (1.0 / HEAD)
            dqkv_ref[pl.ds(row, BLK), cols[p]] = (rstd * (dqhat - qhat * proj_q)).astype(BF16)
            dqg_acc[:, cols[p]] += jnp.sum(dqn[p] * qhat, axis=0, keepdims=True)
        for b in range(n_kvblk):
            dkn = jnp.zeros((2 * BLK, BLK), F32)
            dvb = jnp.zeros((2 * BLK, BLK), F32)
            for odd in (0, 1):
                j = 2 * b + odd
                dkn = dkn + _fold(dk2[2 * j] + dk2[2 * j + 1], odd, lo)
                dvb = dvb + _fold(dv2[2 * j] + dv2[2 * j + 1], odd, lo)
            kcols = slice(BLK * b, BLK * (b + 1))
            dk_acc[pl.ds(prow, BLK), kcols] += dkn[:BLK]
            dv_acc[pl.ds(prow, BLK), kcols] += dvb[:BLK]
            dk_acc[pl.ds(row, BLK), kcols] += dkn[BLK:]
            dv_acc[pl.ds(row, BLK), kcols] += dvb[BLK:]

        @pl.when(n == nb - 1)
        def _():
            dqg = dqg_acc[:, 0:BLK]
            for p in range(1, aw // BLK):
                dqg = dqg + dqg_acc[:, BLK * p:BLK * (p + 1)]
            dstat_ref[0:1, :] = dqg + pltpu.roll(dqg, HEAD, axis=1)

            def kblock(i, dkg):
                r = pl.multiple_of(i * BLK, BLK)
                for b in range(n_kvblk):
                    kcols = slice(BLK * b, BLK * (b + 1))
                    _, khat, rstd = _head_rms(kall_ref[pl.ds(r, BLK), kcols], kg_ref[...], ones)
                    dkn = dk_acc[pl.ds(r, BLK), kcols]
                    dkhat = dkn * kg_ref[...]
                    proj_k = _segsum(dkhat * khat, ones) * (1.0 / HEAD)
                    dqkv_ref[pl.ds(r, BLK), aw + BLK * b:aw + BLK * (b + 1)] = (rstd * (dkhat - khat * proj_k)).astype(BF16)
                    dqkv_ref[pl.ds(r, BLK), aw + kv + BLK * b:aw + kv + BLK * (b + 1)] = dv_acc[pl.ds(r, BLK), kcols].astype(BF16)
                    dkg = dkg + jnp.sum(dkn * khat, axis=0, keepdims=True)
                return dkg

            dkg = lax.fori_loop(0, nb, kblock, jnp.zeros((1, BLK), F32))
            dstat_ref[1:2, :] = dkg + pltpu.roll(dkg, HEAD, axis=1)

    in_specs = _attn_specs(t, aw, kv) + [
        pl.BlockSpec((BLK, aw), lambda n: (n, 0)),
        pl.BlockSpec((t, kv), lambda n: (0, kb)),
        pl.BlockSpec((BLK, 4 * aw), lambda n: (n, 0)),
        pl.BlockSpec((BLK, BLK), lambda n: (n, 0)),
    ]
    args = [sinks, proj, proj, proj, proj, proj, qg, kg, dy, proj, probs, sink_p]
    body, in_specs, args = _ordered_behind(body, len(args), in_specs, args, after)
    return pl.pallas_call(
        body, name="attn_bwd", grid=(nb,),
        in_specs=in_specs,
        out_specs=[pl.BlockSpec((t, aw + 2 * kv), lambda n: (0, 0)), pl.BlockSpec((8, BLK), lambda n: (0, 0))],
        out_shape=[SDS((t, aw + 2 * kv), BF16), SDS((8, BLK), F32)],
        scratch_shapes=[pltpu.VMEM((t, kv), F32), pltpu.VMEM((t, kv), F32), pltpu.VMEM((1, aw), F32)],
        compiler_params=_params("arbitrary"),
    )(*args)


def _conv_taps(win, w_ref, shift_of):
    rows = win.shape[0]
    acc = None
    for j in range(CONV_K):
        term = pltpu.roll(win, (rows - shift_of(j)) % rows, axis=0)[:BLK] * w_ref[j:j + 1, :]
        acc = term if acc is None else acc + term
    return acc


def _layer_norm_fwd(z):
    mu = jnp.mean(z, axis=-1, keepdims=True)
    zc = z - mu
    rstd = lax.rsqrt(jnp.mean(zc * zc, axis=-1, keepdims=True) + EPS)
    return zc * rstd, rstd


def _layer_norm_bwd(dy, yhat, rstd, g):
    dyh = dy * g
    return rstd * (dyh - jnp.mean(dyh, axis=-1, keepdims=True) - yhat * jnp.mean(dyh * yhat, axis=-1, keepdims=True))


def _conv_fill_glu(a_ref, g_ref, hpad, nb):
    hpad[0:HALO, :] = jnp.zeros((HALO, hpad.shape[1]), F32)

    def fill(i, c):
        r = pl.multiple_of(i * BLK, BLK)
        hpad[pl.ds(pl.multiple_of(r + HALO, HALO), BLK), :] = a_ref[pl.ds(r, BLK), :] * _sigmoid(g_ref[pl.ds(r, BLK), :])
        return c

    lax.fori_loop(0, nb, fill, 0)


def _conv_specs(t, cw, d_in):
    base = (d_in - 4 * cw) // cw
    vec = pl.BlockSpec((1, cw), lambda i: (0, 0))
    return [
        pl.BlockSpec((t, cw), lambda i: (0, base)),
        pl.BlockSpec((t, cw), lambda i: (0, base + 1)),
        pl.BlockSpec((HALO, cw), lambda i: (0, 0)),
        vec, vec, vec,
    ]


def _conv_fwd(proj, cw_pad, cb, lg, lb, cw):
    t, d_in = proj.shape
    nb = t // BLK

    def body(a_ref, g_ref, w_ref, b_ref, lg_ref, lb_ref, o_ref, z_ref, hpad):
        _conv_fill_glu(a_ref, g_ref, hpad, nb)

        def blk(i, c):
            r = pl.multiple_of(i * BLK, BLK)
            z = _conv_taps(hpad[pl.ds(r, BLK + HALO), :], w_ref, lambda j: j + HALO - (CONV_K - 1)) + b_ref[...]
            z_ref[pl.ds(r, BLK), :] = z
            yhat, _ = _layer_norm_fwd(z)
            y = yhat * lg_ref[...] + lb_ref[...]
            o_ref[pl.ds(r, BLK), :] = y * _sigmoid(y)
            return c

        lax.fori_loop(0, nb, blk, 0)

    whole = pl.BlockSpec((t, cw), lambda i: (0, 0))
    return pl.pallas_call(
        body, name="conv_fwd", grid=(1,),
        in_specs=_conv_specs(t, cw, d_in),
        out_specs=[whole, whole],
        out_shape=[SDS((t, cw), F32), SDS((t, cw), F32)],
        scratch_shapes=[pltpu.VMEM((t + HALO, cw), F32)],
        compiler_params=_params("arbitrary"),
    )(proj, proj, cw_pad, cb, lg, lb)


def _conv_bwd(proj, dy, z, cw_pad, cb, lg, lb, cw):
    t, d_in = proj.shape
    nb = t // BLK

    def body(a_ref, g_ref, w_ref, b_ref, lg_ref, lb_ref, dy_ref, z_ref, dc_ref, dw_ref, dvec_ref, hpad, dzpad, dwacc):
        del b_ref
        _conv_fill_glu(a_ref, g_ref, hpad, nb)
        dzpad[t:t + HALO, :] = jnp.zeros((HALO, cw), F32)
        dwacc[...] = jnp.zeros_like(dwacc)

        def blk(i, carry):
            db, dlg, dlb = carry
            r = pl.multiple_of(i * BLK, BLK)
            win = hpad[pl.ds(r, BLK + HALO), :]
            yhat, rstd = _layer_norm_fwd(z_ref[pl.ds(r, BLK), :])
            y = yhat * lg_ref[...] + lb_ref[...]
            sg = _sigmoid(y)
            dyl = dy_ref[pl.ds(r, BLK), :] * (sg * (1.0 + y * (1.0 - sg)))
            dz = _layer_norm_bwd(dyl, yhat, rstd, lg_ref[...])
            dzpad[pl.ds(r, BLK), :] = dz
            for j in range(CONV_K):
                sh = j + HALO - (CONV_K - 1)
                prod = dz * pltpu.roll(win, (BLK + HALO - sh) % (BLK + HALO), axis=0)[:BLK]
                dwacc[8 * j:8 * j + 8, :] += jnp.sum(prod.reshape(BLK // 8, 8, cw), axis=0)
            return (db + jnp.sum(dz, axis=0, keepdims=True),
                    dlg + jnp.sum(dyl * yhat, axis=0, keepdims=True),
                    dlb + jnp.sum(dyl, axis=0, keepdims=True))

        zero = jnp.zeros((1, cw), F32)
        db, dlg, dlb = lax.fori_loop(0, nb, blk, (zero, zero, zero))
        dvec_ref[...] = jnp.zeros_like(dvec_ref)
        dvec_ref[0:1, :] = db
        dvec_ref[1:2, :] = dlg
        dvec_ref[2:3, :] = dlb
        dw_ref[...] = jnp.sum(dwacc[...].reshape(HALO, 8, cw), axis=1)

        def blk2(i, c):
            r = pl.multiple_of(i * BLK, BLK)
            dh = _conv_taps(dzpad[pl.ds(r, BLK + HALO), :], w_ref, lambda j: CONV_K - 1 - j)
            a = a_ref[pl.ds(r, BLK), :]
            sg = _sigmoid(g_ref[pl.ds(r, BLK), :])
            dc_ref[pl.ds(r, BLK), 0:cw] = (dh * sg).astype(BF16)
            dc_ref[pl.ds(r, BLK), cw:2 * cw] = (dh * a * sg * (1.0 - sg)).astype(BF16)
            return c

        lax.fori_loop(0, nb, blk2, 0)

    return pl.pallas_call(
        body, name="conv_bwd", grid=(1,),
        in_specs=_conv_specs(t, cw, d_in) + [pl.BlockSpec((t, cw), lambda i: (0, 0))] * 2,
        out_specs=[pl.BlockSpec((t, 2 * cw), lambda i: (0, 0)), pl.BlockSpec((HALO, cw), lambda i: (0, 0)),
                   pl.BlockSpec((8, cw), lambda i: (0, 0))],
        out_shape=[SDS((t, 2 * cw), BF16), SDS((HALO, cw), F32), SDS((8, cw), F32)],
        scratch_shapes=[pltpu.VMEM((t + HALO, cw), F32), pltpu.VMEM((t + HALO, cw), F32), pltpu.VMEM((8 * HALO, cw), F32)],
        compiler_params=_params("arbitrary"),
    )(proj, proj, cw_pad, cb, lg, lb, dy, z)


def _tril_bf16(w):
    r = lax.broadcasted_iota(jnp.int32, (BLK, BLK), 0)
    c = lax.broadcasted_iota(jnp.int32, (BLK, BLK), 1)
    return jnp.where(r >= c, w, 0.0).astype(BF16)


def _sgu_specs(sw, d_in, heads):
    base = (d_in - 2 * sw) // sw
    vec = pl.BlockSpec((1, sw), lambda n: (0, 0))
    return [
        pl.BlockSpec((BLK, sw), lambda n: (n, base)),
        pl.BlockSpec((BLK, sw), lambda n: (n, base + 1)),
        vec, vec,
        pl.BlockSpec((heads, BLK, BLK), lambda n: (0, 0, 0)),
        pl.BlockSpec((BLK, sw), lambda n: (0, 0)),
    ]


def _sgu_mix(w_ref, vnb, heads, sw, transpose):
    head_of = lax.broadcasted_iota(jnp.int32, (1, sw), 1) // HEAD
    s = jnp.zeros((BLK, sw), F32)
    for h in range(heads):
        wt = _tril_bf16(w_ref[h])
        mixed = _tn(wt, vnb) if transpose else _nn(wt, vnb)
        s = jnp.where(head_of == h, mixed, s)
    return s


def _sgu_fwd(proj, lg, lb, w, bias_full, sw):
    t, d_in = proj.shape
    heads = sw // HEAD

    def body(u_ref, v_ref, lg_ref, lb_ref, w_ref, bias_ref, o_ref):
        vhat, _ = _layer_norm_fwd(v_ref[...])
        vn = (vhat * lg_ref[...] + lb_ref[...]).astype(BF16)
        s = _sgu_mix(w_ref, vn, heads, sw, False) + bias_ref[...]
        o_ref[...] = u_ref[...] * s

    return pl.pallas_call(
        body, name="sgu_fwd", grid=(t // BLK,),
        in_specs=_sgu_specs(sw, d_in, heads),
        out_specs=pl.BlockSpec((BLK, sw), lambda n: (n, 0)),
        out_shape=SDS((t, sw), F32),
        compiler_params=_params("arbitrary"),
    )(proj, proj, lg, lb, w, bias_full)


def _sgu_bwd(proj, dy, lg, lb, w, bias_full, sw, after=None):
    t, d_in = proj.shape
    heads = sw // HEAD
    nb = t // BLK

    def body(u_ref, v_ref, lg_ref, lb_ref, w_ref, bias_ref, dy_ref, ds_ref, dw_ref, db_ref, dvec_ref, dbfull):
        n = pl.program_id(0)

        @pl.when(n == 0)
        def _():
            dw_ref[...] = jnp.zeros_like(dw_ref)
            dvec_ref[...] = jnp.zeros_like(dvec_ref)
            dbfull[...] = jnp.zeros_like(dbfull)

        vhat, rstd = _layer_norm_fwd(v_ref[...])
        vn = (vhat * lg_ref[...] + lb_ref[...]).astype(BF16)
        s = _sgu_mix(w_ref, vn, heads, sw, False) + bias_ref[...]
        dy = dy_ref[...]
        ds_ref[:, 0:sw] = (dy * s).astype(BF16)
        dsv = dy * u_ref[...]
        dbfull[...] += dsv
        head_of = lax.broadcasted_iota(jnp.int32, (1, sw), 1) // HEAD
        r = lax.broadcasted_iota(jnp.int32, (BLK, BLK), 0)
        c = lax.broadcasted_iota(jnp.int32, (BLK, BLK), 1)
        dsb = dsv.astype(BF16)
        for h in range(heads):
            dsh = jnp.where(head_of == h, dsv, 0.0).astype(BF16)
            dw_ref[h] += jnp.where(r >= c, _nt(dsh, vn), 0.0)
        dvn = _sgu_mix(w_ref, dsb, heads, sw, True)
        dvec_ref[0:1, :] += jnp.sum(dvn * vhat, axis=0, keepdims=True)
        dvec_ref[1:2, :] += jnp.sum(dvn, axis=0, keepdims=True)
        ds_ref[:, sw:2 * sw] = _layer_norm_bwd(dvn, vhat, rstd, lg_ref[...]).astype(BF16)

        @pl.when(n == nb - 1)
        def _():
            sel = (lax.broadcasted_iota(jnp.int32, (sw, BLK), 0) // HEAD == lax.broadcasted_iota(jnp.int32, (sw, BLK), 1)).astype(BF16)
            x = dbfull[...]
            hi = x.astype(BF16)
            r1 = x - hi.astype(F32)
            mid = r1.astype(BF16)
            low = (r1 - mid.astype(F32)).astype(BF16)
            db_ref[...] = _nn(hi, sel) + _nn(mid, sel) + _nn(low, sel)

    in_specs = _sgu_specs(sw, d_in, heads) + [pl.BlockSpec((BLK, sw), lambda n: (n, 0))]
    body, in_specs, args = _ordered_behind(body, 7, in_specs, [proj, proj, lg, lb, w, bias_full, dy], after)
    return pl.pallas_call(
        body, name="sgu_bwd", grid=(nb,),
        in_specs=in_specs,
        out_specs=[pl.BlockSpec((BLK, 2 * sw), lambda n: (n, 0)), pl.BlockSpec((heads, BLK, BLK), lambda n: (0, 0, 0)),
                   pl.BlockSpec((BLK, BLK), lambda n: (0, 0)), pl.BlockSpec((8, sw), lambda n: (0, 0))],
        out_shape=[SDS((t, 2 * sw), BF16), SDS((heads, BLK, BLK), F32), SDS((BLK, BLK), F32), SDS((8, sw), F32)],
        scratch_shapes=[pltpu.VMEM((BLK, sw), F32)],
        compiler_params=_params("arbitrary"),
    )(*args)


def _rms_fwd(x, g):
    return (x * lax.rsqrt(jnp.mean(x * x, axis=-1, keepdims=True) + EPS)) * g


def _rms_bwd(dh, x, g):
    rstd = lax.rsqrt(jnp.mean(x * x, axis=-1, keepdims=True) + EPS)
    xhat = x * rstd
    dgx = dh * g
    dx = rstd * (dgx - xhat * jnp.mean(dgx * xhat, axis=-1, keepdims=True))
    return dx, jnp.sum(dh * xhat, axis=0, keepdims=True)


def _rms_matmul(x, g, w, tm, tn, relu2, name, transposed=False):
    t, d = x.shape
    if w.ndim == 3:
        assert w.shape[2] == tn
        n = w.shape[0] * tn
        w_spec = pl.BlockSpec((None, d, tn), lambda i, j: (j, 0, 0))
    elif transposed:
        n = w.shape[0]
        w_spec = pl.BlockSpec((tn, d), lambda i, j: (j, 0))
    else:
        n = w.shape[1]
        w_spec = pl.BlockSpec((d, tn), lambda i, j: (0, j))

    def body(x_ref, g_ref, w_ref, h_ref, *outs):
        @pl.when(pl.program_id(1) == 0)
        def _():
            h_ref[...] = _rms_fwd(x_ref[...], g_ref[...]).astype(BF16)

        acc = _nt(h_ref[...], w_ref[...]) if transposed else _nn(h_ref[...], w_ref[...])
        if relu2:
            r = jnp.maximum(acc, 0.0)
            outs[0][...] = (r * r).astype(BF16)
            outs[1][...] = r.astype(BF16)
        else:
            outs[0][...] = acc

    tile = pl.BlockSpec((tm, tn), lambda i, j: (i, j))
    row = pl.BlockSpec((tm, d), lambda i, j: (i, 0))
    outs = [SDS((t, n), BF16), SDS((t, n), BF16)] if relu2 else [SDS((t, n), F32)]
    return pl.pallas_call(
        body, name=name, grid=(t // tm, n // tn),
        in_specs=[row, pl.BlockSpec((1, d), lambda i, j: (0, 0)), w_spec],
        out_specs=[row] + [tile] * len(outs),
        out_shape=[SDS((t, d), BF16)] + outs,
        compiler_params=_params("parallel", "arbitrary"),
    )(x, g, w)


def _group_rms_matmul(ys, g, w, res, tm, tn):
    t, d = res.shape
    widths = [y.shape[1] for y in ys]
    k = sum(widths)

    def body(*refs):
        y_refs, (g_ref, w_ref, res_ref, mix_ref, o_ref) = refs[:len(ys)], refs[len(ys):]

        @pl.when(pl.program_id(1) == 0)
        def _():
            c = 0
            for y_ref, wd in zip(y_refs, widths):
                mix_ref[:, c:c + wd] = _rms_fwd(y_ref[...], g_ref[:, c:c + wd]).astype(BF16)
                c += wd

        o_ref[...] = res_ref[...] + _nn(mix_ref[...], w_ref[...])

    tile = pl.BlockSpec((tm, tn), lambda i, j: (i, j))
    return pl.pallas_call(
        body, name="mix_out", grid=(t // tm, d // tn),
        in_specs=[pl.BlockSpec((tm, wd), lambda i, j: (i, 0)) for wd in widths] + [
            pl.BlockSpec((1, k), lambda i, j: (0, 0)), pl.BlockSpec((k, tn), lambda i, j: (0, j)), tile],
        out_specs=[pl.BlockSpec((tm, k), lambda i, j: (i, 0)), tile],
        out_shape=[SDS((t, k), BF16), SDS((t, d), F32)],
        compiler_params=_params("parallel", "arbitrary"),
    )(*ys, g, w, res)


def _matmul_res(a, w, res, tm, tn, tk):
    t, k = a.shape
    n = w.shape[1]
    nk = k // tk

    def body(a_ref, w_ref, res_ref, o_ref, acc):
        kk = pl.program_id(2)

        @pl.when(kk == 0)
        def _():
            acc[...] = res_ref[...]

        acc[...] += _nn(a_ref[...], w_ref[...])

        @pl.when(kk == nk - 1)
        def _():
            o_ref[...] = acc[...]

    tile = pl.BlockSpec((tm, tn), lambda i, j, kk: (i, j))
    return pl.pallas_call(
        body, name="mlp_down", grid=(t // tm, n // tn, nk),
        in_specs=[pl.BlockSpec((tm, tk), lambda i, j, kk: (i, kk)), pl.BlockSpec((tk, tn), lambda i, j, kk: (kk, j)), tile],
        out_specs=tile,
        out_shape=SDS((t, n), F32),
        scratch_shapes=[pltpu.VMEM((tm, tn), F32)],
        compiler_params=_params("parallel", "parallel", "arbitrary"),
    )(a, w, res)


def _loss_grad(y, target, tm):
    t, d = y.shape

    def body(y_ref, t_ref, dy_ref, dyb_ref, l_ref):
        @pl.when(pl.program_id(0) == 0)
        def _():
            l_ref[...] = jnp.zeros_like(l_ref)

        err = y_ref[...] - t_ref[...]
        dy = err * (1.0 / d)
        dy_ref[...] = dy
        dyb_ref[...] = dy.astype(BF16)
        per_row = jnp.mean(err * err, axis=-1, keepdims=True)
        l_ref[...] += jnp.sum(per_row, axis=0, keepdims=True) * 0.5

    row = pl.BlockSpec((tm, d), lambda i: (i, 0))
    return pl.pallas_call(
        body, name="loss_grad", grid=(t // tm,),
        in_specs=[row, row], out_specs=[row, row, pl.BlockSpec((8, BLK), lambda i: (0, 0))],
        out_shape=[SDS((t, d), F32), SDS((t, d), BF16), SDS((8, BLK), F32)],
        compiler_params=_params("arbitrary"),
    )(y, target)


def _mlp_dact(dxb, w_down, r, tn, after=None):
    t, d = dxb.shape
    f = w_down.shape[0]

    def body(dxb_ref, w_ref, r_ref, dup_ref):
        dup_ref[...] = (_nt(dxb_ref[...], w_ref[...]) * (2.0 * r_ref[...].astype(F32))).astype(BF16)

    tile = pl.BlockSpec((t, tn), lambda j: (0, j))
    body, in_specs, args = _ordered_behind(
        body, 3, [pl.BlockSpec((t, d), lambda j: (0, 0)), pl.BlockSpec((tn, d), lambda j: (j, 0)), tile],
        [dxb, w_down, r], after)
    return pl.pallas_call(
        body, name="mlp_dact", grid=(f // tn,),
        in_specs=in_specs, out_specs=tile,
        out_shape=SDS((t, f), BF16),
        compiler_params=_params("arbitrary"),
    )(*args)


def _grad_w(a, b, tm, tn, name, col_blocks=False, after=None, rows=None, into=None):
    t, m = a.shape
    n = b.shape[1]
    assert n % tn == 0 and m % tm == 0
    total, first = rows if rows else (m, 0)
    assert first % tm == 0
    first_tile = first // tm

    def body(a_ref, b_ref, *rest):
        rest[-1][...] = _tn(a_ref[...], b_ref[...]).astype(BF16)

    if col_blocks:
        out_spec = pl.BlockSpec((None, tm, tn), lambda i, j: (j, i, 0))
        out_shape = SDS((n // tn, m, tn), BF16)
    else:
        out_spec = pl.BlockSpec((tm, tn), lambda i, j: (first_tile + i, j))
        out_shape = SDS((total, n), BF16)
    in_specs = [pl.BlockSpec((t, tm), lambda i, j: (0, i)), pl.BlockSpec((t, tn), lambda i, j: (0, j))]
    args = [a, b]
    aliases = {}
    if into is not None:
        in_specs, args, aliases = in_specs + [ANY], args + [into], {2: 0}
    body, in_specs, args = _ordered_behind(body, len(args), in_specs, args, after)
    return pl.pallas_call(
        body, name=name, grid=(m // tm, n // tn),
        in_specs=in_specs, out_specs=out_spec, out_shape=out_shape, input_output_aliases=aliases,
        compiler_params=_params("parallel", "arbitrary"),
    )(*args)


def _grad_w_half(a, b, tm, tn, name, by_cols, sel, add=None, after=None):
    t, m = a.shape
    n = b.shape[1]
    if by_cols:
        per = n // N_DEV // tn
        grid, shape = (4, m // tm, per), (4, m, n // N_DEV)
        a_spec = pl.BlockSpec((t, tm), lambda q, i, j, s: (0, i))
        b_spec = pl.BlockSpec((t, tn), lambda q, i, j, s: (0, (2 * q + s[0]) * per + j))
    else:
        per = m // N_DEV // tm
        grid, shape = (4, per, n // tn), (4, m // N_DEV, n)
        a_spec = pl.BlockSpec((t, tm), lambda q, i, j, s: (0, (2 * q + s[0]) * per + i))
        b_spec = pl.BlockSpec((t, tn), lambda q, i, j, s: (0, j))
    assert per >= 1
    tile = pl.BlockSpec((None, tm, tn), lambda q, i, j, s: (q, i, j))

    def body(sel_ref, a_ref, b_ref, *rest):
        acc = _tn(a_ref[...], b_ref[...])
        if add is not None:
            acc = acc + rest[0][...].astype(F32)
        rest[-1][...] = acc.astype(BF16)

    in_specs, args = [a_spec, b_spec], [a, b]
    if add is not None:
        in_specs, args = in_specs + [tile], args + [add]
    if after is not None:
        in_specs, args = in_specs + [ANY], args + [after]
    return pl.pallas_call(
        body, name=name,
        grid_spec=pltpu.PrefetchScalarGridSpec(num_scalar_prefetch=1, grid=grid, in_specs=in_specs, out_specs=tile),
        out_shape=SDS(shape, BF16),
        compiler_params=_params("arbitrary", "arbitrary", "arbitrary"),
    )(sel, *args)


def _mlp_dnorm(dup, w_up, x, g, dres, tm, after=None):
    t, f = dup.shape
    d = x.shape[1]
    nk, _, tk = w_up.shape

    def body(a_ref, w_ref, x_ref, g_ref, dres_ref, dx_ref, dg_ref, acc):
        i, kk = pl.program_id(0), pl.program_id(1)

        @pl.when((i == 0) & (kk == 0))
        def _():
            dg_ref[...] = jnp.zeros_like(dg_ref)

        @pl.when(kk == 0)
        def _():
            acc[...] = jnp.zeros_like(acc)

        acc[...] += _nt(a_ref[...], w_ref[...])

        @pl.when(kk == nk - 1)
        def _():
            dx, dg = _rms_bwd(acc[...], x_ref[...], g_ref[...])
            dx_ref[...] = dres_ref[...] + dx
            dg_ref[0:1, :] += dg

    row = pl.BlockSpec((tm, d), lambda i, kk: (i, 0))
    in_specs = [pl.BlockSpec((tm, tk), lambda i, kk: (i, kk)), pl.BlockSpec((None, d, tk), lambda i, kk: (kk, 0, 0)),
                row, pl.BlockSpec((1, d), lambda i, kk: (0, 0)), row]
    body, in_specs, args = _ordered_behind(body, 5, in_specs, [dup, w_up, x, g, dres], after)
    return pl.pallas_call(
        body, name="mlp_dnorm", grid=(t // tm, nk),
        in_specs=in_specs,
        out_specs=[row, pl.BlockSpec((8, d), lambda i, kk: (0, 0))],
        out_shape=[SDS((t, d), F32), SDS((8, d), F32)],
        scratch_shapes=[pltpu.VMEM((tm, d), F32)],
        compiler_params=_params("arbitrary", "arbitrary"),
    )(*args)


def _mix_dnorm(dx, w_out, ys, g, tm, after=None):
    t, d = dx.shape
    k = w_out.shape[0]
    widths = [y.shape[1] for y in ys]

    def body(dx_ref, w_ref, *refs):
        y_refs = refs[:len(ys)]
        g_ref, dxb_ref = refs[len(ys)], refs[len(ys) + 1]
        dy_refs = refs[len(ys) + 2:2 * len(ys) + 2]
        dg_ref = refs[-1]

        @pl.when(pl.program_id(0) == 0)
        def _():
            dg_ref[...] = jnp.zeros_like(dg_ref)

        dxb = dx_ref[...].astype(BF16)
        dxb_ref[...] = dxb
        dmix = _nt(dxb, w_ref[...])
        c = 0
        for y_ref, dy_ref, wd in zip(y_refs, dy_refs, widths):
            dy, dg = _rms_bwd(dmix[:, c:c + wd], y_ref[...], g_ref[:, c:c + wd])
            dy_ref[...] = dy
            dg_ref[0:1, c:c + wd] += dg
            c += wd

    row = pl.BlockSpec((tm, d), lambda i: (i, 0))
    yspecs = [pl.BlockSpec((tm, wd), lambda i: (i, 0)) for wd in widths]
    in_specs = [row, pl.BlockSpec((k, d), lambda i: (0, 0))] + yspecs + [pl.BlockSpec((1, k), lambda i: (0, 0))]
    body, in_specs, args = _ordered_behind(body, len(in_specs), in_specs, [dx, w_out, *ys, g], after)
    return pl.pallas_call(
        body, name="mix_dnorm", grid=(t // tm,),
        in_specs=in_specs,
        out_specs=[row] + yspecs + [pl.BlockSpec((8, k), lambda i: (0, 0))],
        out_shape=[SDS((t, d), BF16)] + [SDS((t, wd), F32) for wd in widths] + [SDS((8, k), F32)],
        compiler_params=_params("arbitrary"),
    )(*args)


def _in_dnorm(dps, w_in_t, x, g, dres, tm):
    t, d = x.shape
    widths = [p.shape[1] for p in dps]
    offs = [sum(widths[:p]) for p in range(len(dps))]
    n_in = w_in_t.shape[0]

    def body(*refs):
        p_refs = refs[:len(dps)]
        w_ref, x_ref, g_ref, dres_ref, dx_ref, dxb_ref, dg_ref = refs[len(dps):]

        @pl.when(pl.program_id(0) == 0)
        def _():
            dg_ref[...] = jnp.zeros_like(dg_ref)

        acc = None
        for p_ref, off, wd in zip(p_refs, offs, widths):
            term = _nn(p_ref[...], w_ref[off:off + wd, :])
            acc = term if acc is None else acc + term
        dx, dg = _rms_bwd(acc, x_ref[...], g_ref[...])
        dx = dres_ref[...] + dx
        dx_ref[...] = dx
        dxb_ref[...] = dx.astype(BF16)
        dg_ref[0:1, :] += dg

    row = pl.BlockSpec((tm, d), lambda i: (i, 0))
    return pl.pallas_call(
        body, name="in_dnorm", grid=(t // tm,),
        in_specs=[pl.BlockSpec((tm, wd), lambda i: (i, 0)) for wd in widths] + [
            pl.BlockSpec((n_in, d), lambda i: (0, 0)), row, pl.BlockSpec((1, d), lambda i: (0, 0)), row],
        out_specs=[row, row, pl.BlockSpec((8, d), lambda i: (0, 0))],
        out_shape=[SDS((t, d), F32), SDS((t, d), BF16), SDS((8, d), F32)],
        compiler_params=_params("arbitrary"),
    )(*dps, w_in_t, x, g, dres)


def _tile(n, want):
    return min(n, want)


def _row_tile(n, want):
    return max(k for k in range(8, min(n, want) + 1, 8) if n % k == 0)


def _layer_fwd(x, p, fetch, after):
    t, d = x.shape
    aw, kv, cw, sw = d // 2, d // 8, d // 4, d // 4
    tm = _tile(t, 1024)
    w_in = fetch("w_in", after)
    h1, proj = _rms_matmul(x, p["ln1_g"], w_in, tm, 512 if w_in.shape[0] % 512 == 0 else 256, False, "in_proj",
                           transposed=True)
    y_attn, probs, sink_p = _attn_fwd(proj, p["qg"], p["kg"], p["sinks"], aw, kv)
    y_conv, z_conv = _conv_fwd(proj, p["conv_w"], p["conv_b"], p["conv_ln_g"], p["conv_ln_b"], cw)
    y_sgu = _sgu_fwd(proj, p["sgu_ln_g"], p["sgu_ln_b"], p["sgu_w"], p["sgu_bias"], sw)
    ys = [y_attn, y_conv, y_sgu]
    w_out = fetch("w_out", y_sgu)
    mix, x1 = _group_rms_matmul(ys, p["out_norm_g"], w_out, x, tm, _tile(d, 1024))
    w_up = fetch("w_up", x1)
    h2, act, r = _rms_matmul(x1, p["ln2_g"], w_up, tm, w_up.shape[2], True, "mlp_up")
    w_down = fetch("w_down", act)
    x2 = _matmul_res(act, w_down, x1, tm, _tile(d, 2048), 512)
    saved = dict(x=x, h1=h1, proj=proj, ys=ys, mix=mix, x1=x1, h2=h2, act=act, r=r,
                 w_in=w_in, w_out=w_out, w_up=w_up, w_down=w_down, probs=probs, sink_p=sink_p, z_conv=z_conv)
    return x2, saved


def _layer_bwd(dx2, dx2b, p, s, start, finish, share, carry):
    t, d = dx2.shape
    aw, kv, cw, sw = d // 2, d // 8, d // 4, d // 4
    tm = _tile(t, 512)
    dup = _mlp_dact(dx2b, s["w_down"], s["r"], 512, carry[0] if carry else None)
    tok = carry[1](dup) if carry else None
    half = functools.partial
    f = s["act"].shape[1]
    tok = start("w_down", half(_grad_w_half, s["act"], dx2b, _tile(f // N_DEV, 512), _tile(d, 2048), "grad_w_down", False), tok)
    dx1, d_ln2 = _mlp_dnorm(dup, s["w_up"], s["x1"], p["ln2_g"], dx2, tm, tok)
    tok = finish("w_down", dx1)
    tok = start("w_up", half(_grad_w_half, s["h2"], dup, _tile(d, 1024), s["w_up"].shape[2], "grad_w_up", True), tok)
    dx1b, dya, dyc, dys, d_onorm = _mix_dnorm(dx1, s["w_out"], s["ys"], p["out_norm_g"], _tile(t, 256), tok)
    tok = finish("w_up", dx1b)
    dsgu, d_sw, d_sb, d_svec = _sgu_bwd(s["proj"], dys, p["sgu_ln_g"], p["sgu_ln_b"], p["sgu_w"], p["sgu_bias"], sw, tok)
    share("sgu_w", d_sw)
    tok = start("w_out", half(_grad_w_half, s["mix"], dx1b, _tile(d // N_DEV, 512), _tile(d, 2048), "grad_w_out", False), dsgu)
    dqkv, d_attn = _attn_bwd(s["proj"], dya, s["probs"], s["sink_p"], p["qg"], p["kg"], p["sinks"], aw, kv, tok)
    tok = finish("w_out", dqkv)
    dconv, d_cw, d_cvec = _conv_bwd(s["proj"], dyc, s["z_conv"], p["conv_w"], p["conv_b"], p["conv_ln_g"], p["conv_ln_b"], cw)
    dps = [dqkv, dconv, dsgu]
    dx, dxb, d_ln1 = _in_dnorm(dps, s["w_in"], s["x"], p["ln1_g"], dx1, _tile(t, 256))
    heads = sw // HEAD
    share("rest", dict(
        ln1_g=d_ln1[0], q_norm_g=d_attn[0, :HEAD], k_norm_g=d_attn[1, :HEAD], sinks=d_attn[2, :aw // HEAD],
        conv_w=d_cw[:CONV_K], conv_b=d_cvec[0], conv_ln_g=d_cvec[1], conv_ln_b=d_cvec[2],
        sgu_ln_g=d_svec[0], sgu_ln_b=d_svec[1], sgu_b=d_sb[:, :heads].T,
        out_norm_g=d_onorm[0], ln2_g=d_ln2[0]))
    d_in, first, g_in = sum(dp.shape[1] for dp in dps), 0, None
    tm_in = 512 if all(dp.shape[1] % 512 == 0 for dp in dps) else 256
    for k, dp in enumerate(dps):
        g_in = _grad_w(dp, s["h1"], tm_in, _tile(d, 2048), f"grad_w_in_{k}", after=tok if k == 0 else None,
                       rows=(d_in, first), into=g_in)
        first += dp.shape[1]
    tok = start("w_in", g_in, None)
    carry = (tok, functools.partial(finish, "w_in"))
    return dx, dxb, carry


def _layer_params(l, small):
    row = lambda v: v[l][None, :]
    two = lambda v: jnp.tile(v[l], 2)[None, :]
    return dict(
        ln1_g=row(small["ln1_g"]), ln2_g=row(small["ln2_g"]), out_norm_g=row(small["out_norm_g"]),
        qg=two(small["q_norm_g"]), kg=two(small["k_norm_g"]), sinks=small["sinks"][l],
        conv_w=jnp.pad(small["conv_w"][l], ((0, HALO - CONV_K), (0, 0))),
        conv_b=row(small["conv_b"]), conv_ln_g=row(small["conv_ln_g"]), conv_ln_b=row(small["conv_ln_b"]),
        sgu_ln_g=row(small["sgu_ln_g"]), sgu_ln_b=row(small["sgu_ln_b"]), sgu_w=small["sgu_w"][l],
        sgu_bias=jnp.repeat(small["sgu_b"][l].T, HEAD, axis=1),
    )


def _local_step(x, target, small, depth, fetch, start, finish, share, after):
    params = [_layer_params(l, small) for l in range(depth)]
    saved = []
    h = x
    for l in range(depth):
        h, s = _layer_fwd(h, params[l], functools.partial(fetch, l), after)
        after = h
        saved.append(s)
    dy, dyb, lsum = _loss_grad(h, target, _tile(x.shape[0], 512))
    def share_with_loss(kind, grads):
        share(depth - 1, kind, dict(grads, loss=lsum[0, 0:1]) if kind == "rest" else grads)

    carry = None
    for l in reversed(range(depth)):
        dy, dyb, carry = _layer_bwd(dy, dyb, params[l], saved[l], functools.partial(start, l), functools.partial(finish, l),
                                    share_with_loss if l == depth - 1 else functools.partial(share, l), carry)
    return dy, carry


BIG = ("w_in", "w_out", "w_up", "w_down")
HBM = pl.BlockSpec(memory_space=pltpu.HBM)
SEMS = pl.BlockSpec(memory_space=pltpu.SEMAPHORE)
EFFECT = pltpu.SideEffectType.DATAFLOW_SIDE_EFFECTING


def _mesh_pos():
    return lax.axis_index("x"), lax.axis_index("y"), lax.axis_index("c")


def _other_chips(x, y):
    return [(1 - x, y), (x, 1 - y), (1 - x, 1 - y)]


def _copies(plan, refs, sends, recvs):
    x, y, c = _mesh_pos()
    return [pltpu.make_async_remote_copy(src_ref=src, dst_ref=dst, send_sem=sends.at[k], recv_sem=recvs.at[k],
                                         device_id=dev, device_id_type=MESH)
            for k, (src, dst, dev) in enumerate(plan(x, y, c, refs))]


def _gather_plan(x, y, c, refs):
    mine = refs[0].at[4 * x + 2 * y + c]
    return [(mine, mine, (x, y, 1 - c))] + [(mine, mine, (*chip, c)) for chip in _other_chips(x, y)]


def _all_plan(x, y, c, refs):
    mine = refs[0].at[4 * x + 2 * y + c]
    return [(mine, mine, (x ^ (k >> 2), y ^ ((k >> 1) & 1), c ^ (k & 1))) for k in range(1, N_DEV)]


def _pair_plan(x, y, c, refs):
    blocks, land = refs
    return [(blocks.at[2 * q + (1 - c)], land.at[q], (x, y, 1 - c)) for q in range(4)]


def _half_plan(x, y, c, refs):
    blocks, land = refs
    return [(blocks.at[q], land.at[q], (x, y, 1 - c)) for q in range(4)]


def _chips_plan(x, y, c, refs):
    sums, land = refs
    return [(sums.at[2 * chip[0] + chip[1]], land.at[k], (*chip, c)) for k, chip in enumerate(_other_chips(x, y))]


def _start_exchanges(name, groups, after=None):
    flat = [a for arrays, _, _ in groups for a in arrays]
    n_arr, n_g = len(flat), len(groups)
    n_in = n_arr + (after is not None)

    def body(*refs):
        ins, sems, token = refs[:n_arr], refs[n_in:n_in + 2 * n_g], refs[-1]
        off = 0
        for gi, (arrays, plan, _) in enumerate(groups):
            for cp in _copies(plan, ins[off:off + len(arrays)], sems[2 * gi], sems[2 * gi + 1]):
                cp.start()
            off += len(arrays)
        token[...] = jnp.zeros_like(token)

    res = pl.pallas_call(
        body, name=name,
        out_shape=[pltpu.SemaphoreType.DMA((n,)) for _, _, n in groups for _ in (0, 1)]
        + [pltpu.HBM(a.shape, a.dtype) for a in flat] + [SDS((8, BLK), F32)],
        in_specs=[HBM] * n_arr + [ANY] * (after is not None),
        out_specs=[SEMS] * (2 * n_g) + [HBM] * n_arr + [pl.BlockSpec(memory_space=pltpu.VMEM)],
        input_output_aliases={i: 2 * n_g + i for i in range(n_arr)},
        compiler_params=pltpu.CompilerParams(has_side_effects=EFFECT),
    )(*[pltpu.with_memory_space_constraint(a, pltpu.HBM) for a in flat], *([after] if after is not None else []))
    sems, thru, token = res[:2 * n_g], res[2 * n_g:2 * n_g + n_arr], res[-1]
    out, off = [], 0
    for gi, (arrays, _, _) in enumerate(groups):
        out.append((list(thru[off:off + len(arrays)]), sems[2 * gi], sems[2 * gi + 1]))
        off += len(arrays)
    return out, token


def _wait_exchange(name, arrays, sends, recvs, plan, after):
    n = len(arrays)

    def body(*refs):
        for cp in _copies(plan, refs[:n], refs[n], refs[n + 1]):
            cp.wait_send()
            cp.wait_recv()

    return pl.pallas_call(
        body, name=name,
        out_shape=[pltpu.HBM(a.shape, a.dtype) for a in arrays],
        in_specs=[HBM] * n + [SEMS, SEMS, ANY],
        out_specs=[HBM] * n,
        input_output_aliases={i: i for i in range(n)},
        compiler_params=pltpu.CompilerParams(has_side_effects=EFFECT),
    )(*arrays, sends, recvs, after)


def _gather_finish(land, name):
    def body(land_ref, out_ref, send_sems, recv_sems):
        del land_ref
        x, y, c = _mesh_pos()
        cps = []
        for k, chip in enumerate(_other_chips(x, y)):
            block = out_ref.at[4 * chip[0] + 2 * chip[1] + c]
            cps.append(pltpu.make_async_remote_copy(
                src_ref=block, dst_ref=block, send_sem=send_sems.at[k], recv_sem=recv_sems.at[k],
                device_id=(x, y, 1 - c), device_id_type=MESH))
        for cp in cps:
            cp.start()
        for cp in cps:
            cp.wait()

    return pl.pallas_call(
        body, name=name,
        in_specs=[ANY], out_specs=ANY,
        out_shape=SDS(land.shape, land.dtype),
        input_output_aliases={0: 0},
        scratch_shapes=[pltpu.SemaphoreType.DMA((3,)), pltpu.SemaphoreType.DMA((3,))],
    )(land)


def _all_gather(shards, name):
    n = len(shards)

    def body(*refs):
        ins, outs = refs[:n], refs[n:2 * n]
        send_sems, recv_sems, local_sems = refs[2 * n:]
        x, y, c = _mesh_pos()
        me, sibling = (x, y, c), (x, y, 1 - c)
        chips = _other_chips(x, y)

        def copy(a, k, block, to, src=None):
            dst = outs[a].at[4 * block[0] + 2 * block[1] + block[2]]
            return pltpu.make_async_remote_copy(
                src_ref=dst if src is None else src, dst_ref=dst,
                send_sem=send_sems.at[a, k], recv_sem=recv_sems.at[a, k], device_id=to, device_id_type=MESH)

        mine = [pltpu.make_async_copy(ins[a], outs[a].at[4 * x + 2 * y + c], local_sems.at[a]) for a in range(n)]
        for cp in mine:
            cp.start()
        first = []
        for a in range(n):
            first.append(copy(a, 0, me, sibling, src=ins[a]))
            first += [copy(a, 1 + j, me, (*chip, c), src=ins[a]) for j, chip in enumerate(chips)]
        for cp in first:
            cp.start()
        passed = []
        for j, chip in enumerate(chips):
            for a in range(n):
                copy(a, 1 + j, (*chip, c), me).wait_recv()
                fwd = copy(a, 4 + j, (*chip, c), sibling)
                fwd.start()
                passed.append(fwd)
        for a in range(n):
            copy(a, 0, sibling, me).wait_recv()
            for j, chip in enumerate(chips):
                copy(a, 4 + j, (*chip, 1 - c), me).wait_recv()
        for cp in first + passed:
            cp.wait_send()
        for cp in mine:
            cp.wait()

    outs = pl.pallas_call(
        body, name=name,
        in_specs=[ANY] * n, out_specs=[ANY] * n,
        out_shape=[SDS((N_DEV,) + s.shape, s.dtype) for s in shards],
        scratch_shapes=[pltpu.SemaphoreType.DMA((n, 7)), pltpu.SemaphoreType.DMA((n, 7)), pltpu.SemaphoreType.DMA((n,))],
    )(*shards)
    return list(outs)


def _pair_add(own, got, c, name):
    _, r, cols = own.shape
    tr = _row_tile(r, 512)

    def body(c_ref, own_ref, got_ref, o_ref):
        o_ref[...] = (own_ref[...].astype(F32) + got_ref[...].astype(F32)).astype(BF16)

    return pl.pallas_call(
        body, name=name,
        grid_spec=pltpu.PrefetchScalarGridSpec(
            num_scalar_prefetch=1, grid=(4, r // tr),
            in_specs=[pl.BlockSpec((None, tr, cols), lambda q, i, c_ref: (2 * q + c_ref[0], i, 0)),
                      pl.BlockSpec((None, tr, cols), lambda q, i, c_ref: (q, i, 0))],
            out_specs=pl.BlockSpec((None, tr, cols), lambda q, i, c_ref: (q, i, 0))),
        out_shape=SDS((4, r, cols), BF16),
        compiler_params=_params("arbitrary", "arbitrary"),
    )(c, own, got)


def _adamw(w, g, m, v):
    m = ADAM_B1 * m + (1.0 - ADAM_B1) * g
    v = ADAM_B2 * v + (1.0 - ADAM_B2) * (g * g)
    m_hat = m / (1.0 - ADAM_B1 ** ADAM_STEP)
    v_hat = v / (1.0 - ADAM_B2 ** ADAM_STEP)
    delta = -ADAM_LR * (m_hat / (jnp.sqrt(v_hat) + ADAM_EPS) + ADAM_WD * w)
    return delta, m, v


def _adamw_layer(chip_sum, got, chip, w, m, v, layer, prev, name):
    depth, r, cols = w.shape
    tr = _row_tile(r, 256)

    def body(chip_ref, sum_ref, got_ref, w_ref, m_ref, v_ref, *rest):
        g_out, d_out, m_out, v_out, token = rest[-5:]
        g = sum_ref[...].astype(F32) + got_ref[0].astype(F32) + got_ref[1].astype(F32) + got_ref[2].astype(F32)
        delta, mm, vv = _adamw(w_ref[...], g, m_ref[...], v_ref[...])
        g_out[...] = g
        d_out[...] = delta
        m_out[...] = mm
        v_out[...] = vv
        token[...] = jnp.zeros_like(token)

    shard = pl.BlockSpec((None, tr, cols), lambda i, chip_ref: (layer, i, 0))
    in_specs = [pl.BlockSpec((None, tr, cols), lambda i, chip_ref: (chip_ref[0], i, 0)),
                pl.BlockSpec((3, tr, cols), lambda i, chip_ref: (0, i, 0)), shard, shard, shard]
    args = [chip, chip_sum, got, w, m, v]
    aliases = {}
    if prev is not None:
        in_specs += [ANY] * 4
        aliases = {len(args) + k: k for k in range(4)}
        args += list(prev)
    res = pl.pallas_call(
        body, name=name,
        grid_spec=pltpu.PrefetchScalarGridSpec(
            num_scalar_prefetch=1, grid=(r // tr,), in_specs=in_specs,
            out_specs=[shard] * 4 + [pl.BlockSpec((8, BLK), lambda i, chip_ref: (0, 0))]),
        out_shape=[SDS(w.shape, F32)] * 4 + [SDS((8, BLK), F32)],
        input_output_aliases=aliases,
        compiler_params=_params("arbitrary"),
    )(*args)
    return list(res[:4]), res[4]


WIDE = ("ln1_g", "out_norm_g", "ln2_g", "conv_b", "conv_ln_g", "conv_ln_b", "sgu_ln_g", "sgu_ln_b")
NARROW = ("q_norm_g", "k_norm_g", "sinks")


def _small_rows(w):
    rows, r = {}, 0
    for n in WIDE:
        rows[n] = (r, w[n].shape[1] // BLK)
        r += rows[n][1]
    for n in NARROW + ("loss",):
        rows[n] = (r, 1)
        r += 1
    r = -(-r // 8) * 8
    rows["sgu_b"] = (r, w["sgu_b"].shape[1])
    r += -(-rows["sgu_b"][1] // 8) * 8
    rows["conv_w"] = (r, N_DEV * HALO)
    return rows, r + N_DEV * HALO


def _pack_small(small, rows, total):
    parts, r = [], 0

    def put(name, block):
        nonlocal r
        first = rows[name][0]
        if first > r:
            parts.append(jnp.zeros((first - r, BLK), F32))
        parts.append(block)
        r = first + block.shape[0]

    for n in WIDE:
        put(n, small[n].reshape(-1, BLK))
    for n in NARROW:
        put(n, jnp.pad(small[n], (0, BLK - small[n].shape[0]))[None])
    if "loss" in small:
        put("loss", jnp.pad(small["loss"], (0, BLK - 1))[None])
    put("sgu_b", small["sgu_b"])
    cw = small["conv_w"]
    per_dev = cw.shape[1] // N_DEV
    blocks = jnp.transpose(cw.reshape(CONV_K, N_DEV, per_dev), (1, 0, 2))
    put("conv_w", jnp.pad(blocks, ((0, 0), (0, HALO - CONV_K), (0, BLK - per_dev))).reshape(N_DEV * HALO, BLK))
    if total > r:
        parts.append(jnp.zeros((total - r, BLK), F32))
    return jnp.concatenate(parts)


def _small_update(me, packed, sgu_w_all, w, m, v, rows):
    depth = len(packed)
    total = packed[0].shape[1]
    names = SMALL + ("conv_w",)
    heads = w["sgu_w"].shape[1]
    per_dev = w["conv_w"].shape[2]

    def body(me_ref, *refs):
        packed_refs, sgu_refs = refs[:depth], refs[depth:2 * depth]
        ins = refs[2 * depth:2 * depth + 3 * len(names)]
        outs = refs[2 * depth + 3 * len(names):2 * depth + 7 * len(names)]
        loss_out, acc, conv = refs[-3:]
        io = {n: (ins[3 * i:3 * i + 3], outs[4 * i:4 * i + 4]) for i, n in enumerate(names)}

        def update(n):
            (w_ref, m_ref, v_ref), (g_out, d_out, m_out, v_out) = io[n]
            delta, mm, vv = _adamw(w_ref[...], g_out[...], m_ref[...], v_ref[...])
            d_out[...] = delta
            m_out[...] = mm
            v_out[...] = vv

        mine = pl.ds(pl.multiple_of(rows["conv_w"][0] + HALO * me_ref[0], 8), HALO)
        for l in range(depth):
            s = packed_refs[l][0]
            c = packed_refs[l][0, mine, :]
            for k in range(1, N_DEV):
                s = s + packed_refs[l][k]
                c = c + packed_refs[l][k, mine, :]
            acc[l] = s
            conv[l] = c
        loss_out[...] = acc[depth - 1, rows["loss"][0]:rows["loss"][0] + 1, :]
        for n in WIDE:
            first, nr = rows[n]
            for l in range(depth):
                for j in range(nr):
                    io[n][1][0][l:l + 1, BLK * j:BLK * (j + 1)] = acc[l, first + j:first + j + 1, :]
            update(n)
        for n in NARROW:
            first, lanes = rows[n][0], w[n].shape[1]
            for l in range(depth):
                io[n][1][0][l:l + 1, :] = acc[l, first:first + 1, 0:lanes]
            update(n)
        first, nr = rows["sgu_b"]
        for l in range(depth):
            io["sgu_b"][1][0][l] = acc[l, first:first + nr, :]
            io["conv_w"][1][0][l] = conv[l, 0:CONV_K, 0:per_dev]
        update("sgu_b")
        update("conv_w")
        (w_ref, m_ref, v_ref), (g_out, d_out, m_out, v_out) = io["sgu_w"]
        for l in range(depth):
            for h in range(heads):
                g = sgu_refs[l][0, h]
                for k in range(1, N_DEV):
                    g = g + sgu_refs[l][k, h]
                delta, mm, vv = _adamw(w_ref[l, h], g, m_ref[l, h], v_ref[l, h])
                g_out[l, h] = g
                d_out[l, h] = delta
                m_out[l, h] = mm
                v_out[l, h] = vv

    def whole(a):
        nd = len(a.shape)
        return pl.BlockSpec(a.shape, lambda i, me_ref: (0,) * nd)

    small_in = [t[n] for n in names for t in (w, m, v)]
    res = pl.pallas_call(
        body, name="small_update",
        grid_spec=pltpu.PrefetchScalarGridSpec(
            num_scalar_prefetch=1, grid=(1,),
            in_specs=[whole(a) for a in list(packed) + list(sgu_w_all) + small_in],
            out_specs=[whole(w[n]) for n in names for _ in range(4)] + [pl.BlockSpec((1, BLK), lambda i, me_ref: (0, 0))],
            scratch_shapes=[pltpu.VMEM((depth, total, BLK), F32), pltpu.VMEM((depth, HALO, BLK), F32)]),
        out_shape=[SDS(w[n].shape, F32) for n in names for _ in range(4)] + [SDS((1, BLK), F32)],
        compiler_params=_params("arbitrary"),
    )(me, *packed, *sgu_w_all, *small_in)
    return {n: list(res[4 * i:4 * i + 4]) for i, n in enumerate(names)}, res[-1][0, 0]


def _pack(arrays):
    flat = jnp.concatenate([a.reshape(-1) for a in arrays])
    rows = -(-flat.shape[0] // (8 * BLK)) * 8
    return jnp.pad(flat, (0, rows * BLK - flat.shape[0])).reshape(rows, BLK)


def _unpack(packed, like):
    flat = packed.reshape(-1)
    out, off = [], 0
    for a in like:
        out.append(flat[off:off + a.size].reshape(a.shape))
        off += a.size
    return out


SMALL = ("ln1_g", "q_norm_g", "k_norm_g", "sinks", "conv_b", "conv_ln_g", "conv_ln_b", "sgu_ln_g", "sgu_ln_b",
         "sgu_w", "sgu_b", "out_norm_g", "ln2_g")
ORDER = ("ln1_g", "w_in", "q_norm_g", "k_norm_g", "sinks", "conv_w", "conv_b", "conv_ln_g", "conv_ln_b", "sgu_ln_g",
         "sgu_ln_b", "sgu_w", "sgu_b", "out_norm_g", "w_out", "ln2_g", "w_up", "w_down")


def _step(x, target, w, m, v):
    depth = w["ln1_g"].shape[0]
    xpos, ypos, cpos = _mesh_pos()
    me = 4 * xpos + 2 * ypos + cpos
    c_arr = jnp.reshape(cpos, (1,)).astype(jnp.int32)
    chip_arr = jnp.reshape(2 * xpos + ypos, (1,)).astype(jnp.int32)

    d = x.shape[1]
    def own_block(shard):
        return lax.dynamic_update_slice_in_dim(lax.empty((N_DEV,) + shard.shape, shard.dtype), shard[None], me, axis=0)

    cw = w["conv_w"]
    order = [(0, "conv_w")] + [(l, n) for l in range(depth) for n in BIG]
    started, gather_token = _start_exchanges("gather_start", [
        ([own_block(_pack([cw]) if n == "conv_w" else w[n][l].astype(BF16))], _gather_plan, 4) for l, n in order])
    pending = dict(zip(order, started))

    def fetch(l, n, after):
        arrays, sends, recvs = pending.pop((l, n))
        land, = _wait_exchange(f"gather_wait_{l}_{n}", arrays, sends, recvs, _gather_plan,
                               gather_token if after is None else after)
        full = _gather_finish(land, "gather_finish_" + n)
        return full if n in ("w_up", "conv_w") else full.reshape(-1, d)

    cw_all = fetch(0, "conv_w", None)
    cw_full = jnp.concatenate([_unpack(cw_all[k], [cw])[0] for k in range(N_DEV)], axis=-1)
    small = {n: w[n] for n in SMALL}
    small["conv_w"] = cw_full

    to_sibling, inflight, shared, leaving = {}, [], {}, []
    rows, total = _small_rows(w)

    def flush(name, first=()):
        groups = list(first) + [g for _, g in leaving]
        started, token = _start_exchanges(name, groups)
        for (record, _), going in zip(leaving, started[len(first):]):
            record(going)
        leaving.clear()
        return started[:len(first)], token

    def start(l, n, grad, after):
        if callable(grad):
            blocks, plan = grad(sel=1 - c_arr, after=after), _half_plan
        else:
            blocks, plan = grad.reshape(N_DEV, -1, d), _pair_plan
        (going,), token = flush(f"grads_start_{l}_{n}", [([blocks, lax.empty((4,) + blocks.shape[1:], BF16)], plan, 4)])
        to_sibling[l, n] = (going, plan, grad)
        return token

    def finish(l, n, after):
        (arrays, sends, recvs), plan, grad = to_sibling.pop((l, n))
        blocks, from_sibling = _wait_exchange(f"pair_wait_{l}_{n}", arrays, sends, recvs, plan, after)
        if callable(grad):
            chip_sums = grad(sel=c_arr, add=from_sibling)
        else:
            chip_sums = _pair_add(blocks, from_sibling, c_arr, "pair_add_" + n)
        leaving.append((lambda going: inflight.append((l, n, going)),
                        ([chip_sums, lax.empty((3,) + chip_sums.shape[1:], BF16)], _chips_plan, 3)))
        return chip_sums

    def share(l, kind, grads):
        block = grads if kind == "sgu_w" else _pack_small(grads, rows, total)
        leaving.append((lambda going: shared.__setitem__((l, kind), going), ([own_block(block)], _all_plan, N_DEV - 1)))

    grad_x, (after, finish_last) = _local_step(x, target, small, depth, fetch, start, finish, share, None)

    as3d = lambda a: a.reshape(depth, -1, a.shape[-1])
    results = {n: None for n in BIG}

    def settle(l, n, going, after):
        arrays, sends, recvs = going
        chip_sums, got = _wait_exchange(f"reduce_wait_{l}_{n}", arrays, sends, recvs, _chips_plan, after)
        results[n], token = _adamw_layer(chip_sums, got, chip_arr, as3d(w[n]), as3d(m[n]), as3d(v[n]), l, results[n],
                                         f"adamw_{l}_{n}")
        return token

    early, late = inflight[:2], inflight[2:]
    for l, n, going in early:
        after = settle(l, n, going, after)
    finish_last(after)
    _, after = flush("grads_start_last")

    landed = {}
    for key in sorted(shared):
        arrays, sends, recvs = shared[key]
        landed[key], = _wait_exchange(f"small_wait_{key[0]}_{key[1]}", arrays, sends, recvs, _all_plan, after)
        after = landed[key]
    out, loss = _small_update(jnp.reshape(me, (1,)).astype(jnp.int32), [landed[l, "rest"] for l in range(depth)],
                              [landed[l, "sgu_w"] for l in range(depth)], w, m, v, rows)
    after = out["sinks"][1]

    for l, n, going in late + inflight[-1:]:
        after = settle(l, n, going, after)
    out.update({n: [r.reshape(w[n].shape) for r in results[n]] for n in BIG})
    return (loss, grad_x[None]) + tuple(out[n][k] for k in range(4) for n in ORDER)


def kernel(x, ln1_g, w_in, q_norm_g, k_norm_g, sinks, conv_w, conv_b, conv_ln_g, conv_ln_b, sgu_ln_g, sgu_ln_b, sgu_w, sgu_b, out_norm_g, w_out, ln2_g, w_up, w_down, loss_target, m_ln1_g, m_w_in, m_q_norm_g, m_k_norm_g, m_sinks, m_conv_w, m_conv_b, m_conv_ln_g, m_conv_ln_b, m_sgu_ln_g, m_sgu_ln_b, m_sgu_w, m_sgu_b, m_out_norm_g, m_w_out, m_ln2_g, m_w_up, m_w_down, v_ln1_g, v_w_in, v_q_norm_g, v_k_norm_g, v_sinks, v_conv_w, v_conv_b, v_conv_ln_g, v_conv_ln_b, v_sgu_ln_g, v_sgu_ln_b, v_sgu_w, v_sgu_b, v_out_norm_g, v_w_out, v_ln2_g, v_w_up, v_w_down):
    w = dict(zip(ORDER, (ln1_g, w_in, q_norm_g, k_norm_g, sinks, conv_w, conv_b, conv_ln_g, conv_ln_b, sgu_ln_g, sgu_ln_b,
                         sgu_w, sgu_b, out_norm_g, w_out, ln2_g, w_up, w_down)))
    m = dict(zip(ORDER, (m_ln1_g, m_w_in, m_q_norm_g, m_k_norm_g, m_sinks, m_conv_w, m_conv_b, m_conv_ln_g, m_conv_ln_b,
                         m_sgu_ln_g, m_sgu_ln_b, m_sgu_w, m_sgu_b, m_out_norm_g, m_w_out, m_ln2_g, m_w_up, m_w_down)))
    v = dict(zip(ORDER, (v_ln1_g, v_w_in, v_q_norm_g, v_k_norm_g, v_sinks, v_conv_w, v_conv_b, v_conv_ln_g, v_conv_ln_b,
                         v_sgu_ln_g, v_sgu_ln_b, v_sgu_w, v_sgu_b, v_out_norm_g, v_w_out, v_ln2_g, v_w_up, v_w_down)))
    for group in (w, m, v):
        group["w_in"] = jnp.swapaxes(group["w_in"], 1, 2)
    out = list(_step(x[0], loss_target[0], w, m, v))
    for k in range(4):
        i = 2 + k * len(ORDER) + ORDER.index("w_in")
        out[i] = jnp.swapaxes(out[i], 1, 2)
    return tuple(out)
```

```python
import functools

import jax
import jax.numpy as jnp
from jax import lax
from jax.experimental import pallas as pl
from jax.experimental.pallas import tpu as pltpu

F32 = jnp.float32
BF16 = jnp.bfloat16
SDS = jax.ShapeDtypeStruct

EPS = 1e-6
NEG_INF = -1e30
HEAD = 64
BLK = 128
CONV_K = 31
HALO = 32
N_DEV = 8

ADAM_LR = 0.001
ADAM_B1 = 0.9
ADAM_B2 = 0.999
ADAM_EPS = 1e-08
ADAM_WD = 0.01
ADAM_STEP = 10

VMEM_LIMIT = 56 * 1024 * 1024

MESH = pl.DeviceIdType.MESH


def _params(*sem):
    return pltpu.CompilerParams(dimension_semantics=sem, vmem_limit_bytes=VMEM_LIMIT)


def _nt(a, b):
    return lax.dot_general(a, b, (((1,), (1,)), ((), ())), preferred_element_type=F32)


def _tn(a, b):
    return lax.dot_general(a, b, (((0,), (0,)), ((), ())), preferred_element_type=F32)


def _nn(a, b):
    return jnp.dot(a, b, preferred_element_type=F32)


def _sigmoid(x):
    return 1.0 / (1.0 + jnp.exp(-x))


ANY = pl.BlockSpec(memory_space=pl.ANY)


def _ordered_behind(body, n_in, in_specs, args, after):
    if after is None:
        return body, in_specs, args
    return (lambda *refs: body(*refs[:n_in], *refs[n_in + 1:])), list(in_specs) + [ANY], list(args) + [after]


def _seg_ones():
    return lax.broadcasted_iota(jnp.int32, (1, BLK), 1) < HEAD


def _segsum(x, first):
    head0 = jnp.where(first, x, 0.0)
    s0 = jnp.sum(head0, axis=-1, keepdims=True)
    s1 = jnp.sum(x - head0, axis=-1, keepdims=True)
    return jnp.where(first, s0, s1)


def _head_rms(x, gain, ones):
    rstd = lax.rsqrt(_segsum(x * x, ones) * (1.0 / HEAD) + EPS)
    xhat = x * rstd
    return xhat * gain, xhat, rstd


def _expand(x, odd, lo):
    if odd:
        xl = pltpu.roll(jnp.where(lo, 0.0, x), HEAD, axis=1)
    else:
        xl = jnp.where(lo, x, 0.0)
    xh = pltpu.roll(xl, HEAD, axis=1)
    return jnp.concatenate([xl, xh], axis=0).astype(BF16)


def _fold(g2, odd, lo):
    r = g2.shape[0] // 2
    s = jnp.where(lo, g2[:r], 0.0) + pltpu.roll(jnp.where(lo, 0.0, g2[r:]), HEAD, axis=1)
    if odd:
        s = pltpu.roll(s, HEAD, axis=1)
    return s


def _attn_mask(n):
    qi = lax.broadcasted_iota(jnp.int32, (BLK, 2 * BLK), 0)
    sj = lax.broadcasted_iota(jnp.int32, (BLK, 2 * BLK), 1)
    rel = qi + BLK - sj
    return (rel >= 0) & (rel < BLK) & ((sj >= BLK) | (n > 0))


def _attn_specs(t, aw, kv):
    prev = lambda n: jnp.maximum(n - 1, 0)
    kb, vb = aw // kv, aw // kv + 1
    return [
        pl.BlockSpec(memory_space=pltpu.SMEM),
        pl.BlockSpec((BLK, aw), lambda n: (n, 0)),
        pl.BlockSpec((BLK, kv), lambda n: (prev(n), kb)),
        pl.BlockSpec((BLK, kv), lambda n: (n, kb)),
        pl.BlockSpec((BLK, kv), lambda n: (prev(n), vb)),
        pl.BlockSpec((BLK, kv), lambda n: (n, vb)),
        pl.BlockSpec((1, BLK), lambda n: (0, 0)),
        pl.BlockSpec((1, BLK), lambda n: (0, 0)),
    ]


def _softmax_pair(s2, valid, sink0, sink1):
    out, psink = [], []
    for half, sink in ((0, sink0), (1, sink1)):
        s = jnp.where(valid, s2[:, 2 * BLK * half:2 * BLK * (half + 1)], NEG_INF)
        m = jnp.maximum(jnp.max(s, axis=-1, keepdims=True), sink)
        p = jnp.exp(s - m)
        es = jnp.exp(sink - m)
        inv = 1.0 / (jnp.sum(p, axis=-1, keepdims=True) + es)
        out.append(p * inv)
        psink.append(es * inv)
    return jnp.concatenate(out, axis=1), psink


def _attn_fwd(proj, qg, kg, sinks, aw, kv):
    t = proj.shape[0]
    n_pairs, n_kvblk = aw // BLK, kv // BLK

    def body(sink_ref, q_ref, kp_ref, kc_ref, vp_ref, vc_ref, qg_ref, kg_ref, o_ref, probs_ref, sink_p_ref):
        n = pl.program_id(0)
        ones = _seg_ones()
        lo = lax.broadcasted_iota(jnp.int32, (1, BLK), 1) < HEAD
        valid = _attn_mask(n)
        kraw = jnp.concatenate([kp_ref[...], kc_ref[...]], axis=0)
        vraw = jnp.concatenate([vp_ref[...], vc_ref[...]], axis=0)
        qn = [_head_rms(q_ref[:, BLK * p:BLK * (p + 1)], qg_ref[...], ones)[0].astype(BF16) for p in range(n_pairs)]
        k2, v2 = [], []
        for b in range(n_kvblk):
            kn = _head_rms(kraw[:, BLK * b:BLK * (b + 1)], kg_ref[...], ones)[0]
            for odd in (0, 1):
                k2.append(_expand(kn, odd, lo))
                v2.append(_expand(vraw[:, BLK * b:BLK * (b + 1)], odd, lo))
        s2 = [_nt(qn[p], k2[p // 2]) * (HEAD ** -0.5) for p in range(n_pairs)]
        soft = [_softmax_pair(s2[p], valid, sink_ref[2 * p], sink_ref[2 * p + 1]) for p in range(n_pairs)]
        lane = lax.broadcasted_iota(jnp.int32, (1, BLK), 1)
        sink_p = jnp.zeros((BLK, BLK), F32)
        for p in range(n_pairs):
            probs_ref[:, 4 * BLK * p:4 * BLK * (p + 1)] = soft[p][0]
            for half in (0, 1):
                sink_p = jnp.where(lane == 2 * p + half, soft[p][1][half], sink_p)
            o_ref[:, BLK * p:BLK * (p + 1)] = _nn(soft[p][0].astype(BF16), v2[p // 2])
        sink_p_ref[...] = sink_p

    return pl.pallas_call(
        body, name="attn_fwd", grid=(t // BLK,),
        in_specs=_attn_specs(t, aw, kv),
        out_specs=[pl.BlockSpec((BLK, aw), lambda n: (n, 0)), pl.BlockSpec((BLK, 4 * aw), lambda n: (n, 0)),
                   pl.BlockSpec((BLK, BLK), lambda n: (n, 0))],
        out_shape=[SDS((t, aw), F32), SDS((t, 4 * aw), F32), SDS((t, BLK), F32)],
        compiler_params=_params("arbitrary"),
    )(sinks, proj, proj, proj, proj, proj, qg, kg)


def _attn_bwd(proj, dy, probs, sink_p, qg, kg, sinks, aw, kv, after=None):
    t = proj.shape[0]
    nb = t // BLK
    n_kvblk = kv // BLK
    kb = aw // kv

    def body(sink_ref, q_ref, kp_ref, kc_ref, vp_ref, vc_ref, qg_ref, kg_ref, dy_ref, kall_ref, probs_ref, sink_p_ref,
             dqkv_ref, dstat_ref, dk_acc, dv_acc, dqg_acc):
        del sink_ref
        n = pl.program_id(0)
        ones = _seg_ones()
        lane = lax.broadcasted_iota(jnp.int32, (1, BLK), 1)
        lo = lane < HEAD

        @pl.when(n == 0)
        def _():
            dk_acc[...] = jnp.zeros_like(dk_acc)
            dv_acc[...] = jnp.zeros_like(dv_acc)
            dqg_acc[...] = jnp.zeros_like(dqg_acc)
            dstat_ref[...] = jnp.zeros_like(dstat_ref)

        kraw = jnp.concatenate([kp_ref[...], kc_ref[...]], axis=0)
        vraw = jnp.concatenate([vp_ref[...], vc_ref[...]], axis=0)
        row = pl.multiple_of(n * BLK, BLK)
        prow = pl.multiple_of(jnp.maximum(n - 1, 0) * BLK, BLK)
        pairs = range(aw // BLK)
        cols = [slice(BLK * p, BLK * (p + 1)) for p in pairs]
        qs = [_head_rms(q_ref[:, cols[p]], qg_ref[...], ones) for p in pairs]
        qb = [qs[p][0].astype(BF16) for p in pairs]
        k2, v2 = [], []
        for b in range(n_kvblk):
            kn = _head_rms(kraw[:, BLK * b:BLK * (b + 1)], kg_ref[...], ones)[0]
            for odd in (0, 1):
                k2.append(_expand(kn, odd, lo))
                v2.append(_expand(vraw[:, BLK * b:BLK * (b + 1)], odd, lo))
        p2 = [probs_ref[:, 4 * BLK * p:4 * BLK * (p + 1)] for p in pairs]
        dob = [dy_ref[:, cols[p]].astype(BF16) for p in pairs]
        dp2 = [_nt(dob[p], v2[p // 2]) for p in pairs]
        deltas = jnp.zeros((BLK, BLK), F32)
        ds2 = []
        for p in pairs:
            ds = []
            for half in (0, 1):
                hs = slice(2 * BLK * half, 2 * BLK * (half + 1))
                ph = p2[p][:, hs]
                delta = jnp.sum(ph * dp2[p][:, hs], axis=-1, keepdims=True)
                ds.append(ph * (dp2[p][:, hs] - delta))
                deltas = jnp.where(lane == 2 * p + half, delta, deltas)
            ds2.append((jnp.concatenate(ds, axis=1) * (HEAD ** -0.5)).astype(BF16))
        dstat_ref[2:3, :] -= jnp.sum(sink_p_ref[...] * deltas, axis=0, keepdims=True)
        dqn = [_nn(ds2[p], k2[p // 2]) for p in pairs]
        dk2 = [_tn(ds2[p], qb[p]) for p in pairs]
        dv2 = [_tn(p2[p].astype(BF16), dob[p]) for p in pairs]
        for p in pairs:
            _, qhat, rstd = qs[p]
            dqhat = dqn[p] * qg_ref[...]
            proj_q = _segsum(dqhat * qhat, ones) * (1.0 / HEAD)
            dqkv_ref[pl.ds(row, BLK), cols[p]] = (rstd * (dqhat - qhat * proj_q)).astype(BF16)
            dqg_acc[:, cols[p]] += jnp.sum(dqn[p] * qhat, axis=0, keepdims=True)
        for b in range(n_kvblk):
            dkn = jnp.zeros((2 * BLK, BLK), F32)
            dvb = jnp.zeros((2 * BLK, BLK), F32)
            for odd in (0, 1):
                j = 2 * b + odd
                dkn = dkn + _fold(dk2[2 * j] + dk2[2 * j + 1], odd, lo)
                dvb = dvb + _fold(dv2[2 * j] + dv2[2 * j + 1], odd, lo)
            kcols = slice(BLK * b, BLK * (b + 1))
            dk_acc[pl.ds(prow, BLK), kcols] += dkn[:BLK]
            dv_acc[pl.ds(prow, BLK), kcols] += dvb[:BLK]
            dk_acc[pl.ds(row, BLK), kcols] += dkn[BLK:]
            dv_acc[pl.ds(row, BLK), kcols] += dvb[BLK:]

        @pl.when(n == nb - 1)
        def _():
            dqg = dqg_acc[:, 0:BLK]
            for p in range(1, aw // BLK):
                dqg = dqg + dqg_acc[:, BLK * p:BLK * (p + 1)]
            dstat_ref[0:1, :] = dqg + pltpu.roll(dqg, HEAD, axis=1)

            def kblock(i, dkg):
                r = pl.multiple_of(i * BLK, BLK)
                for b in range(n_kvblk):
                    kcols = slice(BLK * b, BLK * (b + 1))
                    _, khat, rstd = _head_rms(kall_ref[pl.ds(r, BLK), kcols], kg_ref[...], ones)
                    dkn = dk_acc[pl.ds(r, BLK), kcols]
                    dkhat = dkn * kg_ref[...]
                    proj_k = _segsum(dkhat * khat, ones) * (1.0 / HEAD)
                    dqkv_ref[pl.ds(r, BLK), aw + BLK * b:aw + BLK * (b + 1)] = (rstd * (dkhat - khat * proj_k)).astype(BF16)
                    dqkv_ref[pl.ds(r, BLK), aw + kv + BLK * b:aw + kv + BLK * (b + 1)] = dv_acc[pl.ds(r, BLK), kcols].astype(BF16)
                    dkg = dkg + jnp.sum(dkn * khat, axis=0, keepdims=True)
                return dkg

            dkg = lax.fori_loop(0, nb, kblock, jnp.zeros((1, BLK), F32))
            dstat_ref[1:2, :] = dkg + pltpu.roll(dkg, HEAD, axis=1)

    in_specs = _attn_specs(t, aw, kv) + [
        pl.BlockSpec((BLK, aw), lambda n: (n, 0)),
        pl.BlockSpec((t, kv), lambda n: (0, kb)),
        pl.BlockSpec((BLK, 4 * aw), lambda n: (n, 0)),
        pl.BlockSpec((BLK, BLK), lambda n: (n, 0)),
    ]
    args = [sinks, proj, proj, proj, proj, proj, qg, kg, dy, proj, probs, sink_p]
    body, in_specs, args = _ordered_behind(body, len(args), in_specs, args, after)
    return pl.pallas_call(
        body, name="attn_bwd", grid=(nb,),
        in_specs=in_specs,
        out_specs=[pl.BlockSpec((t, aw + 2 * kv), lambda n: (0, 0)), pl.BlockSpec((8, BLK), lambda n: (0, 0))],
        out_shape=[SDS((t, aw + 2 * kv), BF16), SDS((8, BLK), F32)],
        scratch_shapes=[pltpu.VMEM((t, kv), F32), pltpu.VMEM((t, kv), F32), pltpu.VMEM((1, aw), F32)],
        compiler_params=_params("arbitrary"),
    )(*args)


def _conv_taps(win, w_ref, shift_of):
    rows = win.shape[0]
    acc = None
    for j in range(CONV_K):
        term = pltpu.roll(win, (rows - shift_of(j)) % rows, axis=0)[:BLK] * w_ref[j:j + 1, :]
        acc = term if acc is None else acc + term
    return acc


def _layer_norm_fwd(z):
    mu = jnp.mean(z, axis=-1, keepdims=True)
    zc = z - mu
    rstd = lax.rsqrt(jnp.mean(zc * zc, axis=-1, keepdims=True) + EPS)
    return zc * rstd, rstd


def _layer_norm_bwd(dy, yhat, rstd, g):
    dyh = dy * g
    return rstd * (dyh - jnp.mean(dyh, axis=-1, keepdims=True) - yhat * jnp.mean(dyh * yhat, axis=-1, keepdims=True))


def _conv_fill_glu(a_ref, g_ref, hpad, nb):
    hpad[0:HALO, :] = jnp.zeros((HALO, hpad.shape[1]), F32)

    def fill(i, c):
        r = pl.multiple_of(i * BLK, BLK)
        hpad[pl.ds(pl.multiple_of(r + HALO, HALO), BLK), :] = a_ref[pl.ds(r, BLK), :] * _sigmoid(g_ref[pl.ds(r, BLK), :])
        return c

    lax.fori_loop(0, nb, fill, 0)


def _conv_specs(t, cw, d_in):
    base = (d_in - 4 * cw) // cw
    vec = pl.BlockSpec((1, cw), lambda i: (0, 0))
    return [
        pl.BlockSpec((t, cw), lambda i: (0, base)),
        pl.BlockSpec((t, cw), lambda i: (0, base + 1)),
        pl.BlockSpec((HALO, cw), lambda i: (0, 0)),
        vec, vec, vec,
    ]


def _conv_fwd(proj, cw_pad, cb, lg, lb, cw):
    t, d_in = proj.shape
    nb = t // BLK

    def body(a_ref, g_ref, w_ref, b_ref, lg_ref, lb_ref, o_ref, z_ref, hpad):
        _conv_fill_glu(a_ref, g_ref, hpad, nb)

        def blk(i, c):
            r = pl.multiple_of(i * BLK, BLK)
            z = _conv_taps(hpad[pl.ds(r, BLK + HALO), :], w_ref, lambda j: j + HALO - (CONV_K - 1)) + b_ref[...]
            z_ref[pl.ds(r, BLK), :] = z
            yhat, _ = _layer_norm_fwd(z)
            y = yhat * lg_ref[...] + lb_ref[...]
            o_ref[pl.ds(r, BLK), :] = y * _sigmoid(y)
            return c

        lax.fori_loop(0, nb, blk, 0)

    whole = pl.BlockSpec((t, cw), lambda i: (0, 0))
    return pl.pallas_call(
        body, name="conv_fwd", grid=(1,),
        in_specs=_conv_specs(t, cw, d_in),
        out_specs=[whole, whole],
        out_shape=[SDS((t, cw), F32), SDS((t, cw), F32)],
        scratch_shapes=[pltpu.VMEM((t + HALO, cw), F32)],
        compiler_params=_params("arbitrary"),
    )(proj, proj, cw_pad, cb, lg, lb)


def _conv_bwd(proj, dy, z, cw_pad, cb, lg, lb, cw):
    t, d_in = proj.shape
    nb = t // BLK

    def body(a_ref, g_ref, w_ref, b_ref, lg_ref, lb_ref, dy_ref, z_ref, dc_ref, dw_ref, dvec_ref, hpad, dzpad, dwacc):
        del b_ref
        _conv_fill_glu(a_ref, g_ref, hpad, nb)
        dzpad[t:t + HALO, :] = jnp.zeros((HALO, cw), F32)
        dwacc[...] = jnp.zeros_like(dwacc)

        def blk(i, carry):
            db, dlg, dlb = carry
            r = pl.multiple_of(i * BLK, BLK)
            win = hpad[pl.ds(r, BLK + HALO), :]
            yhat, rstd = _layer_norm_fwd(z_ref[pl.ds(r, BLK), :])
            y = yhat * lg_ref[...] + lb_ref[...]
            sg = _sigmoid(y)
            dyl = dy_ref[pl.ds(r, BLK), :] * (sg * (1.0 + y * (1.0 - sg)))
            dz = _layer_norm_bwd(dyl, yhat, rstd, lg_ref[...])
            dzpad[pl.ds(r, BLK), :] = dz
            for j in range(CONV_K):
                sh = j + HALO - (CONV_K - 1)
                prod = dz * pltpu.roll(win, (BLK + HALO - sh) % (BLK + HALO), axis=0)[:BLK]
                dwacc[8 * j:8 * j + 8, :] += jnp.sum(prod.reshape(BLK // 8, 8, cw), axis=0)
            return (db + jnp.sum(dz, axis=0, keepdims=True),
                    dlg + jnp.sum(dyl * yhat, axis=0, keepdims=True),
                    dlb + jnp.sum(dyl, axis=0, keepdims=True))

        zero = jnp.zeros((1, cw), F32)
        db, dlg, dlb = lax.fori_loop(0, nb, blk, (zero, zero, zero))
        dvec_ref[...] = jnp.zeros_like(dvec_ref)
        dvec_ref[0:1, :] = db
        dvec_ref[1:2, :] = dlg
        dvec_ref[2:3, :] = dlb
        dw_ref[...] = jnp.sum(dwacc[...].reshape(HALO, 8, cw), axis=1)

        def blk2(i, c):
            r = pl.multiple_of(i * BLK, BLK)
            dh = _conv_taps(dzpad[pl.ds(r, BLK + HALO), :], w_ref, lambda j: CONV_K - 1 - j)
            a = a_ref[pl.ds(r, BLK), :]
            sg = _sigmoid(g_ref[pl.ds(r, BLK), :])
            dc_ref[pl.ds(r, BLK), 0:cw] = (dh * sg).astype(BF16)
            dc_ref[pl.ds(r, BLK), cw:2 * cw] = (dh * a * sg * (1.0 - sg)).astype(BF16)
            return c

        lax.fori_loop(0, nb, blk2, 0)

    return pl.pallas_call(
        body, name="conv_bwd", grid=(1,),
        in_specs=_conv_specs(t, cw, d_in) + [pl.BlockSpec((t, cw), lambda i: (0, 0))] * 2,
        out_specs=[pl.BlockSpec((t, 2 * cw), lambda i: (0, 0)), pl.BlockSpec((HALO, cw), lambda i: (0, 0)),
                   pl.BlockSpec((8, cw), lambda i: (0, 0))],
        out_shape=[SDS((t, 2 * cw), BF16), SDS((HALO, cw), F32), SDS((8, cw), F32)],
        scratch_shapes=[pltpu.VMEM((t + HALO, cw), F32), pltpu.VMEM((t + HALO, cw), F32), pltpu.VMEM((8 * HALO, cw), F32)],
        compiler_params=_params("arbitrary"),
    )(proj, proj, cw_pad, cb, lg, lb, dy, z)


def _tril_bf16(w):
    r = lax.broadcasted_iota(jnp.int32, (BLK, BLK), 0)
    c = lax.broadcasted_iota(jnp.int32, (BLK, BLK), 1)
    return jnp.where(r >= c, w, 0.0).astype(BF16)


def _sgu_specs(sw, d_in, heads):
    base = (d_in - 2 * sw) // sw
    vec = pl.BlockSpec((1, sw), lambda n: (0, 0))
    return [
        pl.BlockSpec((BLK, sw), lambda n: (n, base)),
        pl.BlockSpec((BLK, sw), lambda n: (n, base + 1)),
        vec, vec,
        pl.BlockSpec((heads, BLK, BLK), lambda n: (0, 0, 0)),
        pl.BlockSpec((BLK, sw), lambda n: (0, 0)),
    ]


def _sgu_mix(w_ref, vnb, heads, sw, transpose):
    head_of = lax.broadcasted_iota(jnp.int32, (1, sw), 1) // HEAD
    s = jnp.zeros((BLK, sw), F32)
    for h in range(heads):
        wt = _tril_bf16(w_ref[h])
        mixed = _tn(wt, vnb) if transpose else _nn(wt, vnb)
        s = jnp.where(head_of == h, mixed, s)
    return s


def _sgu_fwd(proj, lg, lb, w, bias_full, sw):
    t, d_in = proj.shape
    heads = sw // HEAD

    def body(u_ref, v_ref, lg_ref, lb_ref, w_ref, bias_ref, o_ref):
        vhat, _ = _layer_norm_fwd(v_ref[...])
        vn = (vhat * lg_ref[...] + lb_ref[...]).astype(BF16)
        s = _sgu_mix(w_ref, vn, heads, sw, False) + bias_ref[...]
        o_ref[...] = u_ref[...] * s

    return pl.pallas_call(
        body, name="sgu_fwd", grid=(t // BLK,),
        in_specs=_sgu_specs(sw, d_in, heads),
        out_specs=pl.BlockSpec((BLK, sw), lambda n: (n, 0)),
        out_shape=SDS((t, sw), F32),
        compiler_params=_params("arbitrary"),
    )(proj, proj, lg, lb, w, bias_full)


def _sgu_bwd(proj, dy, lg, lb, w, bias_full, sw, after=None):
    t, d_in = proj.shape
    heads = sw // HEAD
    nb = t // BLK

    def body(u_ref, v_ref, lg_ref, lb_ref, w_ref, bias_ref, dy_ref, ds_ref, dw_ref, db_ref, dvec_ref, dbfull):
        n = pl.program_id(0)

        @pl.when(n == 0)
        def _():
            dw_ref[...] = jnp.zeros_like(dw_ref)
            dvec_ref[...] = jnp.zeros_like(dvec_ref)
            dbfull[...] = jnp.zeros_like(dbfull)

        vhat, rstd = _layer_norm_fwd(v_ref[...])
        vn = (vhat * lg_ref[...] + lb_ref[...]).astype(BF16)
        s = _sgu_mix(w_ref, vn, heads, sw, False) + bias_ref[...]
        dy = dy_ref[...]
        ds_ref[:, 0:sw] = (dy * s).astype(BF16)
        dsv = dy * u_ref[...]
        dbfull[...] += dsv
        head_of = lax.broadcasted_iota(jnp.int32, (1, sw), 1) // HEAD
        r = lax.broadcasted_iota(jnp.int32, (BLK, BLK), 0)
        c = lax.broadcasted_iota(jnp.int32, (BLK, BLK), 1)
        dsb = dsv.astype(BF16)
        for h in range(heads):
            dsh = jnp.where(head_of == h, dsv, 0.0).astype(BF16)
            dw_ref[h] += jnp.where(r >= c, _nt(dsh, vn), 0.0)
        dvn = _sgu_mix(w_ref, dsb, heads, sw, True)
        dvec_ref[0:1, :] += jnp.sum(dvn * vhat, axis=0, keepdims=True)
        dvec_ref[1:2, :] += jnp.sum(dvn, axis=0, keepdims=True)
        ds_ref[:, sw:2 * sw] = _layer_norm_bwd(dvn, vhat, rstd, lg_ref[...]).astype(BF16)

        @pl.when(n == nb - 1)
        def _():
            sel = (lax.broadcasted_iota(jnp.int32, (sw, BLK), 0) // HEAD == lax.broadcasted_iota(jnp.int32, (sw, BLK), 1)).astype(BF16)
            x = dbfull[...]
            hi = x.astype(BF16)
            r1 = x - hi.astype(F32)
            mid = r1.astype(BF16)
            low = (r1 - mid.astype(F32)).astype(BF16)
            db_ref[...] = _nn(hi, sel) + _nn(mid, sel) + _nn(low, sel)

    in_specs = _sgu_specs(sw, d_in, heads) + [pl.BlockSpec((BLK, sw), lambda n: (n, 0))]
    body, in_specs, args = _ordered_behind(body, 7, in_specs, [proj, proj, lg, lb, w, bias_full, dy], after)
    return pl.pallas_call(
        body, name="sgu_bwd", grid=(nb,),
        in_specs=in_specs,
        out_specs=[pl.BlockSpec((BLK, 2 * sw), lambda n: (n, 0)), pl.BlockSpec((heads, BLK, BLK), lambda n: (0, 0, 0)),
                   pl.BlockSpec((BLK, BLK), lambda n: (0, 0)), pl.BlockSpec((8, sw), lambda n: (0, 0))],
        out_shape=[SDS((t, 2 * sw), BF16), SDS((heads, BLK, BLK), F32), SDS((BLK, BLK), F32), SDS((8, sw), F32)],
        scratch_shapes=[pltpu.VMEM((BLK, sw), F32)],
        compiler_params=_params("arbitrary"),
    )(*args)


def _rms_fwd(x, g):
    return (x * lax.rsqrt(jnp.mean(x * x, axis=-1, keepdims=True) + EPS)) * g


def _rms_bwd(dh, x, g):
    rstd = lax.rsqrt(jnp.mean(x * x, axis=-1, keepdims=True) + EPS)
    xhat = x * rstd
    dgx = dh * g
    dx = rstd * (dgx - xhat * jnp.mean(dgx * xhat, axis=-1, keepdims=True))
    return dx, jnp.sum(dh * xhat, axis=0, keepdims=True)


def _rms_matmul(x, g, w, tm, tn, relu2, name, transposed=False):
    t, d = x.shape
    if w.ndim == 3:
        assert w.shape[2] == tn
        n = w.shape[0] * tn
        w_spec = pl.BlockSpec((None, d, tn), lambda i, j: (j, 0, 0))
    elif transposed:
        n = w.shape[0]
        w_spec = pl.BlockSpec((tn, d), lambda i, j: (j, 0))
    else:
        n = w.shape[1]
        w_spec = pl.BlockSpec((d, tn), lambda i, j: (0, j))

    def body(x_ref, g_ref, w_ref, h_ref, *outs):
        @pl.when(pl.program_id(1) == 0)
        def _():
            h_ref[...] = _rms_fwd(x_ref[...], g_ref[...]).astype(BF16)

        acc = _nt(h_ref[...], w_ref[...]) if transposed else _nn(h_ref[...], w_ref[...])
        if relu2:
            r = jnp.maximum(acc, 0.0)
            outs[0][...] = (r * r).astype(BF16)
            outs[1][...] = r.astype(BF16)
        else:
            outs[0][...] = acc

    tile = pl.BlockSpec((tm, tn), lambda i, j: (i, j))
    row = pl.BlockSpec((tm, d), lambda i, j: (i, 0))
    outs = [SDS((t, n), BF16), SDS((t, n), BF16)] if relu2 else [SDS((t, n), F32)]
    return pl.pallas_call(
        body, name=name, grid=(t // tm, n // tn),
        in_specs=[row, pl.BlockSpec((1, d), lambda i, j: (0, 0)), w_spec],
        out_specs=[row] + [tile] * len(outs),
        out_shape=[SDS((t, d), BF16)] + outs,
        compiler_params=_params("parallel", "arbitrary"),
    )(x, g, w)


def _group_rms_matmul(ys, g, w, res, tm, tn):
    t, d = res.shape
    widths = [y.shape[1] for y in ys]
    k = sum(widths)

    def body(*refs):
        y_refs, (g_ref, w_ref, res_ref, mix_ref, o_ref) = refs[:len(ys)], refs[len(ys):]

        @pl.when(pl.program_id(1) == 0)
        def _():
            c = 0
            for y_ref, wd in zip(y_refs, widths):
                mix_ref[:, c:c + wd] = _rms_fwd(y_ref[...], g_ref[:, c:c + wd]).astype(BF16)
                c += wd

        o_ref[...] = res_ref[...] + _nn(mix_ref[...], w_ref[...])

    tile = pl.BlockSpec((tm, tn), lambda i, j: (i, j))
    return pl.pallas_call(
        body, name="mix_out", grid=(t // tm, d // tn),
        in_specs=[pl.BlockSpec((tm, wd), lambda i, j: (i, 0)) for wd in widths] + [
            pl.BlockSpec((1, k), lambda i, j: (0, 0)), pl.BlockSpec((k, tn), lambda i, j: (0, j)), tile],
        out_specs=[pl.BlockSpec((tm, k), lambda i, j: (i, 0)), tile],
        out_shape=[SDS((t, k), BF16), SDS((t, d), F32)],
        compiler_params=_params("parallel", "arbitrary"),
    )(*ys, g, w, res)


def _matmul_res(a, w, res, tm, tn, tk):
    t, k = a.shape
    n = w.shape[1]
    nk = k // tk

    def body(a_ref, w_ref, res_ref, o_ref, acc):
        kk = pl.program_id(2)

        @pl.when(kk == 0)
        def _():
            acc[...] = res_ref[...]

        acc[...] += _nn(a_ref[...], w_ref[...])

        @pl.when(kk == nk - 1)
        def _():
            o_ref[...] = acc[...]

    tile = pl.BlockSpec((tm, tn), lambda i, j, kk: (i, j))
    return pl.pallas_call(
        body, name="mlp_down", grid=(t // tm, n // tn, nk),
        in_specs=[pl.BlockSpec((tm, tk), lambda i, j, kk: (i, kk)), pl.BlockSpec((tk, tn), lambda i, j, kk: (kk, j)), tile],
        out_specs=tile,
        out_shape=SDS((t, n), F32),
        scratch_shapes=[pltpu.VMEM((tm, tn), F32)],
        compiler_params=_params("parallel", "parallel", "arbitrary"),
    )(a, w, res)


def _loss_grad(y, target, tm):
    t, d = y.shape

    def body(y_ref, t_ref, dy_ref, dyb_ref, l_ref):
        @pl.when(pl.program_id(0) == 0)
        def _():
            l_ref[...] = jnp.zeros_like(l_ref)

        err = y_ref[...] - t_ref[...]
        dy = err * (1.0 / d)
        dy_ref[...] = dy
        dyb_ref[...] = dy.astype(BF16)
        per_row = jnp.mean(err * err, axis=-1, keepdims=True)
        l_ref[...] += jnp.sum(per_row, axis=0, keepdims=True) * 0.5

    row = pl.BlockSpec((tm, d), lambda i: (i, 0))
    return pl.pallas_call(
        body, name="loss_grad", grid=(t // tm,),
        in_specs=[row, row], out_specs=[row, row, pl.BlockSpec((8, BLK), lambda i: (0, 0))],
        out_shape=[SDS((t, d), F32), SDS((t, d), BF16), SDS((8, BLK), F32)],
        compiler_params=_params("arbitrary"),
    )(y, target)


def _mlp_dact(dxb, w_down, r, tn, after=None):
    t, d = dxb.shape
    f = w_down.shape[0]

    def body(dxb_ref, w_ref, r_ref, dup_ref):
        dup_ref[...] = (_nt(dxb_ref[...], w_ref[...]) * (2.0 * r_ref[...].astype(F32))).astype(BF16)

    tile = pl.BlockSpec((t, tn), lambda j: (0, j))
    body, in_specs, args = _ordered_behind(
        body, 3, [pl.BlockSpec((t, d), lambda j: (0, 0)), pl.BlockSpec((tn, d), lambda j: (j, 0)), tile],
        [dxb, w_down, r], after)
    return pl.pallas_call(
        body, name="mlp_dact", grid=(f // tn,),
        in_specs=in_specs, out_specs=tile,
        out_shape=SDS((t, f), BF16),
        compiler_params=_params("arbitrary"),
    )(*args)


def _grad_w_stacked(pieces, b, tm, name, after=None):
    t, n = b.shape
    tiles = [p.shape[1] // tm for p in pieces]
    first = [sum(tiles[:k]) for k in range(len(pieces))]

    def body(*refs):
        b_ref, o_ref = refs[len(pieces)], refs[-1]
        i = pl.program_id(0)
        for k, p_ref in enumerate(refs[:len(pieces)]):
            @pl.when((i >= first[k]) & (i < first[k] + tiles[k]))
            def _(p_ref=p_ref):
                o_ref[...] = _tn(p_ref[...], b_ref[...]).astype(BF16)

    in_specs = [pl.BlockSpec((t, tm), lambda i, k=k: (0, jnp.clip(i - first[k], 0, tiles[k] - 1))) for k in range(len(pieces))]
    in_specs.append(pl.BlockSpec((t, n), lambda i: (0, 0)))
    body, in_specs, args = _ordered_behind(body, len(pieces) + 1, in_specs, [*pieces, b], after)
    return pl.pallas_call(
        body, name=name, grid=(sum(tiles),),
        in_specs=in_specs, out_specs=pl.BlockSpec((tm, n), lambda i: (i, 0)),
        out_shape=SDS((sum(tiles) * tm, n), BF16),
        compiler_params=_params("arbitrary"),
    )(*args)


def _grad_w_half(a, b, tm, tn, name, by_cols, sel, add=None, after=None):
    t, m = a.shape
    n = b.shape[1]
    if by_cols:
        per = n // N_DEV // tn
        grid, shape = (4, m // tm, per), (4, m, n // N_DEV)
        a_spec = pl.BlockSpec((t, tm), lambda q, i, j, s: (0, i))
        b_spec = pl.BlockSpec((t, tn), lambda q, i, j, s: (0, (2 * q + s[0]) * per + j))
    else:
        per = m // N_DEV // tm
        grid, shape = (4, per, n // tn), (4, m // N_DEV, n)
        a_spec = pl.BlockSpec((t, tm), lambda q, i, j, s: (0, (2 * q + s[0]) * per + i))
        b_spec = pl.BlockSpec((t, tn), lambda q, i, j, s: (0, j))
    assert per >= 1
    tile = pl.BlockSpec((None, tm, tn), lambda q, i, j, s: (q, i, j))

    def body(sel_ref, a_ref, b_ref, *rest):
        acc = _tn(a_ref[...], b_ref[...])
        if add is not None:
            acc = acc + rest[0][...].astype(F32)
        rest[-1][...] = acc.astype(BF16)

    in_specs, args = [a_spec, b_spec], [a, b]
    if add is not None:
        in_specs, args = in_specs + [tile], args + [add]
    if after is not None:
        in_specs, args = in_specs + [ANY], args + [after]
    return pl.pallas_call(
        body, name=name,
        grid_spec=pltpu.PrefetchScalarGridSpec(num_scalar_prefetch=1, grid=grid, in_specs=in_specs, out_specs=tile),
        out_shape=SDS(shape, BF16),
        compiler_params=_params("arbitrary", "arbitrary", "arbitrary"),
    )(sel, *args)


def _mlp_dnorm(dup, w_up, x, g, dres, tm, after=None):
    t, f = dup.shape
    d = x.shape[1]
    nk, _, tk = w_up.shape

    def body(a_ref, w_ref, x_ref, g_ref, dres_ref, dx_ref, dg_ref, acc):
        i, kk = pl.program_id(0), pl.program_id(1)

        @pl.when((i == 0) & (kk == 0))
        def _():
            dg_ref[...] = jnp.zeros_like(dg_ref)

        @pl.when(kk == 0)
        def _():
            acc[...] = jnp.zeros_like(acc)

        acc[...] += _nt(a_ref[...], w_ref[...])

        @pl.when(kk == nk - 1)
        def _():
            dx, dg = _rms_bwd(acc[...], x_ref[...], g_ref[...])
            dx_ref[...] = dres_ref[...] + dx
            dg_ref[0:1, :] += dg

    row = pl.BlockSpec((tm, d), lambda i, kk: (i, 0))
    in_specs = [pl.BlockSpec((tm, tk), lambda i, kk: (i, kk)), pl.BlockSpec((None, d, tk), lambda i, kk: (kk, 0, 0)),
                row, pl.BlockSpec((1, d), lambda i, kk: (0, 0)), row]
    body, in_specs, args = _ordered_behind(body, 5, in_specs, [dup, w_up, x, g, dres], after)
    return pl.pallas_call(
        body, name="mlp_dnorm", grid=(t // tm, nk),
        in_specs=in_specs,
        out_specs=[row, pl.BlockSpec((8, d), lambda i, kk: (0, 0))],
        out_shape=[SDS((t, d), F32), SDS((8, d), F32)],
        scratch_shapes=[pltpu.VMEM((tm, d), F32)],
        compiler_params=_params("arbitrary", "arbitrary"),
    )(*args)


def _mix_dnorm(dx, w_out, ys, g, tm, after=None):
    t, d = dx.shape
    k = w_out.shape[0]
    widths = [y.shape[1] for y in ys]

    def body(dx_ref, w_ref, *refs):
        y_refs = refs[:len(ys)]
        g_ref, dxb_ref = refs[len(ys)], refs[len(ys) + 1]
        dy_refs = refs[len(ys) + 2:2 * len(ys) + 2]
        dg_ref = refs[-1]

        @pl.when(pl.program_id(0) == 0)
        def _():
            dg_ref[...] = jnp.zeros_like(dg_ref)

        dxb = dx_ref[...].astype(BF16)
        dxb_ref[...] = dxb
        dmix = _nt(dxb, w_ref[...])
        c = 0
        for y_ref, dy_ref, wd in zip(y_refs, dy_refs, widths):
            dy, dg = _rms_bwd(dmix[:, c:c + wd], y_ref[...], g_ref[:, c:c + wd])
            dy_ref[...] = dy
            dg_ref[0:1, c:c + wd] += dg
            c += wd

    row = pl.BlockSpec((tm, d), lambda i: (i, 0))
    yspecs = [pl.BlockSpec((tm, wd), lambda i: (i, 0)) for wd in widths]
    in_specs = [row, pl.BlockSpec((k, d), lambda i: (0, 0))] + yspecs + [pl.BlockSpec((1, k), lambda i: (0, 0))]
    body, in_specs, args = _ordered_behind(body, len(in_specs), in_specs, [dx, w_out, *ys, g], after)
    return pl.pallas_call(
        body, name="mix_dnorm", grid=(t // tm,),
        in_specs=in_specs,
        out_specs=[row] + yspecs + [pl.BlockSpec((8, k), lambda i: (0, 0))],
        out_shape=[SDS((t, d), BF16)] + [SDS((t, wd), F32) for wd in widths] + [SDS((8, k), F32)],
        compiler_params=_params("arbitrary"),
    )(*args)


def _in_dnorm(dps, w_in_t, x, g, dres, tm):
    t, d = x.shape
    widths = [p.shape[1] for p in dps]
    offs = [sum(widths[:p]) for p in range(len(dps))]
    n_in = w_in_t.shape[0]

    def body(*refs):
        p_refs = refs[:len(dps)]
        w_ref, x_ref, g_ref, dres_ref, dx_ref, dxb_ref, dg_ref = refs[len(dps):]

        @pl.when(pl.program_id(0) == 0)
        def _():
            dg_ref[...] = jnp.zeros_like(dg_ref)

        acc = None
        for p_ref, off, wd in zip(p_refs, offs, widths):
            term = _nn(p_ref[...], w_ref[off:off + wd, :])
            acc = term if acc is None else acc + term
        dx, dg = _rms_bwd(acc, x_ref[...], g_ref[...])
        dx = dres_ref[...] + dx
        dx_ref[...] = dx
        dxb_ref[...] = dx.astype(BF16)
        dg_ref[0:1, :] += dg

    row = pl.BlockSpec((tm, d), lambda i: (i, 0))
    return pl.pallas_call(
        body, name="in_dnorm", grid=(t // tm,),
        in_specs=[pl.BlockSpec((tm, wd), lambda i: (i, 0)) for wd in widths] + [
            pl.BlockSpec((n_in, d), lambda i: (0, 0)), row, pl.BlockSpec((1, d), lambda i: (0, 0)), row],
        out_specs=[row, row, pl.BlockSpec((8, d), lambda i: (0, 0))],
        out_shape=[SDS((t, d), F32), SDS((t, d), BF16), SDS((8, d), F32)],
        compiler_params=_params("arbitrary"),
    )(*dps, w_in_t, x, g, dres)


def _tile(n, want):
    return min(n, want)


def _row_tile(n, want):
    return max(k for k in range(8, min(n, want) + 1, 8) if n % k == 0)


def _layer_fwd(x, p, fetch, after):
    t, d = x.shape
    aw, kv, cw, sw = d // 2, d // 8, d // 4, d // 4
    tm = _tile(t, 1024)
    w_in = fetch("w_in", after)
    h1, proj = _rms_matmul(x, p["ln1_g"], w_in, tm, 512 if w_in.shape[0] % 512 == 0 else 256, False, "in_proj",
                           transposed=True)
    y_attn, probs, sink_p = _attn_fwd(proj, p["qg"], p["kg"], p["sinks"], aw, kv)
    y_conv, z_conv = _conv_fwd(proj, p["conv_w"], p["conv_b"], p["conv_ln_g"], p["conv_ln_b"], cw)
    y_sgu = _sgu_fwd(proj, p["sgu_ln_g"], p["sgu_ln_b"], p["sgu_w"], p["sgu_bias"], sw)
    ys = [y_attn, y_conv, y_sgu]
    w_out = fetch("w_out", y_sgu)
    mix, x1 = _group_rms_matmul(ys, p["out_norm_g"], w_out, x, tm, _tile(d, 1024))
    w_up = fetch("w_up", x1)
    h2, act, r = _rms_matmul(x1, p["ln2_g"], w_up, tm, w_up.shape[2], True, "mlp_up")
    w_down = fetch("w_down", act)
    x2 = _matmul_res(act, w_down, x1, tm, _tile(d, 2048), 512)
    saved = dict(x=x, h1=h1, proj=proj, ys=ys, mix=mix, x1=x1, h2=h2, act=act, r=r,
                 w_in=w_in, w_out=w_out, w_up=w_up, w_down=w_down, probs=probs, sink_p=sink_p, z_conv=z_conv)
    return x2, saved


def _layer_bwd(dx2, dx2b, p, s, start, finish, share, carry):
    t, d = dx2.shape
    aw, kv, cw, sw = d // 2, d // 8, d // 4, d // 4
    tm = _tile(t, 512)
    dup = _mlp_dact(dx2b, s["w_down"], s["r"], 512, carry[0] if carry else None)
    tok = carry[1](dup) if carry else None
    half = functools.partial
    f = s["act"].shape[1]
    tok = start("w_down", half(_grad_w_half, s["act"], dx2b, _tile(f // N_DEV, 512), _tile(d, 2048), "grad_w_down", False), tok)
    dx1, d_ln2 = _mlp_dnorm(dup, s["w_up"], s["x1"], p["ln2_g"], dx2, tm, tok)
    tok = finish("w_down", dx1)
    tok = start("w_up", half(_grad_w_half, s["h2"], dup, _tile(d, 1024), s["w_up"].shape[2], "grad_w_up", True), tok)
    dx1b, dya, dyc, dys, d_onorm = _mix_dnorm(dx1, s["w_out"], s["ys"], p["out_norm_g"], _tile(t, 256), tok)
    tok = finish("w_up", dx1b)
    dsgu, d_sw, d_sb, d_svec = _sgu_bwd(s["proj"], dys, p["sgu_ln_g"], p["sgu_ln_b"], p["sgu_w"], p["sgu_bias"], sw, tok)
    share("sgu_w", d_sw)
    tok = start("w_out", half(_grad_w_half, s["mix"], dx1b, _tile(d // N_DEV, 512), _tile(d, 2048), "grad_w_out", False), dsgu)
    dqkv, d_attn = _attn_bwd(s["proj"], dya, s["probs"], s["sink_p"], p["qg"], p["kg"], p["sinks"], aw, kv, tok)
    tok = finish("w_out", dqkv)
    dconv, d_cw, d_cvec = _conv_bwd(s["proj"], dyc, s["z_conv"], p["conv_w"], p["conv_b"], p["conv_ln_g"], p["conv_ln_b"], cw)
    dps = [dqkv, dconv, dsgu]
    dx, dxb, d_ln1 = _in_dnorm(dps, s["w_in"], s["x"], p["ln1_g"], dx1, _tile(t, 256))
    heads = sw // HEAD
    share("rest", dict(
        ln1_g=d_ln1[0], q_norm_g=d_attn[0, :HEAD], k_norm_g=d_attn[1, :HEAD], sinks=d_attn[2, :aw // HEAD],
        conv_w=d_cw[:CONV_K], conv_b=d_cvec[0], conv_ln_g=d_cvec[1], conv_ln_b=d_cvec[2],
        sgu_ln_g=d_svec[0], sgu_ln_b=d_svec[1], sgu_b=d_sb[:, :heads].T,
        out_norm_g=d_onorm[0], ln2_g=d_ln2[0]))
    tm_in = 512 if all(dp.shape[1] % 512 == 0 for dp in dps) else 256
    tok = start("w_in", _grad_w_stacked(dps, s["h1"], tm_in, "grad_w_in", tok), None)
    carry = (tok, functools.partial(finish, "w_in"))
    return dx, dxb, carry


def _layer_params(l, small):
    row = lambda v: v[l][None, :]
    two = lambda v: jnp.tile(v[l], 2)[None, :]
    return dict(
        ln1_g=row(small["ln1_g"]), ln2_g=row(small["ln2_g"]), out_norm_g=row(small["out_norm_g"]),
        qg=two(small["q_norm_g"]), kg=two(small["k_norm_g"]), sinks=small["sinks"][l],
        conv_w=jnp.pad(small["conv_w"][l], ((0, HALO - CONV_K), (0, 0))),
        conv_b=row(small["conv_b"]), conv_ln_g=row(small["conv_ln_g"]), conv_ln_b=row(small["conv_ln_b"]),
        sgu_ln_g=row(small["sgu_ln_g"]), sgu_ln_b=row(small["sgu_ln_b"]), sgu_w=small["sgu_w"][l],
        sgu_bias=jnp.repeat(small["sgu_b"][l].T, HEAD, axis=1),
    )


def _local_step(x, target, small, depth, fetch, start, finish, share, after):
    params = [_layer_params(l, small) for l in range(depth)]
    saved = []
    h = x
    for l in range(depth):
        h, s = _layer_fwd(h, params[l], functools.partial(fetch, l), after)
        after = h
        saved.append(s)
    dy, dyb, lsum = _loss_grad(h, target, _tile(x.shape[0], 512))
    def share_with_loss(kind, grads):
        share(depth - 1, kind, dict(grads, loss=lsum[0, 0:1]) if kind == "rest" else grads)

    carry = None
    for l in reversed(range(depth)):
        dy, dyb, carry = _layer_bwd(dy, dyb, params[l], saved[l], functools.partial(start, l), functools.partial(finish, l),
                                    share_with_loss if l == depth - 1 else functools.partial(share, l), carry)
    return dy, carry


BIG = ("w_in", "w_out", "w_up", "w_down")
HBM = pl.BlockSpec(memory_space=pltpu.HBM)
SEMS = pl.BlockSpec(memory_space=pltpu.SEMAPHORE)
EFFECT = pltpu.SideEffectType.DATAFLOW_SIDE_EFFECTING


def _mesh_pos():
    return lax.axis_index("x"), lax.axis_index("y"), lax.axis_index("c")


def _other_chips(x, y):
    return [(1 - x, y), (x, 1 - y), (1 - x, 1 - y)]


def _copies(plan, refs, sends, recvs):
    x, y, c = _mesh_pos()
    return [pltpu.make_async_remote_copy(src_ref=src, dst_ref=dst, send_sem=sends.at[k], recv_sem=recvs.at[k],
                                         device_id=dev, device_id_type=MESH)
            for k, (src, dst, dev) in enumerate(plan(x, y, c, refs))]


def _gather_plan(x, y, c, refs):
    mine = refs[0].at[4 * x + 2 * y + c]
    return [(mine, mine, (x, y, 1 - c))] + [(mine, mine, (*chip, c)) for chip in _other_chips(x, y)]


def _all_plan(x, y, c, refs):
    mine = refs[0].at[4 * x + 2 * y + c]
    return [(mine, mine, (x ^ (k >> 2), y ^ ((k >> 1) & 1), c ^ (k & 1))) for k in range(1, N_DEV)]


def _pair_plan(x, y, c, refs):
    blocks, land = refs
    return [(blocks.at[2 * q + (1 - c)], land.at[q], (x, y, 1 - c)) for q in range(4)]


def _half_plan(x, y, c, refs):
    blocks, land = refs
    return [(blocks.at[q], land.at[q], (x, y, 1 - c)) for q in range(4)]


def _chips_plan(x, y, c, refs):
    sums, land = refs
    return [(sums.at[2 * chip[0] + chip[1]], land.at[k], (*chip, c)) for k, chip in enumerate(_other_chips(x, y))]


def _start_exchanges(name, groups, after=None):
    flat = [a for arrays, _, _ in groups for a in arrays]
    n_arr, n_g = len(flat), len(groups)
    n_in = n_arr + (after is not None)

    def body(*refs):
        ins, sems, token = refs[:n_arr], refs[n_in:n_in + 2 * n_g], refs[-1]
        off = 0
        for gi, (arrays, plan, _) in enumerate(groups):
            for cp in _copies(plan, ins[off:off + len(arrays)], sems[2 * gi], sems[2 * gi + 1]):
                cp.start()
            off += len(arrays)
        token[...] = jnp.zeros_like(token)

    res = pl.pallas_call(
        body, name=name,
        out_shape=[pltpu.SemaphoreType.DMA((n,)) for _, _, n in groups for _ in (0, 1)]
        + [pltpu.HBM(a.shape, a.dtype) for a in flat] + [SDS((8, BLK), F32)],
        in_specs=[HBM] * n_arr + [ANY] * (after is not None),
        out_specs=[SEMS] * (2 * n_g) + [HBM] * n_arr + [pl.BlockSpec(memory_space=pltpu.VMEM)],
        input_output_aliases={i: 2 * n_g + i for i in range(n_arr)},
        compiler_params=pltpu.CompilerParams(has_side_effects=EFFECT),
    )(*[pltpu.with_memory_space_constraint(a, pltpu.HBM) for a in flat], *([after] if after is not None else []))
    sems, thru, token = res[:2 * n_g], res[2 * n_g:2 * n_g + n_arr], res[-1]
    out, off = [], 0
    for gi, (arrays, _, _) in enumerate(groups):
        out.append((list(thru[off:off + len(arrays)]), sems[2 * gi], sems[2 * gi + 1]))
        off += len(arrays)
    return out, token


def _wait_exchange(name, arrays, sends, recvs, plan, after):
    n = len(arrays)

    def body(*refs):
        for cp in _copies(plan, refs[:n], refs[n], refs[n + 1]):
            cp.wait_send()
            cp.wait_recv()

    return pl.pallas_call(
        body, name=name,
        out_shape=[pltpu.HBM(a.shape, a.dtype) for a in arrays],
        in_specs=[HBM] * n + [SEMS, SEMS, ANY],
        out_specs=[HBM] * n,
        input_output_aliases={i: i for i in range(n)},
        compiler_params=pltpu.CompilerParams(has_side_effects=EFFECT),
    )(*arrays, sends, recvs, after)


def _wait_exchanges(name, groups, after):
    flat = [a for (arrays, _, _), _ in groups for a in arrays]
    n_arr, n_g = len(flat), len(groups)

    def body(*refs):
        off = 0
        for gi, ((arrays, _, _), plan) in enumerate(groups):
            for cp in _copies(plan, refs[off:off + len(arrays)], refs[n_arr + 2 * gi], refs[n_arr + 2 * gi + 1]):
                cp.wait_send()
                cp.wait_recv()
            off += len(arrays)

    res = pl.pallas_call(
        body, name=name,
        out_shape=[pltpu.HBM(a.shape, a.dtype) for a in flat],
        in_specs=[HBM] * n_arr + [SEMS] * (2 * n_g) + [ANY],
        out_specs=[HBM] * n_arr,
        input_output_aliases={i: i for i in range(n_arr)},
        compiler_params=pltpu.CompilerParams(has_side_effects=EFFECT),
    )(*flat, *[s for (_, sends, recvs), _ in groups for s in (sends, recvs)], after)
    out, off = [], 0
    for (arrays, _, _), _ in groups:
        out.append(list(res[off:off + len(arrays)]))
        off += len(arrays)
    return out


def _gather_finish(land, name):
    def body(land_ref, out_ref, send_sems, recv_sems):
        del land_ref
        x, y, c = _mesh_pos()
        cps = []
        for k, chip in enumerate(_other_chips(x, y)):
            block = out_ref.at[4 * chip[0] + 2 * chip[1] + c]
            cps.append(pltpu.make_async_remote_copy(
                src_ref=block, dst_ref=block, send_sem=send_sems.at[k], recv_sem=recv_sems.at[k],
                device_id=(x, y, 1 - c), device_id_type=MESH))
        for cp in cps:
            cp.start()
        for cp in cps:
            cp.wait()

    return pl.pallas_call(
        body, name=name,
        in_specs=[ANY], out_specs=ANY,
        out_shape=SDS(land.shape, land.dtype),
        input_output_aliases={0: 0},
        scratch_shapes=[pltpu.SemaphoreType.DMA((3,)), pltpu.SemaphoreType.DMA((3,))],
    )(land)


def _pair_add(own, got, c, name):
    _, r, cols = own.shape
    tr = _row_tile(r, 512)

    def body(c_ref, own_ref, got_ref, o_ref):
        o_ref[...] = (own_ref[...].astype(F32) + got_ref[...].astype(F32)).astype(BF16)

    return pl.pallas_call(
        body, name=name,
        grid_spec=pltpu.PrefetchScalarGridSpec(
            num_scalar_prefetch=1, grid=(4, r // tr),
            in_specs=[pl.BlockSpec((None, tr, cols), lambda q, i, c_ref: (2 * q + c_ref[0], i, 0)),
                      pl.BlockSpec((None, tr, cols), lambda q, i, c_ref: (q, i, 0))],
            out_specs=pl.BlockSpec((None, tr, cols), lambda q, i, c_ref: (q, i, 0))),
        out_shape=SDS((4, r, cols), BF16),
        compiler_params=_params("arbitrary", "arbitrary"),
    )(c, own, got)


def _adamw(w, g, m, v):
    m = ADAM_B1 * m + (1.0 - ADAM_B1) * g
    v = ADAM_B2 * v + (1.0 - ADAM_B2) * (g * g)
    m_hat = m / (1.0 - ADAM_B1 ** ADAM_STEP)
    v_hat = v / (1.0 - ADAM_B2 ** ADAM_STEP)
    delta = -ADAM_LR * (m_hat / (jnp.sqrt(v_hat) + ADAM_EPS) + ADAM_WD * w)
    return delta, m, v


def _adamw_layer(chip_sum, got, chip, w, m, v, layer, prev, name, after=None):
    depth, r, cols = w.shape
    tr = _row_tile(r, 256)

    def body(chip_ref, sum_ref, got_ref, w_ref, m_ref, v_ref, *rest):
        g_out, d_out, m_out, v_out, token = rest[-5:]
        g = sum_ref[...].astype(F32) + got_ref[0].astype(F32) + got_ref[1].astype(F32) + got_ref[2].astype(F32)
        delta, mm, vv = _adamw(w_ref[...], g, m_ref[...], v_ref[...])
        g_out[...] = g
        d_out[...] = delta
        m_out[...] = mm
        v_out[...] = vv
        token[...] = jnp.zeros_like(token)

    shard = pl.BlockSpec((None, tr, cols), lambda i, chip_ref: (layer, i, 0))
    in_specs = [pl.BlockSpec((None, tr, cols), lambda i, chip_ref: (chip_ref[0], i, 0)),
                pl.BlockSpec((3, tr, cols), lambda i, chip_ref: (0, i, 0)), shard, shard, shard]
    args = [chip, chip_sum, got, w, m, v]
    aliases = {}
    if prev is not None:
        in_specs += [ANY] * 4
        aliases = {len(args) + k: k for k in range(4)}
        args += list(prev)
    if after is not None:
        in_specs, args = in_specs + [ANY], args + [after]
    res = pl.pallas_call(
        body, name=name,
        grid_spec=pltpu.PrefetchScalarGridSpec(
            num_scalar_prefetch=1, grid=(r // tr,), in_specs=in_specs,
            out_specs=[shard] * 4 + [pl.BlockSpec((8, BLK), lambda i, chip_ref: (0, 0))]),
        out_shape=[SDS(w.shape, F32)] * 4 + [SDS((8, BLK), F32)],
        input_output_aliases=aliases,
        compiler_params=_params("arbitrary"),
    )(*args)
    return list(res[:4]), res[4]


WIDE = ("ln1_g", "out_norm_g", "ln2_g", "conv_b", "conv_ln_g", "conv_ln_b", "sgu_ln_g", "sgu_ln_b")
NARROW = ("q_norm_g", "k_norm_g", "sinks")


def _small_rows(w):
    rows, r = {}, 0
    for n in WIDE:
        rows[n] = (r, w[n].shape[1] // BLK)
        r += rows[n][1]
    for n in NARROW + ("loss",):
        rows[n] = (r, 1)
        r += 1
    r = -(-r // 8) * 8
    rows["sgu_b"] = (r, w["sgu_b"].shape[1])
    r += -(-rows["sgu_b"][1] // 8) * 8
    rows["conv_w"] = (r, N_DEV * HALO)
    return rows, r + N_DEV * HALO


def _pack_small(small, rows, total):
    parts, r = [], 0

    def put(name, block):
        nonlocal r
        first = rows[name][0]
        if first > r:
            parts.append(jnp.zeros((first - r, BLK), F32))
        parts.append(block)
        r = first + block.shape[0]

    for n in WIDE:
        put(n, small[n].reshape(-1, BLK))
    for n in NARROW:
        put(n, jnp.pad(small[n], (0, BLK - small[n].shape[0]))[None])
    if "loss" in small:
        put("loss", jnp.pad(small["loss"], (0, BLK - 1))[None])
    put("sgu_b", small["sgu_b"])
    cw = small["conv_w"]
    per_dev = cw.shape[1] // N_DEV
    blocks = jnp.transpose(cw.reshape(CONV_K, N_DEV, per_dev), (1, 0, 2))
    put("conv_w", jnp.pad(blocks, ((0, 0), (0, HALO - CONV_K), (0, BLK - per_dev))).reshape(N_DEV * HALO, BLK))
    if total > r:
        parts.append(jnp.zeros((total - r, BLK), F32))
    return jnp.concatenate(parts)


def _small_update(me, packed, sgu_w_all, w, m, v, rows):
    depth = len(packed)
    total = packed[0].shape[1]
    names = SMALL + ("conv_w",)
    heads = w["sgu_w"].shape[1]
    per_dev = w["conv_w"].shape[2]

    def body(me_ref, *refs):
        packed_refs, sgu_refs = refs[:depth], refs[depth:2 * depth]
        ins = refs[2 * depth:2 * depth + 3 * len(names)]
        outs = refs[2 * depth + 3 * len(names):2 * depth + 7 * len(names)]
        loss_out, acc, conv = refs[-3:]
        io = {n: (ins[3 * i:3 * i + 3], outs[4 * i:4 * i + 4]) for i, n in enumerate(names)}

        def update(n):
            (w_ref, m_ref, v_ref), (g_out, d_out, m_out, v_out) = io[n]
            delta, mm, vv = _adamw(w_ref[...], g_out[...], m_ref[...], v_ref[...])
            d_out[...] = delta
            m_out[...] = mm
            v_out[...] = vv

        mine = pl.ds(pl.multiple_of(rows["conv_w"][0] + HALO * me_ref[0], 8), HALO)
        for l in range(depth):
            s = packed_refs[l][0]
            c = packed_refs[l][0, mine, :]
            for k in range(1, N_DEV):
                s = s + packed_refs[l][k]
                c = c + packed_refs[l][k, mine, :]
            acc[l] = s
            conv[l] = c
        loss_out[...] = acc[depth - 1, rows["loss"][0]:rows["loss"][0] + 1, :]
        for n in WIDE:
            first, nr = rows[n]
            for l in range(depth):
                for j in range(nr):
                    io[n][1][0][l:l + 1, BLK * j:BLK * (j + 1)] = acc[l, first + j:first + j + 1, :]
            update(n)
        for n in NARROW:
            first, lanes = rows[n][0], w[n].shape[1]
            for l in range(depth):
                io[n][1][0][l:l + 1, :] = acc[l, first:first + 1, 0:lanes]
            update(n)
        first, nr = rows["sgu_b"]
        for l in range(depth):
            io["sgu_b"][1][0][l] = acc[l, first:first + nr, :]
            io["conv_w"][1][0][l] = conv[l, 0:CONV_K, 0:per_dev]
        update("sgu_b")
        update("conv_w")
        (w_ref, m_ref, v_ref), (g_out, d_out, m_out, v_out) = io["sgu_w"]
        for l in range(depth):
            for h in range(heads):
                g = sgu_refs[l][0, h]
                for k in range(1, N_DEV):
                    g = g + sgu_refs[l][k, h]
                delta, mm, vv = _adamw(w_ref[l, h], g, m_ref[l, h], v_ref[l, h])
                g_out[l, h] = g
                d_out[l, h] = delta
                m_out[l, h] = mm
                v_out[l, h] = vv

    def whole(a):
        nd = len(a.shape)
        return pl.BlockSpec(a.shape, lambda i, me_ref: (0,) * nd)

    small_in = [t[n] for n in names for t in (w, m, v)]
    res = pl.pallas_call(
        body, name="small_update",
        grid_spec=pltpu.PrefetchScalarGridSpec(
            num_scalar_prefetch=1, grid=(1,),
            in_specs=[whole(a) for a in list(packed) + list(sgu_w_all) + small_in],
            out_specs=[whole(w[n]) for n in names for _ in range(4)] + [pl.BlockSpec((1, BLK), lambda i, me_ref: (0, 0))],
            scratch_shapes=[pltpu.VMEM((depth, total, BLK), F32), pltpu.VMEM((depth, HALO, BLK), F32)]),
        out_shape=[SDS(w[n].shape, F32) for n in names for _ in range(4)] + [SDS((1, BLK), F32)],
        compiler_params=_params("arbitrary"),
    )(me, *packed, *sgu_w_all, *small_in)
    return {n: list(res[4 * i:4 * i + 4]) for i, n in enumerate(names)}, res[-1][0, 0]


def _pack(arrays):
    flat = jnp.concatenate([a.reshape(-1) for a in arrays])
    rows = -(-flat.shape[0] // (8 * BLK)) * 8
    return jnp.pad(flat, (0, rows * BLK - flat.shape[0])).reshape(rows, BLK)


def _unpack(packed, like):
    flat = packed.reshape(-1)
    out, off = [], 0
    for a in like:
        out.append(flat[off:off + a.size].reshape(a.shape))
        off += a.size
    return out


SMALL = ("ln1_g", "q_norm_g", "k_norm_g", "sinks", "conv_b", "conv_ln_g", "conv_ln_b", "sgu_ln_g", "sgu_ln_b",
         "sgu_w", "sgu_b", "out_norm_g", "ln2_g")
ORDER = ("ln1_g", "w_in", "q_norm_g", "k_norm_g", "sinks", "conv_w", "conv_b", "conv_ln_g", "conv_ln_b", "sgu_ln_g",
         "sgu_ln_b", "sgu_w", "sgu_b", "out_norm_g", "w_out", "ln2_g", "w_up", "w_down")


def _step(x, target, w, m, v):
    depth = w["ln1_g"].shape[0]
    xpos, ypos, cpos = _mesh_pos()
    me = 4 * xpos + 2 * ypos + cpos
    c_arr = jnp.reshape(cpos, (1,)).astype(jnp.int32)
    chip_arr = jnp.reshape(2 * xpos + ypos, (1,)).astype(jnp.int32)

    d = x.shape[1]
    def own_block(shard):
        return lax.dynamic_update_slice_in_dim(lax.empty((N_DEV,) + shard.shape, shard.dtype), shard[None], me, axis=0)

    cw = w["conv_w"]
    order = [(0, "conv_w")] + [(l, n) for l in range(depth) for n in BIG]
    started, gather_token = _start_exchanges("gather_start", [
        ([own_block(_pack([cw]) if n == "conv_w" else w[n][l].astype(BF16))], _gather_plan, 4) for l, n in order])
    pending = dict(zip(order, started))

    def fetch(l, n, after):
        arrays, sends, recvs = pending.pop((l, n))
        land, = _wait_exchange(f"gather_wait_{l}_{n}", arrays, sends, recvs, _gather_plan,
                               gather_token if after is None else after)
        full = _gather_finish(land, "gather_finish_" + n)
        return full if n in ("w_up", "conv_w") else full.reshape(-1, d)

    cw_all = fetch(0, "conv_w", None)
    cw_full = jnp.concatenate([_unpack(cw_all[k], [cw])[0] for k in range(N_DEV)], axis=-1)
    small = {n: w[n] for n in SMALL}
    small["conv_w"] = cw_full

    to_sibling, inflight, shared, leaving = {}, [], {}, []
    rows, total = _small_rows(w)

    def flush(name, first=()):
        groups = list(first) + [g for _, g in leaving]
        started, token = _start_exchanges(name, groups)
        for (record, _), going in zip(leaving, started[len(first):]):
            record(going)
        leaving.clear()
        return started[:len(first)], token

    def start(l, n, grad, after):
        if callable(grad):
            blocks, plan = grad(sel=1 - c_arr, after=after), _half_plan
        else:
            blocks, plan = grad.reshape(N_DEV, -1, d), _pair_plan
        (going,), token = flush(f"grads_start_{l}_{n}", [([blocks, lax.empty((4,) + blocks.shape[1:], BF16)], plan, 4)])
        to_sibling[l, n] = (going, plan, grad)
        return token

    def finish(l, n, after):
        (arrays, sends, recvs), plan, grad = to_sibling.pop((l, n))
        blocks, from_sibling = _wait_exchange(f"pair_wait_{l}_{n}", arrays, sends, recvs, plan, after)
        if callable(grad):
            chip_sums = grad(sel=c_arr, add=from_sibling)
        else:
            chip_sums = _pair_add(blocks, from_sibling, c_arr, "pair_add_" + n)
        leaving.append((lambda going: inflight.append((l, n, going)),
                        ([chip_sums, lax.empty((3,) + chip_sums.shape[1:], BF16)], _chips_plan, 3)))
        return chip_sums

    def share(l, kind, grads):
        block = grads if kind == "sgu_w" else _pack_small(grads, rows, total)
        leaving.append((lambda going: shared.__setitem__((l, kind), going), ([own_block(block)], _all_plan, N_DEV - 1)))

    grad_x, (after, finish_last) = _local_step(x, target, small, depth, fetch, start, finish, share, None)

    as3d = lambda a: a.reshape(depth, -1, a.shape[-1])
    results = {n: None for n in BIG}

    landed = {}

    def await_layer(name, reduces, after):
        smalls = sorted(k for k in shared if k not in landed and all(k[0] >= l for l, _, _ in reduces))
        got = _wait_exchanges(name, [(going, _chips_plan) for _, _, going in reduces]
                              + [(shared[k], _all_plan) for k in smalls], after)
        for k, (arr,) in zip(smalls, got[len(reduces):]):
            landed[k] = arr
        return got[:len(reduces)]

    def update(l, n, chip_sums, got, after):
        results[n], token = _adamw_layer(chip_sums, got, chip_arr, as3d(w[n]), as3d(m[n]), as3d(v[n]), l, results[n],
                                         f"adamw_{l}_{n}", after)
        return token

    first, rest = inflight[:4], inflight[4:]
    arrived = await_layer("grads_wait_first", first, after)
    after = None
    for (l, n, _), (chip_sums, got) in list(zip(first, arrived))[:2]:
        after = update(l, n, chip_sums, got, after)
    finish_last(after)
    _, after = flush("grads_start_last")
    for (l, n, _), (chip_sums, got) in list(zip(first, arrived))[2:]:
        after = update(l, n, chip_sums, got, after)
    arrived = await_layer("grads_wait_rest", rest, after)
    out, loss = _small_update(jnp.reshape(me, (1,)).astype(jnp.int32), [landed[l, "rest"] for l in range(depth)],
                              [landed[l, "sgu_w"] for l in range(depth)], w, m, v, rows)
    after = out["sinks"][1]
    for (l, n, _), (chip_sums, got) in zip(rest, arrived):
        after = update(l, n, chip_sums, got, after)
    (chip_sums, got), = await_layer("grads_wait_last", inflight[-1:], after)
    l, n, _ = inflight[-1]
    update(l, n, chip_sums, got, None)
    out.update({n: [r.reshape(w[n].shape) for r in results[n]] for n in BIG})
    return (loss, grad_x[None]) + tuple(out[n][k] for k in range(4) for n in ORDER)


def kernel(x, ln1_g, w_in, q_norm_g, k_norm_g, sinks, conv_w, conv_b, conv_ln_g, conv_ln_b, sgu_ln_g, sgu_ln_b, sgu_w, sgu_b, out_norm_g, w_out, ln2_g, w_up, w_down, loss_target, m_ln1_g, m_w_in, m_q_norm_g, m_k_norm_g, m_sinks, m_conv_w, m_conv_b, m_conv_ln_g, m_conv_ln_b, m_sgu_ln_g, m_sgu_ln_b, m_sgu_w, m_sgu_b, m_out_norm_g, m_w_out, m_ln2_g, m_w_up, m_w_down, v_ln1_g, v_w_in, v_q_norm_g, v_k_norm_g, v_sinks, v_conv_w, v_conv_b, v_conv_ln_g, v_conv_ln_b, v_sgu_ln_g, v_sgu_ln_b, v_sgu_w, v_sgu_b, v_out_norm_g, v_w_out, v_ln2_g, v_w_up, v_w_down):
    w = dict(zip(ORDER, (ln1_g, w_in, q_norm_g, k_norm_g, sinks, conv_w, conv_b, conv_ln_g, conv_ln_b, sgu_ln_g, sgu_ln_b,
                         sgu_w, sgu_b, out_norm_g, w_out, ln2_g, w_up, w_down)))
    m = dict(zip(ORDER, (m_ln1_g, m_w_in, m_q_norm_g, m_k_norm_g, m_sinks, m_conv_w, m_conv_b, m_conv_ln_g, m_conv_ln_b,
                         m_sgu_ln_g, m_sgu_ln_b, m_sgu_w, m_sgu_b, m_out_norm_g, m_w_out, m_ln2_g, m_w_up, m_w_down)))
    v = dict(zip(ORDER, (v_ln1_g, v_w_in, v_q_norm_g, v_k_norm_g, v_sinks, v_conv_w, v_conv_b, v_conv_ln_g, v_conv_ln_b,
                         v_sgu_ln_g, v_sgu_ln_b, v_sgu_w, v_sgu_b, v_out_norm_g, v_w_out, v_ln2_g, v_w_up, v_w_down)))
    for group in (w, m, v):
        group["w_in"] = jnp.swapaxes(group["w_in"], 1, 2)
    out = list(_step(x[0], loss_target[0], w, m, v))
    for k in range(4):
        i = 2 + k * len(ORDER) + ORDER.index("w_in")
        out[i] = jnp.swapaxes(out[i], 1, 2)
    return tuple(out)
```

```python
import functools

import jax
import jax.numpy as jnp
from jax import lax
from jax.experimental import pallas as pl
from jax.experimental.pallas import tpu as pltpu

F32 = jnp.float32
BF16 = jnp.bfloat16
SDS = jax.ShapeDtypeStruct

EPS = 1e-6
NEG_INF = -1e30
HEAD = 64
BLK = 128
CONV_K = 31
HALO = 32
N_DEV = 8

ADAM_LR = 0.001
ADAM_B1 = 0.9
ADAM_B2 = 0.999
ADAM_EPS = 1e-08
ADAM_WD = 0.01
ADAM_STEP = 10

VMEM_LIMIT = 56 * 1024 * 1024

ROWS_WIDE = 1024
ROWS_F32 = 512
ROWS_NORM = 256
COLS = 512
COLS_WIDE = 2048

MESH = pl.DeviceIdType.MESH


def _params(*sem):
    return pltpu.CompilerParams(dimension_semantics=sem, vmem_limit_bytes=VMEM_LIMIT)


def _nt(a, b):
    return lax.dot_general(a, b, (((1,), (1,)), ((), ())), preferred_element_type=F32)


def _tn(a, b):
    return lax.dot_general(a, b, (((0,), (0,)), ((), ())), preferred_element_type=F32)


def _nn(a, b):
    return jnp.dot(a, b, preferred_element_type=F32)


def _sigmoid(x):
    return 1.0 / (1.0 + jnp.exp(-x))


ANY = pl.BlockSpec(memory_space=pl.ANY)


def _ordered_behind(body, n_in, in_specs, args, after):
    if after is None:
        return body, in_specs, args
    return (lambda *refs: body(*refs[:n_in], *refs[n_in + 1:])), list(in_specs) + [ANY], list(args) + [after]


def _segsum(x, first):
    head0 = jnp.where(first, x, 0.0)
    s0 = jnp.sum(head0, axis=-1, keepdims=True)
    s1 = jnp.sum(x - head0, axis=-1, keepdims=True)
    return jnp.where(first, s0, s1)


def _head_rms(x, gain, first):
    rstd = lax.rsqrt(_segsum(x * x, first) * (1.0 / HEAD) + EPS)
    xhat = x * rstd
    return xhat * gain, xhat, rstd


def _expand(x, odd, lo):
    if odd:
        xl = pltpu.roll(jnp.where(lo, 0.0, x), HEAD, axis=1)
    else:
        xl = jnp.where(lo, x, 0.0)
    xh = pltpu.roll(xl, HEAD, axis=1)
    return jnp.concatenate([xl, xh], axis=0).astype(BF16)


def _fold(g2, odd, lo):
    r = g2.shape[0] // 2
    s = jnp.where(lo, g2[:r], 0.0) + pltpu.roll(jnp.where(lo, 0.0, g2[r:]), HEAD, axis=1)
    if odd:
        s = pltpu.roll(s, HEAD, axis=1)
    return s


def _attn_mask(n):
    qi = lax.broadcasted_iota(jnp.int32, (BLK, 2 * BLK), 0)
    sj = lax.broadcasted_iota(jnp.int32, (BLK, 2 * BLK), 1)
    rel = qi + BLK - sj
    return (rel >= 0) & (rel < BLK) & ((sj >= BLK) | (n > 0))


def _attn_specs(t, aw, kv):
    prev = lambda n: jnp.maximum(n - 1, 0)
    kb, vb = aw // kv, aw // kv + 1
    return [
        pl.BlockSpec(memory_space=pltpu.SMEM),
        pl.BlockSpec((BLK, aw), lambda n: (n, 0)),
        pl.BlockSpec((BLK, kv), lambda n: (prev(n), kb)),
        pl.BlockSpec((BLK, kv), lambda n: (n, kb)),
        pl.BlockSpec((BLK, kv), lambda n: (prev(n), vb)),
        pl.BlockSpec((BLK, kv), lambda n: (n, vb)),
        pl.BlockSpec((1, BLK), lambda n: (0, 0)),
        pl.BlockSpec((1, BLK), lambda n: (0, 0)),
    ]


def _softmax_pair(s2, valid, sink0, sink1):
    out, psink = [], []
    for half, sink in ((0, sink0), (1, sink1)):
        s = jnp.where(valid, s2[:, 2 * BLK * half:2 * BLK * (half + 1)], NEG_INF)
        m = jnp.maximum(jnp.max(s, axis=-1, keepdims=True), sink)
        p = jnp.exp(s - m)
        es = jnp.exp(sink - m)
        inv = 1.0 / (jnp.sum(p, axis=-1, keepdims=True) + es)
        out.append(p * inv)
        psink.append(es * inv)
    return jnp.concatenate(out, axis=1), psink


def _attn_fwd(proj, qg, kg, sinks, aw, kv):
    t = proj.shape[0]
    n_pairs, n_kvblk = aw // BLK, kv // BLK

    def body(sink_ref, q_ref, kp_ref, kc_ref, vp_ref, vc_ref, qg_ref, kg_ref, o_ref, probs_ref, sink_p_ref):
        n = pl.program_id(0)
        lo = lax.broadcasted_iota(jnp.int32, (1, BLK), 1) < HEAD
        valid = _attn_mask(n)
        kraw = jnp.concatenate([kp_ref[...], kc_ref[...]], axis=0)
        vraw = jnp.concatenate([vp_ref[...], vc_ref[...]], axis=0)
        qn = [_head_rms(q_ref[:, BLK * p:BLK * (p + 1)], qg_ref[...], lo)[0].astype(BF16) for p in range(n_pairs)]
        k2, v2 = [], []
        for b in range(n_kvblk):
            kn = _head_rms(kraw[:, BLK * b:BLK * (b + 1)], kg_ref[...], lo)[0]
            for odd in (0, 1):
                k2.append(_expand(kn, odd, lo))
                v2.append(_expand(vraw[:, BLK * b:BLK * (b + 1)], odd, lo))
        s2 = [_nt(qn[p], k2[p // 2]) * (HEAD ** -0.5) for p in range(n_pairs)]
        soft = [_softmax_pair(s2[p], valid, sink_ref[2 * p], sink_ref[2 * p + 1]) for p in range(n_pairs)]
        lane = lax.broadcasted_iota(jnp.int32, (1, BLK), 1)
        sink_p = jnp.zeros((BLK, BLK), F32)
        for p in range(n_pairs):
            probs_ref[:, 4 * BLK * p:4 * BLK * (p + 1)] = soft[p][0]
            for half in (0, 1):
                sink_p = jnp.where(lane == 2 * p + half, soft[p][1][half], sink_p)
            o_ref[:, BLK * p:BLK * (p + 1)] = _nn(soft[p][0].astype(BF16), v2[p // 2])
        sink_p_ref[...] = sink_p

    return pl.pallas_call(
        body, name="attn_fwd", grid=(t // BLK,),
        in_specs=_attn_specs(t, aw, kv),
        out_specs=[pl.BlockSpec((BLK, aw), lambda n: (n, 0)), pl.BlockSpec((BLK, 4 * aw), lambda n: (n, 0)),
                   pl.BlockSpec((BLK, BLK), lambda n: (n, 0))],
        out_shape=[SDS((t, aw), F32), SDS((t, 4 * aw), F32), SDS((t, BLK), F32)],
        compiler_params=_params("arbitrary"),
    )(sinks, proj, proj, proj, proj, proj, qg, kg)


def _attn_bwd(proj, dy, probs, sink_p, qg, kg, sinks, aw, kv, after=None):
    t = proj.shape[0]
    nb = t // BLK
    n_kvblk = kv // BLK
    kb = aw // kv

    def body(sink_ref, q_ref, kp_ref, kc_ref, vp_ref, vc_ref, qg_ref, kg_ref, dy_ref, kall_ref, probs_ref, sink_p_ref,
             dqkv_ref, dstat_ref, dk_acc, dv_acc, dqg_acc):
        del sink_ref
        n = pl.program_id(0)
        lane = lax.broadcasted_iota(jnp.int32, (1, BLK), 1)
        lo = lane < HEAD

        @pl.when(n == 0)
        def _():
            dk_acc[...] = jnp.zeros_like(dk_acc)
            dv_acc[...] = jnp.zeros_like(dv_acc)
            dqg_acc[...] = jnp.zeros_like(dqg_acc)
            dstat_ref[...] = jnp.zeros_like(dstat_ref)

        kraw = jnp.concatenate([kp_ref[...], kc_ref[...]], axis=0)
        vraw = jnp.concatenate([vp_ref[...], vc_ref[...]], axis=0)
        row = pl.multiple_of(n * BLK, BLK)
        prow = pl.multiple_of(jnp.maximum(n - 1, 0) * BLK, BLK)
        pairs = range(aw // BLK)
        cols = [slice(BLK * p, BLK * (p + 1)) for p in pairs]
        qs = [_head_rms(q_ref[:, cols[p]], qg_ref[...], lo) for p in pairs]
        qb = [qs[p][0].astype(BF16) for p in pairs]
        k2, v2 = [], []
        for b in range(n_kvblk):
            kn = _head_rms(kraw[:, BLK * b:BLK * (b + 1)], kg_ref[...], lo)[0]
            for odd in (0, 1):
                k2.append(_expand(kn, odd, lo))
                v2.append(_expand(vraw[:, BLK * b:BLK * (b + 1)], odd, lo))
        p2 = [probs_ref[:, 4 * BLK * p:4 * BLK * (p + 1)] for p in pairs]
        dob = [dy_ref[:, cols[p]].astype(BF16) for p in pairs]
        dp2 = [_nt(dob[p], v2[p // 2]) for p in pairs]
        deltas = jnp.zeros((BLK, BLK), F32)
        ds2 = []
        for p in pairs:
            ds = []
            for half in (0, 1):
                hs = slice(2 * BLK * half, 2 * BLK * (half + 1))
                ph = p2[p][:, hs]
                delta = jnp.sum(ph * dp2[p][:, hs], axis=-1, keepdims=True)
                ds.append(ph * (dp2[p][:, hs] - delta))
                deltas = jnp.where(lane == 2 * p + half, delta, deltas)
            ds2.append((jnp.concatenate(ds, axis=1) * (HEAD ** -0.5)).astype(BF16))
        dstat_ref[2:3, :] -= jnp.sum(sink_p_ref[...] * deltas, axis=0, keepdims=True)
        dqn = [_nn(ds2[p], k2[p // 2]) for p in pairs]
        dk2 = [_tn(ds2[p], qb[p]) for p in pairs]
        dv2 = [_tn(p2[p].astype(BF16), dob[p]) for p in pairs]
        for p in pairs:
            _, qhat, rstd = qs[p]
            dqhat = dqn[p] * qg_ref[...]
            proj_q = _segsum(dqhat * qhat, lo) * (1.0 / HEAD)
            dqkv_ref[pl.ds(row, BLK), cols[p]] = (rstd * (dqhat - qhat * proj_q)).astype(BF16)
            dqg_acc[:, cols[p]] += jnp.sum(dqn[p] * qhat, axis=0, keepdims=True)
        for b in range(n_kvblk):
            dkn = jnp.zeros((2 * BLK, BLK), F32)
            dvb = jnp.zeros((2 * BLK, BLK), F32)
            for odd in (0, 1):
                j = 2 * b + odd
                dkn = dkn + _fold(dk2[2 * j] + dk2[2 * j + 1], odd, lo)
                dvb = dvb + _fold(dv2[2 * j] + dv2[2 * j + 1], odd, lo)
            kcols = slice(BLK * b, BLK * (b + 1))
            dk_acc[pl.ds(prow, BLK), kcols] += dkn[:BLK]
            dv_acc[pl.ds(prow, BLK), kcols] += dvb[:BLK]
            dk_acc[pl.ds(row, BLK), kcols] += dkn[BLK:]
            dv_acc[pl.ds(row, BLK), kcols] += dvb[BLK:]

        @pl.when(n == nb - 1)
        def _():
            dqg = dqg_acc[:, 0:BLK]
            for p in range(1, aw // BLK):
                dqg = dqg + dqg_acc[:, BLK * p:BLK * (p + 1)]
            dstat_ref[0:1, :] = dqg + pltpu.roll(dqg, HEAD, axis=1)

            def kblock(i, dkg):
                r = pl.multiple_of(i * BLK, BLK)
                for b in range(n_kvblk):
                    kcols = slice(BLK * b, BLK * (b + 1))
                    _, khat, rstd = _head_rms(kall_ref[pl.ds(r, BLK), kcols], kg_ref[...], lo)
                    dkn = dk_acc[pl.ds(r, BLK), kcols]
                    dkhat = dkn * kg_ref[...]
                    proj_k = _segsum(dkhat * khat, lo) * (1.0 / HEAD)
                    dqkv_ref[pl.ds(r, BLK), aw + BLK * b:aw + BLK * (b + 1)] = (rstd * (dkhat - khat * proj_k)).astype(BF16)
                    dqkv_ref[pl.ds(r, BLK), aw + kv + BLK * b:aw + kv + BLK * (b + 1)] = dv_acc[pl.ds(r, BLK), kcols].astype(BF16)
                    dkg = dkg + jnp.sum(dkn * khat, axis=0, keepdims=True)
                return dkg

            dkg = lax.fori_loop(0, nb, kblock, jnp.zeros((1, BLK), F32))
            dstat_ref[1:2, :] = dkg + pltpu.roll(dkg, HEAD, axis=1)

    in_specs = _attn_specs(t, aw, kv) + [
        pl.BlockSpec((BLK, aw), lambda n: (n, 0)),
        pl.BlockSpec((t, kv), lambda n: (0, kb)),
        pl.BlockSpec((BLK, 4 * aw), lambda n: (n, 0)),
        pl.BlockSpec((BLK, BLK), lambda n: (n, 0)),
    ]
    args = [sinks, proj, proj, proj, proj, proj, qg, kg, dy, proj, probs, sink_p]
    body, in_specs, args = _ordered_behind(body, len(args), in_specs, args, after)
    return pl.pallas_call(
        body, name="attn_bwd", grid=(nb,),
        in_specs=in_specs,
        out_specs=[pl.BlockSpec((t, aw + 2 * kv), lambda n: (0, 0)), pl.BlockSpec((8, BLK), lambda n: (0, 0))],
        out_shape=[SDS((t, aw + 2 * kv), BF16), SDS((8, BLK), F32)],
        scratch_shapes=[pltpu.VMEM((t, kv), F32), pltpu.VMEM((t, kv), F32), pltpu.VMEM((1, aw), F32)],
        compiler_params=_params("arbitrary"),
    )(*args)


def _conv_taps(win, w_ref, shift_of):
    rows = win.shape[0]
    acc = None
    for j in range(CONV_K):
        term = pltpu.roll(win, (rows - shift_of(j)) % rows, axis=0)[:BLK] * w_ref[j:j + 1, :]
        acc = term if acc is None else acc + term
    return acc


def _layer_norm_fwd(z):
    mu = jnp.mean(z, axis=-1, keepdims=True)
    zc = z - mu
    rstd = lax.rsqrt(jnp.mean(zc * zc, axis=-1, keepdims=True) + EPS)
    return zc * rstd, rstd


def _layer_norm_bwd(dy, yhat, rstd, g):
    dyh = dy * g
    return rstd * (dyh - jnp.mean(dyh, axis=-1, keepdims=True) - yhat * jnp.mean(dyh * yhat, axis=-1, keepdims=True))


def _conv_fill_glu(a_ref, g_ref, hpad, nb):
    hpad[0:HALO, :] = jnp.zeros((HALO, hpad.shape[1]), F32)

    def fill(i, c):
        r = pl.multiple_of(i * BLK, BLK)
        hpad[pl.ds(pl.multiple_of(r + HALO, HALO), BLK), :] = a_ref[pl.ds(r, BLK), :] * _sigmoid(g_ref[pl.ds(r, BLK), :])
        return c

    lax.fori_loop(0, nb, fill, 0)


def _conv_specs(t, cw, d_in):
    base = (d_in - 4 * cw) // cw
    vec = pl.BlockSpec((1, cw), lambda i: (0, 0))
    return [
        pl.BlockSpec((t, cw), lambda i: (0, base)),
        pl.BlockSpec((t, cw), lambda i: (0, base + 1)),
        pl.BlockSpec((HALO, cw), lambda i: (0, 0)),
        vec, vec, vec,
    ]


def _conv_fwd(proj, cw_pad, cb, lg, lb, cw):
    t, d_in = proj.shape
    nb = t // BLK

    def body(a_ref, g_ref, w_ref, b_ref, lg_ref, lb_ref, o_ref, z_ref, hpad):
        _conv_fill_glu(a_ref, g_ref, hpad, nb)

        def blk(i, c):
            r = pl.multiple_of(i * BLK, BLK)
            z = _conv_taps(hpad[pl.ds(r, BLK + HALO), :], w_ref, lambda j: j + HALO - (CONV_K - 1)) + b_ref[...]
            z_ref[pl.ds(r, BLK), :] = z
            yhat, _ = _layer_norm_fwd(z)
            y = yhat * lg_ref[...] + lb_ref[...]
            o_ref[pl.ds(r, BLK), :] = y * _sigmoid(y)
            return c

        lax.fori_loop(0, nb, blk, 0)

    whole = pl.BlockSpec((t, cw), lambda i: (0, 0))
    return pl.pallas_call(
        body, name="conv_fwd", grid=(1,),
        in_specs=_conv_specs(t, cw, d_in),
        out_specs=[whole, whole],
        out_shape=[SDS((t, cw), F32), SDS((t, cw), F32)],
        scratch_shapes=[pltpu.VMEM((t + HALO, cw), F32)],
        compiler_params=_params("arbitrary"),
    )(proj, proj, cw_pad, cb, lg, lb)


def _conv_bwd(proj, dy, z, cw_pad, cb, lg, lb, cw):
    t, d_in = proj.shape
    nb = t // BLK

    def body(a_ref, g_ref, w_ref, b_ref, lg_ref, lb_ref, dy_ref, z_ref, dc_ref, dw_ref, dvec_ref, hpad, dzpad, dwacc):
        del b_ref
        _conv_fill_glu(a_ref, g_ref, hpad, nb)
        dzpad[t:t + HALO, :] = jnp.zeros((HALO, cw), F32)
        dwacc[...] = jnp.zeros_like(dwacc)

        def blk(i, carry):
            db, dlg, dlb = carry
            r = pl.multiple_of(i * BLK, BLK)
            win = hpad[pl.ds(r, BLK + HALO), :]
            yhat, rstd = _layer_norm_fwd(z_ref[pl.ds(r, BLK), :])
            y = yhat * lg_ref[...] + lb_ref[...]
            sg = _sigmoid(y)
            dyl = dy_ref[pl.ds(r, BLK), :] * (sg * (1.0 + y * (1.0 - sg)))
            dz = _layer_norm_bwd(dyl, yhat, rstd, lg_ref[...])
            dzpad[pl.ds(r, BLK), :] = dz
            for j in range(CONV_K):
                sh = j + HALO - (CONV_K - 1)
                prod = dz * pltpu.roll(win, (BLK + HALO - sh) % (BLK + HALO), axis=0)[:BLK]
                dwacc[8 * j:8 * j + 8, :] += jnp.sum(prod.reshape(BLK // 8, 8, cw), axis=0)
            return (db + jnp.sum(dz, axis=0, keepdims=True),
                    dlg + jnp.sum(dyl * yhat, axis=0, keepdims=True),
                    dlb + jnp.sum(dyl, axis=0, keepdims=True))

        zero = jnp.zeros((1, cw), F32)
        db, dlg, dlb = lax.fori_loop(0, nb, blk, (zero, zero, zero))
        dvec_ref[...] = jnp.zeros_like(dvec_ref)
        dvec_ref[0:1, :] = db
        dvec_ref[1:2, :] = dlg
        dvec_ref[2:3, :] = dlb
        dw_ref[...] = jnp.sum(dwacc[...].reshape(HALO, 8, cw), axis=1)

        def blk2(i, c):
            r = pl.multiple_of(i * BLK, BLK)
            dh = _conv_taps(dzpad[pl.ds(r, BLK + HALO), :], w_ref, lambda j: CONV_K - 1 - j)
            a = a_ref[pl.ds(r, BLK), :]
            sg = _sigmoid(g_ref[pl.ds(r, BLK), :])
            dc_ref[pl.ds(r, BLK), 0:cw] = (dh * sg).astype(BF16)
            dc_ref[pl.ds(r, BLK), cw:2 * cw] = (dh * a * sg * (1.0 - sg)).astype(BF16)
            return c

        lax.fori_loop(0, nb, blk2, 0)

    return pl.pallas_call(
        body, name="conv_bwd", grid=(1,),
        in_specs=_conv_specs(t, cw, d_in) + [pl.BlockSpec((t, cw), lambda i: (0, 0))] * 2,
        out_specs=[pl.BlockSpec((t, 2 * cw), lambda i: (0, 0)), pl.BlockSpec((HALO, cw), lambda i: (0, 0)),
                   pl.BlockSpec((8, cw), lambda i: (0, 0))],
        out_shape=[SDS((t, 2 * cw), BF16), SDS((HALO, cw), F32), SDS((8, cw), F32)],
        scratch_shapes=[pltpu.VMEM((t + HALO, cw), F32), pltpu.VMEM((t + HALO, cw), F32), pltpu.VMEM((8 * HALO, cw), F32)],
        compiler_params=_params("arbitrary"),
    )(proj, proj, cw_pad, cb, lg, lb, dy, z)


def _tril_bf16(w):
    r = lax.broadcasted_iota(jnp.int32, (BLK, BLK), 0)
    c = lax.broadcasted_iota(jnp.int32, (BLK, BLK), 1)
    return jnp.where(r >= c, w, 0.0).astype(BF16)


def _sgu_specs(sw, d_in, heads):
    base = (d_in - 2 * sw) // sw
    vec = pl.BlockSpec((1, sw), lambda n: (0, 0))
    return [
        pl.BlockSpec((BLK, sw), lambda n: (n, base)),
        pl.BlockSpec((BLK, sw), lambda n: (n, base + 1)),
        vec, vec,
        pl.BlockSpec((heads, BLK, BLK), lambda n: (0, 0, 0)),
        pl.BlockSpec((BLK, sw), lambda n: (0, 0)),
    ]


def _sgu_mix(w_ref, vnb, heads, sw, transpose):
    head_of = lax.broadcasted_iota(jnp.int32, (1, sw), 1) // HEAD
    s = jnp.zeros((BLK, sw), F32)
    for h in range(heads):
        wt = _tril_bf16(w_ref[h])
        mixed = _tn(wt, vnb) if transpose else _nn(wt, vnb)
        s = jnp.where(head_of == h, mixed, s)
    return s


def _sgu_fwd(proj, lg, lb, w, bias_full, sw):
    t, d_in = proj.shape
    heads = sw // HEAD

    def body(u_ref, v_ref, lg_ref, lb_ref, w_ref, bias_ref, o_ref):
        vhat, _ = _layer_norm_fwd(v_ref[...])
        vn = (vhat * lg_ref[...] + lb_ref[...]).astype(BF16)
        s = _sgu_mix(w_ref, vn, heads, sw, False) + bias_ref[...]
        o_ref[...] = u_ref[...] * s

    return pl.pallas_call(
        body, name="sgu_fwd", grid=(t // BLK,),
        in_specs=_sgu_specs(sw, d_in, heads),
        out_specs=pl.BlockSpec((BLK, sw), lambda n: (n, 0)),
        out_shape=SDS((t, sw), F32),
        compiler_params=_params("arbitrary"),
    )(proj, proj, lg, lb, w, bias_full)


def _sgu_bwd(proj, dy, lg, lb, w, bias_full, sw, after=None):
    t, d_in = proj.shape
    heads = sw // HEAD
    nb = t // BLK

    def body(u_ref, v_ref, lg_ref, lb_ref, w_ref, bias_ref, dy_ref, ds_ref, dw_ref, db_ref, dvec_ref, dbfull):
        n = pl.program_id(0)

        @pl.when(n == 0)
        def _():
            dw_ref[...] = jnp.zeros_like(dw_ref)
            dvec_ref[...] = jnp.zeros_like(dvec_ref)
            dbfull[...] = jnp.zeros_like(dbfull)

        vhat, rstd = _layer_norm_fwd(v_ref[...])
        vn = (vhat * lg_ref[...] + lb_ref[...]).astype(BF16)
        s = _sgu_mix(w_ref, vn, heads, sw, False) + bias_ref[...]
        dy = dy_ref[...]
        ds_ref[:, 0:sw] = (dy * s).astype(BF16)
        dsv = dy * u_ref[...]
        dbfull[...] += dsv
        head_of = lax.broadcasted_iota(jnp.int32, (1, sw), 1) // HEAD
        r = lax.broadcasted_iota(jnp.int32, (BLK, BLK), 0)
        c = lax.broadcasted_iota(jnp.int32, (BLK, BLK), 1)
        dsb = dsv.astype(BF16)
        for h in range(heads):
            dsh = jnp.where(head_of == h, dsv, 0.0).astype(BF16)
            dw_ref[h] += jnp.where(r >= c, _nt(dsh, vn), 0.0)
        dvn = _sgu_mix(w_ref, dsb, heads, sw, True)
        dvec_ref[0:1, :] += jnp.sum(dvn * vhat, axis=0, keepdims=True)
        dvec_ref[1:2, :] += jnp.sum(dvn, axis=0, keepdims=True)
        ds_ref[:, sw:2 * sw] = _layer_norm_bwd(dvn, vhat, rstd, lg_ref[...]).astype(BF16)

        @pl.when(n == nb - 1)
        def _():
            sel = (lax.broadcasted_iota(jnp.int32, (sw, BLK), 0) // HEAD == lax.broadcasted_iota(jnp.int32, (sw, BLK), 1)).astype(BF16)
            x = dbfull[...]
            hi = x.astype(BF16)
            r1 = x - hi.astype(F32)
            mid = r1.astype(BF16)
            low = (r1 - mid.astype(F32)).astype(BF16)
            db_ref[...] = _nn(hi, sel) + _nn(mid, sel) + _nn(low, sel)

    in_specs = _sgu_specs(sw, d_in, heads) + [pl.BlockSpec((BLK, sw), lambda n: (n, 0))]
    body, in_specs, args = _ordered_behind(body, 7, in_specs, [proj, proj, lg, lb, w, bias_full, dy], after)
    return pl.pallas_call(
        body, name="sgu_bwd", grid=(nb,),
        in_specs=in_specs,
        out_specs=[pl.BlockSpec((BLK, 2 * sw), lambda n: (n, 0)), pl.BlockSpec((heads, BLK, BLK), lambda n: (0, 0, 0)),
                   pl.BlockSpec((BLK, BLK), lambda n: (0, 0)), pl.BlockSpec((8, sw), lambda n: (0, 0))],
        out_shape=[SDS((t, 2 * sw), BF16), SDS((heads, BLK, BLK), F32), SDS((BLK, BLK), F32), SDS((8, sw), F32)],
        scratch_shapes=[pltpu.VMEM((BLK, sw), F32)],
        compiler_params=_params("arbitrary"),
    )(*args)


def _rms_fwd(x, g):
    return (x * lax.rsqrt(jnp.mean(x * x, axis=-1, keepdims=True) + EPS)) * g


def _rms_bwd(dh, x, g):
    rstd = lax.rsqrt(jnp.mean(x * x, axis=-1, keepdims=True) + EPS)
    xhat = x * rstd
    dgx = dh * g
    dx = rstd * (dgx - xhat * jnp.mean(dgx * xhat, axis=-1, keepdims=True))
    return dx, jnp.sum(dh * xhat, axis=0, keepdims=True)


def _rms_matmul(x, g, w, tm, tn, relu2, name, transposed=False):
    t, d = x.shape
    if w.ndim == 3:
        assert w.shape[2] == tn
        n = w.shape[0] * tn
        w_spec = pl.BlockSpec((None, d, tn), lambda i, j: (j, 0, 0))
    elif transposed:
        n = w.shape[0]
        w_spec = pl.BlockSpec((tn, d), lambda i, j: (j, 0))
    else:
        n = w.shape[1]
        w_spec = pl.BlockSpec((d, tn), lambda i, j: (0, j))

    def body(x_ref, g_ref, w_ref, h_ref, *outs):
        @pl.when(pl.program_id(1) == 0)
        def _():
            h_ref[...] = _rms_fwd(x_ref[...], g_ref[...]).astype(BF16)

        acc = _nt(h_ref[...], w_ref[...]) if transposed else _nn(h_ref[...], w_ref[...])
        if relu2:
            r = jnp.maximum(acc, 0.0)
            outs[0][...] = (r * r).astype(BF16)
            outs[1][...] = r.astype(BF16)
        else:
            outs[0][...] = acc

    tile = pl.BlockSpec((tm, tn), lambda i, j: (i, j))
    row = pl.BlockSpec((tm, d), lambda i, j: (i, 0))
    outs = [SDS((t, n), BF16), SDS((t, n), BF16)] if relu2 else [SDS((t, n), F32)]
    return pl.pallas_call(
        body, name=name, grid=(t // tm, n // tn),
        in_specs=[row, pl.BlockSpec((1, d), lambda i, j: (0, 0)), w_spec],
        out_specs=[row] + [tile] * len(outs),
        out_shape=[SDS((t, d), BF16)] + outs,
        compiler_params=_params("parallel", "arbitrary"),
    )(x, g, w)


def _group_rms_matmul(ys, g, w, res, tm, tn):
    t, d = res.shape
    widths = [y.shape[1] for y in ys]
    k = sum(widths)

    def body(*refs):
        y_refs, (g_ref, w_ref, res_ref, mix_ref, o_ref) = refs[:len(ys)], refs[len(ys):]

        @pl.when(pl.program_id(1) == 0)
        def _():
            c = 0
            for y_ref, wd in zip(y_refs, widths):
                mix_ref[:, c:c + wd] = _rms_fwd(y_ref[...], g_ref[:, c:c + wd]).astype(BF16)
                c += wd

        o_ref[...] = res_ref[...] + _nn(mix_ref[...], w_ref[...])

    tile = pl.BlockSpec((tm, tn), lambda i, j: (i, j))
    return pl.pallas_call(
        body, name="mix_out", grid=(t // tm, d // tn),
        in_specs=[pl.BlockSpec((tm, wd), lambda i, j: (i, 0)) for wd in widths] + [
            pl.BlockSpec((1, k), lambda i, j: (0, 0)), pl.BlockSpec((k, tn), lambda i, j: (0, j)), tile],
        out_specs=[pl.BlockSpec((tm, k), lambda i, j: (i, 0)), tile],
        out_shape=[SDS((t, k), BF16), SDS((t, d), F32)],
        compiler_params=_params("parallel", "arbitrary"),
    )(*ys, g, w, res)


def _matmul_res(a, w, res, tm, tn, tk):
    t, k = a.shape
    n = w.shape[1]
    nk = k // tk

    def body(a_ref, w_ref, res_ref, o_ref, acc):
        kk = pl.program_id(2)

        @pl.when(kk == 0)
        def _():
            acc[...] = res_ref[...]

        acc[...] += _nn(a_ref[...], w_ref[...])

        @pl.when(kk == nk - 1)
        def _():
            o_ref[...] = acc[...]

    tile = pl.BlockSpec((tm, tn), lambda i, j, kk: (i, j))
    return pl.pallas_call(
        body, name="mlp_down", grid=(t // tm, n // tn, nk),
        in_specs=[pl.BlockSpec((tm, tk), lambda i, j, kk: (i, kk)), pl.BlockSpec((tk, tn), lambda i, j, kk: (kk, j)), tile],
        out_specs=tile,
        out_shape=SDS((t, n), F32),
        scratch_shapes=[pltpu.VMEM((tm, tn), F32)],
        compiler_params=_params("parallel", "parallel", "arbitrary"),
    )(a, w, res)


def _loss_grad(y, target, tm):
    t, d = y.shape

    def body(y_ref, t_ref, dy_ref, dyb_ref, l_ref):
        @pl.when(pl.program_id(0) == 0)
        def _():
            l_ref[...] = jnp.zeros_like(l_ref)

        err = y_ref[...] - t_ref[...]
        dy = err * (1.0 / d)
        dy_ref[...] = dy
        dyb_ref[...] = dy.astype(BF16)
        per_row = jnp.mean(err * err, axis=-1, keepdims=True)
        l_ref[...] += jnp.sum(per_row, axis=0, keepdims=True) * 0.5

    row = pl.BlockSpec((tm, d), lambda i: (i, 0))
    return pl.pallas_call(
        body, name="loss_grad", grid=(t // tm,),
        in_specs=[row, row], out_specs=[row, row, pl.BlockSpec((8, BLK), lambda i: (0, 0))],
        out_shape=[SDS((t, d), F32), SDS((t, d), BF16), SDS((8, BLK), F32)],
        compiler_params=_params("arbitrary"),
    )(y, target)


def _mlp_dact(dxb, w_down, r, tn, after=None):
    t, d = dxb.shape
    f = w_down.shape[0]

    def body(dxb_ref, w_ref, r_ref, dup_ref):
        dup_ref[...] = (_nt(dxb_ref[...], w_ref[...]) * (2.0 * r_ref[...].astype(F32))).astype(BF16)

    tile = pl.BlockSpec((t, tn), lambda j: (0, j))
    body, in_specs, args = _ordered_behind(
        body, 3, [pl.BlockSpec((t, d), lambda j: (0, 0)), pl.BlockSpec((tn, d), lambda j: (j, 0)), tile],
        [dxb, w_down, r], after)
    return pl.pallas_call(
        body, name="mlp_dact", grid=(f // tn,),
        in_specs=in_specs, out_specs=tile,
        out_shape=SDS((t, f), BF16),
        compiler_params=_params("arbitrary"),
    )(*args)


def _grad_w_stacked(pieces, b, tm, name, after=None):
    t, n = b.shape
    tiles = [p.shape[1] // tm for p in pieces]
    first = [sum(tiles[:k]) for k in range(len(pieces))]

    def body(*refs):
        b_ref, o_ref = refs[len(pieces)], refs[-1]
        i = pl.program_id(0)
        for k, p_ref in enumerate(refs[:len(pieces)]):
            @pl.when((i >= first[k]) & (i < first[k] + tiles[k]))
            def _(p_ref=p_ref):
                o_ref[...] = _tn(p_ref[...], b_ref[...]).astype(BF16)

    in_specs = [pl.BlockSpec((t, tm), lambda i, k=k: (0, jnp.clip(i - first[k], 0, tiles[k] - 1))) for k in range(len(pieces))]
    in_specs.append(pl.BlockSpec((t, n), lambda i: (0, 0)))
    body, in_specs, args = _ordered_behind(body, len(pieces) + 1, in_specs, [*pieces, b], after)
    return pl.pallas_call(
        body, name=name, grid=(sum(tiles),),
        in_specs=in_specs, out_specs=pl.BlockSpec((tm, n), lambda i: (i, 0)),
        out_shape=SDS((sum(tiles) * tm, n), BF16),
        compiler_params=_params("arbitrary"),
    )(*args)


def _grad_w_half(a, b, tm, tn, name, by_cols, sel, add=None, after=None):
    t, m = a.shape
    n = b.shape[1]
    if by_cols:
        per = n // N_DEV // tn
        grid, shape = (4, m // tm, per), (4, m, n // N_DEV)
        a_spec = pl.BlockSpec((t, tm), lambda q, i, j, s: (0, i))
        b_spec = pl.BlockSpec((t, tn), lambda q, i, j, s: (0, (2 * q + s[0]) * per + j))
    else:
        per = m // N_DEV // tm
        grid, shape = (4, per, n // tn), (4, m // N_DEV, n)
        a_spec = pl.BlockSpec((t, tm), lambda q, i, j, s: (0, (2 * q + s[0]) * per + i))
        b_spec = pl.BlockSpec((t, tn), lambda q, i, j, s: (0, j))
    assert per >= 1
    tile = pl.BlockSpec((None, tm, tn), lambda q, i, j, s: (q, i, j))

    def body(sel_ref, a_ref, b_ref, *rest):
        acc = _tn(a_ref[...], b_ref[...])
        if add is not None:
            acc = acc + rest[0][...].astype(F32)
        rest[-1][...] = acc.astype(BF16)

    in_specs, args = [a_spec, b_spec], [a, b]
    if add is not None:
        in_specs, args = in_specs + [tile], args + [add]
    if after is not None:
        in_specs, args = in_specs + [ANY], args + [after]
    return pl.pallas_call(
        body, name=name,
        grid_spec=pltpu.PrefetchScalarGridSpec(num_scalar_prefetch=1, grid=grid, in_specs=in_specs, out_specs=tile),
        out_shape=SDS(shape, BF16),
        compiler_params=_params("arbitrary", "arbitrary", "arbitrary"),
    )(sel, *args)


def _mlp_dnorm(dup, w_up, x, g, dres, tm, after=None):
    t, f = dup.shape
    d = x.shape[1]
    nk, _, tk = w_up.shape

    def body(a_ref, w_ref, x_ref, g_ref, dres_ref, dx_ref, dg_ref, acc):
        i, kk = pl.program_id(0), pl.program_id(1)

        @pl.when((i == 0) & (kk == 0))
        def _():
            dg_ref[...] = jnp.zeros_like(dg_ref)

        @pl.when(kk == 0)
        def _():
            acc[...] = jnp.zeros_like(acc)

        acc[...] += _nt(a_ref[...], w_ref[...])

        @pl.when(kk == nk - 1)
        def _():
            dx, dg = _rms_bwd(acc[...], x_ref[...], g_ref[...])
            dx_ref[...] = dres_ref[...] + dx
            dg_ref[0:1, :] += dg

    row = pl.BlockSpec((tm, d), lambda i, kk: (i, 0))
    in_specs = [pl.BlockSpec((tm, tk), lambda i, kk: (i, kk)), pl.BlockSpec((None, d, tk), lambda i, kk: (kk, 0, 0)),
                row, pl.BlockSpec((1, d), lambda i, kk: (0, 0)), row]
    body, in_specs, args = _ordered_behind(body, 5, in_specs, [dup, w_up, x, g, dres], after)
    return pl.pallas_call(
        body, name="mlp_dnorm", grid=(t // tm, nk),
        in_specs=in_specs,
        out_specs=[row, pl.BlockSpec((8, d), lambda i, kk: (0, 0))],
        out_shape=[SDS((t, d), F32), SDS((8, d), F32)],
        scratch_shapes=[pltpu.VMEM((tm, d), F32)],
        compiler_params=_params("arbitrary", "arbitrary"),
    )(*args)


def _mix_dnorm(dx, w_out, ys, g, tm, after=None):
    t, d = dx.shape
    k = w_out.shape[0]
    widths = [y.shape[1] for y in ys]

    def body(dx_ref, w_ref, *refs):
        y_refs = refs[:len(ys)]
        g_ref, dxb_ref = refs[len(ys)], refs[len(ys) + 1]
        dy_refs = refs[len(ys) + 2:2 * len(ys) + 2]
        dg_ref = refs[-1]

        @pl.when(pl.program_id(0) == 0)
        def _():
            dg_ref[...] = jnp.zeros_like(dg_ref)

        dxb = dx_ref[...].astype(BF16)
        dxb_ref[...] = dxb
        dmix = _nt(dxb, w_ref[...])
        c = 0
        for y_ref, dy_ref, wd in zip(y_refs, dy_refs, widths):
            dy, dg = _rms_bwd(dmix[:, c:c + wd], y_ref[...], g_ref[:, c:c + wd])
            dy_ref[...] = dy
            dg_ref[0:1, c:c + wd] += dg
            c += wd

    row = pl.BlockSpec((tm, d), lambda i: (i, 0))
    yspecs = [pl.BlockSpec((tm, wd), lambda i: (i, 0)) for wd in widths]
    in_specs = [row, pl.BlockSpec((k, d), lambda i: (0, 0))] + yspecs + [pl.BlockSpec((1, k), lambda i: (0, 0))]
    body, in_specs, args = _ordered_behind(body, len(in_specs), in_specs, [dx, w_out, *ys, g], after)
    return pl.pallas_call(
        body, name="mix_dnorm", grid=(t // tm,),
        in_specs=in_specs,
        out_specs=[row] + yspecs + [pl.BlockSpec((8, k), lambda i: (0, 0))],
        out_shape=[SDS((t, d), BF16)] + [SDS((t, wd), F32) for wd in widths] + [SDS((8, k), F32)],
        compiler_params=_params("arbitrary"),
    )(*args)


def _in_dnorm(dps, w_in_t, x, g, dres, tm):
    t, d = x.shape
    widths = [p.shape[1] for p in dps]
    offs = [sum(widths[:p]) for p in range(len(dps))]
    n_in = w_in_t.shape[0]

    def body(*refs):
        p_refs = refs[:len(dps)]
        w_ref, x_ref, g_ref, dres_ref, dx_ref, dxb_ref, dg_ref = refs[len(dps):]

        @pl.when(pl.program_id(0) == 0)
        def _():
            dg_ref[...] = jnp.zeros_like(dg_ref)

        acc = None
        for p_ref, off, wd in zip(p_refs, offs, widths):
            term = _nn(p_ref[...], w_ref[off:off + wd, :])
            acc = term if acc is None else acc + term
        dx, dg = _rms_bwd(acc, x_ref[...], g_ref[...])
        dx = dres_ref[...] + dx
        dx_ref[...] = dx
        dxb_ref[...] = dx.astype(BF16)
        dg_ref[0:1, :] += dg

    row = pl.BlockSpec((tm, d), lambda i: (i, 0))
    return pl.pallas_call(
        body, name="in_dnorm", grid=(t // tm,),
        in_specs=[pl.BlockSpec((tm, wd), lambda i: (i, 0)) for wd in widths] + [
            pl.BlockSpec((n_in, d), lambda i: (0, 0)), row, pl.BlockSpec((1, d), lambda i: (0, 0)), row],
        out_specs=[row, row, pl.BlockSpec((8, d), lambda i: (0, 0))],
        out_shape=[SDS((t, d), F32), SDS((t, d), BF16), SDS((8, d), F32)],
        compiler_params=_params("arbitrary"),
    )(*dps, w_in_t, x, g, dres)


def _tile(n, want):
    return min(n, want)


def _row_tile(n, want):
    return max(k for k in range(8, min(n, want) + 1, 8) if n % k == 0)


def _layer_fwd(x, p, fetch, after):
    t, d = x.shape
    aw, kv, cw, sw = d // 2, d // 8, d // 4, d // 4
    tm = _tile(t, ROWS_WIDE)
    w_in = fetch("w_in", after)
    h1, proj = _rms_matmul(x, p["ln1_g"], w_in, tm, COLS if w_in.shape[0] % COLS == 0 else COLS // 2, False, "in_proj",
                           transposed=True)
    y_attn, probs, sink_p = _attn_fwd(proj, p["qg"], p["kg"], p["sinks"], aw, kv)
    y_conv, z_conv = _conv_fwd(proj, p["conv_w"], p["conv_b"], p["conv_ln_g"], p["conv_ln_b"], cw)
    y_sgu = _sgu_fwd(proj, p["sgu_ln_g"], p["sgu_ln_b"], p["sgu_w"], p["sgu_bias"], sw)
    ys = [y_attn, y_conv, y_sgu]
    w_out = fetch("w_out", y_sgu)
    mix, x1 = _group_rms_matmul(ys, p["out_norm_g"], w_out, x, tm, _tile(d, COLS_WIDE // 2))
    w_up = fetch("w_up", x1)
    h2, act, r = _rms_matmul(x1, p["ln2_g"], w_up, tm, w_up.shape[2], True, "mlp_up")
    w_down = fetch("w_down", act)
    x2 = _matmul_res(act, w_down, x1, tm, _tile(d, COLS_WIDE), COLS)
    saved = dict(x=x, h1=h1, proj=proj, ys=ys, mix=mix, x1=x1, h2=h2, act=act, r=r,
                 w_in=w_in, w_out=w_out, w_up=w_up, w_down=w_down, probs=probs, sink_p=sink_p, z_conv=z_conv)
    return x2, saved


def _layer_bwd(dx2, dx2b, p, s, start, finish, share, carry):
    t, d = dx2.shape
    aw, kv, cw, sw = d // 2, d // 8, d // 4, d // 4
    tm = _tile(t, ROWS_F32)
    dup = _mlp_dact(dx2b, s["w_down"], s["r"], COLS, carry[0] if carry else None)
    tok = carry[1](dup) if carry else None
    half = functools.partial
    f = s["act"].shape[1]
    tok = start("w_down", half(_grad_w_half, s["act"], dx2b, _tile(f // N_DEV, COLS), _tile(d, COLS_WIDE), "grad_w_down", False), tok)
    dx1, d_ln2 = _mlp_dnorm(dup, s["w_up"], s["x1"], p["ln2_g"], dx2, tm, tok)
    tok = finish("w_down", dx1)
    tok = start("w_up", half(_grad_w_half, s["h2"], dup, _tile(d, ROWS_WIDE), s["w_up"].shape[2], "grad_w_up", True), tok)
    dx1b, dya, dyc, dys, d_onorm = _mix_dnorm(dx1, s["w_out"], s["ys"], p["out_norm_g"], _tile(t, ROWS_NORM), tok)
    tok = finish("w_up", dx1b)
    dsgu, d_sw, d_sb, d_svec = _sgu_bwd(s["proj"], dys, p["sgu_ln_g"], p["sgu_ln_b"], p["sgu_w"], p["sgu_bias"], sw, tok)
    share("sgu_w", d_sw)
    tok = start("w_out", half(_grad_w_half, s["mix"], dx1b, _tile(d // N_DEV, COLS), _tile(d, COLS_WIDE), "grad_w_out", False), dsgu)
    dqkv, d_attn = _attn_bwd(s["proj"], dya, s["probs"], s["sink_p"], p["qg"], p["kg"], p["sinks"], aw, kv, tok)
    tok = finish("w_out", dqkv)
    dconv, d_cw, d_cvec = _conv_bwd(s["proj"], dyc, s["z_conv"], p["conv_w"], p["conv_b"], p["conv_ln_g"], p["conv_ln_b"], cw)
    dps = [dqkv, dconv, dsgu]
    dx, dxb, d_ln1 = _in_dnorm(dps, s["w_in"], s["x"], p["ln1_g"], dx1, _tile(t, ROWS_NORM))
    heads = sw // HEAD
    share("rest", dict(
        ln1_g=d_ln1[0], q_norm_g=d_attn[0, :HEAD], k_norm_g=d_attn[1, :HEAD], sinks=d_attn[2, :aw // HEAD],
        conv_w=d_cw[:CONV_K], conv_b=d_cvec[0], conv_ln_g=d_cvec[1], conv_ln_b=d_cvec[2],
        sgu_ln_g=d_svec[0], sgu_ln_b=d_svec[1], sgu_b=d_sb[:, :heads].T,
        out_norm_g=d_onorm[0], ln2_g=d_ln2[0]))
    tm_in = COLS if all(dp.shape[1] % COLS == 0 for dp in dps) else COLS // 2
    tok = start("w_in", _grad_w_stacked(dps, s["h1"], tm_in, "grad_w_in", tok), None)
    carry = (tok, functools.partial(finish, "w_in"))
    return dx, dxb, carry


def _layer_params(l, small):
    row = lambda v: v[l][None, :]
    two = lambda v: jnp.tile(v[l], 2)[None, :]
    return dict(
        ln1_g=row(small["ln1_g"]), ln2_g=row(small["ln2_g"]), out_norm_g=row(small["out_norm_g"]),
        qg=two(small["q_norm_g"]), kg=two(small["k_norm_g"]), sinks=small["sinks"][l],
        conv_w=jnp.pad(small["conv_w"][l], ((0, HALO - CONV_K), (0, 0))),
        conv_b=row(small["conv_b"]), conv_ln_g=row(small["conv_ln_g"]), conv_ln_b=row(small["conv_ln_b"]),
        sgu_ln_g=row(small["sgu_ln_g"]), sgu_ln_b=row(small["sgu_ln_b"]), sgu_w=small["sgu_w"][l],
        sgu_bias=jnp.repeat(small["sgu_b"][l].T, HEAD, axis=1),
    )


def _local_step(x, target, small, depth, fetch, start, finish, share, after):
    params = [_layer_params(l, small) for l in range(depth)]
    saved = []
    h = x
    for l in range(depth):
        h, s = _layer_fwd(h, params[l], functools.partial(fetch, l), after)
        after = h
        saved.append(s)
    dy, dyb, lsum = _loss_grad(h, target, _tile(x.shape[0], ROWS_F32))
    def share_with_loss(kind, grads):
        share(depth - 1, kind, dict(grads, loss=lsum[0, 0:1]) if kind == "rest" else grads)

    carry = None
    for l in reversed(range(depth)):
        dy, dyb, carry = _layer_bwd(dy, dyb, params[l], saved[l], functools.partial(start, l), functools.partial(finish, l),
                                    share_with_loss if l == depth - 1 else functools.partial(share, l), carry)
    return dy, carry


BIG = ("w_in", "w_out", "w_up", "w_down")
HBM = pl.BlockSpec(memory_space=pltpu.HBM)
SEMS = pl.BlockSpec(memory_space=pltpu.SEMAPHORE)
EFFECT = pltpu.SideEffectType.DATAFLOW_SIDE_EFFECTING


def _mesh_pos():
    return lax.axis_index("x"), lax.axis_index("y"), lax.axis_index("c")


def _other_chips(x, y):
    return [(1 - x, y), (x, 1 - y), (1 - x, 1 - y)]


def _copies(plan, refs, sends, recvs):
    x, y, c = _mesh_pos()
    return [pltpu.make_async_remote_copy(src_ref=src, dst_ref=dst, send_sem=sends.at[k], recv_sem=recvs.at[k],
                                         device_id=dev, device_id_type=MESH)
            for k, (src, dst, dev) in enumerate(plan(x, y, c, refs))]


def _gather_plan(x, y, c, refs):
    mine = refs[0].at[4 * x + 2 * y + c]
    return [(mine, mine, (x, y, 1 - c))] + [(mine, mine, (*chip, c)) for chip in _other_chips(x, y)]


def _all_plan(x, y, c, refs):
    mine = refs[0].at[4 * x + 2 * y + c]
    return [(mine, mine, (x ^ (k >> 2), y ^ ((k >> 1) & 1), c ^ (k & 1))) for k in range(1, N_DEV)]


def _pair_plan(x, y, c, refs):
    blocks, land = refs
    return [(blocks.at[2 * q + (1 - c)], land.at[q], (x, y, 1 - c)) for q in range(4)]


def _half_plan(x, y, c, refs):
    blocks, land = refs
    return [(blocks.at[q], land.at[q], (x, y, 1 - c)) for q in range(4)]


def _chips_plan(x, y, c, refs):
    sums, land = refs
    return [(sums.at[2 * chip[0] + chip[1]], land.at[k], (*chip, c)) for k, chip in enumerate(_other_chips(x, y))]


def _start_exchanges(name, groups, after=None):
    flat = [a for arrays, _, _ in groups for a in arrays]
    n_arr, n_g = len(flat), len(groups)
    n_in = n_arr + (after is not None)

    def body(*refs):
        ins, sems, token = refs[:n_arr], refs[n_in:n_in + 2 * n_g], refs[-1]
        off = 0
        for gi, (arrays, plan, _) in enumerate(groups):
            for cp in _copies(plan, ins[off:off + len(arrays)], sems[2 * gi], sems[2 * gi + 1]):
                cp.start()
            off += len(arrays)
        token[...] = jnp.zeros_like(token)

    res = pl.pallas_call(
        body, name=name,
        out_shape=[pltpu.SemaphoreType.DMA((n,)) for _, _, n in groups for _ in (0, 1)]
        + [pltpu.HBM(a.shape, a.dtype) for a in flat] + [SDS((8, BLK), F32)],
        in_specs=[HBM] * n_arr + [ANY] * (after is not None),
        out_specs=[SEMS] * (2 * n_g) + [HBM] * n_arr + [pl.BlockSpec(memory_space=pltpu.VMEM)],
        input_output_aliases={i: 2 * n_g + i for i in range(n_arr)},
        compiler_params=pltpu.CompilerParams(has_side_effects=EFFECT),
    )(*[pltpu.with_memory_space_constraint(a, pltpu.HBM) for a in flat], *([after] if after is not None else []))
    sems, thru, token = res[:2 * n_g], res[2 * n_g:2 * n_g + n_arr], res[-1]
    out, off = [], 0
    for gi, (arrays, _, _) in enumerate(groups):
        out.append((list(thru[off:off + len(arrays)]), sems[2 * gi], sems[2 * gi + 1]))
        off += len(arrays)
    return out, token


def _wait_exchange(name, arrays, sends, recvs, plan, after):
    n = len(arrays)

    def body(*refs):
        for cp in _copies(plan, refs[:n], refs[n], refs[n + 1]):
            cp.wait_send()
            cp.wait_recv()

    return pl.pallas_call(
        body, name=name,
        out_shape=[pltpu.HBM(a.shape, a.dtype) for a in arrays],
        in_specs=[HBM] * n + [SEMS, SEMS, ANY],
        out_specs=[HBM] * n,
        input_output_aliases={i: i for i in range(n)},
        compiler_params=pltpu.CompilerParams(has_side_effects=EFFECT),
    )(*arrays, sends, recvs, after)


def _wait_exchanges(name, groups, after):
    flat = [a for (arrays, _, _), _ in groups for a in arrays]
    n_arr, n_g = len(flat), len(groups)

    def body(*refs):
        off = 0
        for gi, ((arrays, _, _), plan) in enumerate(groups):
            for cp in _copies(plan, refs[off:off + len(arrays)], refs[n_arr + 2 * gi], refs[n_arr + 2 * gi + 1]):
                cp.wait_send()
                cp.wait_recv()
            off += len(arrays)

    res = pl.pallas_call(
        body, name=name,
        out_shape=[pltpu.HBM(a.shape, a.dtype) for a in flat],
        in_specs=[HBM] * n_arr + [SEMS] * (2 * n_g) + [ANY],
        out_specs=[HBM] * n_arr,
        input_output_aliases={i: i for i in range(n_arr)},
        compiler_params=pltpu.CompilerParams(has_side_effects=EFFECT),
    )(*flat, *[s for (_, sends, recvs), _ in groups for s in (sends, recvs)], after)
    out, off = [], 0
    for (arrays, _, _), _ in groups:
        out.append(list(res[off:off + len(arrays)]))
        off += len(arrays)
    return out


def _gather_finish(land, name):
    def body(land_ref, out_ref, send_sems, recv_sems):
        del land_ref
        x, y, c = _mesh_pos()
        cps = []
        for k, chip in enumerate(_other_chips(x, y)):
            block = out_ref.at[4 * chip[0] + 2 * chip[1] + c]
            cps.append(pltpu.make_async_remote_copy(
                src_ref=block, dst_ref=block, send_sem=send_sems.at[k], recv_sem=recv_sems.at[k],
                device_id=(x, y, 1 - c), device_id_type=MESH))
        for cp in cps:
            cp.start()
        for cp in cps:
            cp.wait()

    return pl.pallas_call(
        body, name=name,
        in_specs=[ANY], out_specs=ANY,
        out_shape=SDS(land.shape, land.dtype),
        input_output_aliases={0: 0},
        scratch_shapes=[pltpu.SemaphoreType.DMA((3,)), pltpu.SemaphoreType.DMA((3,))],
    )(land)


def _pair_add(own, got, c, name):
    _, r, cols = own.shape
    tr = _row_tile(r, ROWS_F32)

    def body(c_ref, own_ref, got_ref, o_ref):
        o_ref[...] = (own_ref[...].astype(F32) + got_ref[...].astype(F32)).astype(BF16)

    return pl.pallas_call(
        body, name=name,
        grid_spec=pltpu.PrefetchScalarGridSpec(
            num_scalar_prefetch=1, grid=(4, r // tr),
            in_specs=[pl.BlockSpec((None, tr, cols), lambda q, i, c_ref: (2 * q + c_ref[0], i, 0)),
                      pl.BlockSpec((None, tr, cols), lambda q, i, c_ref: (q, i, 0))],
            out_specs=pl.BlockSpec((None, tr, cols), lambda q, i, c_ref: (q, i, 0))),
        out_shape=SDS((4, r, cols), BF16),
        compiler_params=_params("arbitrary", "arbitrary"),
    )(c, own, got)


def _adamw(w, g, m, v):
    m = ADAM_B1 * m + (1.0 - ADAM_B1) * g
    v = ADAM_B2 * v + (1.0 - ADAM_B2) * (g * g)
    m_hat = m / (1.0 - ADAM_B1 ** ADAM_STEP)
    v_hat = v / (1.0 - ADAM_B2 ** ADAM_STEP)
    delta = -ADAM_LR * (m_hat / (jnp.sqrt(v_hat) + ADAM_EPS) + ADAM_WD * w)
    return delta, m, v


def _adamw_layer(chip_sum, got, chip, w, m, v, layer, prev, name, after=None):
    depth, r, cols = w.shape
    tr = _row_tile(r, ROWS_NORM)

    def body(chip_ref, sum_ref, got_ref, w_ref, m_ref, v_ref, *rest):
        g_out, d_out, m_out, v_out, token = rest[-5:]
        g = sum_ref[...].astype(F32) + got_ref[0].astype(F32) + got_ref[1].astype(F32) + got_ref[2].astype(F32)
        delta, mm, vv = _adamw(w_ref[...], g, m_ref[...], v_ref[...])
        g_out[...] = g
        d_out[...] = delta
        m_out[...] = mm
        v_out[...] = vv
        token[...] = jnp.zeros_like(token)

    shard = pl.BlockSpec((None, tr, cols), lambda i, chip_ref: (layer, i, 0))
    in_specs = [pl.BlockSpec((None, tr, cols), lambda i, chip_ref: (chip_ref[0], i, 0)),
                pl.BlockSpec((3, tr, cols), lambda i, chip_ref: (0, i, 0)), shard, shard, shard]
    args = [chip, chip_sum, got, w, m, v]
    aliases = {}
    if prev is not None:
        in_specs += [ANY] * 4
        aliases = {len(args) + k: k for k in range(4)}
        args += list(prev)
    if after is not None:
        in_specs, args = in_specs + [ANY], args + [after]
    res = pl.pallas_call(
        body, name=name,
        grid_spec=pltpu.PrefetchScalarGridSpec(
            num_scalar_prefetch=1, grid=(r // tr,), in_specs=in_specs,
            out_specs=[shard] * 4 + [pl.BlockSpec((8, BLK), lambda i, chip_ref: (0, 0))]),
        out_shape=[SDS(w.shape, F32)] * 4 + [SDS((8, BLK), F32)],
        input_output_aliases=aliases,
        compiler_params=_params("arbitrary"),
    )(*args)
    return list(res[:4]), res[4]


WIDE = ("ln1_g", "out_norm_g", "ln2_g", "conv_b", "conv_ln_g", "conv_ln_b", "sgu_ln_g", "sgu_ln_b")
NARROW = ("q_norm_g", "k_norm_g", "sinks")


def _small_rows(w):
    rows, r = {}, 0
    for n in WIDE:
        rows[n] = (r, w[n].shape[1] // BLK)
        r += rows[n][1]
    for n in NARROW + ("loss",):
        rows[n] = (r, 1)
        r += 1
    r = -(-r // 8) * 8
    rows["sgu_b"] = (r, w["sgu_b"].shape[1])
    r += -(-rows["sgu_b"][1] // 8) * 8
    rows["conv_w"] = (r, N_DEV * HALO)
    return rows, r + N_DEV * HALO


def _pack_small(small, rows, total):
    parts, r = [], 0

    def put(name, block):
        nonlocal r
        first = rows[name][0]
        if first > r:
            parts.append(jnp.zeros((first - r, BLK), F32))
        parts.append(block)
        r = first + block.shape[0]

    for n in WIDE:
        put(n, small[n].reshape(-1, BLK))
    for n in NARROW:
        put(n, jnp.pad(small[n], (0, BLK - small[n].shape[0]))[None])
    if "loss" in small:
        put("loss", jnp.pad(small["loss"], (0, BLK - 1))[None])
    put("sgu_b", small["sgu_b"])
    cw = small["conv_w"]
    per_dev = cw.shape[1] // N_DEV
    blocks = jnp.transpose(cw.reshape(CONV_K, N_DEV, per_dev), (1, 0, 2))
    put("conv_w", jnp.pad(blocks, ((0, 0), (0, HALO - CONV_K), (0, BLK - per_dev))).reshape(N_DEV * HALO, BLK))
    if total > r:
        parts.append(jnp.zeros((total - r, BLK), F32))
    return jnp.concatenate(parts)


def _small_update(me, packed, sgu_w_all, w, m, v, rows):
    depth = len(packed)
    total = packed[0].shape[1]
    names = SMALL + ("conv_w",)
    heads = w["sgu_w"].shape[1]
    per_dev = w["conv_w"].shape[2]

    def body(me_ref, *refs):
        packed_refs, sgu_refs = refs[:depth], refs[depth:2 * depth]
        ins = refs[2 * depth:2 * depth + 3 * len(names)]
        outs = refs[2 * depth + 3 * len(names):2 * depth + 7 * len(names)]
        loss_out, acc, conv = refs[-3:]
        io = {n: (ins[3 * i:3 * i + 3], outs[4 * i:4 * i + 4]) for i, n in enumerate(names)}

        def update(n):
            (w_ref, m_ref, v_ref), (g_out, d_out, m_out, v_out) = io[n]
            delta, mm, vv = _adamw(w_ref[...], g_out[...], m_ref[...], v_ref[...])
            d_out[...] = delta
            m_out[...] = mm
            v_out[...] = vv

        mine = pl.ds(pl.multiple_of(rows["conv_w"][0] + HALO * me_ref[0], 8), HALO)
        for l in range(depth):
            s = packed_refs[l][0]
            c = packed_refs[l][0, mine, :]
            for k in range(1, N_DEV):
                s = s + packed_refs[l][k]
                c = c + packed_refs[l][k, mine, :]
            acc[l] = s
            conv[l] = c
        loss_out[...] = acc[depth - 1, rows["loss"][0]:rows["loss"][0] + 1, :]
        for n in WIDE:
            first, nr = rows[n]
            for l in range(depth):
                for j in range(nr):
                    io[n][1][0][l:l + 1, BLK * j:BLK * (j + 1)] = acc[l, first + j:first + j + 1, :]
            update(n)
        for n in NARROW:
            first, lanes = rows[n][0], w[n].shape[1]
            for l in range(depth):
                io[n][1][0][l:l + 1, :] = acc[l, first:first + 1, 0:lanes]
            update(n)
        first, nr = rows["sgu_b"]
        for l in range(depth):
            io["sgu_b"][1][0][l] = acc[l, first:first + nr, :]
            io["conv_w"][1][0][l] = conv[l, 0:CONV_K, 0:per_dev]
        update("sgu_b")
        update("conv_w")
        (w_ref, m_ref, v_ref), (g_out, d_out, m_out, v_out) = io["sgu_w"]
        for l in range(depth):
            for h in range(heads):
                g = sgu_refs[l][0, h]
                for k in range(1, N_DEV):
                    g = g + sgu_refs[l][k, h]
                delta, mm, vv = _adamw(w_ref[l, h], g, m_ref[l, h], v_ref[l, h])
                g_out[l, h] = g
                d_out[l, h] = delta
                m_out[l, h] = mm
                v_out[l, h] = vv

    def whole(a):
        nd = len(a.shape)
        return pl.BlockSpec(a.shape, lambda i, me_ref: (0,) * nd)

    small_in = [t[n] for n in names for t in (w, m, v)]
    res = pl.pallas_call(
        body, name="small_update",
        grid_spec=pltpu.PrefetchScalarGridSpec(
            num_scalar_prefetch=1, grid=(1,),
            in_specs=[whole(a) for a in list(packed) + list(sgu_w_all) + small_in],
            out_specs=[whole(w[n]) for n in names for _ in range(4)] + [pl.BlockSpec((1, BLK), lambda i, me_ref: (0, 0))],
            scratch_shapes=[pltpu.VMEM((depth, total, BLK), F32), pltpu.VMEM((depth, HALO, BLK), F32)]),
        out_shape=[SDS(w[n].shape, F32) for n in names for _ in range(4)] + [SDS((1, BLK), F32)],
        compiler_params=_params("arbitrary"),
    )(me, *packed, *sgu_w_all, *small_in)
    return {n: list(res[4 * i:4 * i + 4]) for i, n in enumerate(names)}, res[-1][0, 0]


def _pack(arrays):
    flat = jnp.concatenate([a.reshape(-1) for a in arrays])
    rows = -(-flat.shape[0] // (8 * BLK)) * 8
    return jnp.pad(flat, (0, rows * BLK - flat.shape[0])).reshape(rows, BLK)


def _unpack(packed, like):
    flat = packed.reshape(-1)
    out, off = [], 0
    for a in like:
        out.append(flat[off:off + a.size].reshape(a.shape))
        off += a.size
    return out


SMALL = ("ln1_g", "q_norm_g", "k_norm_g", "sinks", "conv_b", "conv_ln_g", "conv_ln_b", "sgu_ln_g", "sgu_ln_b",
         "sgu_w", "sgu_b", "out_norm_g", "ln2_g")
ORDER = ("ln1_g", "w_in", "q_norm_g", "k_norm_g", "sinks", "conv_w", "conv_b", "conv_ln_g", "conv_ln_b", "sgu_ln_g",
         "sgu_ln_b", "sgu_w", "sgu_b", "out_norm_g", "w_out", "ln2_g", "w_up", "w_down")


def _step(x, target, w, m, v):
    depth = w["ln1_g"].shape[0]
    xpos, ypos, cpos = _mesh_pos()
    me = 4 * xpos + 2 * ypos + cpos
    c_arr = jnp.reshape(cpos, (1,)).astype(jnp.int32)
    chip_arr = jnp.reshape(2 * xpos + ypos, (1,)).astype(jnp.int32)

    d = x.shape[1]
    def own_block(shard):
        return lax.dynamic_update_slice_in_dim(lax.empty((N_DEV,) + shard.shape, shard.dtype), shard[None], me, axis=0)

    cw = w["conv_w"]
    order = [(0, "conv_w")] + [(l, n) for l in range(depth) for n in BIG]
    started, gather_token = _start_exchanges("gather_start", [
        ([own_block(_pack([cw]) if n == "conv_w" else w[n][l].astype(BF16))], _gather_plan, 4) for l, n in order])
    pending = dict(zip(order, started))

    def fetch(l, n, after):
        arrays, sends, recvs = pending.pop((l, n))
        land, = _wait_exchange(f"gather_wait_{l}_{n}", arrays, sends, recvs, _gather_plan,
                               gather_token if after is None else after)
        full = _gather_finish(land, "gather_finish_" + n)
        return full if n in ("w_up", "conv_w") else full.reshape(-1, d)

    cw_all = fetch(0, "conv_w", None)
    cw_full = jnp.concatenate([_unpack(cw_all[k], [cw])[0] for k in range(N_DEV)], axis=-1)
    small, _ = lax.optimization_barrier(({n: w[n] for n in SMALL}, gather_token))
    small["conv_w"] = cw_full

    to_sibling, inflight, shared, leaving = {}, [], {}, []
    rows, total = _small_rows(w)

    def flush(name, first=()):
        groups = list(first) + [g for _, g in leaving]
        started, token = _start_exchanges(name, groups)
        for (record, _), going in zip(leaving, started[len(first):]):
            record(going)
        leaving.clear()
        return started[:len(first)], token

    def start(l, n, grad, after):
        if callable(grad):
            blocks, plan = grad(sel=1 - c_arr, after=after), _half_plan
        else:
            blocks, plan = grad.reshape(N_DEV, -1, d), _pair_plan
        (going,), token = flush(f"grads_start_{l}_{n}", [([blocks, lax.empty((4,) + blocks.shape[1:], BF16)], plan, 4)])
        to_sibling[l, n] = (going, plan, grad)
        return token

    def finish(l, n, after):
        (arrays, sends, recvs), plan, grad = to_sibling.pop((l, n))
        blocks, from_sibling = _wait_exchange(f"pair_wait_{l}_{n}", arrays, sends, recvs, plan, after)
        if callable(grad):
            chip_sums = grad(sel=c_arr, add=from_sibling)
        else:
            chip_sums = _pair_add(blocks, from_sibling, c_arr, "pair_add_" + n)
        leaving.append((lambda going: inflight.append((l, n, going)),
                        ([chip_sums, lax.empty((3,) + chip_sums.shape[1:], BF16)], _chips_plan, 3)))
        return chip_sums

    def share(l, kind, grads):
        block = grads if kind == "sgu_w" else _pack_small(grads, rows, total)
        leaving.append((lambda going: shared.__setitem__((l, kind), going), ([own_block(block)], _all_plan, N_DEV - 1)))

    grad_x, (after, finish_last) = _local_step(x, target, small, depth, fetch, start, finish, share, None)

    as3d = lambda a: a.reshape(depth, -1, a.shape[-1])
    results = {n: None for n in BIG}

    landed = {}

    def await_layer(name, reduces, after):
        smalls = sorted(k for k in shared if k not in landed and all(k[0] >= l for l, _, _ in reduces))
        got = _wait_exchanges(name, [(going, _chips_plan) for _, _, going in reduces]
                              + [(shared[k], _all_plan) for k in smalls], after)
        for k, (arr,) in zip(smalls, got[len(reduces):]):
            landed[k] = arr
        return got[:len(reduces)]

    def update(l, n, chip_sums, got, after):
        results[n], token = _adamw_layer(chip_sums, got, chip_arr, as3d(w[n]), as3d(m[n]), as3d(v[n]), l, results[n],
                                         f"adamw_{l}_{n}", after)
        return token

    first, rest = inflight[:4], inflight[4:]
    arrived = await_layer("grads_wait_first", first, after)
    after = None
    for (l, n, _), (chip_sums, got) in list(zip(first, arrived))[:2]:
        after = update(l, n, chip_sums, got, after)
    finish_last(after)
    _, after = flush("grads_start_last")
    for (l, n, _), (chip_sums, got) in list(zip(first, arrived))[2:]:
        after = update(l, n, chip_sums, got, after)
    arrived = await_layer("grads_wait_rest", rest, after)
    out, loss = _small_update(jnp.reshape(me, (1,)).astype(jnp.int32), [landed[l, "rest"] for l in range(depth)],
                              [landed[l, "sgu_w"] for l in range(depth)], w, m, v, rows)
    after = out["sinks"][1]
    for (l, n, _), (chip_sums, got) in zip(rest, arrived):
        after = update(l, n, chip_sums, got, after)
    (chip_sums, got), = await_layer("grads_wait_last", inflight[-1:], after)
    l, n, _ = inflight[-1]
    update(l, n, chip_sums, got, None)
    out.update({n: [r.reshape(w[n].shape) for r in results[n]] for n in BIG})
    return (loss, grad_x[None]) + tuple(out[n][k] for k in range(4) for n in ORDER)


def kernel(x, ln1_g, w_in, q_norm_g, k_norm_g, sinks, conv_w, conv_b, conv_ln_g, conv_ln_b, sgu_ln_g, sgu_ln_b, sgu_w, sgu_b, out_norm_g, w_out, ln2_g, w_up, w_down, loss_target, m_ln1_g, m_w_in, m_q_norm_g, m_k_norm_g, m_sinks, m_conv_w, m_conv_b, m_conv_ln_g, m_conv_ln_b, m_sgu_ln_g, m_sgu_ln_b, m_sgu_w, m_sgu_b, m_out_norm_g, m_w_out, m_ln2_g, m_w_up, m_w_down, v_ln1_g, v_w_in, v_q_norm_g, v_k_norm_g, v_sinks, v_conv_w, v_conv_b, v_conv_ln_g, v_conv_ln_b, v_sgu_ln_g, v_sgu_ln_b, v_sgu_w, v_sgu_b, v_out_norm_g, v_w_out, v_ln2_g, v_w_up, v_w_down):
    w = dict(zip(ORDER, (ln1_g, w_in, q_norm_g, k_norm_g, sinks, conv_w, conv_b, conv_ln_g, conv_ln_b, sgu_ln_g, sgu_ln_b,
                         sgu_w, sgu_b, out_norm_g, w_out, ln2_g, w_up, w_down)))
    m = dict(zip(ORDER, (m_ln1_g, m_w_in, m_q_norm_g, m_k_norm_g, m_sinks, m_conv_w, m_conv_b, m_conv_ln_g, m_conv_ln_b,
                         m_sgu_ln_g, m_sgu_ln_b, m_sgu_w, m_sgu_b, m_out_norm_g, m_w_out, m_ln2_g, m_w_up, m_w_down)))
    v = dict(zip(ORDER, (v_ln1_g, v_w_in, v_q_norm_g, v_k_norm_g, v_sinks, v_conv_w, v_conv_b, v_conv_ln_g, v_conv_ln_b,
                         v_sgu_ln_g, v_sgu_ln_b, v_sgu_w, v_sgu_b, v_out_norm_g, v_w_out, v_ln2_g, v_w_up, v_w_down)))
    for group in (w, m, v):
        group["w_in"] = jnp.swapaxes(group["w_in"], 1, 2)
    out = list(_step(x[0], loss_target[0], w, m, v))
    for k in range(4):
        i = 2 + k * len(ORDER) + ORDER.index("w_in")
        out[i] = jnp.swapaxes(out[i], 1, 2)
    return tuple(out)
```

```python
import functools

import jax
import jax.numpy as jnp
from jax import lax
from jax.experimental import pallas as pl
from jax.experimental.pallas import tpu as pltpu

F32 = jnp.float32
BF16 = jnp.bfloat16
SDS = jax.ShapeDtypeStruct

EPS = 1e-6
NEG_INF = -1e30
HEAD = 64
BLK = 128
CONV_K = 31
HALO = 32
N_DEV = 8

ADAM_LR = 0.001
ADAM_B1 = 0.9
ADAM_B2 = 0.999
ADAM_EPS = 1e-08
ADAM_WD = 0.01
ADAM_STEP = 10

VMEM_LIMIT = 56 * 1024 * 1024

ROWS_WIDE = 1024
ROWS_F32 = 512
ROWS_NORM = 256
COLS = 512
COLS_WIDE = 2048

MESH = pl.DeviceIdType.MESH


def _params(*sem):
    return pltpu.CompilerParams(dimension_semantics=sem, vmem_limit_bytes=VMEM_LIMIT)


def _nt(a, b):
    return lax.dot_general(a, b, (((1,), (1,)), ((), ())), preferred_element_type=F32)


def _tn(a, b):
    return lax.dot_general(a, b, (((0,), (0,)), ((), ())), preferred_element_type=F32)


def _nn(a, b):
    return jnp.dot(a, b, preferred_element_type=F32)


def _sigmoid(x):
    return 1.0 / (1.0 + jnp.exp(-x))


ANY = pl.BlockSpec(memory_space=pl.ANY)


def _ordered_behind(body, n_in, in_specs, args, after):
    if after is None:
        return body, in_specs, args
    return (lambda *refs: body(*refs[:n_in], *refs[n_in + 1:])), list(in_specs) + [ANY], list(args) + [after]


def _segsum(x, first):
    head0 = jnp.where(first, x, 0.0)
    s0 = jnp.sum(head0, axis=-1, keepdims=True)
    s1 = jnp.sum(x - head0, axis=-1, keepdims=True)
    return jnp.where(first, s0, s1)


def _head_rms(x, gain, first):
    rstd = lax.rsqrt(_segsum(x * x, first) * (1.0 / HEAD) + EPS)
    xhat = x * rstd
    return xhat * gain, xhat, rstd


def _expand(x, odd, lo):
    if odd:
        xl = pltpu.roll(jnp.where(lo, 0.0, x), HEAD, axis=1)
    else:
        xl = jnp.where(lo, x, 0.0)
    xh = pltpu.roll(xl, HEAD, axis=1)
    return jnp.concatenate([xl, xh], axis=0).astype(BF16)


def _attn_mask(n):
    qi = lax.broadcasted_iota(jnp.int32, (BLK, 2 * BLK), 0)
    sj = lax.broadcasted_iota(jnp.int32, (BLK, 2 * BLK), 1)
    rel = qi + BLK - sj
    return (rel >= 0) & (rel < BLK) & ((sj >= BLK) | (n > 0))


def _attn_specs(t, aw, kv):
    prev = lambda n: jnp.maximum(n - 1, 0)
    kb, vb = aw // kv, aw // kv + 1
    return [
        pl.BlockSpec(memory_space=pltpu.SMEM),
        pl.BlockSpec((BLK, aw), lambda n: (n, 0)),
        pl.BlockSpec((BLK, kv), lambda n: (prev(n), kb)),
        pl.BlockSpec((BLK, kv), lambda n: (n, kb)),
        pl.BlockSpec((BLK, kv), lambda n: (prev(n), vb)),
        pl.BlockSpec((BLK, kv), lambda n: (n, vb)),
        pl.BlockSpec((1, BLK), lambda n: (0, 0)),
        pl.BlockSpec((1, BLK), lambda n: (0, 0)),
    ]


def _softmax_pair(s2, valid, sink0, sink1):
    out, psink = [], []
    for half, sink in ((0, sink0), (1, sink1)):
        s = jnp.where(valid, s2[:, 2 * BLK * half:2 * BLK * (half + 1)], NEG_INF)
        m = jnp.maximum(jnp.max(s, axis=-1, keepdims=True), sink)
        p = jnp.exp(s - m)
        es = jnp.exp(sink - m)
        inv = 1.0 / (jnp.sum(p, axis=-1, keepdims=True) + es)
        out.append(p * inv)
        psink.append(es * inv)
    return jnp.concatenate(out, axis=1), psink


def _attn_fwd(proj, qg, kg, sinks, aw, kv):
    t = proj.shape[0]
    n_pairs, n_kvblk = aw // BLK, kv // BLK

    def body(sink_ref, q_ref, kp_ref, kc_ref, vp_ref, vc_ref, qg_ref, kg_ref, o_ref, probs_ref, sink_p_ref):
        n = pl.program_id(0)
        lo = lax.broadcasted_iota(jnp.int32, (1, BLK), 1) < HEAD
        valid = _attn_mask(n)
        kraw = jnp.concatenate([kp_ref[...], kc_ref[...]], axis=0)
        vraw = jnp.concatenate([vp_ref[...], vc_ref[...]], axis=0)
        qn = [_head_rms(q_ref[:, BLK * p:BLK * (p + 1)], qg_ref[...], lo)[0].astype(BF16) for p in range(n_pairs)]
        k2, v2 = [], []
        for b in range(n_kvblk):
            kn = _head_rms(kraw[:, BLK * b:BLK * (b + 1)], kg_ref[...], lo)[0]
            for odd in (0, 1):
                k2.append(_expand(kn, odd, lo))
                v2.append(_expand(vraw[:, BLK * b:BLK * (b + 1)], odd, lo))
        s2 = [_nt(qn[p], k2[p // 2]) * (HEAD ** -0.5) for p in range(n_pairs)]
        soft = [_softmax_pair(s2[p], valid, sink_ref[2 * p], sink_ref[2 * p + 1]) for p in range(n_pairs)]
        lane = lax.broadcasted_iota(jnp.int32, (1, BLK), 1)
        sink_p = jnp.zeros((BLK, BLK), F32)
        for p in range(n_pairs):
            probs_ref[:, 4 * BLK * p:4 * BLK * (p + 1)] = soft[p][0]
            for half in (0, 1):
                sink_p = jnp.where(lane == 2 * p + half, soft[p][1][half], sink_p)
            o_ref[:, BLK * p:BLK * (p + 1)] = _nn(soft[p][0].astype(BF16), v2[p // 2])
        sink_p_ref[...] = sink_p

    return pl.pallas_call(
        body, name="attn_fwd", grid=(t // BLK,),
        in_specs=_attn_specs(t, aw, kv),
        out_specs=[pl.BlockSpec((BLK, aw), lambda n: (n, 0)), pl.BlockSpec((BLK, 4 * aw), lambda n: (n, 0)),
                   pl.BlockSpec((BLK, BLK), lambda n: (n, 0))],
        out_shape=[SDS((t, aw), F32), SDS((t, 4 * aw), F32), SDS((t, BLK), F32)],
        compiler_params=_params("arbitrary"),
    )(sinks, proj, proj, proj, proj, proj, qg, kg)


def _attn_bwd(proj, dy, probs, sink_p, qg, kg, sinks, aw, kv, after=None):
    t = proj.shape[0]
    nb = t // BLK
    n_kvblk = kv // BLK
    kb = aw // kv

    def body(sink_ref, q_ref, kp_ref, kc_ref, vp_ref, vc_ref, qg_ref, kg_ref, dy_ref, kall_ref, probs_ref, sink_p_ref,
             dqkv_ref, dstat_ref, dk_acc, dv_acc, dqg_acc):
        del sink_ref
        n = pl.program_id(0)
        lane = lax.broadcasted_iota(jnp.int32, (1, BLK), 1)
        lo = lane < HEAD

        @pl.when(n == 0)
        def _():
            dk_acc[...] = jnp.zeros_like(dk_acc)
            dv_acc[...] = jnp.zeros_like(dv_acc)
            dqg_acc[...] = jnp.zeros_like(dqg_acc)
            dstat_ref[...] = jnp.zeros_like(dstat_ref)

        kraw = jnp.concatenate([kp_ref[...], kc_ref[...]], axis=0)
        vraw = jnp.concatenate([vp_ref[...], vc_ref[...]], axis=0)
        row = pl.multiple_of(n * BLK, BLK)
        prow = pl.multiple_of(jnp.maximum(n - 1, 0) * BLK, BLK)
        pairs = range(aw // BLK)
        cols = [slice(BLK * p, BLK * (p + 1)) for p in pairs]
        qs = [_head_rms(q_ref[:, cols[p]], qg_ref[...], lo) for p in pairs]
        qb = [qs[p][0].astype(BF16) for p in pairs]
        k2, v2 = [], []
        for b in range(n_kvblk):
            kn = _head_rms(kraw[:, BLK * b:BLK * (b + 1)], kg_ref[...], lo)[0]
            for odd in (0, 1):
                k2.append(_expand(kn, odd, lo))
                v2.append(_expand(vraw[:, BLK * b:BLK * (b + 1)], odd, lo))
        p2 = [probs_ref[:, 4 * BLK * p:4 * BLK * (p + 1)] for p in pairs]
        dob = [dy_ref[:, cols[p]].astype(BF16) for p in pairs]
        dp2 = [_nt(dob[p], v2[p // 2]) for p in pairs]
        deltas = jnp.zeros((BLK, BLK), F32)
        ds2 = []
        for p in pairs:
            ds = []
            for half in (0, 1):
                hs = slice(2 * BLK * half, 2 * BLK * (half + 1))
                ph = p2[p][:, hs]
                delta = jnp.sum(ph * dp2[p][:, hs], axis=-1, keepdims=True)
                ds.append(ph * (dp2[p][:, hs] - delta))
                deltas = jnp.where(lane == 2 * p + half, delta, deltas)
            ds2.append((jnp.concatenate(ds, axis=1) * (HEAD ** -0.5)).astype(BF16))
        dstat_ref[2:3, :] -= jnp.sum(sink_p_ref[...] * deltas, axis=0, keepdims=True)
        dqn = [_nn(ds2[p], k2[p // 2]) for p in pairs]
        dk2t = [_tn(qb[p], ds2[p]) for p in pairs]
        dv2t = [_tn(dob[p], p2[p].astype(BF16)) for p in pairs]
        for p in pairs:
            _, qhat, rstd = qs[p]
            dqhat = dqn[p] * qg_ref[...]
            proj_q = _segsum(dqhat * qhat, lo) * (1.0 / HEAD)
            dqkv_ref[pl.ds(row, BLK), cols[p]] = (rstd * (dqhat - qhat * proj_q)).astype(BF16)
            dqg_acc[:, cols[p]] += jnp.sum(dqn[p] * qhat, axis=0, keepdims=True)
        def head_sum(parts, j):
            return sum(a[:HEAD, :2 * BLK] + a[HEAD:, 2 * BLK:] for a in parts[2 * j:2 * j + 2])

        for b in range(n_kvblk):
            dkn = jnp.concatenate([head_sum(dk2t, 2 * b), head_sum(dk2t, 2 * b + 1)], axis=0).T
            dvb = jnp.concatenate([head_sum(dv2t, 2 * b), head_sum(dv2t, 2 * b + 1)], axis=0).T
            kcols = slice(BLK * b, BLK * (b + 1))
            dk_acc[pl.ds(prow, BLK), kcols] += dkn[:BLK]
            dv_acc[pl.ds(prow, BLK), kcols] += dvb[:BLK]
            dk_acc[pl.ds(row, BLK), kcols] += dkn[BLK:]
            dv_acc[pl.ds(row, BLK), kcols] += dvb[BLK:]

        @pl.when(n == nb - 1)
        def _():
            dqg = dqg_acc[:, 0:BLK]
            for p in range(1, aw // BLK):
                dqg = dqg + dqg_acc[:, BLK * p:BLK * (p + 1)]
            dstat_ref[0:1, :] = dqg + pltpu.roll(dqg, HEAD, axis=1)

            def kblock(i, dkg):
                r = pl.multiple_of(i * BLK, BLK)
                for b in range(n_kvblk):
                    kcols = slice(BLK * b, BLK * (b + 1))
                    _, khat, rstd = _head_rms(kall_ref[pl.ds(r, BLK), kcols], kg_ref[...], lo)
                    dkn = dk_acc[pl.ds(r, BLK), kcols]
                    dkhat = dkn * kg_ref[...]
                    proj_k = _segsum(dkhat * khat, lo) * (1.0 / HEAD)
                    dqkv_ref[pl.ds(r, BLK), aw + BLK * b:aw + BLK * (b + 1)] = (rstd * (dkhat - khat * proj_k)).astype(BF16)
                    dqkv_ref[pl.ds(r, BLK), aw + kv + BLK * b:aw + kv + BLK * (b + 1)] = dv_acc[pl.ds(r, BLK), kcols].astype(BF16)
                    dkg = dkg + jnp.sum(dkn * khat, axis=0, keepdims=True)
                return dkg

            dkg = lax.fori_loop(0, nb, kblock, jnp.zeros((1, BLK), F32))
            dstat_ref[1:2, :] = dkg + pltpu.roll(dkg, HEAD, axis=1)

    in_specs = _attn_specs(t, aw, kv) + [
        pl.BlockSpec((BLK, aw), lambda n: (n, 0)),
        pl.BlockSpec((t, kv), lambda n: (0, kb)),
        pl.BlockSpec((BLK, 4 * aw), lambda n: (n, 0)),
        pl.BlockSpec((BLK, BLK), lambda n: (n, 0)),
    ]
    args = [sinks, proj, proj, proj, proj, proj, qg, kg, dy, proj, probs, sink_p]
    body, in_specs, args = _ordered_behind(body, len(args), in_specs, args, after)
    return pl.pallas_call(
        body, name="attn_bwd", grid=(nb,),
        in_specs=in_specs,
        out_specs=[pl.BlockSpec((t, aw + 2 * kv), lambda n: (0, 0)), pl.BlockSpec((8, BLK), lambda n: (0, 0))],
        out_shape=[SDS((t, aw + 2 * kv), BF16), SDS((8, BLK), F32)],
        scratch_shapes=[pltpu.VMEM((t, kv), F32), pltpu.VMEM((t, kv), F32), pltpu.VMEM((1, aw), F32)],
        compiler_params=_params("arbitrary"),
    )(*args)


def _conv_taps(win, w_ref, shift_of):
    rows = win.shape[0]
    acc = None
    for j in range(CONV_K):
        term = pltpu.roll(win, (rows - shift_of(j)) % rows, axis=0)[:BLK] * w_ref[j:j + 1, :]
        acc = term if acc is None else acc + term
    return acc


def _layer_norm_fwd(z):
    mu = jnp.mean(z, axis=-1, keepdims=True)
    zc = z - mu
    rstd = lax.rsqrt(jnp.mean(zc * zc, axis=-1, keepdims=True) + EPS)
    return zc * rstd, rstd


def _layer_norm_bwd(dy, yhat, rstd, g):
    dyh = dy * g
    return rstd * (dyh - jnp.mean(dyh, axis=-1, keepdims=True) - yhat * jnp.mean(dyh * yhat, axis=-1, keepdims=True))


def _conv_fill_glu(a_ref, g_ref, hpad, nb):
    hpad[0:HALO, :] = jnp.zeros((HALO, hpad.shape[1]), F32)

    def fill(i, c):
        r = pl.multiple_of(i * BLK, BLK)
        hpad[pl.ds(pl.multiple_of(r + HALO, HALO), BLK), :] = a_ref[pl.ds(r, BLK), :] * _sigmoid(g_ref[pl.ds(r, BLK), :])
        return c

    lax.fori_loop(0, nb, fill, 0)


def _conv_specs(t, cw, d_in):
    base = (d_in - 4 * cw) // cw
    vec = pl.BlockSpec((1, cw), lambda i: (0, 0))
    return [
        pl.BlockSpec((t, cw), lambda i: (0, base)),
        pl.BlockSpec((t, cw), lambda i: (0, base + 1)),
        pl.BlockSpec((HALO, cw), lambda i: (0, 0)),
        vec, vec, vec,
    ]


def _conv_fwd(proj, cw_pad, cb, lg, lb, cw):
    t, d_in = proj.shape
    nb = t // BLK

    def body(a_ref, g_ref, w_ref, b_ref, lg_ref, lb_ref, o_ref, z_ref, hpad):
        _conv_fill_glu(a_ref, g_ref, hpad, nb)

        def blk(i, c):
            r = pl.multiple_of(i * BLK, BLK)
            z = _conv_taps(hpad[pl.ds(r, BLK + HALO), :], w_ref, lambda j: j + HALO - (CONV_K - 1)) + b_ref[...]
            z_ref[pl.ds(r, BLK), :] = z
            yhat, _ = _layer_norm_fwd(z)
            y = yhat * lg_ref[...] + lb_ref[...]
            o_ref[pl.ds(r, BLK), :] = y * _sigmoid(y)
            return c

        lax.fori_loop(0, nb, blk, 0)

    whole = pl.BlockSpec((t, cw), lambda i: (0, 0))
    return pl.pallas_call(
        body, name="conv_fwd", grid=(1,),
        in_specs=_conv_specs(t, cw, d_in),
        out_specs=[whole, whole],
        out_shape=[SDS((t, cw), F32), SDS((t, cw), F32)],
        scratch_shapes=[pltpu.VMEM((t + HALO, cw), F32)],
        compiler_params=_params("arbitrary"),
    )(proj, proj, cw_pad, cb, lg, lb)


def _conv_bwd(proj, dy, z, cw_pad, cb, lg, lb, cw):
    t, d_in = proj.shape
    nb = t // BLK

    def body(a_ref, g_ref, w_ref, b_ref, lg_ref, lb_ref, dy_ref, z_ref, dc_ref, dw_ref, dvec_ref, hpad, dzpad, dwacc):
        del b_ref
        _conv_fill_glu(a_ref, g_ref, hpad, nb)
        dzpad[t:t + HALO, :] = jnp.zeros((HALO, cw), F32)
        dwacc[...] = jnp.zeros_like(dwacc)

        def blk(i, carry):
            db, dlg, dlb = carry
            r = pl.multiple_of(i * BLK, BLK)
            win = hpad[pl.ds(r, BLK + HALO), :]
            yhat, rstd = _layer_norm_fwd(z_ref[pl.ds(r, BLK), :])
            y = yhat * lg_ref[...] + lb_ref[...]
            sg = _sigmoid(y)
            dyl = dy_ref[pl.ds(r, BLK), :] * (sg * (1.0 + y * (1.0 - sg)))
            dz = _layer_norm_bwd(dyl, yhat, rstd, lg_ref[...])
            dzpad[pl.ds(r, BLK), :] = dz
            for j in range(CONV_K):
                sh = j + HALO - (CONV_K - 1)
                prod = dz * pltpu.roll(win, (BLK + HALO - sh) % (BLK + HALO), axis=0)[:BLK]
                dwacc[8 * j:8 * j + 8, :] += jnp.sum(prod.reshape(BLK // 8, 8, cw), axis=0)
            return (db + jnp.sum(dz, axis=0, keepdims=True),
                    dlg + jnp.sum(dyl * yhat, axis=0, keepdims=True),
                    dlb + jnp.sum(dyl, axis=0, keepdims=True))

        zero = jnp.zeros((1, cw), F32)
        db, dlg, dlb = lax.fori_loop(0, nb, blk, (zero, zero, zero))
        dvec_ref[...] = jnp.zeros_like(dvec_ref)
        dvec_ref[0:1, :] = db
        dvec_ref[1:2, :] = dlg
        dvec_ref[2:3, :] = dlb
        dw_ref[...] = jnp.sum(dwacc[...].reshape(HALO, 8, cw), axis=1)

        def blk2(i, c):
            r = pl.multiple_of(i * BLK, BLK)
            dh = _conv_taps(dzpad[pl.ds(r, BLK + HALO), :], w_ref, lambda j: CONV_K - 1 - j)
            a = a_ref[pl.ds(r, BLK), :]
            sg = _sigmoid(g_ref[pl.ds(r, BLK), :])
            dc_ref[pl.ds(r, BLK), 0:cw] = (dh * sg).astype(BF16)
            dc_ref[pl.ds(r, BLK), cw:2 * cw] = (dh * a * sg * (1.0 - sg)).astype(BF16)
            return c

        lax.fori_loop(0, nb, blk2, 0)

    return pl.pallas_call(
        body, name="conv_bwd", grid=(1,),
        in_specs=_conv_specs(t, cw, d_in) + [pl.BlockSpec((t, cw), lambda i: (0, 0))] * 2,
        out_specs=[pl.BlockSpec((t, 2 * cw), lambda i: (0, 0)), pl.BlockSpec((HALO, cw), lambda i: (0, 0)),
                   pl.BlockSpec((8, cw), lambda i: (0, 0))],
        out_shape=[SDS((t, 2 * cw), BF16), SDS((HALO, cw), F32), SDS((8, cw), F32)],
        scratch_shapes=[pltpu.VMEM((t + HALO, cw), F32), pltpu.VMEM((t + HALO, cw), F32), pltpu.VMEM((8 * HALO, cw), F32)],
        compiler_params=_params("arbitrary"),
    )(proj, proj, cw_pad, cb, lg, lb, dy, z)


def _tril_bf16(w):
    r = lax.broadcasted_iota(jnp.int32, (BLK, BLK), 0)
    c = lax.broadcasted_iota(jnp.int32, (BLK, BLK), 1)
    return jnp.where(r >= c, w, 0.0).astype(BF16)


def _sgu_specs(sw, d_in, heads):
    base = (d_in - 2 * sw) // sw
    vec = pl.BlockSpec((1, sw), lambda n: (0, 0))
    return [
        pl.BlockSpec((BLK, sw), lambda n: (n, base)),
        pl.BlockSpec((BLK, sw), lambda n: (n, base + 1)),
        vec, vec,
        pl.BlockSpec((heads, BLK, BLK), lambda n: (0, 0, 0)),
        pl.BlockSpec((BLK, sw), lambda n: (0, 0)),
    ]


def _sgu_mix(w_ref, vnb, heads, sw, transpose):
    head_of = lax.broadcasted_iota(jnp.int32, (1, sw), 1) // HEAD
    s = jnp.zeros((BLK, sw), F32)
    for h in range(heads):
        wt = _tril_bf16(w_ref[h])
        mixed = _tn(wt, vnb) if transpose else _nn(wt, vnb)
        s = jnp.where(head_of == h, mixed, s)
    return s


def _sgu_fwd(proj, lg, lb, w, bias_full, sw):
    t, d_in = proj.shape
    heads = sw // HEAD

    def body(u_ref, v_ref, lg_ref, lb_ref, w_ref, bias_ref, o_ref):
        vhat, _ = _layer_norm_fwd(v_ref[...])
        vn = (vhat * lg_ref[...] + lb_ref[...]).astype(BF16)
        s = _sgu_mix(w_ref, vn, heads, sw, False) + bias_ref[...]
        o_ref[...] = u_ref[...] * s

    return pl.pallas_call(
        body, name="sgu_fwd", grid=(t // BLK,),
        in_specs=_sgu_specs(sw, d_in, heads),
        out_specs=pl.BlockSpec((BLK, sw), lambda n: (n, 0)),
        out_shape=SDS((t, sw), F32),
        compiler_params=_params("arbitrary"),
    )(proj, proj, lg, lb, w, bias_full)


def _sgu_bwd(proj, dy, lg, lb, w, bias_full, sw, after=None):
    t, d_in = proj.shape
    heads = sw // HEAD
    nb = t // BLK

    def body(u_ref, v_ref, lg_ref, lb_ref, w_ref, bias_ref, dy_ref, ds_ref, dw_ref, db_ref, dvec_ref, dbfull):
        n = pl.program_id(0)

        @pl.when(n == 0)
        def _():
            dw_ref[...] = jnp.zeros_like(dw_ref)
            dvec_ref[...] = jnp.zeros_like(dvec_ref)
            dbfull[...] = jnp.zeros_like(dbfull)

        vhat, rstd = _layer_norm_fwd(v_ref[...])
        vn = (vhat * lg_ref[...] + lb_ref[...]).astype(BF16)
        s = _sgu_mix(w_ref, vn, heads, sw, False) + bias_ref[...]
        dy = dy_ref[...]
        ds_ref[:, 0:sw] = (dy * s).astype(BF16)
        dsv = dy * u_ref[...]
        dbfull[...] += dsv
        head_of = lax.broadcasted_iota(jnp.int32, (1, sw), 1) // HEAD
        r = lax.broadcasted_iota(jnp.int32, (BLK, BLK), 0)
        c = lax.broadcasted_iota(jnp.int32, (BLK, BLK), 1)
        dsb = dsv.astype(BF16)
        for h in range(heads):
            dsh = jnp.where(head_of == h, dsv, 0.0).astype(BF16)
            dw_ref[h] += jnp.where(r >= c, _nt(dsh, vn), 0.0)
        dvn = _sgu_mix(w_ref, dsb, heads, sw, True)
        dvec_ref[0:1, :] += jnp.sum(dvn * vhat, axis=0, keepdims=True)
        dvec_ref[1:2, :] += jnp.sum(dvn, axis=0, keepdims=True)
        ds_ref[:, sw:2 * sw] = _layer_norm_bwd(dvn, vhat, rstd, lg_ref[...]).astype(BF16)

        @pl.when(n == nb - 1)
        def _():
            sel = (lax.broadcasted_iota(jnp.int32, (sw, BLK), 0) // HEAD == lax.broadcasted_iota(jnp.int32, (sw, BLK), 1)).astype(BF16)
            x = dbfull[...]
            hi = x.astype(BF16)
            r1 = x - hi.astype(F32)
            mid = r1.astype(BF16)
            low = (r1 - mid.astype(F32)).astype(BF16)
            db_ref[...] = _nn(hi, sel) + _nn(mid, sel) + _nn(low, sel)

    in_specs = _sgu_specs(sw, d_in, heads) + [pl.BlockSpec((BLK, sw), lambda n: (n, 0))]
    body, in_specs, args = _ordered_behind(body, 7, in_specs, [proj, proj, lg, lb, w, bias_full, dy], after)
    return pl.pallas_call(
        body, name="sgu_bwd", grid=(nb,),
        in_specs=in_specs,
        out_specs=[pl.BlockSpec((BLK, 2 * sw), lambda n: (n, 0)), pl.BlockSpec((heads, BLK, BLK), lambda n: (0, 0, 0)),
                   pl.BlockSpec((BLK, BLK), lambda n: (0, 0)), pl.BlockSpec((8, sw), lambda n: (0, 0))],
        out_shape=[SDS((t, 2 * sw), BF16), SDS((heads, BLK, BLK), F32), SDS((BLK, BLK), F32), SDS((8, sw), F32)],
        scratch_shapes=[pltpu.VMEM((BLK, sw), F32)],
        compiler_params=_params("arbitrary"),
    )(*args)


def _rms_fwd(x, g):
    return (x * lax.rsqrt(jnp.mean(x * x, axis=-1, keepdims=True) + EPS)) * g


def _rms_bwd(dh, x, g):
    rstd = lax.rsqrt(jnp.mean(x * x, axis=-1, keepdims=True) + EPS)
    xhat = x * rstd
    dgx = dh * g
    dx = rstd * (dgx - xhat * jnp.mean(dgx * xhat, axis=-1, keepdims=True))
    return dx, jnp.sum(dh * xhat, axis=0, keepdims=True)


def _rms_matmul(x, g, w, tm, tn, relu2, name, transposed=False):
    t, d = x.shape
    if w.ndim == 3:
        assert w.shape[2] == tn
        n = w.shape[0] * tn
        w_spec = pl.BlockSpec((None, d, tn), lambda i, j: (j, 0, 0))
    elif transposed:
        n = w.shape[0]
        w_spec = pl.BlockSpec((tn, d), lambda i, j: (j, 0))
    else:
        n = w.shape[1]
        w_spec = pl.BlockSpec((d, tn), lambda i, j: (0, j))

    def body(x_ref, g_ref, w_ref, h_ref, *outs):
        @pl.when(pl.program_id(1) == 0)
        def _():
            h_ref[...] = _rms_fwd(x_ref[...], g_ref[...]).astype(BF16)

        acc = _nt(h_ref[...], w_ref[...]) if transposed else _nn(h_ref[...], w_ref[...])
        if relu2:
            r = jnp.maximum(acc, 0.0)
            outs[0][...] = (r * r).astype(BF16)
            outs[1][...] = r.astype(BF16)
        else:
            outs[0][...] = acc

    tile = pl.BlockSpec((tm, tn), lambda i, j: (i, j))
    row = pl.BlockSpec((tm, d), lambda i, j: (i, 0))
    outs = [SDS((t, n), BF16), SDS((t, n), BF16)] if relu2 else [SDS((t, n), F32)]
    return pl.pallas_call(
        body, name=name, grid=(t // tm, n // tn),
        in_specs=[row, pl.BlockSpec((1, d), lambda i, j: (0, 0)), w_spec],
        out_specs=[row] + [tile] * len(outs),
        out_shape=[SDS((t, d), BF16)] + outs,
        compiler_params=_params("parallel", "arbitrary"),
    )(x, g, w)


def _group_rms_matmul(ys, g, w, res, tm, tn):
    t, d = res.shape
    widths = [y.shape[1] for y in ys]
    k = sum(widths)

    def body(*refs):
        y_refs, (g_ref, w_ref, res_ref, mix_ref, o_ref) = refs[:len(ys)], refs[len(ys):]

        @pl.when(pl.program_id(1) == 0)
        def _():
            c = 0
            for y_ref, wd in zip(y_refs, widths):
                mix_ref[:, c:c + wd] = _rms_fwd(y_ref[...], g_ref[:, c:c + wd]).astype(BF16)
                c += wd

        o_ref[...] = res_ref[...] + _nn(mix_ref[...], w_ref[...])

    tile = pl.BlockSpec((tm, tn), lambda i, j: (i, j))
    return pl.pallas_call(
        body, name="mix_out", grid=(t // tm, d // tn),
        in_specs=[pl.BlockSpec((tm, wd), lambda i, j: (i, 0)) for wd in widths] + [
            pl.BlockSpec((1, k), lambda i, j: (0, 0)), pl.BlockSpec((k, tn), lambda i, j: (0, j)), tile],
        out_specs=[pl.BlockSpec((tm, k), lambda i, j: (i, 0)), tile],
        out_shape=[SDS((t, k), BF16), SDS((t, d), F32)],
        compiler_params=_params("parallel", "arbitrary"),
    )(*ys, g, w, res)


def _matmul_res(a, w, res, tm, tn, tk):
    t, k = a.shape
    n = w.shape[1]
    nk = k // tk

    def body(a_ref, w_ref, res_ref, o_ref, acc):
        kk = pl.program_id(2)

        @pl.when(kk == 0)
        def _():
            acc[...] = res_ref[...]

        acc[...] += _nn(a_ref[...], w_ref[...])

        @pl.when(kk == nk - 1)
        def _():
            o_ref[...] = acc[...]

    tile = pl.BlockSpec((tm, tn), lambda i, j, kk: (i, j))
    return pl.pallas_call(
        body, name="mlp_down", grid=(t // tm, n // tn, nk),
        in_specs=[pl.BlockSpec((tm, tk), lambda i, j, kk: (i, kk)), pl.BlockSpec((tk, tn), lambda i, j, kk: (kk, j)), tile],
        out_specs=tile,
        out_shape=SDS((t, n), F32),
        scratch_shapes=[pltpu.VMEM((tm, tn), F32)],
        compiler_params=_params("parallel", "parallel", "arbitrary"),
    )(a, w, res)


def _loss_grad(y, target, tm):
    t, d = y.shape

    def body(y_ref, t_ref, dy_ref, dyb_ref, l_ref):
        @pl.when(pl.program_id(0) == 0)
        def _():
            l_ref[...] = jnp.zeros_like(l_ref)

        err = y_ref[...] - t_ref[...]
        dy = err * (1.0 / d)
        dy_ref[...] = dy
        dyb_ref[...] = dy.astype(BF16)
        per_row = jnp.mean(err * err, axis=-1, keepdims=True)
        l_ref[...] += jnp.sum(per_row, axis=0, keepdims=True) * 0.5

    row = pl.BlockSpec((tm, d), lambda i: (i, 0))
    return pl.pallas_call(
        body, name="loss_grad", grid=(t // tm,),
        in_specs=[row, row], out_specs=[row, row, pl.BlockSpec((8, BLK), lambda i: (0, 0))],
        out_shape=[SDS((t, d), F32), SDS((t, d), BF16), SDS((8, BLK), F32)],
        compiler_params=_params("arbitrary"),
    )(y, target)


def _mlp_dact(dxb, w_down, r, tn, after=None):
    t, d = dxb.shape
    f = w_down.shape[0]

    def body(dxb_ref, w_ref, r_ref, dup_ref):
        dup_ref[...] = (_nt(dxb_ref[...], w_ref[...]) * (2.0 * r_ref[...].astype(F32))).astype(BF16)

    tile = pl.BlockSpec((t, tn), lambda j: (0, j))
    body, in_specs, args = _ordered_behind(
        body, 3, [pl.BlockSpec((t, d), lambda j: (0, 0)), pl.BlockSpec((tn, d), lambda j: (j, 0)), tile],
        [dxb, w_down, r], after)
    return pl.pallas_call(
        body, name="mlp_dact", grid=(f // tn,),
        in_specs=in_specs, out_specs=tile,
        out_shape=SDS((t, f), BF16),
        compiler_params=_params("arbitrary"),
    )(*args)


def _grad_w_stacked(pieces, b, tm, name, after=None):
    t, n = b.shape
    tiles = [p.shape[1] // tm for p in pieces]
    first = [sum(tiles[:k]) for k in range(len(pieces))]

    def body(*refs):
        b_ref, o_ref = refs[len(pieces)], refs[-1]
        i = pl.program_id(0)
        for k, p_ref in enumerate(refs[:len(pieces)]):
            @pl.when((i >= first[k]) & (i < first[k] + tiles[k]))
            def _(p_ref=p_ref):
                o_ref[...] = _tn(p_ref[...], b_ref[...]).astype(BF16)

    in_specs = [pl.BlockSpec((t, tm), lambda i, k=k: (0, jnp.clip(i - first[k], 0, tiles[k] - 1))) for k in range(len(pieces))]
    in_specs.append(pl.BlockSpec((t, n), lambda i: (0, 0)))
    body, in_specs, args = _ordered_behind(body, len(pieces) + 1, in_specs, [*pieces, b], after)
    return pl.pallas_call(
        body, name=name, grid=(sum(tiles),),
        in_specs=in_specs, out_specs=pl.BlockSpec((tm, n), lambda i: (i, 0)),
        out_shape=SDS((sum(tiles) * tm, n), BF16),
        compiler_params=_params("arbitrary"),
    )(*args)


def _grad_w_half(a, b, tm, tn, name, by_cols, sel, add=None, after=None):
    t, m = a.shape
    n = b.shape[1]
    if by_cols:
        per = n // N_DEV // tn
        grid, shape = (4, m // tm, per), (4, m, n // N_DEV)
        a_spec = pl.BlockSpec((t, tm), lambda q, i, j, s: (0, i))
        b_spec = pl.BlockSpec((t, tn), lambda q, i, j, s: (0, (2 * q + s[0]) * per + j))
    else:
        per = m // N_DEV // tm
        grid, shape = (4, per, n // tn), (4, m // N_DEV, n)
        a_spec = pl.BlockSpec((t, tm), lambda q, i, j, s: (0, (2 * q + s[0]) * per + i))
        b_spec = pl.BlockSpec((t, tn), lambda q, i, j, s: (0, j))
    assert per >= 1
    tile = pl.BlockSpec((None, tm, tn), lambda q, i, j, s: (q, i, j))

    def body(sel_ref, a_ref, b_ref, *rest):
        acc = _tn(a_ref[...], b_ref[...])
        if add is not None:
            acc = acc + rest[0][...].astype(F32)
        rest[-1][...] = acc.astype(BF16)

    in_specs, args = [a_spec, b_spec], [a, b]
    if add is not None:
        in_specs, args = in_specs + [tile], args + [add]
    if after is not None:
        in_specs, args = in_specs + [ANY], args + [after]
    return pl.pallas_call(
        body, name=name,
        grid_spec=pltpu.PrefetchScalarGridSpec(num_scalar_prefetch=1, grid=grid, in_specs=in_specs, out_specs=tile),
        out_shape=SDS(shape, BF16),
        compiler_params=_params("arbitrary", "arbitrary", "arbitrary"),
    )(sel, *args)


def _mlp_dnorm(dup, w_up, x, g, dres, tm, after=None):
    t, f = dup.shape
    d = x.shape[1]
    nk, _, tk = w_up.shape

    def body(a_ref, w_ref, x_ref, g_ref, dres_ref, dx_ref, dg_ref, acc):
        i, kk = pl.program_id(0), pl.program_id(1)

        @pl.when((i == 0) & (kk == 0))
        def _():
            dg_ref[...] = jnp.zeros_like(dg_ref)

        @pl.when(kk == 0)
        def _():
            acc[...] = jnp.zeros_like(acc)

        acc[...] += _nt(a_ref[...], w_ref[...])

        @pl.when(kk == nk - 1)
        def _():
            dx, dg = _rms_bwd(acc[...], x_ref[...], g_ref[...])
            dx_ref[...] = dres_ref[...] + dx
            dg_ref[0:1, :] += dg

    row = pl.BlockSpec((tm, d), lambda i, kk: (i, 0))
    in_specs = [pl.BlockSpec((tm, tk), lambda i, kk: (i, kk)), pl.BlockSpec((None, d, tk), lambda i, kk: (kk, 0, 0)),
                row, pl.BlockSpec((1, d), lambda i, kk: (0, 0)), row]
    body, in_specs, args = _ordered_behind(body, 5, in_specs, [dup, w_up, x, g, dres], after)
    return pl.pallas_call(
        body, name="mlp_dnorm", grid=(t // tm, nk),
        in_specs=in_specs,
        out_specs=[row, pl.BlockSpec((8, d), lambda i, kk: (0, 0))],
        out_shape=[SDS((t, d), F32), SDS((8, d), F32)],
        scratch_shapes=[pltpu.VMEM((tm, d), F32)],
        compiler_params=_params("arbitrary", "arbitrary"),
    )(*args)


def _mix_dnorm(dx, w_out, ys, g, tm, after=None):
    t, d = dx.shape
    k = w_out.shape[0]
    widths = [y.shape[1] for y in ys]

    def body(dx_ref, w_ref, *refs):
        y_refs = refs[:len(ys)]
        g_ref, dxb_ref = refs[len(ys)], refs[len(ys) + 1]
        dy_refs = refs[len(ys) + 2:2 * len(ys) + 2]
        dg_ref = refs[-1]

        @pl.when(pl.program_id(0) == 0)
        def _():
            dg_ref[...] = jnp.zeros_like(dg_ref)

        dxb = dx_ref[...].astype(BF16)
        dxb_ref[...] = dxb
        dmix = _nt(dxb, w_ref[...])
        c = 0
        for y_ref, dy_ref, wd in zip(y_refs, dy_refs, widths):
            dy, dg = _rms_bwd(dmix[:, c:c + wd], y_ref[...], g_ref[:, c:c + wd])
            dy_ref[...] = dy
            dg_ref[0:1, c:c + wd] += dg
            c += wd

    row = pl.BlockSpec((tm, d), lambda i: (i, 0))
    yspecs = [pl.BlockSpec((tm, wd), lambda i: (i, 0)) for wd in widths]
    in_specs = [row, pl.BlockSpec((k, d), lambda i: (0, 0))] + yspecs + [pl.BlockSpec((1, k), lambda i: (0, 0))]
    body, in_specs, args = _ordered_behind(body, len(in_specs), in_specs, [dx, w_out, *ys, g], after)
    return pl.pallas_call(
        body, name="mix_dnorm", grid=(t // tm,),
        in_specs=in_specs,
        out_specs=[row] + yspecs + [pl.BlockSpec((8, k), lambda i: (0, 0))],
        out_shape=[SDS((t, d), BF16)] + [SDS((t, wd), F32) for wd in widths] + [SDS((8, k), F32)],
        compiler_params=_params("arbitrary"),
    )(*args)


def _in_dnorm(dps, w_in_t, x, g, dres, tm):
    t, d = x.shape
    widths = [p.shape[1] for p in dps]
    offs = [sum(widths[:p]) for p in range(len(dps))]
    n_in = w_in_t.shape[0]

    def body(*refs):
        p_refs = refs[:len(dps)]
        w_ref, x_ref, g_ref, dres_ref, dx_ref, dxb_ref, dg_ref = refs[len(dps):]

        @pl.when(pl.program_id(0) == 0)
        def _():
            dg_ref[...] = jnp.zeros_like(dg_ref)

        acc = None
        for p_ref, off, wd in zip(p_refs, offs, widths):
            term = _nn(p_ref[...], w_ref[off:off + wd, :])
            acc = term if acc is None else acc + term
        dx, dg = _rms_bwd(acc, x_ref[...], g_ref[...])
        dx = dres_ref[...] + dx
        dx_ref[...] = dx
        dxb_ref[...] = dx.astype(BF16)
        dg_ref[0:1, :] += dg

    row = pl.BlockSpec((tm, d), lambda i: (i, 0))
    return pl.pallas_call(
        body, name="in_dnorm", grid=(t // tm,),
        in_specs=[pl.BlockSpec((tm, wd), lambda i: (i, 0)) for wd in widths] + [
            pl.BlockSpec((n_in, d), lambda i: (0, 0)), row, pl.BlockSpec((1, d), lambda i: (0, 0)), row],
        out_specs=[row, row, pl.BlockSpec((8, d), lambda i: (0, 0))],
        out_shape=[SDS((t, d), F32), SDS((t, d), BF16), SDS((8, d), F32)],
        compiler_params=_params("arbitrary"),
    )(*dps, w_in_t, x, g, dres)


def _tile(n, want):
    return min(n, want)


def _row_tile(n, want):
    return max(k for k in range(8, min(n, want) + 1, 8) if n % k == 0)


def _layer_fwd(x, p, fetch, after):
    t, d = x.shape
    aw, kv, cw, sw = d // 2, d // 8, d // 4, d // 4
    tm = _tile(t, ROWS_WIDE)
    w_in = fetch("w_in", after)
    h1, proj = _rms_matmul(x, p["ln1_g"], w_in, tm, COLS if w_in.shape[0] % COLS == 0 else COLS // 2, False, "in_proj",
                           transposed=True)
    y_attn, probs, sink_p = _attn_fwd(proj, p["qg"], p["kg"], p["sinks"], aw, kv)
    y_conv, z_conv = _conv_fwd(proj, p["conv_w"], p["conv_b"], p["conv_ln_g"], p["conv_ln_b"], cw)
    y_sgu = _sgu_fwd(proj, p["sgu_ln_g"], p["sgu_ln_b"], p["sgu_w"], p["sgu_bias"], sw)
    ys = [y_attn, y_conv, y_sgu]
    w_out = fetch("w_out", y_sgu)
    mix, x1 = _group_rms_matmul(ys, p["out_norm_g"], w_out, x, tm, _tile(d, COLS_WIDE // 2))
    w_up = fetch("w_up", x1)
    h2, act, r = _rms_matmul(x1, p["ln2_g"], w_up, tm, w_up.shape[2], True, "mlp_up")
    w_down = fetch("w_down", act)
    x2 = _matmul_res(act, w_down, x1, tm, _tile(d, COLS_WIDE), COLS)
    saved = dict(x=x, h1=h1, proj=proj, ys=ys, mix=mix, x1=x1, h2=h2, act=act, r=r,
                 w_in=w_in, w_out=w_out, w_up=w_up, w_down=w_down, probs=probs, sink_p=sink_p, z_conv=z_conv)
    return x2, saved


def _layer_bwd(dx2, dx2b, p, s, start, finish, share, carry):
    t, d = dx2.shape
    aw, kv, cw, sw = d // 2, d // 8, d // 4, d // 4
    tm = _tile(t, ROWS_F32)
    dup = _mlp_dact(dx2b, s["w_down"], s["r"], COLS, carry[0] if carry else None)
    tok = carry[1](dup) if carry else None
    half = functools.partial
    f = s["act"].shape[1]
    tok = start("w_down", half(_grad_w_half, s["act"], dx2b, _tile(f // N_DEV, COLS), _tile(d, COLS_WIDE), "grad_w_down", False), tok)
    dx1, d_ln2 = _mlp_dnorm(dup, s["w_up"], s["x1"], p["ln2_g"], dx2, tm, tok)
    tok = finish("w_down", dx1)
    tok = start("w_up", half(_grad_w_half, s["h2"], dup, _tile(d, ROWS_WIDE), s["w_up"].shape[2], "grad_w_up", True), tok)
    dx1b, dya, dyc, dys, d_onorm = _mix_dnorm(dx1, s["w_out"], s["ys"], p["out_norm_g"], _tile(t, ROWS_NORM), tok)
    tok = finish("w_up", dx1b)
    dsgu, d_sw, d_sb, d_svec = _sgu_bwd(s["proj"], dys, p["sgu_ln_g"], p["sgu_ln_b"], p["sgu_w"], p["sgu_bias"], sw, tok)
    share("sgu_w", d_sw)
    tok = start("w_out", half(_grad_w_half, s["mix"], dx1b, _tile(d // N_DEV, COLS), _tile(d, COLS_WIDE), "grad_w_out", False), dsgu)
    dqkv, d_attn = _attn_bwd(s["proj"], dya, s["probs"], s["sink_p"], p["qg"], p["kg"], p["sinks"], aw, kv, tok)
    tok = finish("w_out", dqkv)
    dconv, d_cw, d_cvec = _conv_bwd(s["proj"], dyc, s["z_conv"], p["conv_w"], p["conv_b"], p["conv_ln_g"], p["conv_ln_b"], cw)
    dps = [dqkv, dconv, dsgu]
    dx, dxb, d_ln1 = _in_dnorm(dps, s["w_in"], s["x"], p["ln1_g"], dx1, _tile(t, ROWS_NORM))
    heads = sw // HEAD
    share("rest", dict(
        ln1_g=d_ln1[0], q_norm_g=d_attn[0, :HEAD], k_norm_g=d_attn[1, :HEAD], sinks=d_attn[2, :aw // HEAD],
        conv_w=d_cw[:CONV_K], conv_b=d_cvec[0], conv_ln_g=d_cvec[1], conv_ln_b=d_cvec[2],
        sgu_ln_g=d_svec[0], sgu_ln_b=d_svec[1], sgu_b=d_sb[:, :heads].T,
        out_norm_g=d_onorm[0], ln2_g=d_ln2[0]))
    tm_in = COLS if all(dp.shape[1] % COLS == 0 for dp in dps) else COLS // 2
    tok = start("w_in", _grad_w_stacked(dps, s["h1"], tm_in, "grad_w_in", tok), None)
    carry = (tok, functools.partial(finish, "w_in"))
    return dx, dxb, carry


def _layer_params(l, small):
    row = lambda v: v[l][None, :]
    two = lambda v: jnp.tile(v[l], 2)[None, :]
    return dict(
        ln1_g=row(small["ln1_g"]), ln2_g=row(small["ln2_g"]), out_norm_g=row(small["out_norm_g"]),
        qg=two(small["q_norm_g"]), kg=two(small["k_norm_g"]), sinks=small["sinks"][l],
        conv_w=jnp.pad(small["conv_w"][l], ((0, HALO - CONV_K), (0, 0))),
        conv_b=row(small["conv_b"]), conv_ln_g=row(small["conv_ln_g"]), conv_ln_b=row(small["conv_ln_b"]),
        sgu_ln_g=row(small["sgu_ln_g"]), sgu_ln_b=row(small["sgu_ln_b"]), sgu_w=small["sgu_w"][l],
        sgu_bias=jnp.repeat(small["sgu_b"][l].T, HEAD, axis=1),
    )


def _local_step(x, target, small, depth, fetch, start, finish, share, after):
    params = [_layer_params(l, small) for l in range(depth)]
    saved = []
    h = x
    for l in range(depth):
        h, s = _layer_fwd(h, params[l], functools.partial(fetch, l), after)
        after = h
        saved.append(s)
    dy, dyb, lsum = _loss_grad(h, target, _tile(x.shape[0], ROWS_F32))
    def share_with_loss(kind, grads):
        share(depth - 1, kind, dict(grads, loss=lsum[0, 0:1]) if kind == "rest" else grads)

    carry = None
    for l in reversed(range(depth)):
        dy, dyb, carry = _layer_bwd(dy, dyb, params[l], saved[l], functools.partial(start, l), functools.partial(finish, l),
                                    share_with_loss if l == depth - 1 else functools.partial(share, l), carry)
    return dy, carry


BIG = ("w_in", "w_out", "w_up", "w_down")
HBM = pl.BlockSpec(memory_space=pltpu.HBM)
SEMS = pl.BlockSpec(memory_space=pltpu.SEMAPHORE)
EFFECT = pltpu.SideEffectType.DATAFLOW_SIDE_EFFECTING


def _mesh_pos():
    return lax.axis_index("x"), lax.axis_index("y"), lax.axis_index("c")


def _other_chips(x, y):
    return [(1 - x, y), (x, 1 - y), (1 - x, 1 - y)]


def _copies(plan, refs, sends, recvs):
    x, y, c = _mesh_pos()
    return [pltpu.make_async_remote_copy(src_ref=src, dst_ref=dst, send_sem=sends.at[k], recv_sem=recvs.at[k],
                                         device_id=dev, device_id_type=MESH)
            for k, (src, dst, dev) in enumerate(plan(x, y, c, refs))]


def _gather_plan(x, y, c, refs):
    mine = refs[0].at[4 * x + 2 * y + c]
    return [(mine, mine, (x, y, 1 - c))] + [(mine, mine, (*chip, c)) for chip in _other_chips(x, y)]


def _all_plan(x, y, c, refs):
    mine = refs[0].at[4 * x + 2 * y + c]
    return [(mine, mine, (x ^ (k >> 2), y ^ ((k >> 1) & 1), c ^ (k & 1))) for k in range(1, N_DEV)]


def _pair_plan(x, y, c, refs):
    blocks, land = refs
    return [(blocks.at[2 * q + (1 - c)], land.at[q], (x, y, 1 - c)) for q in range(4)]


def _half_plan(x, y, c, refs):
    blocks, land = refs
    return [(blocks.at[q], land.at[q], (x, y, 1 - c)) for q in range(4)]


def _chips_plan(x, y, c, refs):
    sums, land = refs
    return [(sums.at[2 * chip[0] + chip[1]], land.at[k], (*chip, c)) for k, chip in enumerate(_other_chips(x, y))]


def _start_exchanges(name, groups, after=None):
    flat = [a for arrays, _, _ in groups for a in arrays]
    n_arr, n_g = len(flat), len(groups)
    n_in = n_arr + (after is not None)

    def body(*refs):
        ins, sems, token = refs[:n_arr], refs[n_in:n_in + 2 * n_g], refs[-1]
        off = 0
        for gi, (arrays, plan, _) in enumerate(groups):
            for cp in _copies(plan, ins[off:off + len(arrays)], sems[2 * gi], sems[2 * gi + 1]):
                cp.start()
            off += len(arrays)
        token[...] = jnp.zeros_like(token)

    res = pl.pallas_call(
        body, name=name,
        out_shape=[pltpu.SemaphoreType.DMA((n,)) for _, _, n in groups for _ in (0, 1)]
        + [pltpu.HBM(a.shape, a.dtype) for a in flat] + [SDS((8, BLK), F32)],
        in_specs=[HBM] * n_arr + [ANY] * (after is not None),
        out_specs=[SEMS] * (2 * n_g) + [HBM] * n_arr + [pl.BlockSpec(memory_space=pltpu.VMEM)],
        input_output_aliases={i: 2 * n_g + i for i in range(n_arr)},
        compiler_params=pltpu.CompilerParams(has_side_effects=EFFECT),
    )(*[pltpu.with_memory_space_constraint(a, pltpu.HBM) for a in flat], *([after] if after is not None else []))
    sems, thru, token = res[:2 * n_g], res[2 * n_g:2 * n_g + n_arr], res[-1]
    out, off = [], 0
    for gi, (arrays, _, _) in enumerate(groups):
        out.append((list(thru[off:off + len(arrays)]), sems[2 * gi], sems[2 * gi + 1]))
        off += len(arrays)
    return out, token


def _wait_exchange(name, arrays, sends, recvs, plan, after):
    n = len(arrays)

    def body(*refs):
        for cp in _copies(plan, refs[:n], refs[n], refs[n + 1]):
            cp.wait_send()
            cp.wait_recv()

    return pl.pallas_call(
        body, name=name,
        out_shape=[pltpu.HBM(a.shape, a.dtype) for a in arrays],
        in_specs=[HBM] * n + [SEMS, SEMS, ANY],
        out_specs=[HBM] * n,
        input_output_aliases={i: i for i in range(n)},
        compiler_params=pltpu.CompilerParams(has_side_effects=EFFECT),
    )(*arrays, sends, recvs, after)


def _wait_exchanges(name, groups, after):
    flat = [a for (arrays, _, _), _ in groups for a in arrays]
    n_arr, n_g = len(flat), len(groups)

    def body(*refs):
        off = 0
        for gi, ((arrays, _, _), plan) in enumerate(groups):
            for cp in _copies(plan, refs[off:off + len(arrays)], refs[n_arr + 2 * gi], refs[n_arr + 2 * gi + 1]):
                cp.wait_send()
                cp.wait_recv()
            off += len(arrays)

    res = pl.pallas_call(
        body, name=name,
        out_shape=[pltpu.HBM(a.shape, a.dtype) for a in flat],
        in_specs=[HBM] * n_arr + [SEMS] * (2 * n_g) + [ANY],
        out_specs=[HBM] * n_arr,
        input_output_aliases={i: i for i in range(n_arr)},
        compiler_params=pltpu.CompilerParams(has_side_effects=EFFECT),
    )(*flat, *[s for (_, sends, recvs), _ in groups for s in (sends, recvs)], after)
    out, off = [], 0
    for (arrays, _, _), _ in groups:
        out.append(list(res[off:off + len(arrays)]))
        off += len(arrays)
    return out


def _gather_finish(land, name):
    def body(land_ref, out_ref, send_sems, recv_sems):
        del land_ref
        x, y, c = _mesh_pos()
        cps = []
        for k, chip in enumerate(_other_chips(x, y)):
            block = out_ref.at[4 * chip[0] + 2 * chip[1] + c]
            cps.append(pltpu.make_async_remote_copy(
                src_ref=block, dst_ref=block, send_sem=send_sems.at[k], recv_sem=recv_sems.at[k],
                device_id=(x, y, 1 - c), device_id_type=MESH))
        for cp in cps:
            cp.start()
        for cp in cps:
            cp.wait()

    return pl.pallas_call(
        body, name=name,
        in_specs=[ANY], out_specs=ANY,
        out_shape=SDS(land.shape, land.dtype),
        input_output_aliases={0: 0},
        scratch_shapes=[pltpu.SemaphoreType.DMA((3,)), pltpu.SemaphoreType.DMA((3,))],
    )(land)


def _pair_add(own, got, c, name):
    _, r, cols = own.shape
    tr = _row_tile(r, ROWS_F32)

    def body(c_ref, own_ref, got_ref, o_ref):
        o_ref[...] = (own_ref[...].astype(F32) + got_ref[...].astype(F32)).astype(BF16)

    return pl.pallas_call(
        body, name=name,
        grid_spec=pltpu.PrefetchScalarGridSpec(
            num_scalar_prefetch=1, grid=(4, r // tr),
            in_specs=[pl.BlockSpec((None, tr, cols), lambda q, i, c_ref: (2 * q + c_ref[0], i, 0)),
                      pl.BlockSpec((None, tr, cols), lambda q, i, c_ref: (q, i, 0))],
            out_specs=pl.BlockSpec((None, tr, cols), lambda q, i, c_ref: (q, i, 0))),
        out_shape=SDS((4, r, cols), BF16),
        compiler_params=_params("arbitrary", "arbitrary"),
    )(c, own, got)


def _adamw(w, g, m, v):
    m = ADAM_B1 * m + (1.0 - ADAM_B1) * g
    v = ADAM_B2 * v + (1.0 - ADAM_B2) * (g * g)
    m_hat = m / (1.0 - ADAM_B1 ** ADAM_STEP)
    v_hat = v / (1.0 - ADAM_B2 ** ADAM_STEP)
    delta = -ADAM_LR * (m_hat / (jnp.sqrt(v_hat) + ADAM_EPS) + ADAM_WD * w)
    return delta, m, v


def _adamw_layer(chip_sum, got, chip, w, m, v, layer, prev, name, after=None):
    depth, r, cols = w.shape
    tr = _row_tile(r, ROWS_NORM)

    def body(chip_ref, sum_ref, got_ref, w_ref, m_ref, v_ref, *rest):
        g_out, d_out, m_out, v_out, token = rest[-5:]
        g = sum_ref[...].astype(F32) + got_ref[0].astype(F32) + got_ref[1].astype(F32) + got_ref[2].astype(F32)
        delta, mm, vv = _adamw(w_ref[...], g, m_ref[...], v_ref[...])
        g_out[...] = g
        d_out[...] = delta
        m_out[...] = mm
        v_out[...] = vv
        token[...] = jnp.zeros_like(token)

    shard = pl.BlockSpec((None, tr, cols), lambda i, chip_ref: (layer, i, 0))
    in_specs = [pl.BlockSpec((None, tr, cols), lambda i, chip_ref: (chip_ref[0], i, 0)),
                pl.BlockSpec((3, tr, cols), lambda i, chip_ref: (0, i, 0)), shard, shard, shard]
    args = [chip, chip_sum, got, w, m, v]
    aliases = {}
    if prev is not None:
        in_specs += [ANY] * 4
        aliases = {len(args) + k: k for k in range(4)}
        args += list(prev)
    if after is not None:
        in_specs, args = in_specs + [ANY], args + [after]
    res = pl.pallas_call(
        body, name=name,
        grid_spec=pltpu.PrefetchScalarGridSpec(
            num_scalar_prefetch=1, grid=(r // tr,), in_specs=in_specs,
            out_specs=[shard] * 4 + [pl.BlockSpec((8, BLK), lambda i, chip_ref: (0, 0))]),
        out_shape=[SDS(w.shape, F32)] * 4 + [SDS((8, BLK), F32)],
        input_output_aliases=aliases,
        compiler_params=_params("arbitrary"),
    )(*args)
    return list(res[:4]), res[4]


WIDE = ("ln1_g", "out_norm_g", "ln2_g", "conv_b", "conv_ln_g", "conv_ln_b", "sgu_ln_g", "sgu_ln_b")
NARROW = ("q_norm_g", "k_norm_g", "sinks")


def _small_rows(w):
    rows, r = {}, 0
    for n in WIDE:
        rows[n] = (r, w[n].shape[1] // BLK)
        r += rows[n][1]
    for n in NARROW + ("loss",):
        rows[n] = (r, 1)
        r += 1
    r = -(-r // 8) * 8
    rows["sgu_b"] = (r, w["sgu_b"].shape[1])
    r += -(-rows["sgu_b"][1] // 8) * 8
    rows["conv_w"] = (r, N_DEV * HALO)
    return rows, r + N_DEV * HALO


def _pack_small(small, rows, total):
    parts, r = [], 0

    def put(name, block):
        nonlocal r
        first = rows[name][0]
        if first > r:
            parts.append(jnp.zeros((first - r, BLK), F32))
        parts.append(block)
        r = first + block.shape[0]

    for n in WIDE:
        put(n, small[n].reshape(-1, BLK))
    for n in NARROW:
        put(n, jnp.pad(small[n], (0, BLK - small[n].shape[0]))[None])
    if "loss" in small:
        put("loss", jnp.pad(small["loss"], (0, BLK - 1))[None])
    put("sgu_b", small["sgu_b"])
    cw = small["conv_w"]
    per_dev = cw.shape[1] // N_DEV
    blocks = jnp.transpose(cw.reshape(CONV_K, N_DEV, per_dev), (1, 0, 2))
    put("conv_w", jnp.pad(blocks, ((0, 0), (0, HALO - CONV_K), (0, BLK - per_dev))).reshape(N_DEV * HALO, BLK))
    if total > r:
        parts.append(jnp.zeros((total - r, BLK), F32))
    return jnp.concatenate(parts)


def _small_update(me, packed, sgu_w_all, w, m, v, rows):
    depth = len(packed)
    total = packed[0].shape[1]
    names = SMALL + ("conv_w",)
    heads = w["sgu_w"].shape[1]
    per_dev = w["conv_w"].shape[2]

    def body(me_ref, *refs):
        packed_refs, sgu_refs = refs[:depth], refs[depth:2 * depth]
        ins = refs[2 * depth:2 * depth + 3 * len(names)]
        outs = refs[2 * depth + 3 * len(names):2 * depth + 7 * len(names)]
        loss_out, acc, conv = refs[-3:]
        io = {n: (ins[3 * i:3 * i + 3], outs[4 * i:4 * i + 4]) for i, n in enumerate(names)}

        def update(n):
            (w_ref, m_ref, v_ref), (g_out, d_out, m_out, v_out) = io[n]
            delta, mm, vv = _adamw(w_ref[...], g_out[...], m_ref[...], v_ref[...])
            d_out[...] = delta
            m_out[...] = mm
            v_out[...] = vv

        mine = pl.ds(pl.multiple_of(rows["conv_w"][0] + HALO * me_ref[0], 8), HALO)
        for l in range(depth):
            s = packed_refs[l][0]
            c = packed_refs[l][0, mine, :]
            for k in range(1, N_DEV):
                s = s + packed_refs[l][k]
                c = c + packed_refs[l][k, mine, :]
            acc[l] = s
            conv[l] = c
        loss_out[...] = acc[depth - 1, rows["loss"][0]:rows["loss"][0] + 1, :]
        for n in WIDE:
            first, nr = rows[n]
            for l in range(depth):
                for j in range(nr):
                    io[n][1][0][l:l + 1, BLK * j:BLK * (j + 1)] = acc[l, first + j:first + j + 1, :]
            update(n)
        for n in NARROW:
            first, lanes = rows[n][0], w[n].shape[1]
            for l in range(depth):
                io[n][1][0][l:l + 1, :] = acc[l, first:first + 1, 0:lanes]
            update(n)
        first, nr = rows["sgu_b"]
        for l in range(depth):
            io["sgu_b"][1][0][l] = acc[l, first:first + nr, :]
            io["conv_w"][1][0][l] = conv[l, 0:CONV_K, 0:per_dev]
        update("sgu_b")
        update("conv_w")
        (w_ref, m_ref, v_ref), (g_out, d_out, m_out, v_out) = io["sgu_w"]
        for l in range(depth):
            for h in range(heads):
                g = sgu_refs[l][0, h]
                for k in range(1, N_DEV):
                    g = g + sgu_refs[l][k, h]
                delta, mm, vv = _adamw(w_ref[l, h], g, m_ref[l, h], v_ref[l, h])
                g_out[l, h] = g
                d_out[l, h] = delta
                m_out[l, h] = mm
                v_out[l, h] = vv

    def whole(a):
        nd = len(a.shape)
        return pl.BlockSpec(a.shape, lambda i, me_ref: (0,) * nd)

    small_in = [t[n] for n in names for t in (w, m, v)]
    res = pl.pallas_call(
        body, name="small_update",
        grid_spec=pltpu.PrefetchScalarGridSpec(
            num_scalar_prefetch=1, grid=(1,),
            in_specs=[whole(a) for a in list(packed) + list(sgu_w_all) + small_in],
            out_specs=[whole(w[n]) for n in names for _ in range(4)] + [pl.BlockSpec((1, BLK), lambda i, me_ref: (0, 0))],
            scratch_shapes=[pltpu.VMEM((depth, total, BLK), F32), pltpu.VMEM((depth, HALO, BLK), F32)]),
        out_shape=[SDS(w[n].shape, F32) for n in names for _ in range(4)] + [SDS((1, BLK), F32)],
        compiler_params=_params("arbitrary"),
    )(me, *packed, *sgu_w_all, *small_in)
    return {n: list(res[4 * i:4 * i + 4]) for i, n in enumerate(names)}, res[-1][0, 0]


def _pack(arrays):
    flat = jnp.concatenate([a.reshape(-1) for a in arrays])
    rows = -(-flat.shape[0] // (8 * BLK)) * 8
    return jnp.pad(flat, (0, rows * BLK - flat.shape[0])).reshape(rows, BLK)


def _unpack(packed, like):
    flat = packed.reshape(-1)
    out, off = [], 0
    for a in like:
        out.append(flat[off:off + a.size].reshape(a.shape))
        off += a.size
    return out


SMALL = ("ln1_g", "q_norm_g", "k_norm_g", "sinks", "conv_b", "conv_ln_g", "conv_ln_b", "sgu_ln_g", "sgu_ln_b",
         "sgu_w", "sgu_b", "out_norm_g", "ln2_g")
ORDER = ("ln1_g", "w_in", "q_norm_g", "k_norm_g", "sinks", "conv_w", "conv_b", "conv_ln_g", "conv_ln_b", "sgu_ln_g",
         "sgu_ln_b", "sgu_w", "sgu_b", "out_norm_g", "w_out", "ln2_g", "w_up", "w_down")


def _step(x, target, w, m, v):
    depth = w["ln1_g"].shape[0]
    xpos, ypos, cpos = _mesh_pos()
    me = 4 * xpos + 2 * ypos + cpos
    c_arr = jnp.reshape(cpos, (1,)).astype(jnp.int32)
    chip_arr = jnp.reshape(2 * xpos + ypos, (1,)).astype(jnp.int32)

    d = x.shape[1]
    def own_block(shard):
        return lax.dynamic_update_slice_in_dim(lax.empty((N_DEV,) + shard.shape, shard.dtype), shard[None], me, axis=0)

    cw = w["conv_w"]
    order = [(0, "conv_w")] + [(l, n) for l in range(depth) for n in BIG]
    started, gather_token = _start_exchanges("gather_start", [
        ([own_block(_pack([cw]) if n == "conv_w" else w[n][l].astype(BF16))], _gather_plan, 4) for l, n in order])
    pending = dict(zip(order, started))

    def fetch(l, n, after):
        arrays, sends, recvs = pending.pop((l, n))
        land, = _wait_exchange(f"gather_wait_{l}_{n}", arrays, sends, recvs, _gather_plan,
                               gather_token if after is None else after)
        full = _gather_finish(land, "gather_finish_" + n)
        return full if n in ("w_up", "conv_w") else full.reshape(-1, d)

    cw_all = fetch(0, "conv_w", None)
    cw_full = jnp.concatenate([_unpack(cw_all[k], [cw])[0] for k in range(N_DEV)], axis=-1)
    small, _ = lax.optimization_barrier(({n: w[n] for n in SMALL}, gather_token))
    small["conv_w"] = cw_full

    to_sibling, inflight, shared, leaving = {}, [], {}, []
    rows, total = _small_rows(w)

    def flush(name, first=()):
        groups = list(first) + [g for _, g in leaving]
        started, token = _start_exchanges(name, groups)
        for (record, _), going in zip(leaving, started[len(first):]):
            record(going)
        leaving.clear()
        return started[:len(first)], token

    def start(l, n, grad, after):
        if callable(grad):
            blocks, plan = grad(sel=1 - c_arr, after=after), _half_plan
        else:
            blocks, plan = grad.reshape(N_DEV, -1, d), _pair_plan
        (going,), token = flush(f"grads_start_{l}_{n}", [([blocks, lax.empty((4,) + blocks.shape[1:], BF16)], plan, 4)])
        to_sibling[l, n] = (going, plan, grad)
        return token

    def finish(l, n, after):
        (arrays, sends, recvs), plan, grad = to_sibling.pop((l, n))
        blocks, from_sibling = _wait_exchange(f"pair_wait_{l}_{n}", arrays, sends, recvs, plan, after)
        if callable(grad):
            chip_sums = grad(sel=c_arr, add=from_sibling)
        else:
            chip_sums = _pair_add(blocks, from_sibling, c_arr, "pair_add_" + n)
        leaving.append((lambda going: inflight.append((l, n, going)),
                        ([chip_sums, lax.empty((3,) + chip_sums.shape[1:], BF16)], _chips_plan, 3)))
        return chip_sums

    def share(l, kind, grads):
        block = grads if kind == "sgu_w" else _pack_small(grads, rows, total)
        leaving.append((lambda going: shared.__setitem__((l, kind), going), ([own_block(block)], _all_plan, N_DEV - 1)))

    grad_x, (after, finish_last) = _local_step(x, target, small, depth, fetch, start, finish, share, None)

    as3d = lambda a: a.reshape(depth, -1, a.shape[-1])
    results = {n: None for n in BIG}

    landed = {}

    def await_layer(name, reduces, after):
        smalls = sorted(k for k in shared if k not in landed and all(k[0] >= l for l, _, _ in reduces))
        got = _wait_exchanges(name, [(going, _chips_plan) for _, _, going in reduces]
                              + [(shared[k], _all_plan) for k in smalls], after)
        for k, (arr,) in zip(smalls, got[len(reduces):]):
            landed[k] = arr
        return got[:len(reduces)]

    def update(l, n, chip_sums, got, after):
        results[n], token = _adamw_layer(chip_sums, got, chip_arr, as3d(w[n]), as3d(m[n]), as3d(v[n]), l, results[n],
                                         f"adamw_{l}_{n}", after)
        return token

    first, rest = inflight[:4], inflight[4:]
    arrived = await_layer("grads_wait_first", first, after)
    after = None
    for (l, n, _), (chip_sums, got) in list(zip(first, arrived))[:2]:
        after = update(l, n, chip_sums, got, after)
    finish_last(after)
    _, after = flush("grads_start_last")
    for (l, n, _), (chip_sums, got) in list(zip(first, arrived))[2:]:
        after = update(l, n, chip_sums, got, after)
    arrived = await_layer("grads_wait_rest", rest, after)
    out, loss = _small_update(jnp.reshape(me, (1,)).astype(jnp.int32), [landed[l, "rest"] for l in range(depth)],
                              [landed[l, "sgu_w"] for l in range(depth)], w, m, v, rows)
    after = out["sinks"][1]
    for (l, n, _), (chip_sums, got) in zip(rest, arrived):
        after = update(l, n, chip_sums, got, after)
    (chip_sums, got), = await_layer("grads_wait_last", inflight[-1:], after)
    l, n, _ = inflight[-1]
    update(l, n, chip_sums, got, None)
    out.update({n: [r.reshape(w[n].shape) for r in results[n]] for n in BIG})
    return (loss, grad_x[None]) + tuple(out[n][k] for k in range(4) for n in ORDER)


def kernel(x, ln1_g, w_in, q_norm_g, k_norm_g, sinks, conv_w, conv_b, conv_ln_g, conv_ln_b, sgu_ln_g, sgu_ln_b, sgu_w, sgu_b, out_norm_g, w_out, ln2_g, w_up, w_down, loss_target, m_ln1_g, m_w_in, m_q_norm_g, m_k_norm_g, m_sinks, m_conv_w, m_conv_b, m_conv_ln_g, m_conv_ln_b, m_sgu_ln_g, m_sgu_ln_b, m_sgu_w, m_sgu_b, m_out_norm_g, m_w_out, m_ln2_g, m_w_up, m_w_down, v_ln1_g, v_w_in, v_q_norm_g, v_k_norm_g, v_sinks, v_conv_w, v_conv_b, v_conv_ln_g, v_conv_ln_b, v_sgu_ln_g, v_sgu_ln_b, v_sgu_w, v_sgu_b, v_out_norm_g, v_w_out, v_ln2_g, v_w_up, v_w_down):
    w = dict(zip(ORDER, (ln1_g, w_in, q_norm_g, k_norm_g, sinks, conv_w, conv_b, conv_ln_g, conv_ln_b, sgu_ln_g, sgu_ln_b,
                         sgu_w, sgu_b, out_norm_g, w_out, ln2_g, w_up, w_down)))
    m = dict(zip(ORDER, (m_ln1_g, m_w_in, m_q_norm_g, m_k_norm_g, m_sinks, m_conv_w, m_conv_b, m_conv_ln_g, m_conv_ln_b,
                         m_sgu_ln_g, m_sgu_ln_b, m_sgu_w, m_sgu_b, m_out_norm_g, m_w_out, m_ln2_g, m_w_up, m_w_down)))
    v = dict(zip(ORDER, (v_ln1_g, v_w_in, v_q_norm_g, v_k_norm_g, v_sinks, v_conv_w, v_conv_b, v_conv_ln_g, v_conv_ln_b,
                         v_sgu_ln_g, v_sgu_ln_b, v_sgu_w, v_sgu_b, v_out_norm_g, v_w_out, v_ln2_g, v_w_up, v_w_down)))
    for group in (w, m, v):
        group["w_in"] = jnp.swapaxes(group["w_in"], 1, 2)
    out = list(_step(x[0], loss_target[0], w, m, v))
    for k in range(4):
        i = 2 + k * len(ORDER) + ORDER.index("w_in")
        out[i] = jnp.swapaxes(out[i], 1, 2)
    return tuple(out)
```

```python
import functools

import jax
import jax.numpy as jnp
from jax import lax
from jax.experimental import pallas as pl
from jax.experimental.pallas import tpu as pltpu

F32 = jnp.float32
BF16 = jnp.bfloat16
SDS = jax.ShapeDtypeStruct

EPS = 1e-6
NEG_INF = -1e30
HEAD = 64
BLK = 128
CONV_K = 31
HALO = 32
N_DEV = 8

ADAM_LR = 0.001
ADAM_B1 = 0.9
ADAM_B2 = 0.999
ADAM_EPS = 1e-08
ADAM_WD = 0.01
ADAM_STEP = 10

VMEM_LIMIT = 56 * 1024 * 1024

ROWS_WIDE = 1024
ROWS_F32 = 512
ROWS_NORM = 256
COLS = 512
COLS_WIDE = 2048

MESH = pl.DeviceIdType.MESH


def _params(*sem):
    return pltpu.CompilerParams(dimension_semantics=sem, vmem_limit_bytes=VMEM_LIMIT)


def _nt(a, b):
    return lax.dot_general(a, b, (((1,), (1,)), ((), ())), preferred_element_type=F32)


def _tn(a, b):
    return lax.dot_general(a, b, (((0,), (0,)), ((), ())), preferred_element_type=F32)


def _nn(a, b):
    return jnp.dot(a, b, preferred_element_type=F32)


def _sigmoid(x):
    return 1.0 / (1.0 + jnp.exp(-x))


ANY = pl.BlockSpec(memory_space=pl.ANY)


def _ordered_behind(body, n_in, in_specs, args, after):
    if after is None:
        return body, in_specs, args
    return (lambda *refs: body(*refs[:n_in], *refs[n_in + 1:])), list(in_specs) + [ANY], list(args) + [after]


def _segsum(x, first):
    head0 = jnp.where(first, x, 0.0)
    s0 = jnp.sum(head0, axis=-1, keepdims=True)
    s1 = jnp.sum(x - head0, axis=-1, keepdims=True)
    return jnp.where(first, s0, s1)


def _head_rms(x, gain, first):
    rstd = lax.rsqrt(_segsum(x * x, first) * (1.0 / HEAD) + EPS)
    xhat = x * rstd
    return xhat * gain, xhat, rstd


def _expand(x, odd, lo):
    if odd:
        xl = pltpu.roll(jnp.where(lo, 0.0, x), HEAD, axis=1)
    else:
        xl = jnp.where(lo, x, 0.0)
    xh = pltpu.roll(xl, HEAD, axis=1)
    return jnp.concatenate([xl, xh], axis=0).astype(BF16)


def _attn_mask(n):
    qi = lax.broadcasted_iota(jnp.int32, (BLK, 2 * BLK), 0)
    sj = lax.broadcasted_iota(jnp.int32, (BLK, 2 * BLK), 1)
    rel = qi + BLK - sj
    return (rel >= 0) & (rel < BLK) & ((sj >= BLK) | (n > 0))


def _attn_specs(t, aw, kv):
    prev = lambda n: jnp.maximum(n - 1, 0)
    kb, vb = aw // kv, aw // kv + 1
    return [
        pl.BlockSpec(memory_space=pltpu.SMEM),
        pl.BlockSpec((BLK, aw), lambda n: (n, 0)),
        pl.BlockSpec((BLK, kv), lambda n: (prev(n), kb)),
        pl.BlockSpec((BLK, kv), lambda n: (n, kb)),
        pl.BlockSpec((BLK, kv), lambda n: (prev(n), vb)),
        pl.BlockSpec((BLK, kv), lambda n: (n, vb)),
        pl.BlockSpec((1, BLK), lambda n: (0, 0)),
        pl.BlockSpec((1, BLK), lambda n: (0, 0)),
    ]


def _softmax_pair(s2, valid, sink0, sink1):
    out, psink = [], []
    for half, sink in ((0, sink0), (1, sink1)):
        s = jnp.where(valid, s2[:, 2 * BLK * half:2 * BLK * (half + 1)], NEG_INF)
        m = jnp.maximum(jnp.max(s, axis=-1, keepdims=True), sink)
        p = jnp.exp(s - m)
        es = jnp.exp(sink - m)
        inv = 1.0 / (jnp.sum(p, axis=-1, keepdims=True) + es)
        out.append(p * inv)
        psink.append(es * inv)
    return jnp.concatenate(out, axis=1), psink


def _attn_fwd(proj, qg, kg, sinks, aw, kv):
    t = proj.shape[0]
    n_pairs, n_kvblk = aw // BLK, kv // BLK

    def body(sink_ref, q_ref, kp_ref, kc_ref, vp_ref, vc_ref, qg_ref, kg_ref, o_ref, probs_ref, sink_p_ref):
        n = pl.program_id(0)
        lo = lax.broadcasted_iota(jnp.int32, (1, BLK), 1) < HEAD
        valid = _attn_mask(n)
        kraw = jnp.concatenate([kp_ref[...], kc_ref[...]], axis=0)
        vraw = jnp.concatenate([vp_ref[...], vc_ref[...]], axis=0)
        qn = [_head_rms(q_ref[:, BLK * p:BLK * (p + 1)], qg_ref[...], lo)[0].astype(BF16) for p in range(n_pairs)]
        k2, v2 = [], []
        for b in range(n_kvblk):
            kn = _head_rms(kraw[:, BLK * b:BLK * (b + 1)], kg_ref[...], lo)[0]
            for odd in (0, 1):
                k2.append(_expand(kn, odd, lo))
                v2.append(_expand(vraw[:, BLK * b:BLK * (b + 1)], odd, lo))
        s2 = [_nt(qn[p], k2[p // 2]) * (HEAD ** -0.5) for p in range(n_pairs)]
        soft = [_softmax_pair(s2[p], valid, sink_ref[2 * p], sink_ref[2 * p + 1]) for p in range(n_pairs)]
        lane = lax.broadcasted_iota(jnp.int32, (1, BLK), 1)
        sink_p = jnp.zeros((BLK, BLK), F32)
        for p in range(n_pairs):
            probs_ref[:, 4 * BLK * p:4 * BLK * (p + 1)] = soft[p][0]
            for half in (0, 1):
                sink_p = jnp.where(lane == 2 * p + half, soft[p][1][half], sink_p)
            o_ref[:, BLK * p:BLK * (p + 1)] = _nn(soft[p][0].astype(BF16), v2[p // 2])
        sink_p_ref[...] = sink_p

    return pl.pallas_call(
        body, name="attn_fwd", grid=(t // BLK,),
        in_specs=_attn_specs(t, aw, kv),
        out_specs=[pl.BlockSpec((BLK, aw), lambda n: (n, 0)), pl.BlockSpec((BLK, 4 * aw), lambda n: (n, 0)),
                   pl.BlockSpec((BLK, BLK), lambda n: (n, 0))],
        out_shape=[SDS((t, aw), F32), SDS((t, 4 * aw), F32), SDS((t, BLK), F32)],
        compiler_params=_params("arbitrary"),
    )(sinks, proj, proj, proj, proj, proj, qg, kg)


def _attn_bwd(proj, dy, probs, sink_p, qg, kg, sinks, aw, kv, after=None):
    t = proj.shape[0]
    nb = t // BLK
    n_kvblk = kv // BLK
    kb = aw // kv

    def body(sink_ref, q_ref, kp_ref, kc_ref, vp_ref, vc_ref, qg_ref, kg_ref, dy_ref, kall_ref, probs_ref, sink_p_ref,
             dqkv_ref, dstat_ref, dk_acc, dv_acc, dqg_acc):
        del sink_ref
        n = pl.program_id(0)
        lane = lax.broadcasted_iota(jnp.int32, (1, BLK), 1)
        lo = lane < HEAD

        @pl.when(n == 0)
        def _():
            dk_acc[...] = jnp.zeros_like(dk_acc)
            dv_acc[...] = jnp.zeros_like(dv_acc)
            dqg_acc[...] = jnp.zeros_like(dqg_acc)
            dstat_ref[...] = jnp.zeros_like(dstat_ref)

        kraw = jnp.concatenate([kp_ref[...], kc_ref[...]], axis=0)
        vraw = jnp.concatenate([vp_ref[...], vc_ref[...]], axis=0)
        row = pl.multiple_of(n * BLK, BLK)
        prow = pl.multiple_of(jnp.maximum(n - 1, 0) * BLK, BLK)
        pairs = range(aw // BLK)
        cols = [slice(BLK * p, BLK * (p + 1)) for p in pairs]
        qs = [_head_rms(q_ref[:, cols[p]], qg_ref[...], lo) for p in pairs]
        qb = [qs[p][0].astype(BF16) for p in pairs]
        k2, v2 = [], []
        for b in range(n_kvblk):
            kn = _head_rms(kraw[:, BLK * b:BLK * (b + 1)], kg_ref[...], lo)[0]
            for odd in (0, 1):
                k2.append(_expand(kn, odd, lo))
                v2.append(_expand(vraw[:, BLK * b:BLK * (b + 1)], odd, lo))
        p2 = [probs_ref[:, 4 * BLK * p:4 * BLK * (p + 1)] for p in pairs]
        dob = [dy_ref[:, cols[p]].astype(BF16) for p in pairs]
        dp2 = [_nt(dob[p], v2[p // 2]) for p in pairs]
        deltas = jnp.zeros((BLK, BLK), F32)
        ds2 = []
        for p in pairs:
            ds = []
            for half in (0, 1):
                hs = slice(2 * BLK * half, 2 * BLK * (half + 1))
                ph = p2[p][:, hs]
                delta = jnp.sum(ph * dp2[p][:, hs], axis=-1, keepdims=True)
                ds.append(ph * (dp2[p][:, hs] - delta))
                deltas = jnp.where(lane == 2 * p + half, delta, deltas)
            ds2.append((jnp.concatenate(ds, axis=1) * (HEAD ** -0.5)).astype(BF16))
        dstat_ref[2:3, :] -= jnp.sum(sink_p_ref[...] * deltas, axis=0, keepdims=True)
        dqn = [_nn(ds2[p], k2[p // 2]) for p in pairs]
        dk2t = [_tn(qb[p], ds2[p]) for p in pairs]
        dv2t = [_tn(dob[p], p2[p].astype(BF16)) for p in pairs]
        for p in pairs:
            _, qhat, rstd = qs[p]
            dqhat = dqn[p] * qg_ref[...]
            proj_q = _segsum(dqhat * qhat, lo) * (1.0 / HEAD)
            dqkv_ref[pl.ds(row, BLK), cols[p]] = (rstd * (dqhat - qhat * proj_q)).astype(BF16)
            dqg_acc[:, cols[p]] += jnp.sum(dqn[p] * qhat, axis=0, keepdims=True)
        def head_sum(parts, j):
            return sum(a[:HEAD, :2 * BLK] + a[HEAD:, 2 * BLK:] for a in parts[2 * j:2 * j + 2])

        for b in range(n_kvblk):
            dkn = jnp.concatenate([head_sum(dk2t, 2 * b), head_sum(dk2t, 2 * b + 1)], axis=0).T
            dvb = jnp.concatenate([head_sum(dv2t, 2 * b), head_sum(dv2t, 2 * b + 1)], axis=0).T
            kcols = slice(BLK * b, BLK * (b + 1))
            dk_acc[pl.ds(prow, BLK), kcols] += dkn[:BLK]
            dv_acc[pl.ds(prow, BLK), kcols] += dvb[:BLK]
            dk_acc[pl.ds(row, BLK), kcols] += dkn[BLK:]
            dv_acc[pl.ds(row, BLK), kcols] += dvb[BLK:]

        @pl.when(n == nb - 1)
        def _():
            dqg = dqg_acc[:, 0:BLK]
            for p in range(1, aw // BLK):
                dqg = dqg + dqg_acc[:, BLK * p:BLK * (p + 1)]
            dstat_ref[0:1, :] = dqg + pltpu.roll(dqg, HEAD, axis=1)

            def kblock(i, dkg):
                r = pl.multiple_of(i * BLK, BLK)
                for b in range(n_kvblk):
                    kcols = slice(BLK * b, BLK * (b + 1))
                    _, khat, rstd = _head_rms(kall_ref[pl.ds(r, BLK), kcols], kg_ref[...], lo)
                    dkn = dk_acc[pl.ds(r, BLK), kcols]
                    dkhat = dkn * kg_ref[...]
                    proj_k = _segsum(dkhat * khat, lo) * (1.0 / HEAD)
                    dqkv_ref[pl.ds(r, BLK), aw + BLK * b:aw + BLK * (b + 1)] = (rstd * (dkhat - khat * proj_k)).astype(BF16)
                    dqkv_ref[pl.ds(r, BLK), aw + kv + BLK * b:aw + kv + BLK * (b + 1)] = dv_acc[pl.ds(r, BLK), kcols].astype(BF16)
                    dkg = dkg + jnp.sum(dkn * khat, axis=0, keepdims=True)
                return dkg

            dkg = lax.fori_loop(0, nb, kblock, jnp.zeros((1, BLK), F32))
            dstat_ref[1:2, :] = dkg + pltpu.roll(dkg, HEAD, axis=1)

    in_specs = _attn_specs(t, aw, kv) + [
        pl.BlockSpec((BLK, aw), lambda n: (n, 0)),
        pl.BlockSpec((t, kv), lambda n: (0, kb)),
        pl.BlockSpec((BLK, 4 * aw), lambda n: (n, 0)),
        pl.BlockSpec((BLK, BLK), lambda n: (n, 0)),
    ]
    args = [sinks, proj, proj, proj, proj, proj, qg, kg, dy, proj, probs, sink_p]
    body, in_specs, args = _ordered_behind(body, len(args), in_specs, args, after)
    return pl.pallas_call(
        body, name="attn_bwd", grid=(nb,),
        in_specs=in_specs,
        out_specs=[pl.BlockSpec((t, aw + 2 * kv), lambda n: (0, 0)), pl.BlockSpec((8, BLK), lambda n: (0, 0))],
        out_shape=[SDS((t, aw + 2 * kv), BF16), SDS((8, BLK), F32)],
        scratch_shapes=[pltpu.VMEM((t, kv), F32), pltpu.VMEM((t, kv), F32), pltpu.VMEM((1, aw), F32)],
        compiler_params=_params("arbitrary"),
    )(*args)


def _conv_taps(win, w_ref, shift_of):
    rows = win.shape[0]
    acc = None
    for j in range(CONV_K):
        term = pltpu.roll(win, (rows - shift_of(j)) % rows, axis=0)[:BLK] * w_ref[j:j + 1, :]
        acc = term if acc is None else acc + term
    return acc


def _layer_norm_fwd(z):
    mu = jnp.mean(z, axis=-1, keepdims=True)
    zc = z - mu
    rstd = lax.rsqrt(jnp.mean(zc * zc, axis=-1, keepdims=True) + EPS)
    return zc * rstd, rstd


def _layer_norm_bwd(dy, yhat, rstd, g):
    dyh = dy * g
    return rstd * (dyh - jnp.mean(dyh, axis=-1, keepdims=True) - yhat * jnp.mean(dyh * yhat, axis=-1, keepdims=True))


def _conv_fill_glu(a_ref, g_ref, hpad, nb):
    hpad[0:HALO, :] = jnp.zeros((HALO, hpad.shape[1]), F32)

    def fill(i, c):
        r = pl.multiple_of(i * BLK, BLK)
        hpad[pl.ds(pl.multiple_of(r + HALO, HALO), BLK), :] = a_ref[pl.ds(r, BLK), :] * _sigmoid(g_ref[pl.ds(r, BLK), :])
        return c

    lax.fori_loop(0, nb, fill, 0)


def _conv_specs(t, cw, d_in):
    base = (d_in - 4 * cw) // cw
    vec = pl.BlockSpec((1, cw), lambda i: (0, 0))
    return [
        pl.BlockSpec((t, cw), lambda i: (0, base)),
        pl.BlockSpec((t, cw), lambda i: (0, base + 1)),
        pl.BlockSpec((HALO, cw), lambda i: (0, 0)),
        vec, vec, vec,
    ]


def _conv_fwd(proj, cw_pad, cb, lg, lb, cw):
    t, d_in = proj.shape
    nb = t // BLK

    def body(a_ref, g_ref, w_ref, b_ref, lg_ref, lb_ref, o_ref, z_ref, hpad):
        _conv_fill_glu(a_ref, g_ref, hpad, nb)

        def blk(i, c):
            r = pl.multiple_of(i * BLK, BLK)
            z = _conv_taps(hpad[pl.ds(r, BLK + HALO), :], w_ref, lambda j: j + HALO - (CONV_K - 1)) + b_ref[...]
            z_ref[pl.ds(r, BLK), :] = z
            yhat, _ = _layer_norm_fwd(z)
            y = yhat * lg_ref[...] + lb_ref[...]
            o_ref[pl.ds(r, BLK), :] = y * _sigmoid(y)
            return c

        lax.fori_loop(0, nb, blk, 0)

    whole = pl.BlockSpec((t, cw), lambda i: (0, 0))
    return pl.pallas_call(
        body, name="conv_fwd", grid=(1,),
        in_specs=_conv_specs(t, cw, d_in),
        out_specs=[whole, whole],
        out_shape=[SDS((t, cw), F32), SDS((t, cw), F32)],
        scratch_shapes=[pltpu.VMEM((t + HALO, cw), F32)],
        compiler_params=_params("arbitrary"),
    )(proj, proj, cw_pad, cb, lg, lb)


def _conv_bwd(proj, dy, z, cw_pad, cb, lg, lb, cw):
    t, d_in = proj.shape
    nb = t // BLK

    def body(a_ref, g_ref, w_ref, b_ref, lg_ref, lb_ref, dy_ref, z_ref, dc_ref, dw_ref, dvec_ref, hpad, dzpad, dwacc):
        del b_ref
        _conv_fill_glu(a_ref, g_ref, hpad, nb)
        dzpad[t:t + HALO, :] = jnp.zeros((HALO, cw), F32)
        dwacc[...] = jnp.zeros_like(dwacc)

        def blk(i, carry):
            db, dlg, dlb = carry
            r = pl.multiple_of(i * BLK, BLK)
            win = hpad[pl.ds(r, BLK + HALO), :]
            yhat, rstd = _layer_norm_fwd(z_ref[pl.ds(r, BLK), :])
            y = yhat * lg_ref[...] + lb_ref[...]
            sg = _sigmoid(y)
            dyl = dy_ref[pl.ds(r, BLK), :] * (sg * (1.0 + y * (1.0 - sg)))
            dz = _layer_norm_bwd(dyl, yhat, rstd, lg_ref[...])
            dzpad[pl.ds(r, BLK), :] = dz
            for j in range(CONV_K):
                sh = j + HALO - (CONV_K - 1)
                prod = dz * pltpu.roll(win, (BLK + HALO - sh) % (BLK + HALO), axis=0)[:BLK]
                dwacc[8 * j:8 * j + 8, :] += jnp.sum(prod.reshape(BLK // 8, 8, cw), axis=0)
            return (db + jnp.sum(dz, axis=0, keepdims=True),
                    dlg + jnp.sum(dyl * yhat, axis=0, keepdims=True),
                    dlb + jnp.sum(dyl, axis=0, keepdims=True))

        zero = jnp.zeros((1, cw), F32)
        db, dlg, dlb = lax.fori_loop(0, nb, blk, (zero, zero, zero))
        dvec_ref[...] = jnp.zeros_like(dvec_ref)
        dvec_ref[0:1, :] = db
        dvec_ref[1:2, :] = dlg
        dvec_ref[2:3, :] = dlb
        dw_ref[...] = jnp.sum(dwacc[...].reshape(HALO, 8, cw), axis=1)

        def blk2(i, c):
            r = pl.multiple_of(i * BLK, BLK)
            dh = _conv_taps(dzpad[pl.ds(r, BLK + HALO), :], w_ref, lambda j: CONV_K - 1 - j)
            a = a_ref[pl.ds(r, BLK), :]
            sg = _sigmoid(g_ref[pl.ds(r, BLK), :])
            dc_ref[pl.ds(r, BLK), 0:cw] = (dh * sg).astype(BF16)
            dc_ref[pl.ds(r, BLK), cw:2 * cw] = (dh * a * sg * (1.0 - sg)).astype(BF16)
            return c

        lax.fori_loop(0, nb, blk2, 0)

    return pl.pallas_call(
        body, name="conv_bwd", grid=(1,),
        in_specs=_conv_specs(t, cw, d_in) + [pl.BlockSpec((t, cw), lambda i: (0, 0))] * 2,
        out_specs=[pl.BlockSpec((t, 2 * cw), lambda i: (0, 0)), pl.BlockSpec((HALO, cw), lambda i: (0, 0)),
                   pl.BlockSpec((8, cw), lambda i: (0, 0))],
        out_shape=[SDS((t, 2 * cw), BF16), SDS((HALO, cw), F32), SDS((8, cw), F32)],
        scratch_shapes=[pltpu.VMEM((t + HALO, cw), F32), pltpu.VMEM((t + HALO, cw), F32), pltpu.VMEM((8 * HALO, cw), F32)],
        compiler_params=_params("arbitrary"),
    )(proj, proj, cw_pad, cb, lg, lb, dy, z)


def _tril_bf16(w):
    r = lax.broadcasted_iota(jnp.int32, (BLK, BLK), 0)
    c = lax.broadcasted_iota(jnp.int32, (BLK, BLK), 1)
    return jnp.where(r >= c, w, 0.0).astype(BF16)


def _sgu_specs(sw, d_in, heads):
    base = (d_in - 2 * sw) // sw
    vec = pl.BlockSpec((1, sw), lambda n: (0, 0))
    return [
        pl.BlockSpec((BLK, sw), lambda n: (n, base)),
        pl.BlockSpec((BLK, sw), lambda n: (n, base + 1)),
        vec, vec,
        pl.BlockSpec((heads, BLK, BLK), lambda n: (0, 0, 0)),
        pl.BlockSpec((BLK, sw), lambda n: (0, 0)),
    ]


def _sgu_mix(w_ref, vnb, heads, sw, transpose):
    head_of = lax.broadcasted_iota(jnp.int32, (1, sw), 1) // HEAD
    s = jnp.zeros((BLK, sw), F32)
    for h in range(heads):
        wt = _tril_bf16(w_ref[h])
        mixed = _tn(wt, vnb) if transpose else _nn(wt, vnb)
        s = jnp.where(head_of == h, mixed, s)
    return s


def _sgu_fwd(proj, lg, lb, w, bias_full, sw):
    t, d_in = proj.shape
    heads = sw // HEAD

    def body(u_ref, v_ref, lg_ref, lb_ref, w_ref, bias_ref, o_ref, s_ref):
        vhat, _ = _layer_norm_fwd(v_ref[...])
        vn = (vhat * lg_ref[...] + lb_ref[...]).astype(BF16)
        s = _sgu_mix(w_ref, vn, heads, sw, False) + bias_ref[...]
        s_ref[...] = s
        o_ref[...] = u_ref[...] * s

    tile = pl.BlockSpec((BLK, sw), lambda n: (n, 0))
    return pl.pallas_call(
        body, name="sgu_fwd", grid=(t // BLK,),
        in_specs=_sgu_specs(sw, d_in, heads),
        out_specs=[tile, tile],
        out_shape=[SDS((t, sw), F32), SDS((t, sw), F32)],
        compiler_params=_params("arbitrary"),
    )(proj, proj, lg, lb, w, bias_full)


def _sgu_bwd(proj, dy, gate, lg, lb, w, bias_full, sw, after=None):
    t, d_in = proj.shape
    heads = sw // HEAD
    nb = t // BLK

    def body(u_ref, v_ref, lg_ref, lb_ref, w_ref, bias_ref, dy_ref, s_ref, ds_ref, dw_ref, db_ref, dvec_ref, dbfull):
        del bias_ref
        n = pl.program_id(0)

        @pl.when(n == 0)
        def _():
            dw_ref[...] = jnp.zeros_like(dw_ref)
            dvec_ref[...] = jnp.zeros_like(dvec_ref)
            dbfull[...] = jnp.zeros_like(dbfull)

        vhat, rstd = _layer_norm_fwd(v_ref[...])
        vn = (vhat * lg_ref[...] + lb_ref[...]).astype(BF16)
        dy = dy_ref[...]
        ds_ref[:, 0:sw] = (dy * s_ref[...]).astype(BF16)
        dsv = dy * u_ref[...]
        dbfull[...] += dsv
        head_of = lax.broadcasted_iota(jnp.int32, (1, sw), 1) // HEAD
        r = lax.broadcasted_iota(jnp.int32, (BLK, BLK), 0)
        c = lax.broadcasted_iota(jnp.int32, (BLK, BLK), 1)
        dsb = dsv.astype(BF16)
        for h in range(heads):
            dsh = jnp.where(head_of == h, dsv, 0.0).astype(BF16)
            dw_ref[h] += jnp.where(r >= c, _nt(dsh, vn), 0.0)
        dvn = _sgu_mix(w_ref, dsb, heads, sw, True)
        dvec_ref[0:1, :] += jnp.sum(dvn * vhat, axis=0, keepdims=True)
        dvec_ref[1:2, :] += jnp.sum(dvn, axis=0, keepdims=True)
        ds_ref[:, sw:2 * sw] = _layer_norm_bwd(dvn, vhat, rstd, lg_ref[...]).astype(BF16)

        @pl.when(n == nb - 1)
        def _():
            sel = (lax.broadcasted_iota(jnp.int32, (sw, BLK), 0) // HEAD == lax.broadcasted_iota(jnp.int32, (sw, BLK), 1)).astype(BF16)
            x = dbfull[...]
            hi = x.astype(BF16)
            r1 = x - hi.astype(F32)
            mid = r1.astype(BF16)
            low = (r1 - mid.astype(F32)).astype(BF16)
            db_ref[...] = _nn(hi, sel) + _nn(mid, sel) + _nn(low, sel)

    in_specs = _sgu_specs(sw, d_in, heads) + [pl.BlockSpec((BLK, sw), lambda n: (n, 0))] * 2
    body, in_specs, args = _ordered_behind(body, 8, in_specs, [proj, proj, lg, lb, w, bias_full, dy, gate], after)
    return pl.pallas_call(
        body, name="sgu_bwd", grid=(nb,),
        in_specs=in_specs,
        out_specs=[pl.BlockSpec((BLK, 2 * sw), lambda n: (n, 0)), pl.BlockSpec((heads, BLK, BLK), lambda n: (0, 0, 0)),
                   pl.BlockSpec((BLK, BLK), lambda n: (0, 0)), pl.BlockSpec((8, sw), lambda n: (0, 0))],
        out_shape=[SDS((t, 2 * sw), BF16), SDS((heads, BLK, BLK), F32), SDS((BLK, BLK), F32), SDS((8, sw), F32)],
        scratch_shapes=[pltpu.VMEM((BLK, sw), F32)],
        compiler_params=_params("arbitrary"),
    )(*args)


def _rms_fwd(x, g):
    return (x * lax.rsqrt(jnp.mean(x * x, axis=-1, keepdims=True) + EPS)) * g


def _rms_bwd(dh, x, g):
    rstd = lax.rsqrt(jnp.mean(x * x, axis=-1, keepdims=True) + EPS)
    xhat = x * rstd
    dgx = dh * g
    dx = rstd * (dgx - xhat * jnp.mean(dgx * xhat, axis=-1, keepdims=True))
    return dx, jnp.sum(dh * xhat, axis=0, keepdims=True)


def _rms_matmul(x, g, w, tm, tn, relu2, name, transposed=False):
    t, d = x.shape
    if w.ndim == 3:
        assert w.shape[2] == tn
        n = w.shape[0] * tn
        w_spec = pl.BlockSpec((None, d, tn), lambda i, j: (j, 0, 0))
    elif transposed:
        n = w.shape[0]
        w_spec = pl.BlockSpec((tn, d), lambda i, j: (j, 0))
    else:
        n = w.shape[1]
        w_spec = pl.BlockSpec((d, tn), lambda i, j: (0, j))

    def body(x_ref, g_ref, w_ref, h_ref, *outs):
        @pl.when(pl.program_id(1) == 0)
        def _():
            h_ref[...] = _rms_fwd(x_ref[...], g_ref[...]).astype(BF16)

        acc = _nt(h_ref[...], w_ref[...]) if transposed else _nn(h_ref[...], w_ref[...])
        if relu2:
            r = jnp.maximum(acc, 0.0)
            outs[0][...] = (r * r).astype(BF16)
            outs[1][...] = r.astype(BF16)
        else:
            outs[0][...] = acc

    tile = pl.BlockSpec((tm, tn), lambda i, j: (i, j))
    row = pl.BlockSpec((tm, d), lambda i, j: (i, 0))
    outs = [SDS((t, n), BF16), SDS((t, n), BF16)] if relu2 else [SDS((t, n), F32)]
    return pl.pallas_call(
        body, name=name, grid=(t // tm, n // tn),
        in_specs=[row, pl.BlockSpec((1, d), lambda i, j: (0, 0)), w_spec],
        out_specs=[row] + [tile] * len(outs),
        out_shape=[SDS((t, d), BF16)] + outs,
        compiler_params=_params("parallel", "arbitrary"),
    )(x, g, w)


def _group_rms_matmul(ys, g, w, res, tm, tn):
    t, d = res.shape
    widths = [y.shape[1] for y in ys]
    k = sum(widths)

    def body(*refs):
        y_refs, (g_ref, w_ref, res_ref, mix_ref, o_ref) = refs[:len(ys)], refs[len(ys):]

        @pl.when(pl.program_id(1) == 0)
        def _():
            c = 0
            for y_ref, wd in zip(y_refs, widths):
                mix_ref[:, c:c + wd] = _rms_fwd(y_ref[...], g_ref[:, c:c + wd]).astype(BF16)
                c += wd

        o_ref[...] = res_ref[...] + _nn(mix_ref[...], w_ref[...])

    tile = pl.BlockSpec((tm, tn), lambda i, j: (i, j))
    return pl.pallas_call(
        body, name="mix_out", grid=(t // tm, d // tn),
        in_specs=[pl.BlockSpec((tm, wd), lambda i, j: (i, 0)) for wd in widths] + [
            pl.BlockSpec((1, k), lambda i, j: (0, 0)), pl.BlockSpec((k, tn), lambda i, j: (0, j)), tile],
        out_specs=[pl.BlockSpec((tm, k), lambda i, j: (i, 0)), tile],
        out_shape=[SDS((t, k), BF16), SDS((t, d), F32)],
        compiler_params=_params("parallel", "arbitrary"),
    )(*ys, g, w, res)


def _matmul_res(a, w, res, tm, tn, tk):
    t, k = a.shape
    n = w.shape[1]
    nk = k // tk

    def body(a_ref, w_ref, res_ref, o_ref, acc):
        kk = pl.program_id(2)

        @pl.when(kk == 0)
        def _():
            acc[...] = res_ref[...]

        acc[...] += _nn(a_ref[...], w_ref[...])

        @pl.when(kk == nk - 1)
        def _():
            o_ref[...] = acc[...]

    tile = pl.BlockSpec((tm, tn), lambda i, j, kk: (i, j))
    return pl.pallas_call(
        body, name="mlp_down", grid=(t // tm, n // tn, nk),
        in_specs=[pl.BlockSpec((tm, tk), lambda i, j, kk: (i, kk)), pl.BlockSpec((tk, tn), lambda i, j, kk: (kk, j)), tile],
        out_specs=tile,
        out_shape=SDS((t, n), F32),
        scratch_shapes=[pltpu.VMEM((tm, tn), F32)],
        compiler_params=_params("parallel", "parallel", "arbitrary"),
    )(a, w, res)


def _loss_grad(y, target, tm):
    t, d = y.shape

    def body(y_ref, t_ref, dy_ref, dyb_ref, l_ref):
        @pl.when(pl.program_id(0) == 0)
        def _():
            l_ref[...] = jnp.zeros_like(l_ref)

        err = y_ref[...] - t_ref[...]
        dy = err * (1.0 / d)
        dy_ref[...] = dy
        dyb_ref[...] = dy.astype(BF16)
        per_row = jnp.mean(err * err, axis=-1, keepdims=True)
        l_ref[...] += jnp.sum(per_row, axis=0, keepdims=True) * 0.5

    row = pl.BlockSpec((tm, d), lambda i: (i, 0))
    return pl.pallas_call(
        body, name="loss_grad", grid=(t // tm,),
        in_specs=[row, row], out_specs=[row, row, pl.BlockSpec((8, BLK), lambda i: (0, 0))],
        out_shape=[SDS((t, d), F32), SDS((t, d), BF16), SDS((8, BLK), F32)],
        compiler_params=_params("arbitrary"),
    )(y, target)


def _mlp_dact(dxb, w_down, r, tn, after=None):
    t, d = dxb.shape
    f = w_down.shape[0]

    def body(dxb_ref, w_ref, r_ref, dup_ref):
        dup_ref[...] = (_nt(dxb_ref[...], w_ref[...]) * (2.0 * r_ref[...].astype(F32))).astype(BF16)

    tile = pl.BlockSpec((t, tn), lambda j: (0, j))
    body, in_specs, args = _ordered_behind(
        body, 3, [pl.BlockSpec((t, d), lambda j: (0, 0)), pl.BlockSpec((tn, d), lambda j: (j, 0)), tile],
        [dxb, w_down, r], after)
    return pl.pallas_call(
        body, name="mlp_dact", grid=(f // tn,),
        in_specs=in_specs, out_specs=tile,
        out_shape=SDS((t, f), BF16),
        compiler_params=_params("arbitrary"),
    )(*args)


def _grad_w_stacked(pieces, b, tm, name, after=None):
    t, n = b.shape
    tiles = [p.shape[1] // tm for p in pieces]
    first = [sum(tiles[:k]) for k in range(len(pieces))]

    def body(*refs):
        b_ref, o_ref = refs[len(pieces)], refs[-1]
        i = pl.program_id(0)
        for k, p_ref in enumerate(refs[:len(pieces)]):
            @pl.when((i >= first[k]) & (i < first[k] + tiles[k]))
            def _(p_ref=p_ref):
                o_ref[...] = _tn(p_ref[...], b_ref[...]).astype(BF16)

    in_specs = [pl.BlockSpec((t, tm), lambda i, k=k: (0, jnp.clip(i - first[k], 0, tiles[k] - 1))) for k in range(len(pieces))]
    in_specs.append(pl.BlockSpec((t, n), lambda i: (0, 0)))
    body, in_specs, args = _ordered_behind(body, len(pieces) + 1, in_specs, [*pieces, b], after)
    return pl.pallas_call(
        body, name=name, grid=(sum(tiles),),
        in_specs=in_specs, out_specs=pl.BlockSpec((tm, n), lambda i: (i, 0)),
        out_shape=SDS((sum(tiles) * tm, n), BF16),
        compiler_params=_params("arbitrary"),
    )(*args)


def _grad_w_half(a, b, tm, tn, name, by_cols, sel, add=None, after=None):
    t, m = a.shape
    n = b.shape[1]
    if by_cols:
        per = n // N_DEV // tn
        grid, shape = (4, m // tm, per), (4, m, n // N_DEV)
        a_spec = pl.BlockSpec((t, tm), lambda q, i, j, s: (0, i))
        b_spec = pl.BlockSpec((t, tn), lambda q, i, j, s: (0, (2 * q + s[0]) * per + j))
    else:
        per = m // N_DEV // tm
        grid, shape = (4, per, n // tn), (4, m // N_DEV, n)
        a_spec = pl.BlockSpec((t, tm), lambda q, i, j, s: (0, (2 * q + s[0]) * per + i))
        b_spec = pl.BlockSpec((t, tn), lambda q, i, j, s: (0, j))
    assert per >= 1
    tile = pl.BlockSpec((None, tm, tn), lambda q, i, j, s: (q, i, j))

    def body(sel_ref, a_ref, b_ref, *rest):
        acc = _tn(a_ref[...], b_ref[...])
        if add is not None:
            acc = acc + rest[0][...].astype(F32)
        rest[-1][...] = acc.astype(BF16)

    in_specs, args = [a_spec, b_spec], [a, b]
    if add is not None:
        in_specs, args = in_specs + [tile], args + [add]
    if after is not None:
        in_specs, args = in_specs + [ANY], args + [after]
    return pl.pallas_call(
        body, name=name,
        grid_spec=pltpu.PrefetchScalarGridSpec(num_scalar_prefetch=1, grid=grid, in_specs=in_specs, out_specs=tile),
        out_shape=SDS(shape, BF16),
        compiler_params=_params("arbitrary", "arbitrary", "arbitrary"),
    )(sel, *args)


def _mlp_dnorm(dup, w_up, x, g, dres, tm, after=None):
    t, f = dup.shape
    d = x.shape[1]
    nk, _, tk = w_up.shape

    def body(a_ref, w_ref, x_ref, g_ref, dres_ref, dx_ref, dg_ref, acc):
        i, kk = pl.program_id(0), pl.program_id(1)

        @pl.when((i == 0) & (kk == 0))
        def _():
            dg_ref[...] = jnp.zeros_like(dg_ref)

        @pl.when(kk == 0)
        def _():
            acc[...] = jnp.zeros_like(acc)

        acc[...] += _nt(a_ref[...], w_ref[...])

        @pl.when(kk == nk - 1)
        def _():
            dx, dg = _rms_bwd(acc[...], x_ref[...], g_ref[...])
            dx_ref[...] = dres_ref[...] + dx
            dg_ref[0:1, :] += dg

    row = pl.BlockSpec((tm, d), lambda i, kk: (i, 0))
    in_specs = [pl.BlockSpec((tm, tk), lambda i, kk: (i, kk)), pl.BlockSpec((None, d, tk), lambda i, kk: (kk, 0, 0)),
                row, pl.BlockSpec((1, d), lambda i, kk: (0, 0)), row]
    body, in_specs, args = _ordered_behind(body, 5, in_specs, [dup, w_up, x, g, dres], after)
    return pl.pallas_call(
        body, name="mlp_dnorm", grid=(t // tm, nk),
        in_specs=in_specs,
        out_specs=[row, pl.BlockSpec((8, d), lambda i, kk: (0, 0))],
        out_shape=[SDS((t, d), F32), SDS((8, d), F32)],
        scratch_shapes=[pltpu.VMEM((tm, d), F32)],
        compiler_params=_params("arbitrary", "arbitrary"),
    )(*args)


def _mix_dnorm(dx, w_out, ys, g, tm, after=None):
    t, d = dx.shape
    k = w_out.shape[0]
    widths = [y.shape[1] for y in ys]

    def body(dx_ref, w_ref, *refs):
        y_refs = refs[:len(ys)]
        g_ref, dxb_ref = refs[len(ys)], refs[len(ys) + 1]
        dy_refs = refs[len(ys) + 2:2 * len(ys) + 2]
        dg_ref = refs[-1]

        @pl.when(pl.program_id(0) == 0)
        def _():
            dg_ref[...] = jnp.zeros_like(dg_ref)

        dxb = dx_ref[...].astype(BF16)
        dxb_ref[...] = dxb
        dmix = _nt(dxb, w_ref[...])
        c = 0
        for y_ref, dy_ref, wd in zip(y_refs, dy_refs, widths):
            dy, dg = _rms_bwd(dmix[:, c:c + wd], y_ref[...], g_ref[:, c:c + wd])
            dy_ref[...] = dy
            dg_ref[0:1, c:c + wd] += dg
            c += wd

    row = pl.BlockSpec((tm, d), lambda i: (i, 0))
    yspecs = [pl.BlockSpec((tm, wd), lambda i: (i, 0)) for wd in widths]
    in_specs = [row, pl.BlockSpec((k, d), lambda i: (0, 0))] + yspecs + [pl.BlockSpec((1, k), lambda i: (0, 0))]
    body, in_specs, args = _ordered_behind(body, len(in_specs), in_specs, [dx, w_out, *ys, g], after)
    return pl.pallas_call(
        body, name="mix_dnorm", grid=(t // tm,),
        in_specs=in_specs,
        out_specs=[row] + yspecs + [pl.BlockSpec((8, k), lambda i: (0, 0))],
        out_shape=[SDS((t, d), BF16)] + [SDS((t, wd), F32) for wd in widths] + [SDS((8, k), F32)],
        compiler_params=_params("arbitrary"),
    )(*args)


def _in_dnorm(dps, w_in_t, x, g, dres, tm):
    t, d = x.shape
    widths = [p.shape[1] for p in dps]
    offs = [sum(widths[:p]) for p in range(len(dps))]
    n_in = w_in_t.shape[0]

    def body(*refs):
        p_refs = refs[:len(dps)]
        w_ref, x_ref, g_ref, dres_ref, dx_ref, dxb_ref, dg_ref = refs[len(dps):]

        @pl.when(pl.program_id(0) == 0)
        def _():
            dg_ref[...] = jnp.zeros_like(dg_ref)

        acc = None
        for p_ref, off, wd in zip(p_refs, offs, widths):
            term = _nn(p_ref[...], w_ref[off:off + wd, :])
            acc = term if acc is None else acc + term
        dx, dg = _rms_bwd(acc, x_ref[...], g_ref[...])
        dx = dres_ref[...] + dx
        dx_ref[...] = dx
        dxb_ref[...] = dx.astype(BF16)
        dg_ref[0:1, :] += dg

    row = pl.BlockSpec((tm, d), lambda i: (i, 0))
    return pl.pallas_call(
        body, name="in_dnorm", grid=(t // tm,),
        in_specs=[pl.BlockSpec((tm, wd), lambda i: (i, 0)) for wd in widths] + [
            pl.BlockSpec((n_in, d), lambda i: (0, 0)), row, pl.BlockSpec((1, d), lambda i: (0, 0)), row],
        out_specs=[row, row, pl.BlockSpec((8, d), lambda i: (0, 0))],
        out_shape=[SDS((t, d), F32), SDS((t, d), BF16), SDS((8, d), F32)],
        compiler_params=_params("arbitrary"),
    )(*dps, w_in_t, x, g, dres)


def _tile(n, want):
    return min(n, want)


def _row_tile(n, want):
    return max(k for k in range(8, min(n, want) + 1, 8) if n % k == 0)


def _layer_fwd(x, p, fetch, after):
    t, d = x.shape
    aw, kv, cw, sw = d // 2, d // 8, d // 4, d // 4
    tm = _tile(t, ROWS_WIDE)
    w_in = fetch("w_in", after)
    h1, proj = _rms_matmul(x, p["ln1_g"], w_in, tm, COLS if w_in.shape[0] % COLS == 0 else COLS // 2, False, "in_proj",
                           transposed=True)
    y_attn, probs, sink_p = _attn_fwd(proj, p["qg"], p["kg"], p["sinks"], aw, kv)
    y_conv, z_conv = _conv_fwd(proj, p["conv_w"], p["conv_b"], p["conv_ln_g"], p["conv_ln_b"], cw)
    y_sgu, s_sgu = _sgu_fwd(proj, p["sgu_ln_g"], p["sgu_ln_b"], p["sgu_w"], p["sgu_bias"], sw)
    ys = [y_attn, y_conv, y_sgu]
    w_out = fetch("w_out", y_sgu)
    mix, x1 = _group_rms_matmul(ys, p["out_norm_g"], w_out, x, tm, _tile(d, COLS_WIDE // 2))
    w_up = fetch("w_up", x1)
    h2, act, r = _rms_matmul(x1, p["ln2_g"], w_up, tm, w_up.shape[2], True, "mlp_up")
    w_down = fetch("w_down", act)
    x2 = _matmul_res(act, w_down, x1, tm, _tile(d, COLS_WIDE), COLS)
    saved = dict(x=x, h1=h1, proj=proj, ys=ys, mix=mix, x1=x1, h2=h2, act=act, r=r,
                 w_in=w_in, w_out=w_out, w_up=w_up, w_down=w_down, probs=probs, sink_p=sink_p, z_conv=z_conv, s_sgu=s_sgu)
    return x2, saved


def _layer_bwd(dx2, dx2b, p, s, start, finish, share, carry):
    t, d = dx2.shape
    aw, kv, cw, sw = d // 2, d // 8, d // 4, d // 4
    tm = _tile(t, ROWS_F32)
    dup = _mlp_dact(dx2b, s["w_down"], s["r"], COLS, carry[0] if carry else None)
    tok = carry[1](dup) if carry else None
    half = functools.partial
    f = s["act"].shape[1]
    tok = start("w_down", half(_grad_w_half, s["act"], dx2b, _tile(f // N_DEV, COLS), _tile(d, COLS_WIDE), "grad_w_down", False), tok)
    dx1, d_ln2 = _mlp_dnorm(dup, s["w_up"], s["x1"], p["ln2_g"], dx2, tm, tok)
    tok = finish("w_down", dx1)
    tok = start("w_up", half(_grad_w_half, s["h2"], dup, _tile(d, ROWS_WIDE), s["w_up"].shape[2], "grad_w_up", True), tok)
    dx1b, dya, dyc, dys, d_onorm = _mix_dnorm(dx1, s["w_out"], s["ys"], p["out_norm_g"], _tile(t, ROWS_NORM), tok)
    tok = finish("w_up", dx1b)
    dsgu, d_sw, d_sb, d_svec = _sgu_bwd(s["proj"], dys, s["s_sgu"], p["sgu_ln_g"], p["sgu_ln_b"], p["sgu_w"], p["sgu_bias"], sw, tok)
    share("sgu_w", d_sw)
    tok = start("w_out", half(_grad_w_half, s["mix"], dx1b, _tile(d // N_DEV, COLS), _tile(d, COLS_WIDE), "grad_w_out", False), dsgu)
    dqkv, d_attn = _attn_bwd(s["proj"], dya, s["probs"], s["sink_p"], p["qg"], p["kg"], p["sinks"], aw, kv, tok)
    tok = finish("w_out", dqkv)
    dconv, d_cw, d_cvec = _conv_bwd(s["proj"], dyc, s["z_conv"], p["conv_w"], p["conv_b"], p["conv_ln_g"], p["conv_ln_b"], cw)
    dps = [dqkv, dconv, dsgu]
    dx, dxb, d_ln1 = _in_dnorm(dps, s["w_in"], s["x"], p["ln1_g"], dx1, _tile(t, ROWS_NORM))
    heads = sw // HEAD
    share("rest", dict(
        ln1_g=d_ln1[0], q_norm_g=d_attn[0, :HEAD], k_norm_g=d_attn[1, :HEAD], sinks=d_attn[2, :aw // HEAD],
        conv_w=d_cw[:CONV_K], conv_b=d_cvec[0], conv_ln_g=d_cvec[1], conv_ln_b=d_cvec[2],
        sgu_ln_g=d_svec[0], sgu_ln_b=d_svec[1], sgu_b=d_sb[:, :heads].T,
        out_norm_g=d_onorm[0], ln2_g=d_ln2[0]))
    tm_in = COLS if all(dp.shape[1] % COLS == 0 for dp in dps) else COLS // 2
    tok = start("w_in", _grad_w_stacked(dps, s["h1"], tm_in, "grad_w_in", tok), None)
    carry = (tok, functools.partial(finish, "w_in"))
    return dx, dxb, carry


def _layer_params(l, small):
    row = lambda v: v[l][None, :]
    two = lambda v: jnp.tile(v[l], 2)[None, :]
    return dict(
        ln1_g=row(small["ln1_g"]), ln2_g=row(small["ln2_g"]), out_norm_g=row(small["out_norm_g"]),
        qg=two(small["q_norm_g"]), kg=two(small["k_norm_g"]), sinks=small["sinks"][l],
        conv_w=jnp.pad(small["conv_w"][l], ((0, HALO - CONV_K), (0, 0))),
        conv_b=row(small["conv_b"]), conv_ln_g=row(small["conv_ln_g"]), conv_ln_b=row(small["conv_ln_b"]),
        sgu_ln_g=row(small["sgu_ln_g"]), sgu_ln_b=row(small["sgu_ln_b"]), sgu_w=small["sgu_w"][l],
        sgu_bias=jnp.repeat(small["sgu_b"][l].T, HEAD, axis=1),
    )


def _local_step(x, target, small, depth, fetch, start, finish, share, after):
    params = [_layer_params(l, small) for l in range(depth)]
    saved = []
    h = x
    for l in range(depth):
        h, s = _layer_fwd(h, params[l], functools.partial(fetch, l), after)
        after = h
        saved.append(s)
    dy, dyb, lsum = _loss_grad(h, target, _tile(x.shape[0], ROWS_F32))
    def share_with_loss(kind, grads):
        share(depth - 1, kind, dict(grads, loss=lsum[0, 0:1]) if kind == "rest" else grads)

    carry = None
    for l in reversed(range(depth)):
        dy, dyb, carry = _layer_bwd(dy, dyb, params[l], saved[l], functools.partial(start, l), functools.partial(finish, l),
                                    share_with_loss if l == depth - 1 else functools.partial(share, l), carry)
    return dy, carry


BIG = ("w_in", "w_out", "w_up", "w_down")
HBM = pl.BlockSpec(memory_space=pltpu.HBM)
SEMS = pl.BlockSpec(memory_space=pltpu.SEMAPHORE)
EFFECT = pltpu.SideEffectType.DATAFLOW_SIDE_EFFECTING


def _mesh_pos():
    return lax.axis_index("x"), lax.axis_index("y"), lax.axis_index("c")


def _other_chips(x, y):
    return [(1 - x, y), (x, 1 - y), (1 - x, 1 - y)]


def _copies(plan, refs, sends, recvs):
    x, y, c = _mesh_pos()
    return [pltpu.make_async_remote_copy(src_ref=src, dst_ref=dst, send_sem=sends.at[k], recv_sem=recvs.at[k],
                                         device_id=dev, device_id_type=MESH)
            for k, (src, dst, dev) in enumerate(plan(x, y, c, refs))]


def _gather_plan(x, y, c, refs):
    mine = refs[0].at[4 * x + 2 * y + c]
    return [(mine, mine, (x, y, 1 - c))] + [(mine, mine, (*chip, c)) for chip in _other_chips(x, y)]


def _all_plan(x, y, c, refs):
    mine = refs[0].at[4 * x + 2 * y + c]
    return [(mine, mine, (x ^ (k >> 2), y ^ ((k >> 1) & 1), c ^ (k & 1))) for k in range(1, N_DEV)]


def _pair_plan(x, y, c, refs):
    blocks, land = refs
    return [(blocks.at[2 * q + (1 - c)], land.at[q], (x, y, 1 - c)) for q in range(4)]


def _half_plan(x, y, c, refs):
    blocks, land = refs
    return [(blocks.at[q], land.at[q], (x, y, 1 - c)) for q in range(4)]


def _chips_plan(x, y, c, refs):
    sums, land = refs
    return [(sums.at[2 * chip[0] + chip[1]], land.at[k], (*chip, c)) for k, chip in enumerate(_other_chips(x, y))]


def _start_exchanges(name, groups, after=None):
    flat = [a for arrays, _, _ in groups for a in arrays]
    n_arr, n_g = len(flat), len(groups)
    n_in = n_arr + (after is not None)

    def body(*refs):
        ins, sems, token = refs[:n_arr], refs[n_in:n_in + 2 * n_g], refs[-1]
        off = 0
        for gi, (arrays, plan, _) in enumerate(groups):
            for cp in _copies(plan, ins[off:off + len(arrays)], sems[2 * gi], sems[2 * gi + 1]):
                cp.start()
            off += len(arrays)
        token[...] = jnp.zeros_like(token)

    res = pl.pallas_call(
        body, name=name,
        out_shape=[pltpu.SemaphoreType.DMA((n,)) for _, _, n in groups for _ in (0, 1)]
        + [pltpu.HBM(a.shape, a.dtype) for a in flat] + [SDS((8, BLK), F32)],
        in_specs=[HBM] * n_arr + [ANY] * (after is not None),
        out_specs=[SEMS] * (2 * n_g) + [HBM] * n_arr + [pl.BlockSpec(memory_space=pltpu.VMEM)],
        input_output_aliases={i: 2 * n_g + i for i in range(n_arr)},
        compiler_params=pltpu.CompilerParams(has_side_effects=EFFECT),
    )(*[pltpu.with_memory_space_constraint(a, pltpu.HBM) for a in flat], *([after] if after is not None else []))
    sems, thru, token = res[:2 * n_g], res[2 * n_g:2 * n_g + n_arr], res[-1]
    out, off = [], 0
    for gi, (arrays, _, _) in enumerate(groups):
        out.append((list(thru[off:off + len(arrays)]), sems[2 * gi], sems[2 * gi + 1]))
        off += len(arrays)
    return out, token


def _wait_exchange(name, arrays, sends, recvs, plan, after):
    n = len(arrays)

    def body(*refs):
        for cp in _copies(plan, refs[:n], refs[n], refs[n + 1]):
            cp.wait_send()
            cp.wait_recv()

    return pl.pallas_call(
        body, name=name,
        out_shape=[pltpu.HBM(a.shape, a.dtype) for a in arrays],
        in_specs=[HBM] * n + [SEMS, SEMS, ANY],
        out_specs=[HBM] * n,
        input_output_aliases={i: i for i in range(n)},
        compiler_params=pltpu.CompilerParams(has_side_effects=EFFECT),
    )(*arrays, sends, recvs, after)


def _wait_exchanges(name, groups, after):
    flat = [a for (arrays, _, _), _ in groups for a in arrays]
    n_arr, n_g = len(flat), len(groups)

    def body(*refs):
        off = 0
        for gi, ((arrays, _, _), plan) in enumerate(groups):
            for cp in _copies(plan, refs[off:off + len(arrays)], refs[n_arr + 2 * gi], refs[n_arr + 2 * gi + 1]):
                cp.wait_send()
                cp.wait_recv()
            off += len(arrays)

    res = pl.pallas_call(
        body, name=name,
        out_shape=[pltpu.HBM(a.shape, a.dtype) for a in flat],
        in_specs=[HBM] * n_arr + [SEMS] * (2 * n_g) + [ANY],
        out_specs=[HBM] * n_arr,
        input_output_aliases={i: i for i in range(n_arr)},
        compiler_params=pltpu.CompilerParams(has_side_effects=EFFECT),
    )(*flat, *[s for (_, sends, recvs), _ in groups for s in (sends, recvs)], after)
    out, off = [], 0
    for (arrays, _, _), _ in groups:
        out.append(list(res[off:off + len(arrays)]))
        off += len(arrays)
    return out


def _gather_finish(land, name):
    def body(land_ref, out_ref, send_sems, recv_sems):
        del land_ref
        x, y, c = _mesh_pos()
        cps = []
        for k, chip in enumerate(_other_chips(x, y)):
            block = out_ref.at[4 * chip[0] + 2 * chip[1] + c]
            cps.append(pltpu.make_async_remote_copy(
                src_ref=block, dst_ref=block, send_sem=send_sems.at[k], recv_sem=recv_sems.at[k],
                device_id=(x, y, 1 - c), device_id_type=MESH))
        for cp in cps:
            cp.start()
        for cp in cps:
            cp.wait()

    return pl.pallas_call(
        body, name=name,
        in_specs=[ANY], out_specs=ANY,
        out_shape=SDS(land.shape, land.dtype),
        input_output_aliases={0: 0},
        scratch_shapes=[pltpu.SemaphoreType.DMA((3,)), pltpu.SemaphoreType.DMA((3,))],
    )(land)


def _pair_add(own, got, c, name):
    _, r, cols = own.shape
    tr = _row_tile(r, ROWS_F32)

    def body(c_ref, own_ref, got_ref, o_ref):
        o_ref[...] = (own_ref[...].astype(F32) + got_ref[...].astype(F32)).astype(BF16)

    return pl.pallas_call(
        body, name=name,
        grid_spec=pltpu.PrefetchScalarGridSpec(
            num_scalar_prefetch=1, grid=(4, r // tr),
            in_specs=[pl.BlockSpec((None, tr, cols), lambda q, i, c_ref: (2 * q + c_ref[0], i, 0)),
                      pl.BlockSpec((None, tr, cols), lambda q, i, c_ref: (q, i, 0))],
            out_specs=pl.BlockSpec((None, tr, cols), lambda q, i, c_ref: (q, i, 0))),
        out_shape=SDS((4, r, cols), BF16),
        compiler_params=_params("arbitrary", "arbitrary"),
    )(c, own, got)


def _adamw(w, g, m, v):
    m = ADAM_B1 * m + (1.0 - ADAM_B1) * g
    v = ADAM_B2 * v + (1.0 - ADAM_B2) * (g * g)
    m_hat = m / (1.0 - ADAM_B1 ** ADAM_STEP)
    v_hat = v / (1.0 - ADAM_B2 ** ADAM_STEP)
    delta = -ADAM_LR * (m_hat / (jnp.sqrt(v_hat) + ADAM_EPS) + ADAM_WD * w)
    return delta, m, v


def _adamw_layer(chip_sum, got, chip, w, m, v, layer, prev, name, after=None):
    depth, r, cols = w.shape
    tr = _row_tile(r, ROWS_NORM)

    def body(chip_ref, sum_ref, got_ref, w_ref, m_ref, v_ref, *rest):
        g_out, d_out, m_out, v_out, token = rest[-5:]
        g = sum_ref[...].astype(F32) + got_ref[0].astype(F32) + got_ref[1].astype(F32) + got_ref[2].astype(F32)
        delta, mm, vv = _adamw(w_ref[...], g, m_ref[...], v_ref[...])
        g_out[...] = g
        d_out[...] = delta
        m_out[...] = mm
        v_out[...] = vv
        token[...] = jnp.zeros_like(token)

    shard = pl.BlockSpec((None, tr, cols), lambda i, chip_ref: (layer, i, 0))
    in_specs = [pl.BlockSpec((None, tr, cols), lambda i, chip_ref: (chip_ref[0], i, 0)),
                pl.BlockSpec((3, tr, cols), lambda i, chip_ref: (0, i, 0)), shard, shard, shard]
    args = [chip, chip_sum, got, w, m, v]
    aliases = {}
    if prev is not None:
        in_specs += [ANY] * 4
        aliases = {len(args) + k: k for k in range(4)}
        args += list(prev)
    if after is not None:
        in_specs, args = in_specs + [ANY], args + [after]
    res = pl.pallas_call(
        body, name=name,
        grid_spec=pltpu.PrefetchScalarGridSpec(
            num_scalar_prefetch=1, grid=(r // tr,), in_specs=in_specs,
            out_specs=[shard] * 4 + [pl.BlockSpec((8, BLK), lambda i, chip_ref: (0, 0))]),
        out_shape=[SDS(w.shape, F32)] * 4 + [SDS((8, BLK), F32)],
        input_output_aliases=aliases,
        compiler_params=_params("arbitrary"),
    )(*args)
    return list(res[:4]), res[4]


WIDE = ("ln1_g", "out_norm_g", "ln2_g", "conv_b", "conv_ln_g", "conv_ln_b", "sgu_ln_g", "sgu_ln_b")
NARROW = ("q_norm_g", "k_norm_g", "sinks")


def _small_rows(w):
    rows, r = {}, 0
    for n in WIDE:
        rows[n] = (r, w[n].shape[1] // BLK)
        r += rows[n][1]
    for n in NARROW + ("loss",):
        rows[n] = (r, 1)
        r += 1
    r = -(-r // 8) * 8
    rows["sgu_b"] = (r, w["sgu_b"].shape[1])
    r += -(-rows["sgu_b"][1] // 8) * 8
    rows["conv_w"] = (r, N_DEV * HALO)
    return rows, r + N_DEV * HALO


def _pack_small(small, rows, total):
    parts, r = [], 0

    def put(name, block):
        nonlocal r
        first = rows[name][0]
        if first > r:
            parts.append(jnp.zeros((first - r, BLK), F32))
        parts.append(block)
        r = first + block.shape[0]

    for n in WIDE:
        put(n, small[n].reshape(-1, BLK))
    for n in NARROW:
        put(n, jnp.pad(small[n], (0, BLK - small[n].shape[0]))[None])
    if "loss" in small:
        put("loss", jnp.pad(small["loss"], (0, BLK - 1))[None])
    put("sgu_b", small["sgu_b"])
    cw = small["conv_w"]
    per_dev = cw.shape[1] // N_DEV
    blocks = jnp.transpose(cw.reshape(CONV_K, N_DEV, per_dev), (1, 0, 2))
    put("conv_w", jnp.pad(blocks, ((0, 0), (0, HALO - CONV_K), (0, BLK - per_dev))).reshape(N_DEV * HALO, BLK))
    if total > r:
        parts.append(jnp.zeros((total - r, BLK), F32))
    return jnp.concatenate(parts)


def _small_update(me, packed, sgu_w_all, w, m, v, rows):
    depth = len(packed)
    total = packed[0].shape[1]
    names = SMALL + ("conv_w",)
    heads = w["sgu_w"].shape[1]
    per_dev = w["conv_w"].shape[2]

    def body(me_ref, *refs):
        packed_refs, sgu_refs = refs[:depth], refs[depth:2 * depth]
        ins = refs[2 * depth:2 * depth + 3 * len(names)]
        outs = refs[2 * depth + 3 * len(names):2 * depth + 7 * len(names)]
        loss_out, acc, conv = refs[-3:]
        io = {n: (ins[3 * i:3 * i + 3], outs[4 * i:4 * i + 4]) for i, n in enumerate(names)}

        def update(n):
            (w_ref, m_ref, v_ref), (g_out, d_out, m_out, v_out) = io[n]
            delta, mm, vv = _adamw(w_ref[...], g_out[...], m_ref[...], v_ref[...])
            d_out[...] = delta
            m_out[...] = mm
            v_out[...] = vv

        mine = pl.ds(pl.multiple_of(rows["conv_w"][0] + HALO * me_ref[0], 8), HALO)
        for l in range(depth):
            s = packed_refs[l][0]
            c = packed_refs[l][0, mine, :]
            for k in range(1, N_DEV):
                s = s + packed_refs[l][k]
                c = c + packed_refs[l][k, mine, :]
            acc[l] = s
            conv[l] = c
        loss_out[...] = acc[depth - 1, rows["loss"][0]:rows["loss"][0] + 1, :]
        for n in WIDE:
            first, nr = rows[n]
            for l in range(depth):
                for j in range(nr):
                    io[n][1][0][l:l + 1, BLK * j:BLK * (j + 1)] = acc[l, first + j:first + j + 1, :]
            update(n)
        for n in NARROW:
            first, lanes = rows[n][0], w[n].shape[1]
            for l in range(depth):
                io[n][1][0][l:l + 1, :] = acc[l, first:first + 1, 0:lanes]
            update(n)
        first, nr = rows["sgu_b"]
        for l in range(depth):
            io["sgu_b"][1][0][l] = acc[l, first:first + nr, :]
            io["conv_w"][1][0][l] = conv[l, 0:CONV_K, 0:per_dev]
        update("sgu_b")
        update("conv_w")
        (w_ref, m_ref, v_ref), (g_out, d_out, m_out, v_out) = io["sgu_w"]
        for l in range(depth):
            for h in range(heads):
                g = sgu_refs[l][0, h]
                for k in range(1, N_DEV):
                    g = g + sgu_refs[l][k, h]
                delta, mm, vv = _adamw(w_ref[l, h], g, m_ref[l, h], v_ref[l, h])
                g_out[l, h] = g
                d_out[l, h] = delta
                m_out[l, h] = mm
                v_out[l, h] = vv

    def whole(a):
        nd = len(a.shape)
        return pl.BlockSpec(a.shape, lambda i, me_ref: (0,) * nd)

    small_in = [t[n] for n in names for t in (w, m, v)]
    res = pl.pallas_call(
        body, name="small_update",
        grid_spec=pltpu.PrefetchScalarGridSpec(
            num_scalar_prefetch=1, grid=(1,),
            in_specs=[whole(a) for a in list(packed) + list(sgu_w_all) + small_in],
            out_specs=[whole(w[n]) for n in names for _ in range(4)] + [pl.BlockSpec((1, BLK), lambda i, me_ref: (0, 0))],
            scratch_shapes=[pltpu.VMEM((depth, total, BLK), F32), pltpu.VMEM((depth, HALO, BLK), F32)]),
        out_shape=[SDS(w[n].shape, F32) for n in names for _ in range(4)] + [SDS((1, BLK), F32)],
        compiler_params=_params("arbitrary"),
    )(me, *packed, *sgu_w_all, *small_in)
    return {n: list(res[4 * i:4 * i + 4]) for i, n in enumerate(names)}, res[-1][0, 0]


def _pack(arrays):
    flat = jnp.concatenate([a.reshape(-1) for a in arrays])
    rows = -(-flat.shape[0] // (8 * BLK)) * 8
    return jnp.pad(flat, (0, rows * BLK - flat.shape[0])).reshape(rows, BLK)


def _unpack(packed, like):
    flat = packed.reshape(-1)
    out, off = [], 0
    for a in like:
        out.append(flat[off:off + a.size].reshape(a.shape))
        off += a.size
    return out


SMALL = ("ln1_g", "q_norm_g", "k_norm_g", "sinks", "conv_b", "conv_ln_g", "conv_ln_b", "sgu_ln_g", "sgu_ln_b",
         "sgu_w", "sgu_b", "out_norm_g", "ln2_g")
ORDER = ("ln1_g", "w_in", "q_norm_g", "k_norm_g", "sinks", "conv_w", "conv_b", "conv_ln_g", "conv_ln_b", "sgu_ln_g",
         "sgu_ln_b", "sgu_w", "sgu_b", "out_norm_g", "w_out", "ln2_g", "w_up", "w_down")


def _step(x, target, w, m, v):
    depth = w["ln1_g"].shape[0]
    xpos, ypos, cpos = _mesh_pos()
    me = 4 * xpos + 2 * ypos + cpos
    c_arr = jnp.reshape(cpos, (1,)).astype(jnp.int32)
    chip_arr = jnp.reshape(2 * xpos + ypos, (1,)).astype(jnp.int32)

    d = x.shape[1]
    def own_block(shard):
        return lax.dynamic_update_slice_in_dim(lax.empty((N_DEV,) + shard.shape, shard.dtype), shard[None], me, axis=0)

    cw = w["conv_w"]
    order = [(0, "conv_w")] + [(l, n) for l in range(depth) for n in BIG]
    started, gather_token = _start_exchanges("gather_start", [
        ([own_block(_pack([cw]) if n == "conv_w" else w[n][l].astype(BF16))], _gather_plan, 4) for l, n in order])
    pending = dict(zip(order, started))

    def fetch(l, n, after):
        arrays, sends, recvs = pending.pop((l, n))
        land, = _wait_exchange(f"gather_wait_{l}_{n}", arrays, sends, recvs, _gather_plan,
                               gather_token if after is None else after)
        full = _gather_finish(land, "gather_finish_" + n)
        return full if n in ("w_up", "conv_w") else full.reshape(-1, d)

    cw_all = fetch(0, "conv_w", None)
    cw_full = jnp.concatenate([_unpack(cw_all[k], [cw])[0] for k in range(N_DEV)], axis=-1)
    small, _ = lax.optimization_barrier(({n: w[n] for n in SMALL}, gather_token))
    small["conv_w"] = cw_full

    to_sibling, inflight, shared, leaving = {}, [], {}, []
    rows, total = _small_rows(w)

    def flush(name, first=()):
        groups = list(first) + [g for _, g in leaving]
        started, token = _start_exchanges(name, groups)
        for (record, _), going in zip(leaving, started[len(first):]):
            record(going)
        leaving.clear()
        return started[:len(first)], token

    def start(l, n, grad, after):
        if callable(grad):
            blocks, plan = grad(sel=1 - c_arr, after=after), _half_plan
        else:
            blocks, plan = grad.reshape(N_DEV, -1, d), _pair_plan
        (going,), token = flush(f"grads_start_{l}_{n}", [([blocks, lax.empty((4,) + blocks.shape[1:], BF16)], plan, 4)])
        to_sibling[l, n] = (going, plan, grad)
        return token

    def finish(l, n, after):
        (arrays, sends, recvs), plan, grad = to_sibling.pop((l, n))
        blocks, from_sibling = _wait_exchange(f"pair_wait_{l}_{n}", arrays, sends, recvs, plan, after)
        if callable(grad):
            chip_sums = grad(sel=c_arr, add=from_sibling)
        else:
            chip_sums = _pair_add(blocks, from_sibling, c_arr, "pair_add_" + n)
        leaving.append((lambda going: inflight.append((l, n, going)),
                        ([chip_sums, lax.empty((3,) + chip_sums.shape[1:], BF16)], _chips_plan, 3)))
        return chip_sums

    def share(l, kind, grads):
        block = grads if kind == "sgu_w" else _pack_small(grads, rows, total)
        leaving.append((lambda going: shared.__setitem__((l, kind), going), ([own_block(block)], _all_plan, N_DEV - 1)))

    grad_x, (after, finish_last) = _local_step(x, target, small, depth, fetch, start, finish, share, None)

    as3d = lambda a: a.reshape(depth, -1, a.shape[-1])
    results = {n: None for n in BIG}

    landed = {}

    def await_layer(name, reduces, after):
        smalls = sorted(k for k in shared if k not in landed and all(k[0] >= l for l, _, _ in reduces))
        got = _wait_exchanges(name, [(going, _chips_plan) for _, _, going in reduces]
                              + [(shared[k], _all_plan) for k in smalls], after)
        for k, (arr,) in zip(smalls, got[len(reduces):]):
            landed[k] = arr
        return got[:len(reduces)]

    def update(l, n, chip_sums, got, after):
        results[n], token = _adamw_layer(chip_sums, got, chip_arr, as3d(w[n]), as3d(m[n]), as3d(v[n]), l, results[n],
                                         f"adamw_{l}_{n}", after)
        return token

    first, rest = inflight[:4], inflight[4:]
    arrived = await_layer("grads_wait_first", first, after)
    after = None
    for (l, n, _), (chip_sums, got) in list(zip(first, arrived))[:2]:
        after = update(l, n, chip_sums, got, after)
    finish_last(after)
    _, after = flush("grads_start_last")
    for (l, n, _), (chip_sums, got) in list(zip(first, arrived))[2:]:
        after = update(l, n, chip_sums, got, after)
    arrived = await_layer("grads_wait_rest", rest, after)
    out, loss = _small_update(jnp.reshape(me, (1,)).astype(jnp.int32), [landed[l, "rest"] for l in range(depth)],
                              [landed[l, "sgu_w"] for l in range(depth)], w, m, v, rows)
    after = out["sinks"][1]
    for (l, n, _), (chip_sums, got) in zip(rest, arrived):
        after = update(l, n, chip_sums, got, after)
    (chip_sums, got), = await_layer("grads_wait_last", inflight[-1:], after)
    l, n, _ = inflight[-1]
    update(l, n, chip_sums, got, None)
    out.update({n: [r.reshape(w[n].shape) for r in results[n]] for n in BIG})
    return (loss, grad_x[None]) + tuple(out[n][k] for k in range(4) for n in ORDER)


def kernel(x, ln1_g, w_in, q_norm_g, k_norm_g, sinks, conv_w, conv_b, conv_ln_g, conv_ln_b, sgu_ln_g, sgu_ln_b, sgu_w, sgu_b, out_norm_g, w_out, ln2_g, w_up, w_down, loss_target, m_ln1_g, m_w_in, m_q_norm_g, m_k_norm_g, m_sinks, m_conv_w, m_conv_b, m_conv_ln_g, m_conv_ln_b, m_sgu_ln_g, m_sgu_ln_b, m_sgu_w, m_sgu_b, m_out_norm_g, m_w_out, m_ln2_g, m_w_up, m_w_down, v_ln1_g, v_w_in, v_q_norm_g, v_k_norm_g, v_sinks, v_conv_w, v_conv_b, v_conv_ln_g, v_conv_ln_b, v_sgu_ln_g, v_sgu_ln_b, v_sgu_w, v_sgu_b, v_out_norm_g, v_w_out, v_ln2_g, v_w_up, v_w_down):
    w = dict(zip(ORDER, (ln1_g, w_in, q_norm_g, k_norm_g, sinks, conv_w, conv_b, conv_ln_g, conv_ln_b, sgu_ln_g, sgu_ln_b,
                         sgu_w, sgu_b, out_norm_g, w_out, ln2_g, w_up, w_down)))
    m = dict(zip(ORDER, (m_ln1_g, m_w_in, m_q_norm_g, m_k_norm_g, m_sinks, m_conv_w, m_conv_b, m_conv_ln_g, m_conv_ln_b,
                         m_sgu_ln_g, m_sgu_ln_b, m_sgu_w, m_sgu_b, m_out_norm_g, m_w_out, m_ln2_g, m_w_up, m_w_down)))
    v = dict(zip(ORDER, (v_ln1_g, v_w_in, v_q_norm_g, v_k_norm_g, v_sinks, v_conv_w, v_conv_b, v_conv_ln_g, v_conv_ln_b,
                         v_sgu_ln_g, v_sgu_ln_b, v_sgu_w, v_sgu_b, v_out_norm_g, v_w_out, v_ln2_g, v_w_up, v_w_down)))
    for group in (w, m, v):
        group["w_in"] = jnp.swapaxes(group["w_in"], 1, 2)
    out = list(_step(x[0], loss_target[0], w, m, v))
    for k in range(4):
        i = 2 + k * len(ORDER) + ORDER.index("w_in")
        out[i] = jnp.swapaxes(out[i], 1, 2)
    return tuple(out)
```

```python
import functools

import jax
import jax.numpy as jnp
from jax import lax
from jax.experimental import pallas as pl
from jax.experimental.pallas import tpu as pltpu

F32 = jnp.float32
BF16 = jnp.bfloat16
SDS = jax.ShapeDtypeStruct

EPS = 1e-6
NEG_INF = -1e30
HEAD = 64
BLK = 128
CONV_K = 31
HALO = 32
N_DEV = 8

ADAM_LR = 0.001
ADAM_B1 = 0.9
ADAM_B2 = 0.999
ADAM_EPS = 1e-08
ADAM_WD = 0.01
ADAM_STEP = 10

VMEM_LIMIT = 56 * 1024 * 1024

ROWS_WIDE = 1024
ROWS_F32 = 512
ROWS_NORM = 256
COLS = 512
COLS_WIDE = 2048

MESH = pl.DeviceIdType.MESH


def _params(*sem):
    return pltpu.CompilerParams(dimension_semantics=sem, vmem_limit_bytes=VMEM_LIMIT)


def _nt(a, b):
    return lax.dot_general(a, b, (((1,), (1,)), ((), ())), preferred_element_type=F32)


def _tn(a, b):
    return lax.dot_general(a, b, (((0,), (0,)), ((), ())), preferred_element_type=F32)


def _nn(a, b):
    return jnp.dot(a, b, preferred_element_type=F32)


def _sigmoid(x):
    return 1.0 / (1.0 + jnp.exp(-x))


ANY = pl.BlockSpec(memory_space=pl.ANY)


def _ordered_behind(body, n_in, in_specs, args, after):
    if after is None:
        return body, in_specs, args
    return (lambda *refs: body(*refs[:n_in], *refs[n_in + 1:])), list(in_specs) + [ANY], list(args) + [after]


def _segsum(x, first):
    head0 = jnp.where(first, x, 0.0)
    s0 = jnp.sum(head0, axis=-1, keepdims=True)
    s1 = jnp.sum(x - head0, axis=-1, keepdims=True)
    return jnp.where(first, s0, s1)


def _head_rms(x, gain, first):
    rstd = lax.rsqrt(_segsum(x * x, first) * (1.0 / HEAD) + EPS)
    xhat = x * rstd
    return xhat * gain, xhat, rstd


def _expand(x, odd, lo):
    if odd:
        xl = pltpu.roll(jnp.where(lo, 0.0, x), HEAD, axis=1)
    else:
        xl = jnp.where(lo, x, 0.0)
    xh = pltpu.roll(xl, HEAD, axis=1)
    return jnp.concatenate([xl, xh], axis=0).astype(BF16)


def _attn_mask(n):
    qi = lax.broadcasted_iota(jnp.int32, (BLK, 2 * BLK), 0)
    sj = lax.broadcasted_iota(jnp.int32, (BLK, 2 * BLK), 1)
    rel = qi + BLK - sj
    return (rel >= 0) & (rel < BLK) & ((sj >= BLK) | (n > 0))


def _attn_specs(t, aw, kv):
    prev = lambda n: jnp.maximum(n - 1, 0)
    kb, vb = aw // kv, aw // kv + 1
    return [
        pl.BlockSpec(memory_space=pltpu.SMEM),
        pl.BlockSpec((BLK, aw), lambda n: (n, 0)),
        pl.BlockSpec((BLK, kv), lambda n: (prev(n), kb)),
        pl.BlockSpec((BLK, kv), lambda n: (n, kb)),
        pl.BlockSpec((BLK, kv), lambda n: (prev(n), vb)),
        pl.BlockSpec((BLK, kv), lambda n: (n, vb)),
        pl.BlockSpec((1, BLK), lambda n: (0, 0)),
        pl.BlockSpec((1, BLK), lambda n: (0, 0)),
    ]


def _softmax_pair(s2, valid, sink0, sink1):
    out, psink = [], []
    for half, sink in ((0, sink0), (1, sink1)):
        s = jnp.where(valid, s2[:, 2 * BLK * half:2 * BLK * (half + 1)], NEG_INF)
        m = jnp.maximum(jnp.max(s, axis=-1, keepdims=True), sink)
        p = jnp.exp(s - m)
        es = jnp.exp(sink - m)
        inv = 1.0 / (jnp.sum(p, axis=-1, keepdims=True) + es)
        out.append(p * inv)
        psink.append(es * inv)
    return jnp.concatenate(out, axis=1), psink


def _attn_fwd(proj, qg, kg, sinks, aw, kv):
    t = proj.shape[0]
    n_pairs, n_kvblk = aw // BLK, kv // BLK

    def body(sink_ref, q_ref, kp_ref, kc_ref, vp_ref, vc_ref, qg_ref, kg_ref, o_ref, probs_ref, sink_p_ref):
        n = pl.program_id(0)
        lo = lax.broadcasted_iota(jnp.int32, (1, BLK), 1) < HEAD
        valid = _attn_mask(n)
        kraw = jnp.concatenate([kp_ref[...], kc_ref[...]], axis=0)
        vraw = jnp.concatenate([vp_ref[...], vc_ref[...]], axis=0)
        qn = [_head_rms(q_ref[:, BLK * p:BLK * (p + 1)], qg_ref[...], lo)[0].astype(BF16) for p in range(n_pairs)]
        k2, v2 = [], []
        for b in range(n_kvblk):
            kn = _head_rms(kraw[:, BLK * b:BLK * (b + 1)], kg_ref[...], lo)[0]
            for odd in (0, 1):
                k2.append(_expand(kn, odd, lo))
                v2.append(_expand(vraw[:, BLK * b:BLK * (b + 1)], odd, lo))
        s2 = [_nt(qn[p], k2[p // 2]) * (HEAD ** -0.5) for p in range(n_pairs)]
        soft = [_softmax_pair(s2[p], valid, sink_ref[2 * p], sink_ref[2 * p + 1]) for p in range(n_pairs)]
        lane = lax.broadcasted_iota(jnp.int32, (1, BLK), 1)
        sink_p = jnp.zeros((BLK, BLK), F32)
        for p in range(n_pairs):
            probs_ref[:, 4 * BLK * p:4 * BLK * (p + 1)] = soft[p][0]
            for half in (0, 1):
                sink_p = jnp.where(lane == 2 * p + half, soft[p][1][half], sink_p)
            o_ref[:, BLK * p:BLK * (p + 1)] = _nn(soft[p][0].astype(BF16), v2[p // 2])
        sink_p_ref[...] = sink_p

    return pl.pallas_call(
        body, name="attn_fwd", grid=(t // BLK,),
        in_specs=_attn_specs(t, aw, kv),
        out_specs=[pl.BlockSpec((BLK, aw), lambda n: (n, 0)), pl.BlockSpec((BLK, 4 * aw), lambda n: (n, 0)),
                   pl.BlockSpec((BLK, BLK), lambda n: (n, 0))],
        out_shape=[SDS((t, aw), F32), SDS((t, 4 * aw), F32), SDS((t, BLK), F32)],
        compiler_params=_params("arbitrary"),
    )(sinks, proj, proj, proj, proj, proj, qg, kg)


def _attn_bwd(proj, dy, probs, sink_p, qg, kg, sinks, aw, kv, after=None):
    t = proj.shape[0]
    nb = t // BLK
    n_kvblk = kv // BLK
    kb = aw // kv

    def body(sink_ref, q_ref, kp_ref, kc_ref, vp_ref, vc_ref, qg_ref, kg_ref, dy_ref, kall_ref, probs_ref, sink_p_ref,
             dqkv_ref, dstat_ref, dk_acc, dv_acc, dqg_acc):
        del sink_ref
        n = pl.program_id(0)
        lane = lax.broadcasted_iota(jnp.int32, (1, BLK), 1)
        lo = lane < HEAD

        @pl.when(n == 0)
        def _():
            dk_acc[...] = jnp.zeros_like(dk_acc)
            dv_acc[...] = jnp.zeros_like(dv_acc)
            dqg_acc[...] = jnp.zeros_like(dqg_acc)
            dstat_ref[...] = jnp.zeros_like(dstat_ref)

        kraw = jnp.concatenate([kp_ref[...], kc_ref[...]], axis=0)
        vraw = jnp.concatenate([vp_ref[...], vc_ref[...]], axis=0)
        row = pl.multiple_of(n * BLK, BLK)
        prow = pl.multiple_of(jnp.maximum(n - 1, 0) * BLK, BLK)
        pairs = range(aw // BLK)
        cols = [slice(BLK * p, BLK * (p + 1)) for p in pairs]
        qs = [_head_rms(q_ref[:, cols[p]], qg_ref[...], lo) for p in pairs]
        qb = [qs[p][0].astype(BF16) for p in pairs]
        k2, v2 = [], []
        for b in range(n_kvblk):
            kn = _head_rms(kraw[:, BLK * b:BLK * (b + 1)], kg_ref[...], lo)[0]
            for odd in (0, 1):
                k2.append(_expand(kn, odd, lo))
                v2.append(_expand(vraw[:, BLK * b:BLK * (b + 1)], odd, lo))
        p2 = [probs_ref[:, 4 * BLK * p:4 * BLK * (p + 1)] for p in pairs]
        dob = [dy_ref[:, cols[p]].astype(BF16) for p in pairs]
        dp2 = [_nt(dob[p], v2[p // 2]) for p in pairs]
        deltas = jnp.zeros((BLK, BLK), F32)
        ds2 = []
        for p in pairs:
            ds = []
            for half in (0, 1):
                hs = slice(2 * BLK * half, 2 * BLK * (half + 1))
                ph = p2[p][:, hs]
                delta = jnp.sum(ph * dp2[p][:, hs], axis=-1, keepdims=True)
                ds.append(ph * (dp2[p][:, hs] - delta))
                deltas = jnp.where(lane == 2 * p + half, delta, deltas)
            ds2.append((jnp.concatenate(ds, axis=1) * (HEAD ** -0.5)).astype(BF16))
        dstat_ref[2:3, :] -= jnp.sum(sink_p_ref[...] * deltas, axis=0, keepdims=True)
        dqn = [_nn(ds2[p], k2[p // 2]) for p in pairs]
        dk2t = [_tn(qb[p], ds2[p]) for p in pairs]
        dv2t = [_tn(dob[p], p2[p].astype(BF16)) for p in pairs]
        for p in pairs:
            _, qhat, rstd = qs[p]
            dqhat = dqn[p] * qg_ref[...]
            proj_q = _segsum(dqhat * qhat, lo) * (1.0 / HEAD)
            dqkv_ref[pl.ds(row, BLK), cols[p]] = (rstd * (dqhat - qhat * proj_q)).astype(BF16)
            dqg_acc[:, cols[p]] += jnp.sum(dqn[p] * qhat, axis=0, keepdims=True)
        def head_sum(parts, j):
            return sum(a[:HEAD, :2 * BLK] + a[HEAD:, 2 * BLK:] for a in parts[2 * j:2 * j + 2])

        for b in range(n_kvblk):
            dkn = jnp.concatenate([head_sum(dk2t, 2 * b), head_sum(dk2t, 2 * b + 1)], axis=0).T
            dvb = jnp.concatenate([head_sum(dv2t, 2 * b), head_sum(dv2t, 2 * b + 1)], axis=0).T
            kcols = slice(BLK * b, BLK * (b + 1))
            dk_acc[pl.ds(prow, BLK), kcols] += dkn[:BLK]
            dv_acc[pl.ds(prow, BLK), kcols] += dvb[:BLK]
            dk_acc[pl.ds(row, BLK), kcols] += dkn[BLK:]
            dv_acc[pl.ds(row, BLK), kcols] += dvb[BLK:]

        @pl.when(n == nb - 1)
        def _():
            dqg = dqg_acc[:, 0:BLK]
            for p in range(1, aw // BLK):
                dqg = dqg + dqg_acc[:, BLK * p:BLK * (p + 1)]
            dstat_ref[0:1, :] = dqg + pltpu.roll(dqg, HEAD, axis=1)

            def kblock(i, dkg):
                r = pl.multiple_of(i * BLK, BLK)
                for b in range(n_kvblk):
                    kcols = slice(BLK * b, BLK * (b + 1))
                    _, khat, rstd = _head_rms(kall_ref[pl.ds(r, BLK), kcols], kg_ref[...], lo)
                    dkn = dk_acc[pl.ds(r, BLK), kcols]
                    dkhat = dkn * kg_ref[...]
                    proj_k = _segsum(dkhat * khat, lo) * (1.0 / HEAD)
                    dqkv_ref[pl.ds(r, BLK), aw + BLK * b:aw + BLK * (b + 1)] = (rstd * (dkhat - khat * proj_k)).astype(BF16)
                    dqkv_ref[pl.ds(r, BLK), aw + kv + BLK * b:aw + kv + BLK * (b + 1)] = dv_acc[pl.ds(r, BLK), kcols].astype(BF16)
                    dkg = dkg + jnp.sum(dkn * khat, axis=0, keepdims=True)
                return dkg

            dkg = lax.fori_loop(0, nb, kblock, jnp.zeros((1, BLK), F32))
            dstat_ref[1:2, :] = dkg + pltpu.roll(dkg, HEAD, axis=1)

    in_specs = _attn_specs(t, aw, kv) + [
        pl.BlockSpec((BLK, aw), lambda n: (n, 0)),
        pl.BlockSpec((t, kv), lambda n: (0, kb)),
        pl.BlockSpec((BLK, 4 * aw), lambda n: (n, 0)),
        pl.BlockSpec((BLK, BLK), lambda n: (n, 0)),
    ]
    args = [sinks, proj, proj, proj, proj, proj, qg, kg, dy, proj, probs, sink_p]
    body, in_specs, args = _ordered_behind(body, len(args), in_specs, args, after)
    return pl.pallas_call(
        body, name="attn_bwd", grid=(nb,),
        in_specs=in_specs,
        out_specs=[pl.BlockSpec((t, aw + 2 * kv), lambda n: (0, 0)), pl.BlockSpec((8, BLK), lambda n: (0, 0))],
        out_shape=[SDS((t, aw + 2 * kv), BF16), SDS((8, BLK), F32)],
        scratch_shapes=[pltpu.VMEM((t, kv), F32), pltpu.VMEM((t, kv), F32), pltpu.VMEM((1, aw), F32)],
        compiler_params=_params("arbitrary"),
    )(*args)


def _conv_taps(win, w_ref, shift_of):
    rows = win.shape[0]
    acc = None
    for j in range(CONV_K):
        term = pltpu.roll(win, (rows - shift_of(j)) % rows, axis=0)[:BLK] * w_ref[j:j + 1, :]
        acc = term if acc is None else acc + term
    return acc


def _layer_norm_fwd(z):
    mu = jnp.mean(z, axis=-1, keepdims=True)
    zc = z - mu
    rstd = lax.rsqrt(jnp.mean(zc * zc, axis=-1, keepdims=True) + EPS)
    return zc * rstd, rstd


def _layer_norm_bwd(dy, yhat, rstd, g):
    dyh = dy * g
    return rstd * (dyh - jnp.mean(dyh, axis=-1, keepdims=True) - yhat * jnp.mean(dyh * yhat, axis=-1, keepdims=True))


def _conv_fill_glu(a_ref, g_ref, hpad, nb):
    hpad[0:HALO, :] = jnp.zeros((HALO, hpad.shape[1]), F32)

    def fill(i, c):
        r = pl.multiple_of(i * BLK, BLK)
        hpad[pl.ds(pl.multiple_of(r + HALO, HALO), BLK), :] = a_ref[pl.ds(r, BLK), :] * _sigmoid(g_ref[pl.ds(r, BLK), :])
        return c

    lax.fori_loop(0, nb, fill, 0)


def _conv_specs(t, cw, d_in):
    base = (d_in - 4 * cw) // cw
    vec = pl.BlockSpec((1, cw), lambda i: (0, 0))
    return [
        pl.BlockSpec((t, cw), lambda i: (0, base)),
        pl.BlockSpec((t, cw), lambda i: (0, base + 1)),
        pl.BlockSpec((HALO, cw), lambda i: (0, 0)),
        vec, vec, vec,
    ]


def _conv_fwd(proj, cw_pad, cb, lg, lb, cw):
    t, d_in = proj.shape
    nb = t // BLK

    def body(a_ref, g_ref, w_ref, b_ref, lg_ref, lb_ref, o_ref, z_ref, hpad):
        _conv_fill_glu(a_ref, g_ref, hpad, nb)

        def blk(i, c):
            r = pl.multiple_of(i * BLK, BLK)
            z = _conv_taps(hpad[pl.ds(r, BLK + HALO), :], w_ref, lambda j: j + HALO - (CONV_K - 1)) + b_ref[...]
            z_ref[pl.ds(r, BLK), :] = z
            yhat, _ = _layer_norm_fwd(z)
            y = yhat * lg_ref[...] + lb_ref[...]
            o_ref[pl.ds(r, BLK), :] = y * _sigmoid(y)
            return c

        lax.fori_loop(0, nb, blk, 0)

    whole = pl.BlockSpec((t, cw), lambda i: (0, 0))
    return pl.pallas_call(
        body, name="conv_fwd", grid=(1,),
        in_specs=_conv_specs(t, cw, d_in),
        out_specs=[whole, whole],
        out_shape=[SDS((t, cw), F32), SDS((t, cw), F32)],
        scratch_shapes=[pltpu.VMEM((t + HALO, cw), F32)],
        compiler_params=_params("arbitrary"),
    )(proj, proj, cw_pad, cb, lg, lb)


def _conv_bwd(proj, dy, z, cw_pad, cb, lg, lb, cw):
    t, d_in = proj.shape
    nb = t // BLK

    def body(a_ref, g_ref, w_ref, b_ref, lg_ref, lb_ref, dy_ref, z_ref, dc_ref, dw_ref, dvec_ref, hpad, dzpad, dwacc):
        del b_ref
        _conv_fill_glu(a_ref, g_ref, hpad, nb)
        dzpad[t:t + HALO, :] = jnp.zeros((HALO, cw), F32)
        dwacc[...] = jnp.zeros_like(dwacc)

        def blk(i, carry):
            db, dlg, dlb = carry
            r = pl.multiple_of(i * BLK, BLK)
            win = hpad[pl.ds(r, BLK + HALO), :]
            yhat, rstd = _layer_norm_fwd(z_ref[pl.ds(r, BLK), :])
            y = yhat * lg_ref[...] + lb_ref[...]
            sg = _sigmoid(y)
            dyl = dy_ref[pl.ds(r, BLK), :] * (sg * (1.0 + y * (1.0 - sg)))
            dz = _layer_norm_bwd(dyl, yhat, rstd, lg_ref[...])
            dzpad[pl.ds(r, BLK), :] = dz
            for j in range(CONV_K):
                sh = j + HALO - (CONV_K - 1)
                prod = dz * pltpu.roll(win, (BLK + HALO - sh) % (BLK + HALO), axis=0)[:BLK]
                dwacc[8 * j:8 * j + 8, :] += jnp.sum(prod.reshape(BLK // 8, 8, cw), axis=0)
            return (db + jnp.sum(dz, axis=0, keepdims=True),
                    dlg + jnp.sum(dyl * yhat, axis=0, keepdims=True),
                    dlb + jnp.sum(dyl, axis=0, keepdims=True))

        zero = jnp.zeros((1, cw), F32)
        db, dlg, dlb = lax.fori_loop(0, nb, blk, (zero, zero, zero))
        dvec_ref[...] = jnp.zeros_like(dvec_ref)
        dvec_ref[0:1, :] = db
        dvec_ref[1:2, :] = dlg
        dvec_ref[2:3, :] = dlb
        dw_ref[...] = jnp.sum(dwacc[...].reshape(HALO, 8, cw), axis=1)

        def blk2(i, c):
            r = pl.multiple_of(i * BLK, BLK)
            dh = _conv_taps(dzpad[pl.ds(r, BLK + HALO), :], w_ref, lambda j: CONV_K - 1 - j)
            a = a_ref[pl.ds(r, BLK), :]
            sg = _sigmoid(g_ref[pl.ds(r, BLK), :])
            dc_ref[pl.ds(r, BLK), 0:cw] = (dh * sg).astype(BF16)
            dc_ref[pl.ds(r, BLK), cw:2 * cw] = (dh * a * sg * (1.0 - sg)).astype(BF16)
            return c

        lax.fori_loop(0, nb, blk2, 0)

    return pl.pallas_call(
        body, name="conv_bwd", grid=(1,),
        in_specs=_conv_specs(t, cw, d_in) + [pl.BlockSpec((t, cw), lambda i: (0, 0))] * 2,
        out_specs=[pl.BlockSpec((t, 2 * cw), lambda i: (0, 0)), pl.BlockSpec((HALO, cw), lambda i: (0, 0)),
                   pl.BlockSpec((8, cw), lambda i: (0, 0))],
        out_shape=[SDS((t, 2 * cw), BF16), SDS((HALO, cw), F32), SDS((8, cw), F32)],
        scratch_shapes=[pltpu.VMEM((t + HALO, cw), F32), pltpu.VMEM((t + HALO, cw), F32), pltpu.VMEM((8 * HALO, cw), F32)],
        compiler_params=_params("arbitrary"),
    )(proj, proj, cw_pad, cb, lg, lb, dy, z)


def _tril_bf16(w):
    r = lax.broadcasted_iota(jnp.int32, (BLK, BLK), 0)
    c = lax.broadcasted_iota(jnp.int32, (BLK, BLK), 1)
    return jnp.where(r >= c, w, 0.0).astype(BF16)


def _sgu_specs(sw, d_in, heads):
    base = (d_in - 2 * sw) // sw
    vec = pl.BlockSpec((1, sw), lambda n: (0, 0))
    return [
        pl.BlockSpec((BLK, sw), lambda n: (n, base)),
        pl.BlockSpec((BLK, sw), lambda n: (n, base + 1)),
        vec, vec,
        pl.BlockSpec((heads, BLK, BLK), lambda n: (0, 0, 0)),
        pl.BlockSpec((BLK, sw), lambda n: (0, 0)),
    ]


def _sgu_mix(w_ref, vnb, heads, sw, transpose):
    head_of = lax.broadcasted_iota(jnp.int32, (1, sw), 1) // HEAD
    s = jnp.zeros((BLK, sw), F32)
    for h in range(heads):
        wt = _tril_bf16(w_ref[h])
        mixed = _tn(wt, vnb) if transpose else _nn(wt, vnb)
        s = jnp.where(head_of == h, mixed, s)
    return s


def _sgu_fwd(proj, lg, lb, w, bias_full, sw):
    t, d_in = proj.shape
    heads = sw // HEAD

    def body(u_ref, v_ref, lg_ref, lb_ref, w_ref, bias_ref, o_ref, s_ref):
        vhat, _ = _layer_norm_fwd(v_ref[...])
        vn = (vhat * lg_ref[...] + lb_ref[...]).astype(BF16)
        s = _sgu_mix(w_ref, vn, heads, sw, False) + bias_ref[...]
        s_ref[...] = s
        o_ref[...] = u_ref[...] * s

    tile = pl.BlockSpec((BLK, sw), lambda n: (n, 0))
    return pl.pallas_call(
        body, name="sgu_fwd", grid=(t // BLK,),
        in_specs=_sgu_specs(sw, d_in, heads),
        out_specs=[tile, tile],
        out_shape=[SDS((t, sw), F32), SDS((t, sw), F32)],
        compiler_params=_params("arbitrary"),
    )(proj, proj, lg, lb, w, bias_full)


def _sgu_bwd(proj, dy, gate, lg, lb, w, bias_full, sw, after=None):
    t, d_in = proj.shape
    heads = sw // HEAD
    nb = t // BLK

    def body(u_ref, v_ref, lg_ref, lb_ref, w_ref, bias_ref, dy_ref, s_ref, ds_ref, dw_ref, db_ref, dvec_ref, dbfull):
        del bias_ref
        n = pl.program_id(0)

        @pl.when(n == 0)
        def _():
            dw_ref[...] = jnp.zeros_like(dw_ref)
            dvec_ref[...] = jnp.zeros_like(dvec_ref)
            dbfull[...] = jnp.zeros_like(dbfull)

        vhat, rstd = _layer_norm_fwd(v_ref[...])
        vn = (vhat * lg_ref[...] + lb_ref[...]).astype(BF16)
        dy = dy_ref[...]
        ds_ref[:, 0:sw] = (dy * s_ref[...]).astype(BF16)
        dsv = dy * u_ref[...]
        dbfull[...] += dsv
        head_of = lax.broadcasted_iota(jnp.int32, (1, sw), 1) // HEAD
        r = lax.broadcasted_iota(jnp.int32, (BLK, BLK), 0)
        c = lax.broadcasted_iota(jnp.int32, (BLK, BLK), 1)
        dsb = dsv.astype(BF16)
        for h in range(heads):
            dsh = jnp.where(head_of == h, dsv, 0.0).astype(BF16)
            dw_ref[h] += jnp.where(r >= c, _nt(dsh, vn), 0.0)
        dvn = _sgu_mix(w_ref, dsb, heads, sw, True)
        dvec_ref[0:1, :] += jnp.sum(dvn * vhat, axis=0, keepdims=True)
        dvec_ref[1:2, :] += jnp.sum(dvn, axis=0, keepdims=True)
        ds_ref[:, sw:2 * sw] = _layer_norm_bwd(dvn, vhat, rstd, lg_ref[...]).astype(BF16)

        @pl.when(n == nb - 1)
        def _():
            sel = (lax.broadcasted_iota(jnp.int32, (sw, BLK), 0) // HEAD == lax.broadcasted_iota(jnp.int32, (sw, BLK), 1)).astype(BF16)
            x = dbfull[...]
            hi = x.astype(BF16)
            r1 = x - hi.astype(F32)
            mid = r1.astype(BF16)
            low = (r1 - mid.astype(F32)).astype(BF16)
            db_ref[...] = _nn(hi, sel) + _nn(mid, sel) + _nn(low, sel)

    in_specs = _sgu_specs(sw, d_in, heads) + [pl.BlockSpec((BLK, sw), lambda n: (n, 0))] * 2
    body, in_specs, args = _ordered_behind(body, 8, in_specs, [proj, proj, lg, lb, w, bias_full, dy, gate], after)
    return pl.pallas_call(
        body, name="sgu_bwd", grid=(nb,),
        in_specs=in_specs,
        out_specs=[pl.BlockSpec((BLK, 2 * sw), lambda n: (n, 0)), pl.BlockSpec((heads, BLK, BLK), lambda n: (0, 0, 0)),
                   pl.BlockSpec((BLK, BLK), lambda n: (0, 0)), pl.BlockSpec((8, sw), lambda n: (0, 0))],
        out_shape=[SDS((t, 2 * sw), BF16), SDS((heads, BLK, BLK), F32), SDS((BLK, BLK), F32), SDS((8, sw), F32)],
        scratch_shapes=[pltpu.VMEM((BLK, sw), F32)],
        compiler_params=_params("arbitrary"),
    )(*args)


def _rms_fwd(x, g):
    return (x * lax.rsqrt(jnp.mean(x * x, axis=-1, keepdims=True) + EPS)) * g


def _rms_bwd(dh, x, g):
    rstd = lax.rsqrt(jnp.mean(x * x, axis=-1, keepdims=True) + EPS)
    xhat = x * rstd
    dgx = dh * g
    dx = rstd * (dgx - xhat * jnp.mean(dgx * xhat, axis=-1, keepdims=True))
    return dx, jnp.sum(dh * xhat, axis=0, keepdims=True)


def _rms_matmul(x, g, w, tm, tn, relu2, name, transposed=False):
    t, d = x.shape
    if w.ndim == 3:
        assert w.shape[2] == tn
        n = w.shape[0] * tn
        w_spec = pl.BlockSpec((None, d, tn), lambda i, j: (j, 0, 0))
    elif transposed:
        n = w.shape[0]
        w_spec = pl.BlockSpec((tn, d), lambda i, j: (j, 0))
    else:
        n = w.shape[1]
        w_spec = pl.BlockSpec((d, tn), lambda i, j: (0, j))

    def body(x_ref, g_ref, w_ref, h_ref, *outs):
        @pl.when(pl.program_id(1) == 0)
        def _():
            h_ref[...] = _rms_fwd(x_ref[...], g_ref[...]).astype(BF16)

        acc = _nt(h_ref[...], w_ref[...]) if transposed else _nn(h_ref[...], w_ref[...])
        if relu2:
            r = jnp.maximum(acc, 0.0)
            outs[0][...] = (r * r).astype(BF16)
            outs[1][...] = r.astype(BF16)
        else:
            outs[0][...] = acc

    tile = pl.BlockSpec((tm, tn), lambda i, j: (i, j))
    row = pl.BlockSpec((tm, d), lambda i, j: (i, 0))
    outs = [SDS((t, n), BF16), SDS((t, n), BF16)] if relu2 else [SDS((t, n), F32)]
    return pl.pallas_call(
        body, name=name, grid=(t // tm, n // tn),
        in_specs=[row, pl.BlockSpec((1, d), lambda i, j: (0, 0)), w_spec],
        out_specs=[row] + [tile] * len(outs),
        out_shape=[SDS((t, d), BF16)] + outs,
        compiler_params=_params("parallel", "arbitrary"),
    )(x, g, w)


def _group_rms_matmul(ys, g, w, res, tm, tn):
    t, d = res.shape
    widths = [y.shape[1] for y in ys]
    k = sum(widths)

    def body(*refs):
        y_refs, (g_ref, w_ref, res_ref, mix_ref, o_ref) = refs[:len(ys)], refs[len(ys):]

        @pl.when(pl.program_id(1) == 0)
        def _():
            c = 0
            for y_ref, wd in zip(y_refs, widths):
                mix_ref[:, c:c + wd] = _rms_fwd(y_ref[...], g_ref[:, c:c + wd]).astype(BF16)
                c += wd

        o_ref[...] = res_ref[...] + _nn(mix_ref[...], w_ref[...])

    tile = pl.BlockSpec((tm, tn), lambda i, j: (i, j))
    return pl.pallas_call(
        body, name="mix_out", grid=(t // tm, d // tn),
        in_specs=[pl.BlockSpec((tm, wd), lambda i, j: (i, 0)) for wd in widths] + [
            pl.BlockSpec((1, k), lambda i, j: (0, 0)), pl.BlockSpec((k, tn), lambda i, j: (0, j)), tile],
        out_specs=[pl.BlockSpec((tm, k), lambda i, j: (i, 0)), tile],
        out_shape=[SDS((t, k), BF16), SDS((t, d), F32)],
        compiler_params=_params("parallel", "arbitrary"),
    )(*ys, g, w, res)


def _matmul_res(a, w, res, tm, tn, tk):
    t, k = a.shape
    n = w.shape[1]
    nk = k // tk

    def body(a_ref, w_ref, res_ref, o_ref, acc):
        kk = pl.program_id(2)

        @pl.when(kk == 0)
        def _():
            acc[...] = res_ref[...]

        acc[...] += _nn(a_ref[...], w_ref[...])

        @pl.when(kk == nk - 1)
        def _():
            o_ref[...] = acc[...]

    tile = pl.BlockSpec((tm, tn), lambda i, j, kk: (i, j))
    return pl.pallas_call(
        body, name="mlp_down", grid=(t // tm, n // tn, nk),
        in_specs=[pl.BlockSpec((tm, tk), lambda i, j, kk: (i, kk)), pl.BlockSpec((tk, tn), lambda i, j, kk: (kk, j)), tile],
        out_specs=tile,
        out_shape=SDS((t, n), F32),
        scratch_shapes=[pltpu.VMEM((tm, tn), F32)],
        compiler_params=_params("parallel", "parallel", "arbitrary"),
    )(a, w, res)


def _loss_grad(y, target, tm):
    t, d = y.shape

    def body(y_ref, t_ref, dy_ref, dyb_ref, l_ref):
        @pl.when(pl.program_id(0) == 0)
        def _():
            l_ref[...] = jnp.zeros_like(l_ref)

        err = y_ref[...] - t_ref[...]
        dy = err * (1.0 / d)
        dy_ref[...] = dy
        dyb_ref[...] = dy.astype(BF16)
        per_row = jnp.mean(err * err, axis=-1, keepdims=True)
        l_ref[...] += jnp.sum(per_row, axis=0, keepdims=True) * 0.5

    row = pl.BlockSpec((tm, d), lambda i: (i, 0))
    return pl.pallas_call(
        body, name="loss_grad", grid=(t // tm,),
        in_specs=[row, row], out_specs=[row, row, pl.BlockSpec((8, BLK), lambda i: (0, 0))],
        out_shape=[SDS((t, d), F32), SDS((t, d), BF16), SDS((8, BLK), F32)],
        compiler_params=_params("arbitrary"),
    )(y, target)


def _mlp_dact(dxb, w_down, r, tn, after=None):
    t, d = dxb.shape
    f = w_down.shape[0]

    def body(dxb_ref, w_ref, r_ref, dup_ref):
        dup_ref[...] = (_nt(dxb_ref[...], w_ref[...]) * (2.0 * r_ref[...].astype(F32))).astype(BF16)

    tile = pl.BlockSpec((t, tn), lambda j: (0, j))
    body, in_specs, args = _ordered_behind(
        body, 3, [pl.BlockSpec((t, d), lambda j: (0, 0)), pl.BlockSpec((tn, d), lambda j: (j, 0)), tile],
        [dxb, w_down, r], after)
    return pl.pallas_call(
        body, name="mlp_dact", grid=(f // tn,),
        in_specs=in_specs, out_specs=tile,
        out_shape=SDS((t, f), BF16),
        compiler_params=_params("arbitrary"),
    )(*args)


def _grad_w_stacked(pieces, b, tm, name, after=None):
    t, n = b.shape
    tiles = [p.shape[1] // tm for p in pieces]
    first = [sum(tiles[:k]) for k in range(len(pieces))]

    def body(*refs):
        b_ref, o_ref = refs[len(pieces)], refs[-1]
        i = pl.program_id(0)
        for k, p_ref in enumerate(refs[:len(pieces)]):
            @pl.when((i >= first[k]) & (i < first[k] + tiles[k]))
            def _(p_ref=p_ref):
                o_ref[...] = _tn(p_ref[...], b_ref[...]).astype(BF16)

    in_specs = [pl.BlockSpec((t, tm), lambda i, k=k: (0, jnp.clip(i - first[k], 0, tiles[k] - 1))) for k in range(len(pieces))]
    in_specs.append(pl.BlockSpec((t, n), lambda i: (0, 0)))
    body, in_specs, args = _ordered_behind(body, len(pieces) + 1, in_specs, [*pieces, b], after)
    return pl.pallas_call(
        body, name=name, grid=(sum(tiles),),
        in_specs=in_specs, out_specs=pl.BlockSpec((tm, n), lambda i: (i, 0)),
        out_shape=SDS((sum(tiles) * tm, n), BF16),
        compiler_params=_params("arbitrary"),
    )(*args)


def _grad_w_half(a, b, tm, tn, name, by_cols, sel, add=None, after=None):
    t, m = a.shape
    n = b.shape[1]
    if by_cols:
        per = n // N_DEV // tn
        grid, shape = (4, m // tm, per), (4, m, n // N_DEV)
        a_spec = pl.BlockSpec((t, tm), lambda q, i, j, s: (0, i))
        b_spec = pl.BlockSpec((t, tn), lambda q, i, j, s: (0, (2 * q + s[0]) * per + j))
    else:
        per = m // N_DEV // tm
        grid, shape = (4, per, n // tn), (4, m // N_DEV, n)
        a_spec = pl.BlockSpec((t, tm), lambda q, i, j, s: (0, (2 * q + s[0]) * per + i))
        b_spec = pl.BlockSpec((t, tn), lambda q, i, j, s: (0, j))
    assert per >= 1
    tile = pl.BlockSpec((None, tm, tn), lambda q, i, j, s: (q, i, j))

    def body(sel_ref, a_ref, b_ref, *rest):
        acc = _tn(a_ref[...], b_ref[...])
        if add is not None:
            acc = acc + rest[0][...].astype(F32)
        rest[-1][...] = acc.astype(BF16)

    in_specs, args = [a_spec, b_spec], [a, b]
    if add is not None:
        in_specs, args = in_specs + [tile], args + [add]
    if after is not None:
        in_specs, args = in_specs + [ANY], args + [after]
    return pl.pallas_call(
        body, name=name,
        grid_spec=pltpu.PrefetchScalarGridSpec(num_scalar_prefetch=1, grid=grid, in_specs=in_specs, out_specs=tile),
        out_shape=SDS(shape, BF16),
        compiler_params=_params("arbitrary", "arbitrary", "arbitrary"),
    )(sel, *args)


def _mlp_dnorm(dup, w_up, x, g, dres, tm, after=None):
    t, f = dup.shape
    d = x.shape[1]
    nk, _, tk = w_up.shape

    def body(a_ref, w_ref, x_ref, g_ref, dres_ref, dx_ref, dg_ref, acc):
        i, kk = pl.program_id(0), pl.program_id(1)

        @pl.when((i == 0) & (kk == 0))
        def _():
            dg_ref[...] = jnp.zeros_like(dg_ref)

        @pl.when(kk == 0)
        def _():
            acc[...] = jnp.zeros_like(acc)

        acc[...] += _nt(a_ref[...], w_ref[...])

        @pl.when(kk == nk - 1)
        def _():
            dx, dg = _rms_bwd(acc[...], x_ref[...], g_ref[...])
            dx_ref[...] = dres_ref[...] + dx
            dg_ref[0:1, :] += dg

    row = pl.BlockSpec((tm, d), lambda i, kk: (i, 0))
    in_specs = [pl.BlockSpec((tm, tk), lambda i, kk: (i, kk)), pl.BlockSpec((None, d, tk), lambda i, kk: (kk, 0, 0)),
                row, pl.BlockSpec((1, d), lambda i, kk: (0, 0)), row]
    body, in_specs, args = _ordered_behind(body, 5, in_specs, [dup, w_up, x, g, dres], after)
    return pl.pallas_call(
        body, name="mlp_dnorm", grid=(t // tm, nk),
        in_specs=in_specs,
        out_specs=[row, pl.BlockSpec((8, d), lambda i, kk: (0, 0))],
        out_shape=[SDS((t, d), F32), SDS((8, d), F32)],
        scratch_shapes=[pltpu.VMEM((tm, d), F32)],
        compiler_params=_params("arbitrary", "arbitrary"),
    )(*args)


def _mix_dnorm(dx, w_out, ys, g, tm, after=None):
    t, d = dx.shape
    k = w_out.shape[0]
    widths = [y.shape[1] for y in ys]

    def body(dx_ref, w_ref, *refs):
        y_refs = refs[:len(ys)]
        g_ref, dxb_ref = refs[len(ys)], refs[len(ys) + 1]
        dy_refs = refs[len(ys) + 2:2 * len(ys) + 2]
        dg_ref = refs[-1]

        @pl.when(pl.program_id(0) == 0)
        def _():
            dg_ref[...] = jnp.zeros_like(dg_ref)

        dxb = dx_ref[...].astype(BF16)
        dxb_ref[...] = dxb
        dmix = _nt(dxb, w_ref[...])
        c = 0
        for y_ref, dy_ref, wd in zip(y_refs, dy_refs, widths):
            dy, dg = _rms_bwd(dmix[:, c:c + wd], y_ref[...], g_ref[:, c:c + wd])
            dy_ref[...] = dy
            dg_ref[0:1, c:c + wd] += dg
            c += wd

    row = pl.BlockSpec((tm, d), lambda i: (i, 0))
    yspecs = [pl.BlockSpec((tm, wd), lambda i: (i, 0)) for wd in widths]
    in_specs = [row, pl.BlockSpec((k, d), lambda i: (0, 0))] + yspecs + [pl.BlockSpec((1, k), lambda i: (0, 0))]
    body, in_specs, args = _ordered_behind(body, len(in_specs), in_specs, [dx, w_out, *ys, g], after)
    return pl.pallas_call(
        body, name="mix_dnorm", grid=(t // tm,),
        in_specs=in_specs,
        out_specs=[row] + yspecs + [pl.BlockSpec((8, k), lambda i: (0, 0))],
        out_shape=[SDS((t, d), BF16)] + [SDS((t, wd), F32) for wd in widths] + [SDS((8, k), F32)],
        compiler_params=_params("arbitrary"),
    )(*args)


def _in_dnorm(dps, w_in_t, x, g, dres, tm):
    t, d = x.shape
    widths = [p.shape[1] for p in dps]
    offs = [sum(widths[:p]) for p in range(len(dps))]
    n_in = w_in_t.shape[0]

    def body(*refs):
        p_refs = refs[:len(dps)]
        w_ref, x_ref, g_ref, dres_ref, dx_ref, dxb_ref, dg_ref = refs[len(dps):]

        @pl.when(pl.program_id(0) == 0)
        def _():
            dg_ref[...] = jnp.zeros_like(dg_ref)

        acc = None
        for p_ref, off, wd in zip(p_refs, offs, widths):
            term = _nn(p_ref[...], w_ref[off:off + wd, :])
            acc = term if acc is None else acc + term
        dx, dg = _rms_bwd(acc, x_ref[...], g_ref[...])
        dx = dres_ref[...] + dx
        dx_ref[...] = dx
        dxb_ref[...] = dx.astype(BF16)
        dg_ref[0:1, :] += dg

    row = pl.BlockSpec((tm, d), lambda i: (i, 0))
    return pl.pallas_call(
        body, name="in_dnorm", grid=(t // tm,),
        in_specs=[pl.BlockSpec((tm, wd), lambda i: (i, 0)) for wd in widths] + [
            pl.BlockSpec((n_in, d), lambda i: (0, 0)), row, pl.BlockSpec((1, d), lambda i: (0, 0)), row],
        out_specs=[row, row, pl.BlockSpec((8, d), lambda i: (0, 0))],
        out_shape=[SDS((t, d), F32), SDS((t, d), BF16), SDS((8, d), F32)],
        compiler_params=_params("arbitrary"),
    )(*dps, w_in_t, x, g, dres)


def _tile(n, want):
    return min(n, want)


def _row_tile(n, want):
    return max(k for k in range(8, min(n, want) + 1, 8) if n % k == 0)


def _layer_fwd(x, p, fetch, after):
    t, d = x.shape
    aw, kv, cw, sw = d // 2, d // 8, d // 4, d // 4
    tm = _tile(t, ROWS_WIDE)
    w_in = fetch("w_in", after)
    h1, proj = _rms_matmul(x, p["ln1_g"], w_in, tm, COLS if w_in.shape[0] % COLS == 0 else COLS // 2, False, "in_proj",
                           transposed=True)
    y_attn, probs, sink_p = _attn_fwd(proj, p["qg"], p["kg"], p["sinks"], aw, kv)
    y_conv, z_conv = _conv_fwd(proj, p["conv_w"], p["conv_b"], p["conv_ln_g"], p["conv_ln_b"], cw)
    y_sgu, s_sgu = _sgu_fwd(proj, p["sgu_ln_g"], p["sgu_ln_b"], p["sgu_w"], p["sgu_bias"], sw)
    ys = [y_attn, y_conv, y_sgu]
    w_out = fetch("w_out", y_sgu)
    mix, x1 = _group_rms_matmul(ys, p["out_norm_g"], w_out, x, tm, _tile(d, COLS_WIDE // 2))
    w_up = fetch("w_up", x1)
    h2, act, r = _rms_matmul(x1, p["ln2_g"], w_up, tm, w_up.shape[2], True, "mlp_up")
    w_down = fetch("w_down", act)
    x2 = _matmul_res(act, w_down, x1, tm, _tile(d, COLS_WIDE), COLS)
    saved = dict(x=x, h1=h1, proj=proj, ys=ys, mix=mix, x1=x1, h2=h2, act=act, r=r,
                 w_in=w_in, w_out=w_out, w_up=w_up, w_down=w_down, probs=probs, sink_p=sink_p, z_conv=z_conv, s_sgu=s_sgu)
    return x2, saved


def _layer_bwd(dx2, dx2b, p, s, start, finish, share, carry):
    t, d = dx2.shape
    aw, kv, cw, sw = d // 2, d // 8, d // 4, d // 4
    tm = _tile(t, ROWS_F32)
    dup = _mlp_dact(dx2b, s["w_down"], s["r"], COLS, carry[0] if carry else None)
    tok = carry[1](dup) if carry else None
    half = functools.partial
    f = s["act"].shape[1]
    tok = start("w_down", half(_grad_w_half, s["act"], dx2b, _tile(f // N_DEV, COLS), _tile(d, COLS_WIDE), "grad_w_down", False), tok)
    dx1, d_ln2 = _mlp_dnorm(dup, s["w_up"], s["x1"], p["ln2_g"], dx2, tm, tok)
    tok = finish("w_down", dx1)
    tok = start("w_up", half(_grad_w_half, s["h2"], dup, _tile(d, ROWS_WIDE), s["w_up"].shape[2], "grad_w_up", True), tok)
    dx1b, dya, dyc, dys, d_onorm = _mix_dnorm(dx1, s["w_out"], s["ys"], p["out_norm_g"], _tile(t, ROWS_NORM), tok)
    tok = finish("w_up", dx1b)
    dsgu, d_sw, d_sb, d_svec = _sgu_bwd(s["proj"], dys, s["s_sgu"], p["sgu_ln_g"], p["sgu_ln_b"], p["sgu_w"], p["sgu_bias"], sw, tok)
    share("sgu_w", d_sw)
    tok = start("w_out", half(_grad_w_half, s["mix"], dx1b, _tile(d // N_DEV, COLS), _tile(d, COLS_WIDE), "grad_w_out", False), dsgu)
    dqkv, d_attn = _attn_bwd(s["proj"], dya, s["probs"], s["sink_p"], p["qg"], p["kg"], p["sinks"], aw, kv, tok)
    tok = finish("w_out", dqkv)
    dconv, d_cw, d_cvec = _conv_bwd(s["proj"], dyc, s["z_conv"], p["conv_w"], p["conv_b"], p["conv_ln_g"], p["conv_ln_b"], cw)
    dps = [dqkv, dconv, dsgu]
    dx, dxb, d_ln1 = _in_dnorm(dps, s["w_in"], s["x"], p["ln1_g"], dx1, _tile(t, ROWS_NORM))
    heads = sw // HEAD
    share("rest", dict(
        ln1_g=d_ln1[0], q_norm_g=d_attn[0, :HEAD], k_norm_g=d_attn[1, :HEAD], sinks=d_attn[2, :aw // HEAD],
        conv_w=d_cw[:CONV_K], conv_b=d_cvec[0], conv_ln_g=d_cvec[1], conv_ln_b=d_cvec[2],
        sgu_ln_g=d_svec[0], sgu_ln_b=d_svec[1], sgu_b=d_sb[:, :heads].T,
        out_norm_g=d_onorm[0], ln2_g=d_ln2[0]))
    tm_in = COLS if all(dp.shape[1] % COLS == 0 for dp in dps) else COLS // 2
    tok = start("w_in", _grad_w_stacked(dps, s["h1"], tm_in, "grad_w_in", tok), None)
    carry = (tok, functools.partial(finish, "w_in"))
    return dx, dxb, carry


def _layer_params(l, small):
    row = lambda v: v[l][None, :]
    two = lambda v: jnp.tile(v[l], 2)[None, :]
    return dict(
        ln1_g=row(small["ln1_g"]), ln2_g=row(small["ln2_g"]), out_norm_g=row(small["out_norm_g"]),
        qg=two(small["q_norm_g"]), kg=two(small["k_norm_g"]), sinks=small["sinks"][l],
        conv_w=jnp.pad(small["conv_w"][l], ((0, HALO - CONV_K), (0, 0))),
        conv_b=row(small["conv_b"]), conv_ln_g=row(small["conv_ln_g"]), conv_ln_b=row(small["conv_ln_b"]),
        sgu_ln_g=row(small["sgu_ln_g"]), sgu_ln_b=row(small["sgu_ln_b"]), sgu_w=small["sgu_w"][l],
        sgu_bias=jnp.repeat(small["sgu_b"][l].T, HEAD, axis=1),
    )


def _local_step(x, target, small, depth, fetch, start, finish, share, after):
    params = [_layer_params(l, small) for l in range(depth)]
    saved = []
    h = x
    for l in range(depth):
        h, s = _layer_fwd(h, params[l], functools.partial(fetch, l), after)
        after = h
        saved.append(s)
    dy, dyb, lsum = _loss_grad(h, target, _tile(x.shape[0], ROWS_F32))
    def share_with_loss(kind, grads):
        share(depth - 1, kind, dict(grads, loss=lsum[0, 0:1]) if kind == "rest" else grads)

    carry = None
    for l in reversed(range(depth)):
        dy, dyb, carry = _layer_bwd(dy, dyb, params[l], saved[l], functools.partial(start, l), functools.partial(finish, l),
                                    share_with_loss if l == depth - 1 else functools.partial(share, l), carry)
    return dy, carry


BIG = ("w_in", "w_out", "w_up", "w_down")
HBM = pl.BlockSpec(memory_space=pltpu.HBM)
SEMS = pl.BlockSpec(memory_space=pltpu.SEMAPHORE)
EFFECT = pltpu.SideEffectType.DATAFLOW_SIDE_EFFECTING


def _mesh_pos():
    return lax.axis_index("x"), lax.axis_index("y"), lax.axis_index("c")


def _other_chips(x, y):
    return [(1 - x, y), (x, 1 - y), (1 - x, 1 - y)]


def _copies(plan, refs, sends, recvs):
    x, y, c = _mesh_pos()
    return [pltpu.make_async_remote_copy(src_ref=src, dst_ref=dst, send_sem=sends.at[k], recv_sem=recvs.at[k],
                                         device_id=dev, device_id_type=MESH)
            for k, (src, dst, dev) in enumerate(plan(x, y, c, refs))]


def _gather_plan(x, y, c, refs):
    mine = refs[0].at[4 * x + 2 * y + c]
    return [(mine, mine, (x, y, 1 - c))] + [(mine, mine, (*chip, c)) for chip in _other_chips(x, y)]


def _near_plan(x, y, c, refs):
    mine = refs[0].at[4 * x + 2 * y + c]
    return [(mine, mine, (x, y, 1 - c)), (mine, mine, (1 - x, y, c)), (mine, mine, (x, 1 - y, c))]


def _relay_plan(x, y, c, refs):
    land = refs[0]
    half = land.shape[1] // 2
    return [(land.at[4 * (1 - x) + 2 * y + c, pl.ds(0, half)], land.at[4 * (1 - x) + 2 * y + c, pl.ds(0, half)], (x, 1 - y, c)),
            (land.at[4 * x + 2 * (1 - y) + c, pl.ds(half, half)], land.at[4 * x + 2 * (1 - y) + c, pl.ds(half, half)],
             (1 - x, y, c))]


def _all_plan(x, y, c, refs):
    mine = refs[0].at[4 * x + 2 * y + c]
    return [(mine, mine, (x ^ (k >> 2), y ^ ((k >> 1) & 1), c ^ (k & 1))) for k in range(1, N_DEV)]


def _pair_plan(x, y, c, refs):
    blocks, land = refs
    return [(blocks.at[2 * q + (1 - c)], land.at[q], (x, y, 1 - c)) for q in range(4)]


def _half_plan(x, y, c, refs):
    blocks, land = refs
    return [(blocks.at[q], land.at[q], (x, y, 1 - c)) for q in range(4)]


def _chips_plan(x, y, c, refs):
    sums, land = refs
    return [(sums.at[2 * chip[0] + chip[1]], land.at[k], (*chip, c)) for k, chip in enumerate(_other_chips(x, y))]


def _start_exchanges(name, groups, after=None):
    flat = [a for arrays, _, _ in groups for a in arrays]
    n_arr, n_g = len(flat), len(groups)
    n_in = n_arr + (after is not None)

    def body(*refs):
        ins, sems, token = refs[:n_arr], refs[n_in:n_in + 2 * n_g], refs[-1]
        off = 0
        for gi, (arrays, plan, _) in enumerate(groups):
            for cp in _copies(plan, ins[off:off + len(arrays)], sems[2 * gi], sems[2 * gi + 1]):
                cp.start()
            off += len(arrays)
        token[...] = jnp.zeros_like(token)

    res = pl.pallas_call(
        body, name=name,
        out_shape=[pltpu.SemaphoreType.DMA((n,)) for _, _, n in groups for _ in (0, 1)]
        + [pltpu.HBM(a.shape, a.dtype) for a in flat] + [SDS((8, BLK), F32)],
        in_specs=[HBM] * n_arr + [ANY] * (after is not None),
        out_specs=[SEMS] * (2 * n_g) + [HBM] * n_arr + [pl.BlockSpec(memory_space=pltpu.VMEM)],
        input_output_aliases={i: 2 * n_g + i for i in range(n_arr)},
        compiler_params=pltpu.CompilerParams(has_side_effects=EFFECT),
    )(*[pltpu.with_memory_space_constraint(a, pltpu.HBM) for a in flat], *([after] if after is not None else []))
    sems, thru, token = res[:2 * n_g], res[2 * n_g:2 * n_g + n_arr], res[-1]
    out, off = [], 0
    for gi, (arrays, _, _) in enumerate(groups):
        out.append((list(thru[off:off + len(arrays)]), sems[2 * gi], sems[2 * gi + 1]))
        off += len(arrays)
    return out, token


def _wait_exchange(name, arrays, sends, recvs, plan, after):
    n = len(arrays)

    def body(*refs):
        for cp in _copies(plan, refs[:n], refs[n], refs[n + 1]):
            cp.wait_send()
            cp.wait_recv()

    return pl.pallas_call(
        body, name=name,
        out_shape=[pltpu.HBM(a.shape, a.dtype) for a in arrays],
        in_specs=[HBM] * n + [SEMS, SEMS, ANY],
        out_specs=[HBM] * n,
        input_output_aliases={i: i for i in range(n)},
        compiler_params=pltpu.CompilerParams(has_side_effects=EFFECT),
    )(*arrays, sends, recvs, after)


def _wait_exchanges(name, groups, after):
    flat = [a for (arrays, _, _), _ in groups for a in arrays]
    n_arr, n_g = len(flat), len(groups)

    def body(*refs):
        off = 0
        for gi, ((arrays, _, _), plan) in enumerate(groups):
            for cp in _copies(plan, refs[off:off + len(arrays)], refs[n_arr + 2 * gi], refs[n_arr + 2 * gi + 1]):
                cp.wait_send()
                cp.wait_recv()
            off += len(arrays)

    res = pl.pallas_call(
        body, name=name,
        out_shape=[pltpu.HBM(a.shape, a.dtype) for a in flat],
        in_specs=[HBM] * n_arr + [SEMS] * (2 * n_g) + [ANY],
        out_specs=[HBM] * n_arr,
        input_output_aliases={i: i for i in range(n_arr)},
        compiler_params=pltpu.CompilerParams(has_side_effects=EFFECT),
    )(*flat, *[s for (_, sends, recvs), _ in groups for s in (sends, recvs)], after)
    out, off = [], 0
    for (arrays, _, _), _ in groups:
        out.append(list(res[off:off + len(arrays)]))
        off += len(arrays)
    return out


def _gather_finish(land, name):
    def body(land_ref, out_ref, send_sems, recv_sems):
        del land_ref
        x, y, c = _mesh_pos()
        cps = []
        for k, chip in enumerate(_other_chips(x, y)):
            block = out_ref.at[4 * chip[0] + 2 * chip[1] + c]
            cps.append(pltpu.make_async_remote_copy(
                src_ref=block, dst_ref=block, send_sem=send_sems.at[k], recv_sem=recv_sems.at[k],
                device_id=(x, y, 1 - c), device_id_type=MESH))
        for cp in cps:
            cp.start()
        for cp in cps:
            cp.wait()

    return pl.pallas_call(
        body, name=name,
        in_specs=[ANY], out_specs=ANY,
        out_shape=SDS(land.shape, land.dtype),
        input_output_aliases={0: 0},
        scratch_shapes=[pltpu.SemaphoreType.DMA((3,)), pltpu.SemaphoreType.DMA((3,))],
    )(land)


def _pair_add(own, got, c, name):
    _, r, cols = own.shape
    tr = _row_tile(r, ROWS_F32)

    def body(c_ref, own_ref, got_ref, o_ref):
        o_ref[...] = (own_ref[...].astype(F32) + got_ref[...].astype(F32)).astype(BF16)

    return pl.pallas_call(
        body, name=name,
        grid_spec=pltpu.PrefetchScalarGridSpec(
            num_scalar_prefetch=1, grid=(4, r // tr),
            in_specs=[pl.BlockSpec((None, tr, cols), lambda q, i, c_ref: (2 * q + c_ref[0], i, 0)),
                      pl.BlockSpec((None, tr, cols), lambda q, i, c_ref: (q, i, 0))],
            out_specs=pl.BlockSpec((None, tr, cols), lambda q, i, c_ref: (q, i, 0))),
        out_shape=SDS((4, r, cols), BF16),
        compiler_params=_params("arbitrary", "arbitrary"),
    )(c, own, got)


def _adamw(w, g, m, v):
    m = ADAM_B1 * m + (1.0 - ADAM_B1) * g
    v = ADAM_B2 * v + (1.0 - ADAM_B2) * (g * g)
    m_hat = m / (1.0 - ADAM_B1 ** ADAM_STEP)
    v_hat = v / (1.0 - ADAM_B2 ** ADAM_STEP)
    delta = -ADAM_LR * (m_hat / (jnp.sqrt(v_hat) + ADAM_EPS) + ADAM_WD * w)
    return delta, m, v


def _adamw_layer(chip_sum, got, chip, w, m, v, layer, prev, name, after=None):
    depth, r, cols = w.shape
    tr = _row_tile(r, ROWS_NORM)

    def body(chip_ref, sum_ref, got_ref, w_ref, m_ref, v_ref, *rest):
        g_out, d_out, m_out, v_out, token = rest[-5:]
        g = sum_ref[...].astype(F32) + got_ref[0].astype(F32) + got_ref[1].astype(F32) + got_ref[2].astype(F32)
        delta, mm, vv = _adamw(w_ref[...], g, m_ref[...], v_ref[...])
        g_out[...] = g
        d_out[...] = delta
        m_out[...] = mm
        v_out[...] = vv
        token[...] = jnp.zeros_like(token)

    shard = pl.BlockSpec((None, tr, cols), lambda i, chip_ref: (layer, i, 0))
    in_specs = [pl.BlockSpec((None, tr, cols), lambda i, chip_ref: (chip_ref[0], i, 0)),
                pl.BlockSpec((3, tr, cols), lambda i, chip_ref: (0, i, 0)), shard, shard, shard]
    args = [chip, chip_sum, got, w, m, v]
    aliases = {}
    if prev is not None:
        in_specs += [ANY] * 4
        aliases = {len(args) + k: k for k in range(4)}
        args += list(prev)
    if after is not None:
        in_specs, args = in_specs + [ANY], args + [after]
    res = pl.pallas_call(
        body, name=name,
        grid_spec=pltpu.PrefetchScalarGridSpec(
            num_scalar_prefetch=1, grid=(r // tr,), in_specs=in_specs,
            out_specs=[shard] * 4 + [pl.BlockSpec((8, BLK), lambda i, chip_ref: (0, 0))]),
        out_shape=[SDS(w.shape, F32)] * 4 + [SDS((8, BLK), F32)],
        input_output_aliases=aliases,
        compiler_params=_params("arbitrary"),
    )(*args)
    return list(res[:4]), res[4]


WIDE = ("ln1_g", "out_norm_g", "ln2_g", "conv_b", "conv_ln_g", "conv_ln_b", "sgu_ln_g", "sgu_ln_b")
NARROW = ("q_norm_g", "k_norm_g", "sinks")


def _small_rows(w):
    rows, r = {}, 0
    for n in WIDE:
        rows[n] = (r, w[n].shape[1] // BLK)
        r += rows[n][1]
    for n in NARROW + ("loss",):
        rows[n] = (r, 1)
        r += 1
    r = -(-r // 8) * 8
    rows["sgu_b"] = (r, w["sgu_b"].shape[1])
    r += -(-rows["sgu_b"][1] // 8) * 8
    rows["conv_w"] = (r, N_DEV * HALO)
    return rows, r + N_DEV * HALO


def _pack_small(small, rows, total):
    parts, r = [], 0

    def put(name, block):
        nonlocal r
        first = rows[name][0]
        if first > r:
            parts.append(jnp.zeros((first - r, BLK), F32))
        parts.append(block)
        r = first + block.shape[0]

    for n in WIDE:
        put(n, small[n].reshape(-1, BLK))
    for n in NARROW:
        put(n, jnp.pad(small[n], (0, BLK - small[n].shape[0]))[None])
    if "loss" in small:
        put("loss", jnp.pad(small["loss"], (0, BLK - 1))[None])
    put("sgu_b", small["sgu_b"])
    cw = small["conv_w"]
    per_dev = cw.shape[1] // N_DEV
    blocks = jnp.transpose(cw.reshape(CONV_K, N_DEV, per_dev), (1, 0, 2))
    put("conv_w", jnp.pad(blocks, ((0, 0), (0, HALO - CONV_K), (0, BLK - per_dev))).reshape(N_DEV * HALO, BLK))
    if total > r:
        parts.append(jnp.zeros((total - r, BLK), F32))
    return jnp.concatenate(parts)


def _small_update(me, packed, sgu_w_all, w, m, v, rows):
    depth = len(packed)
    total = packed[0].shape[1]
    names = SMALL + ("conv_w",)
    heads = w["sgu_w"].shape[1]
    per_dev = w["conv_w"].shape[2]

    def body(me_ref, *refs):
        packed_refs, sgu_refs = refs[:depth], refs[depth:2 * depth]
        ins = refs[2 * depth:2 * depth + 3 * len(names)]
        outs = refs[2 * depth + 3 * len(names):2 * depth + 7 * len(names)]
        loss_out, acc, conv = refs[-3:]
        io = {n: (ins[3 * i:3 * i + 3], outs[4 * i:4 * i + 4]) for i, n in enumerate(names)}

        def update(n):
            (w_ref, m_ref, v_ref), (g_out, d_out, m_out, v_out) = io[n]
            delta, mm, vv = _adamw(w_ref[...], g_out[...], m_ref[...], v_ref[...])
            d_out[...] = delta
            m_out[...] = mm
            v_out[...] = vv

        mine = pl.ds(pl.multiple_of(rows["conv_w"][0] + HALO * me_ref[0], 8), HALO)
        for l in range(depth):
            s = packed_refs[l][0]
            c = packed_refs[l][0, mine, :]
            for k in range(1, N_DEV):
                s = s + packed_refs[l][k]
                c = c + packed_refs[l][k, mine, :]
            acc[l] = s
            conv[l] = c
        loss_out[...] = acc[depth - 1, rows["loss"][0]:rows["loss"][0] + 1, :]
        for n in WIDE:
            first, nr = rows[n]
            for l in range(depth):
                for j in range(nr):
                    io[n][1][0][l:l + 1, BLK * j:BLK * (j + 1)] = acc[l, first + j:first + j + 1, :]
            update(n)
        for n in NARROW:
            first, lanes = rows[n][0], w[n].shape[1]
            for l in range(depth):
                io[n][1][0][l:l + 1, :] = acc[l, first:first + 1, 0:lanes]
            update(n)
        first, nr = rows["sgu_b"]
        for l in range(depth):
            io["sgu_b"][1][0][l] = acc[l, first:first + nr, :]
            io["conv_w"][1][0][l] = conv[l, 0:CONV_K, 0:per_dev]
        update("sgu_b")
        update("conv_w")
        (w_ref, m_ref, v_ref), (g_out, d_out, m_out, v_out) = io["sgu_w"]
        for l in range(depth):
            for h in range(heads):
                g = sgu_refs[l][0, h]
                for k in range(1, N_DEV):
                    g = g + sgu_refs[l][k, h]
                delta, mm, vv = _adamw(w_ref[l, h], g, m_ref[l, h], v_ref[l, h])
                g_out[l, h] = g
                d_out[l, h] = delta
                m_out[l, h] = mm
                v_out[l, h] = vv

    def whole(a):
        nd = len(a.shape)
        return pl.BlockSpec(a.shape, lambda i, me_ref: (0,) * nd)

    small_in = [t[n] for n in names for t in (w, m, v)]
    res = pl.pallas_call(
        body, name="small_update",
        grid_spec=pltpu.PrefetchScalarGridSpec(
            num_scalar_prefetch=1, grid=(1,),
            in_specs=[whole(a) for a in list(packed) + list(sgu_w_all) + small_in],
            out_specs=[whole(w[n]) for n in names for _ in range(4)] + [pl.BlockSpec((1, BLK), lambda i, me_ref: (0, 0))],
            scratch_shapes=[pltpu.VMEM((depth, total, BLK), F32), pltpu.VMEM((depth, HALO, BLK), F32)]),
        out_shape=[SDS(w[n].shape, F32) for n in names for _ in range(4)] + [SDS((1, BLK), F32)],
        compiler_params=_params("arbitrary"),
    )(me, *packed, *sgu_w_all, *small_in)
    return {n: list(res[4 * i:4 * i + 4]) for i, n in enumerate(names)}, res[-1][0, 0]


def _pack(arrays):
    flat = jnp.concatenate([a.reshape(-1) for a in arrays])
    rows = -(-flat.shape[0] // (8 * BLK)) * 8
    return jnp.pad(flat, (0, rows * BLK - flat.shape[0])).reshape(rows, BLK)


def _unpack(packed, like):
    flat = packed.reshape(-1)
    out, off = [], 0
    for a in like:
        out.append(flat[off:off + a.size].reshape(a.shape))
        off += a.size
    return out


SMALL = ("ln1_g", "q_norm_g", "k_norm_g", "sinks", "conv_b", "conv_ln_g", "conv_ln_b", "sgu_ln_g", "sgu_ln_b",
         "sgu_w", "sgu_b", "out_norm_g", "ln2_g")
ORDER = ("ln1_g", "w_in", "q_norm_g", "k_norm_g", "sinks", "conv_w", "conv_b", "conv_ln_g", "conv_ln_b", "sgu_ln_g",
         "sgu_ln_b", "sgu_w", "sgu_b", "out_norm_g", "w_out", "ln2_g", "w_up", "w_down")


def _step(x, target, w, m, v):
    depth = w["ln1_g"].shape[0]
    xpos, ypos, cpos = _mesh_pos()
    me = 4 * xpos + 2 * ypos + cpos
    c_arr = jnp.reshape(cpos, (1,)).astype(jnp.int32)
    chip_arr = jnp.reshape(2 * xpos + ypos, (1,)).astype(jnp.int32)

    d = x.shape[1]
    def own_block(shard):
        return lax.dynamic_update_slice_in_dim(lax.empty((N_DEV,) + shard.shape, shard.dtype), shard[None], me, axis=0)

    cw = w["conv_w"]
    order = [(0, "conv_w")] + [(l, n) for l in range(depth) for n in BIG]
    plans = {k: (_near_plan, 3) if k == order[-1] else (_gather_plan, 4) for k in order}
    started, gather_token = _start_exchanges("gather_start", [
        ([own_block(_pack([cw]) if n == "conv_w" else w[n][l].astype(BF16))], *plans[l, n]) for l, n in order])
    pending = dict(zip(order, started))

    def fetch(l, n, after):
        arrays, sends, recvs = pending.pop((l, n))
        land, = _wait_exchange(f"gather_wait_{l}_{n}", arrays, sends, recvs, plans[l, n][0],
                               gather_token if after is None else after)
        if plans[l, n][0] is _near_plan:
            ((arrays, sends, recvs),), relayed = _start_exchanges("gather_relay", [([land], _relay_plan, 2)])
            land, = _wait_exchange("gather_relay_wait", arrays, sends, recvs, _relay_plan, relayed)
        full = _gather_finish(land, "gather_finish_" + n)
        return full if n in ("w_up", "conv_w") else full.reshape(-1, d)

    cw_all = fetch(0, "conv_w", None)
    cw_full = jnp.concatenate([_unpack(cw_all[k], [cw])[0] for k in range(N_DEV)], axis=-1)
    small, _ = lax.optimization_barrier(({n: w[n] for n in SMALL}, gather_token))
    small["conv_w"] = cw_full

    to_sibling, inflight, shared, leaving = {}, [], {}, []
    rows, total = _small_rows(w)

    def flush(name, first=()):
        groups = list(first) + [g for _, g in leaving]
        started, token = _start_exchanges(name, groups)
        for (record, _), going in zip(leaving, started[len(first):]):
            record(going)
        leaving.clear()
        return started[:len(first)], token

    def start(l, n, grad, after):
        if callable(grad):
            blocks, plan = grad(sel=1 - c_arr, after=after), _half_plan
        else:
            blocks, plan = grad.reshape(N_DEV, -1, d), _pair_plan
        (going,), token = flush(f"grads_start_{l}_{n}", [([blocks, lax.empty((4,) + blocks.shape[1:], BF16)], plan, 4)])
        to_sibling[l, n] = (going, plan, grad)
        return token

    def finish(l, n, after):
        (arrays, sends, recvs), plan, grad = to_sibling.pop((l, n))
        blocks, from_sibling = _wait_exchange(f"pair_wait_{l}_{n}", arrays, sends, recvs, plan, after)
        if callable(grad):
            chip_sums = grad(sel=c_arr, add=from_sibling)
        else:
            chip_sums = _pair_add(blocks, from_sibling, c_arr, "pair_add_" + n)
        leaving.append((lambda going: inflight.append((l, n, going)),
                        ([chip_sums, lax.empty((3,) + chip_sums.shape[1:], BF16)], _chips_plan, 3)))
        return chip_sums

    def share(l, kind, grads):
        block = grads if kind == "sgu_w" else _pack_small(grads, rows, total)
        leaving.append((lambda going: shared.__setitem__((l, kind), going), ([own_block(block)], _all_plan, N_DEV - 1)))

    grad_x, (after, finish_last) = _local_step(x, target, small, depth, fetch, start, finish, share, None)

    as3d = lambda a: a.reshape(depth, -1, a.shape[-1])
    results = {n: None for n in BIG}

    landed = {}

    def await_layer(name, reduces, after):
        smalls = sorted(k for k in shared if k not in landed and all(k[0] >= l for l, _, _ in reduces))
        got = _wait_exchanges(name, [(going, _chips_plan) for _, _, going in reduces]
                              + [(shared[k], _all_plan) for k in smalls], after)
        for k, (arr,) in zip(smalls, got[len(reduces):]):
            landed[k] = arr
        return got[:len(reduces)]

    def update(l, n, chip_sums, got, after):
        results[n], token = _adamw_layer(chip_sums, got, chip_arr, as3d(w[n]), as3d(m[n]), as3d(v[n]), l, results[n],
                                         f"adamw_{l}_{n}", after)
        return token

    first, rest = inflight[:4], inflight[4:]
    arrived = await_layer("grads_wait_first", first, after)
    after = None
    for (l, n, _), (chip_sums, got) in list(zip(first, arrived))[:2]:
        after = update(l, n, chip_sums, got, after)
    finish_last(after)
    _, after = flush("grads_start_last")
    for (l, n, _), (chip_sums, got) in list(zip(first, arrived))[2:]:
        after = update(l, n, chip_sums, got, after)
    arrived = await_layer("grads_wait_rest", rest, after)
    out, loss = _small_update(jnp.reshape(me, (1,)).astype(jnp.int32), [landed[l, "rest"] for l in range(depth)],
                              [landed[l, "sgu_w"] for l in range(depth)], w, m, v, rows)
    after = out["sinks"][1]
    for (l, n, _), (chip_sums, got) in zip(rest, arrived):
        after = update(l, n, chip_sums, got, after)
    (chip_sums, got), = await_layer("grads_wait_last", inflight[-1:], after)
    l, n, _ = inflight[-1]
    update(l, n, chip_sums, got, None)
    out.update({n: [r.reshape(w[n].shape) for r in results[n]] for n in BIG})
    return (loss, grad_x[None]) + tuple(out[n][k] for k in range(4) for n in ORDER)


def kernel(x, ln1_g, w_in, q_norm_g, k_norm_g, sinks, conv_w, conv_b, conv_ln_g, conv_ln_b, sgu_ln_g, sgu_ln_b, sgu_w, sgu_b, out_norm_g, w_out, ln2_g, w_up, w_down, loss_target, m_ln1_g, m_w_in, m_q_norm_g, m_k_norm_g, m_sinks, m_conv_w, m_conv_b, m_conv_ln_g, m_conv_ln_b, m_sgu_ln_g, m_sgu_ln_b, m_sgu_w, m_sgu_b, m_out_norm_g, m_w_out, m_ln2_g, m_w_up, m_w_down, v_ln1_g, v_w_in, v_q_norm_g, v_k_norm_g, v_sinks, v_conv_w, v_conv_b, v_conv_ln_g, v_conv_ln_b, v_sgu_ln_g, v_sgu_ln_b, v_sgu_w, v_sgu_b, v_out_norm_g, v_w_out, v_ln2_g, v_w_up, v_w_down):
    w = dict(zip(ORDER, (ln1_g, w_in, q_norm_g, k_norm_g, sinks, conv_w, conv_b, conv_ln_g, conv_ln_b, sgu_ln_g, sgu_ln_b,
                         sgu_w, sgu_b, out_norm_g, w_out, ln2_g, w_up, w_down)))
    m = dict(zip(ORDER, (m_ln1_g, m_w_in, m_q_norm_g, m_k_norm_g, m_sinks, m_conv_w, m_conv_b, m_conv_ln_g, m_conv_ln_b,
                         m_sgu_ln_g, m_sgu_ln_b, m_sgu_w, m_sgu_b, m_out_norm_g, m_w_out, m_ln2_g, m_w_up, m_w_down)))
    v = dict(zip(ORDER, (v_ln1_g, v_w_in, v_q_norm_g, v_k_norm_g, v_sinks, v_conv_w, v_conv_b, v_conv_ln_g, v_conv_ln_b,
                         v_sgu_ln_g, v_sgu_ln_b, v_sgu_w, v_sgu_b, v_out_norm_g, v_w_out, v_ln2_g, v_w_up, v_w_down)))
    for group in (w, m, v):
        group["w_in"] = jnp.swapaxes(group["w_in"], 1, 2)
    out = list(_step(x[0], loss_target[0], w, m, v))
    for k in range(4):
        i = 2 + k * len(ORDER) + ORDER.index("w_in")
        out[i] = jnp.swapaxes(out[i], 1, 2)
    return tuple(out)
```

```python
import functools

import jax
import jax.numpy as jnp
from jax import lax
from jax.experimental import pallas as pl
from jax.experimental.pallas import tpu as pltpu

F32 = jnp.float32
BF16 = jnp.bfloat16
SDS = jax.ShapeDtypeStruct

EPS = 1e-6
NEG_INF = -1e30
HEAD = 64
BLK = 128
CONV_K = 31
HALO = 32
N_DEV = 8

ADAM_LR = 0.001
ADAM_B1 = 0.9
ADAM_B2 = 0.999
ADAM_EPS = 1e-08
ADAM_WD = 0.01
ADAM_STEP = 10

VMEM_LIMIT = 56 * 1024 * 1024

ROWS_WIDE = 1024
ROWS_F32 = 512
ROWS_NORM = 256
COLS = 512
COLS_WIDE = 2048

MESH = pl.DeviceIdType.MESH


def _params(*sem):
    return pltpu.CompilerParams(dimension_semantics=sem, vmem_limit_bytes=VMEM_LIMIT)


def _nt(a, b):
    return lax.dot_general(a, b, (((1,), (1,)), ((), ())), preferred_element_type=F32)


def _tn(a, b):
    return lax.dot_general(a, b, (((0,), (0,)), ((), ())), preferred_element_type=F32)


def _nn(a, b):
    return jnp.dot(a, b, preferred_element_type=F32)


def _sigmoid(x):
    return 1.0 / (1.0 + jnp.exp(-x))


ANY = pl.BlockSpec(memory_space=pl.ANY)


def _ordered_behind(body, n_in, in_specs, args, after):
    if after is None:
        return body, in_specs, args
    return (lambda *refs: body(*refs[:n_in], *refs[n_in + 1:])), list(in_specs) + [ANY], list(args) + [after]


def _segsum(x, first):
    head0 = jnp.where(first, x, 0.0)
    s0 = jnp.sum(head0, axis=-1, keepdims=True)
    s1 = jnp.sum(x - head0, axis=-1, keepdims=True)
    return jnp.where(first, s0, s1)


def _head_rms(x, gain, first):
    rstd = lax.rsqrt(_segsum(x * x, first) * (1.0 / HEAD) + EPS)
    xhat = x * rstd
    return xhat * gain, xhat, rstd


def _expand(x, odd, lo):
    if odd:
        xl = pltpu.roll(jnp.where(lo, 0.0, x), HEAD, axis=1)
    else:
        xl = jnp.where(lo, x, 0.0)
    xh = pltpu.roll(xl, HEAD, axis=1)
    return jnp.concatenate([xl, xh], axis=0).astype(BF16)


def _attn_mask(n):
    qi = lax.broadcasted_iota(jnp.int32, (BLK, 2 * BLK), 0)
    sj = lax.broadcasted_iota(jnp.int32, (BLK, 2 * BLK), 1)
    rel = qi + BLK - sj
    return (rel >= 0) & (rel < BLK) & ((sj >= BLK) | (n > 0))


def _attn_specs(t, aw, kv):
    prev = lambda n: jnp.maximum(n - 1, 0)
    kb, vb = aw // kv, aw // kv + 1
    return [
        pl.BlockSpec(memory_space=pltpu.SMEM),
        pl.BlockSpec((BLK, aw), lambda n: (n, 0)),
        pl.BlockSpec((BLK, kv), lambda n: (prev(n), kb)),
        pl.BlockSpec((BLK, kv), lambda n: (n, kb)),
        pl.BlockSpec((BLK, kv), lambda n: (prev(n), vb)),
        pl.BlockSpec((BLK, kv), lambda n: (n, vb)),
        pl.BlockSpec((1, BLK), lambda n: (0, 0)),
        pl.BlockSpec((1, BLK), lambda n: (0, 0)),
    ]


def _softmax_pair(s2, valid, sink0, sink1):
    out, psink = [], []
    for half, sink in ((0, sink0), (1, sink1)):
        s = jnp.where(valid, s2[:, 2 * BLK * half:2 * BLK * (half + 1)], NEG_INF)
        m = jnp.maximum(jnp.max(s, axis=-1, keepdims=True), sink)
        p = jnp.exp(s - m)
        es = jnp.exp(sink - m)
        inv = 1.0 / (jnp.sum(p, axis=-1, keepdims=True) + es)
        out.append(p * inv)
        psink.append(es * inv)
    return jnp.concatenate(out, axis=1), psink


def _attn_fwd(proj, qg, kg, sinks, aw, kv):
    t = proj.shape[0]
    n_pairs, n_kvblk = aw // BLK, kv // BLK

    def body(sink_ref, q_ref, kp_ref, kc_ref, vp_ref, vc_ref, qg_ref, kg_ref, o_ref, probs_ref, sink_p_ref):
        n = pl.program_id(0)
        lo = lax.broadcasted_iota(jnp.int32, (1, BLK), 1) < HEAD
        valid = _attn_mask(n)
        kraw = jnp.concatenate([kp_ref[...], kc_ref[...]], axis=0)
        vraw = jnp.concatenate([vp_ref[...], vc_ref[...]], axis=0)
        qn = [_head_rms(q_ref[:, BLK * p:BLK * (p + 1)], qg_ref[...], lo)[0].astype(BF16) for p in range(n_pairs)]
        k2, v2 = [], []
        for b in range(n_kvblk):
            kn = _head_rms(kraw[:, BLK * b:BLK * (b + 1)], kg_ref[...], lo)[0]
            for odd in (0, 1):
                k2.append(_expand(kn, odd, lo))
                v2.append(_expand(vraw[:, BLK * b:BLK * (b + 1)], odd, lo))
        s2 = [_nt(qn[p], k2[p // 2]) * (HEAD ** -0.5) for p in range(n_pairs)]
        soft = [_softmax_pair(s2[p], valid, sink_ref[2 * p], sink_ref[2 * p + 1]) for p in range(n_pairs)]
        lane = lax.broadcasted_iota(jnp.int32, (1, BLK), 1)
        sink_p = jnp.zeros((BLK, BLK), F32)
        for p in range(n_pairs):
            probs_ref[:, 4 * BLK * p:4 * BLK * (p + 1)] = soft[p][0]
            for half in (0, 1):
                sink_p = jnp.where(lane == 2 * p + half, soft[p][1][half], sink_p)
            o_ref[:, BLK * p:BLK * (p + 1)] = _nn(soft[p][0].astype(BF16), v2[p // 2])
        sink_p_ref[...] = sink_p

    return pl.pallas_call(
        body, name="attn_fwd", grid=(t // BLK,),
        in_specs=_attn_specs(t, aw, kv),
        out_specs=[pl.BlockSpec((BLK, aw), lambda n: (n, 0)), pl.BlockSpec((BLK, 4 * aw), lambda n: (n, 0)),
                   pl.BlockSpec((BLK, BLK), lambda n: (n, 0))],
        out_shape=[SDS((t, aw), F32), SDS((t, 4 * aw), F32), SDS((t, BLK), F32)],
        compiler_params=_params("arbitrary"),
    )(sinks, proj, proj, proj, proj, proj, qg, kg)


def _attn_bwd(proj, dy, probs, sink_p, qg, kg, sinks, aw, kv, after=None):
    t = proj.shape[0]
    nb = t // BLK
    n_kvblk = kv // BLK
    kb = aw // kv

    def body(sink_ref, q_ref, kp_ref, kc_ref, vp_ref, vc_ref, qg_ref, kg_ref, dy_ref, kall_ref, probs_ref, sink_p_ref,
             dqkv_ref, dstat_ref, dk_acc, dv_acc, dqg_acc):
        del sink_ref
        n = pl.program_id(0)
        lane = lax.broadcasted_iota(jnp.int32, (1, BLK), 1)
        lo = lane < HEAD

        @pl.when(n == 0)
        def _():
            dk_acc[...] = jnp.zeros_like(dk_acc)
            dv_acc[...] = jnp.zeros_like(dv_acc)
            dqg_acc[...] = jnp.zeros_like(dqg_acc)
            dstat_ref[...] = jnp.zeros_like(dstat_ref)

        kraw = jnp.concatenate([kp_ref[...], kc_ref[...]], axis=0)
        vraw = jnp.concatenate([vp_ref[...], vc_ref[...]], axis=0)
        row = pl.multiple_of(n * BLK, BLK)
        prow = pl.multiple_of(jnp.maximum(n - 1, 0) * BLK, BLK)
        pairs = range(aw // BLK)
        cols = [slice(BLK * p, BLK * (p + 1)) for p in pairs]
        qs = [_head_rms(q_ref[:, cols[p]], qg_ref[...], lo) for p in pairs]
        qb = [qs[p][0].astype(BF16) for p in pairs]
        k2, v2 = [], []
        for b in range(n_kvblk):
            kn = _head_rms(kraw[:, BLK * b:BLK * (b + 1)], kg_ref[...], lo)[0]
            for odd in (0, 1):
                k2.append(_expand(kn, odd, lo))
                v2.append(_expand(vraw[:, BLK * b:BLK * (b + 1)], odd, lo))
        p2 = [probs_ref[:, 4 * BLK * p:4 * BLK * (p + 1)] for p in pairs]
        dob = [dy_ref[:, cols[p]].astype(BF16) for p in pairs]
        dp2 = [_nt(dob[p], v2[p // 2]) for p in pairs]
        deltas = jnp.zeros((BLK, BLK), F32)
        ds2 = []
        for p in pairs:
            ds = []
            for half in (0, 1):
                hs = slice(2 * BLK * half, 2 * BLK * (half + 1))
                ph = p2[p][:, hs]
                delta = jnp.sum(ph * dp2[p][:, hs], axis=-1, keepdims=True)
                ds.append(ph * (dp2[p][:, hs] - delta))
                deltas = jnp.where(lane == 2 * p + half, delta, deltas)
            ds2.append((jnp.concatenate(ds, axis=1) * (HEAD ** -0.5)).astype(BF16))
        dstat_ref[2:3, :] -= jnp.sum(sink_p_ref[...] * deltas, axis=0, keepdims=True)
        dqn = [_nn(ds2[p], k2[p // 2]) for p in pairs]
        dk2t = [_tn(qb[p], ds2[p]) for p in pairs]
        dv2t = [_tn(dob[p], p2[p].astype(BF16)) for p in pairs]
        for p in pairs:
            _, qhat, rstd = qs[p]
            dqhat = dqn[p] * qg_ref[...]
            proj_q = _segsum(dqhat * qhat, lo) * (1.0 / HEAD)
            dqkv_ref[pl.ds(row, BLK), cols[p]] = (rstd * (dqhat - qhat * proj_q)).astype(BF16)
            dqg_acc[:, cols[p]] += jnp.sum(dqn[p] * qhat, axis=0, keepdims=True)
        def head_sum(parts, j):
            return sum(a[:HEAD, :2 * BLK] + a[HEAD:, 2 * BLK:] for a in parts[2 * j:2 * j + 2])

        for b in range(n_kvblk):
            dkn = jnp.concatenate([head_sum(dk2t, 2 * b), head_sum(dk2t, 2 * b + 1)], axis=0).T
            dvb = jnp.concatenate([head_sum(dv2t, 2 * b), head_sum(dv2t, 2 * b + 1)], axis=0).T
            kcols = slice(BLK * b, BLK * (b + 1))
            dk_acc[pl.ds(prow, BLK), kcols] += dkn[:BLK]
            dv_acc[pl.ds(prow, BLK), kcols] += dvb[:BLK]
            dk_acc[pl.ds(row, BLK), kcols] += dkn[BLK:]
            dv_acc[pl.ds(row, BLK), kcols] += dvb[BLK:]

        @pl.when(n == nb - 1)
        def _():
            dqg = dqg_acc[:, 0:BLK]
            for p in range(1, aw // BLK):
                dqg = dqg + dqg_acc[:, BLK * p:BLK * (p + 1)]
            dstat_ref[0:1, :] = dqg + pltpu.roll(dqg, HEAD, axis=1)

            def kblock(i, dkg):
                r = pl.multiple_of(i * BLK, BLK)
                for b in range(n_kvblk):
                    kcols = slice(BLK * b, BLK * (b + 1))
                    _, khat, rstd = _head_rms(kall_ref[pl.ds(r, BLK), kcols], kg_ref[...], lo)
                    dkn = dk_acc[pl.ds(r, BLK), kcols]
                    dkhat = dkn * kg_ref[...]
                    proj_k = _segsum(dkhat * khat, lo) * (1.0 / HEAD)
                    dqkv_ref[pl.ds(r, BLK), aw + BLK * b:aw + BLK * (b + 1)] = (rstd * (dkhat - khat * proj_k)).astype(BF16)
                    dqkv_ref[pl.ds(r, BLK), aw + kv + BLK * b:aw + kv + BLK * (b + 1)] = dv_acc[pl.ds(r, BLK), kcols].astype(BF16)
                    dkg = dkg + jnp.sum(dkn * khat, axis=0, keepdims=True)
                return dkg

            dkg = lax.fori_loop(0, nb, kblock, jnp.zeros((1, BLK), F32))
            dstat_ref[1:2, :] = dkg + pltpu.roll(dkg, HEAD, axis=1)

    in_specs = _attn_specs(t, aw, kv) + [
        pl.BlockSpec((BLK, aw), lambda n: (n, 0)),
        pl.BlockSpec((t, kv), lambda n: (0, kb)),
        pl.BlockSpec((BLK, 4 * aw), lambda n: (n, 0)),
        pl.BlockSpec((BLK, BLK), lambda n: (n, 0)),
    ]
    args = [sinks, proj, proj, proj, proj, proj, qg, kg, dy, proj, probs, sink_p]
    body, in_specs, args = _ordered_behind(body, len(args), in_specs, args, after)
    return pl.pallas_call(
        body, name="attn_bwd", grid=(nb,),
        in_specs=in_specs,
        out_specs=[pl.BlockSpec((t, aw + 2 * kv), lambda n: (0, 0)), pl.BlockSpec((8, BLK), lambda n: (0, 0))],
        out_shape=[SDS((t, aw + 2 * kv), BF16), SDS((8, BLK), F32)],
        scratch_shapes=[pltpu.VMEM((t, kv), F32), pltpu.VMEM((t, kv), F32), pltpu.VMEM((1, aw), F32)],
        compiler_params=_params("arbitrary"),
    )(*args)


def _conv_taps(win, w_ref, shift_of):
    rows = win.shape[0]
    acc = None
    for j in range(CONV_K):
        term = pltpu.roll(win, (rows - shift_of(j)) % rows, axis=0)[:BLK] * w_ref[j:j + 1, :]
        acc = term if acc is None else acc + term
    return acc


def _layer_norm_fwd(z):
    mu = jnp.mean(z, axis=-1, keepdims=True)
    zc = z - mu
    rstd = lax.rsqrt(jnp.mean(zc * zc, axis=-1, keepdims=True) + EPS)
    return zc * rstd, rstd


def _layer_norm_bwd(dy, yhat, rstd, g):
    dyh = dy * g
    return rstd * (dyh - jnp.mean(dyh, axis=-1, keepdims=True) - yhat * jnp.mean(dyh * yhat, axis=-1, keepdims=True))


def _conv_fill_glu(a_ref, g_ref, hpad, nb):
    hpad[0:HALO, :] = jnp.zeros((HALO, hpad.shape[1]), F32)

    def fill(i, c):
        r = pl.multiple_of(i * BLK, BLK)
        hpad[pl.ds(pl.multiple_of(r + HALO, HALO), BLK), :] = a_ref[pl.ds(r, BLK), :] * _sigmoid(g_ref[pl.ds(r, BLK), :])
        return c

    lax.fori_loop(0, nb, fill, 0)


def _conv_specs(t, cw, d_in):
    base = (d_in - 4 * cw) // cw
    vec = pl.BlockSpec((1, cw), lambda i: (0, 0))
    return [
        pl.BlockSpec((t, cw), lambda i: (0, base)),
        pl.BlockSpec((t, cw), lambda i: (0, base + 1)),
        pl.BlockSpec((HALO, cw), lambda i: (0, 0)),
        vec, vec, vec,
    ]


def _conv_fwd(proj, cw_pad, cb, lg, lb, cw):
    t, d_in = proj.shape
    nb = t // BLK

    def body(a_ref, g_ref, w_ref, b_ref, lg_ref, lb_ref, o_ref, z_ref, hpad):
        _conv_fill_glu(a_ref, g_ref, hpad, nb)

        def blk(i, c):
            r = pl.multiple_of(i * BLK, BLK)
            z = _conv_taps(hpad[pl.ds(r, BLK + HALO), :], w_ref, lambda j: j + HALO - (CONV_K - 1)) + b_ref[...]
            z_ref[pl.ds(r, BLK), :] = z
            yhat, _ = _layer_norm_fwd(z)
            y = yhat * lg_ref[...] + lb_ref[...]
            o_ref[pl.ds(r, BLK), :] = y * _sigmoid(y)
            return c

        lax.fori_loop(0, nb, blk, 0)

    whole = pl.BlockSpec((t, cw), lambda i: (0, 0))
    return pl.pallas_call(
        body, name="conv_fwd", grid=(1,),
        in_specs=_conv_specs(t, cw, d_in),
        out_specs=[whole, whole],
        out_shape=[SDS((t, cw), F32), SDS((t, cw), F32)],
        scratch_shapes=[pltpu.VMEM((t + HALO, cw), F32)],
        compiler_params=_params("arbitrary"),
    )(proj, proj, cw_pad, cb, lg, lb)


def _conv_bwd(proj, dy, z, cw_pad, cb, lg, lb, cw):
    t, d_in = proj.shape
    nb = t // BLK

    def body(a_ref, g_ref, w_ref, b_ref, lg_ref, lb_ref, dy_ref, z_ref, dc_ref, dw_ref, dvec_ref, hpad, dzpad, dwacc):
        del b_ref
        _conv_fill_glu(a_ref, g_ref, hpad, nb)
        dzpad[t:t + HALO, :] = jnp.zeros((HALO, cw), F32)
        dwacc[...] = jnp.zeros_like(dwacc)

        def blk(i, carry):
            db, dlg, dlb = carry
            r = pl.multiple_of(i * BLK, BLK)
            win = hpad[pl.ds(r, BLK + HALO), :]
            yhat, rstd = _layer_norm_fwd(z_ref[pl.ds(r, BLK), :])
            y = yhat * lg_ref[...] + lb_ref[...]
            sg = _sigmoid(y)
            dyl = dy_ref[pl.ds(r, BLK), :] * (sg * (1.0 + y * (1.0 - sg)))
            dz = _layer_norm_bwd(dyl, yhat, rstd, lg_ref[...])
            dzpad[pl.ds(r, BLK), :] = dz
            for j in range(CONV_K):
                sh = j + HALO - (CONV_K - 1)
                prod = dz * pltpu.roll(win, (BLK + HALO - sh) % (BLK + HALO), axis=0)[:BLK]
                dwacc[8 * j:8 * j + 8, :] += jnp.sum(prod.reshape(BLK // 8, 8, cw), axis=0)
            return (db + jnp.sum(dz, axis=0, keepdims=True),
                    dlg + jnp.sum(dyl * yhat, axis=0, keepdims=True),
                    dlb + jnp.sum(dyl, axis=0, keepdims=True))

        zero = jnp.zeros((1, cw), F32)
        db, dlg, dlb = lax.fori_loop(0, nb, blk, (zero, zero, zero))
        dvec_ref[...] = jnp.zeros_like(dvec_ref)
        dvec_ref[0:1, :] = db
        dvec_ref[1:2, :] = dlg
        dvec_ref[2:3, :] = dlb
        dw_ref[...] = jnp.sum(dwacc[...].reshape(HALO, 8, cw), axis=1)

        def blk2(i, c):
            r = pl.multiple_of(i * BLK, BLK)
            dh = _conv_taps(dzpad[pl.ds(r, BLK + HALO), :], w_ref, lambda j: CONV_K - 1 - j)
            a = a_ref[pl.ds(r, BLK), :]
            sg = _sigmoid(g_ref[pl.ds(r, BLK), :])
            dc_ref[pl.ds(r, BLK), 0:cw] = (dh * sg).astype(BF16)
            dc_ref[pl.ds(r, BLK), cw:2 * cw] = (dh * a * sg * (1.0 - sg)).astype(BF16)
            return c

        lax.fori_loop(0, nb, blk2, 0)

    return pl.pallas_call(
        body, name="conv_bwd", grid=(1,),
        in_specs=_conv_specs(t, cw, d_in) + [pl.BlockSpec((t, cw), lambda i: (0, 0))] * 2,
        out_specs=[pl.BlockSpec((t, 2 * cw), lambda i: (0, 0)), pl.BlockSpec((HALO, cw), lambda i: (0, 0)),
                   pl.BlockSpec((8, cw), lambda i: (0, 0))],
        out_shape=[SDS((t, 2 * cw), BF16), SDS((HALO, cw), F32), SDS((8, cw), F32)],
        scratch_shapes=[pltpu.VMEM((t + HALO, cw), F32), pltpu.VMEM((t + HALO, cw), F32), pltpu.VMEM((8 * HALO, cw), F32)],
        compiler_params=_params("arbitrary"),
    )(proj, proj, cw_pad, cb, lg, lb, dy, z)


def _tril_bf16(w):
    r = lax.broadcasted_iota(jnp.int32, (BLK, BLK), 0)
    c = lax.broadcasted_iota(jnp.int32, (BLK, BLK), 1)
    return jnp.where(r >= c, w, 0.0).astype(BF16)


def _sgu_specs(sw, d_in, heads):
    base = (d_in - 2 * sw) // sw
    vec = pl.BlockSpec((1, sw), lambda n: (0, 0))
    return [
        pl.BlockSpec((BLK, sw), lambda n: (n, base)),
        pl.BlockSpec((BLK, sw), lambda n: (n, base + 1)),
        vec, vec,
        pl.BlockSpec((heads, BLK, BLK), lambda n: (0, 0, 0)),
        pl.BlockSpec((BLK, sw), lambda n: (0, 0)),
    ]


def _sgu_mix(w_ref, vnb, heads, sw, transpose):
    head_of = lax.broadcasted_iota(jnp.int32, (1, sw), 1) // HEAD
    s = jnp.zeros((BLK, sw), F32)
    for h in range(heads):
        wt = _tril_bf16(w_ref[h])
        mixed = _tn(wt, vnb) if transpose else _nn(wt, vnb)
        s = jnp.where(head_of == h, mixed, s)
    return s


def _sgu_fwd(proj, lg, lb, w, bias_full, sw):
    t, d_in = proj.shape
    heads = sw // HEAD

    def body(u_ref, v_ref, lg_ref, lb_ref, w_ref, bias_ref, o_ref, s_ref):
        vhat, _ = _layer_norm_fwd(v_ref[...])
        vn = (vhat * lg_ref[...] + lb_ref[...]).astype(BF16)
        s = _sgu_mix(w_ref, vn, heads, sw, False) + bias_ref[...]
        s_ref[...] = s
        o_ref[...] = u_ref[...] * s

    tile = pl.BlockSpec((BLK, sw), lambda n: (n, 0))
    return pl.pallas_call(
        body, name="sgu_fwd", grid=(t // BLK,),
        in_specs=_sgu_specs(sw, d_in, heads),
        out_specs=[tile, tile],
        out_shape=[SDS((t, sw), F32), SDS((t, sw), F32)],
        compiler_params=_params("arbitrary"),
    )(proj, proj, lg, lb, w, bias_full)


def _sgu_bwd(proj, dy, gate, lg, lb, w, bias_full, sw, after=None):
    t, d_in = proj.shape
    heads = sw // HEAD
    nb = t // BLK

    def body(u_ref, v_ref, lg_ref, lb_ref, w_ref, bias_ref, dy_ref, s_ref, ds_ref, dw_ref, db_ref, dvec_ref, dbfull):
        del bias_ref
        n = pl.program_id(0)

        @pl.when(n == 0)
        def _():
            dw_ref[...] = jnp.zeros_like(dw_ref)
            dvec_ref[...] = jnp.zeros_like(dvec_ref)
            dbfull[...] = jnp.zeros_like(dbfull)

        vhat, rstd = _layer_norm_fwd(v_ref[...])
        vn = (vhat * lg_ref[...] + lb_ref[...]).astype(BF16)
        dy = dy_ref[...]
        ds_ref[:, 0:sw] = (dy * s_ref[...]).astype(BF16)
        dsv = dy * u_ref[...]
        dbfull[...] += dsv
        head_of = lax.broadcasted_iota(jnp.int32, (1, sw), 1) // HEAD
        r = lax.broadcasted_iota(jnp.int32, (BLK, BLK), 0)
        c = lax.broadcasted_iota(jnp.int32, (BLK, BLK), 1)
        dsb = dsv.astype(BF16)
        for h in range(heads):
            dsh = jnp.where(head_of == h, dsv, 0.0).astype(BF16)
            dw_ref[h] += jnp.where(r >= c, _nt(dsh, vn), 0.0)
        dvn = _sgu_mix(w_ref, dsb, heads, sw, True)
        dvec_ref[0:1, :] += jnp.sum(dvn * vhat, axis=0, keepdims=True)
        dvec_ref[1:2, :] += jnp.sum(dvn, axis=0, keepdims=True)
        ds_ref[:, sw:2 * sw] = _layer_norm_bwd(dvn, vhat, rstd, lg_ref[...]).astype(BF16)

        @pl.when(n == nb - 1)
        def _():
            sel = (lax.broadcasted_iota(jnp.int32, (sw, BLK), 0) // HEAD == lax.broadcasted_iota(jnp.int32, (sw, BLK), 1)).astype(BF16)
            x = dbfull[...]
            hi = x.astype(BF16)
            r1 = x - hi.astype(F32)
            mid = r1.astype(BF16)
            low = (r1 - mid.astype(F32)).astype(BF16)
            db_ref[...] = _nn(hi, sel) + _nn(mid, sel) + _nn(low, sel)

    in_specs = _sgu_specs(sw, d_in, heads) + [pl.BlockSpec((BLK, sw), lambda n: (n, 0))] * 2
    body, in_specs, args = _ordered_behind(body, 8, in_specs, [proj, proj, lg, lb, w, bias_full, dy, gate], after)
    return pl.pallas_call(
        body, name="sgu_bwd", grid=(nb,),
        in_specs=in_specs,
        out_specs=[pl.BlockSpec((BLK, 2 * sw), lambda n: (n, 0)), pl.BlockSpec((heads, BLK, BLK), lambda n: (0, 0, 0)),
                   pl.BlockSpec((BLK, BLK), lambda n: (0, 0)), pl.BlockSpec((8, sw), lambda n: (0, 0))],
        out_shape=[SDS((t, 2 * sw), BF16), SDS((heads, BLK, BLK), F32), SDS((BLK, BLK), F32), SDS((8, sw), F32)],
        scratch_shapes=[pltpu.VMEM((BLK, sw), F32)],
        compiler_params=_params("arbitrary"),
    )(*args)


def _rms_fwd(x, g):
    return (x * lax.rsqrt(jnp.mean(x * x, axis=-1, keepdims=True) + EPS)) * g


def _rms_bwd(dh, x, g):
    rstd = lax.rsqrt(jnp.mean(x * x, axis=-1, keepdims=True) + EPS)
    xhat = x * rstd
    dgx = dh * g
    dx = rstd * (dgx - xhat * jnp.mean(dgx * xhat, axis=-1, keepdims=True))
    return dx, jnp.sum(dh * xhat, axis=0, keepdims=True)


def _rms_matmul(x, g, w, tm, tn, relu2, name, transposed=False):
    t, d = x.shape
    if w.ndim == 3:
        assert w.shape[2] == tn
        n = w.shape[0] * tn
        w_spec = pl.BlockSpec((None, d, tn), lambda i, j: (j, 0, 0))
    elif transposed:
        n = w.shape[0]
        w_spec = pl.BlockSpec((tn, d), lambda i, j: (j, 0))
    else:
        n = w.shape[1]
        w_spec = pl.BlockSpec((d, tn), lambda i, j: (0, j))

    def body(x_ref, g_ref, w_ref, h_ref, *outs):
        @pl.when(pl.program_id(1) == 0)
        def _():
            h_ref[...] = _rms_fwd(x_ref[...], g_ref[...]).astype(BF16)

        acc = _nt(h_ref[...], w_ref[...]) if transposed else _nn(h_ref[...], w_ref[...])
        if relu2:
            r = jnp.maximum(acc, 0.0)
            outs[0][...] = (r * r).astype(BF16)
            outs[1][...] = r.astype(BF16)
        else:
            outs[0][...] = acc

    tile = pl.BlockSpec((tm, tn), lambda i, j: (i, j))
    row = pl.BlockSpec((tm, d), lambda i, j: (i, 0))
    outs = [SDS((t, n), BF16), SDS((t, n), BF16)] if relu2 else [SDS((t, n), F32)]
    return pl.pallas_call(
        body, name=name, grid=(t // tm, n // tn),
        in_specs=[row, pl.BlockSpec((1, d), lambda i, j: (0, 0)), w_spec],
        out_specs=[row] + [tile] * len(outs),
        out_shape=[SDS((t, d), BF16)] + outs,
        compiler_params=_params("parallel", "arbitrary"),
    )(x, g, w)


def _group_rms_matmul(ys, g, w, res, tm, tn):
    t, d = res.shape
    widths = [y.shape[1] for y in ys]
    k = sum(widths)

    def body(*refs):
        y_refs, (g_ref, w_ref, res_ref, mix_ref, o_ref) = refs[:len(ys)], refs[len(ys):]

        @pl.when(pl.program_id(1) == 0)
        def _():
            c = 0
            for y_ref, wd in zip(y_refs, widths):
                mix_ref[:, c:c + wd] = _rms_fwd(y_ref[...], g_ref[:, c:c + wd]).astype(BF16)
                c += wd

        o_ref[...] = res_ref[...] + _nn(mix_ref[...], w_ref[...])

    tile = pl.BlockSpec((tm, tn), lambda i, j: (i, j))
    return pl.pallas_call(
        body, name="mix_out", grid=(t // tm, d // tn),
        in_specs=[pl.BlockSpec((tm, wd), lambda i, j: (i, 0)) for wd in widths] + [
            pl.BlockSpec((1, k), lambda i, j: (0, 0)), pl.BlockSpec((k, tn), lambda i, j: (0, j)), tile],
        out_specs=[pl.BlockSpec((tm, k), lambda i, j: (i, 0)), tile],
        out_shape=[SDS((t, k), BF16), SDS((t, d), F32)],
        compiler_params=_params("parallel", "arbitrary"),
    )(*ys, g, w, res)


def _matmul_res(a, w, res, tm, tn, tk):
    t, k = a.shape
    n = w.shape[1]
    nk = k // tk

    def body(a_ref, w_ref, res_ref, o_ref, acc):
        kk = pl.program_id(2)

        @pl.when(kk == 0)
        def _():
            acc[...] = res_ref[...]

        acc[...] += _nn(a_ref[...], w_ref[...])

        @pl.when(kk == nk - 1)
        def _():
            o_ref[...] = acc[...]

    tile = pl.BlockSpec((tm, tn), lambda i, j, kk: (i, j))
    return pl.pallas_call(
        body, name="mlp_down", grid=(t // tm, n // tn, nk),
        in_specs=[pl.BlockSpec((tm, tk), lambda i, j, kk: (i, kk)), pl.BlockSpec((tk, tn), lambda i, j, kk: (kk, j)), tile],
        out_specs=tile,
        out_shape=SDS((t, n), F32),
        scratch_shapes=[pltpu.VMEM((tm, tn), F32)],
        compiler_params=_params("parallel", "parallel", "arbitrary"),
    )(a, w, res)


def _loss_grad(y, target, tm):
    t, d = y.shape

    def body(y_ref, t_ref, dy_ref, dyb_ref, l_ref):
        @pl.when(pl.program_id(0) == 0)
        def _():
            l_ref[...] = jnp.zeros_like(l_ref)

        err = y_ref[...] - t_ref[...]
        dy = err * (1.0 / d)
        dy_ref[...] = dy
        dyb_ref[...] = dy.astype(BF16)
        per_row = jnp.mean(err * err, axis=-1, keepdims=True)
        l_ref[...] += jnp.sum(per_row, axis=0, keepdims=True) * 0.5

    row = pl.BlockSpec((tm, d), lambda i: (i, 0))
    return pl.pallas_call(
        body, name="loss_grad", grid=(t // tm,),
        in_specs=[row, row], out_specs=[row, row, pl.BlockSpec((8, BLK), lambda i: (0, 0))],
        out_shape=[SDS((t, d), F32), SDS((t, d), BF16), SDS((8, BLK), F32)],
        compiler_params=_params("arbitrary"),
    )(y, target)


def _mlp_dact(dxb, w_down, r, tn, after=None):
    t, d = dxb.shape
    f = w_down.shape[0]

    def body(dxb_ref, w_ref, r_ref, dup_ref):
        dup_ref[...] = (_nt(dxb_ref[...], w_ref[...]) * (2.0 * r_ref[...].astype(F32))).astype(BF16)

    tile = pl.BlockSpec((t, tn), lambda j: (0, j))
    body, in_specs, args = _ordered_behind(
        body, 3, [pl.BlockSpec((t, d), lambda j: (0, 0)), pl.BlockSpec((tn, d), lambda j: (j, 0)), tile],
        [dxb, w_down, r], after)
    return pl.pallas_call(
        body, name="mlp_dact", grid=(f // tn,),
        in_specs=in_specs, out_specs=tile,
        out_shape=SDS((t, f), BF16),
        compiler_params=_params("arbitrary"),
    )(*args)


def _grad_w_stacked(pieces, b, tm, name, after=None):
    t, n = b.shape
    tiles = [p.shape[1] // tm for p in pieces]
    first = [sum(tiles[:k]) for k in range(len(pieces))]

    def body(*refs):
        b_ref, o_ref = refs[len(pieces)], refs[-1]
        i = pl.program_id(0)
        for k, p_ref in enumerate(refs[:len(pieces)]):
            @pl.when((i >= first[k]) & (i < first[k] + tiles[k]))
            def _(p_ref=p_ref):
                o_ref[...] = _tn(p_ref[...], b_ref[...]).astype(BF16)

    in_specs = [pl.BlockSpec((t, tm), lambda i, k=k: (0, jnp.clip(i - first[k], 0, tiles[k] - 1))) for k in range(len(pieces))]
    in_specs.append(pl.BlockSpec((t, n), lambda i: (0, 0)))
    body, in_specs, args = _ordered_behind(body, len(pieces) + 1, in_specs, [*pieces, b], after)
    return pl.pallas_call(
        body, name=name, grid=(sum(tiles),),
        in_specs=in_specs, out_specs=pl.BlockSpec((tm, n), lambda i: (i, 0)),
        out_shape=SDS((sum(tiles) * tm, n), BF16),
        compiler_params=_params("arbitrary"),
    )(*args)


def _grad_w_half(a, b, tm, tn, name, by_cols, sel, add=None, after=None):
    t, m = a.shape
    n = b.shape[1]
    if by_cols:
        per = n // N_DEV // tn
        grid, shape = (4, m // tm, per), (4, m, n // N_DEV)
        a_spec = pl.BlockSpec((t, tm), lambda q, i, j, s: (0, i))
        b_spec = pl.BlockSpec((t, tn), lambda q, i, j, s: (0, (2 * q + s[0]) * per + j))
    else:
        per = m // N_DEV // tm
        grid, shape = (4, per, n // tn), (4, m // N_DEV, n)
        a_spec = pl.BlockSpec((t, tm), lambda q, i, j, s: (0, (2 * q + s[0]) * per + i))
        b_spec = pl.BlockSpec((t, tn), lambda q, i, j, s: (0, j))
    assert per >= 1
    tile = pl.BlockSpec((None, tm, tn), lambda q, i, j, s: (q, i, j))

    def body(sel_ref, a_ref, b_ref, *rest):
        acc = _tn(a_ref[...], b_ref[...])
        if add is not None:
            acc = acc + rest[0][...].astype(F32)
        rest[-1][...] = acc.astype(BF16)

    in_specs, args = [a_spec, b_spec], [a, b]
    if add is not None:
        in_specs, args = in_specs + [tile], args + [add]
    if after is not None:
        in_specs, args = in_specs + [ANY], args + [after]
    return pl.pallas_call(
        body, name=name,
        grid_spec=pltpu.PrefetchScalarGridSpec(num_scalar_prefetch=1, grid=grid, in_specs=in_specs, out_specs=tile),
        out_shape=SDS(shape, BF16),
        compiler_params=_params("arbitrary", "arbitrary", "arbitrary"),
    )(sel, *args)


def _mlp_dnorm(dup, w_up, x, g, dres, tm, after=None):
    t, f = dup.shape
    d = x.shape[1]
    nk, _, tk = w_up.shape

    def body(a_ref, w_ref, x_ref, g_ref, dres_ref, dx_ref, dg_ref, acc):
        i, kk = pl.program_id(0), pl.program_id(1)

        @pl.when((i == 0) & (kk == 0))
        def _():
            dg_ref[...] = jnp.zeros_like(dg_ref)

        @pl.when(kk == 0)
        def _():
            acc[...] = jnp.zeros_like(acc)

        acc[...] += _nt(a_ref[...], w_ref[...])

        @pl.when(kk == nk - 1)
        def _():
            dx, dg = _rms_bwd(acc[...], x_ref[...], g_ref[...])
            dx_ref[...] = dres_ref[...] + dx
            dg_ref[0:1, :] += dg

    row = pl.BlockSpec((tm, d), lambda i, kk: (i, 0))
    in_specs = [pl.BlockSpec((tm, tk), lambda i, kk: (i, kk)), pl.BlockSpec((None, d, tk), lambda i, kk: (kk, 0, 0)),
                row, pl.BlockSpec((1, d), lambda i, kk: (0, 0)), row]
    body, in_specs, args = _ordered_behind(body, 5, in_specs, [dup, w_up, x, g, dres], after)
    return pl.pallas_call(
        body, name="mlp_dnorm", grid=(t // tm, nk),
        in_specs=in_specs,
        out_specs=[row, pl.BlockSpec((8, d), lambda i, kk: (0, 0))],
        out_shape=[SDS((t, d), F32), SDS((8, d), F32)],
        scratch_shapes=[pltpu.VMEM((tm, d), F32)],
        compiler_params=_params("arbitrary", "arbitrary"),
    )(*args)


def _mix_dnorm(dx, w_out, ys, g, tm, after=None):
    t, d = dx.shape
    k = w_out.shape[0]
    widths = [y.shape[1] for y in ys]

    def body(dx_ref, w_ref, *refs):
        y_refs = refs[:len(ys)]
        g_ref, dxb_ref = refs[len(ys)], refs[len(ys) + 1]
        dy_refs = refs[len(ys) + 2:2 * len(ys) + 2]
        dg_ref = refs[-1]

        @pl.when(pl.program_id(0) == 0)
        def _():
            dg_ref[...] = jnp.zeros_like(dg_ref)

        dxb = dx_ref[...].astype(BF16)
        dxb_ref[...] = dxb
        dmix = _nt(dxb, w_ref[...])
        c = 0
        for y_ref, dy_ref, wd in zip(y_refs, dy_refs, widths):
            dy, dg = _rms_bwd(dmix[:, c:c + wd], y_ref[...], g_ref[:, c:c + wd])
            dy_ref[...] = dy
            dg_ref[0:1, c:c + wd] += dg
            c += wd

    row = pl.BlockSpec((tm, d), lambda i: (i, 0))
    yspecs = [pl.BlockSpec((tm, wd), lambda i: (i, 0)) for wd in widths]
    in_specs = [row, pl.BlockSpec((k, d), lambda i: (0, 0))] + yspecs + [pl.BlockSpec((1, k), lambda i: (0, 0))]
    body, in_specs, args = _ordered_behind(body, len(in_specs), in_specs, [dx, w_out, *ys, g], after)
    return pl.pallas_call(
        body, name="mix_dnorm", grid=(t // tm,),
        in_specs=in_specs,
        out_specs=[row] + yspecs + [pl.BlockSpec((8, k), lambda i: (0, 0))],
        out_shape=[SDS((t, d), BF16)] + [SDS((t, wd), F32) for wd in widths] + [SDS((8, k), F32)],
        compiler_params=_params("arbitrary"),
    )(*args)


def _in_dnorm(dps, w_in_t, x, g, dres, tm):
    t, d = x.shape
    widths = [p.shape[1] for p in dps]
    offs = [sum(widths[:p]) for p in range(len(dps))]
    n_in = w_in_t.shape[0]

    def body(*refs):
        p_refs = refs[:len(dps)]
        w_ref, x_ref, g_ref, dres_ref, dx_ref, dxb_ref, dg_ref = refs[len(dps):]

        @pl.when(pl.program_id(0) == 0)
        def _():
            dg_ref[...] = jnp.zeros_like(dg_ref)

        acc = None
        for p_ref, off, wd in zip(p_refs, offs, widths):
            term = _nn(p_ref[...], w_ref[off:off + wd, :])
            acc = term if acc is None else acc + term
        dx, dg = _rms_bwd(acc, x_ref[...], g_ref[...])
        dx = dres_ref[...] + dx
        dx_ref[...] = dx
        dxb_ref[...] = dx.astype(BF16)
        dg_ref[0:1, :] += dg

    row = pl.BlockSpec((tm, d), lambda i: (i, 0))
    return pl.pallas_call(
        body, name="in_dnorm", grid=(t // tm,),
        in_specs=[pl.BlockSpec((tm, wd), lambda i: (i, 0)) for wd in widths] + [
            pl.BlockSpec((n_in, d), lambda i: (0, 0)), row, pl.BlockSpec((1, d), lambda i: (0, 0)), row],
        out_specs=[row, row, pl.BlockSpec((8, d), lambda i: (0, 0))],
        out_shape=[SDS((t, d), F32), SDS((t, d), BF16), SDS((8, d), F32)],
        compiler_params=_params("arbitrary"),
    )(*dps, w_in_t, x, g, dres)


def _tile(n, want):
    return min(n, want)


def _row_tile(n, want):
    return max(k for k in range(8, min(n, want) + 1, 8) if n % k == 0)


def _layer_fwd(x, p, fetch, prepare, after):
    t, d = x.shape
    aw, kv, cw, sw = d // 2, d // 8, d // 4, d // 4
    tm = _tile(t, ROWS_WIDE)
    w_in = fetch("w_in", after)
    h1, proj = _rms_matmul(x, p["ln1_g"], w_in, tm, COLS if w_in.shape[0] % COLS == 0 else COLS // 2, False, "in_proj",
                           transposed=True)
    y_attn, probs, sink_p = _attn_fwd(proj, p["qg"], p["kg"], p["sinks"], aw, kv)
    y_conv, z_conv = _conv_fwd(proj, p["conv_w"], p["conv_b"], p["conv_ln_g"], p["conv_ln_b"], cw)
    y_sgu, s_sgu = _sgu_fwd(proj, p["sgu_ln_g"], p["sgu_ln_b"], p["sgu_w"], p["sgu_bias"], sw)
    ys = [y_attn, y_conv, y_sgu]
    w_out = fetch("w_out", prepare("w_up", y_sgu))
    mix, x1 = _group_rms_matmul(ys, p["out_norm_g"], w_out, x, tm, _tile(d, COLS_WIDE // 2))
    w_up = fetch("w_up", prepare("w_down", x1))
    h2, act, r = _rms_matmul(x1, p["ln2_g"], w_up, tm, w_up.shape[2], True, "mlp_up")
    w_down = fetch("w_down", act)
    x2 = _matmul_res(act, w_down, x1, tm, _tile(d, COLS_WIDE), COLS)
    saved = dict(x=x, h1=h1, proj=proj, ys=ys, mix=mix, x1=x1, h2=h2, act=act, r=r,
                 w_in=w_in, w_out=w_out, w_up=w_up, w_down=w_down, probs=probs, sink_p=sink_p, z_conv=z_conv, s_sgu=s_sgu)
    return x2, saved


def _layer_bwd(dx2, dx2b, p, s, start, finish, share, carry):
    t, d = dx2.shape
    aw, kv, cw, sw = d // 2, d // 8, d // 4, d // 4
    tm = _tile(t, ROWS_F32)
    dup = _mlp_dact(dx2b, s["w_down"], s["r"], COLS, carry[0] if carry else None)
    tok = carry[1](dup) if carry else None
    half = functools.partial
    f = s["act"].shape[1]
    tok = start("w_down", half(_grad_w_half, s["act"], dx2b, _tile(f // N_DEV, COLS), _tile(d, COLS_WIDE), "grad_w_down", False), tok)
    dx1, d_ln2 = _mlp_dnorm(dup, s["w_up"], s["x1"], p["ln2_g"], dx2, tm, tok)
    tok = finish("w_down", dx1)
    tok = start("w_up", half(_grad_w_half, s["h2"], dup, _tile(d, ROWS_WIDE), s["w_up"].shape[2], "grad_w_up", True), tok)
    dx1b, dya, dyc, dys, d_onorm = _mix_dnorm(dx1, s["w_out"], s["ys"], p["out_norm_g"], _tile(t, ROWS_NORM), tok)
    tok = finish("w_up", dx1b)
    dsgu, d_sw, d_sb, d_svec = _sgu_bwd(s["proj"], dys, s["s_sgu"], p["sgu_ln_g"], p["sgu_ln_b"], p["sgu_w"], p["sgu_bias"], sw, tok)
    share("sgu_w", d_sw)
    tok = start("w_out", half(_grad_w_half, s["mix"], dx1b, _tile(d // N_DEV, COLS), _tile(d, COLS_WIDE), "grad_w_out", False), dsgu)
    dqkv, d_attn = _attn_bwd(s["proj"], dya, s["probs"], s["sink_p"], p["qg"], p["kg"], p["sinks"], aw, kv, tok)
    tok = finish("w_out", dqkv)
    dconv, d_cw, d_cvec = _conv_bwd(s["proj"], dyc, s["z_conv"], p["conv_w"], p["conv_b"], p["conv_ln_g"], p["conv_ln_b"], cw)
    dps = [dqkv, dconv, dsgu]
    dx, dxb, d_ln1 = _in_dnorm(dps, s["w_in"], s["x"], p["ln1_g"], dx1, _tile(t, ROWS_NORM))
    heads = sw // HEAD
    share("rest", dict(
        ln1_g=d_ln1[0], q_norm_g=d_attn[0, :HEAD], k_norm_g=d_attn[1, :HEAD], sinks=d_attn[2, :aw // HEAD],
        conv_w=d_cw[:CONV_K], conv_b=d_cvec[0], conv_ln_g=d_cvec[1], conv_ln_b=d_cvec[2],
        sgu_ln_g=d_svec[0], sgu_ln_b=d_svec[1], sgu_b=d_sb[:, :heads].T,
        out_norm_g=d_onorm[0], ln2_g=d_ln2[0]))
    tm_in = COLS if all(dp.shape[1] % COLS == 0 for dp in dps) else COLS // 2
    tok = start("w_in", _grad_w_stacked(dps, s["h1"], tm_in, "grad_w_in", tok), None)
    carry = (tok, functools.partial(finish, "w_in"))
    return dx, dxb, carry


def _layer_params(l, small):
    row = lambda v: v[l][None, :]
    two = lambda v: jnp.tile(v[l], 2)[None, :]
    return dict(
        ln1_g=row(small["ln1_g"]), ln2_g=row(small["ln2_g"]), out_norm_g=row(small["out_norm_g"]),
        qg=two(small["q_norm_g"]), kg=two(small["k_norm_g"]), sinks=small["sinks"][l],
        conv_w=jnp.pad(small["conv_w"][l], ((0, HALO - CONV_K), (0, 0))),
        conv_b=row(small["conv_b"]), conv_ln_g=row(small["conv_ln_g"]), conv_ln_b=row(small["conv_ln_b"]),
        sgu_ln_g=row(small["sgu_ln_g"]), sgu_ln_b=row(small["sgu_ln_b"]), sgu_w=small["sgu_w"][l],
        sgu_bias=jnp.repeat(small["sgu_b"][l].T, HEAD, axis=1),
    )


def _local_step(x, target, small, depth, fetch, start, finish, share, after, prepare=lambda l, n, after: after):
    params = [_layer_params(l, small) for l in range(depth)]
    saved = []
    h = x
    for l in range(depth):
        h, s = _layer_fwd(h, params[l], functools.partial(fetch, l), functools.partial(prepare, l), after)
        after = h
        saved.append(s)
    dy, dyb, lsum = _loss_grad(h, target, _tile(x.shape[0], ROWS_F32))
    def share_with_loss(kind, grads):
        share(depth - 1, kind, dict(grads, loss=lsum[0, 0:1]) if kind == "rest" else grads)

    carry = None
    for l in reversed(range(depth)):
        dy, dyb, carry = _layer_bwd(dy, dyb, params[l], saved[l], functools.partial(start, l), functools.partial(finish, l),
                                    share_with_loss if l == depth - 1 else functools.partial(share, l), carry)
    return dy, carry


BIG = ("w_in", "w_out", "w_up", "w_down")
HBM = pl.BlockSpec(memory_space=pltpu.HBM)
SEMS = pl.BlockSpec(memory_space=pltpu.SEMAPHORE)
EFFECT = pltpu.SideEffectType.DATAFLOW_SIDE_EFFECTING


def _mesh_pos():
    return lax.axis_index("x"), lax.axis_index("y"), lax.axis_index("c")


def _other_chips(x, y):
    return [(1 - x, y), (x, 1 - y), (1 - x, 1 - y)]


def _copies(plan, refs, sends, recvs):
    x, y, c = _mesh_pos()
    return [pltpu.make_async_remote_copy(src_ref=src, dst_ref=dst, send_sem=sends.at[k], recv_sem=recvs.at[k],
                                         device_id=dev, device_id_type=MESH)
            for k, (src, dst, dev) in enumerate(plan(x, y, c, refs))]


def _gather_plan(x, y, c, refs):
    mine = refs[0].at[4 * x + 2 * y + c]
    return [(mine, mine, (x, y, 1 - c))] + [(mine, mine, (*chip, c)) for chip in _other_chips(x, y)]


def _near_plan(x, y, c, refs):
    mine = refs[0].at[4 * x + 2 * y + c]
    return [(mine, mine, (x, y, 1 - c)), (mine, mine, (1 - x, y, c)), (mine, mine, (x, 1 - y, c))]


def _relay_plan(x, y, c, refs):
    land = refs[0]
    half = land.shape[1] // 2
    return [(land.at[4 * (1 - x) + 2 * y + c, pl.ds(0, half)], land.at[4 * (1 - x) + 2 * y + c, pl.ds(0, half)], (x, 1 - y, c)),
            (land.at[4 * x + 2 * (1 - y) + c, pl.ds(half, half)], land.at[4 * x + 2 * (1 - y) + c, pl.ds(half, half)],
             (1 - x, y, c))]


def _all_plan(x, y, c, refs):
    mine = refs[0].at[4 * x + 2 * y + c]
    return [(mine, mine, (x ^ (k >> 2), y ^ ((k >> 1) & 1), c ^ (k & 1))) for k in range(1, N_DEV)]


def _pair_plan(x, y, c, refs):
    blocks, land = refs
    return [(blocks.at[2 * q + (1 - c)], land.at[q], (x, y, 1 - c)) for q in range(4)]


def _half_plan(x, y, c, refs):
    blocks, land = refs
    return [(blocks.at[q], land.at[q], (x, y, 1 - c)) for q in range(4)]


def _chips_plan(x, y, c, refs):
    sums, land = refs
    return [(sums.at[2 * chip[0] + chip[1]], land.at[k], (*chip, c)) for k, chip in enumerate(_other_chips(x, y))]


def _start_exchanges(name, groups, after=None):
    flat = [a for arrays, _, _ in groups for a in arrays]
    n_arr, n_g = len(flat), len(groups)
    n_in = n_arr + (after is not None)

    def body(*refs):
        ins, sems, token = refs[:n_arr], refs[n_in:n_in + 2 * n_g], refs[-1]
        off = 0
        for gi, (arrays, plan, _) in enumerate(groups):
            for cp in _copies(plan, ins[off:off + len(arrays)], sems[2 * gi], sems[2 * gi + 1]):
                cp.start()
            off += len(arrays)
        token[...] = jnp.zeros_like(token)

    res = pl.pallas_call(
        body, name=name,
        out_shape=[pltpu.SemaphoreType.DMA((n,)) for _, _, n in groups for _ in (0, 1)]
        + [pltpu.HBM(a.shape, a.dtype) for a in flat] + [SDS((8, BLK), F32)],
        in_specs=[HBM] * n_arr + [ANY] * (after is not None),
        out_specs=[SEMS] * (2 * n_g) + [HBM] * n_arr + [pl.BlockSpec(memory_space=pltpu.VMEM)],
        input_output_aliases={i: 2 * n_g + i for i in range(n_arr)},
        compiler_params=pltpu.CompilerParams(has_side_effects=EFFECT),
    )(*[pltpu.with_memory_space_constraint(a, pltpu.HBM) for a in flat], *([after] if after is not None else []))
    sems, thru, token = res[:2 * n_g], res[2 * n_g:2 * n_g + n_arr], res[-1]
    out, off = [], 0
    for gi, (arrays, _, _) in enumerate(groups):
        out.append((list(thru[off:off + len(arrays)]), sems[2 * gi], sems[2 * gi + 1]))
        off += len(arrays)
    return out, token


def _wait_exchange(name, arrays, sends, recvs, plan, after):
    n = len(arrays)

    def body(*refs):
        for cp in _copies(plan, refs[:n], refs[n], refs[n + 1]):
            cp.wait_send()
            cp.wait_recv()

    return pl.pallas_call(
        body, name=name,
        out_shape=[pltpu.HBM(a.shape, a.dtype) for a in arrays],
        in_specs=[HBM] * n + [SEMS, SEMS, ANY],
        out_specs=[HBM] * n,
        input_output_aliases={i: i for i in range(n)},
        compiler_params=pltpu.CompilerParams(has_side_effects=EFFECT),
    )(*arrays, sends, recvs, after)


def _wait_exchanges(name, groups, after):
    flat = [a for (arrays, _, _), _ in groups for a in arrays]
    n_arr, n_g = len(flat), len(groups)

    def body(*refs):
        off = 0
        for gi, ((arrays, _, _), plan) in enumerate(groups):
            for cp in _copies(plan, refs[off:off + len(arrays)], refs[n_arr + 2 * gi], refs[n_arr + 2 * gi + 1]):
                cp.wait_send()
                cp.wait_recv()
            off += len(arrays)

    res = pl.pallas_call(
        body, name=name,
        out_shape=[pltpu.HBM(a.shape, a.dtype) for a in flat],
        in_specs=[HBM] * n_arr + [SEMS] * (2 * n_g) + [ANY],
        out_specs=[HBM] * n_arr,
        input_output_aliases={i: i for i in range(n_arr)},
        compiler_params=pltpu.CompilerParams(has_side_effects=EFFECT),
    )(*flat, *[s for (_, sends, recvs), _ in groups for s in (sends, recvs)], after)
    out, off = [], 0
    for (arrays, _, _), _ in groups:
        out.append(list(res[off:off + len(arrays)]))
        off += len(arrays)
    return out


def _gather_finish(land, name):
    def body(land_ref, out_ref, send_sems, recv_sems):
        del land_ref
        x, y, c = _mesh_pos()
        cps = []
        for k, chip in enumerate(_other_chips(x, y)):
            block = out_ref.at[4 * chip[0] + 2 * chip[1] + c]
            cps.append(pltpu.make_async_remote_copy(
                src_ref=block, dst_ref=block, send_sem=send_sems.at[k], recv_sem=recv_sems.at[k],
                device_id=(x, y, 1 - c), device_id_type=MESH))
        for cp in cps:
            cp.start()
        for cp in cps:
            cp.wait()

    return pl.pallas_call(
        body, name=name,
        in_specs=[ANY], out_specs=ANY,
        out_shape=SDS(land.shape, land.dtype),
        input_output_aliases={0: 0},
        scratch_shapes=[pltpu.SemaphoreType.DMA((3,)), pltpu.SemaphoreType.DMA((3,))],
    )(land)


def _pair_add(own, got, c, name):
    _, r, cols = own.shape
    tr = _row_tile(r, ROWS_F32)

    def body(c_ref, own_ref, got_ref, o_ref):
        o_ref[...] = (own_ref[...].astype(F32) + got_ref[...].astype(F32)).astype(BF16)

    return pl.pallas_call(
        body, name=name,
        grid_spec=pltpu.PrefetchScalarGridSpec(
            num_scalar_prefetch=1, grid=(4, r // tr),
            in_specs=[pl.BlockSpec((None, tr, cols), lambda q, i, c_ref: (2 * q + c_ref[0], i, 0)),
                      pl.BlockSpec((None, tr, cols), lambda q, i, c_ref: (q, i, 0))],
            out_specs=pl.BlockSpec((None, tr, cols), lambda q, i, c_ref: (q, i, 0))),
        out_shape=SDS((4, r, cols), BF16),
        compiler_params=_params("arbitrary", "arbitrary"),
    )(c, own, got)


def _adamw(w, g, m, v):
    m = ADAM_B1 * m + (1.0 - ADAM_B1) * g
    v = ADAM_B2 * v + (1.0 - ADAM_B2) * (g * g)
    m_hat = m / (1.0 - ADAM_B1 ** ADAM_STEP)
    v_hat = v / (1.0 - ADAM_B2 ** ADAM_STEP)
    delta = -ADAM_LR * (m_hat / (jnp.sqrt(v_hat) + ADAM_EPS) + ADAM_WD * w)
    return delta, m, v


def _adamw_layer(chip_sum, got, chip, w, m, v, layer, prev, name, after=None):
    depth, r, cols = w.shape
    tr = _row_tile(r, ROWS_NORM)

    def body(chip_ref, sum_ref, got_ref, w_ref, m_ref, v_ref, *rest):
        g_out, d_out, m_out, v_out, token = rest[-5:]
        g = sum_ref[...].astype(F32) + got_ref[0].astype(F32) + got_ref[1].astype(F32) + got_ref[2].astype(F32)
        delta, mm, vv = _adamw(w_ref[...], g, m_ref[...], v_ref[...])
        g_out[...] = g
        d_out[...] = delta
        m_out[...] = mm
        v_out[...] = vv
        token[...] = jnp.zeros_like(token)

    shard = pl.BlockSpec((None, tr, cols), lambda i, chip_ref: (layer, i, 0))
    in_specs = [pl.BlockSpec((None, tr, cols), lambda i, chip_ref: (chip_ref[0], i, 0)),
                pl.BlockSpec((3, tr, cols), lambda i, chip_ref: (0, i, 0)), shard, shard, shard]
    args = [chip, chip_sum, got, w, m, v]
    aliases = {}
    if prev is not None:
        in_specs += [ANY] * 4
        aliases = {len(args) + k: k for k in range(4)}
        args += list(prev)
    if after is not None:
        in_specs, args = in_specs + [ANY], args + [after]
    res = pl.pallas_call(
        body, name=name,
        grid_spec=pltpu.PrefetchScalarGridSpec(
            num_scalar_prefetch=1, grid=(r // tr,), in_specs=in_specs,
            out_specs=[shard] * 4 + [pl.BlockSpec((8, BLK), lambda i, chip_ref: (0, 0))]),
        out_shape=[SDS(w.shape, F32)] * 4 + [SDS((8, BLK), F32)],
        input_output_aliases=aliases,
        compiler_params=_params("arbitrary"),
    )(*args)
    return list(res[:4]), res[4]


WIDE = ("ln1_g", "out_norm_g", "ln2_g", "conv_b", "conv_ln_g", "conv_ln_b", "sgu_ln_g", "sgu_ln_b")
NARROW = ("q_norm_g", "k_norm_g", "sinks")


def _small_rows(w):
    rows, r = {}, 0
    for n in WIDE:
        rows[n] = (r, w[n].shape[1] // BLK)
        r += rows[n][1]
    for n in NARROW + ("loss",):
        rows[n] = (r, 1)
        r += 1
    r = -(-r // 8) * 8
    rows["sgu_b"] = (r, w["sgu_b"].shape[1])
    r += -(-rows["sgu_b"][1] // 8) * 8
    rows["conv_w"] = (r, N_DEV * HALO)
    return rows, r + N_DEV * HALO


def _pack_small(small, rows, total):
    parts, r = [], 0

    def put(name, block):
        nonlocal r
        first = rows[name][0]
        if first > r:
            parts.append(jnp.zeros((first - r, BLK), F32))
        parts.append(block)
        r = first + block.shape[0]

    for n in WIDE:
        put(n, small[n].reshape(-1, BLK))
    for n in NARROW:
        put(n, jnp.pad(small[n], (0, BLK - small[n].shape[0]))[None])
    if "loss" in small:
        put("loss", jnp.pad(small["loss"], (0, BLK - 1))[None])
    put("sgu_b", small["sgu_b"])
    cw = small["conv_w"]
    per_dev = cw.shape[1] // N_DEV
    blocks = jnp.transpose(cw.reshape(CONV_K, N_DEV, per_dev), (1, 0, 2))
    put("conv_w", jnp.pad(blocks, ((0, 0), (0, HALO - CONV_K), (0, BLK - per_dev))).reshape(N_DEV * HALO, BLK))
    if total > r:
        parts.append(jnp.zeros((total - r, BLK), F32))
    return jnp.concatenate(parts)


def _small_update(me, packed, sgu_w_all, w, m, v, rows):
    depth = len(packed)
    total = packed[0].shape[1]
    names = SMALL + ("conv_w",)
    heads = w["sgu_w"].shape[1]
    per_dev = w["conv_w"].shape[2]

    def body(me_ref, *refs):
        packed_refs, sgu_refs = refs[:depth], refs[depth:2 * depth]
        ins = refs[2 * depth:2 * depth + 3 * len(names)]
        outs = refs[2 * depth + 3 * len(names):2 * depth + 7 * len(names)]
        loss_out, acc, conv = refs[-3:]
        io = {n: (ins[3 * i:3 * i + 3], outs[4 * i:4 * i + 4]) for i, n in enumerate(names)}

        def update(n):
            (w_ref, m_ref, v_ref), (g_out, d_out, m_out, v_out) = io[n]
            delta, mm, vv = _adamw(w_ref[...], g_out[...], m_ref[...], v_ref[...])
            d_out[...] = delta
            m_out[...] = mm
            v_out[...] = vv

        mine = pl.ds(pl.multiple_of(rows["conv_w"][0] + HALO * me_ref[0], 8), HALO)
        for l in range(depth):
            s = packed_refs[l][0]
            c = packed_refs[l][0, mine, :]
            for k in range(1, N_DEV):
                s = s + packed_refs[l][k]
                c = c + packed_refs[l][k, mine, :]
            acc[l] = s
            conv[l] = c
        loss_out[...] = acc[depth - 1, rows["loss"][0]:rows["loss"][0] + 1, :]
        for n in WIDE:
            first, nr = rows[n]
            for l in range(depth):
                for j in range(nr):
                    io[n][1][0][l:l + 1, BLK * j:BLK * (j + 1)] = acc[l, first + j:first + j + 1, :]
            update(n)
        for n in NARROW:
            first, lanes = rows[n][0], w[n].shape[1]
            for l in range(depth):
                io[n][1][0][l:l + 1, :] = acc[l, first:first + 1, 0:lanes]
            update(n)
        first, nr = rows["sgu_b"]
        for l in range(depth):
            io["sgu_b"][1][0][l] = acc[l, first:first + nr, :]
            io["conv_w"][1][0][l] = conv[l, 0:CONV_K, 0:per_dev]
        update("sgu_b")
        update("conv_w")
        (w_ref, m_ref, v_ref), (g_out, d_out, m_out, v_out) = io["sgu_w"]
        for l in range(depth):
            for h in range(heads):
                g = sgu_refs[l][0, h]
                for k in range(1, N_DEV):
                    g = g + sgu_refs[l][k, h]
                delta, mm, vv = _adamw(w_ref[l, h], g, m_ref[l, h], v_ref[l, h])
                g_out[l, h] = g
                d_out[l, h] = delta
                m_out[l, h] = mm
                v_out[l, h] = vv

    def whole(a):
        nd = len(a.shape)
        return pl.BlockSpec(a.shape, lambda i, me_ref: (0,) * nd)

    small_in = [t[n] for n in names for t in (w, m, v)]
    res = pl.pallas_call(
        body, name="small_update",
        grid_spec=pltpu.PrefetchScalarGridSpec(
            num_scalar_prefetch=1, grid=(1,),
            in_specs=[whole(a) for a in list(packed) + list(sgu_w_all) + small_in],
            out_specs=[whole(w[n]) for n in names for _ in range(4)] + [pl.BlockSpec((1, BLK), lambda i, me_ref: (0, 0))],
            scratch_shapes=[pltpu.VMEM((depth, total, BLK), F32), pltpu.VMEM((depth, HALO, BLK), F32)]),
        out_shape=[SDS(w[n].shape, F32) for n in names for _ in range(4)] + [SDS((1, BLK), F32)],
        compiler_params=_params("arbitrary"),
    )(me, *packed, *sgu_w_all, *small_in)
    return {n: list(res[4 * i:4 * i + 4]) for i, n in enumerate(names)}, res[-1][0, 0]


def _pack(arrays):
    flat = jnp.concatenate([a.reshape(-1) for a in arrays])
    rows = -(-flat.shape[0] // (8 * BLK)) * 8
    return jnp.pad(flat, (0, rows * BLK - flat.shape[0])).reshape(rows, BLK)


def _unpack(packed, like):
    flat = packed.reshape(-1)
    out, off = [], 0
    for a in like:
        out.append(flat[off:off + a.size].reshape(a.shape))
        off += a.size
    return out


SMALL = ("ln1_g", "q_norm_g", "k_norm_g", "sinks", "conv_b", "conv_ln_g", "conv_ln_b", "sgu_ln_g", "sgu_ln_b",
         "sgu_w", "sgu_b", "out_norm_g", "ln2_g")
ORDER = ("ln1_g", "w_in", "q_norm_g", "k_norm_g", "sinks", "conv_w", "conv_b", "conv_ln_g", "conv_ln_b", "sgu_ln_g",
         "sgu_ln_b", "sgu_w", "sgu_b", "out_norm_g", "w_out", "ln2_g", "w_up", "w_down")


def _step(x, target, w, m, v):
    depth = w["ln1_g"].shape[0]
    xpos, ypos, cpos = _mesh_pos()
    me = 4 * xpos + 2 * ypos + cpos
    c_arr = jnp.reshape(cpos, (1,)).astype(jnp.int32)
    chip_arr = jnp.reshape(2 * xpos + ypos, (1,)).astype(jnp.int32)

    d = x.shape[1]
    def own_block(shard):
        return lax.dynamic_update_slice_in_dim(lax.empty((N_DEV,) + shard.shape, shard.dtype), shard[None], me, axis=0)

    cw = w["conv_w"]
    order = [(0, "conv_w")] + [(l, n) for l in range(depth) for n in BIG]
    plans = {k: (_near_plan, 3) if k in order[-2:] else (_gather_plan, 4) for k in order}
    started, gather_token = _start_exchanges("gather_start", [
        ([own_block(_pack([cw]) if n == "conv_w" else w[n][l].astype(BF16))], *plans[l, n]) for l, n in order])
    pending, relaying = dict(zip(order, started)), {}

    def prepare(l, n, after):
        if plans[l, n][0] is not _near_plan:
            return after
        arrays, sends, recvs = pending.pop((l, n))
        land, = _wait_exchange(f"gather_wait_{l}_{n}", arrays, sends, recvs, _near_plan, after)
        (relaying[l, n],), token = _start_exchanges(f"gather_relay_{n}", [([land], _relay_plan, 2)])
        return token

    def fetch(l, n, after):
        if (l, n) in relaying:
            arrays, sends, recvs = relaying.pop((l, n))
            land, = _wait_exchange(f"gather_relay_wait_{n}", arrays, sends, recvs, _relay_plan, after)
        else:
            arrays, sends, recvs = pending.pop((l, n))
            land, = _wait_exchange(f"gather_wait_{l}_{n}", arrays, sends, recvs, plans[l, n][0],
                                   gather_token if after is None else after)
        full = _gather_finish(land, "gather_finish_" + n)
        return full if n in ("w_up", "conv_w") else full.reshape(-1, d)

    cw_all = fetch(0, "conv_w", None)
    cw_full = jnp.concatenate([_unpack(cw_all[k], [cw])[0] for k in range(N_DEV)], axis=-1)
    small, _ = lax.optimization_barrier(({n: w[n] for n in SMALL}, gather_token))
    small["conv_w"] = cw_full

    to_sibling, inflight, shared, leaving = {}, [], {}, []
    rows, total = _small_rows(w)

    def flush(name, first=()):
        groups = list(first) + [g for _, g in leaving]
        started, token = _start_exchanges(name, groups)
        for (record, _), going in zip(leaving, started[len(first):]):
            record(going)
        leaving.clear()
        return started[:len(first)], token

    def start(l, n, grad, after):
        if callable(grad):
            blocks, plan = grad(sel=1 - c_arr, after=after), _half_plan
        else:
            blocks, plan = grad.reshape(N_DEV, -1, d), _pair_plan
        (going,), token = flush(f"grads_start_{l}_{n}", [([blocks, lax.empty((4,) + blocks.shape[1:], BF16)], plan, 4)])
        to_sibling[l, n] = (going, plan, grad)
        return token

    def finish(l, n, after):
        (arrays, sends, recvs), plan, grad = to_sibling.pop((l, n))
        blocks, from_sibling = _wait_exchange(f"pair_wait_{l}_{n}", arrays, sends, recvs, plan, after)
        if callable(grad):
            chip_sums = grad(sel=c_arr, add=from_sibling)
        else:
            chip_sums = _pair_add(blocks, from_sibling, c_arr, "pair_add_" + n)
        leaving.append((lambda going: inflight.append((l, n, going)),
                        ([chip_sums, lax.empty((3,) + chip_sums.shape[1:], BF16)], _chips_plan, 3)))
        return chip_sums

    def share(l, kind, grads):
        block = grads if kind == "sgu_w" else _pack_small(grads, rows, total)
        leaving.append((lambda going: shared.__setitem__((l, kind), going), ([own_block(block)], _all_plan, N_DEV - 1)))

    grad_x, (after, finish_last) = _local_step(x, target, small, depth, fetch, start, finish, share, None, prepare)

    as3d = lambda a: a.reshape(depth, -1, a.shape[-1])
    results = {n: None for n in BIG}

    landed = {}

    def await_layer(name, reduces, after):
        smalls = sorted(k for k in shared if k not in landed and all(k[0] >= l for l, _, _ in reduces))
        got = _wait_exchanges(name, [(going, _chips_plan) for _, _, going in reduces]
                              + [(shared[k], _all_plan) for k in smalls], after)
        for k, (arr,) in zip(smalls, got[len(reduces):]):
            landed[k] = arr
        return got[:len(reduces)]

    def update(l, n, chip_sums, got, after):
        results[n], token = _adamw_layer(chip_sums, got, chip_arr, as3d(w[n]), as3d(m[n]), as3d(v[n]), l, results[n],
                                         f"adamw_{l}_{n}", after)
        return token

    first, rest = inflight[:4], inflight[4:]
    arrived = await_layer("grads_wait_first", first, after)
    after = None
    for (l, n, _), (chip_sums, got) in list(zip(first, arrived))[:2]:
        after = update(l, n, chip_sums, got, after)
    finish_last(after)
    _, after = flush("grads_start_last")
    for (l, n, _), (chip_sums, got) in list(zip(first, arrived))[2:]:
        after = update(l, n, chip_sums, got, after)
    arrived = await_layer("grads_wait_rest", rest, after)
    out, loss = _small_update(jnp.reshape(me, (1,)).astype(jnp.int32), [landed[l, "rest"] for l in range(depth)],
                              [landed[l, "sgu_w"] for l in range(depth)], w, m, v, rows)
    after = out["sinks"][1]
    for (l, n, _), (chip_sums, got) in zip(rest, arrived):
        after = update(l, n, chip_sums, got, after)
    (chip_sums, got), = await_layer("grads_wait_last", inflight[-1:], after)
    l, n, _ = inflight[-1]
    update(l, n, chip_sums, got, None)
    out.update({n: [r.reshape(w[n].shape) for r in results[n]] for n in BIG})
    return (loss, grad_x[None]) + tuple(out[n][k] for k in range(4) for n in ORDER)


def kernel(x, ln1_g, w_in, q_norm_g, k_norm_g, sinks, conv_w, conv_b, conv_ln_g, conv_ln_b, sgu_ln_g, sgu_ln_b, sgu_w, sgu_b, out_norm_g, w_out, ln2_g, w_up, w_down, loss_target, m_ln1_g, m_w_in, m_q_norm_g, m_k_norm_g, m_sinks, m_conv_w, m_conv_b, m_conv_ln_g, m_conv_ln_b, m_sgu_ln_g, m_sgu_ln_b, m_sgu_w, m_sgu_b, m_out_norm_g, m_w_out, m_ln2_g, m_w_up, m_w_down, v_ln1_g, v_w_in, v_q_norm_g, v_k_norm_g, v_sinks, v_conv_w, v_conv_b, v_conv_ln_g, v_conv_ln_b, v_sgu_ln_g, v_sgu_ln_b, v_sgu_w, v_sgu_b, v_out_norm_g, v_w_out, v_ln2_g, v_w_up, v_w_down):
    w = dict(zip(ORDER, (ln1_g, w_in, q_norm_g, k_norm_g, sinks, conv_w, conv_b, conv_ln_g, conv_ln_b, sgu_ln_g, sgu_ln_b,
                         sgu_w, sgu_b, out_norm_g, w_out, ln2_g, w_up, w_down)))
    m = dict(zip(ORDER, (m_ln1_g, m_w_in, m_q_norm_g, m_k_norm_g, m_sinks, m_conv_w, m_conv_b, m_conv_ln_g, m_conv_ln_b,
                         m_sgu_ln_g, m_sgu_ln_b, m_sgu_w, m_sgu_b, m_out_norm_g, m_w_out, m_ln2_g, m_w_up, m_w_down)))
    v = dict(zip(ORDER, (v_ln1_g, v_w_in, v_q_norm_g, v_k_norm_g, v_sinks, v_conv_w, v_conv_b, v_conv_ln_g, v_conv_ln_b,
                         v_sgu_ln_g, v_sgu_ln_b, v_sgu_w, v_sgu_b, v_out_norm_g, v_w_out, v_ln2_g, v_w_up, v_w_down)))
    for group in (w, m, v):
        group["w_in"] = jnp.swapaxes(group["w_in"], 1, 2)
    out = list(_step(x[0], loss_target[0], w, m, v))
    for k in range(4):
        i = 2 + k * len(ORDER) + ORDER.index("w_in")
        out[i] = jnp.swapaxes(out[i], 1, 2)
    return tuple(out)
```
